```python
import jax, jax.numpy as jnp
from jax import lax
import numpy as np

D_MODEL = 2048
BATCH = 8
SEQ = 4096
DEPTH = 1

N_META = 16
BLK = 128
PAD_FRONT = BLK - N_META
MLA_HEADS = 8
MLA_NOPE = 128
MLA_ROPE = 64
MLA_V = 128
MLA_Q_RANK = 512
MLA_KV_RANK = 256
MLA_WIDTH = MLA_HEADS * MLA_V
RET_HEADS = 8
RET_DK = 128
RET_DV = 128
RET_WIDTH = RET_HEADS * RET_DV
ROPE_BASE = 10000.0
NORM_EPS = 1e-6
GN_EPS = 1e-5
N_BRANCH = 2
NEG_INF = -1e30
IN_SPLITS = (MLA_Q_RANK, MLA_KV_RANK, MLA_ROPE, MLA_WIDTH,
             RET_HEADS * RET_DK, RET_HEADS * RET_DK, RET_WIDTH, RET_WIDTH,
             N_BRANCH * D_MODEL)
IN_WIDTH = sum(IN_SPLITS)

kernel_name = "mla_retention_gated_hybrid"


def rmsnorm(x, w):
    xf = x.astype(jnp.float32)
    y = xf * lax.rsqrt(jnp.mean(xf * xf, axis=-1, keepdims=True) + NORM_EPS)
    return (y * w.astype(jnp.float32)).astype(x.dtype)


def rope(t, pos):
    d = t.shape[-1]
    inv = ROPE_BASE ** (-jnp.arange(0, d, 2, dtype=jnp.float32) / d)
    ang = pos.astype(jnp.float32)[:, None] * inv[None, :]
    ang = ang.reshape(ang.shape[:1] + (1,) * (t.ndim - 3) + ang.shape[1:])
    cos = jnp.cos(ang).astype(t.dtype)
    sin = jnp.sin(ang).astype(t.dtype)
    t1, t2 = t[..., : d // 2], t[..., d // 2:]
    return jnp.concatenate([t1 * cos - t2 * sin, t1 * sin + t2 * cos], axis=-1)


def pad_front(t):
    return jnp.pad(t, ((0, 0), (PAD_FRONT, 0)) + ((0, 0),) * (t.ndim - 2))


def mla_branch(c_q, c_kv, k_pe, pos, q_norm_w, w_uq, kv_norm_w, w_ukv):
    B, L, _ = c_q.shape
    q = (rmsnorm(c_q, q_norm_w) @ w_uq).reshape(B, L, MLA_HEADS, MLA_NOPE + MLA_ROPE)
    q_nope, q_pe = q[..., :MLA_NOPE], rope(q[..., MLA_NOPE:], pos)
    kv = (rmsnorm(c_kv, kv_norm_w) @ w_ukv).reshape(B, L, MLA_HEADS, MLA_NOPE + MLA_V)
    k_nope, v = kv[..., :MLA_NOPE], kv[..., MLA_NOPE:]
    k_pe = rope(k_pe, pos)
    q_nope, q_pe, k_nope, v, k_pe = (pad_front(t) for t in (q_nope, q_pe, k_nope, v, k_pe))
    Lp = L + PAD_FRONT
    scale = (MLA_NOPE + MLA_ROPE) ** -0.5
    kidx = jnp.arange(Lp)

    def block(i):
        start = i * BLK
        qn = lax.dynamic_slice_in_dim(q_nope, start, BLK, axis=1)
        qp = lax.dynamic_slice_in_dim(q_pe, start, BLK, axis=1)
        s = (jnp.einsum('bqhd,bkhd->bhqk', qn, k_nope)
             + jnp.einsum('bqhd,bkd->bhqk', qp, k_pe)).astype(jnp.float32) * scale
        qidx = start + jnp.arange(BLK)
        mask = (kidx[None, :] <= qidx[:, None]) & (kidx[None, :] >= PAD_FRONT)
        s = jnp.where(mask[None, None], s, NEG_INF)
        p = jax.nn.softmax(s, axis=-1).astype(v.dtype)
        return jnp.einsum('bhqk,bkhd->bqhd', p, v)

    o = lax.map(block, jnp.arange(Lp // BLK))
    o = o.transpose(1, 0, 2, 3, 4).reshape(B, Lp, MLA_WIDTH)
    return o[:, PAD_FRONT:]


def retention_branch(r_q, r_k, r_v, pos, gn_w, gn_b):
    B, L, _ = r_q.shape
    dt = r_q.dtype
    q = rope(r_q.reshape(B, L, RET_HEADS, RET_DK), pos)
    k = rope(r_k.reshape(B, L, RET_HEADS, RET_DK), pos) * (RET_DK ** -0.5)
    v = r_v.reshape(B, L, RET_HEADS, RET_DV)
    q, k, v = (pad_front(t).astype(jnp.float32) for t in (q, k, v))
    Lp = L + PAD_FRONT
    nc = Lp // BLK

    def chunk(t):
        return t.reshape(B, nc, BLK, RET_HEADS, -1).transpose(0, 3, 1, 2, 4)

    qc, kc, vc = chunk(q), chunk(k), chunk(v)
    log_g = jnp.log1p(-(2.0 ** (-5.0 - jnp.arange(RET_HEADS, dtype=jnp.float32))))
    n = jnp.arange(BLK, dtype=jnp.float32)
    diff = n[:, None] - n[None, :]
    decay_in = jnp.where(diff >= 0, jnp.exp(log_g[:, None, None] * jnp.maximum(diff, 0.0)), 0.0)
    zeta = jnp.exp(log_g[:, None] * (BLK - 1.0 - n))
    xi = jnp.exp(log_g[:, None] * (n + 1.0))
    g_chunk = jnp.exp(log_g * BLK)
    s = jnp.einsum('bhcnd,bhcmd->bhcnm', qc, kc) * decay_in[None, :, None]
    inner = jnp.einsum('bhcnm,bhcme->bhcne', s, vc)
    kv_chunk = jnp.einsum('bhcmd,bhcme->cbhde', kc * zeta[None, :, None, :, None], vc)

    def step(R, kv):
        return R * g_chunk[None, :, None, None] + kv, R

    _, R_prev = lax.scan(step, jnp.zeros((B, RET_HEADS, RET_DK, RET_DV), jnp.float32), kv_chunk)
    cross = jnp.einsum('bhcnd,cbhde->bhcne', qc, R_prev) * xi[None, :, None, :, None]
    o = (inner + cross).transpose(0, 2, 3, 1, 4).reshape(B, Lp, RET_HEADS, RET_DV)[:, PAD_FRONT:]
    mu = jnp.mean(o, axis=-1, keepdims=True)
    var = jnp.mean(jnp.square(o - mu), axis=-1, keepdims=True)
    o = ((o - mu) * lax.rsqrt(var + GN_EPS)).reshape(B, L, RET_WIDTH)
    o = o * gn_w.astype(jnp.float32) + gn_b.astype(jnp.float32)
    return o.astype(dt)


def hybrid_layer(h, pos, norm_w, w_in, mla_q_norm_w, mla_w_uq, mla_kv_norm_w, mla_w_ukv,
                 ret_gn_w, ret_gn_b, w_branch_mla, w_branch_ret, w_out):
    B, L, D = h.shape
    xn = rmsnorm(h, norm_w)
    proj = xn @ w_in
    offs, acc = [], 0
    for w in IN_SPLITS[:-1]:
        acc += w
        offs.append(acc)
    c_q, c_kv, k_pe, z_mla, r_q, r_k, r_v, z_ret, gate_logits = jnp.split(proj, offs, axis=-1)
    y_mla = mla_branch(c_q, c_kv, k_pe, pos, mla_q_norm_w, mla_w_uq, mla_kv_norm_w, mla_w_ukv) * jax.nn.silu(z_mla)
    y_ret = retention_branch(r_q, r_k, r_v, pos, ret_gn_w, ret_gn_b) * jax.nn.silu(z_ret)
    gates = jax.nn.sigmoid(gate_logits.astype(jnp.float32)).astype(h.dtype).reshape(B, L, N_BRANCH, D)
    merged = gates[:, :, 0] * (y_mla @ w_branch_mla) + gates[:, :, 1] * (y_ret @ w_branch_ret)
    return h + merged @ w_out


def _fwd_setup_inputs(seed: int = 0) -> dict:
    key = jax.random.key(seed)
    ks = jax.random.split(key, 16)
    f32 = jnp.float32

    def w(k, shape, fan_in):
        return jax.random.normal(k, shape, f32) * (fan_in ** -0.5)

    def gain(k, shape):
        return 1.0 + 0.02 * jax.random.normal(k, shape, f32)

    return {
        "x": jax.random.normal(ks[0], (BATCH, SEQ, D_MODEL), f32),
        "meta": jax.random.normal(ks[1], (N_META, D_MODEL), f32),
        "norm_w": gain(ks[2], (DEPTH, D_MODEL)),
        "w_in": w(ks[3], (DEPTH, D_MODEL, IN_WIDTH), D_MODEL),
        "mla_q_norm_w": gain(ks[4], (DEPTH, MLA_Q_RANK)),
        "mla_w_uq": w(ks[5], (DEPTH, MLA_Q_RANK, MLA_HEADS * (MLA_NOPE + MLA_ROPE)), MLA_Q_RANK),
        "mla_kv_norm_w": gain(ks[6], (DEPTH, MLA_KV_RANK)),
        "mla_w_ukv": w(ks[7], (DEPTH, MLA_KV_RANK, MLA_HEADS * (MLA_NOPE + MLA_V)), MLA_KV_RANK),
        "ret_gn_w": gain(ks[8], (DEPTH, RET_WIDTH)),
        "ret_gn_b": 0.02 * jax.random.normal(ks[9], (DEPTH, RET_WIDTH), f32),
        "w_branch_mla": w(ks[10], (DEPTH, MLA_WIDTH, D_MODEL), MLA_WIDTH),
        "w_branch_ret": w(ks[11], (DEPTH, RET_WIDTH, D_MODEL), RET_WIDTH),
        "w_out": w(ks[12], (DEPTH, D_MODEL, D_MODEL), D_MODEL),
        "final_norm_w": gain(ks[13], (D_MODEL,)),
    }


def _fwd_reference(x, meta, norm_w, w_in, mla_q_norm_w, mla_w_uq, mla_kv_norm_w, mla_w_ukv,
              ret_gn_w, ret_gn_b, w_branch_mla, w_branch_ret, w_out, final_norm_w):
    B = x.shape[0]
    h = jnp.concatenate([jnp.broadcast_to(meta.astype(x.dtype)[None], (B, N_META, D_MODEL)), x], axis=1)
    pos = jnp.arange(h.shape[1])
    for l in range(DEPTH):
        h = hybrid_layer(h, pos, norm_w[l], w_in[l], mla_q_norm_w[l], mla_w_uq[l],
                         mla_kv_norm_w[l], mla_w_ukv[l], ret_gn_w[l], ret_gn_b[l],
                         w_branch_mla[l], w_branch_ret[l], w_out[l])
    h = rmsnorm(h, final_norm_w)
    return h[:, N_META:]


import jax as _jax
import jax.numpy as _jnp

TWIN_FORMAT = 'train_step'
FWD_PARAMS = ['x', 'meta', 'norm_w', 'w_in', 'mla_q_norm_w', 'mla_w_uq', 'mla_kv_norm_w', 'mla_w_ukv', 'ret_gn_w', 'ret_gn_b', 'w_branch_mla', 'w_branch_ret', 'w_out', 'final_norm_w']
TWIN_WEIGHTS = ['meta', 'norm_w', 'w_in', 'mla_q_norm_w', 'mla_w_uq', 'mla_kv_norm_w', 'mla_w_ukv', 'ret_gn_w', 'ret_gn_b', 'w_branch_mla', 'w_branch_ret', 'w_out', 'final_norm_w']
TWIN_DIFF_INPUT = 'x'
TWIN_INPUTS = ['x', 'meta', 'norm_w', 'w_in', 'mla_q_norm_w', 'mla_w_uq', 'mla_kv_norm_w', 'mla_w_ukv', 'ret_gn_w', 'ret_gn_b', 'w_branch_mla', 'w_branch_ret', 'w_out', 'final_norm_w', 'loss_target', 'm_meta', 'm_norm_w', 'm_w_in', 'm_mla_q_norm_w', 'm_mla_w_uq', 'm_mla_kv_norm_w', 'm_mla_w_ukv', 'm_ret_gn_w', 'm_ret_gn_b', 'm_w_branch_mla', 'm_w_branch_ret', 'm_w_out', 'm_final_norm_w', 'v_meta', 'v_norm_w', 'v_w_in', 'v_mla_q_norm_w', 'v_mla_w_uq', 'v_mla_kv_norm_w', 'v_mla_w_ukv', 'v_ret_gn_w', 'v_ret_gn_b', 'v_w_branch_mla', 'v_w_branch_ret', 'v_w_out', 'v_final_norm_w']
TWIN_OUTPUTS = ['loss', 'grad_x', 'grad_meta', 'grad_norm_w', 'grad_w_in', 'grad_mla_q_norm_w', 'grad_mla_w_uq', 'grad_mla_kv_norm_w', 'grad_mla_w_ukv', 'grad_ret_gn_w', 'grad_ret_gn_b', 'grad_w_branch_mla', 'grad_w_branch_ret', 'grad_w_out', 'grad_final_norm_w', 'delta_meta', 'delta_norm_w', 'delta_w_in', 'delta_mla_q_norm_w', 'delta_mla_w_uq', 'delta_mla_kv_norm_w', 'delta_mla_w_ukv', 'delta_ret_gn_w', 'delta_ret_gn_b', 'delta_w_branch_mla', 'delta_w_branch_ret', 'delta_w_out', 'delta_final_norm_w', 'new_m_meta', 'new_m_norm_w', 'new_m_w_in', 'new_m_mla_q_norm_w', 'new_m_mla_w_uq', 'new_m_mla_kv_norm_w', 'new_m_mla_w_ukv', 'new_m_ret_gn_w', 'new_m_ret_gn_b', 'new_m_w_branch_mla', 'new_m_w_branch_ret', 'new_m_w_out', 'new_m_final_norm_w', 'new_v_meta', 'new_v_norm_w', 'new_v_w_in', 'new_v_mla_q_norm_w', 'new_v_mla_w_uq', 'new_v_mla_kv_norm_w', 'new_v_mla_w_ukv', 'new_v_ret_gn_w', 'new_v_ret_gn_b', 'new_v_w_branch_mla', 'new_v_w_branch_ret', 'new_v_w_out', 'new_v_final_norm_w']
TWIN_LEAF_KINDS = {'loss': 'loss', 'grad_x': 'grad_x', 'grad_meta': 'grad_w', 'grad_norm_w': 'grad_w', 'grad_w_in': 'grad_w', 'grad_mla_q_norm_w': 'grad_w', 'grad_mla_w_uq': 'grad_w', 'grad_mla_kv_norm_w': 'grad_w', 'grad_mla_w_ukv': 'grad_w', 'grad_ret_gn_w': 'grad_w', 'grad_ret_gn_b': 'grad_w', 'grad_w_branch_mla': 'grad_w', 'grad_w_branch_ret': 'grad_w', 'grad_w_out': 'grad_w', 'grad_final_norm_w': 'grad_w', 'delta_meta': 'delta_w', 'delta_norm_w': 'delta_w', 'delta_w_in': 'delta_w', 'delta_mla_q_norm_w': 'delta_w', 'delta_mla_w_uq': 'delta_w', 'delta_mla_kv_norm_w': 'delta_w', 'delta_mla_w_ukv': 'delta_w', 'delta_ret_gn_w': 'delta_w', 'delta_ret_gn_b': 'delta_w', 'delta_w_branch_mla': 'delta_w', 'delta_w_branch_ret': 'delta_w', 'delta_w_out': 'delta_w', 'delta_final_norm_w': 'delta_w', 'new_m_meta': 'new_m', 'new_m_norm_w': 'new_m', 'new_m_w_in': 'new_m', 'new_m_mla_q_norm_w': 'new_m', 'new_m_mla_w_uq': 'new_m', 'new_m_mla_kv_norm_w': 'new_m', 'new_m_mla_w_ukv': 'new_m', 'new_m_ret_gn_w': 'new_m', 'new_m_ret_gn_b': 'new_m', 'new_m_w_branch_mla': 'new_m', 'new_m_w_branch_ret': 'new_m', 'new_m_w_out': 'new_m', 'new_m_final_norm_w': 'new_m', 'new_v_meta': 'new_v', 'new_v_norm_w': 'new_v', 'new_v_w_in': 'new_v', 'new_v_mla_q_norm_w': 'new_v', 'new_v_mla_w_uq': 'new_v', 'new_v_mla_kv_norm_w': 'new_v', 'new_v_mla_w_ukv': 'new_v', 'new_v_ret_gn_w': 'new_v', 'new_v_ret_gn_b': 'new_v', 'new_v_w_branch_mla': 'new_v', 'new_v_w_branch_ret': 'new_v', 'new_v_w_out': 'new_v', 'new_v_final_norm_w': 'new_v'}


def _forward(args):
    return _fwd_reference(*[args[k] for k in FWD_PARAMS])


def _output_shape():
    out = _jax.eval_shape(lambda: _forward(_fwd_setup_inputs(0)))
    return out.shape, out.dtype

N_MICROBATCH = 1
ADAM_LR = 0.001
ADAM_B1 = 0.9
ADAM_B2 = 0.999
ADAM_EPS = 1e-08
ADAM_WD = 0.01
ADAM_STEP = 10
PER_EXAMPLE_BATCH_AXIS = {'x': 0, 'loss_target': 0}
SHARED_INPUTS = []
_WEIGHT_DTYPES = {'meta': _jnp.float32, 'norm_w': _jnp.float32, 'w_in': _jnp.float32, 'mla_q_norm_w': _jnp.float32, 'mla_w_uq': _jnp.float32, 'mla_kv_norm_w': _jnp.float32, 'mla_w_ukv': _jnp.float32, 'ret_gn_w': _jnp.float32, 'ret_gn_b': _jnp.float32, 'w_branch_mla': _jnp.float32, 'w_branch_ret': _jnp.float32, 'w_out': _jnp.float32, 'final_norm_w': _jnp.float32}
MOMENT_SCALE = {'meta': 3.492903e-03, 'norm_w': 5.736037e-02, 'w_in': 2.569763e-02, 'mla_q_norm_w': 8.844656e-03, 'mla_w_uq': 5.063795e-03, 'mla_kv_norm_w': 1.845622e-02, 'mla_w_ukv': 5.931225e-03, 'ret_gn_w': 3.906727e-02, 'ret_gn_b': 3.886715e-02, 'w_branch_mla': 4.616062e-03, 'w_branch_ret': 2.733420e-02, 'w_out': 2.759401e-02, 'final_norm_w': 1.598990e+01}


def _to_microbatches(a, axis):
    t = _jnp.moveaxis(a, axis, 0)
    t = t.reshape((N_MICROBATCH, t.shape[0] // N_MICROBATCH) + t.shape[1:])
    return _jnp.moveaxis(t, 1, axis + 1)


def setup_inputs(seed: int = 0) -> dict:
    inp = _fwd_setup_inputs(seed)
    key = _jax.random.fold_in(_jax.random.key(seed), 7919)
    shape, _ = _output_shape()
    out = dict(inp)
    out["loss_target"] = _jax.random.normal(_jax.random.fold_in(key, 0), shape, _jnp.float32)
    for i, name in enumerate(TWIN_WEIGHTS):
        w = inp[name].astype(_jnp.float32)
        if MOMENT_SCALE is None:
            s = _jnp.sqrt(_jnp.mean(_jnp.square(w)) + 1e-30)
        else:
            s = MOMENT_SCALE[name]
        km, kv = _jax.random.split(_jax.random.fold_in(key, i + 1))
        out[name] = w
        out["m_" + name] = s * _jax.random.normal(km, w.shape, _jnp.float32)
        out["v_" + name] = (s * s) * _jax.random.uniform(kv, w.shape, _jnp.float32, 0.5, 1.5)
    if N_MICROBATCH > 1:
        for name, axis in PER_EXAMPLE_BATCH_AXIS.items():
            out[name] = _to_microbatches(out[name], axis)
    return {'x': out['x'], 'meta': out['meta'], 'norm_w': out['norm_w'], 'w_in': out['w_in'], 'mla_q_norm_w': out['mla_q_norm_w'], 'mla_w_uq': out['mla_w_uq'], 'mla_kv_norm_w': out['mla_kv_norm_w'], 'mla_w_ukv': out['mla_w_ukv'], 'ret_gn_w': out['ret_gn_w'], 'ret_gn_b': out['ret_gn_b'], 'w_branch_mla': out['w_branch_mla'], 'w_branch_ret': out['w_branch_ret'], 'w_out': out['w_out'], 'final_norm_w': out['final_norm_w'], 'loss_target': out['loss_target'], 'm_meta': out['m_meta'], 'm_norm_w': out['m_norm_w'], 'm_w_in': out['m_w_in'], 'm_mla_q_norm_w': out['m_mla_q_norm_w'], 'm_mla_w_uq': out['m_mla_w_uq'], 'm_mla_kv_norm_w': out['m_mla_kv_norm_w'], 'm_mla_w_ukv': out['m_mla_w_ukv'], 'm_ret_gn_w': out['m_ret_gn_w'], 'm_ret_gn_b': out['m_ret_gn_b'], 'm_w_branch_mla': out['m_w_branch_mla'], 'm_w_branch_ret': out['m_w_branch_ret'], 'm_w_out': out['m_w_out'], 'm_final_norm_w': out['m_final_norm_w'], 'v_meta': out['v_meta'], 'v_norm_w': out['v_norm_w'], 'v_w_in': out['v_w_in'], 'v_mla_q_norm_w': out['v_mla_q_norm_w'], 'v_mla_w_uq': out['v_mla_w_uq'], 'v_mla_kv_norm_w': out['v_mla_kv_norm_w'], 'v_mla_w_ukv': out['v_mla_w_ukv'], 'v_ret_gn_w': out['v_ret_gn_w'], 'v_ret_gn_b': out['v_ret_gn_b'], 'v_w_branch_mla': out['v_w_branch_mla'], 'v_w_branch_ret': out['v_w_branch_ret'], 'v_w_out': out['v_w_out'], 'v_final_norm_w': out['v_final_norm_w']}


def _loss(weights, diff, rest, loss_target):
    with _jax.named_scope("forward"):
        args = {**rest, TWIN_DIFF_INPUT: diff, **{k: w.astype(_WEIGHT_DTYPES[k]) for k, w in weights.items()}}
        y = _forward(args)
    with _jax.named_scope("loss_head"):
        err = _jnp.square(y.astype(_jnp.float32) - loss_target)
        return 0.5 * _jnp.sum(_jnp.mean(err, axis=-1)) if err.ndim else 0.5 * err


def _adamw(w, g, m, v):
    m = ADAM_B1 * m + (1.0 - ADAM_B1) * g
    v = ADAM_B2 * v + (1.0 - ADAM_B2) * _jnp.square(g)
    m_hat = m / (1.0 - ADAM_B1 ** ADAM_STEP)
    v_hat = v / (1.0 - ADAM_B2 ** ADAM_STEP)
    delta = -ADAM_LR * (m_hat / (_jnp.sqrt(v_hat) + ADAM_EPS) + ADAM_WD * w)
    return delta, m, v


def reference(x, meta, norm_w, w_in, mla_q_norm_w, mla_w_uq, mla_kv_norm_w, mla_w_ukv, ret_gn_w, ret_gn_b, w_branch_mla, w_branch_ret, w_out, final_norm_w, loss_target, m_meta, m_norm_w, m_w_in, m_mla_q_norm_w, m_mla_w_uq, m_mla_kv_norm_w, m_mla_w_ukv, m_ret_gn_w, m_ret_gn_b, m_w_branch_mla, m_w_branch_ret, m_w_out, m_final_norm_w, v_meta, v_norm_w, v_w_in, v_mla_q_norm_w, v_mla_w_uq, v_mla_kv_norm_w, v_mla_w_ukv, v_ret_gn_w, v_ret_gn_b, v_w_branch_mla, v_w_branch_ret, v_w_out, v_final_norm_w):
    given = dict(x=x, meta=meta, norm_w=norm_w, w_in=w_in, mla_q_norm_w=mla_q_norm_w, mla_w_uq=mla_w_uq, mla_kv_norm_w=mla_kv_norm_w, mla_w_ukv=mla_w_ukv, ret_gn_w=ret_gn_w, ret_gn_b=ret_gn_b, w_branch_mla=w_branch_mla, w_branch_ret=w_branch_ret, w_out=w_out, final_norm_w=final_norm_w, loss_target=loss_target, m_meta=m_meta, m_norm_w=m_norm_w, m_w_in=m_w_in, m_mla_q_norm_w=m_mla_q_norm_w, m_mla_w_uq=m_mla_w_uq, m_mla_kv_norm_w=m_mla_kv_norm_w, m_mla_w_ukv=m_mla_w_ukv, m_ret_gn_w=m_ret_gn_w, m_ret_gn_b=m_ret_gn_b, m_w_branch_mla=m_w_branch_mla, m_w_branch_ret=m_w_branch_ret, m_w_out=m_w_out, m_final_norm_w=m_final_norm_w, v_meta=v_meta, v_norm_w=v_norm_w, v_w_in=v_w_in, v_mla_q_norm_w=v_mla_q_norm_w, v_mla_w_uq=v_mla_w_uq, v_mla_kv_norm_w=v_mla_kv_norm_w, v_mla_w_ukv=v_mla_w_ukv, v_ret_gn_w=v_ret_gn_w, v_ret_gn_b=v_ret_gn_b, v_w_branch_mla=v_w_branch_mla, v_w_branch_ret=v_w_branch_ret, v_w_out=v_w_out, v_final_norm_w=v_final_norm_w)
    weights = {n: given[n] for n in TWIN_WEIGHTS}
    shared = {n: given[n] for n in SHARED_INPUTS}
    per_example = {n: given[n] for n in ['x']}
    grad_fn = _jax.value_and_grad(_loss, argnums=(0, 1))

    def one_microbatch(ex, loss_target):
        ex = dict(ex)
        diff = ex.pop(TWIN_DIFF_INPUT)
        return grad_fn(weights, diff, {**shared, **ex}, loss_target)

    if N_MICROBATCH == 1:
        loss, (grad_w, grad_x) = one_microbatch(per_example, given["loss_target"])
    else:
        def body(carry, xs):
            loss_sum, grad_sum = carry
            l_k, (gw_k, gx_k) = one_microbatch(xs[0], xs[1])
            with _jax.named_scope("update"):
                return (loss_sum + l_k, _jax.tree.map(_jnp.add, grad_sum, gw_k)), gx_k

        init = (_jnp.zeros((), _jnp.float32), _jax.tree.map(_jnp.zeros_like, weights))
        (loss, grad_w), grad_x = _jax.lax.scan(body, init, (per_example, given["loss_target"]))
    with _jax.named_scope("update"):
        delta_w, new_m, new_v = {}, {}, {}
        for n in TWIN_WEIGHTS:
            delta_w[n], new_m[n], new_v[n] = _adamw(weights[n], grad_w[n], given["m_" + n], given["v_" + n])
    return (loss, grad_x, *[grad_w[n] for n in TWIN_WEIGHTS], *[delta_w[n] for n in TWIN_WEIGHTS],
            *[new_m[n] for n in TWIN_WEIGHTS], *[new_v[n] for n in TWIN_WEIGHTS])
```

```python
import functools
import math

import numpy as np
import jax
import jax.numpy as jnp
from jax import lax
from jax.experimental import pallas as pl
from jax.experimental.pallas import tpu as pltpu

F32 = jnp.float32
BF16 = jnp.bfloat16
MESH = pl.DeviceIdType.MESH

D = 2048
N_META = 16
BLK = 128
PAD_FRONT = BLK - N_META
HEADS = 8
DH = 128
ROPE_D = 64
Q_RANK = 512
KV_RANK = 256
WIDTH = HEADS * DH
ROPE_BASE = 10000.0
NORM_EPS = 1e-6
GN_EPS = 1e-5
NEG_INF = -1e30
ATT_SCALE = (DH + ROPE_D) ** -0.5
RET_SCALE = DH ** -0.5
IN_WIDTH = 10048
N_CHIPS = 4
IN_SHARD = IN_WIDTH // N_CHIPS
ADAM_LR, ADAM_B1, ADAM_B2, ADAM_EPS, ADAM_WD, ADAM_STEP = 0.001, 0.9, 0.999, 1e-08, 0.01, 10

ROT = Q_RANK + KV_RANK + ROPE_D
Z_MLA, R_Q, R_K, R_V, Z_RET, GATE0, GATE1 = 0, 1024, 2048, 3072, 4096, 5120, 5120 + D
C_Q = IN_WIDTH - ROT
C_KV = C_Q + Q_RANK
K_PE = C_KV + KV_RANK
PROJ_W = 10240

BIG_OUT, BIG_BM, BIG_BR, BIG_UQ, BIG_UKV, BIG_JUNK, BIG_IN = 0, 2048, 3072, 4096, 4480, 4736, 5120
BIG_ROWS = BIG_IN + PROJ_W
ZERO_ROWS = PROJ_W - IN_WIDTH
PK_OUT, PK_BM, PK_BR, PK_UQ, PK_UKV, PK_IN, PK_PAD, PK_ROWS = 0, 512, 768, 1024, 1120, 1184, 3696, 3712
PK_HALF = PK_ROWS // 2
SM_LOSS, SM_NORM, SM_QN, SM_KVN, SM_GNW, SM_GNB, SM_FIN, SM_META, SM_ROWS = 0, 1, 17, 21, 23, 31, 39, 55, 312

VMEM_LIMIT = 56 * 1024 * 1024


def _pieces(shard):
    j = shard
    out = [(PK_OUT, BIG_OUT + 512 * j, 512), (PK_BM, BIG_BM + 256 * j, 256), (PK_BR, BIG_BR + 256 * j, 256),
           (PK_UQ, BIG_UQ + 96 * j, 96), (PK_UKV, BIG_UKV + 64 * j, 64)]
    if j == 0:
        out += [(PK_IN, BIG_IN + C_Q, ROT), (PK_IN + ROT, BIG_IN, IN_SHARD - ROT)]
    else:
        out += [(PK_IN, BIG_IN + IN_SHARD * j - ROT, IN_SHARD)]
    out += [(PK_PAD, BIG_JUNK + 16 * j, 16)]
    return out


def _half_pieces(shard, half):
    lo, hi = half * PK_HALF, (half + 1) * PK_HALF
    out = []
    for p, b, n in _pieces(shard):
        s, e = max(p, lo), min(p + n, hi)
        if e > s:
            out.append((s, b + (s - p), e - s))
    return out


def _row_tile(rows, cap):
    best = BLK
    for t in range(BLK, cap + 1, BLK):
        if rows % t == 0:
            best = t
    return best


def _cparams(sem):
    return pltpu.CompilerParams(dimension_semantics=sem, vmem_limit_bytes=VMEM_LIMIT)


def _dot(a, b, form):
    dn = {"nt": (((1,), (1,)), ((), ())), "nn": (((1,), (0,)), ((), ())), "tn": (((0,), (0,)), ((), ()))}[form]
    return lax.dot_general(a, b, dn, preferred_element_type=F32)


def _sigmoid(v):
    return 1.0 / (1.0 + jnp.exp(-v))


def _matmul(a, b, form, m, n, k, tm, tn, tk, out_dtype, name, a_off=(0, 0), b_off=(0, 0), n_outer=False):
    nk = k // tk
    gi, gj = m // tm, n // tn

    def ij(g0, g1):
        return (g1, g0) if n_outer else (g0, g1)

    if form == "nt":
        a_spec = pl.BlockSpec((tm, tk), lambda g0, g1, kk: (ij(g0, g1)[0] + a_off[0], kk + a_off[1]))
        b_spec = pl.BlockSpec((tn, tk), lambda g0, g1, kk: (ij(g0, g1)[1] + b_off[0], kk + b_off[1]))
    elif form == "nn":
        a_spec = pl.BlockSpec((tm, tk), lambda g0, g1, kk: (ij(g0, g1)[0] + a_off[0], kk + a_off[1]))
        b_spec = pl.BlockSpec((tk, tn), lambda g0, g1, kk: (kk + b_off[0], ij(g0, g1)[1] + b_off[1]))
    else:
        a_spec = pl.BlockSpec((tk, tm), lambda g0, g1, kk: (kk + a_off[0], ij(g0, g1)[0] + a_off[1]))
        b_spec = pl.BlockSpec((tk, tn), lambda g0, g1, kk: (kk + b_off[0], ij(g0, g1)[1] + b_off[1]))
    o_spec = pl.BlockSpec((tm, tn), lambda g0, g1, kk: ij(g0, g1))

    def body(a_ref, b_ref, o_ref, *acc):
        p = _dot(a_ref[...], b_ref[...], form)
        if nk == 1:
            o_ref[...] = p.astype(o_ref.dtype)
        else:
            acc_ref, = acc
            kk = pl.program_id(2)

            @pl.when(kk == 0)
            def _():
                acc_ref[...] = p

            @pl.when(kk > 0)
            def _():
                acc_ref[...] += p

            @pl.when(kk == nk - 1)
            def _():
                o_ref[...] = acc_ref[...].astype(o_ref.dtype)

    return pl.pallas_call(
        body, name=name, grid=(gj, gi, nk) if n_outer else (gi, gj, nk), in_specs=[a_spec, b_spec], out_specs=o_spec,
        out_shape=jax.ShapeDtypeStruct((m, n), out_dtype),
        scratch_shapes=[] if nk == 1 else [pltpu.VMEM((tm, tn), F32)],
        compiler_params=_cparams(("parallel", "parallel", "arbitrary")),
    )(a, b)


def _rms_in(x, hp0, norm_w, s):
    def body(x_ref, hp0_ref, w_ref, h_ref, xn_ref, r_ref):
        def run(hv):
            r = lax.rsqrt(jnp.mean(hv * hv, axis=-1, keepdims=True) + NORM_EPS)
            h_ref[...] = hv
            xn_ref[...] = (hv * r * w_ref[...]).astype(BF16)
            r_ref[...] = r

        @pl.when(pl.program_id(0) == 0)
        def _():
            run(hp0_ref[...])

        @pl.when(pl.program_id(0) > 0)
        def _():
            run(x_ref[...])

    return pl.pallas_call(
        body, name="rms_in", grid=(s // BLK,),
        in_specs=[pl.BlockSpec((BLK, D), lambda i: (jnp.maximum(i - 1, 0), 0)),
                  pl.BlockSpec((BLK, D), lambda i: (0, 0)), pl.BlockSpec((1, D), lambda i: (0, 0))],
        out_specs=[pl.BlockSpec((BLK, D), lambda i: (i, 0)), pl.BlockSpec((BLK, D), lambda i: (i, 0)),
                   pl.BlockSpec((BLK, 1), lambda i: (i, 0))],
        out_shape=[jax.ShapeDtypeStruct((s, D), F32), jax.ShapeDtypeStruct((s, D), BF16),
                   jax.ShapeDtypeStruct((s, 1), F32)],
        compiler_params=_cparams(("arbitrary",)),
    )(x, hp0, norm_w)


def _rope64(t, c, sa, sb):
    return t * c + pltpu.roll(t, DH - ROPE_D // 2, 1) * sa + pltpu.roll(t, ROPE_D // 2, 1) * sb


def _rope128(t, c, sg):
    return t * c + pltpu.roll(t, DH // 2, 1) * sg


def _mla_prep(proj, qn_w, kvn_w, wqn, wqp, wukv, c64, sa64, sb64, s):
    tm = _row_tile(s, 384)

    def body(cq_ref, ckv_ref, kpe_ref, qnw_ref, kvnw_ref, wqn_ref, wqp_ref, wukv_ref, c_ref, sa_ref, sb_ref,
             cqn_ref, rq_ref, ckvn_ref, rkv_ref, qn_ref, qp_ref, kv_ref, kp_ref):
        cq = cq_ref[...].astype(F32)
        rq = lax.rsqrt(jnp.mean(cq * cq, axis=-1, keepdims=True) + NORM_EPS)
        cqn = (cq * rq * qnw_ref[...]).astype(BF16)
        ckv = ckv_ref[...].astype(F32)
        rkv = lax.rsqrt(jnp.mean(ckv * ckv, axis=-1, keepdims=True) + NORM_EPS)
        ckvn = (ckv * rkv * kvnw_ref[...]).astype(BF16)
        cqn_ref[...] = cqn
        rq_ref[...] = rq
        ckvn_ref[...] = ckvn
        rkv_ref[...] = rkv
        qn_ref[...] = _dot(cqn, wqn_ref[...], "nt").astype(BF16)
        kv_ref[...] = _dot(ckvn, wukv_ref[...], "nt").astype(BF16)
        c, sa, sb = c_ref[...], sa_ref[...], sb_ref[...]
        qp = _dot(cqn, wqp_ref[...], "nt")
        for h in range(HEADS):
            sl = slice(h * DH, (h + 1) * DH)
            qp_ref[:, sl] = _rope64(qp[:, sl], c, sa, sb).astype(BF16)
        kp_ref[...] = _rope64(kpe_ref[...].astype(F32), c, sa, sb).astype(BF16)

    row = lambda w, cb: pl.BlockSpec((tm, w), lambda i: (i, cb))
    full = lambda a: pl.BlockSpec(a.shape, lambda i: (0, 0))
    return pl.pallas_call(
        body, name="mla_prep", grid=(s // tm,),
        in_specs=[row(Q_RANK, C_Q // Q_RANK), row(KV_RANK, C_KV // KV_RANK), row(DH, K_PE // DH),
                  full(qn_w), full(kvn_w), full(wqn), full(wqp), full(wukv), row(DH, 0), row(DH, 0), row(DH, 0)],
        out_specs=[row(Q_RANK, 0), row(1, 0), row(KV_RANK, 0), row(1, 0), row(WIDTH, 0), row(WIDTH, 0),
                   row(2 * WIDTH, 0), row(DH, 0)],
        out_shape=[jax.ShapeDtypeStruct((s, Q_RANK), BF16), jax.ShapeDtypeStruct((s, 1), F32),
                   jax.ShapeDtypeStruct((s, KV_RANK), BF16), jax.ShapeDtypeStruct((s, 1), F32),
                   jax.ShapeDtypeStruct((s, WIDTH), BF16), jax.ShapeDtypeStruct((s, WIDTH), BF16),
                   jax.ShapeDtypeStruct((s, 2 * WIDTH), BF16), jax.ShapeDtypeStruct((s, DH), BF16)],
        compiler_params=_cparams(("parallel",)),
    )(proj, proj, proj, qn_w, kvn_w, wqn, wqp, wukv, c64, sa64, sb64)


def _att_mask(qi, ki, t):
    row = qi * t + lax.broadcasted_iota(jnp.int32, (t, t), 0)
    col = ki * t + lax.broadcasted_iota(jnp.int32, (t, t), 1)
    return (col <= row) & (col >= PAD_FRONT)


def _silu(z):
    return z * _sigmoid(z)


def _attn_fwd(qn, qp, kv, kp, proj, s):
    t = _row_tile(s, 384)

    def body(qn_ref, qp_ref, kn_ref, v_ref, kp_ref, z_ref, o_ref, y_ref, lse_ref):
        qi = pl.program_id(1)
        q_n, q_p = qn_ref[...], qp_ref[...]

        def step(ki, carry):
            m, l, acc = carry
            ks = pl.ds(pl.multiple_of(ki * t, t), t)
            sc = (_dot(q_n, kn_ref[ks, :], "nt") + _dot(q_p, kp_ref[ks, :], "nt")) * ATT_SCALE
            sc = jnp.where(_att_mask(qi, ki, t), sc, NEG_INF)
            m_new = jnp.maximum(m, jnp.max(sc, axis=-1, keepdims=True))
            alpha = jnp.exp(m - m_new)
            p = jnp.exp(sc - m_new)
            l = alpha * l + jnp.sum(p, axis=-1, keepdims=True)
            acc = alpha * acc + _dot(p.astype(BF16), v_ref[ks, :], "nn")
            return m_new, l, acc

        init = (jnp.full((t, 1), NEG_INF, F32), jnp.zeros((t, 1), F32), jnp.zeros((t, DH), F32))
        m, l, acc = lax.fori_loop(0, qi + 1, step, init)
        o = acc / l
        o_ref[...] = o.astype(BF16)
        y_ref[...] = (o * _silu(z_ref[...].astype(F32))).astype(BF16)
        lse_ref[0] = m + jnp.log(l)

    qtile = pl.BlockSpec((t, DH), lambda h, i: (i, h))
    return pl.pallas_call(
        body, name="attn_fwd", grid=(HEADS, s // t),
        in_specs=[qtile, qtile, pl.BlockSpec((s, DH), lambda h, i: (0, 2 * h)),
                  pl.BlockSpec((s, DH), lambda h, i: (0, 2 * h + 1)), pl.BlockSpec((s, DH), lambda h, i: (0, 0)),
                  pl.BlockSpec((t, DH), lambda h, i: (i, Z_MLA // DH + h))],
        out_specs=[qtile, qtile, pl.BlockSpec((1, t, 1), lambda h, i: (h, i, 0))],
        out_shape=[jax.ShapeDtypeStruct((s, WIDTH), BF16), jax.ShapeDtypeStruct((s, WIDTH), BF16),
                   jax.ShapeDtypeStruct((HEADS, s, 1), F32)],
        compiler_params=_cparams(("parallel", "arbitrary")),
    )(qn, qp, kv, kv, kp, proj)


def _ret_consts():
    log_g = np.log1p(-(2.0 ** (-5.0 - np.arange(HEADS, dtype=np.float64))))
    n = np.arange(BLK, dtype=np.float64)
    diff = n[:, None] - n[None, :]
    decay = np.where(diff >= 0, np.exp(log_g[:, None, None] * np.maximum(diff, 0.0)), 0.0)
    zeta = np.exp(log_g[:, None] * (BLK - 1.0 - n))[:, :, None]
    xi = np.exp(log_g[:, None] * (n + 1.0))[:, :, None]
    gamma = [float(np.float32(np.exp(g * BLK))) for g in log_g]
    return (jnp.asarray(decay, F32), jnp.asarray(zeta, F32), jnp.asarray(xi, F32)), gamma


def _ret_fwd(proj, gn_w, gn_b, c128, s128, consts, gamma, s):
    nb = s // BLK
    decay, zeta, xi = consts

    def body(rq_ref, rk_ref, rv_ref, z_ref, gw_ref, gb_ref, c_ref, s_ref, dm_ref, zt_ref, xi_ref,
             y_ref, on_ref, rstd_ref, qr_ref, kr_ref, rall_ref, state):
        @pl.when(pl.program_id(0) == 0)
        def _():
            state[...] = jnp.zeros_like(state)

        c, sg = c_ref[...], s_ref[...]
        for h in range(HEADS):
            sl = slice(h * DH, (h + 1) * DH)
            q = _rope128(rq_ref[:, sl].astype(F32), c, sg).astype(BF16)
            kf = _rope128(rk_ref[:, sl].astype(F32), c, sg) * RET_SCALE
            k = kf.astype(BF16)
            v = rv_ref[:, sl]
            qr_ref[:, sl] = q
            kr_ref[:, sl] = k
            r_prev = state[h]
            rall_ref[0, h] = r_prev
            a = _dot(q, k, "nt") * dm_ref[h]
            o = _dot(a.astype(BF16), v, "nn") + _dot(q, r_prev.astype(BF16), "nn") * xi_ref[h]
            state[h] = r_prev * gamma[h] + _dot((kf * zt_ref[h]).astype(BF16), v, "tn")
            mu = jnp.mean(o, axis=-1, keepdims=True)
            var = jnp.mean(jnp.square(o - mu), axis=-1, keepdims=True)
            rstd = lax.rsqrt(var + GN_EPS)
            on = (o - mu) * rstd
            rstd_ref[h] = rstd
            on_ref[:, sl] = on.astype(BF16)
            ogn = on * gw_ref[:, sl] + gb_ref[:, sl]
            y_ref[:, sl] = (ogn * _silu(z_ref[:, sl].astype(F32))).astype(BF16)

    seg = lambda cb: pl.BlockSpec((BLK, WIDTH), lambda i: (i, cb))
    full = lambda a: pl.BlockSpec(a.shape, lambda i: (0,) * a.ndim)
    tab = pl.BlockSpec((BLK, DH), lambda i: (i, 0))
    return pl.pallas_call(
        body, name="ret_fwd", grid=(nb,),
        in_specs=[seg(R_Q // WIDTH), seg(R_K // WIDTH), seg(R_V // WIDTH), seg(Z_RET // WIDTH), full(gn_w), full(gn_b),
                  tab, tab, full(decay), full(zeta), full(xi)],
        out_specs=[seg(0), seg(0), pl.BlockSpec((HEADS, BLK, 1), lambda i: (0, i, 0)), seg(0), seg(0),
                   pl.BlockSpec((1, HEADS, DH, DH), lambda i: (i, 0, 0, 0))],
        out_shape=[jax.ShapeDtypeStruct((s, WIDTH), BF16), jax.ShapeDtypeStruct((s, WIDTH), BF16),
                   jax.ShapeDtypeStruct((HEADS, s, 1), F32), jax.ShapeDtypeStruct((s, WIDTH), BF16),
                   jax.ShapeDtypeStruct((s, WIDTH), BF16), jax.ShapeDtypeStruct((nb, HEADS, DH, DH), F32)],
        scratch_shapes=[pltpu.VMEM((HEADS, DH, DH), F32)],
        compiler_params=_cparams(("arbitrary",)),
    )(proj, proj, proj, proj, gn_w, gn_b, c128, s128, decay, zeta, xi)


def _merge_fwd(y_mla, y_ret, wbm, wbr, proj, s):
    tm, tn = _row_tile(s, 1408), 512

    def body(ym_ref, yr_ref, wm_ref, wr_ref, g0_ref, g1_ref, um_ref, ur_ref, mg_ref):
        um = _dot(ym_ref[...], wm_ref[...], "nt")
        ur = _dot(yr_ref[...], wr_ref[...], "nt")
        um_ref[...] = um.astype(BF16)
        ur_ref[...] = ur.astype(BF16)
        mg_ref[...] = (_sigmoid(g0_ref[...].astype(F32)) * um + _sigmoid(g1_ref[...].astype(F32)) * ur).astype(BF16)

    yspec = pl.BlockSpec((tm, WIDTH), lambda i, j: (i, 0))
    wspec = pl.BlockSpec((tn, WIDTH), lambda i, j: (j, 0))
    ospec = pl.BlockSpec((tm, tn), lambda i, j: (i, j))
    return pl.pallas_call(
        body, name="merge_fwd", grid=(s // tm, D // tn),
        in_specs=[yspec, yspec, wspec, wspec, pl.BlockSpec((tm, tn), lambda i, j: (i, GATE0 // tn + j)),
                  pl.BlockSpec((tm, tn), lambda i, j: (i, GATE1 // tn + j))],
        out_specs=[ospec, ospec, ospec],
        out_shape=[jax.ShapeDtypeStruct((s, D), BF16)] * 3,
        compiler_params=_cparams(("parallel", "parallel")),
    )(y_mla, y_ret, wbm, wbr, proj, proj)


def _out_fwd(merged, big, h, s):
    tm, tn = _row_tile(s, 1408), 512

    def body(m_ref, w_ref, h_ref, o_ref):
        o_ref[...] = h_ref[...] + _dot(m_ref[...], w_ref[...], "nn")

    return pl.pallas_call(
        body, name="out_fwd", grid=(s // tm, D // tn),
        in_specs=[pl.BlockSpec((tm, D), lambda i, j: (i, 0)), pl.BlockSpec((D, tn), lambda i, j: (BIG_OUT // D, j)),
                  pl.BlockSpec((tm, tn), lambda i, j: (i, j))],
        out_specs=pl.BlockSpec((tm, tn), lambda i, j: (i, j)),
        out_shape=jax.ShapeDtypeStruct((s, D), F32),
        compiler_params=_cparams(("parallel", "parallel")),
    )(merged, big, h)


def _loss_bwd(h2, target, fin_w, s):
    nb = s // BLK

    def body(h2_ref, t_ref, w_ref, dh_ref, dhb_ref, loss_ref, dw_ref):
        i = pl.program_id(0)

        @pl.when(i == 0)
        def _():
            dh_ref[...] = jnp.zeros_like(dh_ref)
            dhb_ref[...] = jnp.zeros_like(dhb_ref)
            loss_ref[...] = jnp.zeros_like(loss_ref)
            dw_ref[...] = jnp.zeros_like(dw_ref)

        @pl.when(i > 0)
        def _():
            hv = h2_ref[...]
            w = w_ref[...]
            r = lax.rsqrt(jnp.mean(hv * hv, axis=-1, keepdims=True) + NORM_EPS)
            nrm = hv * r
            e = nrm * w - t_ref[...]
            loss_ref[...] += jnp.full(loss_ref.shape, 0.5 / D, F32) * jnp.sum(e * e)
            dy = e * (1.0 / D)
            dw_ref[...] += jnp.sum(dy * nrm, axis=0, keepdims=True)
            g = dy * w
            dh = r * (g - nrm * jnp.mean(g * nrm, axis=-1, keepdims=True))
            dh_ref[...] = dh
            dhb_ref[...] = dh.astype(BF16)

    blk = pl.BlockSpec((BLK, D), lambda i: (i, 0))
    return pl.pallas_call(
        body, name="loss_bwd", grid=(nb,),
        in_specs=[blk, pl.BlockSpec((BLK, D), lambda i: (jnp.maximum(i - 1, 0), 0)), pl.BlockSpec((1, D), lambda i: (0, 0))],
        out_specs=[blk, blk, pl.BlockSpec((8, 128), lambda i: (0, 0)), pl.BlockSpec((1, D), lambda i: (0, 0))],
        out_shape=[jax.ShapeDtypeStruct((s, D), F32), jax.ShapeDtypeStruct((s, D), BF16),
                   jax.ShapeDtypeStruct((8, 128), F32), jax.ShapeDtypeStruct((1, D), F32)],
        compiler_params=_cparams(("arbitrary",)),
    )(h2, target, fin_w)


def _merge_bwd(dh2b, big, u_mla, u_ret, proj, s):
    tm, tn = _row_tile(s, 1408), 512

    def body(d_ref, w_ref, um_ref, ur_ref, g0_ref, g1_ref, dg0_ref, dg1_ref, dum_ref, dur_ref):
        dm = _dot(d_ref[...], w_ref[...], "nt")
        g0 = _sigmoid(g0_ref[...].astype(F32))
        g1 = _sigmoid(g1_ref[...].astype(F32))
        dg0_ref[...] = (dm * um_ref[...].astype(F32) * g0 * (1.0 - g0)).astype(BF16)
        dg1_ref[...] = (dm * ur_ref[...].astype(F32) * g1 * (1.0 - g1)).astype(BF16)
        dum_ref[...] = (dm * g0).astype(BF16)
        dur_ref[...] = (dm * g1).astype(BF16)

    ospec = pl.BlockSpec((tm, tn), lambda i, j: (i, j))
    return pl.pallas_call(
        body, name="merge_bwd", grid=(s // tm, D // tn),
        in_specs=[pl.BlockSpec((tm, D), lambda i, j: (i, 0)), pl.BlockSpec((tn, D), lambda i, j: (BIG_OUT // tn + j, 0)),
                  ospec, ospec, pl.BlockSpec((tm, tn), lambda i, j: (i, GATE0 // tn + j)),
                  pl.BlockSpec((tm, tn), lambda i, j: (i, GATE1 // tn + j))],
        out_specs=[ospec] * 4,
        out_shape=[jax.ShapeDtypeStruct((s, D), BF16)] * 4,
        compiler_params=_cparams(("parallel", "parallel")),
    )(dh2b, big, u_mla, u_ret, proj, proj)


def _dsilu(z):
    sg = _sigmoid(z)
    return sg * (1.0 + z * (1.0 - sg))


def _branch_mla_bwd(du, wbm, o_mla, proj, s):
    tm = _row_tile(s, 384)

    def body(du_ref, w_ref, o_ref, z_ref, do_ref, dz_ref):
        dy = _dot(du_ref[...], w_ref[...], "nn")
        z = z_ref[...].astype(F32)
        do_ref[...] = (dy * _silu(z)).astype(BF16)
        dz_ref[...] = (dy * o_ref[...].astype(F32) * _dsilu(z)).astype(BF16)

    row = lambda w, cb: pl.BlockSpec((tm, w), lambda i: (i, cb))
    return pl.pallas_call(
        body, name="branch_mla_bwd", grid=(s // tm,),
        in_specs=[row(D, 0), pl.BlockSpec((D, WIDTH), lambda i: (0, 0)), row(WIDTH, 0), row(WIDTH, Z_MLA // WIDTH)],
        out_specs=[row(WIDTH, 0), row(WIDTH, 0)],
        out_shape=[jax.ShapeDtypeStruct((s, WIDTH), BF16)] * 2,
        compiler_params=_cparams(("parallel",)),
    )(du, wbm, o_mla, proj)


def _branch_ret_bwd(du, wbr, on, rstd, gn_w, gn_b, proj, s):
    tm = BLK

    def body(du_ref, w_ref, on_ref, rstd_ref, gw_ref, gb_ref, z_ref, do_ref, dz_ref, dgw_ref, dgb_ref):
        @pl.when(pl.program_id(0) == 0)
        def _():
            dgw_ref[...] = jnp.zeros_like(dgw_ref)
            dgb_ref[...] = jnp.zeros_like(dgb_ref)

        dy = _dot(du_ref[...], w_ref[...], "nn")
        z = z_ref[...].astype(F32)
        on = on_ref[...].astype(F32)
        gw = gw_ref[...]
        dogn = dy * _silu(z)
        dz_ref[...] = (dy * (on * gw + gb_ref[...]) * _dsilu(z)).astype(BF16)
        dgw_ref[...] += jnp.sum(dogn * on, axis=0, keepdims=True)
        dgb_ref[...] += jnp.sum(dogn, axis=0, keepdims=True)
        don = dogn * gw
        for h in range(HEADS):
            sl = slice(h * DH, (h + 1) * DH)
            dn, nh = don[:, sl], on[:, sl]
            do = rstd_ref[h] * (dn - jnp.mean(dn, axis=-1, keepdims=True)
                                - nh * jnp.mean(dn * nh, axis=-1, keepdims=True))
            do_ref[:, sl] = do.astype(BF16)

    row = lambda w, cb: pl.BlockSpec((tm, w), lambda i: (i, cb))
    vec = pl.BlockSpec((1, WIDTH), lambda i: (0, 0))
    return pl.pallas_call(
        body, name="branch_ret_bwd", grid=(s // tm,),
        in_specs=[row(D, 0), pl.BlockSpec((D, WIDTH), lambda i: (0, 0)), row(WIDTH, 0),
                  pl.BlockSpec((HEADS, tm, 1), lambda i: (0, i, 0)), vec, vec, row(WIDTH, Z_RET // WIDTH)],
        out_specs=[row(WIDTH, 0), row(WIDTH, 0), vec, vec],
        out_shape=[jax.ShapeDtypeStruct((s, WIDTH), BF16)] * 2 + [jax.ShapeDtypeStruct((1, WIDTH), F32)] * 2,
        compiler_params=_cparams(("arbitrary",)),
    )(du, wbr, on, rstd, gn_w, gn_b, proj)


def _ret_bwd(qr, kr, proj, rall, do_ret, c128, s128, consts, gamma, s):
    nb = s // BLK
    decay, zeta, xi = consts

    def body(q_ref, k_ref, v_ref, r_ref, do_ref, c_ref, s_ref, dm_ref, zt_ref, xi_ref, dq_ref, dk_ref, dv_ref, gstate):
        @pl.when(pl.program_id(0) == 0)
        def _():
            gstate[...] = jnp.zeros_like(gstate)

        c, sg = c_ref[...], s_ref[...]
        for h in range(HEADS):
            sl = slice(h * DH, (h + 1) * DH)
            q, k, v, do = q_ref[:, sl], k_ref[:, sl], v_ref[:, sl], do_ref[:, sl]
            dm = dm_ref[h]
            g_next = gstate[h]
            gb = g_next.astype(BF16)
            a = (_dot(q, k, "nt") * dm).astype(BF16)
            da = (_dot(do, v, "nt") * dm).astype(BF16)
            dox = (do.astype(F32) * xi_ref[h]).astype(BF16)
            dq = _dot(da, k, "nn") + _dot(dox, r_ref[0, h].astype(BF16), "nt")
            dk = _dot(da, q, "tn") + _dot(v, gb, "nt") * zt_ref[h]
            kz = (k.astype(F32) * zt_ref[h]).astype(BF16)
            dv = _dot(a, do, "tn") + _dot(kz, gb, "nn")
            gstate[h] = g_next * gamma[h] + _dot(q, dox, "tn")
            dk = dk * RET_SCALE
            dq_ref[:, sl] = _rope128(dq, c, -sg).astype(BF16)
            dk_ref[:, sl] = _rope128(dk, c, -sg).astype(BF16)
            dv_ref[:, sl] = dv.astype(BF16)

    rev = lambda cb: pl.BlockSpec((BLK, WIDTH), lambda i: (nb - 1 - i, cb))
    full = lambda a: pl.BlockSpec(a.shape, lambda i: (0,) * a.ndim)
    tab = pl.BlockSpec((BLK, DH), lambda i: (nb - 1 - i, 0))
    return pl.pallas_call(
        body, name="ret_bwd", grid=(nb,),
        in_specs=[rev(0), rev(0), rev(R_V // WIDTH), pl.BlockSpec((1, HEADS, DH, DH), lambda i: (nb - 1 - i, 0, 0, 0)),
                  rev(0), tab, tab, full(decay), full(zeta), full(xi)],
        out_specs=[rev(0), rev(0), rev(0)],
        out_shape=[jax.ShapeDtypeStruct((s, WIDTH), BF16)] * 3,
        scratch_shapes=[pltpu.VMEM((HEADS, DH, DH), F32)],
        compiler_params=_cparams(("arbitrary",)),
    )(qr, kr, proj, rall, do_ret, c128, s128, decay, zeta, xi)


def _attn_bwd_dq(qn, qp, kv, kp, do, o, lse, s):
    t = _row_tile(s, 384)

    def body(qn_ref, qp_ref, kn_ref, v_ref, kp_ref, do_ref, o_ref, lse_ref, dqn_ref, dqp_ref):
        qi = pl.program_id(1)
        q_n, q_p, dov = qn_ref[...], qp_ref[...], do_ref[...]
        lse_t = lse_ref[0]
        delta = jnp.sum(dov.astype(F32) * o_ref[...].astype(F32), axis=-1, keepdims=True)

        def step(ki, carry):
            dqn, dqp = carry
            ks = pl.ds(pl.multiple_of(ki * t, t), t)
            kn, kpv = kn_ref[ks, :], kp_ref[ks, :]
            sc = (_dot(q_n, kn, "nt") + _dot(q_p, kpv, "nt")) * ATT_SCALE
            sc = jnp.where(_att_mask(qi, ki, t), sc, NEG_INF)
            p = jnp.exp(sc - lse_t)
            dp = _dot(dov, v_ref[ks, :], "nt")
            ds = (p * (dp - delta) * ATT_SCALE).astype(BF16)
            return dqn + _dot(ds, kn, "nn"), dqp + _dot(ds, kpv, "nn")

        dqn, dqp = lax.fori_loop(0, qi + 1, step, (jnp.zeros((t, DH), F32), jnp.zeros((t, DH), F32)))
        dqn_ref[...] = dqn.astype(BF16)
        dqp_ref[...] = dqp.astype(BF16)

    qtile = pl.BlockSpec((t, DH), lambda h, i: (i, h))
    return pl.pallas_call(
        body, name="attn_bwd_dq", grid=(HEADS, s // t),
        in_specs=[qtile, qtile, pl.BlockSpec((s, DH), lambda h, i: (0, 2 * h)),
                  pl.BlockSpec((s, DH), lambda h, i: (0, 2 * h + 1)), pl.BlockSpec((s, DH), lambda h, i: (0, 0)),
                  qtile, qtile, pl.BlockSpec((1, t, 1), lambda h, i: (h, i, 0))],
        out_specs=[qtile, qtile],
        out_shape=[jax.ShapeDtypeStruct((s, WIDTH), BF16)] * 2,
        compiler_params=_cparams(("parallel", "arbitrary")),
    )(qn, qp, kv, kv, kp, do, o, lse)


def _attn_bwd_dkv(qn, qp, kv, kp, do, o, lse, s):
    t = _row_tile(s, 384)
    nq = s // t

    def body(qn_ref, qp_ref, kn_ref, v_ref, kp_ref, do_ref, o_ref, lse_ref, dkv_ref, dkp_ref):
        ki = pl.program_id(1)
        kn, v, kpv = kn_ref[...], v_ref[...], kp_ref[...]

        def step(qi, carry):
            dkn, dv, dkp = carry
            qs = pl.ds(pl.multiple_of(qi * t, t), t)
            q_n, q_p, dov = qn_ref[qs, :], qp_ref[qs, :], do_ref[qs, :]
            delta = jnp.sum(dov.astype(F32) * o_ref[qs, :].astype(F32), axis=-1, keepdims=True)
            sc = (_dot(q_n, kn, "nt") + _dot(q_p, kpv, "nt")) * ATT_SCALE
            sc = jnp.where(_att_mask(qi, ki, t), sc, NEG_INF)
            p = jnp.exp(sc - lse_ref[0, qs, :])
            dp = _dot(dov, v, "nt")
            ds = (p * (dp - delta) * ATT_SCALE).astype(BF16)
            return (dkn + _dot(ds, q_n, "tn"), dv + _dot(p.astype(BF16), dov, "tn"), dkp + _dot(ds, q_p, "tn"))

        z = jnp.zeros((t, DH), F32)
        dkn, dv, dkp = lax.fori_loop(ki, nq, step, (z, z, z))
        dkv_ref[:, :DH] = dkn.astype(BF16)
        dkv_ref[:, DH:] = dv.astype(BF16)
        dkp_ref[0] = dkp

    full = lambda: pl.BlockSpec((s, DH), lambda h, i: (0, h))
    return pl.pallas_call(
        body, name="attn_bwd_dkv", grid=(HEADS, nq),
        in_specs=[full(), full(), pl.BlockSpec((t, DH), lambda h, i: (i, 2 * h)),
                  pl.BlockSpec((t, DH), lambda h, i: (i, 2 * h + 1)), pl.BlockSpec((t, DH), lambda h, i: (i, 0)),
                  full(), full(), pl.BlockSpec((1, s, 1), lambda h, i: (h, 0, 0))],
        out_specs=[pl.BlockSpec((t, 2 * DH), lambda h, i: (i, h)), pl.BlockSpec((1, t, DH), lambda h, i: (h, i, 0))],
        out_shape=[jax.ShapeDtypeStruct((s, 2 * WIDTH), BF16), jax.ShapeDtypeStruct((HEADS, s, DH), F32)],
        compiler_params=_cparams(("parallel", "arbitrary")),
    )(qn, qp, kv, kv, kp, do, o, lse)


def _rms_bwd(dn, nhat, r, w):
    g = dn * w
    return r * (g - nhat * jnp.mean(g * nhat, axis=-1, keepdims=True)), dn * nhat


def _mla_prep_bwd(dqn, dqp, dkv, dkp_h, proj, rq, rkv, qn_w, kvn_w, wqn, wqp, wukv, c64, sa64, sb64, s):
    tm = _row_tile(s, 384)

    def body(dqn_ref, dqp_ref, dkv_ref, dkp_ref, cq_ref, ckv_ref, rq_ref, rkv_ref, qnw_ref, kvnw_ref,
             wqn_ref, wqp_ref, wukv_ref, c_ref, sa_ref, sb_ref,
             dcq_ref, dckv_ref, dkpe_ref, dqpr_ref, dqnw_ref, dkvnw_ref):
        @pl.when(pl.program_id(0) == 0)
        def _():
            dqnw_ref[...] = jnp.zeros_like(dqnw_ref)
            dkvnw_ref[...] = jnp.zeros_like(dkvnw_ref)

        c, sa, sb = c_ref[...], sa_ref[...], sb_ref[...]
        for h in range(HEADS):
            sl = slice(h * DH, (h + 1) * DH)
            dqpr_ref[:, sl] = _rope64(dqp_ref[:, sl].astype(F32), c, -sa, -sb).astype(BF16)
        dcqn = _dot(dqn_ref[...], wqn_ref[...], "nn") + _dot(dqpr_ref[...], wqp_ref[...], "nn")
        rq_v = rq_ref[...]
        dcq, prod = _rms_bwd(dcqn, cq_ref[...].astype(F32) * rq_v, rq_v, qnw_ref[...])
        dcq_ref[...] = dcq.astype(BF16)
        dqnw_ref[...] += jnp.sum(prod, axis=0, keepdims=True)
        dckvn = _dot(dkv_ref[...], wukv_ref[...], "nn")
        rkv_v = rkv_ref[...]
        dckv, prod = _rms_bwd(dckvn, ckv_ref[...].astype(F32) * rkv_v, rkv_v, kvnw_ref[...])
        dckv_ref[...] = dckv.astype(BF16)
        dkvnw_ref[...] += jnp.sum(prod, axis=0, keepdims=True)
        dkp = dkp_ref[0]
        for h in range(1, HEADS):
            dkp = dkp + dkp_ref[h]
        dkpe_ref[:, :DH] = _rope64(dkp, c, -sa, -sb).astype(BF16)
        dkpe_ref[:, DH:] = jnp.zeros((tm, DH), BF16)

    row = lambda w, cb: pl.BlockSpec((tm, w), lambda i: (i, cb))
    full = lambda a: pl.BlockSpec(a.shape, lambda i: (0, 0))
    return pl.pallas_call(
        body, name="mla_prep_bwd", grid=(s // tm,),
        in_specs=[row(WIDTH, 0), row(WIDTH, 0), row(2 * WIDTH, 0), pl.BlockSpec((HEADS, tm, DH), lambda i: (0, i, 0)),
                  row(Q_RANK, C_Q // Q_RANK), row(KV_RANK, C_KV // KV_RANK), row(1, 0), row(1, 0),
                  full(qn_w), full(kvn_w), full(wqn), full(wqp), full(wukv), row(DH, 0), row(DH, 0), row(DH, 0)],
        out_specs=[row(Q_RANK, 0), row(KV_RANK, 0), row(2 * DH, 0), row(WIDTH, 0),
                   pl.BlockSpec((1, Q_RANK), lambda i: (0, 0)), pl.BlockSpec((1, KV_RANK), lambda i: (0, 0))],
        out_shape=[jax.ShapeDtypeStruct((s, Q_RANK), BF16), jax.ShapeDtypeStruct((s, KV_RANK), BF16),
                   jax.ShapeDtypeStruct((s, 2 * DH), BF16), jax.ShapeDtypeStruct((s, WIDTH), BF16),
                   jax.ShapeDtypeStruct((1, Q_RANK), F32), jax.ShapeDtypeStruct((1, KV_RANK), F32)],
        compiler_params=_cparams(("arbitrary",)),
    )(dqn, dqp, dkv, dkp_h, proj, proj, rq, rkv, qn_w, kvn_w, wqn, wqp, wukv, c64, sa64, sb64)


def _rms_in_bwd(dxn, h, r, dh2, norm_w, s):
    def body(dxn_ref, h_ref, r_ref, dh2_ref, w_ref, gx_ref, gm_ref, dw_ref):
        i = pl.program_id(0)
        r_v = r_ref[...]
        dx, prod = _rms_bwd(dxn_ref[...], h_ref[...] * r_v, r_v, w_ref[...])
        dh = dh2_ref[...] + dx

        @pl.when(i == 0)
        def _():
            gm_ref[...] = dh
            dw_ref[...] = jnp.sum(prod, axis=0, keepdims=True)

        @pl.when(i > 0)
        def _():
            gx_ref[...] = dh
            dw_ref[...] += jnp.sum(prod, axis=0, keepdims=True)

    blk = pl.BlockSpec((BLK, D), lambda i: (i, 0))
    return pl.pallas_call(
        body, name="rms_in_bwd", grid=(s // BLK,),
        in_specs=[blk, blk, pl.BlockSpec((BLK, 1), lambda i: (i, 0)), blk, pl.BlockSpec((1, D), lambda i: (0, 0))],
        out_specs=[pl.BlockSpec((BLK, D), lambda i: (jnp.maximum(i - 1, 0), 0)), pl.BlockSpec((BLK, D), lambda i: (0, 0)),
                   pl.BlockSpec((1, D), lambda i: (0, 0))],
        out_shape=[jax.ShapeDtypeStruct((s - BLK, D), F32), jax.ShapeDtypeStruct((BLK, D), F32),
                   jax.ShapeDtypeStruct((1, D), F32)],
        compiler_params=_cparams(("arbitrary",)),
    )(dxn, h, r, dh2, norm_w)


def _adam_math(w, g, m, v):
    m = ADAM_B1 * m + (1.0 - ADAM_B1) * g
    v = ADAM_B2 * v + (1.0 - ADAM_B2) * (g * g)
    m_hat = m / (1.0 - ADAM_B1 ** ADAM_STEP)
    v_hat = v / (1.0 - ADAM_B2 ** ADAM_STEP)
    return -ADAM_LR * (m_hat / (jnp.sqrt(v_hat) + ADAM_EPS) + ADAM_WD * w), m, v


def _adamw(w, g, m, v, name):
    rows, cols = w.shape
    tr = rows
    for cand in (128, 64, 32, 16, 8):
        if rows % cand == 0:
            tr = cand
            break

    def body(w_ref, g_ref, m_ref, v_ref, d_ref, nm_ref, nv_ref):
        d_ref[...], nm_ref[...], nv_ref[...] = _adam_math(w_ref[...], g_ref[...], m_ref[...], v_ref[...])

    spec = pl.BlockSpec((tr, cols), lambda i: (i, 0))
    return pl.pallas_call(
        body, name=name, grid=(rows // tr,), in_specs=[spec] * 4, out_specs=[spec] * 3,
        out_shape=[jax.ShapeDtypeStruct((rows, cols), F32)] * 3,
        compiler_params=_cparams(("parallel",)),
    )(w, g, m, v)


def _adamw_small(w, gall, m, v):
    def body(w_ref, g_ref, m_ref, v_ref, gs_ref, d_ref, nm_ref, nv_ref):
        g = g_ref[0]
        for dev in range(1, 8):
            g = g + g_ref[dev]
        gs_ref[...] = g
        d_ref[...], nm_ref[...], nv_ref[...] = _adam_math(w_ref[...], g, m_ref[...], v_ref[...])

    return pl.pallas_call(
        body, name="adamw_small", out_shape=[jax.ShapeDtypeStruct((SM_ROWS, 128), F32)] * 4,
        compiler_params=pltpu.CompilerParams(vmem_limit_bytes=VMEM_LIMIT),
    )(w, gall, m, v)


def _add2(a, b, name):
    rows, cols = a.shape
    tr = 464

    def body(a_ref, b_ref, o_ref):
        o_ref[...] = (a_ref[...].astype(F32) + b_ref[...].astype(F32)).astype(BF16)

    spec = pl.BlockSpec((tr, cols), lambda i: (i, 0))
    return pl.pallas_call(
        body, name=name, grid=(rows // tr,), in_specs=[spec] * 2, out_specs=spec,
        out_shape=jax.ShapeDtypeStruct((rows, cols), BF16), compiler_params=_cparams(("parallel",)),
    )(a, b)


def _add4(mine, land, name):
    rows, cols = mine.shape
    tr = 464

    def body(a_ref, l_ref, o_ref):
        o_ref[...] = ((a_ref[...].astype(F32) + l_ref[0].astype(F32)) + l_ref[1].astype(F32)) + l_ref[2].astype(F32)

    return pl.pallas_call(
        body, name=name, grid=(rows // tr,),
        in_specs=[pl.BlockSpec((tr, cols), lambda i: (i, 0)), pl.BlockSpec((3, tr, cols), lambda i: (0, i, 0))],
        out_specs=pl.BlockSpec((tr, cols), lambda i: (i, 0)),
        out_shape=jax.ShapeDtypeStruct((rows, cols), F32), compiler_params=_cparams(("parallel",)),
    )(mine, land)


ANY = pl.BlockSpec(memory_space=pl.ANY)


def _mesh_pos():
    return lax.axis_index("x"), lax.axis_index("y"), lax.axis_index("c")


def _chip_peer(x, y, c, r):
    return (jnp.bitwise_xor(x, r >> 1), jnp.bitwise_xor(y, r & 1), c)


def _allgather(pack, zrows, meta_loc):
    def body(pack_ref, z_ref, meta_ref, big_ref, metaf_ref, lsem, ssem, rsem, fssem, frsem, msem_s, msem_r):
        x, y, c = _mesh_pos()
        j = 2 * x + y
        me, sibling = (x, y, c), (x, y, 1 - c)

        def half_wait(s_sem, r_sem):
            rows = big_ref.at[pl.ds(0, PK_HALF), :]
            return pltpu.make_async_remote_copy(src_ref=rows, dst_ref=rows, send_sem=s_sem, recv_sem=r_sem,
                                                device_id=me, device_id_type=MESH)

        zero = pltpu.make_async_copy(z_ref, big_ref.at[pl.ds(BIG_IN + IN_WIDTH, ZERO_ROWS), :], lsem.at[0])
        zero.start()
        own_meta = pltpu.make_async_copy(meta_ref, metaf_ref.at[j], lsem.at[1])
        own_meta.start()
        for s in range(N_CHIPS):
            @pl.when(j == s)
            def _():
                for p, b, n in _pieces(s):
                    pltpu.make_async_copy(pack_ref.at[pl.ds(p, n), :], big_ref.at[pl.ds(b, n), :], lsem.at[2]).start()

            for hf in range(2):
                @pl.when((j == s) & (c == hf))
                def _():
                    for r in (1, 2, 3):
                        for p, b, n in _half_pieces(s, hf):
                            pltpu.make_async_remote_copy(
                                src_ref=pack_ref.at[pl.ds(p, n), :], dst_ref=big_ref.at[pl.ds(b, n), :],
                                send_sem=ssem.at[r - 1], recv_sem=rsem.at[r - 1],
                                device_id=_chip_peer(x, y, c, r), device_id_type=MESH).start()

        meta_copies = [pltpu.make_async_remote_copy(
            src_ref=meta_ref, dst_ref=metaf_ref.at[j], send_sem=msem_s.at[r - 1], recv_sem=msem_r.at[r - 1],
            device_id=_chip_peer(x, y, c, r), device_id_type=MESH) for r in (1, 2, 3)]
        for cp in meta_copies:
            cp.start()
        for r in (1, 2, 3):
            half_wait(ssem.at[r - 1], rsem.at[r - 1]).wait_recv()
            src_shard = jnp.bitwise_xor(j, r)
            for s in range(N_CHIPS):
                for hf in range(2):
                    @pl.when((src_shard == s) & (c == hf))
                    def _():
                        for p, b, n in _half_pieces(s, hf):
                            rows = big_ref.at[pl.ds(b, n), :]
                            pltpu.make_async_remote_copy(
                                src_ref=rows, dst_ref=rows, send_sem=fssem.at[r - 1], recv_sem=frsem.at[r - 1],
                                device_id=sibling, device_id_type=MESH).start()
        for r in (1, 2, 3):
            half_wait(fssem.at[r - 1], frsem.at[r - 1]).wait_recv()
        for r in (1, 2, 3):
            half_wait(ssem.at[r - 1], rsem.at[r - 1]).wait_send()
            half_wait(fssem.at[r - 1], frsem.at[r - 1]).wait_send()
        for cp in meta_copies:
            cp.wait_send()
            cp.wait_recv()
        zero.wait()
        own_meta.wait()
        pltpu.make_async_copy(pack_ref, big_ref.at[pl.ds(0, PK_ROWS), :], lsem.at[2]).wait()

    dma3 = pltpu.SemaphoreType.DMA((3,))
    return pl.pallas_call(
        body, name="allgather_weights", in_specs=[ANY, ANY, ANY], out_specs=[ANY, ANY],
        out_shape=[jax.ShapeDtypeStruct((BIG_ROWS, D), BF16), jax.ShapeDtypeStruct((N_CHIPS,) + meta_loc.shape, F32)],
        scratch_shapes=[dma3, dma3, dma3, dma3, dma3, dma3, dma3],
        compiler_params=pltpu.CompilerParams(has_side_effects=True),
    )(pack, zrows, meta_loc)


def _rs_core_exchange(gbig):
    def body(g_ref, own_ref, land_ref, lsem, ssem, rsem):
        x, y, c = _mesh_pos()
        me, sibling = (x, y, c), (x, y, 1 - c)
        for hf in range(2):
            @pl.when(c == hf)
            def _():
                for s in range(N_CHIPS):
                    for p, b, n in _half_pieces(s, hf):
                        pltpu.make_async_copy(g_ref.at[pl.ds(b, n), :], own_ref.at[s, pl.ds(p - hf * PK_HALF, n), :],
                                              lsem).start()
                    for p, b, n in _half_pieces(s, 1 - hf):
                        pltpu.make_async_remote_copy(
                            src_ref=g_ref.at[pl.ds(b, n), :], dst_ref=land_ref.at[s, pl.ds(p - (1 - hf) * PK_HALF, n), :],
                            send_sem=ssem, recv_sem=rsem, device_id=sibling, device_id_type=MESH).start()
        whole = pltpu.make_async_remote_copy(src_ref=land_ref, dst_ref=land_ref, send_sem=ssem, recv_sem=rsem,
                                             device_id=me, device_id_type=MESH)
        whole.wait_recv()
        whole.wait_send()
        pltpu.make_async_copy(own_ref, own_ref, lsem).wait()

    shape = jax.ShapeDtypeStruct((N_CHIPS, PK_HALF, D), BF16)
    return pl.pallas_call(
        body, name="rs_core_exchange", in_specs=[ANY], out_specs=[ANY, ANY], out_shape=[shape, shape],
        scratch_shapes=[pltpu.SemaphoreType.DMA(())] * 3,
        compiler_params=pltpu.CompilerParams(has_side_effects=True),
    )(gbig)


def _rs_chip_exchange(cp):
    def body(cp_ref, land_ref, ssem, rsem):
        x, y, c = _mesh_pos()
        j = 2 * x + y
        copies = [pltpu.make_async_remote_copy(
            src_ref=cp_ref.at[jnp.bitwise_xor(j, r)], dst_ref=land_ref.at[r - 1], send_sem=ssem.at[r - 1],
            recv_sem=rsem.at[r - 1], device_id=_chip_peer(x, y, c, r), device_id_type=MESH) for r in (1, 2, 3)]
        for cpy in copies:
            cpy.start()
        for cpy in copies:
            cpy.wait_recv()
        for cpy in copies:
            cpy.wait_send()

    dma3 = pltpu.SemaphoreType.DMA((3,))
    return pl.pallas_call(
        body, name="rs_chip_exchange", in_specs=[ANY], out_specs=ANY,
        out_shape=jax.ShapeDtypeStruct((3, PK_HALF, D), BF16), scratch_shapes=[dma3, dma3],
        compiler_params=pltpu.CompilerParams(has_side_effects=True),
    )(cp)


def _rs_finish(total_half, small):
    def body(t_ref, sm_ref, full_ref, all_ref, lsem, ssem, rsem, sm_s, sm_r):
        x, y, c = _mesh_pos()
        sibling = (x, y, 1 - c)
        my_id = 4 * x + 2 * y + c
        own = pltpu.make_async_copy(t_ref, full_ref.at[c], lsem.at[0])
        own.start()
        own_sm = pltpu.make_async_copy(sm_ref, all_ref.at[my_id], lsem.at[1])
        own_sm.start()
        half = pltpu.make_async_remote_copy(src_ref=t_ref, dst_ref=full_ref.at[c], send_sem=ssem, recv_sem=rsem,
                                            device_id=sibling, device_id_type=MESH)
        half.start()
        smalls = [pltpu.make_async_remote_copy(
            src_ref=sm_ref, dst_ref=all_ref.at[my_id], send_sem=sm_s.at[r - 1], recv_sem=sm_r.at[r - 1],
            device_id=(jnp.bitwise_xor(x, r >> 2), jnp.bitwise_xor(y, (r >> 1) & 1), jnp.bitwise_xor(c, r & 1)),
            device_id_type=MESH) for r in range(1, 8)]
        for cpy in smalls:
            cpy.start()
        half.wait_recv()
        half.wait_send()
        for cpy in smalls:
            cpy.wait_recv()
        for cpy in smalls:
            cpy.wait_send()
        own.wait()
        own_sm.wait()

    dma7 = pltpu.SemaphoreType.DMA((7,))
    return pl.pallas_call(
        body, name="rs_finish", in_specs=[ANY, ANY], out_specs=[ANY, ANY],
        out_shape=[jax.ShapeDtypeStruct((2, PK_HALF, D), F32), jax.ShapeDtypeStruct((8, SM_ROWS, 128), F32)],
        scratch_shapes=[pltpu.SemaphoreType.DMA((2,)), pltpu.SemaphoreType.DMA(()), pltpu.SemaphoreType.DMA(()), dma7, dma7],
        compiler_params=pltpu.CompilerParams(has_side_effects=True),
    )(total_half, small)


def _rope_tables(s):
    pos = jnp.arange(s, dtype=F32) - PAD_FRONT

    def cs(d):
        inv = ROPE_BASE ** (-jnp.arange(0, d, 2, dtype=F32) / d)
        ang = pos[:, None] * inv[None, :]
        return jnp.cos(ang), jnp.sin(ang)

    c, sn = cs(ROPE_D)
    z = jnp.zeros_like(c)
    c64 = jnp.concatenate([c, c, z, z], axis=1)
    sa64 = jnp.concatenate([-sn, z, z, z], axis=1)
    sb64 = jnp.concatenate([z, sn, z, z], axis=1)
    c, sn = cs(DH)
    return c64, sa64, sb64, jnp.concatenate([c, c], axis=1), jnp.concatenate([-sn, sn], axis=1)


def _grad_mm(a, b, m, n, s, name):
    return _matmul(a, b, "tn", m, n, s, min(m, 512), min(n, 1024), s, BF16, name, n_outer=True)


def _local_step(x2, target2, hp0, big, small_w):
    norm_w, qn_w, kvn_w, gn_w, gn_b, fin_w = small_w
    s = x2.shape[0] + BLK
    tm = _row_tile(s, 1408)
    uq = big[BIG_UQ:BIG_UQ + 384].reshape(HEADS, DH + ROPE_D, Q_RANK)
    wqn = uq[:, :DH].reshape(WIDTH, Q_RANK)
    wqp = jnp.pad(uq[:, DH:], ((0, 0), (0, DH - ROPE_D), (0, 0))).reshape(WIDTH, Q_RANK)
    wukv = big[BIG_UKV:BIG_UKV + 256].reshape(2 * WIDTH, KV_RANK)
    wbm = big[BIG_BM:BIG_BM + 1024].reshape(D, WIDTH)
    wbr = big[BIG_BR:BIG_BR + 1024].reshape(D, WIDTH)
    c64, sa64, sb64, c128, s128 = _rope_tables(s)
    consts, gamma = _ret_consts()

    h, xn, r1 = _rms_in(x2, hp0, norm_w, s)
    proj = _matmul(xn, big, "nt", s, PROJ_W, D, tm, 512, D, BF16, "proj", b_off=(BIG_IN // 512, 0))
    cqn, rq, ckvn, rkv, qn, qp, kv, kp = _mla_prep(proj, qn_w, kvn_w, wqn, wqp, wukv, c64, sa64, sb64, s)
    o_mla, y_mla, lse = _attn_fwd(qn, qp, kv, kp, proj, s)
    y_ret, on, rstd, qr, kr, rall = _ret_fwd(proj, gn_w, gn_b, c128, s128, consts, gamma, s)
    u_mla, u_ret, merged = _merge_fwd(y_mla, y_ret, wbm, wbr, proj, s)
    h2 = _out_fwd(merged, big, h, s)
    dh2, dh2b, loss_blk, dfin = _loss_bwd(h2, target2, fin_w, s)

    dg0, dg1, du_mla, du_ret = _merge_bwd(dh2b, big, u_mla, u_ret, proj, s)
    g_out = _grad_mm(merged, dh2b, D, D, s, "grad_w_out")
    do_mla, dz_mla = _branch_mla_bwd(du_mla, wbm, o_mla, proj, s)
    g_bm = _grad_mm(du_mla, y_mla, D, WIDTH, s, "grad_w_branch_mla")
    do_ret, dz_ret, dgw, dgb = _branch_ret_bwd(du_ret, wbr, on, rstd, gn_w, gn_b, proj, s)
    g_br = _grad_mm(du_ret, y_ret, D, WIDTH, s, "grad_w_branch_ret")
    dr_q, dr_k, dr_v = _ret_bwd(qr, kr, proj, rall, do_ret, c128, s128, consts, gamma, s)
    dqn, dqp = _attn_bwd_dq(qn, qp, kv, kp, do_mla, o_mla, lse, s)
    dkv, dkp_h = _attn_bwd_dkv(qn, qp, kv, kp, do_mla, o_mla, lse, s)
    dcq, dckv, dkpe, dqpr, dqnw, dkvnw = _mla_prep_bwd(dqn, dqp, dkv, dkp_h, proj, rq, rkv, qn_w, kvn_w,
                                                       wqn, wqp, wukv, c64, sa64, sb64, s)
    g_qn = _grad_mm(dqn, cqn, WIDTH, Q_RANK, s, "grad_w_uq_nope")
    g_qp = _grad_mm(dqpr, cqn, WIDTH, Q_RANK, s, "grad_w_uq_rope")
    g_ukv = _grad_mm(dkv, ckvn, 2 * WIDTH, KV_RANK, s, "grad_w_ukv")
    dproj = jnp.concatenate([dz_mla, dr_q, dr_k, dr_v, dz_ret, dg0, dg1, dcq, dckv, dkpe], axis=1)
    dxn = _matmul(dproj, big, "nn", s, D, PROJ_W, tm, 1024, 1024, F32, "dxn", b_off=(BIG_IN // 1024, 0))
    grad_x, gmeta_blk, dnorm = _rms_in_bwd(dxn, h, r1, dh2, norm_w, s)
    g_in = _grad_mm(dproj, xn, PROJ_W, D, s, "grad_w_in")

    g_uq = jnp.concatenate([g_qn.reshape(HEADS, DH, Q_RANK), g_qp.reshape(HEADS, DH, Q_RANK)[:, :ROPE_D]], axis=1)
    gbig = jnp.concatenate([
        g_out, g_bm.reshape(1024, D), g_br.reshape(1024, D), g_uq.reshape(384, D), g_ukv.reshape(256, D),
        jnp.zeros((BIG_IN - BIG_JUNK, D), BF16), g_in], axis=0)
    rows = lambda a: a.reshape(-1, 128)
    small = jnp.concatenate([loss_blk[0:1], rows(dnorm), rows(dqnw), rows(dkvnw), rows(dgw), rows(dgb), rows(dfin),
                             rows(gmeta_blk[PAD_FRONT:]), jnp.zeros((1, 128), F32)], axis=0)
    return grad_x, gbig, small


def _small_rows(ws):
    rows = lambda a: a.reshape(-1, 128)
    return jnp.concatenate([jnp.zeros((1, 128), F32)] + [rows(w) for w in ws]
                           + [jnp.zeros((SM_ROWS - SM_META, 128), F32)], axis=0)


def kernel(x, meta, norm_w, w_in, mla_q_norm_w, mla_w_uq, mla_kv_norm_w, mla_w_ukv, ret_gn_w, ret_gn_b, w_branch_mla, w_branch_ret, w_out, final_norm_w, loss_target, m_meta, m_norm_w, m_w_in, m_mla_q_norm_w, m_mla_w_uq, m_mla_kv_norm_w, m_mla_w_ukv, m_ret_gn_w, m_ret_gn_b, m_w_branch_mla, m_w_branch_ret, m_w_out, m_final_norm_w, v_meta, v_norm_w, v_w_in, v_mla_q_norm_w, v_mla_w_uq, v_mla_kv_norm_w, v_mla_w_ukv, v_ret_gn_w, v_ret_gn_b, v_w_branch_mla, v_w_branch_ret, v_w_out, v_final_norm_w):
    j = 2 * lax.axis_index("x") + lax.axis_index("y")
    tr = lambda w: w[0].T.reshape(-1, D).astype(BF16)
    pack = jnp.concatenate([w_out[0].astype(BF16), tr(w_branch_mla), tr(w_branch_ret), tr(mla_w_uq), tr(mla_w_ukv),
                            tr(w_in), jnp.zeros((PK_ROWS - PK_PAD, D), BF16)], axis=0)
    big, meta_all = _allgather(pack, jnp.zeros((ZERO_ROWS, D), BF16), meta)
    meta_full = meta_all.transpose(1, 0, 2).reshape(N_META, D)
    hp0 = jnp.concatenate([jnp.zeros((PAD_FRONT, D), F32), meta_full], axis=0)
    small_w = (norm_w, mla_q_norm_w, mla_kv_norm_w, ret_gn_w, ret_gn_b, final_norm_w.reshape(1, D))

    grad_x, gbig, small = _local_step(x[0], loss_target[0], hp0, big, small_w)

    own, land = _rs_core_exchange(gbig)
    chip_part = _add2(own.reshape(-1, D), land.reshape(-1, D), "rs_core_add").reshape(N_CHIPS, PK_HALF, D)
    land2 = _rs_chip_exchange(chip_part)
    total_half = _add4(lax.dynamic_index_in_dim(chip_part, j, 0, keepdims=False), land2, "rs_chip_add")
    full, small_all = _rs_finish(total_half, small)
    full = full.reshape(PK_ROWS, D)

    untr = lambda lo, hi, rows: full[lo:hi].reshape(rows, -1).T
    grads = {
        "w_out": full[PK_OUT:PK_BM], "w_branch_mla": untr(PK_BM, PK_BR, 512), "w_branch_ret": untr(PK_BR, PK_UQ, 512),
        "mla_w_uq": untr(PK_UQ, PK_UKV, 384), "mla_w_ukv": untr(PK_UKV, PK_IN, 512), "w_in": untr(PK_IN, PK_PAD, IN_SHARD),
    }
    big_w = {"w_in": (w_in, m_w_in, v_w_in), "mla_w_uq": (mla_w_uq, m_mla_w_uq, v_mla_w_uq),
             "mla_w_ukv": (mla_w_ukv, m_mla_w_ukv, v_mla_w_ukv),
             "w_branch_mla": (w_branch_mla, m_w_branch_mla, v_w_branch_mla),
             "w_branch_ret": (w_branch_ret, m_w_branch_ret, v_w_branch_ret), "w_out": (w_out, m_w_out, v_w_out)}
    res = {}
    for name, (w, m, v) in big_w.items():
        d, nm, nv = _adamw(w[0], grads[name], m[0], v[0], "adamw_" + name)
        res[name] = (grads[name][None], d[None], nm[None], nv[None])

    small_m = (m_norm_w, m_mla_q_norm_w, m_mla_kv_norm_w, m_ret_gn_w, m_ret_gn_b, m_final_norm_w.reshape(1, D))
    small_v = (v_norm_w, v_mla_q_norm_w, v_mla_kv_norm_w, v_ret_gn_w, v_ret_gn_b, v_final_norm_w.reshape(1, D))
    gs, ds, ms, vs = _adamw_small(_small_rows(small_w), small_all, _small_rows(small_m), _small_rows(small_v))
    names = ["norm_w", "mla_q_norm_w", "mla_kv_norm_w", "ret_gn_w", "ret_gn_b", "final_norm_w"]
    bounds = [SM_NORM, SM_QN, SM_KVN, SM_GNW, SM_GNB, SM_FIN, SM_META]
    for k, name in enumerate(names):
        shape = (D,) if name == "final_norm_w" else (1, -1)
        res[name] = tuple(a[bounds[k]:bounds[k + 1]].reshape(shape) for a in (gs, ds, ms, vs))
    g_meta = lax.dynamic_slice_in_dim(gs[SM_META:SM_META + 256].reshape(N_META, D), j * (D // N_CHIPS), D // N_CHIPS, axis=1)
    res["meta"] = (g_meta,) + tuple(_adamw(meta, g_meta, m_meta, v_meta, "adamw_meta"))

    order = ["meta", "norm_w", "w_in", "mla_q_norm_w", "mla_w_uq", "mla_kv_norm_w", "mla_w_ukv", "ret_gn_w", "ret_gn_b",
             "w_branch_mla", "w_branch_ret", "w_out", "final_norm_w"]
    return (gs[0, 0], grad_x[None]) + tuple(res[n][k] for k in range(4) for n in order)
```

```python
import functools
import math

import numpy as np
import jax
import jax.numpy as jnp
from jax import lax
from jax.experimental import pallas as pl
from jax.experimental.pallas import tpu as pltpu

F32 = jnp.float32
BF16 = jnp.bfloat16
MESH = pl.DeviceIdType.MESH

D = 2048
N_META = 16
BLK = 128
PAD_FRONT = BLK - N_META
HEADS = 8
DH = 128
ROPE_D = 64
Q_RANK = 512
KV_RANK = 256
WIDTH = HEADS * DH
ROPE_BASE = 10000.0
NORM_EPS = 1e-6
GN_EPS = 1e-5
NEG_INF = -1e30
ATT_SCALE = (DH + ROPE_D) ** -0.5
RET_SCALE = DH ** -0.5
IN_WIDTH = 10048
N_CHIPS = 4
IN_SHARD = IN_WIDTH // N_CHIPS
ADAM_LR, ADAM_B1, ADAM_B2, ADAM_EPS, ADAM_WD, ADAM_STEP = 0.001, 0.9, 0.999, 1e-08, 0.01, 10

ROT = Q_RANK + KV_RANK + ROPE_D
Z_MLA, R_Q, R_K, R_V, Z_RET, GATE0, GATE1 = 0, 1024, 2048, 3072, 4096, 5120, 5120 + D
C_Q = IN_WIDTH - ROT
C_KV = C_Q + Q_RANK
K_PE = C_KV + KV_RANK
PROJ_W = 10240

BIG_OUT, BIG_BM, BIG_BR, BIG_UQ, BIG_UKV, BIG_JUNK, BIG_IN = 0, 2048, 3072, 4096, 4480, 4736, 5120
BIG_ROWS = BIG_IN + PROJ_W
ZERO_ROWS = PROJ_W - IN_WIDTH
PK_OUT, PK_BM, PK_BR, PK_UQ, PK_UKV, PK_IN, PK_PAD, PK_ROWS = 0, 512, 768, 1024, 1120, 1184, 3696, 3712
PK_HALF = PK_ROWS // 2
SM_LOSS, SM_NORM, SM_QN, SM_KVN, SM_GNW, SM_GNB, SM_FIN, SM_META, SM_ROWS = 0, 1, 17, 21, 23, 31, 39, 55, 312

VMEM_LIMIT = 56 * 1024 * 1024


def _pieces(shard):
    j = shard
    out = [(PK_OUT, BIG_OUT + 512 * j, 512), (PK_BM, BIG_BM + 256 * j, 256), (PK_BR, BIG_BR + 256 * j, 256),
           (PK_UQ, BIG_UQ + 96 * j, 96), (PK_UKV, BIG_UKV + 64 * j, 64)]
    if j == 0:
        out += [(PK_IN, BIG_IN + C_Q, ROT), (PK_IN + ROT, BIG_IN, IN_SHARD - ROT)]
    else:
        out += [(PK_IN, BIG_IN + IN_SHARD * j - ROT, IN_SHARD)]
    out += [(PK_PAD, BIG_JUNK + 16 * j, 16)]
    return out


def _half_pieces(shard, half):
    lo, hi = half * PK_HALF, (half + 1) * PK_HALF
    out = []
    for p, b, n in _pieces(shard):
        s, e = max(p, lo), min(p + n, hi)
        if e > s:
            out.append((s, b + (s - p), e - s))
    return out


def _row_tile(rows, cap):
    best = BLK
    for t in range(BLK, cap + 1, BLK):
        if rows % t == 0:
            best = t
    return best


def _cparams(sem):
    return pltpu.CompilerParams(dimension_semantics=sem, vmem_limit_bytes=VMEM_LIMIT)


def _dot(a, b, form):
    dn = {"nt": (((1,), (1,)), ((), ())), "nn": (((1,), (0,)), ((), ())), "tn": (((0,), (0,)), ((), ()))}[form]
    return lax.dot_general(a, b, dn, preferred_element_type=F32)


def _sigmoid(v):
    return 1.0 / (1.0 + jnp.exp(-v))


def _matmul(a, b, form, m, n, k, tm, tn, tk, out_dtype, name, a_off=(0, 0), b_off=(0, 0), n_outer=False):
    nk = k // tk
    gi, gj = m // tm, n // tn

    def ij(g0, g1):
        return (g1, g0) if n_outer else (g0, g1)

    if form == "nt":
        a_spec = pl.BlockSpec((tm, tk), lambda g0, g1, kk: (ij(g0, g1)[0] + a_off[0], kk + a_off[1]))
        b_spec = pl.BlockSpec((tn, tk), lambda g0, g1, kk: (ij(g0, g1)[1] + b_off[0], kk + b_off[1]))
    elif form == "nn":
        a_spec = pl.BlockSpec((tm, tk), lambda g0, g1, kk: (ij(g0, g1)[0] + a_off[0], kk + a_off[1]))
        b_spec = pl.BlockSpec((tk, tn), lambda g0, g1, kk: (kk + b_off[0], ij(g0, g1)[1] + b_off[1]))
    else:
        a_spec = pl.BlockSpec((tk, tm), lambda g0, g1, kk: (kk + a_off[0], ij(g0, g1)[0] + a_off[1]))
        b_spec = pl.BlockSpec((tk, tn), lambda g0, g1, kk: (kk + b_off[0], ij(g0, g1)[1] + b_off[1]))
    o_spec = pl.BlockSpec((tm, tn), lambda g0, g1, kk: ij(g0, g1))

    def body(a_ref, b_ref, o_ref, *acc):
        p = _dot(a_ref[...], b_ref[...], form)
        if nk == 1:
            o_ref[...] = p.astype(o_ref.dtype)
        else:
            acc_ref, = acc
            kk = pl.program_id(2)

            @pl.when(kk == 0)
            def _():
                acc_ref[...] = p

            @pl.when(kk > 0)
            def _():
                acc_ref[...] += p

            @pl.when(kk == nk - 1)
            def _():
                o_ref[...] = acc_ref[...].astype(o_ref.dtype)

    return pl.pallas_call(
        body, name=name, grid=(gj, gi, nk) if n_outer else (gi, gj, nk), in_specs=[a_spec, b_spec], out_specs=o_spec,
        out_shape=jax.ShapeDtypeStruct((m, n), out_dtype),
        scratch_shapes=[] if nk == 1 else [pltpu.VMEM((tm, tn), F32)],
        compiler_params=_cparams(("parallel", "parallel", "arbitrary")),
    )(a, b)


def _rms_in(x, hp0, norm_w, s):
    def body(x_ref, hp0_ref, w_ref, h_ref, xn_ref, r_ref):
        def run(hv):
            r = lax.rsqrt(jnp.mean(hv * hv, axis=-1, keepdims=True) + NORM_EPS)
            h_ref[...] = hv
            xn_ref[...] = (hv * r * w_ref[...]).astype(BF16)
            r_ref[...] = r

        @pl.when(pl.program_id(0) == 0)
        def _():
            run(hp0_ref[...])

        @pl.when(pl.program_id(0) > 0)
        def _():
            run(x_ref[...])

    return pl.pallas_call(
        body, name="rms_in", grid=(s // BLK,),
        in_specs=[pl.BlockSpec((BLK, D), lambda i: (jnp.maximum(i - 1, 0), 0)),
                  pl.BlockSpec((BLK, D), lambda i: (0, 0)), pl.BlockSpec((1, D), lambda i: (0, 0))],
        out_specs=[pl.BlockSpec((BLK, D), lambda i: (i, 0)), pl.BlockSpec((BLK, D), lambda i: (i, 0)),
                   pl.BlockSpec((BLK, 1), lambda i: (i, 0))],
        out_shape=[jax.ShapeDtypeStruct((s, D), F32), jax.ShapeDtypeStruct((s, D), BF16),
                   jax.ShapeDtypeStruct((s, 1), F32)],
        compiler_params=_cparams(("arbitrary",)),
    )(x, hp0, norm_w)


def _rope64(t, c, sa, sb):
    return t * c + pltpu.roll(t, DH - ROPE_D // 2, 1) * sa + pltpu.roll(t, ROPE_D // 2, 1) * sb


def _rope128(t, c, sg):
    return t * c + pltpu.roll(t, DH // 2, 1) * sg


def _mla_prep(proj, qn_w, kvn_w, wqn, wqp, wukv, c64, sa64, sb64, s):
    tm = _row_tile(s, 384)

    def body(cq_ref, ckv_ref, kpe_ref, qnw_ref, kvnw_ref, wqn_ref, wqp_ref, wukv_ref, c_ref, sa_ref, sb_ref,
             cqn_ref, rq_ref, ckvn_ref, rkv_ref, qn_ref, qp_ref, kv_ref, kp_ref):
        cq = cq_ref[...].astype(F32)
        rq = lax.rsqrt(jnp.mean(cq * cq, axis=-1, keepdims=True) + NORM_EPS)
        cqn = (cq * rq * qnw_ref[...]).astype(BF16)
        ckv = ckv_ref[...].astype(F32)
        rkv = lax.rsqrt(jnp.mean(ckv * ckv, axis=-1, keepdims=True) + NORM_EPS)
        ckvn = (ckv * rkv * kvnw_ref[...]).astype(BF16)
        cqn_ref[...] = cqn
        rq_ref[...] = rq
        ckvn_ref[...] = ckvn
        rkv_ref[...] = rkv
        qn_ref[...] = _dot(cqn, wqn_ref[...], "nt").astype(BF16)
        kv_ref[...] = _dot(ckvn, wukv_ref[...], "nt").astype(BF16)
        c, sa, sb = c_ref[...], sa_ref[...], sb_ref[...]
        qp = _dot(cqn, wqp_ref[...], "nt")
        for h in range(HEADS):
            sl = slice(h * DH, (h + 1) * DH)
            qp_ref[:, sl] = _rope64(qp[:, sl], c, sa, sb).astype(BF16)
        kp_ref[...] = _rope64(kpe_ref[...].astype(F32), c, sa, sb).astype(BF16)

    row = lambda w, cb: pl.BlockSpec((tm, w), lambda i: (i, cb))
    full = lambda a: pl.BlockSpec(a.shape, lambda i: (0, 0))
    return pl.pallas_call(
        body, name="mla_prep", grid=(s // tm,),
        in_specs=[row(Q_RANK, C_Q // Q_RANK), row(KV_RANK, C_KV // KV_RANK), row(DH, K_PE // DH),
                  full(qn_w), full(kvn_w), full(wqn), full(wqp), full(wukv), row(DH, 0), row(DH, 0), row(DH, 0)],
        out_specs=[row(Q_RANK, 0), row(1, 0), row(KV_RANK, 0), row(1, 0), row(WIDTH, 0), row(WIDTH, 0),
                   row(2 * WIDTH, 0), row(DH, 0)],
        out_shape=[jax.ShapeDtypeStruct((s, Q_RANK), BF16), jax.ShapeDtypeStruct((s, 1), F32),
                   jax.ShapeDtypeStruct((s, KV_RANK), BF16), jax.ShapeDtypeStruct((s, 1), F32),
                   jax.ShapeDtypeStruct((s, WIDTH), BF16), jax.ShapeDtypeStruct((s, WIDTH), BF16),
                   jax.ShapeDtypeStruct((s, 2 * WIDTH), BF16), jax.ShapeDtypeStruct((s, DH), BF16)],
        compiler_params=_cparams(("parallel",)),
    )(proj, proj, proj, qn_w, kvn_w, wqn, wqp, wukv, c64, sa64, sb64)


def _att_mask(qi, ki, t):
    row = qi * t + lax.broadcasted_iota(jnp.int32, (t, t), 0)
    col = ki * t + lax.broadcasted_iota(jnp.int32, (t, t), 1)
    return (col <= row) & (col >= PAD_FRONT)


def _silu(z):
    return z * _sigmoid(z)


def _attn_fwd(qn, qp, kv, kp, proj, s):
    t = _row_tile(s, 384)

    def body(qn_ref, qp_ref, kn_ref, v_ref, kp_ref, z_ref, o_ref, y_ref, lse_ref):
        qi = pl.program_id(1)
        q_n, q_p = qn_ref[...], qp_ref[...]

        def step(ki, carry):
            m, l, acc = carry
            ks = pl.ds(pl.multiple_of(ki * t, t), t)
            sc = (_dot(q_n, kn_ref[ks, :], "nt") + _dot(q_p, kp_ref[ks, :], "nt")) * ATT_SCALE
            sc = jnp.where(_att_mask(qi, ki, t), sc, NEG_INF)
            m_new = jnp.maximum(m, jnp.max(sc, axis=-1, keepdims=True))
            alpha = jnp.exp(m - m_new)
            p = jnp.exp(sc - m_new)
            l = alpha * l + jnp.sum(p, axis=-1, keepdims=True)
            acc = alpha * acc + _dot(p.astype(BF16), v_ref[ks, :], "nn")
            return m_new, l, acc

        init = (jnp.full((t, 1), NEG_INF, F32), jnp.zeros((t, 1), F32), jnp.zeros((t, DH), F32))
        m, l, acc = lax.fori_loop(0, qi + 1, step, init)
        o = acc / l
        o_ref[...] = o.astype(BF16)
        y_ref[...] = (o * _silu(z_ref[...].astype(F32))).astype(BF16)
        lse_ref[0] = m + jnp.log(l)

    qtile = pl.BlockSpec((t, DH), lambda h, i: (i, h))
    return pl.pallas_call(
        body, name="attn_fwd", grid=(HEADS, s // t),
        in_specs=[qtile, qtile, pl.BlockSpec((s, DH), lambda h, i: (0, 2 * h)),
                  pl.BlockSpec((s, DH), lambda h, i: (0, 2 * h + 1)), pl.BlockSpec((s, DH), lambda h, i: (0, 0)),
                  pl.BlockSpec((t, DH), lambda h, i: (i, Z_MLA // DH + h))],
        out_specs=[qtile, qtile, pl.BlockSpec((1, t, 1), lambda h, i: (h, i, 0))],
        out_shape=[jax.ShapeDtypeStruct((s, WIDTH), BF16), jax.ShapeDtypeStruct((s, WIDTH), BF16),
                   jax.ShapeDtypeStruct((HEADS, s, 1), F32)],
        compiler_params=_cparams(("parallel", "arbitrary")),
    )(qn, qp, kv, kv, kp, proj)


def _ret_consts():
    log_g = np.log1p(-(2.0 ** (-5.0 - np.arange(HEADS, dtype=np.float64))))
    n = np.arange(BLK, dtype=np.float64)
    diff = n[:, None] - n[None, :]
    decay = np.where(diff >= 0, np.exp(log_g[:, None, None] * np.maximum(diff, 0.0)), 0.0)
    zeta = np.exp(log_g[:, None] * (BLK - 1.0 - n))[:, :, None]
    xi = np.exp(log_g[:, None] * (n + 1.0))[:, :, None]
    gamma = [float(np.float32(np.exp(g * BLK))) for g in log_g]
    return (jnp.asarray(decay, F32), jnp.asarray(zeta, F32), jnp.asarray(xi, F32)), gamma


def _ret_fwd(proj, gn_w, gn_b, c128, s128, consts, gamma, s):
    nb = s // BLK
    decay, zeta, xi = consts

    def body(rq_ref, rk_ref, rv_ref, z_ref, gw_ref, gb_ref, c_ref, s_ref, dm_ref, zt_ref, xi_ref,
             y_ref, on_ref, rstd_ref, qr_ref, kr_ref, rall_ref, state):
        @pl.when(pl.program_id(0) == 0)
        def _():
            state[...] = jnp.zeros_like(state)

        c, sg = c_ref[...], s_ref[...]
        for h in range(HEADS):
            sl = slice(h * DH, (h + 1) * DH)
            q = _rope128(rq_ref[:, sl].astype(F32), c, sg).astype(BF16)
            kf = _rope128(rk_ref[:, sl].astype(F32), c, sg) * RET_SCALE
            k = kf.astype(BF16)
            v = rv_ref[:, sl]
            qr_ref[:, sl] = q
            kr_ref[:, sl] = k
            r_prev = state[h]
            rall_ref[0, h] = r_prev
            a = _dot(q, k, "nt") * dm_ref[h]
            o = _dot(a.astype(BF16), v, "nn") + _dot(q, r_prev.astype(BF16), "nn") * xi_ref[h]
            state[h] = r_prev * gamma[h] + _dot((kf * zt_ref[h]).astype(BF16), v, "tn")
            mu = jnp.mean(o, axis=-1, keepdims=True)
            var = jnp.mean(jnp.square(o - mu), axis=-1, keepdims=True)
            rstd = lax.rsqrt(var + GN_EPS)
            on = (o - mu) * rstd
            rstd_ref[h] = rstd
            on_ref[:, sl] = on.astype(BF16)
            ogn = on * gw_ref[:, sl] + gb_ref[:, sl]
            y_ref[:, sl] = (ogn * _silu(z_ref[:, sl].astype(F32))).astype(BF16)

    seg = lambda cb: pl.BlockSpec((BLK, WIDTH), lambda i: (i, cb))
    full = lambda a: pl.BlockSpec(a.shape, lambda i: (0,) * a.ndim)
    tab = pl.BlockSpec((BLK, DH), lambda i: (i, 0))
    return pl.pallas_call(
        body, name="ret_fwd", grid=(nb,),
        in_specs=[seg(R_Q // WIDTH), seg(R_K // WIDTH), seg(R_V // WIDTH), seg(Z_RET // WIDTH), full(gn_w), full(gn_b),
                  tab, tab, full(decay), full(zeta), full(xi)],
        out_specs=[seg(0), seg(0), pl.BlockSpec((HEADS, BLK, 1), lambda i: (0, i, 0)), seg(0), seg(0),
                   pl.BlockSpec((1, HEADS, DH, DH), lambda i: (i, 0, 0, 0))],
        out_shape=[jax.ShapeDtypeStruct((s, WIDTH), BF16), jax.ShapeDtypeStruct((s, WIDTH), BF16),
                   jax.ShapeDtypeStruct((HEADS, s, 1), F32), jax.ShapeDtypeStruct((s, WIDTH), BF16),
                   jax.ShapeDtypeStruct((s, WIDTH), BF16), jax.ShapeDtypeStruct((nb, HEADS, DH, DH), F32)],
        scratch_shapes=[pltpu.VMEM((HEADS, DH, DH), F32)],
        compiler_params=_cparams(("arbitrary",)),
    )(proj, proj, proj, proj, gn_w, gn_b, c128, s128, decay, zeta, xi)


def _merge_fwd(y_mla, y_ret, wbm, wbr, proj, s):
    tm, tn = _row_tile(s, 1408), 512

    def body(ym_ref, yr_ref, wm_ref, wr_ref, g0_ref, g1_ref, um_ref, ur_ref, mg_ref):
        um = _dot(ym_ref[...], wm_ref[...], "nt")
        ur = _dot(yr_ref[...], wr_ref[...], "nt")
        um_ref[...] = um.astype(BF16)
        ur_ref[...] = ur.astype(BF16)
        mg_ref[...] = (_sigmoid(g0_ref[...].astype(F32)) * um + _sigmoid(g1_ref[...].astype(F32)) * ur).astype(BF16)

    yspec = pl.BlockSpec((tm, WIDTH), lambda i, j: (i, 0))
    wspec = pl.BlockSpec((tn, WIDTH), lambda i, j: (j, 0))
    ospec = pl.BlockSpec((tm, tn), lambda i, j: (i, j))
    return pl.pallas_call(
        body, name="merge_fwd", grid=(s // tm, D // tn),
        in_specs=[yspec, yspec, wspec, wspec, pl.BlockSpec((tm, tn), lambda i, j: (i, GATE0 // tn + j)),
                  pl.BlockSpec((tm, tn), lambda i, j: (i, GATE1 // tn + j))],
        out_specs=[ospec, ospec, ospec],
        out_shape=[jax.ShapeDtypeStruct((s, D), BF16)] * 3,
        compiler_params=_cparams(("parallel", "parallel")),
    )(y_mla, y_ret, wbm, wbr, proj, proj)


def _out_fwd(merged, big, h, s):
    tm, tn = _row_tile(s, 1408), 512

    def body(m_ref, w_ref, h_ref, o_ref):
        o_ref[...] = h_ref[...] + _dot(m_ref[...], w_ref[...], "nn")

    return pl.pallas_call(
        body, name="out_fwd", grid=(s // tm, D // tn),
        in_specs=[pl.BlockSpec((tm, D), lambda i, j: (i, 0)), pl.BlockSpec((D, tn), lambda i, j: (BIG_OUT // D, j)),
                  pl.BlockSpec((tm, tn), lambda i, j: (i, j))],
        out_specs=pl.BlockSpec((tm, tn), lambda i, j: (i, j)),
        out_shape=jax.ShapeDtypeStruct((s, D), F32),
        compiler_params=_cparams(("parallel", "parallel")),
    )(merged, big, h)


def _loss_bwd(h2, target, fin_w, s):
    nb = s // BLK

    def body(h2_ref, t_ref, w_ref, dh_ref, dhb_ref, loss_ref, dw_ref):
        i = pl.program_id(0)

        @pl.when(i == 0)
        def _():
            dh_ref[...] = jnp.zeros_like(dh_ref)
            dhb_ref[...] = jnp.zeros_like(dhb_ref)
            loss_ref[...] = jnp.zeros_like(loss_ref)
            dw_ref[...] = jnp.zeros_like(dw_ref)

        @pl.when(i > 0)
        def _():
            hv = h2_ref[...]
            w = w_ref[...]
            r = lax.rsqrt(jnp.mean(hv * hv, axis=-1, keepdims=True) + NORM_EPS)
            nrm = hv * r
            e = nrm * w - t_ref[...]
            loss_ref[...] += jnp.full(loss_ref.shape, 0.5 / D, F32) * jnp.sum(e * e)
            dy = e * (1.0 / D)
            dw_ref[...] += jnp.sum(dy * nrm, axis=0, keepdims=True)
            g = dy * w
            dh = r * (g - nrm * jnp.mean(g * nrm, axis=-1, keepdims=True))
            dh_ref[...] = dh
            dhb_ref[...] = dh.astype(BF16)

    blk = pl.BlockSpec((BLK, D), lambda i: (i, 0))
    return pl.pallas_call(
        body, name="loss_bwd", grid=(nb,),
        in_specs=[blk, pl.BlockSpec((BLK, D), lambda i: (jnp.maximum(i - 1, 0), 0)), pl.BlockSpec((1, D), lambda i: (0, 0))],
        out_specs=[blk, blk, pl.BlockSpec((8, 128), lambda i: (0, 0)), pl.BlockSpec((1, D), lambda i: (0, 0))],
        out_shape=[jax.ShapeDtypeStruct((s, D), F32), jax.ShapeDtypeStruct((s, D), BF16),
                   jax.ShapeDtypeStruct((8, 128), F32), jax.ShapeDtypeStruct((1, D), F32)],
        compiler_params=_cparams(("arbitrary",)),
    )(h2, target, fin_w)


def _merge_bwd(dh2b, big, u_mla, u_ret, proj, s):
    tm, tn = _row_tile(s, 1408), 512

    def body(d_ref, w_ref, um_ref, ur_ref, g0_ref, g1_ref, dg0_ref, dg1_ref, dum_ref, dur_ref):
        dm = _dot(d_ref[...], w_ref[...], "nt")
        g0 = _sigmoid(g0_ref[...].astype(F32))
        g1 = _sigmoid(g1_ref[...].astype(F32))
        dg0_ref[...] = (dm * um_ref[...].astype(F32) * g0 * (1.0 - g0)).astype(BF16)
        dg1_ref[...] = (dm * ur_ref[...].astype(F32) * g1 * (1.0 - g1)).astype(BF16)
        dum_ref[...] = (dm * g0).astype(BF16)
        dur_ref[...] = (dm * g1).astype(BF16)

    ospec = pl.BlockSpec((tm, tn), lambda i, j: (i, j))
    return pl.pallas_call(
        body, name="merge_bwd", grid=(s // tm, D // tn),
        in_specs=[pl.BlockSpec((tm, D), lambda i, j: (i, 0)), pl.BlockSpec((tn, D), lambda i, j: (BIG_OUT // tn + j, 0)),
                  ospec, ospec, pl.BlockSpec((tm, tn), lambda i, j: (i, GATE0 // tn + j)),
                  pl.BlockSpec((tm, tn), lambda i, j: (i, GATE1 // tn + j))],
        out_specs=[ospec] * 4,
        out_shape=[jax.ShapeDtypeStruct((s, D), BF16)] * 4,
        compiler_params=_cparams(("parallel", "parallel")),
    )(dh2b, big, u_mla, u_ret, proj, proj)


def _dsilu(z):
    sg = _sigmoid(z)
    return sg * (1.0 + z * (1.0 - sg))


def _branch_mla_bwd(du, wbm, o_mla, proj, s):
    tm = _row_tile(s, 384)

    def body(du_ref, w_ref, o_ref, z_ref, do_ref, dz_ref):
        dy = _dot(du_ref[...], w_ref[...], "nn")
        z = z_ref[...].astype(F32)
        do_ref[...] = (dy * _silu(z)).astype(BF16)
        dz_ref[...] = (dy * o_ref[...].astype(F32) * _dsilu(z)).astype(BF16)

    row = lambda w, cb: pl.BlockSpec((tm, w), lambda i: (i, cb))
    return pl.pallas_call(
        body, name="branch_mla_bwd", grid=(s // tm,),
        in_specs=[row(D, 0), pl.BlockSpec((D, WIDTH), lambda i: (0, 0)), row(WIDTH, 0), row(WIDTH, Z_MLA // WIDTH)],
        out_specs=[row(WIDTH, 0), row(WIDTH, 0)],
        out_shape=[jax.ShapeDtypeStruct((s, WIDTH), BF16)] * 2,
        compiler_params=_cparams(("parallel",)),
    )(du, wbm, o_mla, proj)


def _branch_ret_bwd(du, wbr, on, rstd, gn_w, gn_b, proj, s):
    tm = BLK

    def body(du_ref, w_ref, on_ref, rstd_ref, gw_ref, gb_ref, z_ref, do_ref, dz_ref, dgw_ref, dgb_ref):
        @pl.when(pl.program_id(0) == 0)
        def _():
            dgw_ref[...] = jnp.zeros_like(dgw_ref)
            dgb_ref[...] = jnp.zeros_like(dgb_ref)

        dy = _dot(du_ref[...], w_ref[...], "nn")
        z = z_ref[...].astype(F32)
        on = on_ref[...].astype(F32)
        gw = gw_ref[...]
        dogn = dy * _silu(z)
        dz_ref[...] = (dy * (on * gw + gb_ref[...]) * _dsilu(z)).astype(BF16)
        dgw_ref[...] += jnp.sum(dogn * on, axis=0, keepdims=True)
        dgb_ref[...] += jnp.sum(dogn, axis=0, keepdims=True)
        don = dogn * gw
        for h in range(HEADS):
            sl = slice(h * DH, (h + 1) * DH)
            dn, nh = don[:, sl], on[:, sl]
            do = rstd_ref[h] * (dn - jnp.mean(dn, axis=-1, keepdims=True)
                                - nh * jnp.mean(dn * nh, axis=-1, keepdims=True))
            do_ref[:, sl] = do.astype(BF16)

    row = lambda w, cb: pl.BlockSpec((tm, w), lambda i: (i, cb))
    vec = pl.BlockSpec((1, WIDTH), lambda i: (0, 0))
    return pl.pallas_call(
        body, name="branch_ret_bwd", grid=(s // tm,),
        in_specs=[row(D, 0), pl.BlockSpec((D, WIDTH), lambda i: (0, 0)), row(WIDTH, 0),
                  pl.BlockSpec((HEADS, tm, 1), lambda i: (0, i, 0)), vec, vec, row(WIDTH, Z_RET // WIDTH)],
        out_specs=[row(WIDTH, 0), row(WIDTH, 0), vec, vec],
        out_shape=[jax.ShapeDtypeStruct((s, WIDTH), BF16)] * 2 + [jax.ShapeDtypeStruct((1, WIDTH), F32)] * 2,
        compiler_params=_cparams(("arbitrary",)),
    )(du, wbr, on, rstd, gn_w, gn_b, proj)


def _ret_bwd(qr, kr, proj, rall, do_ret, c128, s128, consts, gamma, s):
    nb = s // BLK
    decay, zeta, xi = consts

    def body(q_ref, k_ref, v_ref, r_ref, do_ref, c_ref, s_ref, dm_ref, zt_ref, xi_ref, dq_ref, dk_ref, dv_ref, gstate):
        @pl.when(pl.program_id(0) == 0)
        def _():
            gstate[...] = jnp.zeros_like(gstate)

        c, sg = c_ref[...], s_ref[...]
        for h in range(HEADS):
            sl = slice(h * DH, (h + 1) * DH)
            q, k, v, do = q_ref[:, sl], k_ref[:, sl], v_ref[:, sl], do_ref[:, sl]
            dm = dm_ref[h]
            g_next = gstate[h]
            gb = g_next.astype(BF16)
            a = (_dot(q, k, "nt") * dm).astype(BF16)
            da = (_dot(do, v, "nt") * dm).astype(BF16)
            dox = (do.astype(F32) * xi_ref[h]).astype(BF16)
            dq = _dot(da, k, "nn") + _dot(dox, r_ref[0, h].astype(BF16), "nt")
            dk = _dot(da, q, "tn") + _dot(v, gb, "nt") * zt_ref[h]
            kz = (k.astype(F32) * zt_ref[h]).astype(BF16)
            dv = _dot(a, do, "tn") + _dot(kz, gb, "nn")
            gstate[h] = g_next * gamma[h] + _dot(q, dox, "tn")
            dk = dk * RET_SCALE
            dq_ref[:, sl] = _rope128(dq, c, -sg).astype(BF16)
            dk_ref[:, sl] = _rope128(dk, c, -sg).astype(BF16)
            dv_ref[:, sl] = dv.astype(BF16)

    rev = lambda cb: pl.BlockSpec((BLK, WIDTH), lambda i: (nb - 1 - i, cb))
    full = lambda a: pl.BlockSpec(a.shape, lambda i: (0,) * a.ndim)
    tab = pl.BlockSpec((BLK, DH), lambda i: (nb - 1 - i, 0))
    return pl.pallas_call(
        body, name="ret_bwd", grid=(nb,),
        in_specs=[rev(0), rev(0), rev(R_V // WIDTH), pl.BlockSpec((1, HEADS, DH, DH), lambda i: (nb - 1 - i, 0, 0, 0)),
                  rev(0), tab, tab, full(decay), full(zeta), full(xi)],
        out_specs=[rev(0), rev(0), rev(0)],
        out_shape=[jax.ShapeDtypeStruct((s, WIDTH), BF16)] * 3,
        scratch_shapes=[pltpu.VMEM((HEADS, DH, DH), F32)],
        compiler_params=_cparams(("arbitrary",)),
    )(qr, kr, proj, rall, do_ret, c128, s128, decay, zeta, xi)


def _attn_bwd_dq(qn, qp, kv, kp, do, o, lse, s):
    t = _row_tile(s, 384)

    def body(qn_ref, qp_ref, kn_ref, v_ref, kp_ref, do_ref, o_ref, lse_ref, dqn_ref, dqp_ref):
        qi = pl.program_id(1)
        q_n, q_p, dov = qn_ref[...], qp_ref[...], do_ref[...]
        lse_t = lse_ref[0]
        delta = jnp.sum(dov.astype(F32) * o_ref[...].astype(F32), axis=-1, keepdims=True)

        def step(ki, carry):
            dqn, dqp = carry
            ks = pl.ds(pl.multiple_of(ki * t, t), t)
            kn, kpv = kn_ref[ks, :], kp_ref[ks, :]
            sc = (_dot(q_n, kn, "nt") + _dot(q_p, kpv, "nt")) * ATT_SCALE
            sc = jnp.where(_att_mask(qi, ki, t), sc, NEG_INF)
            p = jnp.exp(sc - lse_t)
            dp = _dot(dov, v_ref[ks, :], "nt")
            ds = (p * (dp - delta) * ATT_SCALE).astype(BF16)
            return dqn + _dot(ds, kn, "nn"), dqp + _dot(ds, kpv, "nn")

        dqn, dqp = lax.fori_loop(0, qi + 1, step, (jnp.zeros((t, DH), F32), jnp.zeros((t, DH), F32)))
        dqn_ref[...] = dqn.astype(BF16)
        dqp_ref[...] = dqp.astype(BF16)

    qtile = pl.BlockSpec((t, DH), lambda h, i: (i, h))
    return pl.pallas_call(
        body, name="attn_bwd_dq", grid=(HEADS, s // t),
        in_specs=[qtile, qtile, pl.BlockSpec((s, DH), lambda h, i: (0, 2 * h)),
                  pl.BlockSpec((s, DH), lambda h, i: (0, 2 * h + 1)), pl.BlockSpec((s, DH), lambda h, i: (0, 0)),
                  qtile, qtile, pl.BlockSpec((1, t, 1), lambda h, i: (h, i, 0))],
        out_specs=[qtile, qtile],
        out_shape=[jax.ShapeDtypeStruct((s, WIDTH), BF16)] * 2,
        compiler_params=_cparams(("parallel", "arbitrary")),
    )(qn, qp, kv, kv, kp, do, o, lse)


def _attn_bwd_dkv(qn, qp, kv, kp, do, o, lse, s):
    t = _row_tile(s, 384)
    nq = s // t

    def body(qn_ref, qp_ref, kn_ref, v_ref, kp_ref, do_ref, o_ref, lse_ref, dkv_ref, dkp_ref):
        ki = pl.program_id(1)
        kn, v, kpv = kn_ref[...], v_ref[...], kp_ref[...]

        def step(qi, carry):
            dkn, dv, dkp = carry
            qs = pl.ds(pl.multiple_of(qi * t, t), t)
            q_n, q_p, dov = qn_ref[qs, :], qp_ref[qs, :], do_ref[qs, :]
            delta = jnp.sum(dov.astype(F32) * o_ref[qs, :].astype(F32), axis=-1, keepdims=True)
            sc = (_dot(q_n, kn, "nt") + _dot(q_p, kpv, "nt")) * ATT_SCALE
            sc = jnp.where(_att_mask(qi, ki, t), sc, NEG_INF)
            p = jnp.exp(sc - lse_ref[0, qs, :])
            dp = _dot(dov, v, "nt")
            ds = (p * (dp - delta) * ATT_SCALE).astype(BF16)
            return (dkn + _dot(ds, q_n, "tn"), dv + _dot(p.astype(BF16), dov, "tn"), dkp + _dot(ds, q_p, "tn"))

        z = jnp.zeros((t, DH), F32)
        dkn, dv, dkp = lax.fori_loop(ki, nq, step, (z, z, z))
        dkv_ref[:, :DH] = dkn.astype(BF16)
        dkv_ref[:, DH:] = dv.astype(BF16)
        dkp_ref[0] = dkp

    full = lambda: pl.BlockSpec((s, DH), lambda h, i: (0, h))
    return pl.pallas_call(
        body, name="attn_bwd_dkv", grid=(HEADS, nq),
        in_specs=[full(), full(), pl.BlockSpec((t, DH), lambda h, i: (i, 2 * h)),
                  pl.BlockSpec((t, DH), lambda h, i: (i, 2 * h + 1)), pl.BlockSpec((t, DH), lambda h, i: (i, 0)),
                  full(), full(), pl.BlockSpec((1, s, 1), lambda h, i: (h, 0, 0))],
        out_specs=[pl.BlockSpec((t, 2 * DH), lambda h, i: (i, h)), pl.BlockSpec((1, t, DH), lambda h, i: (h, i, 0))],
        out_shape=[jax.ShapeDtypeStruct((s, 2 * WIDTH), BF16), jax.ShapeDtypeStruct((HEADS, s, DH), F32)],
        compiler_params=_cparams(("parallel", "arbitrary")),
    )(qn, qp, kv, kv, kp, do, o, lse)


def _rms_bwd(dn, nhat, r, w):
    g = dn * w
    return r * (g - nhat * jnp.mean(g * nhat, axis=-1, keepdims=True)), dn * nhat


def _mla_prep_bwd(dqn, dqp, dkv, dkp_h, proj, rq, rkv, qn_w, kvn_w, wqn, wqp, wukv, c64, sa64, sb64, s):
    tm = _row_tile(s, 384)

    def body(dqn_ref, dqp_ref, dkv_ref, dkp_ref, cq_ref, ckv_ref, rq_ref, rkv_ref, qnw_ref, kvnw_ref,
             wqn_ref, wqp_ref, wukv_ref, c_ref, sa_ref, sb_ref,
             dcq_ref, dckv_ref, dkpe_ref, dqpr_ref, dqnw_ref, dkvnw_ref):
        @pl.when(pl.program_id(0) == 0)
        def _():
            dqnw_ref[...] = jnp.zeros_like(dqnw_ref)
            dkvnw_ref[...] = jnp.zeros_like(dkvnw_ref)

        c, sa, sb = c_ref[...], sa_ref[...], sb_ref[...]
        for h in range(HEADS):
            sl = slice(h * DH, (h + 1) * DH)
            dqpr_ref[:, sl] = _rope64(dqp_ref[:, sl].astype(F32), c, -sa, -sb).astype(BF16)
        dcqn = _dot(dqn_ref[...], wqn_ref[...], "nn") + _dot(dqpr_ref[...], wqp_ref[...], "nn")
        rq_v = rq_ref[...]
        dcq, prod = _rms_bwd(dcqn, cq_ref[...].astype(F32) * rq_v, rq_v, qnw_ref[...])
        dcq_ref[...] = dcq.astype(BF16)
        dqnw_ref[...] += jnp.sum(prod, axis=0, keepdims=True)
        dckvn = _dot(dkv_ref[...], wukv_ref[...], "nn")
        rkv_v = rkv_ref[...]
        dckv, prod = _rms_bwd(dckvn, ckv_ref[...].astype(F32) * rkv_v, rkv_v, kvnw_ref[...])
        dckv_ref[...] = dckv.astype(BF16)
        dkvnw_ref[...] += jnp.sum(prod, axis=0, keepdims=True)
        dkp = dkp_ref[0]
        for h in range(1, HEADS):
            dkp = dkp + dkp_ref[h]
        dkpe_ref[:, :DH] = _rope64(dkp, c, -sa, -sb).astype(BF16)
        dkpe_ref[:, DH:] = jnp.zeros((tm, DH), BF16)

    row = lambda w, cb: pl.BlockSpec((tm, w), lambda i: (i, cb))
    full = lambda a: pl.BlockSpec(a.shape, lambda i: (0, 0))
    return pl.pallas_call(
        body, name="mla_prep_bwd", grid=(s // tm,),
        in_specs=[row(WIDTH, 0), row(WIDTH, 0), row(2 * WIDTH, 0), pl.BlockSpec((HEADS, tm, DH), lambda i: (0, i, 0)),
                  row(Q_RANK, C_Q // Q_RANK), row(KV_RANK, C_KV // KV_RANK), row(1, 0), row(1, 0),
                  full(qn_w), full(kvn_w), full(wqn), full(wqp), full(wukv), row(DH, 0), row(DH, 0), row(DH, 0)],
        out_specs=[row(Q_RANK, 0), row(KV_RANK, 0), row(2 * DH, 0), row(WIDTH, 0),
                   pl.BlockSpec((1, Q_RANK), lambda i: (0, 0)), pl.BlockSpec((1, KV_RANK), lambda i: (0, 0))],
        out_shape=[jax.ShapeDtypeStruct((s, Q_RANK), BF16), jax.ShapeDtypeStruct((s, KV_RANK), BF16),
                   jax.ShapeDtypeStruct((s, 2 * DH), BF16), jax.ShapeDtypeStruct((s, WIDTH), BF16),
                   jax.ShapeDtypeStruct((1, Q_RANK), F32), jax.ShapeDtypeStruct((1, KV_RANK), F32)],
        compiler_params=_cparams(("arbitrary",)),
    )(dqn, dqp, dkv, dkp_h, proj, proj, rq, rkv, qn_w, kvn_w, wqn, wqp, wukv, c64, sa64, sb64)


def _rms_in_bwd(dxn, h, r, dh2, norm_w, s):
    def body(dxn_ref, h_ref, r_ref, dh2_ref, w_ref, gx_ref, gm_ref, dw_ref):
        i = pl.program_id(0)
        r_v = r_ref[...]
        dx, prod = _rms_bwd(dxn_ref[...], h_ref[...] * r_v, r_v, w_ref[...])
        dh = dh2_ref[...] + dx

        @pl.when(i == 0)
        def _():
            gm_ref[...] = dh
            dw_ref[...] = jnp.sum(prod, axis=0, keepdims=True)

        @pl.when(i > 0)
        def _():
            gx_ref[...] = dh
            dw_ref[...] += jnp.sum(prod, axis=0, keepdims=True)

    blk = pl.BlockSpec((BLK, D), lambda i: (i, 0))
    return pl.pallas_call(
        body, name="rms_in_bwd", grid=(s // BLK,),
        in_specs=[blk, blk, pl.BlockSpec((BLK, 1), lambda i: (i, 0)), blk, pl.BlockSpec((1, D), lambda i: (0, 0))],
        out_specs=[pl.BlockSpec((BLK, D), lambda i: (jnp.maximum(i - 1, 0), 0)), pl.BlockSpec((BLK, D), lambda i: (0, 0)),
                   pl.BlockSpec((1, D), lambda i: (0, 0))],
        out_shape=[jax.ShapeDtypeStruct((s - BLK, D), F32), jax.ShapeDtypeStruct((BLK, D), F32),
                   jax.ShapeDtypeStruct((1, D), F32)],
        compiler_params=_cparams(("arbitrary",)),
    )(dxn, h, r, dh2, norm_w)


def _adam_math(w, g, m, v):
    m = ADAM_B1 * m + (1.0 - ADAM_B1) * g
    v = ADAM_B2 * v + (1.0 - ADAM_B2) * (g * g)
    m_hat = m / (1.0 - ADAM_B1 ** ADAM_STEP)
    v_hat = v / (1.0 - ADAM_B2 ** ADAM_STEP)
    return -ADAM_LR * (m_hat / (jnp.sqrt(v_hat) + ADAM_EPS) + ADAM_WD * w), m, v


def _adamw(w, g, m, v, name):
    rows, cols = w.shape
    tr = rows
    for cand in (128, 64, 32, 16, 8):
        if rows % cand == 0:
            tr = cand
            break

    def body(w_ref, g_ref, m_ref, v_ref, d_ref, nm_ref, nv_ref):
        d_ref[...], nm_ref[...], nv_ref[...] = _adam_math(w_ref[...], g_ref[...], m_ref[...], v_ref[...])

    spec = pl.BlockSpec((tr, cols), lambda i: (i, 0))
    return pl.pallas_call(
        body, name=name, grid=(rows // tr,), in_specs=[spec] * 4, out_specs=[spec] * 3,
        out_shape=[jax.ShapeDtypeStruct((rows, cols), F32)] * 3,
        compiler_params=_cparams(("parallel",)),
    )(w, g, m, v)


def _adamw_small(w, gall, m, v):
    def body(w_ref, g_ref, m_ref, v_ref, gs_ref, d_ref, nm_ref, nv_ref):
        g = g_ref[0]
        for dev in range(1, 8):
            g = g + g_ref[dev]
        gs_ref[...] = g
        d_ref[...], nm_ref[...], nv_ref[...] = _adam_math(w_ref[...], g, m_ref[...], v_ref[...])

    return pl.pallas_call(
        body, name="adamw_small", out_shape=[jax.ShapeDtypeStruct((SM_ROWS, 128), F32)] * 4,
        compiler_params=pltpu.CompilerParams(vmem_limit_bytes=VMEM_LIMIT),
    )(w, gall, m, v)


ADD_ROWS = 464
HALF_BLOCKS = PK_HALF // ADD_ROWS


def _rs_core_add(gpk, land, pos):
    def body(pos_ref, a_ref, b_ref, o_ref):
        o_ref[...] = (a_ref[...].astype(F32) + b_ref[...].astype(F32)).astype(BF16)

    return pl.pallas_call(
        body, name="rs_core_add",
        grid_spec=pltpu.PrefetchScalarGridSpec(
            num_scalar_prefetch=1, grid=(N_CHIPS, HALF_BLOCKS),
            in_specs=[pl.BlockSpec((1, ADD_ROWS, D), lambda s, i, pos_ref: (s, pos_ref[1] * HALF_BLOCKS + i, 0)),
                      pl.BlockSpec((1, ADD_ROWS, D), lambda s, i, pos_ref: (s, i, 0))],
            out_specs=pl.BlockSpec((1, ADD_ROWS, D), lambda s, i, pos_ref: (s, i, 0))),
        out_shape=jax.ShapeDtypeStruct((N_CHIPS, PK_HALF, D), BF16),
        compiler_params=_cparams(("parallel", "parallel")),
    )(pos, gpk, land)


def _rs_chip_add(cp, land, pos):
    def body(pos_ref, a_ref, l_ref, o_ref):
        o_ref[0] = ((a_ref[0].astype(F32) + l_ref[0].astype(F32)) + l_ref[1].astype(F32)) + l_ref[2].astype(F32)

    return pl.pallas_call(
        body, name="rs_chip_add",
        grid_spec=pltpu.PrefetchScalarGridSpec(
            num_scalar_prefetch=1, grid=(HALF_BLOCKS,),
            in_specs=[pl.BlockSpec((1, ADD_ROWS, D), lambda i, pos_ref: (pos_ref[0], i, 0)),
                      pl.BlockSpec((3, ADD_ROWS, D), lambda i, pos_ref: (0, i, 0))],
            out_specs=pl.BlockSpec((1, ADD_ROWS, D), lambda i, pos_ref: (pos_ref[1], i, 0))),
        out_shape=jax.ShapeDtypeStruct((2, PK_HALF, D), F32),
        compiler_params=_cparams(("parallel",)),
    )(pos, cp, land)


ANY = pl.BlockSpec(memory_space=pl.ANY)


def _mesh_pos():
    return lax.axis_index("x"), lax.axis_index("y"), lax.axis_index("c")


def _chip_peer(x, y, c, r):
    return (jnp.bitwise_xor(x, r >> 1), jnp.bitwise_xor(y, r & 1), c)


def _allgather(big0, pack, meta_loc):
    def body(big0_ref, pack_ref, meta_ref, big_ref, metaf_ref, lsem, ssem, rsem, fssem, frsem, msem_s, msem_r):
        del big0_ref
        x, y, c = _mesh_pos()
        j = 2 * x + y
        me, sibling = (x, y, c), (x, y, 1 - c)

        def half_wait(s_sem, r_sem):
            rows = big_ref.at[pl.ds(0, PK_HALF), :]
            return pltpu.make_async_remote_copy(src_ref=rows, dst_ref=rows, send_sem=s_sem, recv_sem=r_sem,
                                                device_id=me, device_id_type=MESH)

        own_meta = pltpu.make_async_copy(meta_ref, metaf_ref.at[j], lsem)
        own_meta.start()
        for s in range(N_CHIPS):
            for hf in range(2):
                @pl.when((j == s) & (c == hf))
                def _():
                    for r in (1, 2, 3):
                        for p, b, n in _half_pieces(s, hf):
                            pltpu.make_async_remote_copy(
                                src_ref=pack_ref.at[pl.ds(p, n), :], dst_ref=big_ref.at[pl.ds(b, n), :],
                                send_sem=ssem.at[r - 1], recv_sem=rsem.at[r - 1],
                                device_id=_chip_peer(x, y, c, r), device_id_type=MESH).start()

        meta_copies = [pltpu.make_async_remote_copy(
            src_ref=meta_ref, dst_ref=metaf_ref.at[j], send_sem=msem_s.at[r - 1], recv_sem=msem_r.at[r - 1],
            device_id=_chip_peer(x, y, c, r), device_id_type=MESH) for r in (1, 2, 3)]
        for cp in meta_copies:
            cp.start()
        for r in (1, 2, 3):
            half_wait(ssem.at[r - 1], rsem.at[r - 1]).wait_recv()
            src_shard = jnp.bitwise_xor(j, r)
            for s in range(N_CHIPS):
                for hf in range(2):
                    @pl.when((src_shard == s) & (c == hf))
                    def _():
                        for p, b, n in _half_pieces(s, hf):
                            rows = big_ref.at[pl.ds(b, n), :]
                            pltpu.make_async_remote_copy(
                                src_ref=rows, dst_ref=rows, send_sem=fssem.at[r - 1], recv_sem=frsem.at[r - 1],
                                device_id=sibling, device_id_type=MESH).start()
        for r in (1, 2, 3):
            half_wait(fssem.at[r - 1], frsem.at[r - 1]).wait_recv()
        for r in (1, 2, 3):
            half_wait(ssem.at[r - 1], rsem.at[r - 1]).wait_send()
            half_wait(fssem.at[r - 1], frsem.at[r - 1]).wait_send()
        for cp in meta_copies:
            cp.wait_send()
            cp.wait_recv()
        own_meta.wait()

    dma3 = pltpu.SemaphoreType.DMA((3,))
    return pl.pallas_call(
        body, name="allgather_weights", in_specs=[ANY, ANY, ANY], out_specs=[ANY, ANY],
        out_shape=[jax.ShapeDtypeStruct((BIG_ROWS, D), BF16), jax.ShapeDtypeStruct((N_CHIPS,) + meta_loc.shape, F32)],
        scratch_shapes=[pltpu.SemaphoreType.DMA(()), dma3, dma3, dma3, dma3, dma3, dma3],
        input_output_aliases={0: 0},
        compiler_params=pltpu.CompilerParams(has_side_effects=True),
    )(big0, pack, meta_loc)


RS_SPLIT = 2


def _rs_core_exchange(gpk):
    n = PK_HALF // RS_SPLIT

    def body(g_ref, land_ref, ssem, rsem):
        x, y, c = _mesh_pos()
        me, sibling = (x, y, c), (x, y, 1 - c)
        for s in range(N_CHIPS):
            for q in range(RS_SPLIT):
                src = pl.ds(pl.multiple_of((1 - c) * PK_HALF + q * n, 16), n)
                pltpu.make_async_remote_copy(
                    src_ref=g_ref.at[s, src, :], dst_ref=land_ref.at[s, pl.ds(q * n, n), :],
                    send_sem=ssem, recv_sem=rsem, device_id=sibling, device_id_type=MESH).start()
        whole = pltpu.make_async_remote_copy(src_ref=land_ref, dst_ref=land_ref, send_sem=ssem, recv_sem=rsem,
                                             device_id=me, device_id_type=MESH)
        whole.wait_recv()
        whole.wait_send()

    return pl.pallas_call(
        body, name="rs_core_exchange", in_specs=[ANY], out_specs=ANY,
        out_shape=jax.ShapeDtypeStruct((N_CHIPS, PK_HALF, D), BF16),
        scratch_shapes=[pltpu.SemaphoreType.DMA(())] * 2,
        compiler_params=pltpu.CompilerParams(has_side_effects=True),
    )(gpk)


def _rs_chip_exchange(cp):
    def body(cp_ref, land_ref, ssem, rsem):
        x, y, c = _mesh_pos()
        j = 2 * x + y
        copies = [pltpu.make_async_remote_copy(
            src_ref=cp_ref.at[jnp.bitwise_xor(j, r)], dst_ref=land_ref.at[r - 1], send_sem=ssem.at[r - 1],
            recv_sem=rsem.at[r - 1], device_id=_chip_peer(x, y, c, r), device_id_type=MESH) for r in (1, 2, 3)]
        for cpy in copies:
            cpy.start()
        for cpy in copies:
            cpy.wait_recv()
        for cpy in copies:
            cpy.wait_send()

    dma3 = pltpu.SemaphoreType.DMA((3,))
    return pl.pallas_call(
        body, name="rs_chip_exchange", in_specs=[ANY], out_specs=ANY,
        out_shape=jax.ShapeDtypeStruct((3, PK_HALF, D), BF16), scratch_shapes=[dma3, dma3],
        compiler_params=pltpu.CompilerParams(has_side_effects=True),
    )(cp)


def _rs_finish(full, small):
    n = PK_HALF // RS_SPLIT

    def body(full_in_ref, sm_ref, full_ref, all_ref, lsem, ssem, rsem, sm_s, sm_r):
        del full_in_ref
        x, y, c = _mesh_pos()
        me, sibling = (x, y, c), (x, y, 1 - c)
        my_id = 4 * x + 2 * y + c
        own_sm = pltpu.make_async_copy(sm_ref, all_ref.at[my_id], lsem)
        own_sm.start()
        for q in range(RS_SPLIT):
            rows = full_ref.at[c, pl.ds(q * n, n), :]
            pltpu.make_async_remote_copy(src_ref=rows, dst_ref=rows, send_sem=ssem, recv_sem=rsem,
                                         device_id=sibling, device_id_type=MESH).start()
        half = pltpu.make_async_remote_copy(src_ref=full_ref.at[c], dst_ref=full_ref.at[c], send_sem=ssem,
                                            recv_sem=rsem, device_id=me, device_id_type=MESH)
        smalls = [pltpu.make_async_remote_copy(
            src_ref=sm_ref, dst_ref=all_ref.at[my_id], send_sem=sm_s.at[r - 1], recv_sem=sm_r.at[r - 1],
            device_id=(jnp.bitwise_xor(x, r >> 2), jnp.bitwise_xor(y, (r >> 1) & 1), jnp.bitwise_xor(c, r & 1)),
            device_id_type=MESH) for r in range(1, 8)]
        for cpy in smalls:
            cpy.start()
        half.wait_recv()
        half.wait_send()
        for cpy in smalls:
            cpy.wait_recv()
        for cpy in smalls:
            cpy.wait_send()
        own_sm.wait()

    dma7 = pltpu.SemaphoreType.DMA((7,))
    return pl.pallas_call(
        body, name="rs_finish", in_specs=[ANY, ANY], out_specs=[ANY, ANY],
        out_shape=[jax.ShapeDtypeStruct((2, PK_HALF, D), F32), jax.ShapeDtypeStruct((8, SM_ROWS, 128), F32)],
        scratch_shapes=[pltpu.SemaphoreType.DMA(()), pltpu.SemaphoreType.DMA(()), pltpu.SemaphoreType.DMA(()), dma7, dma7],
        input_output_aliases={0: 0},
        compiler_params=pltpu.CompilerParams(has_side_effects=True),
    )(full, small)


def _rope_tables(s):
    pos = jnp.arange(s, dtype=F32) - PAD_FRONT

    def cs(d):
        inv = ROPE_BASE ** (-jnp.arange(0, d, 2, dtype=F32) / d)
        ang = pos[:, None] * inv[None, :]
        return jnp.cos(ang), jnp.sin(ang)

    c, sn = cs(ROPE_D)
    z = jnp.zeros_like(c)
    c64 = jnp.concatenate([c, c, z, z], axis=1)
    sa64 = jnp.concatenate([-sn, z, z, z], axis=1)
    sb64 = jnp.concatenate([z, sn, z, z], axis=1)
    c, sn = cs(DH)
    return c64, sa64, sb64, jnp.concatenate([c, c], axis=1), jnp.concatenate([-sn, sn], axis=1)


def _grad_mm(a, b, m, n, s, name):
    return _matmul(a, b, "tn", m, n, s, min(m, 512), min(n, 1024), s, BF16, name, n_outer=True)


def _local_step(x2, target2, hp0, big, small_w):
    norm_w, qn_w, kvn_w, gn_w, gn_b, fin_w = small_w
    s = x2.shape[0] + BLK
    tm = _row_tile(s, 1408)
    uq = big[BIG_UQ:BIG_UQ + 384].reshape(HEADS, DH + ROPE_D, Q_RANK)
    wqn = uq[:, :DH].reshape(WIDTH, Q_RANK)
    wqp = jnp.pad(uq[:, DH:], ((0, 0), (0, DH - ROPE_D), (0, 0))).reshape(WIDTH, Q_RANK)
    wukv = big[BIG_UKV:BIG_UKV + 256].reshape(2 * WIDTH, KV_RANK)
    wbm = big[BIG_BM:BIG_BM + 1024].reshape(D, WIDTH)
    wbr = big[BIG_BR:BIG_BR + 1024].reshape(D, WIDTH)
    c64, sa64, sb64, c128, s128 = _rope_tables(s)
    consts, gamma = _ret_consts()

    h, xn, r1 = _rms_in(x2, hp0, norm_w, s)
    proj = _matmul(xn, big, "nt", s, PROJ_W, D, tm, 512, D, BF16, "proj", b_off=(BIG_IN // 512, 0))
    cqn, rq, ckvn, rkv, qn, qp, kv, kp = _mla_prep(proj, qn_w, kvn_w, wqn, wqp, wukv, c64, sa64, sb64, s)
    o_mla, y_mla, lse = _attn_fwd(qn, qp, kv, kp, proj, s)
    y_ret, on, rstd, qr, kr, rall = _ret_fwd(proj, gn_w, gn_b, c128, s128, consts, gamma, s)
    u_mla, u_ret, merged = _merge_fwd(y_mla, y_ret, wbm, wbr, proj, s)
    h2 = _out_fwd(merged, big, h, s)
    dh2, dh2b, loss_blk, dfin = _loss_bwd(h2, target2, fin_w, s)

    dg0, dg1, du_mla, du_ret = _merge_bwd(dh2b, big, u_mla, u_ret, proj, s)
    g_out = _grad_mm(merged, dh2b, D, D, s, "grad_w_out")
    do_mla, dz_mla = _branch_mla_bwd(du_mla, wbm, o_mla, proj, s)
    g_bm = _grad_mm(du_mla, y_mla, D, WIDTH, s, "grad_w_branch_mla")
    do_ret, dz_ret, dgw, dgb = _branch_ret_bwd(du_ret, wbr, on, rstd, gn_w, gn_b, proj, s)
    g_br = _grad_mm(du_ret, y_ret, D, WIDTH, s, "grad_w_branch_ret")
    dr_q, dr_k, dr_v = _ret_bwd(qr, kr, proj, rall, do_ret, c128, s128, consts, gamma, s)
    dqn, dqp = _attn_bwd_dq(qn, qp, kv, kp, do_mla, o_mla, lse, s)
    dkv, dkp_h = _attn_bwd_dkv(qn, qp, kv, kp, do_mla, o_mla, lse, s)
    dcq, dckv, dkpe, dqpr, dqnw, dkvnw = _mla_prep_bwd(dqn, dqp, dkv, dkp_h, proj, rq, rkv, qn_w, kvn_w,
                                                       wqn, wqp, wukv, c64, sa64, sb64, s)
    g_qn = _grad_mm(dqn, cqn, WIDTH, Q_RANK, s, "grad_w_uq_nope")
    g_qp = _grad_mm(dqpr, cqn, WIDTH, Q_RANK, s, "grad_w_uq_rope")
    g_ukv = _grad_mm(dkv, ckvn, 2 * WIDTH, KV_RANK, s, "grad_w_ukv")
    dproj = jnp.concatenate([dz_mla, dr_q, dr_k, dr_v, dz_ret, dg0, dg1, dcq, dckv, dkpe], axis=1)
    dxn = _matmul(dproj, big, "nn", s, D, PROJ_W, tm, 1024, 1024, F32, "dxn", b_off=(BIG_IN // 1024, 0))
    grad_x, gmeta_blk, dnorm = _rms_in_bwd(dxn, h, r1, dh2, norm_w, s)
    g_in = _grad_mm(dproj, xn, PROJ_W, D, s, "grad_w_in")

    g_uq = jnp.concatenate([g_qn.reshape(HEADS, DH, Q_RANK), g_qp.reshape(HEADS, DH, Q_RANK)[:, :ROPE_D]], axis=1)
    parts = [(g_out, 512), (g_bm.reshape(1024, D), 256), (g_br.reshape(1024, D), 256), (g_uq.reshape(384, D), 96),
             (g_ukv.reshape(256, D), 64)]

    def shard_rows(sh):
        out = [p[n * sh:n * (sh + 1)] for p, n in parts]
        if sh == 0:
            out += [g_in[C_Q:C_Q + ROT], g_in[:IN_SHARD - ROT]]
        else:
            out += [g_in[IN_SHARD * sh - ROT:IN_SHARD * (sh + 1) - ROT]]
        return jnp.concatenate(out + [jnp.zeros((PK_ROWS - PK_PAD, D), BF16)], axis=0)

    gbig = jnp.stack([shard_rows(sh) for sh in range(N_CHIPS)])
    rows = lambda a: a.reshape(-1, 128)
    small = jnp.concatenate([loss_blk[0:1], rows(dnorm), rows(dqnw), rows(dkvnw), rows(dgw), rows(dgb), rows(dfin),
                             rows(gmeta_blk[PAD_FRONT:]), jnp.zeros((1, 128), F32)], axis=0)
    return grad_x, gbig, small


def _small_rows(ws):
    rows = lambda a: a.reshape(-1, 128)
    return jnp.concatenate([jnp.zeros((1, 128), F32)] + [rows(w) for w in ws]
                           + [jnp.zeros((SM_ROWS - SM_META, 128), F32)], axis=0)


def kernel(x, meta, norm_w, w_in, mla_q_norm_w, mla_w_uq, mla_kv_norm_w, mla_w_ukv, ret_gn_w, ret_gn_b, w_branch_mla, w_branch_ret, w_out, final_norm_w, loss_target, m_meta, m_norm_w, m_w_in, m_mla_q_norm_w, m_mla_w_uq, m_mla_kv_norm_w, m_mla_w_ukv, m_ret_gn_w, m_ret_gn_b, m_w_branch_mla, m_w_branch_ret, m_w_out, m_final_norm_w, v_meta, v_norm_w, v_w_in, v_mla_q_norm_w, v_mla_w_uq, v_mla_kv_norm_w, v_mla_w_ukv, v_ret_gn_w, v_ret_gn_b, v_w_branch_mla, v_w_branch_ret, v_w_out, v_final_norm_w):
    j = 2 * lax.axis_index("x") + lax.axis_index("y")
    tr = lambda w: w[0].T.reshape(-1, D).astype(BF16)
    pack = jnp.concatenate([w_out[0].astype(BF16), tr(w_branch_mla), tr(w_branch_ret), tr(mla_w_uq), tr(mla_w_ukv),
                            tr(w_in), jnp.zeros((PK_ROWS - PK_PAD, D), BF16)], axis=0)
    pos = jnp.stack([j, lax.axis_index("c")]).astype(jnp.int32)
    put = lambda buf, lo, hi, off: lax.dynamic_update_slice(buf, pack[lo:hi], (off, 0))
    big0 = lax.empty((BIG_ROWS, D), BF16)
    big0 = put(big0, PK_OUT, PK_BM, BIG_OUT + 512 * j)
    big0 = put(big0, PK_BM, PK_BR, BIG_BM + 256 * j)
    big0 = put(big0, PK_BR, PK_UQ, BIG_BR + 256 * j)
    big0 = put(big0, PK_UQ, PK_UKV, BIG_UQ + 96 * j)
    big0 = put(big0, PK_UKV, PK_IN, BIG_UKV + 64 * j)
    big0 = put(big0, PK_IN, PK_IN + ROT, BIG_IN + jnp.where(j == 0, C_Q, IN_SHARD * j - ROT))
    big0 = put(big0, PK_IN + ROT, PK_PAD, BIG_IN + IN_SHARD * j)
    big0 = lax.dynamic_update_slice(big0, jnp.zeros((ZERO_ROWS, D), BF16), (BIG_IN + IN_WIDTH, 0))
    big, meta_all = _allgather(big0, pack, meta)
    meta_full = meta_all.transpose(1, 0, 2).reshape(N_META, D)
    hp0 = jnp.concatenate([jnp.zeros((PAD_FRONT, D), F32), meta_full], axis=0)
    small_w = (norm_w, mla_q_norm_w, mla_kv_norm_w, ret_gn_w, ret_gn_b, final_norm_w.reshape(1, D))

    grad_x, gbig, small = _local_step(x[0], loss_target[0], hp0, big, small_w)

    land = _rs_core_exchange(gbig)
    chip_part = _rs_core_add(gbig, land, pos)
    land2 = _rs_chip_exchange(chip_part)
    full, small_all = _rs_finish(_rs_chip_add(chip_part, land2, pos), small)
    full = full.reshape(PK_ROWS, D)

    untr = lambda lo, hi, rows: full[lo:hi].reshape(rows, -1).T
    grads = {
        "w_out": full[PK_OUT:PK_BM], "w_branch_mla": untr(PK_BM, PK_BR, 512), "w_branch_ret": untr(PK_BR, PK_UQ, 512),
        "mla_w_uq": untr(PK_UQ, PK_UKV, 384), "mla_w_ukv": untr(PK_UKV, PK_IN, 512), "w_in": untr(PK_IN, PK_PAD, IN_SHARD),
    }
    big_w = {"w_in": (w_in, m_w_in, v_w_in), "mla_w_uq": (mla_w_uq, m_mla_w_uq, v_mla_w_uq),
             "mla_w_ukv": (mla_w_ukv, m_mla_w_ukv, v_mla_w_ukv),
             "w_branch_mla": (w_branch_mla, m_w_branch_mla, v_w_branch_mla),
             "w_branch_ret": (w_branch_ret, m_w_branch_ret, v_w_branch_ret), "w_out": (w_out, m_w_out, v_w_out)}
    res = {}
    for name, (w, m, v) in big_w.items():
        d, nm, nv = _adamw(w[0], grads[name], m[0], v[0], "adamw_" + name)
        res[name] = (grads[name][None], d[None], nm[None], nv[None])

    small_m = (m_norm_w, m_mla_q_norm_w, m_mla_kv_norm_w, m_ret_gn_w, m_ret_gn_b, m_final_norm_w.reshape(1, D))
    small_v = (v_norm_w, v_mla_q_norm_w, v_mla_kv_norm_w, v_ret_gn_w, v_ret_gn_b, v_final_norm_w.reshape(1, D))
    gs, ds, ms, vs = _adamw_small(_small_rows(small_w), small_all, _small_rows(small_m), _small_rows(small_v))
    names = ["norm_w", "mla_q_norm_w", "mla_kv_norm_w", "ret_gn_w", "ret_gn_b", "final_norm_w"]
    bounds = [SM_NORM, SM_QN, SM_KVN, SM_GNW, SM_GNB, SM_FIN, SM_META]
    for k, name in enumerate(names):
        shape = (D,) if name == "final_norm_w" else (1, -1)
        res[name] = tuple(a[bounds[k]:bounds[k + 1]].reshape(shape) for a in (gs, ds, ms, vs))
    g_meta = lax.dynamic_slice_in_dim(gs[SM_META:SM_META + 256].reshape(N_META, D), j * (D // N_CHIPS), D // N_CHIPS, axis=1)
    res["meta"] = (g_meta,) + tuple(_adamw(meta, g_meta, m_meta, v_meta, "adamw_meta"))

    order = ["meta", "norm_w", "w_in", "mla_q_norm_w", "mla_w_uq", "mla_kv_norm_w", "mla_w_ukv", "ret_gn_w", "ret_gn_b",
             "w_branch_mla", "w_branch_ret", "w_out", "final_norm_w"]
    return (gs[0, 0], grad_x[None]) + tuple(res[n][k] for k in range(4) for n in order)
```

```python
import functools
import math

import numpy as np
import jax
import jax.numpy as jnp
from jax import lax
from jax.experimental import pallas as pl
from jax.experimental.pallas import tpu as pltpu

F32 = jnp.float32
BF16 = jnp.bfloat16
MESH = pl.DeviceIdType.MESH

D = 2048
N_META = 16
BLK = 128
PAD_FRONT = BLK - N_META
HEADS = 8
DH = 128
ROPE_D = 64
Q_RANK = 512
KV_RANK = 256
WIDTH = HEADS * DH
ROPE_BASE = 10000.0
NORM_EPS = 1e-6
GN_EPS = 1e-5
NEG_INF = -1e30
ATT_SCALE = (DH + ROPE_D) ** -0.5
RET_SCALE = DH ** -0.5
IN_WIDTH = 10048
N_CHIPS = 4
IN_SHARD = IN_WIDTH // N_CHIPS
ADAM_LR, ADAM_B1, ADAM_B2, ADAM_EPS, ADAM_WD, ADAM_STEP = 0.001, 0.9, 0.999, 1e-08, 0.01, 10

ROT = Q_RANK + KV_RANK + ROPE_D
Z_MLA, R_Q, R_K, R_V, Z_RET, GATE0, GATE1 = 0, 1024, 2048, 3072, 4096, 5120, 5120 + D
C_Q = IN_WIDTH - ROT
C_KV = C_Q + Q_RANK
K_PE = C_KV + KV_RANK
PROJ_W = 10240

BIG_OUT, BIG_BM, BIG_BR, BIG_UQ, BIG_UKV, BIG_JUNK, BIG_IN = 0, 2048, 3072, 4096, 4480, 4736, 5120
BIG_ROWS = BIG_IN + PROJ_W
ZERO_ROWS = PROJ_W - IN_WIDTH
PK_OUT, PK_BM, PK_BR, PK_UQ, PK_UKV, PK_IN, PK_PAD, PK_ROWS = 0, 512, 768, 1024, 1120, 1184, 3696, 3712
PK_HALF = PK_ROWS // 2
SM_LOSS, SM_NORM, SM_QN, SM_KVN, SM_GNW, SM_GNB, SM_FIN, SM_META, SM_ROWS = 0, 1, 17, 21, 23, 31, 39, 55, 312

VMEM_LIMIT = 56 * 1024 * 1024


def _pieces(shard):
    j = shard
    out = [(PK_OUT, BIG_OUT + 512 * j, 512), (PK_BM, BIG_BM + 256 * j, 256), (PK_BR, BIG_BR + 256 * j, 256),
           (PK_UQ, BIG_UQ + 96 * j, 96), (PK_UKV, BIG_UKV + 64 * j, 64)]
    if j == 0:
        out += [(PK_IN, BIG_IN + C_Q, ROT), (PK_IN + ROT, BIG_IN, IN_SHARD - ROT)]
    else:
        out += [(PK_IN, BIG_IN + IN_SHARD * j - ROT, IN_SHARD)]
    out += [(PK_PAD, BIG_JUNK + 16 * j, 16)]
    return out


def _half_pieces(shard, half):
    lo, hi = half * PK_HALF, (half + 1) * PK_HALF
    out = []
    for p, b, n in _pieces(shard):
        s, e = max(p, lo), min(p + n, hi)
        if e > s:
            out.append((s, b + (s - p), e - s))
    return out


def _row_tile(rows, cap):
    best = BLK
    for t in range(BLK, cap + 1, BLK):
        if rows % t == 0:
            best = t
    return best


def _cparams(sem):
    return pltpu.CompilerParams(dimension_semantics=sem, vmem_limit_bytes=VMEM_LIMIT)


def _dot(a, b, form):
    dn = {"nt": (((1,), (1,)), ((), ())), "nn": (((1,), (0,)), ((), ())), "tn": (((0,), (0,)), ((), ()))}[form]
    return lax.dot_general(a, b, dn, preferred_element_type=F32)


def _sigmoid(v):
    return 1.0 / (1.0 + jnp.exp(-v))


def _matmul(a, b, form, m, n, k, tm, tn, tk, out_dtype, name, a_off=(0, 0), b_off=(0, 0), n_outer=False):
    nk = k // tk
    gi, gj = m // tm, n // tn

    def ij(g0, g1):
        return (g1, g0) if n_outer else (g0, g1)

    if form == "nt":
        a_spec = pl.BlockSpec((tm, tk), lambda g0, g1, kk: (ij(g0, g1)[0] + a_off[0], kk + a_off[1]))
        b_spec = pl.BlockSpec((tn, tk), lambda g0, g1, kk: (ij(g0, g1)[1] + b_off[0], kk + b_off[1]))
    elif form == "nn":
        a_spec = pl.BlockSpec((tm, tk), lambda g0, g1, kk: (ij(g0, g1)[0] + a_off[0], kk + a_off[1]))
        b_spec = pl.BlockSpec((tk, tn), lambda g0, g1, kk: (kk + b_off[0], ij(g0, g1)[1] + b_off[1]))
    else:
        a_spec = pl.BlockSpec((tk, tm), lambda g0, g1, kk: (kk + a_off[0], ij(g0, g1)[0] + a_off[1]))
        b_spec = pl.BlockSpec((tk, tn), lambda g0, g1, kk: (kk + b_off[0], ij(g0, g1)[1] + b_off[1]))
    o_spec = pl.BlockSpec((tm, tn), lambda g0, g1, kk: ij(g0, g1))

    def body(a_ref, b_ref, o_ref, *acc):
        p = _dot(a_ref[...], b_ref[...], form)
        if nk == 1:
            o_ref[...] = p.astype(o_ref.dtype)
        else:
            acc_ref, = acc
            kk = pl.program_id(2)

            @pl.when(kk == 0)
            def _():
                acc_ref[...] = p

            @pl.when(kk > 0)
            def _():
                acc_ref[...] += p

            @pl.when(kk == nk - 1)
            def _():
                o_ref[...] = acc_ref[...].astype(o_ref.dtype)

    return pl.pallas_call(
        body, name=name, grid=(gj, gi, nk) if n_outer else (gi, gj, nk), in_specs=[a_spec, b_spec], out_specs=o_spec,
        out_shape=jax.ShapeDtypeStruct((m, n), out_dtype),
        scratch_shapes=[] if nk == 1 else [pltpu.VMEM((tm, tn), F32)],
        compiler_params=_cparams(("parallel", "parallel", "arbitrary")),
    )(a, b)


def _rms_in(x, hp0, norm_w, s):
    def body(x_ref, hp0_ref, w_ref, h_ref, xn_ref, r_ref):
        def run(hv):
            r = lax.rsqrt(jnp.mean(hv * hv, axis=-1, keepdims=True) + NORM_EPS)
            h_ref[...] = hv
            xn_ref[...] = (hv * r * w_ref[...]).astype(BF16)
            r_ref[...] = r

        @pl.when(pl.program_id(0) == 0)
        def _():
            run(hp0_ref[...])

        @pl.when(pl.program_id(0) > 0)
        def _():
            run(x_ref[...])

    return pl.pallas_call(
        body, name="rms_in", grid=(s // BLK,),
        in_specs=[pl.BlockSpec((BLK, D), lambda i: (jnp.maximum(i - 1, 0), 0)),
                  pl.BlockSpec((BLK, D), lambda i: (0, 0)), pl.BlockSpec((1, D), lambda i: (0, 0))],
        out_specs=[pl.BlockSpec((BLK, D), lambda i: (i, 0)), pl.BlockSpec((BLK, D), lambda i: (i, 0)),
                   pl.BlockSpec((BLK, 1), lambda i: (i, 0))],
        out_shape=[jax.ShapeDtypeStruct((s, D), F32), jax.ShapeDtypeStruct((s, D), BF16),
                   jax.ShapeDtypeStruct((s, 1), F32)],
        compiler_params=_cparams(("arbitrary",)),
    )(x, hp0, norm_w)


def _rope64(t, c, sa, sb):
    return t * c + pltpu.roll(t, t.shape[1] - ROPE_D // 2, 1) * sa + pltpu.roll(t, ROPE_D // 2, 1) * sb


def _rope128(t, c, sg):
    return t * c + pltpu.roll(t, DH // 2, 1) * sg


HD = 2 * DH


def _mla_prep(proj, qn_w, kvn_w, wq, wukv, qtab, ktab, s):
    tm = _row_tile(s, 384)

    def body(cq_ref, ckv_ref, kpe_ref, qnw_ref, kvnw_ref, wq_ref, wukv_ref, qc_ref, qa_ref, qb_ref,
             kc_ref, ka_ref, kb_ref, cqn_ref, rq_ref, ckvn_ref, rkv_ref, qq_ref, kk_ref, vv_ref):
        cq = cq_ref[...].astype(F32)
        rq = lax.rsqrt(jnp.mean(cq * cq, axis=-1, keepdims=True) + NORM_EPS)
        cqn = (cq * rq * qnw_ref[...]).astype(BF16)
        ckv = ckv_ref[...].astype(F32)
        rkv = lax.rsqrt(jnp.mean(ckv * ckv, axis=-1, keepdims=True) + NORM_EPS)
        ckvn = (ckv * rkv * kvnw_ref[...]).astype(BF16)
        cqn_ref[...] = cqn
        rq_ref[...] = rq
        ckvn_ref[...] = ckvn
        rkv_ref[...] = rkv
        q = _dot(cqn, wq_ref[...], "nt")
        kv = _dot(ckvn, wukv_ref[...], "nt")
        kp = _rope64(kpe_ref[...].astype(F32), kc_ref[...], ka_ref[...], kb_ref[...]).astype(BF16)
        qc, qa, qb = qc_ref[...], qa_ref[...], qb_ref[...]
        ones = jnp.ones((tm, DH), BF16)
        for h in range(HEADS):
            lo, mid, hi = h * HD, h * HD + DH, (h + 1) * HD
            qq_ref[:, lo:hi] = _rope64(q[:, lo:hi], qc, qa, qb).astype(BF16)
            kk_ref[:, lo:mid] = kv[:, lo:mid].astype(BF16)
            kk_ref[:, mid:hi] = kp
            vv_ref[:, lo:mid] = kv[:, mid:hi].astype(BF16)
            vv_ref[:, mid:hi] = ones

    row = lambda w, cb: pl.BlockSpec((tm, w), lambda i: (i, cb))
    full = lambda a: pl.BlockSpec(a.shape, lambda i: (0, 0))
    wide = jax.ShapeDtypeStruct((s, HEADS * HD), BF16)
    return pl.pallas_call(
        body, name="mla_prep", grid=(s // tm,),
        in_specs=[row(Q_RANK, C_Q // Q_RANK), row(KV_RANK, C_KV // KV_RANK), row(DH, K_PE // DH),
                  full(qn_w), full(kvn_w), full(wq), full(wukv), row(HD, 0), row(HD, 0), row(HD, 0),
                  row(DH, 0), row(DH, 0), row(DH, 0)],
        out_specs=[row(Q_RANK, 0), row(1, 0), row(KV_RANK, 0), row(1, 0), row(HEADS * HD, 0), row(HEADS * HD, 0),
                   row(HEADS * HD, 0)],
        out_shape=[jax.ShapeDtypeStruct((s, Q_RANK), BF16), jax.ShapeDtypeStruct((s, 1), F32),
                   jax.ShapeDtypeStruct((s, KV_RANK), BF16), jax.ShapeDtypeStruct((s, 1), F32), wide, wide, wide],
        compiler_params=_cparams(("parallel",)),
    )(proj, proj, proj, qn_w, kvn_w, wq, wukv, *qtab, *ktab)


def _att_mask(qi, ki, t):
    key = ki * t + lax.broadcasted_iota(jnp.int32, (t, t), 0)
    qry = qi * t + lax.broadcasted_iota(jnp.int32, (t, t), 1)
    return (key <= qry) & (key >= PAD_FRONT)


def _silu(z):
    return z * _sigmoid(z)


def _attn_fwd(qq, kk, vv, proj, s):
    t = _row_tile(s, 384)

    def body(q_ref, k_ref, v_ref, z_ref, o_ref, y_ref, lse_ref, acc_ref, m_ref):
        qi = pl.program_id(1)
        q = q_ref[...]
        m_ref[...] = jnp.full(m_ref.shape, NEG_INF, F32)
        acc_ref[...] = jnp.zeros(acc_ref.shape, F32)

        def tile(ki, masked):
            ks = pl.ds(pl.multiple_of(ki * t, t), t)
            st = _dot(k_ref[ks, :], q, "nt")
            if masked:
                st = jnp.where(_att_mask(qi, ki, t), st, NEG_INF)
            m_old = m_ref[...]
            m_new = jnp.maximum(m_old, jnp.max(st, axis=0, keepdims=True))
            pt = jnp.exp(st - m_new).astype(BF16)
            acc_ref[...] = jnp.exp(m_old - m_new) * acc_ref[...] + _dot(v_ref[ks, :], pt, "tn")
            m_ref[...] = m_new

        tile(0, True)

        def inner(ki, carry):
            tile(ki, False)
            return carry

        lax.fori_loop(1, qi, inner, 0)

        @pl.when(qi > 0)
        def _():
            tile(qi, True)

        l = acc_ref[DH:DH + 1, :]
        o = (acc_ref[:DH, :] / l).T
        o_ref[...] = o.astype(BF16)
        y_ref[...] = (o * _silu(z_ref[...].astype(F32))).astype(BF16)
        lse_ref[0, 0] = m_ref[...] + jnp.log(l)

    qtile = pl.BlockSpec((t, DH), lambda h, i: (i, h))
    head = pl.BlockSpec((s, HD), lambda h, i: (0, h))
    return pl.pallas_call(
        body, name="attn_fwd", grid=(HEADS, s // t),
        in_specs=[pl.BlockSpec((t, HD), lambda h, i: (i, h)), head, head,
                  pl.BlockSpec((t, DH), lambda h, i: (i, Z_MLA // DH + h))],
        out_specs=[qtile, qtile, pl.BlockSpec((1, 1, 1, t), lambda h, i: (h, i, 0, 0))],
        out_shape=[jax.ShapeDtypeStruct((s, WIDTH), BF16), jax.ShapeDtypeStruct((s, WIDTH), BF16),
                   jax.ShapeDtypeStruct((HEADS, s // t, 1, t), F32)],
        scratch_shapes=[pltpu.VMEM((HD, t), F32), pltpu.VMEM((1, t), F32)],
        compiler_params=_cparams(("parallel", "arbitrary")),
    )(qq, kk, vv, proj)


def _ret_consts():
    log_g = np.log1p(-(2.0 ** (-5.0 - np.arange(HEADS, dtype=np.float64))))
    n = np.arange(BLK, dtype=np.float64)
    diff = n[:, None] - n[None, :]
    decay = np.where(diff >= 0, np.exp(log_g[:, None, None] * np.maximum(diff, 0.0)), 0.0)
    zeta = np.exp(log_g[:, None] * (BLK - 1.0 - n))[:, :, None]
    xi = np.exp(log_g[:, None] * (n + 1.0))[:, :, None]
    gamma = [float(np.float32(np.exp(g * BLK))) for g in log_g]
    return (jnp.asarray(decay, F32), jnp.asarray(zeta, F32), jnp.asarray(xi, F32)), gamma


def _ret_fwd(proj, gn_w, gn_b, c128, s128, consts, gamma, s):
    nb = s // BLK
    decay, zeta, xi = consts

    def body(rq_ref, rk_ref, rv_ref, z_ref, gw_ref, gb_ref, c_ref, s_ref, dm_ref, zt_ref, xi_ref,
             y_ref, on_ref, rstd_ref, qr_ref, kr_ref, rall_ref, state):
        @pl.when(pl.program_id(0) == 0)
        def _():
            state[...] = jnp.zeros_like(state)

        c, sg = c_ref[...], s_ref[...]
        for h in range(HEADS):
            sl = slice(h * DH, (h + 1) * DH)
            q = _rope128(rq_ref[:, sl].astype(F32), c, sg).astype(BF16)
            kf = _rope128(rk_ref[:, sl].astype(F32), c, sg) * RET_SCALE
            k = kf.astype(BF16)
            v = rv_ref[:, sl]
            qr_ref[:, sl] = q
            kr_ref[:, sl] = k
            r_prev = state[h]
            rall_ref[0, h] = r_prev
            a = _dot(q, k, "nt") * dm_ref[h]
            o = _dot(a.astype(BF16), v, "nn") + _dot(q, r_prev.astype(BF16), "nn") * xi_ref[h]
            state[h] = r_prev * gamma[h] + _dot((kf * zt_ref[h]).astype(BF16), v, "tn")
            mu = jnp.mean(o, axis=-1, keepdims=True)
            var = jnp.mean(jnp.square(o - mu), axis=-1, keepdims=True)
            rstd = lax.rsqrt(var + GN_EPS)
            on = (o - mu) * rstd
            rstd_ref[h] = rstd
            on_ref[:, sl] = on.astype(BF16)
            ogn = on * gw_ref[:, sl] + gb_ref[:, sl]
            y_ref[:, sl] = (ogn * _silu(z_ref[:, sl].astype(F32))).astype(BF16)

    seg = lambda cb: pl.BlockSpec((BLK, WIDTH), lambda i: (i, cb))
    full = lambda a: pl.BlockSpec(a.shape, lambda i: (0,) * a.ndim)
    tab = pl.BlockSpec((BLK, DH), lambda i: (i, 0))
    return pl.pallas_call(
        body, name="ret_fwd", grid=(nb,),
        in_specs=[seg(R_Q // WIDTH), seg(R_K // WIDTH), seg(R_V // WIDTH), seg(Z_RET // WIDTH), full(gn_w), full(gn_b),
                  tab, tab, full(decay), full(zeta), full(xi)],
        out_specs=[seg(0), seg(0), pl.BlockSpec((HEADS, BLK, 1), lambda i: (0, i, 0)), seg(0), seg(0),
                   pl.BlockSpec((1, HEADS, DH, DH), lambda i: (i, 0, 0, 0))],
        out_shape=[jax.ShapeDtypeStruct((s, WIDTH), BF16), jax.ShapeDtypeStruct((s, WIDTH), BF16),
                   jax.ShapeDtypeStruct((HEADS, s, 1), F32), jax.ShapeDtypeStruct((s, WIDTH), BF16),
                   jax.ShapeDtypeStruct((s, WIDTH), BF16), jax.ShapeDtypeStruct((nb, HEADS, DH, DH), F32)],
        scratch_shapes=[pltpu.VMEM((HEADS, DH, DH), F32)],
        compiler_params=_cparams(("arbitrary",)),
    )(proj, proj, proj, proj, gn_w, gn_b, c128, s128, decay, zeta, xi)


def _merge_fwd(y_mla, y_ret, wbm, wbr, proj, s):
    tm, tn = _row_tile(s, 1408), 512

    def body(ym_ref, yr_ref, wm_ref, wr_ref, g0_ref, g1_ref, um_ref, ur_ref, mg_ref):
        um = _dot(ym_ref[...], wm_ref[...], "nt")
        ur = _dot(yr_ref[...], wr_ref[...], "nt")
        um_ref[...] = um.astype(BF16)
        ur_ref[...] = ur.astype(BF16)
        mg_ref[...] = (_sigmoid(g0_ref[...].astype(F32)) * um + _sigmoid(g1_ref[...].astype(F32)) * ur).astype(BF16)

    yspec = pl.BlockSpec((tm, WIDTH), lambda i, j: (i, 0))
    wspec = pl.BlockSpec((tn, WIDTH), lambda i, j: (j, 0))
    ospec = pl.BlockSpec((tm, tn), lambda i, j: (i, j))
    return pl.pallas_call(
        body, name="merge_fwd", grid=(s // tm, D // tn),
        in_specs=[yspec, yspec, wspec, wspec, pl.BlockSpec((tm, tn), lambda i, j: (i, GATE0 // tn + j)),
                  pl.BlockSpec((tm, tn), lambda i, j: (i, GATE1 // tn + j))],
        out_specs=[ospec, ospec, ospec],
        out_shape=[jax.ShapeDtypeStruct((s, D), BF16)] * 3,
        compiler_params=_cparams(("parallel", "parallel")),
    )(y_mla, y_ret, wbm, wbr, proj, proj)


def _out_fwd(merged, big, h, s):
    tm, tn = _row_tile(s, 1408), 512

    def body(m_ref, w_ref, h_ref, o_ref):
        o_ref[...] = h_ref[...] + _dot(m_ref[...], w_ref[...], "nn")

    return pl.pallas_call(
        body, name="out_fwd", grid=(s // tm, D // tn),
        in_specs=[pl.BlockSpec((tm, D), lambda i, j: (i, 0)), pl.BlockSpec((D, tn), lambda i, j: (BIG_OUT // D, j)),
                  pl.BlockSpec((tm, tn), lambda i, j: (i, j))],
        out_specs=pl.BlockSpec((tm, tn), lambda i, j: (i, j)),
        out_shape=jax.ShapeDtypeStruct((s, D), F32),
        compiler_params=_cparams(("parallel", "parallel")),
    )(merged, big, h)


def _loss_bwd(h2, target, fin_w, s):
    nb = s // BLK

    def body(h2_ref, t_ref, w_ref, dh_ref, dhb_ref, loss_ref, dw_ref):
        i = pl.program_id(0)

        @pl.when(i == 0)
        def _():
            dh_ref[...] = jnp.zeros_like(dh_ref)
            dhb_ref[...] = jnp.zeros_like(dhb_ref)
            loss_ref[...] = jnp.zeros_like(loss_ref)
            dw_ref[...] = jnp.zeros_like(dw_ref)

        @pl.when(i > 0)
        def _():
            hv = h2_ref[...]
            w = w_ref[...]
            r = lax.rsqrt(jnp.mean(hv * hv, axis=-1, keepdims=True) + NORM_EPS)
            nrm = hv * r
            e = nrm * w - t_ref[...]
            loss_ref[...] += jnp.full(loss_ref.shape, 0.5 / D, F32) * jnp.sum(e * e)
            dy = e * (1.0 / D)
            dw_ref[...] += jnp.sum(dy * nrm, axis=0, keepdims=True)
            g = dy * w
            dh = r * (g - nrm * jnp.mean(g * nrm, axis=-1, keepdims=True))
            dh_ref[...] = dh
            dhb_ref[...] = dh.astype(BF16)

    blk = pl.BlockSpec((BLK, D), lambda i: (i, 0))
    return pl.pallas_call(
        body, name="loss_bwd", grid=(nb,),
        in_specs=[blk, pl.BlockSpec((BLK, D), lambda i: (jnp.maximum(i - 1, 0), 0)), pl.BlockSpec((1, D), lambda i: (0, 0))],
        out_specs=[blk, blk, pl.BlockSpec((8, 128), lambda i: (0, 0)), pl.BlockSpec((1, D), lambda i: (0, 0))],
        out_shape=[jax.ShapeDtypeStruct((s, D), F32), jax.ShapeDtypeStruct((s, D), BF16),
                   jax.ShapeDtypeStruct((8, 128), F32), jax.ShapeDtypeStruct((1, D), F32)],
        compiler_params=_cparams(("arbitrary",)),
    )(h2, target, fin_w)


def _merge_bwd(dh2b, big, u_mla, u_ret, proj, s):
    tm, tn = _row_tile(s, 1408), 512

    def body(d_ref, w_ref, um_ref, ur_ref, g0_ref, g1_ref, dg0_ref, dg1_ref, dum_ref, dur_ref):
        dm = _dot(d_ref[...], w_ref[...], "nt")
        g0 = _sigmoid(g0_ref[...].astype(F32))
        g1 = _sigmoid(g1_ref[...].astype(F32))
        dg0_ref[...] = (dm * um_ref[...].astype(F32) * g0 * (1.0 - g0)).astype(BF16)
        dg1_ref[...] = (dm * ur_ref[...].astype(F32) * g1 * (1.0 - g1)).astype(BF16)
        dum_ref[...] = (dm * g0).astype(BF16)
        dur_ref[...] = (dm * g1).astype(BF16)

    ospec = pl.BlockSpec((tm, tn), lambda i, j: (i, j))
    return pl.pallas_call(
        body, name="merge_bwd", grid=(s // tm, D // tn),
        in_specs=[pl.BlockSpec((tm, D), lambda i, j: (i, 0)), pl.BlockSpec((tn, D), lambda i, j: (BIG_OUT // tn + j, 0)),
                  ospec, ospec, pl.BlockSpec((tm, tn), lambda i, j: (i, GATE0 // tn + j)),
                  pl.BlockSpec((tm, tn), lambda i, j: (i, GATE1 // tn + j))],
        out_specs=[ospec] * 4,
        out_shape=[jax.ShapeDtypeStruct((s, D), BF16)] * 4,
        compiler_params=_cparams(("parallel", "parallel")),
    )(dh2b, big, u_mla, u_ret, proj, proj)


def _dsilu(z):
    sg = _sigmoid(z)
    return sg * (1.0 + z * (1.0 - sg))


def _branch_mla_bwd(du, wbm, o_mla, proj, s):
    tm = _row_tile(s, 384)

    def body(du_ref, w_ref, o_ref, z_ref, do_ref, dz_ref, delta_ref):
        dy = _dot(du_ref[...], w_ref[...], "nn")
        z = z_ref[...].astype(F32)
        o = o_ref[...].astype(F32)
        do = dy * _silu(z)
        do_ref[...] = do.astype(BF16)
        dz_ref[...] = (dy * o * _dsilu(z)).astype(BF16)
        prod = do * o
        for h in range(HEADS):
            delta_ref[h] = jnp.sum(prod[:, h * DH:(h + 1) * DH], axis=-1, keepdims=True)

    row = lambda w, cb: pl.BlockSpec((tm, w), lambda i: (i, cb))
    return pl.pallas_call(
        body, name="branch_mla_bwd", grid=(s // tm,),
        in_specs=[row(D, 0), pl.BlockSpec((D, WIDTH), lambda i: (0, 0)), row(WIDTH, 0), row(WIDTH, Z_MLA // WIDTH)],
        out_specs=[row(WIDTH, 0), row(WIDTH, 0), pl.BlockSpec((HEADS, tm, 1), lambda i: (0, i, 0))],
        out_shape=[jax.ShapeDtypeStruct((s, WIDTH), BF16)] * 2 + [jax.ShapeDtypeStruct((HEADS, s, 1), F32)],
        compiler_params=_cparams(("parallel",)),
    )(du, wbm, o_mla, proj)


def _branch_ret_bwd(du, wbr, on, rstd, gn_w, gn_b, proj, s):
    tm = BLK

    def body(du_ref, w_ref, on_ref, rstd_ref, gw_ref, gb_ref, z_ref, do_ref, dz_ref, dgw_ref, dgb_ref):
        @pl.when(pl.program_id(0) == 0)
        def _():
            dgw_ref[...] = jnp.zeros_like(dgw_ref)
            dgb_ref[...] = jnp.zeros_like(dgb_ref)

        dy = _dot(du_ref[...], w_ref[...], "nn")
        z = z_ref[...].astype(F32)
        on = on_ref[...].astype(F32)
        gw = gw_ref[...]
        dogn = dy * _silu(z)
        dz_ref[...] = (dy * (on * gw + gb_ref[...]) * _dsilu(z)).astype(BF16)
        dgw_ref[...] += jnp.sum(dogn * on, axis=0, keepdims=True)
        dgb_ref[...] += jnp.sum(dogn, axis=0, keepdims=True)
        don = dogn * gw
        for h in range(HEADS):
            sl = slice(h * DH, (h + 1) * DH)
            dn, nh = don[:, sl], on[:, sl]
            do = rstd_ref[h] * (dn - jnp.mean(dn, axis=-1, keepdims=True)
                                - nh * jnp.mean(dn * nh, axis=-1, keepdims=True))
            do_ref[:, sl] = do.astype(BF16)

    row = lambda w, cb: pl.BlockSpec((tm, w), lambda i: (i, cb))
    vec = pl.BlockSpec((1, WIDTH), lambda i: (0, 0))
    return pl.pallas_call(
        body, name="branch_ret_bwd", grid=(s // tm,),
        in_specs=[row(D, 0), pl.BlockSpec((D, WIDTH), lambda i: (0, 0)), row(WIDTH, 0),
                  pl.BlockSpec((HEADS, tm, 1), lambda i: (0, i, 0)), vec, vec, row(WIDTH, Z_RET // WIDTH)],
        out_specs=[row(WIDTH, 0), row(WIDTH, 0), vec, vec],
        out_shape=[jax.ShapeDtypeStruct((s, WIDTH), BF16)] * 2 + [jax.ShapeDtypeStruct((1, WIDTH), F32)] * 2,
        compiler_params=_cparams(("arbitrary",)),
    )(du, wbr, on, rstd, gn_w, gn_b, proj)


def _ret_bwd(qr, kr, proj, rall, do_ret, c128, s128, consts, gamma, s):
    nb = s // BLK
    decay, zeta, xi = consts

    def body(q_ref, k_ref, v_ref, r_ref, do_ref, c_ref, s_ref, dm_ref, zt_ref, xi_ref, dq_ref, dk_ref, dv_ref, gstate):
        @pl.when(pl.program_id(0) == 0)
        def _():
            gstate[...] = jnp.zeros_like(gstate)

        c, sg = c_ref[...], s_ref[...]
        for h in range(HEADS):
            sl = slice(h * DH, (h + 1) * DH)
            q, k, v, do = q_ref[:, sl], k_ref[:, sl], v_ref[:, sl], do_ref[:, sl]
            dm = dm_ref[h]
            g_next = gstate[h]
            gb = g_next.astype(BF16)
            a = (_dot(q, k, "nt") * dm).astype(BF16)
            da = (_dot(do, v, "nt") * dm).astype(BF16)
            dox = (do.astype(F32) * xi_ref[h]).astype(BF16)
            dq = _dot(da, k, "nn") + _dot(dox, r_ref[0, h].astype(BF16), "nt")
            dk = _dot(da, q, "tn") + _dot(v, gb, "nt") * zt_ref[h]
            kz = (k.astype(F32) * zt_ref[h]).astype(BF16)
            dv = _dot(a, do, "tn") + _dot(kz, gb, "nn")
            gstate[h] = g_next * gamma[h] + _dot(q, dox, "tn")
            dk = dk * RET_SCALE
            dq_ref[:, sl] = _rope128(dq, c, -sg).astype(BF16)
            dk_ref[:, sl] = _rope128(dk, c, -sg).astype(BF16)
            dv_ref[:, sl] = dv.astype(BF16)

    rev = lambda cb: pl.BlockSpec((BLK, WIDTH), lambda i: (nb - 1 - i, cb))
    full = lambda a: pl.BlockSpec(a.shape, lambda i: (0,) * a.ndim)
    tab = pl.BlockSpec((BLK, DH), lambda i: (nb - 1 - i, 0))
    return pl.pallas_call(
        body, name="ret_bwd", grid=(nb,),
        in_specs=[rev(0), rev(0), rev(R_V // WIDTH), pl.BlockSpec((1, HEADS, DH, DH), lambda i: (nb - 1 - i, 0, 0, 0)),
                  rev(0), tab, tab, full(decay), full(zeta), full(xi)],
        out_specs=[rev(0), rev(0), rev(0)],
        out_shape=[jax.ShapeDtypeStruct((s, WIDTH), BF16)] * 3,
        scratch_shapes=[pltpu.VMEM((HEADS, DH, DH), F32)],
        compiler_params=_cparams(("arbitrary",)),
    )(qr, kr, proj, rall, do_ret, c128, s128, decay, zeta, xi)


def _attn_bwd(qq, kk, vv, do, lse, delta, s):
    t = _row_tile(s, 384)
    n = s // t

    def body(q_ref, k_ref, v_ref, do_ref, lse_ref, delta_ref, dq_ref, dk_ref, dv_ref, dk_acc, dv_acc):
        ki = pl.program_id(1)

        @pl.when(ki == 0)
        def _():
            dq_ref[...] = jnp.zeros(dq_ref.shape, F32)

        k, v = k_ref[...], v_ref[...]
        dk_acc[...] = jnp.zeros(dk_acc.shape, F32)
        dv_acc[...] = jnp.zeros(dv_acc.shape, F32)

        def tile(qi, masked):
            qs = pl.ds(pl.multiple_of(qi * t, t), t)
            q, dov = q_ref[qs, :], do_ref[qs, :]
            st = _dot(k, q, "nt")
            if masked:
                st = jnp.where(_att_mask(qi, ki, t), st, NEG_INF)
            pt = jnp.exp(st - lse_ref[0, qi])
            dv_acc[...] += _dot(pt.astype(BF16), dov, "nn")
            dst = (pt * (_dot(v, dov, "nt") - delta_ref[0, qi])).astype(BF16)
            dk_acc[...] += _dot(dst, q, "nn")
            dq_ref[qs, :] += _dot(dst, k, "tn")

        def masked_tile(qi, carry):
            tile(qi, True)
            return carry

        def plain_tile(qi, carry):
            tile(qi, False)
            return carry

        tile(ki, True)

        @pl.when(ki == 0)
        def _():
            lax.fori_loop(1, n, masked_tile, 0)

        @pl.when(ki > 0)
        def _():
            lax.fori_loop(ki + 1, n, plain_tile, 0)

        dk_ref[...] = dk_acc[...].astype(BF16)
        dv_ref[...] = dv_acc[...].astype(BF16)

    head = lambda w: pl.BlockSpec((s, w), lambda h, i: (0, h))
    stat = pl.BlockSpec((1, n, 1, t), lambda h, i: (h, 0, 0, 0))
    return pl.pallas_call(
        body, name="attn_bwd", grid=(HEADS, n),
        in_specs=[head(HD), pl.BlockSpec((t, HD), lambda h, i: (i, h)), pl.BlockSpec((t, DH), lambda h, i: (i, 2 * h)),
                  head(DH), stat, stat],
        out_specs=[head(HD), pl.BlockSpec((t, HD), lambda h, i: (i, h)), pl.BlockSpec((t, DH), lambda h, i: (i, h))],
        out_shape=[jax.ShapeDtypeStruct((s, HEADS * HD), F32), jax.ShapeDtypeStruct((s, HEADS * HD), BF16),
                   jax.ShapeDtypeStruct((s, WIDTH), BF16)],
        scratch_shapes=[pltpu.VMEM((t, HD), F32), pltpu.VMEM((t, DH), F32)],
        compiler_params=_cparams(("parallel", "arbitrary")),
    )(qq, kk, vv, do, lse, delta)


def _rms_bwd(dn, nhat, r, w):
    g = dn * w
    return r * (g - nhat * jnp.mean(g * nhat, axis=-1, keepdims=True)), dn * nhat


def _mla_prep_bwd(dqq, dkk, dvv, proj, rq, rkv, qn_w, kvn_w, wq, wukv, qtab, ktab, s):
    tm = _row_tile(s, 384)

    def body(dqq_ref, dkk_ref, dvv_ref, cq_ref, ckv_ref, rq_ref, rkv_ref, qnw_ref, kvnw_ref, wq_ref, wukv_ref,
             qc_ref, qa_ref, qb_ref, kc_ref, ka_ref, kb_ref,
             dcq_ref, dckv_ref, dkpe_ref, dq_ref, dkv_ref, dqnw_ref, dkvnw_ref):
        @pl.when(pl.program_id(0) == 0)
        def _():
            dqnw_ref[...] = jnp.zeros_like(dqnw_ref)
            dkvnw_ref[...] = jnp.zeros_like(dkvnw_ref)

        qc, qa, qb = qc_ref[...], qa_ref[...], qb_ref[...]
        dkp = jnp.zeros((tm, DH), F32)
        for h in range(HEADS):
            lo, mid, hi = h * HD, h * HD + DH, (h + 1) * HD
            dq_ref[:, lo:hi] = _rope64(dqq_ref[:, lo:hi], qc, -qa, -qb).astype(BF16)
            dkv_ref[:, lo:mid] = dkk_ref[:, lo:mid]
            dkv_ref[:, mid:hi] = dvv_ref[:, h * DH:(h + 1) * DH]
            dkp = dkp + dkk_ref[:, mid:hi].astype(F32)
        dcqn = _dot(dq_ref[...], wq_ref[...], "nn")
        rq_v = rq_ref[...]
        dcq, prod = _rms_bwd(dcqn, cq_ref[...].astype(F32) * rq_v, rq_v, qnw_ref[...])
        dcq_ref[...] = dcq.astype(BF16)
        dqnw_ref[...] += jnp.sum(prod, axis=0, keepdims=True)
        dckvn = _dot(dkv_ref[...], wukv_ref[...], "nn")
        rkv_v = rkv_ref[...]
        dckv, prod = _rms_bwd(dckvn, ckv_ref[...].astype(F32) * rkv_v, rkv_v, kvnw_ref[...])
        dckv_ref[...] = dckv.astype(BF16)
        dkvnw_ref[...] += jnp.sum(prod, axis=0, keepdims=True)
        dkpe_ref[:, :DH] = _rope64(dkp, kc_ref[...], -ka_ref[...], -kb_ref[...]).astype(BF16)
        dkpe_ref[:, DH:] = jnp.zeros((tm, DH), BF16)

    row = lambda w, cb: pl.BlockSpec((tm, w), lambda i: (i, cb))
    full = lambda a: pl.BlockSpec(a.shape, lambda i: (0, 0))
    wide = jax.ShapeDtypeStruct((s, HEADS * HD), BF16)
    return pl.pallas_call(
        body, name="mla_prep_bwd", grid=(s // tm,),
        in_specs=[row(HEADS * HD, 0), row(HEADS * HD, 0), row(WIDTH, 0),
                  row(Q_RANK, C_Q // Q_RANK), row(KV_RANK, C_KV // KV_RANK), row(1, 0), row(1, 0),
                  full(qn_w), full(kvn_w), full(wq), full(wukv), row(HD, 0), row(HD, 0), row(HD, 0),
                  row(DH, 0), row(DH, 0), row(DH, 0)],
        out_specs=[row(Q_RANK, 0), row(KV_RANK, 0), row(2 * DH, 0), row(HEADS * HD, 0), row(HEADS * HD, 0),
                   pl.BlockSpec((1, Q_RANK), lambda i: (0, 0)), pl.BlockSpec((1, KV_RANK), lambda i: (0, 0))],
        out_shape=[jax.ShapeDtypeStruct((s, Q_RANK), BF16), jax.ShapeDtypeStruct((s, KV_RANK), BF16),
                   jax.ShapeDtypeStruct((s, 2 * DH), BF16), wide, wide,
                   jax.ShapeDtypeStruct((1, Q_RANK), F32), jax.ShapeDtypeStruct((1, KV_RANK), F32)],
        compiler_params=_cparams(("arbitrary",)),
    )(dqq, dkk, dvv, proj, proj, rq, rkv, qn_w, kvn_w, wq, wukv, *qtab, *ktab)


def _rms_in_bwd(dxn, h, r, dh2, norm_w, s):
    def body(dxn_ref, h_ref, r_ref, dh2_ref, w_ref, gx_ref, gm_ref, dw_ref):
        i = pl.program_id(0)
        r_v = r_ref[...]
        dx, prod = _rms_bwd(dxn_ref[...], h_ref[...] * r_v, r_v, w_ref[...])
        dh = dh2_ref[...] + dx

        @pl.when(i == 0)
        def _():
            gm_ref[...] = dh
            dw_ref[...] = jnp.sum(prod, axis=0, keepdims=True)

        @pl.when(i > 0)
        def _():
            gx_ref[...] = dh
            dw_ref[...] += jnp.sum(prod, axis=0, keepdims=True)

    blk = pl.BlockSpec((BLK, D), lambda i: (i, 0))
    return pl.pallas_call(
        body, name="rms_in_bwd", grid=(s // BLK,),
        in_specs=[blk, blk, pl.BlockSpec((BLK, 1), lambda i: (i, 0)), blk, pl.BlockSpec((1, D), lambda i: (0, 0))],
        out_specs=[pl.BlockSpec((BLK, D), lambda i: (jnp.maximum(i - 1, 0), 0)), pl.BlockSpec((BLK, D), lambda i: (0, 0)),
                   pl.BlockSpec((1, D), lambda i: (0, 0))],
        out_shape=[jax.ShapeDtypeStruct((s - BLK, D), F32), jax.ShapeDtypeStruct((BLK, D), F32),
                   jax.ShapeDtypeStruct((1, D), F32)],
        compiler_params=_cparams(("arbitrary",)),
    )(dxn, h, r, dh2, norm_w)


def _adam_math(w, g, m, v):
    m = ADAM_B1 * m + (1.0 - ADAM_B1) * g
    v = ADAM_B2 * v + (1.0 - ADAM_B2) * (g * g)
    m_hat = m / (1.0 - ADAM_B1 ** ADAM_STEP)
    v_hat = v / (1.0 - ADAM_B2 ** ADAM_STEP)
    return -ADAM_LR * (m_hat / (jnp.sqrt(v_hat) + ADAM_EPS) + ADAM_WD * w), m, v


def _adamw(w, g, m, v, name):
    rows, cols = w.shape
    tr = rows
    for cand in (128, 64, 32, 16, 8):
        if rows % cand == 0:
            tr = cand
            break

    def body(w_ref, g_ref, m_ref, v_ref, d_ref, nm_ref, nv_ref):
        d_ref[...], nm_ref[...], nv_ref[...] = _adam_math(w_ref[...], g_ref[...], m_ref[...], v_ref[...])

    spec = pl.BlockSpec((tr, cols), lambda i: (i, 0))
    return pl.pallas_call(
        body, name=name, grid=(rows // tr,), in_specs=[spec] * 4, out_specs=[spec] * 3,
        out_shape=[jax.ShapeDtypeStruct((rows, cols), F32)] * 3,
        compiler_params=_cparams(("parallel",)),
    )(w, g, m, v)


def _adamw_small(w, gall, m, v):
    def body(w_ref, g_ref, m_ref, v_ref, gs_ref, d_ref, nm_ref, nv_ref):
        g = g_ref[0]
        for dev in range(1, 8):
            g = g + g_ref[dev]
        gs_ref[...] = g
        d_ref[...], nm_ref[...], nv_ref[...] = _adam_math(w_ref[...], g, m_ref[...], v_ref[...])

    return pl.pallas_call(
        body, name="adamw_small", out_shape=[jax.ShapeDtypeStruct((SM_ROWS, 128), F32)] * 4,
        compiler_params=pltpu.CompilerParams(vmem_limit_bytes=VMEM_LIMIT),
    )(w, gall, m, v)


ADD_ROWS = 464
HALF_BLOCKS = PK_HALF // ADD_ROWS


def _rs_core_add(gpk, land, pos):
    def body(pos_ref, a_ref, b_ref, o_ref):
        o_ref[...] = (a_ref[...].astype(F32) + b_ref[...].astype(F32)).astype(BF16)

    return pl.pallas_call(
        body, name="rs_core_add",
        grid_spec=pltpu.PrefetchScalarGridSpec(
            num_scalar_prefetch=1, grid=(N_CHIPS, HALF_BLOCKS),
            in_specs=[pl.BlockSpec((1, ADD_ROWS, D), lambda s, i, pos_ref: (s, pos_ref[1] * HALF_BLOCKS + i, 0)),
                      pl.BlockSpec((1, ADD_ROWS, D), lambda s, i, pos_ref: (s, i, 0))],
            out_specs=pl.BlockSpec((1, ADD_ROWS, D), lambda s, i, pos_ref: (s, i, 0))),
        out_shape=jax.ShapeDtypeStruct((N_CHIPS, PK_HALF, D), BF16),
        compiler_params=_cparams(("parallel", "parallel")),
    )(pos, gpk, land)


def _rs_chip_add(cp, land, pos):
    def body(pos_ref, a_ref, l_ref, o_ref):
        o_ref[0] = ((a_ref[0].astype(F32) + l_ref[0].astype(F32)) + l_ref[1].astype(F32)) + l_ref[2].astype(F32)

    return pl.pallas_call(
        body, name="rs_chip_add",
        grid_spec=pltpu.PrefetchScalarGridSpec(
            num_scalar_prefetch=1, grid=(HALF_BLOCKS,),
            in_specs=[pl.BlockSpec((1, ADD_ROWS, D), lambda i, pos_ref: (pos_ref[0], i, 0)),
                      pl.BlockSpec((3, ADD_ROWS, D), lambda i, pos_ref: (0, i, 0))],
            out_specs=pl.BlockSpec((1, ADD_ROWS, D), lambda i, pos_ref: (pos_ref[1], i, 0))),
        out_shape=jax.ShapeDtypeStruct((2, PK_HALF, D), F32),
        compiler_params=_cparams(("parallel",)),
    )(pos, cp, land)


ANY = pl.BlockSpec(memory_space=pl.ANY)


def _mesh_pos():
    return lax.axis_index("x"), lax.axis_index("y"), lax.axis_index("c")


def _chip_peer(x, y, c, r):
    return (jnp.bitwise_xor(x, r >> 1), jnp.bitwise_xor(y, r & 1), c)


def _allgather(big0, pack, meta_loc):
    def body(big0_ref, pack_ref, meta_ref, big_ref, metaf_ref, lsem, ssem, rsem, fssem, frsem, msem_s, msem_r):
        del big0_ref
        x, y, c = _mesh_pos()
        j = 2 * x + y
        me, sibling = (x, y, c), (x, y, 1 - c)

        def half_wait(s_sem, r_sem):
            rows = big_ref.at[pl.ds(0, PK_HALF), :]
            return pltpu.make_async_remote_copy(src_ref=rows, dst_ref=rows, send_sem=s_sem, recv_sem=r_sem,
                                                device_id=me, device_id_type=MESH)

        own_meta = pltpu.make_async_copy(meta_ref, metaf_ref.at[j], lsem)
        own_meta.start()
        for s in range(N_CHIPS):
            for hf in range(2):
                @pl.when((j == s) & (c == hf))
                def _():
                    for r in (1, 2, 3):
                        for p, b, n in _half_pieces(s, hf):
                            pltpu.make_async_remote_copy(
                                src_ref=pack_ref.at[pl.ds(p, n), :], dst_ref=big_ref.at[pl.ds(b, n), :],
                                send_sem=ssem.at[r - 1], recv_sem=rsem.at[r - 1],
                                device_id=_chip_peer(x, y, c, r), device_id_type=MESH).start()

        meta_copies = [pltpu.make_async_remote_copy(
            src_ref=meta_ref, dst_ref=metaf_ref.at[j], send_sem=msem_s.at[r - 1], recv_sem=msem_r.at[r - 1],
            device_id=_chip_peer(x, y, c, r), device_id_type=MESH) for r in (1, 2, 3)]
        for cp in meta_copies:
            cp.start()
        for r in (1, 2, 3):
            half_wait(ssem.at[r - 1], rsem.at[r - 1]).wait_recv()
            src_shard = jnp.bitwise_xor(j, r)
            for s in range(N_CHIPS):
                for hf in range(2):
                    @pl.when((src_shard == s) & (c == hf))
                    def _():
                        for p, b, n in _half_pieces(s, hf):
                            rows = big_ref.at[pl.ds(b, n), :]
                            pltpu.make_async_remote_copy(
                                src_ref=rows, dst_ref=rows, send_sem=fssem.at[r - 1], recv_sem=frsem.at[r - 1],
                                device_id=sibling, device_id_type=MESH).start()
        for r in (1, 2, 3):
            half_wait(fssem.at[r - 1], frsem.at[r - 1]).wait_recv()
        for r in (1, 2, 3):
            half_wait(ssem.at[r - 1], rsem.at[r - 1]).wait_send()
            half_wait(fssem.at[r - 1], frsem.at[r - 1]).wait_send()
        for cp in meta_copies:
            cp.wait_send()
            cp.wait_recv()
        own_meta.wait()

    dma3 = pltpu.SemaphoreType.DMA((3,))
    return pl.pallas_call(
        body, name="allgather_weights", in_specs=[ANY, ANY, ANY], out_specs=[ANY, ANY],
        out_shape=[jax.ShapeDtypeStruct((BIG_ROWS, D), BF16), jax.ShapeDtypeStruct((N_CHIPS,) + meta_loc.shape, F32)],
        scratch_shapes=[pltpu.SemaphoreType.DMA(()), dma3, dma3, dma3, dma3, dma3, dma3],
        input_output_aliases={0: 0},
        compiler_params=pltpu.CompilerParams(has_side_effects=True),
    )(big0, pack, meta_loc)


RS_SPLIT = 2


def _rs_core_exchange(gpk):
    n = PK_HALF // RS_SPLIT

    def body(g_ref, land_ref, ssem, rsem):
        x, y, c = _mesh_pos()
        me, sibling = (x, y, c), (x, y, 1 - c)
        for s in range(N_CHIPS):
            for q in range(RS_SPLIT):
                src = pl.ds(pl.multiple_of((1 - c) * PK_HALF + q * n, 16), n)
                pltpu.make_async_remote_copy(
                    src_ref=g_ref.at[s, src, :], dst_ref=land_ref.at[s, pl.ds(q * n, n), :],
                    send_sem=ssem, recv_sem=rsem, device_id=sibling, device_id_type=MESH).start()
        whole = pltpu.make_async_remote_copy(src_ref=land_ref, dst_ref=land_ref, send_sem=ssem, recv_sem=rsem,
                                             device_id=me, device_id_type=MESH)
        whole.wait_recv()
        whole.wait_send()

    return pl.pallas_call(
        body, name="rs_core_exchange", in_specs=[ANY], out_specs=ANY,
        out_shape=jax.ShapeDtypeStruct((N_CHIPS, PK_HALF, D), BF16),
        scratch_shapes=[pltpu.SemaphoreType.DMA(())] * 2,
        compiler_params=pltpu.CompilerParams(has_side_effects=True),
    )(gpk)


def _rs_chip_exchange(cp):
    def body(cp_ref, land_ref, ssem, rsem):
        x, y, c = _mesh_pos()
        j = 2 * x + y
        copies = [pltpu.make_async_remote_copy(
            src_ref=cp_ref.at[jnp.bitwise_xor(j, r)], dst_ref=land_ref.at[r - 1], send_sem=ssem.at[r - 1],
            recv_sem=rsem.at[r - 1], device_id=_chip_peer(x, y, c, r), device_id_type=MESH) for r in (1, 2, 3)]
        for cpy in copies:
            cpy.start()
        for cpy in copies:
            cpy.wait_recv()
        for cpy in copies:
            cpy.wait_send()

    dma3 = pltpu.SemaphoreType.DMA((3,))
    return pl.pallas_call(
        body, name="rs_chip_exchange", in_specs=[ANY], out_specs=ANY,
        out_shape=jax.ShapeDtypeStruct((3, PK_HALF, D), BF16), scratch_shapes=[dma3, dma3],
        compiler_params=pltpu.CompilerParams(has_side_effects=True),
    )(cp)


def _rs_finish(full, small):
    n = PK_HALF // RS_SPLIT

    def body(full_in_ref, sm_ref, full_ref, all_ref, lsem, ssem, rsem, sm_s, sm_r):
        del full_in_ref
        x, y, c = _mesh_pos()
        me, sibling = (x, y, c), (x, y, 1 - c)
        my_id = 4 * x + 2 * y + c
        own_sm = pltpu.make_async_copy(sm_ref, all_ref.at[my_id], lsem)
        own_sm.start()
        for q in range(RS_SPLIT):
            rows = full_ref.at[c, pl.ds(q * n, n), :]
            pltpu.make_async_remote_copy(src_ref=rows, dst_ref=rows, send_sem=ssem, recv_sem=rsem,
                                         device_id=sibling, device_id_type=MESH).start()
        half = pltpu.make_async_remote_copy(src_ref=full_ref.at[c], dst_ref=full_ref.at[c], send_sem=ssem,
                                            recv_sem=rsem, device_id=me, device_id_type=MESH)
        smalls = [pltpu.make_async_remote_copy(
            src_ref=sm_ref, dst_ref=all_ref.at[my_id], send_sem=sm_s.at[r - 1], recv_sem=sm_r.at[r - 1],
            device_id=(jnp.bitwise_xor(x, r >> 2), jnp.bitwise_xor(y, (r >> 1) & 1), jnp.bitwise_xor(c, r & 1)),
            device_id_type=MESH) for r in range(1, 8)]
        for cpy in smalls:
            cpy.start()
        half.wait_recv()
        half.wait_send()
        for cpy in smalls:
            cpy.wait_recv()
        for cpy in smalls:
            cpy.wait_send()
        own_sm.wait()

    dma7 = pltpu.SemaphoreType.DMA((7,))
    return pl.pallas_call(
        body, name="rs_finish", in_specs=[ANY, ANY], out_specs=[ANY, ANY],
        out_shape=[jax.ShapeDtypeStruct((2, PK_HALF, D), F32), jax.ShapeDtypeStruct((8, SM_ROWS, 128), F32)],
        scratch_shapes=[pltpu.SemaphoreType.DMA(()), pltpu.SemaphoreType.DMA(()), pltpu.SemaphoreType.DMA(()), dma7, dma7],
        input_output_aliases={0: 0},
        compiler_params=pltpu.CompilerParams(has_side_effects=True),
    )(full, small)


def _rope_tables(s):
    pos = jnp.arange(s, dtype=F32) - PAD_FRONT

    def cs(d):
        inv = ROPE_BASE ** (-jnp.arange(0, d, 2, dtype=F32) / d)
        ang = pos[:, None] * inv[None, :]
        return jnp.cos(ang), jnp.sin(ang)

    c, sn = cs(ROPE_D)
    z = jnp.zeros_like(c)
    ktab = (jnp.concatenate([c, c, z, z], axis=1), jnp.concatenate([-sn, z, z, z], axis=1),
            jnp.concatenate([z, sn, z, z], axis=1))
    one = jnp.ones_like(c)
    qtab = tuple(ATT_SCALE * jnp.concatenate(parts, axis=1) for parts in (
        [one, one, one, one, c, c, z, z], [z, z, z, z, -sn, z, z, z], [z, z, z, z, z, sn, z, z]))
    c, sn = cs(DH)
    return qtab, ktab, jnp.concatenate([c, c], axis=1), jnp.concatenate([-sn, sn], axis=1)


def _grad_mm(a, b, m, n, s, name):
    return _matmul(a, b, "tn", m, n, s, min(m, 512), min(n, 1024), s, BF16, name, n_outer=True)


def _local_step(x2, target2, hp0, big, small_w):
    norm_w, qn_w, kvn_w, gn_w, gn_b, fin_w = small_w
    s = x2.shape[0] + BLK
    tm = _row_tile(s, 1408)
    uq = big[BIG_UQ:BIG_UQ + 384].reshape(HEADS, DH + ROPE_D, Q_RANK)
    wq = jnp.pad(uq, ((0, 0), (0, HD - DH - ROPE_D), (0, 0))).reshape(HEADS * HD, Q_RANK)
    wukv = big[BIG_UKV:BIG_UKV + 256].reshape(2 * WIDTH, KV_RANK)
    wbm = big[BIG_BM:BIG_BM + 1024].reshape(D, WIDTH)
    wbr = big[BIG_BR:BIG_BR + 1024].reshape(D, WIDTH)
    qtab, ktab, c128, s128 = _rope_tables(s)
    consts, gamma = _ret_consts()

    h, xn, r1 = _rms_in(x2, hp0, norm_w, s)
    proj = _matmul(xn, big, "nt", s, PROJ_W, D, tm, 512, D, BF16, "proj", b_off=(BIG_IN // 512, 0))
    cqn, rq, ckvn, rkv, qq, kk, vv = _mla_prep(proj, qn_w, kvn_w, wq, wukv, qtab, ktab, s)
    o_mla, y_mla, lse = _attn_fwd(qq, kk, vv, proj, s)
    y_ret, on, rstd, qr, kr, rall = _ret_fwd(proj, gn_w, gn_b, c128, s128, consts, gamma, s)
    u_mla, u_ret, merged = _merge_fwd(y_mla, y_ret, wbm, wbr, proj, s)
    h2 = _out_fwd(merged, big, h, s)
    dh2, dh2b, loss_blk, dfin = _loss_bwd(h2, target2, fin_w, s)

    dg0, dg1, du_mla, du_ret = _merge_bwd(dh2b, big, u_mla, u_ret, proj, s)
    g_out = _grad_mm(merged, dh2b, D, D, s, "grad_w_out")
    do_mla, dz_mla, delta = _branch_mla_bwd(du_mla, wbm, o_mla, proj, s)
    g_bm = _grad_mm(du_mla, y_mla, D, WIDTH, s, "grad_w_branch_mla")
    do_ret, dz_ret, dgw, dgb = _branch_ret_bwd(du_ret, wbr, on, rstd, gn_w, gn_b, proj, s)
    g_br = _grad_mm(du_ret, y_ret, D, WIDTH, s, "grad_w_branch_ret")
    dr_q, dr_k, dr_v = _ret_bwd(qr, kr, proj, rall, do_ret, c128, s128, consts, gamma, s)
    dqq, dkk, dvv = _attn_bwd(qq, kk, vv, do_mla, lse, delta.reshape(lse.shape), s)
    dcq, dckv, dkpe, dq, dkv, dqnw, dkvnw = _mla_prep_bwd(dqq, dkk, dvv, proj, rq, rkv, qn_w, kvn_w, wq, wukv,
                                                          qtab, ktab, s)
    g_q = _grad_mm(dq, cqn, HEADS * HD, Q_RANK, s, "grad_w_uq")
    g_ukv = _grad_mm(dkv, ckvn, 2 * WIDTH, KV_RANK, s, "grad_w_ukv")
    dproj = jnp.concatenate([dz_mla, dr_q, dr_k, dr_v, dz_ret, dg0, dg1, dcq, dckv, dkpe], axis=1)
    dxn = _matmul(dproj, big, "nn", s, D, PROJ_W, tm, 1024, 1024, F32, "dxn", b_off=(BIG_IN // 1024, 0))
    grad_x, gmeta_blk, dnorm = _rms_in_bwd(dxn, h, r1, dh2, norm_w, s)
    g_in = _grad_mm(dproj, xn, PROJ_W, D, s, "grad_w_in")

    g_uq = g_q.reshape(HEADS, HD, Q_RANK)[:, :DH + ROPE_D]
    parts = [(g_out, 512), (g_bm.reshape(1024, D), 256), (g_br.reshape(1024, D), 256), (g_uq.reshape(384, D), 96),
             (g_ukv.reshape(256, D), 64)]

    def shard_rows(sh):
        out = [p[n * sh:n * (sh + 1)] for p, n in parts]
        if sh == 0:
            out += [g_in[C_Q:C_Q + ROT], g_in[:IN_SHARD - ROT]]
        else:
            out += [g_in[IN_SHARD * sh - ROT:IN_SHARD * (sh + 1) - ROT]]
        return jnp.concatenate(out + [jnp.zeros((PK_ROWS - PK_PAD, D), BF16)], axis=0)

    gbig = jnp.stack([shard_rows(sh) for sh in range(N_CHIPS)])
    rows = lambda a: a.reshape(-1, 128)
    small = jnp.concatenate([loss_blk[0:1], rows(dnorm), rows(dqnw), rows(dkvnw), rows(dgw), rows(dgb), rows(dfin),
                             rows(gmeta_blk[PAD_FRONT:]), jnp.zeros((1, 128), F32)], axis=0)
    return grad_x, gbig, small


def _small_rows(ws):
    rows = lambda a: a.reshape(-1, 128)
    return jnp.concatenate([jnp.zeros((1, 128), F32)] + [rows(w) for w in ws]
                           + [jnp.zeros((SM_ROWS - SM_META, 128), F32)], axis=0)


def kernel(x, meta, norm_w, w_in, mla_q_norm_w, mla_w_uq, mla_kv_norm_w, mla_w_ukv, ret_gn_w, ret_gn_b, w_branch_mla, w_branch_ret, w_out, final_norm_w, loss_target, m_meta, m_norm_w, m_w_in, m_mla_q_norm_w, m_mla_w_uq, m_mla_kv_norm_w, m_mla_w_ukv, m_ret_gn_w, m_ret_gn_b, m_w_branch_mla, m_w_branch_ret, m_w_out, m_final_norm_w, v_meta, v_norm_w, v_w_in, v_mla_q_norm_w, v_mla_w_uq, v_mla_kv_norm_w, v_mla_w_ukv, v_ret_gn_w, v_ret_gn_b, v_w_branch_mla, v_w_branch_ret, v_w_out, v_final_norm_w):
    j = 2 * lax.axis_index("x") + lax.axis_index("y")
    tr = lambda w: w[0].T.reshape(-1, D).astype(BF16)
    pack = jnp.concatenate([w_out[0].astype(BF16), tr(w_branch_mla), tr(w_branch_ret), tr(mla_w_uq), tr(mla_w_ukv),
                            tr(w_in), jnp.zeros((PK_ROWS - PK_PAD, D), BF16)], axis=0)
    pos = jnp.stack([j, lax.axis_index("c")]).astype(jnp.int32)
    put = lambda buf, lo, hi, off: lax.dynamic_update_slice(buf, pack[lo:hi], (off, 0))
    big0 = lax.empty((BIG_ROWS, D), BF16)
    big0 = put(big0, PK_OUT, PK_BM, BIG_OUT + 512 * j)
    big0 = put(big0, PK_BM, PK_BR, BIG_BM + 256 * j)
    big0 = put(big0, PK_BR, PK_UQ, BIG_BR + 256 * j)
    big0 = put(big0, PK_UQ, PK_UKV, BIG_UQ + 96 * j)
    big0 = put(big0, PK_UKV, PK_IN, BIG_UKV + 64 * j)
    big0 = put(big0, PK_IN, PK_IN + ROT, BIG_IN + jnp.where(j == 0, C_Q, IN_SHARD * j - ROT))
    big0 = put(big0, PK_IN + ROT, PK_PAD, BIG_IN + IN_SHARD * j)
    big0 = lax.dynamic_update_slice(big0, jnp.zeros((ZERO_ROWS, D), BF16), (BIG_IN + IN_WIDTH, 0))
    big, meta_all = _allgather(big0, pack, meta)
    meta_full = meta_all.transpose(1, 0, 2).reshape(N_META, D)
    hp0 = jnp.concatenate([jnp.zeros((PAD_FRONT, D), F32), meta_full], axis=0)
    small_w = (norm_w, mla_q_norm_w, mla_kv_norm_w, ret_gn_w, ret_gn_b, final_norm_w.reshape(1, D))

    grad_x, gbig, small = _local_step(x[0], loss_target[0], hp0, big, small_w)

    land = _rs_core_exchange(gbig)
    chip_part = _rs_core_add(gbig, land, pos)
    land2 = _rs_chip_exchange(chip_part)
    full, small_all = _rs_finish(_rs_chip_add(chip_part, land2, pos), small)
    full = full.reshape(PK_ROWS, D)

    untr = lambda lo, hi, rows: full[lo:hi].reshape(rows, -1).T
    grads = {
        "w_out": full[PK_OUT:PK_BM], "w_branch_mla": untr(PK_BM, PK_BR, 512), "w_branch_ret": untr(PK_BR, PK_UQ, 512),
        "mla_w_uq": untr(PK_UQ, PK_UKV, 384), "mla_w_ukv": untr(PK_UKV, PK_IN, 512), "w_in": untr(PK_IN, PK_PAD, IN_SHARD),
    }
    big_w = {"w_in": (w_in, m_w_in, v_w_in), "mla_w_uq": (mla_w_uq, m_mla_w_uq, v_mla_w_uq),
             "mla_w_ukv": (mla_w_ukv, m_mla_w_ukv, v_mla_w_ukv),
             "w_branch_mla": (w_branch_mla, m_w_branch_mla, v_w_branch_mla),
             "w_branch_ret": (w_branch_ret, m_w_branch_ret, v_w_branch_ret), "w_out": (w_out, m_w_out, v_w_out)}
    res = {}
    for name, (w, m, v) in big_w.items():
        d, nm, nv = _adamw(w[0], grads[name], m[0], v[0], "adamw_" + name)
        res[name] = (grads[name][None], d[None], nm[None], nv[None])

    small_m = (m_norm_w, m_mla_q_norm_w, m_mla_kv_norm_w, m_ret_gn_w, m_ret_gn_b, m_final_norm_w.reshape(1, D))
    small_v = (v_norm_w, v_mla_q_norm_w, v_mla_kv_norm_w, v_ret_gn_w, v_ret_gn_b, v_final_norm_w.reshape(1, D))
    gs, ds, ms, vs = _adamw_small(_small_rows(small_w), small_all, _small_rows(small_m), _small_rows(small_v))
    names = ["norm_w", "mla_q_norm_w", "mla_kv_norm_w", "ret_gn_w", "ret_gn_b", "final_norm_w"]
    bounds = [SM_NORM, SM_QN, SM_KVN, SM_GNW, SM_GNB, SM_FIN, SM_META]
    for k, name in enumerate(names):
        shape = (D,) if name == "final_norm_w" else (1, -1)
        res[name] = tuple(a[bounds[k]:bounds[k + 1]].reshape(shape) for a in (gs, ds, ms, vs))
    g_meta = lax.dynamic_slice_in_dim(gs[SM_META:SM_META + 256].reshape(N_META, D), j * (D // N_CHIPS), D // N_CHIPS, axis=1)
    res["meta"] = (g_meta,) + tuple(_adamw(meta, g_meta, m_meta, v_meta, "adamw_meta"))

    order = ["meta", "norm_w", "w_in", "mla_q_norm_w", "mla_w_uq", "mla_kv_norm_w", "mla_w_ukv", "ret_gn_w", "ret_gn_b",
             "w_branch_mla", "w_branch_ret", "w_out", "final_norm_w"]
    return (gs[0, 0], grad_x[None]) + tuple(res[n][k] for k in range(4) for n in order)
```

```python
import functools
import math

import numpy as np
import jax
import jax.numpy as jnp
from jax import lax
from jax.experimental import pallas as pl
from jax.experimental.pallas import tpu as pltpu

F32 = jnp.float32
BF16 = jnp.bfloat16
MESH = pl.DeviceIdType.MESH

D = 2048
N_META = 16
BLK = 128
PAD_FRONT = BLK - N_META
HEADS = 8
DH = 128
ROPE_D = 64
Q_RANK = 512
KV_RANK = 256
WIDTH = HEADS * DH
ROPE_BASE = 10000.0
NORM_EPS = 1e-6
GN_EPS = 1e-5
NEG_INF = -1e30
ATT_SCALE = (DH + ROPE_D) ** -0.5
RET_SCALE = DH ** -0.5
IN_WIDTH = 10048
N_CHIPS = 4
IN_SHARD = IN_WIDTH // N_CHIPS
ADAM_LR, ADAM_B1, ADAM_B2, ADAM_EPS, ADAM_WD, ADAM_STEP = 0.001, 0.9, 0.999, 1e-08, 0.01, 10

R_Q, R_K, R_V, Z_RET, Z_MLA, GATE0, GATE1 = 0, 1024, 2048, 3072, 4096, 5120, 5120 + D
C_Q = 5120 + 2 * D
C_KV = C_Q + Q_RANK
K_PE = C_KV + KV_RANK
PROJ_W = 10240
IN_RUNS = ((0, 832, C_Q), (832, 1856, Z_MLA), (1856, 4928, R_Q), (4928, 5952, Z_RET), (5952, IN_WIDTH, GATE0))

BIG_OUT, BIG_BM, BIG_BR, BIG_UQ, BIG_UKV, BIG_JUNK, BIG_IN = 0, 2048, 3072, 4096, 4480, 4736, 5120
BIG_ROWS = BIG_IN + PROJ_W
ZERO_ROWS = PROJ_W - IN_WIDTH
PK_OUT, PK_BM, PK_BR, PK_UQ, PK_UKV, PK_IN, PK_PAD, PK_ROWS = 0, 512, 768, 1024, 1120, 1184, 3696, 3712
PK_HALF = PK_ROWS // 2
SM_LOSS, SM_NORM, SM_QN, SM_KVN, SM_GNW, SM_GNB, SM_FIN, SM_META, SM_ROWS = 0, 8, 24, 32, 40, 48, 56, 72, 328

VMEM_LIMIT = 56 * 1024 * 1024


def _pieces(shard):
    j = shard
    out = [(PK_OUT, BIG_OUT + 512 * j, 512), (PK_BM, BIG_BM + 256 * j, 256), (PK_BR, BIG_BR + 256 * j, 256),
           (PK_UQ, BIG_UQ + 96 * j, 96), (PK_UKV, BIG_UKV + 64 * j, 64)]
    for lo, hi, new in IN_RUNS:
        a, b = max(lo, IN_SHARD * j), min(hi, IN_SHARD * (j + 1))
        if b > a:
            out.append((PK_IN + a - IN_SHARD * j, BIG_IN + new + a - lo, b - a))
    out += [(PK_PAD, BIG_JUNK + 16 * j, 16)]
    return out


def _in_cuts():
    cuts = sorted({0, IN_SHARD} | {b - IN_SHARD * j for j in range(N_CHIPS) for lo, hi, _ in IN_RUNS for b in (lo, hi)
                                   if 0 < b - IN_SHARD * j < IN_SHARD})

    def new_row(col):
        return next(BIG_IN + new + col - lo for lo, hi, new in IN_RUNS if lo <= col < hi)

    return cuts, [[new_row(IN_SHARD * j + c) for c in cuts[:-1]] for j in range(N_CHIPS)]


def _half_pieces(shard, half):
    lo, hi = half * PK_HALF, (half + 1) * PK_HALF
    out = []
    for p, b, n in _pieces(shard):
        s, e = max(p, lo), min(p + n, hi)
        if e > s:
            out.append((s, b + (s - p), e - s))
    return out


def _row_tile(rows, cap):
    best = BLK
    for t in range(BLK, cap + 1, BLK):
        if rows % t == 0:
            best = t
    return best


def _cparams(sem):
    return pltpu.CompilerParams(dimension_semantics=sem, vmem_limit_bytes=VMEM_LIMIT)


def _dot(a, b, form):
    dn = {"nt": (((1,), (1,)), ((), ())), "nn": (((1,), (0,)), ((), ())), "tn": (((0,), (0,)), ((), ()))}[form]
    return lax.dot_general(a, b, dn, preferred_element_type=F32)


def _sigmoid(v):
    return 1.0 / (1.0 + jnp.exp(-v))


def _matmul(a, b, form, m, n, k, tm, tn, tk, out_dtype, name, a_off=(0, 0), b_off=(0, 0), n_outer=False):
    nk = k // tk
    gi, gj = m // tm, n // tn

    def ij(g0, g1):
        return (g1, g0) if n_outer else (g0, g1)

    if form == "nt":
        a_spec = pl.BlockSpec((tm, tk), lambda g0, g1, kk: (ij(g0, g1)[0] + a_off[0], kk + a_off[1]))
        b_spec = pl.BlockSpec((tn, tk), lambda g0, g1, kk: (ij(g0, g1)[1] + b_off[0], kk + b_off[1]))
    elif form == "nn":
        a_spec = pl.BlockSpec((tm, tk), lambda g0, g1, kk: (ij(g0, g1)[0] + a_off[0], kk + a_off[1]))
        b_spec = pl.BlockSpec((tk, tn), lambda g0, g1, kk: (kk + b_off[0], ij(g0, g1)[1] + b_off[1]))
    else:
        a_spec = pl.BlockSpec((tk, tm), lambda g0, g1, kk: (kk + a_off[0], ij(g0, g1)[0] + a_off[1]))
        b_spec = pl.BlockSpec((tk, tn), lambda g0, g1, kk: (kk + b_off[0], ij(g0, g1)[1] + b_off[1]))
    o_spec = pl.BlockSpec((tm, tn), lambda g0, g1, kk: ij(g0, g1))

    def body(a_ref, b_ref, o_ref, *acc):
        p = _dot(a_ref[...], b_ref[...], form)
        if nk == 1:
            o_ref[...] = p.astype(o_ref.dtype)
        else:
            acc_ref, = acc
            kk = pl.program_id(2)

            @pl.when(kk == 0)
            def _():
                acc_ref[...] = p

            @pl.when(kk > 0)
            def _():
                acc_ref[...] += p

            @pl.when(kk == nk - 1)
            def _():
                o_ref[...] = acc_ref[...].astype(o_ref.dtype)

    return pl.pallas_call(
        body, name=name, grid=(gj, gi, nk) if n_outer else (gi, gj, nk), in_specs=[a_spec, b_spec], out_specs=o_spec,
        out_shape=jax.ShapeDtypeStruct((m, n), out_dtype),
        scratch_shapes=[] if nk == 1 else [pltpu.VMEM((tm, tn), F32)],
        compiler_params=_cparams(("parallel", "parallel", "arbitrary")),
    )(a, b)


def _rms_in(x, hp0, norm_w, s):
    def body(x_ref, hp0_ref, w_ref, h_ref, xn_ref, r_ref):
        def run(hv):
            r = lax.rsqrt(jnp.mean(hv * hv, axis=-1, keepdims=True) + NORM_EPS)
            h_ref[...] = hv
            xn_ref[...] = (hv * r * w_ref[...]).astype(BF16)
            r_ref[...] = r

        @pl.when(pl.program_id(0) == 0)
        def _():
            run(hp0_ref[...])

        @pl.when(pl.program_id(0) > 0)
        def _():
            run(x_ref[...])

    return pl.pallas_call(
        body, name="rms_in", grid=(s // BLK,),
        in_specs=[pl.BlockSpec((BLK, D), lambda i: (jnp.maximum(i - 1, 0), 0)),
                  pl.BlockSpec((BLK, D), lambda i: (0, 0)), pl.BlockSpec((1, D), lambda i: (0, 0))],
        out_specs=[pl.BlockSpec((BLK, D), lambda i: (i, 0)), pl.BlockSpec((BLK, D), lambda i: (i, 0)),
                   pl.BlockSpec((BLK, 1), lambda i: (i, 0))],
        out_shape=[jax.ShapeDtypeStruct((s, D), F32), jax.ShapeDtypeStruct((s, D), BF16),
                   jax.ShapeDtypeStruct((s, 1), F32)],
        compiler_params=_cparams(("arbitrary",)),
    )(x, hp0, norm_w)


def _rope64(t, c, sa, sb):
    return t * c + pltpu.roll(t, t.shape[1] - ROPE_D // 2, 1) * sa + pltpu.roll(t, ROPE_D // 2, 1) * sb


def _rope128(t, c, sg):
    return t * c + pltpu.roll(t, DH // 2, 1) * sg


HD = 2 * DH


def _mla_prep(proj, qn_w, kvn_w, wq, wukv, qtab, ktab, s):
    tm = _row_tile(s, 384)

    def body(cq_ref, ckv_ref, kpe_ref, qnw_ref, kvnw_ref, wq_ref, wukv_ref, qc_ref, qa_ref, qb_ref,
             kc_ref, ka_ref, kb_ref, cqn_ref, rq_ref, ckvn_ref, rkv_ref, qq_ref, kk_ref, vv_ref):
        cq = cq_ref[...].astype(F32)
        rq = lax.rsqrt(jnp.mean(cq * cq, axis=-1, keepdims=True) + NORM_EPS)
        cqn = (cq * rq * qnw_ref[...]).astype(BF16)
        ckv = ckv_ref[...].astype(F32)
        rkv = lax.rsqrt(jnp.mean(ckv * ckv, axis=-1, keepdims=True) + NORM_EPS)
        ckvn = (ckv * rkv * kvnw_ref[...]).astype(BF16)
        cqn_ref[...] = cqn
        rq_ref[...] = rq
        ckvn_ref[...] = ckvn
        rkv_ref[...] = rkv
        q = _dot(cqn, wq_ref[...], "nt")
        kv = _dot(ckvn, wukv_ref[...], "nt")
        kp = _rope64(kpe_ref[...].astype(F32), kc_ref[...], ka_ref[...], kb_ref[...]).astype(BF16)
        qc, qa, qb = qc_ref[...], qa_ref[...], qb_ref[...]
        ones = jnp.ones((tm, DH), BF16)
        for h in range(HEADS):
            lo, mid, hi = h * HD, h * HD + DH, (h + 1) * HD
            qq_ref[:, lo:hi] = _rope64(q[:, lo:hi], qc, qa, qb).astype(BF16)
            kk_ref[:, lo:mid] = kv[:, lo:mid].astype(BF16)
            kk_ref[:, mid:hi] = kp
            vv_ref[:, lo:mid] = kv[:, mid:hi].astype(BF16)
            vv_ref[:, mid:hi] = ones

    row = lambda w, cb: pl.BlockSpec((tm, w), lambda i: (i, cb))
    full = lambda a: pl.BlockSpec(a.shape, lambda i: (0, 0))
    wide = jax.ShapeDtypeStruct((s, HEADS * HD), BF16)
    return pl.pallas_call(
        body, name="mla_prep", grid=(s // tm,),
        in_specs=[row(Q_RANK, C_Q // Q_RANK), row(KV_RANK, C_KV // KV_RANK), row(DH, K_PE // DH),
                  full(qn_w), full(kvn_w), full(wq), full(wukv), row(HD, 0), row(HD, 0), row(HD, 0),
                  row(DH, 0), row(DH, 0), row(DH, 0)],
        out_specs=[row(Q_RANK, 0), row(1, 0), row(KV_RANK, 0), row(1, 0), row(HEADS * HD, 0), row(HEADS * HD, 0),
                   row(HEADS * HD, 0)],
        out_shape=[jax.ShapeDtypeStruct((s, Q_RANK), BF16), jax.ShapeDtypeStruct((s, 1), F32),
                   jax.ShapeDtypeStruct((s, KV_RANK), BF16), jax.ShapeDtypeStruct((s, 1), F32), wide, wide, wide],
        compiler_params=_cparams(("parallel",)),
    )(proj, proj, proj, qn_w, kvn_w, wq, wukv, *qtab, *ktab)


def _att_mask(qi, ki, t):
    key = ki * t + lax.broadcasted_iota(jnp.int32, (t, t), 0)
    qry = qi * t + lax.broadcasted_iota(jnp.int32, (t, t), 1)
    return (key <= qry) & (key >= PAD_FRONT)


def _silu(z):
    return z * _sigmoid(z)


def _attn_fwd(qq, kk, vv, proj, s):
    t = _row_tile(s, 384)

    def body(q_ref, k_ref, v_ref, z_ref, o_ref, y_ref, lse_ref, acc_ref, m_ref):
        qi = pl.program_id(1)
        q = q_ref[...]
        m_ref[...] = jnp.full(m_ref.shape, NEG_INF, F32)
        acc_ref[...] = jnp.zeros(acc_ref.shape, F32)

        def tile(ki, masked):
            ks = pl.ds(pl.multiple_of(ki * t, t), t)
            st = _dot(k_ref[ks, :], q, "nt")
            if masked:
                st = jnp.where(_att_mask(qi, ki, t), st, NEG_INF)
            m_old = m_ref[...]
            m_new = jnp.maximum(m_old, jnp.max(st, axis=0, keepdims=True))
            pt = jnp.exp(st - m_new).astype(BF16)
            acc_ref[...] = jnp.exp(m_old - m_new) * acc_ref[...] + _dot(v_ref[ks, :], pt, "tn")
            m_ref[...] = m_new

        tile(0, True)

        def inner(ki, carry):
            tile(ki, False)
            return carry

        lax.fori_loop(1, qi, inner, 0)

        @pl.when(qi > 0)
        def _():
            tile(qi, True)

        l = acc_ref[DH:DH + 1, :]
        o = (acc_ref[:DH, :] / l).T
        o_ref[...] = o.astype(BF16)
        y_ref[...] = (o * _silu(z_ref[...].astype(F32))).astype(BF16)
        lse_ref[0, 0] = m_ref[...] + jnp.log(l)

    qtile = pl.BlockSpec((t, DH), lambda h, i: (i, h))
    head = pl.BlockSpec((s, HD), lambda h, i: (0, h))
    return pl.pallas_call(
        body, name="attn_fwd", grid=(HEADS, s // t),
        in_specs=[pl.BlockSpec((t, HD), lambda h, i: (i, h)), head, head,
                  pl.BlockSpec((t, DH), lambda h, i: (i, Z_MLA // DH + h))],
        out_specs=[qtile, qtile, pl.BlockSpec((1, 1, 1, t), lambda h, i: (h, i, 0, 0))],
        out_shape=[jax.ShapeDtypeStruct((s, WIDTH), BF16), jax.ShapeDtypeStruct((s, WIDTH), BF16),
                   jax.ShapeDtypeStruct((HEADS, s // t, 1, t), F32)],
        scratch_shapes=[pltpu.VMEM((HD, t), F32), pltpu.VMEM((1, t), F32)],
        compiler_params=_cparams(("parallel", "arbitrary")),
    )(qq, kk, vv, proj)


def _ret_consts():
    log_g = np.log1p(-(2.0 ** (-5.0 - np.arange(HEADS, dtype=np.float64))))
    n = np.arange(BLK, dtype=np.float64)
    diff = n[:, None] - n[None, :]
    decay = np.where(diff >= 0, np.exp(log_g[:, None, None] * np.maximum(diff, 0.0)), 0.0)
    zeta = np.exp(log_g[:, None] * (BLK - 1.0 - n))[:, :, None]
    xi = np.exp(log_g[:, None] * (n + 1.0))[:, :, None]
    gamma = [float(np.float32(np.exp(g * BLK))) for g in log_g]
    return (jnp.asarray(decay, F32), jnp.asarray(zeta, F32), jnp.asarray(xi, F32)), gamma


def _ret_fwd(proj, gn_w, gn_b, c128, s128, consts, gamma, s):
    nb = s // BLK
    decay, zeta, xi = consts

    def body(rq_ref, rk_ref, rv_ref, z_ref, gw_ref, gb_ref, c_ref, s_ref, dm_ref, zt_ref, xi_ref,
             y_ref, on_ref, rstd_ref, qr_ref, kr_ref, rall_ref, state):
        @pl.when(pl.program_id(0) == 0)
        def _():
            state[...] = jnp.zeros_like(state)

        c, sg = c_ref[...], s_ref[...]
        for h in range(HEADS):
            sl = slice(h * DH, (h + 1) * DH)
            q = _rope128(rq_ref[:, sl].astype(F32), c, sg).astype(BF16)
            kf = _rope128(rk_ref[:, sl].astype(F32), c, sg) * RET_SCALE
            k = kf.astype(BF16)
            v = rv_ref[:, sl]
            qr_ref[:, sl] = q
            kr_ref[:, sl] = k
            r_prev = state[h]
            rall_ref[0, h] = r_prev
            a = _dot(q, k, "nt") * dm_ref[h]
            o = _dot(a.astype(BF16), v, "nn") + _dot(q, r_prev.astype(BF16), "nn") * xi_ref[h]
            state[h] = r_prev * gamma[h] + _dot((kf * zt_ref[h]).astype(BF16), v, "tn")
            mu = jnp.mean(o, axis=-1, keepdims=True)
            var = jnp.mean(jnp.square(o - mu), axis=-1, keepdims=True)
            rstd = lax.rsqrt(var + GN_EPS)
            on = (o - mu) * rstd
            rstd_ref[h] = rstd
            on_ref[:, sl] = on.astype(BF16)
            ogn = on * gw_ref[:, sl] + gb_ref[:, sl]
            y_ref[:, sl] = (ogn * _silu(z_ref[:, sl].astype(F32))).astype(BF16)

    seg = lambda cb: pl.BlockSpec((BLK, WIDTH), lambda i: (i, cb))
    full = lambda a: pl.BlockSpec(a.shape, lambda i: (0,) * a.ndim)
    tab = pl.BlockSpec((BLK, DH), lambda i: (i, 0))
    return pl.pallas_call(
        body, name="ret_fwd", grid=(nb,),
        in_specs=[seg(R_Q // WIDTH), seg(R_K // WIDTH), seg(R_V // WIDTH), seg(Z_RET // WIDTH), full(gn_w), full(gn_b),
                  tab, tab, full(decay), full(zeta), full(xi)],
        out_specs=[seg(0), seg(0), pl.BlockSpec((HEADS, BLK, 1), lambda i: (0, i, 0)), seg(0), seg(0),
                   pl.BlockSpec((1, HEADS, DH, DH), lambda i: (i, 0, 0, 0))],
        out_shape=[jax.ShapeDtypeStruct((s, WIDTH), BF16), jax.ShapeDtypeStruct((s, WIDTH), BF16),
                   jax.ShapeDtypeStruct((HEADS, s, 1), F32), jax.ShapeDtypeStruct((s, WIDTH), BF16),
                   jax.ShapeDtypeStruct((s, WIDTH), BF16), jax.ShapeDtypeStruct((nb, HEADS, DH, DH), F32)],
        scratch_shapes=[pltpu.VMEM((HEADS, DH, DH), F32)],
        compiler_params=_cparams(("arbitrary",)),
    )(proj, proj, proj, proj, gn_w, gn_b, c128, s128, decay, zeta, xi)


def _merge_fwd(y_mla, y_ret, wbm, wbr, proj, s):
    tm, tn = _row_tile(s, 1408), 512

    def body(ym_ref, yr_ref, wm_ref, wr_ref, g0_ref, g1_ref, um_ref, ur_ref, mg_ref):
        um = _dot(ym_ref[...], wm_ref[...], "nt")
        ur = _dot(yr_ref[...], wr_ref[...], "nt")
        um_ref[...] = um.astype(BF16)
        ur_ref[...] = ur.astype(BF16)
        mg_ref[...] = (_sigmoid(g0_ref[...].astype(F32)) * um + _sigmoid(g1_ref[...].astype(F32)) * ur).astype(BF16)

    yspec = pl.BlockSpec((tm, WIDTH), lambda i, j: (i, 0))
    wspec = pl.BlockSpec((tn, WIDTH), lambda i, j: (j, 0))
    ospec = pl.BlockSpec((tm, tn), lambda i, j: (i, j))
    return pl.pallas_call(
        body, name="merge_fwd", grid=(s // tm, D // tn),
        in_specs=[yspec, yspec, wspec, wspec, pl.BlockSpec((tm, tn), lambda i, j: (i, GATE0 // tn + j)),
                  pl.BlockSpec((tm, tn), lambda i, j: (i, GATE1 // tn + j))],
        out_specs=[ospec, ospec, ospec],
        out_shape=[jax.ShapeDtypeStruct((s, D), BF16)] * 3,
        compiler_params=_cparams(("parallel", "parallel")),
    )(y_mla, y_ret, wbm, wbr, proj, proj)


def _out_fwd(merged, big, h, s):
    tm, tn = _row_tile(s, 1408), 512

    def body(m_ref, w_ref, h_ref, o_ref):
        o_ref[...] = h_ref[...] + _dot(m_ref[...], w_ref[...], "nn")

    return pl.pallas_call(
        body, name="out_fwd", grid=(s // tm, D // tn),
        in_specs=[pl.BlockSpec((tm, D), lambda i, j: (i, 0)), pl.BlockSpec((D, tn), lambda i, j: (BIG_OUT // D, j)),
                  pl.BlockSpec((tm, tn), lambda i, j: (i, j))],
        out_specs=pl.BlockSpec((tm, tn), lambda i, j: (i, j)),
        out_shape=jax.ShapeDtypeStruct((s, D), F32),
        compiler_params=_cparams(("parallel", "parallel")),
    )(merged, big, h)


def _loss_bwd(h2, target, fin_w, s):
    nb = s // BLK

    def body(h2_ref, t_ref, w_ref, dh_ref, dhb_ref, loss_ref, dw_ref):
        i = pl.program_id(0)

        @pl.when(i == 0)
        def _():
            dh_ref[...] = jnp.zeros_like(dh_ref)
            dhb_ref[...] = jnp.zeros_like(dhb_ref)
            loss_ref[...] = jnp.zeros_like(loss_ref)
            dw_ref[...] = jnp.zeros_like(dw_ref)

        @pl.when(i > 0)
        def _():
            hv = h2_ref[...]
            w = w_ref[...]
            r = lax.rsqrt(jnp.mean(hv * hv, axis=-1, keepdims=True) + NORM_EPS)
            nrm = hv * r
            e = nrm * w - t_ref[...]
            loss_ref[...] += jnp.full(loss_ref.shape, 0.5 / D, F32) * jnp.sum(e * e)
            dy = e * (1.0 / D)
            dw_ref[...] += jnp.sum(dy * nrm, axis=0, keepdims=True)
            g = dy * w
            dh = r * (g - nrm * jnp.mean(g * nrm, axis=-1, keepdims=True))
            dh_ref[...] = dh
            dhb_ref[...] = dh.astype(BF16)

    blk = pl.BlockSpec((BLK, D), lambda i: (i, 0))
    return pl.pallas_call(
        body, name="loss_bwd", grid=(nb,),
        in_specs=[blk, pl.BlockSpec((BLK, D), lambda i: (jnp.maximum(i - 1, 0), 0)), pl.BlockSpec((1, D), lambda i: (0, 0))],
        out_specs=[blk, blk, pl.BlockSpec((8, 128), lambda i: (0, 0)), pl.BlockSpec((1, D), lambda i: (0, 0))],
        out_shape=[jax.ShapeDtypeStruct((s, D), F32), jax.ShapeDtypeStruct((s, D), BF16),
                   jax.ShapeDtypeStruct((8, 128), F32), jax.ShapeDtypeStruct((1, D), F32)],
        compiler_params=_cparams(("arbitrary",)),
    )(h2, target, fin_w)


def _merge_bwd(dh2b, big, u_mla, u_ret, proj, s):
    tm, tn = _row_tile(s, 1408), 512

    def body(d_ref, w_ref, um_ref, ur_ref, gate_ref, dproj_ref, dum_ref, dur_ref, dm_ref):
        branch = pl.program_id(2)

        @pl.when(branch == 0)
        def _():
            dm_ref[...] = _dot(d_ref[...], w_ref[...], "nt")

        dm = dm_ref[...]
        gt = _sigmoid(gate_ref[...].astype(F32))

        @pl.when(branch == 0)
        def _():
            dproj_ref[...] = (dm * um_ref[...].astype(F32) * gt * (1.0 - gt)).astype(BF16)
            dum_ref[...] = (dm * gt).astype(BF16)

        @pl.when(branch == 1)
        def _():
            dproj_ref[...] = (dm * ur_ref[...].astype(F32) * gt * (1.0 - gt)).astype(BF16)
            dur_ref[...] = (dm * gt).astype(BF16)

    ospec = pl.BlockSpec((tm, tn), lambda i, j, b: (i, j))
    gate = pl.BlockSpec((tm, tn), lambda i, j, b: (i, GATE0 // tn + b * (D // tn) + j))
    return pl.pallas_call(
        body, name="merge_bwd", grid=(s // tm, D // tn, 2),
        in_specs=[pl.BlockSpec((tm, D), lambda i, j, b: (i, 0)),
                  pl.BlockSpec((tn, D), lambda i, j, b: (BIG_OUT // tn + j, 0)), ospec, ospec, gate],
        out_specs=[gate, ospec, ospec],
        out_shape=[jax.ShapeDtypeStruct((s, PROJ_W), BF16), jax.ShapeDtypeStruct((s, D), BF16),
                   jax.ShapeDtypeStruct((s, D), BF16)],
        scratch_shapes=[pltpu.VMEM((tm, tn), F32)],
        compiler_params=_cparams(("parallel", "parallel", "arbitrary")),
    )(dh2b, big, u_mla, u_ret, proj)


def _dsilu(z):
    sg = _sigmoid(z)
    return sg * (1.0 + z * (1.0 - sg))


def _branch_mla_bwd(dproj, du, wbm, o_mla, proj, s):
    tm = _row_tile(s, 384)

    def body(dproj_in, du_ref, w_ref, o_ref, z_ref, dz_ref, do_ref, delta_ref):
        del dproj_in
        dy = _dot(du_ref[...], w_ref[...], "nn")
        z = z_ref[...].astype(F32)
        o = o_ref[...].astype(F32)
        do = dy * _silu(z)
        do_ref[...] = do.astype(BF16)
        dz_ref[...] = (dy * o * _dsilu(z)).astype(BF16)
        prod = do * o
        for h in range(HEADS):
            delta_ref[h] = jnp.sum(prod[:, h * DH:(h + 1) * DH], axis=-1, keepdims=True)

    row = lambda w, cb: pl.BlockSpec((tm, w), lambda i: (i, cb))
    return pl.pallas_call(
        body, name="branch_mla_bwd", grid=(s // tm,),
        in_specs=[ANY, row(D, 0), pl.BlockSpec((D, WIDTH), lambda i: (0, 0)), row(WIDTH, 0),
                  row(WIDTH, Z_MLA // WIDTH)],
        out_specs=[row(WIDTH, Z_MLA // WIDTH), row(WIDTH, 0), pl.BlockSpec((HEADS, tm, 1), lambda i: (0, i, 0))],
        out_shape=[jax.ShapeDtypeStruct((s, PROJ_W), BF16), jax.ShapeDtypeStruct((s, WIDTH), BF16),
                   jax.ShapeDtypeStruct((HEADS, s, 1), F32)],
        input_output_aliases={0: 0},
        compiler_params=_cparams(("parallel",)),
    )(dproj, du, wbm, o_mla, proj)


def _branch_ret_bwd(dproj, du, wbr, on, rstd, gn_w, gn_b, proj, s):
    tm = BLK

    def body(dproj_in, du_ref, w_ref, on_ref, rstd_ref, gw_ref, gb_ref, z_ref, dz_ref, do_ref, dgw_ref, dgb_ref):
        del dproj_in

        @pl.when(pl.program_id(0) == 0)
        def _():
            dgw_ref[...] = jnp.zeros_like(dgw_ref)
            dgb_ref[...] = jnp.zeros_like(dgb_ref)

        dy = _dot(du_ref[...], w_ref[...], "nn")
        z = z_ref[...].astype(F32)
        on = on_ref[...].astype(F32)
        gw = gw_ref[...]
        dogn = dy * _silu(z)
        dz_ref[...] = (dy * (on * gw + gb_ref[...]) * _dsilu(z)).astype(BF16)
        dgw_ref[...] += jnp.sum(dogn * on, axis=0, keepdims=True)
        dgb_ref[...] += jnp.sum(dogn, axis=0, keepdims=True)
        don = dogn * gw
        for h in range(HEADS):
            sl = slice(h * DH, (h + 1) * DH)
            dn, nh = don[:, sl], on[:, sl]
            do = rstd_ref[h] * (dn - jnp.mean(dn, axis=-1, keepdims=True)
                                - nh * jnp.mean(dn * nh, axis=-1, keepdims=True))
            do_ref[:, sl] = do.astype(BF16)

    row = lambda w, cb: pl.BlockSpec((tm, w), lambda i: (i, cb))
    vec = pl.BlockSpec((1, WIDTH), lambda i: (0, 0))
    return pl.pallas_call(
        body, name="branch_ret_bwd", grid=(s // tm,),
        in_specs=[ANY, row(D, 0), pl.BlockSpec((D, WIDTH), lambda i: (0, 0)), row(WIDTH, 0),
                  pl.BlockSpec((HEADS, tm, 1), lambda i: (0, i, 0)), vec, vec, row(WIDTH, Z_RET // WIDTH)],
        out_specs=[row(WIDTH, Z_RET // WIDTH), row(WIDTH, 0), vec, vec],
        out_shape=[jax.ShapeDtypeStruct((s, PROJ_W), BF16), jax.ShapeDtypeStruct((s, WIDTH), BF16)]
        + [jax.ShapeDtypeStruct((1, WIDTH), F32)] * 2,
        input_output_aliases={0: 0},
        compiler_params=_cparams(("arbitrary",)),
    )(dproj, du, wbr, on, rstd, gn_w, gn_b, proj)


def _ret_bwd(dproj, qr, kr, proj, rall, do_ret, c128, s128, consts, gamma, s):
    nb = s // BLK
    decay, zeta, xi = consts

    def body(dproj_in, q_ref, k_ref, v_ref, r_ref, do_ref, c_ref, s_ref, dm_ref, zt_ref, xi_ref, out_ref, gstate):
        del dproj_in
        dq_ref, dk_ref, dv_ref = (out_ref.at[:, pl.ds(off, WIDTH)] for off in (R_Q, R_K, R_V))

        @pl.when(pl.program_id(0) == 0)
        def _():
            gstate[...] = jnp.zeros_like(gstate)

        c, sg = c_ref[...], s_ref[...]
        for h in range(HEADS):
            sl = slice(h * DH, (h + 1) * DH)
            q, k, v, do = q_ref[:, sl], k_ref[:, sl], v_ref[:, sl], do_ref[:, sl]
            dm = dm_ref[h]
            g_next = gstate[h]
            gb = g_next.astype(BF16)
            a = (_dot(q, k, "nt") * dm).astype(BF16)
            da = (_dot(do, v, "nt") * dm).astype(BF16)
            dox = (do.astype(F32) * xi_ref[h]).astype(BF16)
            dq = _dot(da, k, "nn") + _dot(dox, r_ref[0, h].astype(BF16), "nt")
            dk = _dot(da, q, "tn") + _dot(v, gb, "nt") * zt_ref[h]
            kz = (k.astype(F32) * zt_ref[h]).astype(BF16)
            dv = _dot(a, do, "tn") + _dot(kz, gb, "nn")
            gstate[h] = g_next * gamma[h] + _dot(q, dox, "tn")
            dk = dk * RET_SCALE
            dq_ref[:, sl] = _rope128(dq, c, -sg).astype(BF16)
            dk_ref[:, sl] = _rope128(dk, c, -sg).astype(BF16)
            dv_ref[:, sl] = dv.astype(BF16)

    rev = lambda cb: pl.BlockSpec((BLK, WIDTH), lambda i: (nb - 1 - i, cb))
    full = lambda a: pl.BlockSpec(a.shape, lambda i: (0,) * a.ndim)
    tab = pl.BlockSpec((BLK, DH), lambda i: (nb - 1 - i, 0))
    return pl.pallas_call(
        body, name="ret_bwd", grid=(nb,),
        in_specs=[ANY, rev(0), rev(0), rev(R_V // WIDTH),
                  pl.BlockSpec((1, HEADS, DH, DH), lambda i: (nb - 1 - i, 0, 0, 0)),
                  rev(0), tab, tab, full(decay), full(zeta), full(xi)],
        out_specs=pl.BlockSpec((BLK, 3 * WIDTH), lambda i: (nb - 1 - i, R_Q // (3 * WIDTH))),
        out_shape=jax.ShapeDtypeStruct((s, PROJ_W), BF16),
        scratch_shapes=[pltpu.VMEM((HEADS, DH, DH), F32)],
        input_output_aliases={0: 0},
        compiler_params=_cparams(("arbitrary",)),
    )(dproj, qr, kr, proj, rall, do_ret, c128, s128, decay, zeta, xi)


def _attn_bwd(qq, kk, vv, do, lse, delta, s):
    t = _row_tile(s, 384)
    n = s // t

    def body(q_ref, k_ref, v_ref, do_ref, lse_ref, delta_ref, dq_ref, dk_ref, dv_ref, dk_acc, dv_acc):
        ki = pl.program_id(1)

        @pl.when(ki == 0)
        def _():
            dq_ref[...] = jnp.zeros(dq_ref.shape, F32)

        k, v = k_ref[...], v_ref[...]
        dk_acc[...] = jnp.zeros(dk_acc.shape, F32)
        dv_acc[...] = jnp.zeros(dv_acc.shape, F32)

        def tile(qi, masked):
            qs = pl.ds(pl.multiple_of(qi * t, t), t)
            q, dov = q_ref[qs, :], do_ref[qs, :]
            st = _dot(k, q, "nt")
            if masked:
                st = jnp.where(_att_mask(qi, ki, t), st, NEG_INF)
            pt = jnp.exp(st - lse_ref[0, qi])
            dv_acc[...] += _dot(pt.astype(BF16), dov, "nn")
            dst = (pt * (_dot(v, dov, "nt") - delta_ref[0, qi])).astype(BF16)
            dk_acc[...] += _dot(dst, q, "nn")
            dq_ref[qs, :] += _dot(dst, k, "tn")

        def masked_tile(qi, carry):
            tile(qi, True)
            return carry

        def plain_tile(qi, carry):
            tile(qi, False)
            return carry

        tile(ki, True)

        @pl.when(ki == 0)
        def _():
            lax.fori_loop(1, n, masked_tile, 0)

        @pl.when(ki > 0)
        def _():
            lax.fori_loop(ki + 1, n, plain_tile, 0)

        dk_ref[...] = dk_acc[...].astype(BF16)
        dv_ref[...] = dv_acc[...].astype(BF16)

    head = lambda w: pl.BlockSpec((s, w), lambda h, i: (0, h))
    stat = pl.BlockSpec((1, n, 1, t), lambda h, i: (h, 0, 0, 0))
    return pl.pallas_call(
        body, name="attn_bwd", grid=(HEADS, n),
        in_specs=[head(HD), pl.BlockSpec((t, HD), lambda h, i: (i, h)), pl.BlockSpec((t, DH), lambda h, i: (i, 2 * h)),
                  head(DH), stat, stat],
        out_specs=[head(HD), pl.BlockSpec((t, HD), lambda h, i: (i, h)), pl.BlockSpec((t, DH), lambda h, i: (i, h))],
        out_shape=[jax.ShapeDtypeStruct((s, HEADS * HD), F32), jax.ShapeDtypeStruct((s, HEADS * HD), BF16),
                   jax.ShapeDtypeStruct((s, WIDTH), BF16)],
        scratch_shapes=[pltpu.VMEM((t, HD), F32), pltpu.VMEM((t, DH), F32)],
        compiler_params=_cparams(("parallel", "arbitrary")),
    )(qq, kk, vv, do, lse, delta)


def _rms_bwd(dn, nhat, r, w):
    g = dn * w
    return r * (g - nhat * jnp.mean(g * nhat, axis=-1, keepdims=True)), dn * nhat


def _mla_prep_bwd(dproj, dqq, dkk, dvv, proj, rq, rkv, qn_w, kvn_w, wq, wukv, qtab, ktab, s):
    tm = _row_tile(s, 384)
    tail = PROJ_W - C_Q

    def body(dproj_in, dqq_ref, dkk_ref, dvv_ref, cq_ref, ckv_ref, rq_ref, rkv_ref, qnw_ref, kvnw_ref, wq_ref,
             wukv_ref, qc_ref, qa_ref, qb_ref, kc_ref, ka_ref, kb_ref,
             tail_ref, dq_ref, dkv_ref, dqnw_ref, dkvnw_ref):
        del dproj_in
        dcq_ref = tail_ref.at[:, pl.ds(0, Q_RANK)]
        dckv_ref = tail_ref.at[:, pl.ds(C_KV - C_Q, KV_RANK)]
        dkpe_ref = tail_ref.at[:, pl.ds(K_PE - C_Q, 2 * DH)]

        @pl.when(pl.program_id(0) == 0)
        def _():
            dqnw_ref[...] = jnp.zeros_like(dqnw_ref)
            dkvnw_ref[...] = jnp.zeros_like(dkvnw_ref)

        qc, qa, qb = qc_ref[...], qa_ref[...], qb_ref[...]
        dkp = jnp.zeros((tm, DH), F32)
        for h in range(HEADS):
            lo, mid, hi = h * HD, h * HD + DH, (h + 1) * HD
            dq_ref[:, lo:hi] = _rope64(dqq_ref[:, lo:hi], qc, -qa, -qb).astype(BF16)
            dkv_ref[:, lo:mid] = dkk_ref[:, lo:mid]
            dkv_ref[:, mid:hi] = dvv_ref[:, h * DH:(h + 1) * DH]
            dkp = dkp + dkk_ref[:, mid:hi].astype(F32)
        dcqn = _dot(dq_ref[...], wq_ref[...], "nn")
        rq_v = rq_ref[...]
        dcq, prod = _rms_bwd(dcqn, cq_ref[...].astype(F32) * rq_v, rq_v, qnw_ref[...])
        dcq_ref[...] = dcq.astype(BF16)
        dqnw_ref[...] += jnp.sum(prod, axis=0, keepdims=True)
        dckvn = _dot(dkv_ref[...], wukv_ref[...], "nn")
        rkv_v = rkv_ref[...]
        dckv, prod = _rms_bwd(dckvn, ckv_ref[...].astype(F32) * rkv_v, rkv_v, kvnw_ref[...])
        dckv_ref[...] = dckv.astype(BF16)
        dkvnw_ref[...] += jnp.sum(prod, axis=0, keepdims=True)
        dkpe_ref[:, :DH] = _rope64(dkp, kc_ref[...], -ka_ref[...], -kb_ref[...]).astype(BF16)
        dkpe_ref[:, DH:] = jnp.zeros((tm, DH), BF16)

    row = lambda w, cb: pl.BlockSpec((tm, w), lambda i: (i, cb))
    full = lambda a: pl.BlockSpec(a.shape, lambda i: (0, 0))
    wide = jax.ShapeDtypeStruct((s, HEADS * HD), BF16)
    return pl.pallas_call(
        body, name="mla_prep_bwd", grid=(s // tm,),
        in_specs=[ANY, row(HEADS * HD, 0), row(HEADS * HD, 0), row(WIDTH, 0),
                  row(Q_RANK, C_Q // Q_RANK), row(KV_RANK, C_KV // KV_RANK), row(1, 0), row(1, 0),
                  full(qn_w), full(kvn_w), full(wq), full(wukv), row(HD, 0), row(HD, 0), row(HD, 0),
                  row(DH, 0), row(DH, 0), row(DH, 0)],
        out_specs=[row(tail, C_Q // tail), row(HEADS * HD, 0), row(HEADS * HD, 0),
                   pl.BlockSpec((1, Q_RANK), lambda i: (0, 0)), pl.BlockSpec((1, KV_RANK), lambda i: (0, 0))],
        out_shape=[jax.ShapeDtypeStruct((s, PROJ_W), BF16), wide, wide,
                   jax.ShapeDtypeStruct((1, Q_RANK), F32), jax.ShapeDtypeStruct((1, KV_RANK), F32)],
        input_output_aliases={0: 0},
        compiler_params=_cparams(("arbitrary",)),
    )(dproj, dqq, dkk, dvv, proj, proj, rq, rkv, qn_w, kvn_w, wq, wukv, *qtab, *ktab)


def _rms_in_bwd(dxn, h, r, dh2, norm_w, s):
    def body(dxn_ref, h_ref, r_ref, dh2_ref, w_ref, gx_ref, gm_ref, dw_ref):
        i = pl.program_id(0)
        r_v = r_ref[...]
        dx, prod = _rms_bwd(dxn_ref[...], h_ref[...] * r_v, r_v, w_ref[...])
        dh = dh2_ref[...] + dx

        @pl.when(i == 0)
        def _():
            gm_ref[...] = dh
            dw_ref[...] = jnp.sum(prod, axis=0, keepdims=True)

        @pl.when(i > 0)
        def _():
            gx_ref[...] = dh
            dw_ref[...] += jnp.sum(prod, axis=0, keepdims=True)

    blk = pl.BlockSpec((BLK, D), lambda i: (i, 0))
    return pl.pallas_call(
        body, name="rms_in_bwd", grid=(s // BLK,),
        in_specs=[blk, blk, pl.BlockSpec((BLK, 1), lambda i: (i, 0)), blk, pl.BlockSpec((1, D), lambda i: (0, 0))],
        out_specs=[pl.BlockSpec((BLK, D), lambda i: (jnp.maximum(i - 1, 0), 0)), pl.BlockSpec((BLK, D), lambda i: (0, 0)),
                   pl.BlockSpec((1, D), lambda i: (0, 0))],
        out_shape=[jax.ShapeDtypeStruct((s - BLK, D), F32), jax.ShapeDtypeStruct((BLK, D), F32),
                   jax.ShapeDtypeStruct((1, D), F32)],
        compiler_params=_cparams(("arbitrary",)),
    )(dxn, h, r, dh2, norm_w)


def _adam_math(w, g, m, v):
    m = ADAM_B1 * m + (1.0 - ADAM_B1) * g
    v = ADAM_B2 * v + (1.0 - ADAM_B2) * (g * g)
    m_hat = m / (1.0 - ADAM_B1 ** ADAM_STEP)
    v_hat = v / (1.0 - ADAM_B2 ** ADAM_STEP)
    return -ADAM_LR * (m_hat / (jnp.sqrt(v_hat) + ADAM_EPS) + ADAM_WD * w), m, v


def _adamw(w, g, m, v, name):
    rows, cols = w.shape
    tr = rows
    for cand in (128, 64, 32, 16, 8):
        if rows % cand == 0:
            tr = cand
            break

    def body(w_ref, g_ref, m_ref, v_ref, d_ref, nm_ref, nv_ref):
        d_ref[...], nm_ref[...], nv_ref[...] = _adam_math(w_ref[...], g_ref[...], m_ref[...], v_ref[...])

    spec = pl.BlockSpec((tr, cols), lambda i: (i, 0))
    return pl.pallas_call(
        body, name=name, grid=(rows // tr,), in_specs=[spec] * 4, out_specs=[spec] * 3,
        out_shape=[jax.ShapeDtypeStruct((rows, cols), F32)] * 3,
        compiler_params=_cparams(("parallel",)),
    )(w, g, m, v)


def _adamw_small(w, gall, m, v):
    def body(w_ref, g_ref, m_ref, v_ref, gs_ref, d_ref, nm_ref, nv_ref):
        g = g_ref[0]
        for dev in range(1, 8):
            g = g + g_ref[dev]
        gs_ref[...] = g
        d_ref[...], nm_ref[...], nv_ref[...] = _adam_math(w_ref[...], g, m_ref[...], v_ref[...])

    return pl.pallas_call(
        body, name="adamw_small", out_shape=[jax.ShapeDtypeStruct((SM_ROWS, 128), F32)] * 4,
        compiler_params=pltpu.CompilerParams(vmem_limit_bytes=VMEM_LIMIT),
    )(w, gall, m, v)


ADD_ROWS = 464
HALF_BLOCKS = PK_HALF // ADD_ROWS


def _rs_core_add(gpk, land, pos):
    def body(pos_ref, a_ref, b_ref, o_ref):
        o_ref[...] = (a_ref[...].astype(F32) + b_ref[...].astype(F32)).astype(BF16)

    return pl.pallas_call(
        body, name="rs_core_add",
        grid_spec=pltpu.PrefetchScalarGridSpec(
            num_scalar_prefetch=1, grid=(N_CHIPS, HALF_BLOCKS),
            in_specs=[pl.BlockSpec((1, ADD_ROWS, D), lambda s, i, pos_ref: (s, pos_ref[1] * HALF_BLOCKS + i, 0)),
                      pl.BlockSpec((1, ADD_ROWS, D), lambda s, i, pos_ref: (s, i, 0))],
            out_specs=pl.BlockSpec((1, ADD_ROWS, D), lambda s, i, pos_ref: (s, i, 0))),
        out_shape=jax.ShapeDtypeStruct((N_CHIPS, PK_HALF, D), BF16),
        compiler_params=_cparams(("parallel", "parallel")),
    )(pos, gpk, land)


def _rs_chip_add(cp, land, pos):
    def body(pos_ref, a_ref, l_ref, o_ref):
        o_ref[0] = ((a_ref[0].astype(F32) + l_ref[0].astype(F32)) + l_ref[1].astype(F32)) + l_ref[2].astype(F32)

    return pl.pallas_call(
        body, name="rs_chip_add",
        grid_spec=pltpu.PrefetchScalarGridSpec(
            num_scalar_prefetch=1, grid=(HALF_BLOCKS,),
            in_specs=[pl.BlockSpec((1, ADD_ROWS, D), lambda i, pos_ref: (pos_ref[0], i, 0)),
                      pl.BlockSpec((3, ADD_ROWS, D), lambda i, pos_ref: (0, i, 0))],
            out_specs=pl.BlockSpec((1, ADD_ROWS, D), lambda i, pos_ref: (pos_ref[1], i, 0))),
        out_shape=jax.ShapeDtypeStruct((2, PK_HALF, D), F32),
        compiler_params=_cparams(("parallel",)),
    )(pos, cp, land)


ANY = pl.BlockSpec(memory_space=pl.ANY)


def _mesh_pos():
    return lax.axis_index("x"), lax.axis_index("y"), lax.axis_index("c")


def _chip_peer(x, y, c, r):
    return (jnp.bitwise_xor(x, r >> 1), jnp.bitwise_xor(y, r & 1), c)


def _allgather(big0, meta_loc):
    def body(big0_ref, meta_ref, big_ref, metaf_ref, lsem, ssem, rsem, fssem, frsem, msem_s, msem_r):
        del big0_ref
        x, y, c = _mesh_pos()
        j = 2 * x + y
        me, sibling = (x, y, c), (x, y, 1 - c)

        def half_wait(s_sem, r_sem):
            rows = big_ref.at[pl.ds(0, PK_HALF), :]
            return pltpu.make_async_remote_copy(src_ref=rows, dst_ref=rows, send_sem=s_sem, recv_sem=r_sem,
                                                device_id=me, device_id_type=MESH)

        own_meta = pltpu.make_async_copy(meta_ref, metaf_ref.at[j], lsem)
        own_meta.start()
        for s in range(N_CHIPS):
            for hf in range(2):
                @pl.when((j == s) & (c == hf))
                def _():
                    for r in (1, 2, 3):
                        for _, b, n in _half_pieces(s, hf):
                            rows = big_ref.at[pl.ds(b, n), :]
                            pltpu.make_async_remote_copy(
                                src_ref=rows, dst_ref=rows, send_sem=ssem.at[r - 1], recv_sem=rsem.at[r - 1],
                                device_id=_chip_peer(x, y, c, r), device_id_type=MESH).start()

        meta_copies = [pltpu.make_async_remote_copy(
            src_ref=meta_ref, dst_ref=metaf_ref.at[j], send_sem=msem_s.at[r - 1], recv_sem=msem_r.at[r - 1],
            device_id=_chip_peer(x, y, c, r), device_id_type=MESH) for r in (1, 2, 3)]
        for cp in meta_copies:
            cp.start()
        for r in (1, 2, 3):
            half_wait(ssem.at[r - 1], rsem.at[r - 1]).wait_recv()
            src_shard = jnp.bitwise_xor(j, r)
            for s in range(N_CHIPS):
                for hf in range(2):
                    @pl.when((src_shard == s) & (c == hf))
                    def _():
                        for p, b, n in _half_pieces(s, hf):
                            rows = big_ref.at[pl.ds(b, n), :]
                            pltpu.make_async_remote_copy(
                                src_ref=rows, dst_ref=rows, send_sem=fssem.at[r - 1], recv_sem=frsem.at[r - 1],
                                device_id=sibling, device_id_type=MESH).start()
        for r in (1, 2, 3):
            half_wait(fssem.at[r - 1], frsem.at[r - 1]).wait_recv()
        for r in (1, 2, 3):
            half_wait(ssem.at[r - 1], rsem.at[r - 1]).wait_send()
            half_wait(fssem.at[r - 1], frsem.at[r - 1]).wait_send()
        for cp in meta_copies:
            cp.wait_send()
            cp.wait_recv()
        own_meta.wait()

    dma3 = pltpu.SemaphoreType.DMA((3,))
    return pl.pallas_call(
        body, name="allgather_weights", in_specs=[ANY, ANY], out_specs=[ANY, ANY],
        out_shape=[jax.ShapeDtypeStruct((BIG_ROWS, D), BF16), jax.ShapeDtypeStruct((N_CHIPS,) + meta_loc.shape, F32)],
        scratch_shapes=[pltpu.SemaphoreType.DMA(()), dma3, dma3, dma3, dma3, dma3, dma3],
        input_output_aliases={0: 0},
        compiler_params=pltpu.CompilerParams(has_side_effects=True),
    )(big0, meta_loc)


RS_SPLIT = 2


def _rs_core_exchange(gpk):
    n = PK_HALF // RS_SPLIT

    def body(g_ref, land_ref, ssem, rsem):
        x, y, c = _mesh_pos()
        me, sibling = (x, y, c), (x, y, 1 - c)
        for s in range(N_CHIPS):
            for q in range(RS_SPLIT):
                src = pl.ds(pl.multiple_of((1 - c) * PK_HALF + q * n, 16), n)
                pltpu.make_async_remote_copy(
                    src_ref=g_ref.at[s, src, :], dst_ref=land_ref.at[s, pl.ds(q * n, n), :],
                    send_sem=ssem, recv_sem=rsem, device_id=sibling, device_id_type=MESH).start()
        whole = pltpu.make_async_remote_copy(src_ref=land_ref, dst_ref=land_ref, send_sem=ssem, recv_sem=rsem,
                                             device_id=me, device_id_type=MESH)
        whole.wait_recv()
        whole.wait_send()

    return pl.pallas_call(
        body, name="rs_core_exchange", in_specs=[ANY], out_specs=ANY,
        out_shape=jax.ShapeDtypeStruct((N_CHIPS, PK_HALF, D), BF16),
        scratch_shapes=[pltpu.SemaphoreType.DMA(())] * 2,
        compiler_params=pltpu.CompilerParams(has_side_effects=True),
    )(gpk)


def _rs_chip_exchange(cp):
    def body(cp_ref, land_ref, ssem, rsem):
        x, y, c = _mesh_pos()
        j = 2 * x + y
        copies = [pltpu.make_async_remote_copy(
            src_ref=cp_ref.at[jnp.bitwise_xor(j, r)], dst_ref=land_ref.at[r - 1], send_sem=ssem.at[r - 1],
            recv_sem=rsem.at[r - 1], device_id=_chip_peer(x, y, c, r), device_id_type=MESH) for r in (1, 2, 3)]
        for cpy in copies:
            cpy.start()
        for cpy in copies:
            cpy.wait_recv()
        for cpy in copies:
            cpy.wait_send()

    dma3 = pltpu.SemaphoreType.DMA((3,))
    return pl.pallas_call(
        body, name="rs_chip_exchange", in_specs=[ANY], out_specs=ANY,
        out_shape=jax.ShapeDtypeStruct((3, PK_HALF, D), BF16), scratch_shapes=[dma3, dma3],
        compiler_params=pltpu.CompilerParams(has_side_effects=True),
    )(cp)


def _rs_finish(full, small):
    n = PK_HALF // RS_SPLIT

    def body(full_in_ref, sm_ref, full_ref, all_ref, lsem, ssem, rsem, sm_s, sm_r):
        del full_in_ref
        x, y, c = _mesh_pos()
        me, sibling = (x, y, c), (x, y, 1 - c)
        my_id = 4 * x + 2 * y + c
        own_sm = pltpu.make_async_copy(sm_ref, all_ref.at[my_id], lsem)
        own_sm.start()
        for q in range(RS_SPLIT):
            rows = full_ref.at[c, pl.ds(q * n, n), :]
            pltpu.make_async_remote_copy(src_ref=rows, dst_ref=rows, send_sem=ssem, recv_sem=rsem,
                                         device_id=sibling, device_id_type=MESH).start()
        half = pltpu.make_async_remote_copy(src_ref=full_ref.at[c], dst_ref=full_ref.at[c], send_sem=ssem,
                                            recv_sem=rsem, device_id=me, device_id_type=MESH)
        smalls = [pltpu.make_async_remote_copy(
            src_ref=sm_ref, dst_ref=all_ref.at[my_id], send_sem=sm_s.at[r - 1], recv_sem=sm_r.at[r - 1],
            device_id=(jnp.bitwise_xor(x, r >> 2), jnp.bitwise_xor(y, (r >> 1) & 1), jnp.bitwise_xor(c, r & 1)),
            device_id_type=MESH) for r in range(1, 8)]
        for cpy in smalls:
            cpy.start()
        half.wait_recv()
        half.wait_send()
        for cpy in smalls:
            cpy.wait_recv()
        for cpy in smalls:
            cpy.wait_send()
        own_sm.wait()

    dma7 = pltpu.SemaphoreType.DMA((7,))
    return pl.pallas_call(
        body, name="rs_finish", in_specs=[ANY, ANY], out_specs=[ANY, ANY],
        out_shape=[jax.ShapeDtypeStruct((2, PK_HALF, D), F32), jax.ShapeDtypeStruct((8, SM_ROWS, 128), F32)],
        scratch_shapes=[pltpu.SemaphoreType.DMA(()), pltpu.SemaphoreType.DMA(()), pltpu.SemaphoreType.DMA(()), dma7, dma7],
        input_output_aliases={0: 0},
        compiler_params=pltpu.CompilerParams(has_side_effects=True),
    )(full, small)


def _rope_tables(s):
    pos = jnp.arange(s, dtype=F32) - PAD_FRONT

    def cs(d):
        inv = ROPE_BASE ** (-jnp.arange(0, d, 2, dtype=F32) / d)
        ang = pos[:, None] * inv[None, :]
        return jnp.cos(ang), jnp.sin(ang)

    c, sn = cs(ROPE_D)
    z = jnp.zeros_like(c)
    ktab = (jnp.concatenate([c, c, z, z], axis=1), jnp.concatenate([-sn, z, z, z], axis=1),
            jnp.concatenate([z, sn, z, z], axis=1))
    one = jnp.ones_like(c)
    qtab = tuple(ATT_SCALE * jnp.concatenate(parts, axis=1) for parts in (
        [one, one, one, one, c, c, z, z], [z, z, z, z, -sn, z, z, z], [z, z, z, z, z, sn, z, z]))
    c, sn = cs(DH)
    return qtab, ktab, jnp.concatenate([c, c], axis=1), jnp.concatenate([-sn, sn], axis=1)


def _grad_mm(a, b, m, n, s, name):
    return _matmul(a, b, "tn", m, n, s, min(m, 512), min(n, 1024), s, BF16, name, n_outer=True)


def _local_step(x2, target2, hp0, big, small_w):
    norm_w, qn_w, kvn_w, gn_w, gn_b, fin_w = small_w
    s = x2.shape[0] + BLK
    tm = _row_tile(s, 1408)
    uq = big[BIG_UQ:BIG_UQ + 384].reshape(HEADS, DH + ROPE_D, Q_RANK)
    wq = jnp.pad(uq, ((0, 0), (0, HD - DH - ROPE_D), (0, 0))).reshape(HEADS * HD, Q_RANK)
    wukv = big[BIG_UKV:BIG_UKV + 256].reshape(2 * WIDTH, KV_RANK)
    wbm = big[BIG_BM:BIG_BM + 1024].reshape(D, WIDTH)
    wbr = big[BIG_BR:BIG_BR + 1024].reshape(D, WIDTH)
    qtab, ktab, c128, s128 = _rope_tables(s)
    consts, gamma = _ret_consts()

    h, xn, r1 = _rms_in(x2, hp0, norm_w, s)
    proj = _matmul(xn, big, "nt", s, PROJ_W, D, tm, 512, D, BF16, "proj", b_off=(BIG_IN // 512, 0))
    cqn, rq, ckvn, rkv, qq, kk, vv = _mla_prep(proj, qn_w, kvn_w, wq, wukv, qtab, ktab, s)
    o_mla, y_mla, lse = _attn_fwd(qq, kk, vv, proj, s)
    y_ret, on, rstd, qr, kr, rall = _ret_fwd(proj, gn_w, gn_b, c128, s128, consts, gamma, s)
    u_mla, u_ret, merged = _merge_fwd(y_mla, y_ret, wbm, wbr, proj, s)
    h2 = _out_fwd(merged, big, h, s)
    dh2, dh2b, loss_blk, dfin = _loss_bwd(h2, target2, fin_w, s)

    dproj, du_mla, du_ret = _merge_bwd(dh2b, big, u_mla, u_ret, proj, s)
    g_out = _grad_mm(merged, dh2b, D, D, s, "grad_w_out")
    dproj, do_mla, delta = _branch_mla_bwd(dproj, du_mla, wbm, o_mla, proj, s)
    g_bm = _grad_mm(du_mla, y_mla, D, WIDTH, s, "grad_w_branch_mla")
    dproj, do_ret, dgw, dgb = _branch_ret_bwd(dproj, du_ret, wbr, on, rstd, gn_w, gn_b, proj, s)
    g_br = _grad_mm(du_ret, y_ret, D, WIDTH, s, "grad_w_branch_ret")
    dproj = _ret_bwd(dproj, qr, kr, proj, rall, do_ret, c128, s128, consts, gamma, s)
    dqq, dkk, dvv = _attn_bwd(qq, kk, vv, do_mla, lse, delta.reshape(lse.shape), s)
    dproj, dq, dkv, dqnw, dkvnw = _mla_prep_bwd(dproj, dqq, dkk, dvv, proj, rq, rkv, qn_w, kvn_w, wq, wukv,
                                                qtab, ktab, s)
    g_q = _grad_mm(dq, cqn, HEADS * HD, Q_RANK, s, "grad_w_uq")
    g_ukv = _grad_mm(dkv, ckvn, 2 * WIDTH, KV_RANK, s, "grad_w_ukv")
    g_in = _grad_mm(dproj, xn, PROJ_W, D, s, "grad_w_in")
    dxn = _matmul(dproj, big, "nn", s, D, PROJ_W, tm, 1024, 1024, F32, "dxn", b_off=(BIG_IN // 1024, 0))
    grad_x, gmeta_blk, dnorm = _rms_in_bwd(dxn, h, r1, dh2, norm_w, s)

    g_uq = g_q.reshape(HEADS, HD, Q_RANK)[:, :DH + ROPE_D]
    parts = [(g_out, 512), (g_bm.reshape(1024, D), 256), (g_br.reshape(1024, D), 256), (g_uq.reshape(384, D), 96),
             (g_ukv.reshape(256, D), 64)]

    def shard_rows(sh):
        out = [p[n * sh:n * (sh + 1)] for p, n in parts]
        out += [g_in[b - BIG_IN:b - BIG_IN + n] for p, b, n in _pieces(sh) if PK_IN <= p < PK_PAD]
        return jnp.concatenate(out + [jnp.zeros((PK_ROWS - PK_PAD, D), BF16)], axis=0)

    gbig = jnp.stack([shard_rows(sh) for sh in range(N_CHIPS)])
    small = _small_rows((dnorm, dqnw, dkvnw, dgw, dgb, dfin), loss_blk, gmeta_blk[PAD_FRONT:])
    return grad_x, gbig, small


def _small_rows(ws, first=None, last=None):
    def part(a, rows):
        a = a.reshape(-1, 128)
        return a if a.shape[0] == rows else jnp.pad(a, ((0, rows - a.shape[0]), (0, 0)))

    bounds = (SM_NORM, SM_QN, SM_KVN, SM_GNW, SM_GNB, SM_FIN, SM_META)
    first = jnp.zeros((SM_NORM, 128), F32) if first is None else first
    last = jnp.zeros((SM_ROWS - SM_META, 128), F32) if last is None else last
    return jnp.concatenate([part(first, SM_NORM)] + [part(w, bounds[k + 1] - bounds[k]) for k, w in enumerate(ws)]
                           + [part(last, SM_ROWS - SM_META)], axis=0)


def kernel(x, meta, norm_w, w_in, mla_q_norm_w, mla_w_uq, mla_kv_norm_w, mla_w_ukv, ret_gn_w, ret_gn_b, w_branch_mla, w_branch_ret, w_out, final_norm_w, loss_target, m_meta, m_norm_w, m_w_in, m_mla_q_norm_w, m_mla_w_uq, m_mla_kv_norm_w, m_mla_w_ukv, m_ret_gn_w, m_ret_gn_b, m_w_branch_mla, m_w_branch_ret, m_w_out, m_final_norm_w, v_meta, v_norm_w, v_w_in, v_mla_q_norm_w, v_mla_w_uq, v_mla_kv_norm_w, v_mla_w_ukv, v_ret_gn_w, v_ret_gn_b, v_w_branch_mla, v_w_branch_ret, v_w_out, v_final_norm_w):
    j = 2 * lax.axis_index("x") + lax.axis_index("y")
    tr = lambda w: w[0].T.reshape(-1, D).astype(BF16)
    pos = jnp.stack([j, lax.axis_index("c")]).astype(jnp.int32)
    put = lambda buf, rows, off: lax.dynamic_update_slice(buf, rows, (off, 0))
    big0 = lax.empty((BIG_ROWS, D), BF16)
    big0 = put(big0, w_out[0].astype(BF16), BIG_OUT + 512 * j)
    big0 = put(big0, tr(w_branch_mla), BIG_BM + 256 * j)
    big0 = put(big0, tr(w_branch_ret), BIG_BR + 256 * j)
    big0 = put(big0, tr(mla_w_uq), BIG_UQ + 96 * j)
    big0 = put(big0, tr(mla_w_ukv), BIG_UKV + 64 * j)
    big0 = put(big0, jnp.zeros((PK_ROWS - PK_PAD, D), BF16), BIG_JUNK + 16 * j)
    cuts, dest = _in_cuts()
    w_in_t = tr(w_in)
    dest = jnp.asarray(dest, jnp.int32)[j]
    for k in range(len(cuts) - 1):
        big0 = put(big0, w_in_t[cuts[k]:cuts[k + 1]], dest[k])
    big0 = put(big0, jnp.zeros((ZERO_ROWS, D), BF16), BIG_IN + IN_WIDTH)
    big, meta_all = _allgather(big0, meta)
    meta_full = meta_all.transpose(1, 0, 2).reshape(N_META, D)
    hp0 = jnp.concatenate([jnp.zeros((PAD_FRONT, D), F32), meta_full], axis=0)
    small_w = (norm_w, mla_q_norm_w, mla_kv_norm_w, ret_gn_w, ret_gn_b, final_norm_w.reshape(1, D))

    grad_x, gbig, small = _local_step(x[0], loss_target[0], hp0, big, small_w)

    land = _rs_core_exchange(gbig)
    chip_part = _rs_core_add(gbig, land, pos)
    land2 = _rs_chip_exchange(chip_part)
    full, small_all = _rs_finish(_rs_chip_add(chip_part, land2, pos), small)
    full = full.reshape(PK_ROWS, D)

    untr = lambda lo, hi, rows: full[lo:hi].reshape(rows, -1).T
    grads = {
        "w_out": full[PK_OUT:PK_BM], "w_branch_mla": untr(PK_BM, PK_BR, 512), "w_branch_ret": untr(PK_BR, PK_UQ, 512),
        "mla_w_uq": untr(PK_UQ, PK_UKV, 384), "mla_w_ukv": untr(PK_UKV, PK_IN, 512), "w_in": untr(PK_IN, PK_PAD, IN_SHARD),
    }
    big_w = {"w_in": (w_in, m_w_in, v_w_in), "mla_w_uq": (mla_w_uq, m_mla_w_uq, v_mla_w_uq),
             "mla_w_ukv": (mla_w_ukv, m_mla_w_ukv, v_mla_w_ukv),
             "w_branch_mla": (w_branch_mla, m_w_branch_mla, v_w_branch_mla),
             "w_branch_ret": (w_branch_ret, m_w_branch_ret, v_w_branch_ret), "w_out": (w_out, m_w_out, v_w_out)}
    res = {}
    for name, (w, m, v) in big_w.items():
        d, nm, nv = _adamw(w[0], grads[name], m[0], v[0], "adamw_" + name)
        res[name] = (grads[name][None], d[None], nm[None], nv[None])

    small_m = (m_norm_w, m_mla_q_norm_w, m_mla_kv_norm_w, m_ret_gn_w, m_ret_gn_b, m_final_norm_w.reshape(1, D))
    small_v = (v_norm_w, v_mla_q_norm_w, v_mla_kv_norm_w, v_ret_gn_w, v_ret_gn_b, v_final_norm_w.reshape(1, D))
    gs, ds, ms, vs = _adamw_small(_small_rows(small_w), small_all, _small_rows(small_m), _small_rows(small_v))
    names = ["norm_w", "mla_q_norm_w", "mla_kv_norm_w", "ret_gn_w", "ret_gn_b", "final_norm_w"]
    bounds = [SM_NORM, SM_QN, SM_KVN, SM_GNW, SM_GNB, SM_FIN]
    for k, name in enumerate(names):
        shape = (D,) if name == "final_norm_w" else (1, -1)
        rows = small_w[k].size // 128
        res[name] = tuple(a[bounds[k]:bounds[k] + rows].reshape(shape) for a in (gs, ds, ms, vs))
    g_meta = lax.dynamic_slice_in_dim(gs[SM_META:SM_META + 256].reshape(N_META, D), j * (D // N_CHIPS), D // N_CHIPS, axis=1)
    res["meta"] = (g_meta,) + tuple(_adamw(meta, g_meta, m_meta, v_meta, "adamw_meta"))

    order = ["meta", "norm_w", "w_in", "mla_q_norm_w", "mla_w_uq", "mla_kv_norm_w", "mla_w_ukv", "ret_gn_w", "ret_gn_b",
             "w_branch_mla", "w_branch_ret", "w_out", "final_norm_w"]
    return (gs[0, 0], grad_x[None]) + tuple(res[n][k] for k in range(4) for n in order)
```

```python
import functools
import math

import numpy as np
import jax
import jax.numpy as jnp
from jax import lax
from jax.experimental import pallas as pl
from jax.experimental.pallas import tpu as pltpu

F32 = jnp.float32
BF16 = jnp.bfloat16
MESH = pl.DeviceIdType.MESH

D = 2048
N_META = 16
BLK = 128
PAD_FRONT = BLK - N_META
HEADS = 8
DH = 128
ROPE_D = 64
Q_RANK = 512
KV_RANK = 256
WIDTH = HEADS * DH
ROPE_BASE = 10000.0
NORM_EPS = 1e-6
GN_EPS = 1e-5
NEG_INF = -1e30
ATT_SCALE = (DH + ROPE_D) ** -0.5
RET_SCALE = DH ** -0.5
IN_WIDTH = 10048
N_CHIPS = 4
IN_SHARD = IN_WIDTH // N_CHIPS
ADAM_LR, ADAM_B1, ADAM_B2, ADAM_EPS, ADAM_WD, ADAM_STEP = 0.001, 0.9, 0.999, 1e-08, 0.01, 10

R_Q, R_K, R_V, Z_RET, Z_MLA, GATE0, GATE1 = 0, 1024, 2048, 3072, 4096, 5120, 5120 + D
C_Q = 5120 + 2 * D
C_KV = C_Q + Q_RANK
K_PE = C_KV + KV_RANK
PROJ_W = 10240
IN_RUNS = ((0, 832, C_Q), (832, 1856, Z_MLA), (1856, 4928, R_Q), (4928, 5952, Z_RET), (5952, IN_WIDTH, GATE0))

BIG_OUT, BIG_BM, BIG_BR, BIG_UQ, BIG_UKV, BIG_JUNK, BIG_IN = 0, 2048, 3072, 4096, 4480, 4736, 5120
BIG_ROWS = BIG_IN + PROJ_W
ZERO_ROWS = PROJ_W - IN_WIDTH
PK_OUT, PK_BM, PK_BR, PK_UQ, PK_UKV, PK_IN, PK_PAD, PK_ROWS = 0, 512, 768, 1024, 1120, 1184, 3696, 3712
PK_HALF = PK_ROWS // 2
SM_LOSS, SM_NORM, SM_QN, SM_KVN, SM_GNW, SM_GNB, SM_FIN, SM_META, SM_ROWS = 0, 8, 24, 32, 40, 48, 56, 72, 328

VMEM_LIMIT = 56 * 1024 * 1024


def _pieces(shard):
    j = shard
    out = [(PK_OUT, BIG_OUT + 512 * j, 512), (PK_BM, BIG_BM + 256 * j, 256), (PK_BR, BIG_BR + 256 * j, 256),
           (PK_UQ, BIG_UQ + 96 * j, 96), (PK_UKV, BIG_UKV + 64 * j, 64)]
    for lo, hi, new in IN_RUNS:
        a, b = max(lo, IN_SHARD * j), min(hi, IN_SHARD * (j + 1))
        if b > a:
            out.append((PK_IN + a - IN_SHARD * j, BIG_IN + new + a - lo, b - a))
    out += [(PK_PAD, BIG_JUNK + 16 * j, 16)]
    return out


def _in_cuts():
    cuts = sorted({0, IN_SHARD} | {b - IN_SHARD * j for j in range(N_CHIPS) for lo, hi, _ in IN_RUNS for b in (lo, hi)
                                   if 0 < b - IN_SHARD * j < IN_SHARD})

    def new_row(col):
        return next(BIG_IN + new + col - lo for lo, hi, new in IN_RUNS if lo <= col < hi)

    return cuts, [[new_row(IN_SHARD * j + c) for c in cuts[:-1]] for j in range(N_CHIPS)]


def _half_pieces(shard, half):
    lo, hi = half * PK_HALF, (half + 1) * PK_HALF
    out = []
    for p, b, n in _pieces(shard):
        s, e = max(p, lo), min(p + n, hi)
        if e > s:
            out.append((s, b + (s - p), e - s))
    return out


def _row_tile(rows, cap):
    best = BLK
    for t in range(BLK, cap + 1, BLK):
        if rows % t == 0:
            best = t
    return best


def _cparams(sem):
    return pltpu.CompilerParams(dimension_semantics=sem, vmem_limit_bytes=VMEM_LIMIT)


def _dot(a, b, form):
    dn = {"nt": (((1,), (1,)), ((), ())), "nn": (((1,), (0,)), ((), ())), "tn": (((0,), (0,)), ((), ()))}[form]
    return lax.dot_general(a, b, dn, preferred_element_type=F32)


def _sigmoid(v):
    return 1.0 / (1.0 + jnp.exp(-v))


def _matmul(a, b, form, m, n, k, tm, tn, tk, out_dtype, name, a_off=(0, 0), b_off=(0, 0), n_outer=False, after=None):
    nk = k // tk
    gi, gj = m // tm, n // tn

    def ij(g0, g1):
        return (g1, g0) if n_outer else (g0, g1)

    if form == "nt":
        a_spec = pl.BlockSpec((tm, tk), lambda g0, g1, kk: (ij(g0, g1)[0] + a_off[0], kk + a_off[1]))
        b_spec = pl.BlockSpec((tn, tk), lambda g0, g1, kk: (ij(g0, g1)[1] + b_off[0], kk + b_off[1]))
    elif form == "nn":
        a_spec = pl.BlockSpec((tm, tk), lambda g0, g1, kk: (ij(g0, g1)[0] + a_off[0], kk + a_off[1]))
        b_spec = pl.BlockSpec((tk, tn), lambda g0, g1, kk: (kk + b_off[0], ij(g0, g1)[1] + b_off[1]))
    else:
        a_spec = pl.BlockSpec((tk, tm), lambda g0, g1, kk: (kk + a_off[0], ij(g0, g1)[0] + a_off[1]))
        b_spec = pl.BlockSpec((tk, tn), lambda g0, g1, kk: (kk + b_off[0], ij(g0, g1)[1] + b_off[1]))
    o_spec = pl.BlockSpec((tm, tn), lambda g0, g1, kk: ij(g0, g1))

    def body(a_ref, b_ref, *rest):
        o_ref, *acc = rest[0 if after is None else 1:]
        p = _dot(a_ref[...], b_ref[...], form)
        if nk == 1:
            o_ref[...] = p.astype(o_ref.dtype)
        else:
            acc_ref, = acc
            kk = pl.program_id(2)

            @pl.when(kk == 0)
            def _():
                acc_ref[...] = p

            @pl.when(kk > 0)
            def _():
                acc_ref[...] += p

            @pl.when(kk == nk - 1)
            def _():
                o_ref[...] = acc_ref[...].astype(o_ref.dtype)

    extra = [] if after is None else [after]
    return pl.pallas_call(
        body, name=name, grid=(gj, gi, nk) if n_outer else (gi, gj, nk),
        in_specs=[a_spec, b_spec] + [pl.BlockSpec(memory_space=pl.ANY)] * len(extra), out_specs=o_spec,
        out_shape=jax.ShapeDtypeStruct((m, n), out_dtype),
        scratch_shapes=[] if nk == 1 else [pltpu.VMEM((tm, tn), F32)],
        compiler_params=_cparams(("parallel", "parallel", "arbitrary")),
    )(a, b, *extra)


def _rms_in(x, hp0, norm_w, s):
    def body(x_ref, hp0_ref, w_ref, h_ref, xn_ref, r_ref):
        def run(hv):
            r = lax.rsqrt(jnp.mean(hv * hv, axis=-1, keepdims=True) + NORM_EPS)
            h_ref[...] = hv
            xn_ref[...] = (hv * r * w_ref[...]).astype(BF16)
            r_ref[...] = r

        @pl.when(pl.program_id(0) == 0)
        def _():
            run(hp0_ref[...])

        @pl.when(pl.program_id(0) > 0)
        def _():
            run(x_ref[...])

    return pl.pallas_call(
        body, name="rms_in", grid=(s // BLK,),
        in_specs=[pl.BlockSpec((BLK, D), lambda i: (jnp.maximum(i - 1, 0), 0)),
                  pl.BlockSpec((BLK, D), lambda i: (0, 0)), pl.BlockSpec((1, D), lambda i: (0, 0))],
        out_specs=[pl.BlockSpec((BLK, D), lambda i: (i, 0)), pl.BlockSpec((BLK, D), lambda i: (i, 0)),
                   pl.BlockSpec((BLK, 1), lambda i: (i, 0))],
        out_shape=[jax.ShapeDtypeStruct((s, D), F32), jax.ShapeDtypeStruct((s, D), BF16),
                   jax.ShapeDtypeStruct((s, 1), F32)],
        compiler_params=_cparams(("arbitrary",)),
    )(x, hp0, norm_w)


def _rope64(t, c, sa, sb):
    return t * c + pltpu.roll(t, t.shape[1] - ROPE_D // 2, 1) * sa + pltpu.roll(t, ROPE_D // 2, 1) * sb


def _rope128(t, c, sg):
    return t * c + pltpu.roll(t, DH // 2, 1) * sg


HD = 2 * DH


def _mla_prep(proj, qn_w, kvn_w, wq, wukv, qtab, ktab, s):
    tm = _row_tile(s, 384)

    def body(cq_ref, ckv_ref, kpe_ref, qnw_ref, kvnw_ref, wq_ref, wukv_ref, qc_ref, qa_ref, qb_ref,
             kc_ref, ka_ref, kb_ref, cqn_ref, rq_ref, ckvn_ref, rkv_ref, qq_ref, kk_ref, vv_ref):
        cq = cq_ref[...].astype(F32)
        rq = lax.rsqrt(jnp.mean(cq * cq, axis=-1, keepdims=True) + NORM_EPS)
        cqn = (cq * rq * qnw_ref[...]).astype(BF16)
        ckv = ckv_ref[...].astype(F32)
        rkv = lax.rsqrt(jnp.mean(ckv * ckv, axis=-1, keepdims=True) + NORM_EPS)
        ckvn = (ckv * rkv * kvnw_ref[...]).astype(BF16)
        cqn_ref[...] = cqn
        rq_ref[...] = rq
        ckvn_ref[...] = ckvn
        rkv_ref[...] = rkv
        q = _dot(cqn, wq_ref[...], "nt")
        kv = _dot(ckvn, wukv_ref[...], "nt")
        kp = _rope64(kpe_ref[...].astype(F32), kc_ref[...], ka_ref[...], kb_ref[...]).astype(BF16)
        qc, qa, qb = qc_ref[...], qa_ref[...], qb_ref[...]
        ones = jnp.ones((tm, DH), BF16)
        for h in range(HEADS):
            lo, mid, hi = h * HD, h * HD + DH, (h + 1) * HD
            qq_ref[:, lo:hi] = _rope64(q[:, lo:hi], qc, qa, qb).astype(BF16)
            kk_ref[:, lo:mid] = kv[:, lo:mid].astype(BF16)
            kk_ref[:, mid:hi] = kp
            vv_ref[:, lo:mid] = kv[:, mid:hi].astype(BF16)
            vv_ref[:, mid:hi] = ones

    row = lambda w, cb: pl.BlockSpec((tm, w), lambda i: (i, cb))
    full = lambda a: pl.BlockSpec(a.shape, lambda i: (0, 0))
    wide = jax.ShapeDtypeStruct((s, HEADS * HD), BF16)
    return pl.pallas_call(
        body, name="mla_prep", grid=(s // tm,),
        in_specs=[row(Q_RANK, C_Q // Q_RANK), row(KV_RANK, C_KV // KV_RANK), row(DH, K_PE // DH),
                  full(qn_w), full(kvn_w), full(wq), full(wukv), row(HD, 0), row(HD, 0), row(HD, 0),
                  row(DH, 0), row(DH, 0), row(DH, 0)],
        out_specs=[row(Q_RANK, 0), row(1, 0), row(KV_RANK, 0), row(1, 0), row(HEADS * HD, 0), row(HEADS * HD, 0),
                   row(HEADS * HD, 0)],
        out_shape=[jax.ShapeDtypeStruct((s, Q_RANK), BF16), jax.ShapeDtypeStruct((s, 1), F32),
                   jax.ShapeDtypeStruct((s, KV_RANK), BF16), jax.ShapeDtypeStruct((s, 1), F32), wide, wide, wide],
        compiler_params=_cparams(("parallel",)),
    )(proj, proj, proj, qn_w, kvn_w, wq, wukv, *qtab, *ktab)


def _att_mask(qi, ki, t):
    key = ki * t + lax.broadcasted_iota(jnp.int32, (t, t), 0)
    qry = qi * t + lax.broadcasted_iota(jnp.int32, (t, t), 1)
    return (key <= qry) & (key >= PAD_FRONT)


def _silu(z):
    return z * _sigmoid(z)


def _attn_fwd(qq, kk, vv, proj, s):
    t = _row_tile(s, 384)

    def body(q_ref, k_ref, v_ref, z_ref, o_ref, y_ref, lse_ref, acc_ref, m_ref):
        qi = pl.program_id(1)
        q = q_ref[...]
        m_ref[...] = jnp.full(m_ref.shape, NEG_INF, F32)
        acc_ref[...] = jnp.zeros(acc_ref.shape, F32)

        def tile(ki, masked):
            ks = pl.ds(pl.multiple_of(ki * t, t), t)
            st = _dot(k_ref[ks, :], q, "nt")
            if masked:
                st = jnp.where(_att_mask(qi, ki, t), st, NEG_INF)
            m_old = m_ref[...]
            m_new = jnp.maximum(m_old, jnp.max(st, axis=0, keepdims=True))
            pt = jnp.exp(st - m_new).astype(BF16)
            acc_ref[...] = jnp.exp(m_old - m_new) * acc_ref[...] + _dot(v_ref[ks, :], pt, "tn")
            m_ref[...] = m_new

        tile(0, True)

        def inner(ki, carry):
            tile(ki, False)
            return carry

        lax.fori_loop(1, qi, inner, 0)

        @pl.when(qi > 0)
        def _():
            tile(qi, True)

        l = acc_ref[DH:DH + 1, :]
        o = (acc_ref[:DH, :] / l).T
        o_ref[...] = o.astype(BF16)
        y_ref[...] = (o * _silu(z_ref[...].astype(F32))).astype(BF16)
        lse_ref[0, 0] = m_ref[...] + jnp.log(l)

    qtile = pl.BlockSpec((t, DH), lambda h, i: (i, h))
    head = pl.BlockSpec((s, HD), lambda h, i: (0, h))
    return pl.pallas_call(
        body, name="attn_fwd", grid=(HEADS, s // t),
        in_specs=[pl.BlockSpec((t, HD), lambda h, i: (i, h)), head, head,
                  pl.BlockSpec((t, DH), lambda h, i: (i, Z_MLA // DH + h))],
        out_specs=[qtile, qtile, pl.BlockSpec((1, 1, 1, t), lambda h, i: (h, i, 0, 0))],
        out_shape=[jax.ShapeDtypeStruct((s, WIDTH), BF16), jax.ShapeDtypeStruct((s, WIDTH), BF16),
                   jax.ShapeDtypeStruct((HEADS, s // t, 1, t), F32)],
        scratch_shapes=[pltpu.VMEM((HD, t), F32), pltpu.VMEM((1, t), F32)],
        compiler_params=_cparams(("parallel", "arbitrary")),
    )(qq, kk, vv, proj)


def _ret_consts():
    log_g = np.log1p(-(2.0 ** (-5.0 - np.arange(HEADS, dtype=np.float64))))
    n = np.arange(BLK, dtype=np.float64)
    diff = n[:, None] - n[None, :]
    decay = np.where(diff >= 0, np.exp(log_g[:, None, None] * np.maximum(diff, 0.0)), 0.0)
    zeta = np.exp(log_g[:, None] * (BLK - 1.0 - n))[:, :, None]
    xi = np.exp(log_g[:, None] * (n + 1.0))[:, :, None]
    gamma = [float(np.float32(np.exp(g * BLK))) for g in log_g]
    return (jnp.asarray(decay, F32), jnp.asarray(zeta, F32), jnp.asarray(xi, F32)), gamma


def _ret_fwd(proj, gn_w, gn_b, c128, s128, consts, gamma, s):
    nb = s // BLK
    decay, zeta, xi = consts

    def body(rq_ref, rk_ref, rv_ref, z_ref, gw_ref, gb_ref, c_ref, s_ref, dm_ref, zt_ref, xi_ref,
             y_ref, on_ref, rstd_ref, qr_ref, kr_ref, rall_ref, state):
        @pl.when(pl.program_id(0) == 0)
        def _():
            state[...] = jnp.zeros_like(state)

        c, sg = c_ref[...], s_ref[...]
        for h in range(HEADS):
            sl = slice(h * DH, (h + 1) * DH)
            q = _rope128(rq_ref[:, sl].astype(F32), c, sg).astype(BF16)
            kf = _rope128(rk_ref[:, sl].astype(F32), c, sg) * RET_SCALE
            k = kf.astype(BF16)
            v = rv_ref[:, sl]
            qr_ref[:, sl] = q
            kr_ref[:, sl] = k
            r_prev = state[h]
            rall_ref[0, h] = r_prev
            a = _dot(q, k, "nt") * dm_ref[h]
            o = _dot(a.astype(BF16), v, "nn") + _dot(q, r_prev.astype(BF16), "nn") * xi_ref[h]
            state[h] = r_prev * gamma[h] + _dot((kf * zt_ref[h]).astype(BF16), v, "tn")
            mu = jnp.mean(o, axis=-1, keepdims=True)
            var = jnp.mean(jnp.square(o - mu), axis=-1, keepdims=True)
            rstd = lax.rsqrt(var + GN_EPS)
            on = (o - mu) * rstd
            rstd_ref[h] = rstd
            on_ref[:, sl] = on.astype(BF16)
            ogn = on * gw_ref[:, sl] + gb_ref[:, sl]
            y_ref[:, sl] = (ogn * _silu(z_ref[:, sl].astype(F32))).astype(BF16)

    seg = lambda cb: pl.BlockSpec((BLK, WIDTH), lambda i: (i, cb))
    full = lambda a: pl.BlockSpec(a.shape, lambda i: (0,) * a.ndim)
    tab = pl.BlockSpec((BLK, DH), lambda i: (i, 0))
    return pl.pallas_call(
        body, name="ret_fwd", grid=(nb,),
        in_specs=[seg(R_Q // WIDTH), seg(R_K // WIDTH), seg(R_V // WIDTH), seg(Z_RET // WIDTH), full(gn_w), full(gn_b),
                  tab, tab, full(decay), full(zeta), full(xi)],
        out_specs=[seg(0), seg(0), pl.BlockSpec((HEADS, BLK, 1), lambda i: (0, i, 0)), seg(0), seg(0),
                   pl.BlockSpec((1, HEADS, DH, DH), lambda i: (i, 0, 0, 0))],
        out_shape=[jax.ShapeDtypeStruct((s, WIDTH), BF16), jax.ShapeDtypeStruct((s, WIDTH), BF16),
                   jax.ShapeDtypeStruct((HEADS, s, 1), F32), jax.ShapeDtypeStruct((s, WIDTH), BF16),
                   jax.ShapeDtypeStruct((s, WIDTH), BF16), jax.ShapeDtypeStruct((nb, HEADS, DH, DH), F32)],
        scratch_shapes=[pltpu.VMEM((HEADS, DH, DH), F32)],
        compiler_params=_cparams(("arbitrary",)),
    )(proj, proj, proj, proj, gn_w, gn_b, c128, s128, decay, zeta, xi)


def _merge_fwd(y_mla, y_ret, wbm, wbr, proj, s):
    tm, tn = _row_tile(s, 1408), 512

    def body(ym_ref, yr_ref, wm_ref, wr_ref, g0_ref, g1_ref, um_ref, ur_ref, mg_ref):
        um = _dot(ym_ref[...], wm_ref[...], "nt")
        ur = _dot(yr_ref[...], wr_ref[...], "nt")
        um_ref[...] = um.astype(BF16)
        ur_ref[...] = ur.astype(BF16)
        mg_ref[...] = (_sigmoid(g0_ref[...].astype(F32)) * um + _sigmoid(g1_ref[...].astype(F32)) * ur).astype(BF16)

    yspec = pl.BlockSpec((tm, WIDTH), lambda i, j: (i, 0))
    wspec = pl.BlockSpec((tn, WIDTH), lambda i, j: (j, 0))
    ospec = pl.BlockSpec((tm, tn), lambda i, j: (i, j))
    return pl.pallas_call(
        body, name="merge_fwd", grid=(s // tm, D // tn),
        in_specs=[yspec, yspec, wspec, wspec, pl.BlockSpec((tm, tn), lambda i, j: (i, GATE0 // tn + j)),
                  pl.BlockSpec((tm, tn), lambda i, j: (i, GATE1 // tn + j))],
        out_specs=[ospec, ospec, ospec],
        out_shape=[jax.ShapeDtypeStruct((s, D), BF16)] * 3,
        compiler_params=_cparams(("parallel", "parallel")),
    )(y_mla, y_ret, wbm, wbr, proj, proj)


def _out_fwd(merged, big, h, s):
    tm, tn = _row_tile(s, 1408), 512

    def body(m_ref, w_ref, h_ref, o_ref):
        o_ref[...] = h_ref[...] + _dot(m_ref[...], w_ref[...], "nn")

    return pl.pallas_call(
        body, name="out_fwd", grid=(s // tm, D // tn),
        in_specs=[pl.BlockSpec((tm, D), lambda i, j: (i, 0)), pl.BlockSpec((D, tn), lambda i, j: (BIG_OUT // D, j)),
                  pl.BlockSpec((tm, tn), lambda i, j: (i, j))],
        out_specs=pl.BlockSpec((tm, tn), lambda i, j: (i, j)),
        out_shape=jax.ShapeDtypeStruct((s, D), F32),
        compiler_params=_cparams(("parallel", "parallel")),
    )(merged, big, h)


def _loss_bwd(h2, target, fin_w, s):
    nb = s // BLK

    def body(h2_ref, t_ref, w_ref, dh_ref, dhb_ref, loss_ref, dw_ref):
        i = pl.program_id(0)

        @pl.when(i == 0)
        def _():
            dh_ref[...] = jnp.zeros_like(dh_ref)
            dhb_ref[...] = jnp.zeros_like(dhb_ref)
            loss_ref[...] = jnp.zeros_like(loss_ref)
            dw_ref[...] = jnp.zeros_like(dw_ref)

        @pl.when(i > 0)
        def _():
            hv = h2_ref[...]
            w = w_ref[...]
            r = lax.rsqrt(jnp.mean(hv * hv, axis=-1, keepdims=True) + NORM_EPS)
            nrm = hv * r
            e = nrm * w - t_ref[...]
            loss_ref[...] += jnp.full(loss_ref.shape, 0.5 / D, F32) * jnp.sum(e * e)
            dy = e * (1.0 / D)
            dw_ref[...] += jnp.sum(dy * nrm, axis=0, keepdims=True)
            g = dy * w
            dh = r * (g - nrm * jnp.mean(g * nrm, axis=-1, keepdims=True))
            dh_ref[...] = dh
            dhb_ref[...] = dh.astype(BF16)

    blk = pl.BlockSpec((BLK, D), lambda i: (i, 0))
    return pl.pallas_call(
        body, name="loss_bwd", grid=(nb,),
        in_specs=[blk, pl.BlockSpec((BLK, D), lambda i: (jnp.maximum(i - 1, 0), 0)), pl.BlockSpec((1, D), lambda i: (0, 0))],
        out_specs=[blk, blk, pl.BlockSpec((8, 128), lambda i: (0, 0)), pl.BlockSpec((1, D), lambda i: (0, 0))],
        out_shape=[jax.ShapeDtypeStruct((s, D), F32), jax.ShapeDtypeStruct((s, D), BF16),
                   jax.ShapeDtypeStruct((8, 128), F32), jax.ShapeDtypeStruct((1, D), F32)],
        compiler_params=_cparams(("arbitrary",)),
    )(h2, target, fin_w)


def _merge_bwd(dh2b, big, u_mla, u_ret, proj, s):
    tm, tn = _row_tile(s, 1408), 512

    def body(d_ref, w_ref, um_ref, ur_ref, gate_ref, dproj_ref, dum_ref, dur_ref, dm_ref):
        branch = pl.program_id(2)

        @pl.when(branch == 0)
        def _():
            dm_ref[...] = _dot(d_ref[...], w_ref[...], "nt")

        dm = dm_ref[...]
        gt = _sigmoid(gate_ref[...].astype(F32))

        @pl.when(branch == 0)
        def _():
            dproj_ref[...] = (dm * um_ref[...].astype(F32) * gt * (1.0 - gt)).astype(BF16)
            dum_ref[...] = (dm * gt).astype(BF16)

        @pl.when(branch == 1)
        def _():
            dproj_ref[...] = (dm * ur_ref[...].astype(F32) * gt * (1.0 - gt)).astype(BF16)
            dur_ref[...] = (dm * gt).astype(BF16)

    ospec = pl.BlockSpec((tm, tn), lambda i, j, b: (i, j))
    gate = pl.BlockSpec((tm, tn), lambda i, j, b: (i, GATE0 // tn + b * (D // tn) + j))
    return pl.pallas_call(
        body, name="merge_bwd", grid=(s // tm, D // tn, 2),
        in_specs=[pl.BlockSpec((tm, D), lambda i, j, b: (i, 0)),
                  pl.BlockSpec((tn, D), lambda i, j, b: (BIG_OUT // tn + j, 0)), ospec, ospec, gate],
        out_specs=[gate, ospec, ospec],
        out_shape=[jax.ShapeDtypeStruct((s, PROJ_W), BF16), jax.ShapeDtypeStruct((s, D), BF16),
                   jax.ShapeDtypeStruct((s, D), BF16)],
        scratch_shapes=[pltpu.VMEM((tm, tn), F32)],
        compiler_params=_cparams(("parallel", "parallel", "arbitrary")),
    )(dh2b, big, u_mla, u_ret, proj)


def _dsilu(z):
    sg = _sigmoid(z)
    return sg * (1.0 + z * (1.0 - sg))


def _branch_mla_bwd(dproj, du, wbm, o_mla, proj, s):
    tm = _row_tile(s, 384)

    def body(dproj_in, du_ref, w_ref, o_ref, z_ref, dz_ref, do_ref, delta_ref):
        del dproj_in
        dy = _dot(du_ref[...], w_ref[...], "nn")
        z = z_ref[...].astype(F32)
        o = o_ref[...].astype(F32)
        do = dy * _silu(z)
        do_ref[...] = do.astype(BF16)
        dz_ref[...] = (dy * o * _dsilu(z)).astype(BF16)
        prod = do * o
        for h in range(HEADS):
            delta_ref[h] = jnp.sum(prod[:, h * DH:(h + 1) * DH], axis=-1, keepdims=True)

    row = lambda w, cb: pl.BlockSpec((tm, w), lambda i: (i, cb))
    return pl.pallas_call(
        body, name="branch_mla_bwd", grid=(s // tm,),
        in_specs=[ANY, row(D, 0), pl.BlockSpec((D, WIDTH), lambda i: (0, 0)), row(WIDTH, 0),
                  row(WIDTH, Z_MLA // WIDTH)],
        out_specs=[row(WIDTH, Z_MLA // WIDTH), row(WIDTH, 0), pl.BlockSpec((HEADS, tm, 1), lambda i: (0, i, 0))],
        out_shape=[jax.ShapeDtypeStruct((s, PROJ_W), BF16), jax.ShapeDtypeStruct((s, WIDTH), BF16),
                   jax.ShapeDtypeStruct((HEADS, s, 1), F32)],
        input_output_aliases={0: 0},
        compiler_params=_cparams(("parallel",)),
    )(dproj, du, wbm, o_mla, proj)


def _branch_ret_bwd(dproj, du, wbr, on, rstd, gn_w, gn_b, proj, s):
    tm = BLK

    def body(dproj_in, du_ref, w_ref, on_ref, rstd_ref, gw_ref, gb_ref, z_ref, dz_ref, do_ref, dgw_ref, dgb_ref):
        del dproj_in

        @pl.when(pl.program_id(0) == 0)
        def _():
            dgw_ref[...] = jnp.zeros_like(dgw_ref)
            dgb_ref[...] = jnp.zeros_like(dgb_ref)

        dy = _dot(du_ref[...], w_ref[...], "nn")
        z = z_ref[...].astype(F32)
        on = on_ref[...].astype(F32)
        gw = gw_ref[...]
        dogn = dy * _silu(z)
        dz_ref[...] = (dy * (on * gw + gb_ref[...]) * _dsilu(z)).astype(BF16)
        dgw_ref[...] += jnp.sum(dogn * on, axis=0, keepdims=True)
        dgb_ref[...] += jnp.sum(dogn, axis=0, keepdims=True)
        don = dogn * gw
        for h in range(HEADS):
            sl = slice(h * DH, (h + 1) * DH)
            dn, nh = don[:, sl], on[:, sl]
            do = rstd_ref[h] * (dn - jnp.mean(dn, axis=-1, keepdims=True)
                                - nh * jnp.mean(dn * nh, axis=-1, keepdims=True))
            do_ref[:, sl] = do.astype(BF16)

    row = lambda w, cb: pl.BlockSpec((tm, w), lambda i: (i, cb))
    vec = pl.BlockSpec((1, WIDTH), lambda i: (0, 0))
    return pl.pallas_call(
        body, name="branch_ret_bwd", grid=(s // tm,),
        in_specs=[ANY, row(D, 0), pl.BlockSpec((D, WIDTH), lambda i: (0, 0)), row(WIDTH, 0),
                  pl.BlockSpec((HEADS, tm, 1), lambda i: (0, i, 0)), vec, vec, row(WIDTH, Z_RET // WIDTH)],
        out_specs=[row(WIDTH, Z_RET // WIDTH), row(WIDTH, 0), vec, vec],
        out_shape=[jax.ShapeDtypeStruct((s, PROJ_W), BF16), jax.ShapeDtypeStruct((s, WIDTH), BF16)]
        + [jax.ShapeDtypeStruct((1, WIDTH), F32)] * 2,
        input_output_aliases={0: 0},
        compiler_params=_cparams(("arbitrary",)),
    )(dproj, du, wbr, on, rstd, gn_w, gn_b, proj)


def _ret_bwd(dproj, qr, kr, proj, rall, do_ret, c128, s128, consts, gamma, s):
    nb = s // BLK
    decay, zeta, xi = consts

    def body(dproj_in, q_ref, k_ref, v_ref, r_ref, do_ref, c_ref, s_ref, dm_ref, zt_ref, xi_ref, out_ref, gstate):
        del dproj_in
        dq_ref, dk_ref, dv_ref = (out_ref.at[:, pl.ds(off, WIDTH)] for off in (R_Q, R_K, R_V))

        @pl.when(pl.program_id(0) == 0)
        def _():
            gstate[...] = jnp.zeros_like(gstate)

        c, sg = c_ref[...], s_ref[...]
        for h in range(HEADS):
            sl = slice(h * DH, (h + 1) * DH)
            q, k, v, do = q_ref[:, sl], k_ref[:, sl], v_ref[:, sl], do_ref[:, sl]
            dm = dm_ref[h]
            g_next = gstate[h]
            gb = g_next.astype(BF16)
            a = (_dot(q, k, "nt") * dm).astype(BF16)
            da = (_dot(do, v, "nt") * dm).astype(BF16)
            dox = (do.astype(F32) * xi_ref[h]).astype(BF16)
            dq = _dot(da, k, "nn") + _dot(dox, r_ref[0, h].astype(BF16), "nt")
            dk = _dot(da, q, "tn") + _dot(v, gb, "nt") * zt_ref[h]
            kz = (k.astype(F32) * zt_ref[h]).astype(BF16)
            dv = _dot(a, do, "tn") + _dot(kz, gb, "nn")
            gstate[h] = g_next * gamma[h] + _dot(q, dox, "tn")
            dk = dk * RET_SCALE
            dq_ref[:, sl] = _rope128(dq, c, -sg).astype(BF16)
            dk_ref[:, sl] = _rope128(dk, c, -sg).astype(BF16)
            dv_ref[:, sl] = dv.astype(BF16)

    rev = lambda cb: pl.BlockSpec((BLK, WIDTH), lambda i: (nb - 1 - i, cb))
    full = lambda a: pl.BlockSpec(a.shape, lambda i: (0,) * a.ndim)
    tab = pl.BlockSpec((BLK, DH), lambda i: (nb - 1 - i, 0))
    return pl.pallas_call(
        body, name="ret_bwd", grid=(nb,),
        in_specs=[ANY, rev(0), rev(0), rev(R_V // WIDTH),
                  pl.BlockSpec((1, HEADS, DH, DH), lambda i: (nb - 1 - i, 0, 0, 0)),
                  rev(0), tab, tab, full(decay), full(zeta), full(xi)],
        out_specs=pl.BlockSpec((BLK, 3 * WIDTH), lambda i: (nb - 1 - i, R_Q // (3 * WIDTH))),
        out_shape=jax.ShapeDtypeStruct((s, PROJ_W), BF16),
        scratch_shapes=[pltpu.VMEM((HEADS, DH, DH), F32)],
        input_output_aliases={0: 0},
        compiler_params=_cparams(("arbitrary",)),
    )(dproj, qr, kr, proj, rall, do_ret, c128, s128, decay, zeta, xi)


def _attn_bwd(qq, kk, vv, do, lse, delta, s):
    t = _row_tile(s, 384)
    n = s // t

    def body(q_ref, k_ref, v_ref, do_ref, lse_ref, delta_ref, dq_ref, dk_ref, dv_ref, dk_acc, dv_acc):
        ki = pl.program_id(1)

        @pl.when(ki == 0)
        def _():
            dq_ref[...] = jnp.zeros(dq_ref.shape, F32)

        k, v = k_ref[...], v_ref[...]
        dk_acc[...] = jnp.zeros(dk_acc.shape, F32)
        dv_acc[...] = jnp.zeros(dv_acc.shape, F32)

        def tile(qi, masked):
            qs = pl.ds(pl.multiple_of(qi * t, t), t)
            q, dov = q_ref[qs, :], do_ref[qs, :]
            st = _dot(k, q, "nt")
            if masked:
                st = jnp.where(_att_mask(qi, ki, t), st, NEG_INF)
            pt = jnp.exp(st - lse_ref[0, qi])
            dv_acc[...] += _dot(pt.astype(BF16), dov, "nn")
            dst = (pt * (_dot(v, dov, "nt") - delta_ref[0, qi])).astype(BF16)
            dk_acc[...] += _dot(dst, q, "nn")
            dq_ref[qs, :] += _dot(dst, k, "tn")

        def masked_tile(qi, carry):
            tile(qi, True)
            return carry

        def plain_tile(qi, carry):
            tile(qi, False)
            return carry

        tile(ki, True)

        @pl.when(ki == 0)
        def _():
            lax.fori_loop(1, n, masked_tile, 0)

        @pl.when(ki > 0)
        def _():
            lax.fori_loop(ki + 1, n, plain_tile, 0)

        dk_ref[...] = dk_acc[...].astype(BF16)
        dv_ref[...] = dv_acc[...].astype(BF16)

    head = lambda w: pl.BlockSpec((s, w), lambda h, i: (0, h))
    stat = pl.BlockSpec((1, n, 1, t), lambda h, i: (h, 0, 0, 0))
    return pl.pallas_call(
        body, name="attn_bwd", grid=(HEADS, n),
        in_specs=[head(HD), pl.BlockSpec((t, HD), lambda h, i: (i, h)), pl.BlockSpec((t, DH), lambda h, i: (i, 2 * h)),
                  head(DH), stat, stat],
        out_specs=[head(HD), pl.BlockSpec((t, HD), lambda h, i: (i, h)), pl.BlockSpec((t, DH), lambda h, i: (i, h))],
        out_shape=[jax.ShapeDtypeStruct((s, HEADS * HD), F32), jax.ShapeDtypeStruct((s, HEADS * HD), BF16),
                   jax.ShapeDtypeStruct((s, WIDTH), BF16)],
        scratch_shapes=[pltpu.VMEM((t, HD), F32), pltpu.VMEM((t, DH), F32)],
        compiler_params=_cparams(("parallel", "arbitrary")),
    )(qq, kk, vv, do, lse, delta)


def _rms_bwd(dn, nhat, r, w):
    g = dn * w
    return r * (g - nhat * jnp.mean(g * nhat, axis=-1, keepdims=True)), dn * nhat


def _mla_prep_bwd(dproj, dqq, dkk, dvv, proj, rq, rkv, qn_w, kvn_w, wq, wukv, qtab, ktab, s):
    tm = _row_tile(s, 384)
    tail = PROJ_W - C_Q

    def body(dproj_in, dqq_ref, dkk_ref, dvv_ref, cq_ref, ckv_ref, rq_ref, rkv_ref, qnw_ref, kvnw_ref, wq_ref,
             wukv_ref, qc_ref, qa_ref, qb_ref, kc_ref, ka_ref, kb_ref,
             tail_ref, dq_ref, dkv_ref, dqnw_ref, dkvnw_ref):
        del dproj_in
        dcq_ref = tail_ref.at[:, pl.ds(0, Q_RANK)]
        dckv_ref = tail_ref.at[:, pl.ds(C_KV - C_Q, KV_RANK)]
        dkpe_ref = tail_ref.at[:, pl.ds(K_PE - C_Q, 2 * DH)]

        @pl.when(pl.program_id(0) == 0)
        def _():
            dqnw_ref[...] = jnp.zeros_like(dqnw_ref)
            dkvnw_ref[...] = jnp.zeros_like(dkvnw_ref)

        qc, qa, qb = qc_ref[...], qa_ref[...], qb_ref[...]
        dkp = jnp.zeros((tm, DH), F32)
        for h in range(HEADS):
            lo, mid, hi = h * HD, h * HD + DH, (h + 1) * HD
            dq_ref[:, lo:hi] = _rope64(dqq_ref[:, lo:hi], qc, -qa, -qb).astype(BF16)
            dkv_ref[:, lo:mid] = dkk_ref[:, lo:mid]
            dkv_ref[:, mid:hi] = dvv_ref[:, h * DH:(h + 1) * DH]
            dkp = dkp + dkk_ref[:, mid:hi].astype(F32)
        dcqn = _dot(dq_ref[...], wq_ref[...], "nn")
        rq_v = rq_ref[...]
        dcq, prod = _rms_bwd(dcqn, cq_ref[...].astype(F32) * rq_v, rq_v, qnw_ref[...])
        dcq_ref[...] = dcq.astype(BF16)
        dqnw_ref[...] += jnp.sum(prod, axis=0, keepdims=True)
        dckvn = _dot(dkv_ref[...], wukv_ref[...], "nn")
        rkv_v = rkv_ref[...]
        dckv, prod = _rms_bwd(dckvn, ckv_ref[...].astype(F32) * rkv_v, rkv_v, kvnw_ref[...])
        dckv_ref[...] = dckv.astype(BF16)
        dkvnw_ref[...] += jnp.sum(prod, axis=0, keepdims=True)
        dkpe_ref[:, :DH] = _rope64(dkp, kc_ref[...], -ka_ref[...], -kb_ref[...]).astype(BF16)
        dkpe_ref[:, DH:] = jnp.zeros((tm, DH), BF16)

    row = lambda w, cb: pl.BlockSpec((tm, w), lambda i: (i, cb))
    full = lambda a: pl.BlockSpec(a.shape, lambda i: (0, 0))
    wide = jax.ShapeDtypeStruct((s, HEADS * HD), BF16)
    return pl.pallas_call(
        body, name="mla_prep_bwd", grid=(s // tm,),
        in_specs=[ANY, row(HEADS * HD, 0), row(HEADS * HD, 0), row(WIDTH, 0),
                  row(Q_RANK, C_Q // Q_RANK), row(KV_RANK, C_KV // KV_RANK), row(1, 0), row(1, 0),
                  full(qn_w), full(kvn_w), full(wq), full(wukv), row(HD, 0), row(HD, 0), row(HD, 0),
                  row(DH, 0), row(DH, 0), row(DH, 0)],
        out_specs=[row(tail, C_Q // tail), row(HEADS * HD, 0), row(HEADS * HD, 0),
                   pl.BlockSpec((1, Q_RANK), lambda i: (0, 0)), pl.BlockSpec((1, KV_RANK), lambda i: (0, 0))],
        out_shape=[jax.ShapeDtypeStruct((s, PROJ_W), BF16), wide, wide,
                   jax.ShapeDtypeStruct((1, Q_RANK), F32), jax.ShapeDtypeStruct((1, KV_RANK), F32)],
        input_output_aliases={0: 0},
        compiler_params=_cparams(("arbitrary",)),
    )(dproj, dqq, dkk, dvv, proj, proj, rq, rkv, qn_w, kvn_w, wq, wukv, *qtab, *ktab)


def _rms_in_bwd(dxn, h, r, dh2, norm_w, s):
    def body(dxn_ref, h_ref, r_ref, dh2_ref, w_ref, gx_ref, gm_ref, dw_ref):
        i = pl.program_id(0)
        r_v = r_ref[...]
        dx, prod = _rms_bwd(dxn_ref[...], h_ref[...] * r_v, r_v, w_ref[...])
        dh = dh2_ref[...] + dx

        @pl.when(i == 0)
        def _():
            gm_ref[...] = dh
            dw_ref[...] = jnp.sum(prod, axis=0, keepdims=True)

        @pl.when(i > 0)
        def _():
            gx_ref[...] = dh
            dw_ref[...] += jnp.sum(prod, axis=0, keepdims=True)

    blk = pl.BlockSpec((BLK, D), lambda i: (i, 0))
    return pl.pallas_call(
        body, name="rms_in_bwd", grid=(s // BLK,),
        in_specs=[blk, blk, pl.BlockSpec((BLK, 1), lambda i: (i, 0)), blk, pl.BlockSpec((1, D), lambda i: (0, 0))],
        out_specs=[pl.BlockSpec((BLK, D), lambda i: (jnp.maximum(i - 1, 0), 0)), pl.BlockSpec((BLK, D), lambda i: (0, 0)),
                   pl.BlockSpec((1, D), lambda i: (0, 0))],
        out_shape=[jax.ShapeDtypeStruct((s - BLK, D), F32), jax.ShapeDtypeStruct((BLK, D), F32),
                   jax.ShapeDtypeStruct((1, D), F32)],
        compiler_params=_cparams(("arbitrary",)),
    )(dxn, h, r, dh2, norm_w)


def _adam_math(w, g, m, v):
    m = ADAM_B1 * m + (1.0 - ADAM_B1) * g
    v = ADAM_B2 * v + (1.0 - ADAM_B2) * (g * g)
    m_hat = m / (1.0 - ADAM_B1 ** ADAM_STEP)
    v_hat = v / (1.0 - ADAM_B2 ** ADAM_STEP)
    return -ADAM_LR * (m_hat / (jnp.sqrt(v_hat) + ADAM_EPS) + ADAM_WD * w), m, v


def _adamw(w, g, m, v, name):
    rows, cols = w.shape
    tr = rows
    for cand in (128, 64, 32, 16, 8):
        if rows % cand == 0:
            tr = cand
            break

    def body(w_ref, g_ref, m_ref, v_ref, d_ref, nm_ref, nv_ref):
        d_ref[...], nm_ref[...], nv_ref[...] = _adam_math(w_ref[...], g_ref[...], m_ref[...], v_ref[...])

    spec = pl.BlockSpec((tr, cols), lambda i: (i, 0))
    return pl.pallas_call(
        body, name=name, grid=(rows // tr,), in_specs=[spec] * 4, out_specs=[spec] * 3,
        out_shape=[jax.ShapeDtypeStruct((rows, cols), F32)] * 3,
        compiler_params=_cparams(("parallel",)),
    )(w, g, m, v)


def _adamw_small(w, gall, m, v):
    def body(w_ref, g_ref, m_ref, v_ref, gs_ref, d_ref, nm_ref, nv_ref):
        g = g_ref[0]
        for dev in range(1, 8):
            g = g + g_ref[dev]
        gs_ref[...] = g
        d_ref[...], nm_ref[...], nv_ref[...] = _adam_math(w_ref[...], g, m_ref[...], v_ref[...])

    return pl.pallas_call(
        body, name="adamw_small", out_shape=[jax.ShapeDtypeStruct((SM_ROWS, 128), F32)] * 4,
        compiler_params=pltpu.CompilerParams(vmem_limit_bytes=VMEM_LIMIT),
    )(w, gall, m, v)


ADD_ROWS = 464
HALF_BLOCKS = PK_HALF // ADD_ROWS


def _rs_core_add(gpk, land, pos):
    def body(pos_ref, a_ref, b_ref, o_ref):
        o_ref[...] = (a_ref[...].astype(F32) + b_ref[...].astype(F32)).astype(BF16)

    return pl.pallas_call(
        body, name="rs_core_add",
        grid_spec=pltpu.PrefetchScalarGridSpec(
            num_scalar_prefetch=1, grid=(N_CHIPS, HALF_BLOCKS),
            in_specs=[pl.BlockSpec((1, ADD_ROWS, D), lambda s, i, pos_ref: (s, pos_ref[1] * HALF_BLOCKS + i, 0)),
                      pl.BlockSpec((1, ADD_ROWS, D), lambda s, i, pos_ref: (s, i, 0))],
            out_specs=pl.BlockSpec((1, ADD_ROWS, D), lambda s, i, pos_ref: (s, i, 0))),
        out_shape=jax.ShapeDtypeStruct((N_CHIPS, PK_HALF, D), BF16),
        compiler_params=_cparams(("parallel", "parallel")),
    )(pos, gpk, land)


def _rs_chip_add(cp, land, pos):
    def body(pos_ref, a_ref, l_ref, o_ref):
        o_ref[0] = ((a_ref[0].astype(F32) + l_ref[0].astype(F32)) + l_ref[1].astype(F32)) + l_ref[2].astype(F32)

    return pl.pallas_call(
        body, name="rs_chip_add",
        grid_spec=pltpu.PrefetchScalarGridSpec(
            num_scalar_prefetch=1, grid=(HALF_BLOCKS,),
            in_specs=[pl.BlockSpec((1, ADD_ROWS, D), lambda i, pos_ref: (pos_ref[0], i, 0)),
                      pl.BlockSpec((3, ADD_ROWS, D), lambda i, pos_ref: (0, i, 0))],
            out_specs=pl.BlockSpec((1, ADD_ROWS, D), lambda i, pos_ref: (pos_ref[1], i, 0))),
        out_shape=jax.ShapeDtypeStruct((2, PK_HALF, D), F32),
        compiler_params=_cparams(("parallel",)),
    )(pos, cp, land)


ANY = pl.BlockSpec(memory_space=pl.ANY)


def _mesh_pos():
    return lax.axis_index("x"), lax.axis_index("y"), lax.axis_index("c")


def _chip_peer(x, y, c, r):
    return (jnp.bitwise_xor(x, r >> 1), jnp.bitwise_xor(y, r & 1), c)


def _allgather(big0, meta_loc):
    def body(big0_ref, meta_ref, big_ref, metaf_ref, lsem, ssem, rsem, fssem, frsem, msem_s, msem_r):
        del big0_ref
        x, y, c = _mesh_pos()
        j = 2 * x + y
        me, sibling = (x, y, c), (x, y, 1 - c)

        def half_wait(s_sem, r_sem):
            rows = big_ref.at[pl.ds(0, PK_HALF), :]
            return pltpu.make_async_remote_copy(src_ref=rows, dst_ref=rows, send_sem=s_sem, recv_sem=r_sem,
                                                device_id=me, device_id_type=MESH)

        own_meta = pltpu.make_async_copy(meta_ref, metaf_ref.at[j], lsem)
        own_meta.start()
        for s in range(N_CHIPS):
            for hf in range(2):
                @pl.when((j == s) & (c == hf))
                def _():
                    for r in (1, 2, 3):
                        for _, b, n in _half_pieces(s, hf):
                            rows = big_ref.at[pl.ds(b, n), :]
                            pltpu.make_async_remote_copy(
                                src_ref=rows, dst_ref=rows, send_sem=ssem.at[r - 1], recv_sem=rsem.at[r - 1],
                                device_id=_chip_peer(x, y, c, r), device_id_type=MESH).start()

        meta_copies = [pltpu.make_async_remote_copy(
            src_ref=meta_ref, dst_ref=metaf_ref.at[j], send_sem=msem_s.at[r - 1], recv_sem=msem_r.at[r - 1],
            device_id=_chip_peer(x, y, c, r), device_id_type=MESH) for r in (1, 2, 3)]
        for cp in meta_copies:
            cp.start()
        for r in (1, 2, 3):
            half_wait(ssem.at[r - 1], rsem.at[r - 1]).wait_recv()
            src_shard = jnp.bitwise_xor(j, r)
            for s in range(N_CHIPS):
                for hf in range(2):
                    @pl.when((src_shard == s) & (c == hf))
                    def _():
                        for p, b, n in _half_pieces(s, hf):
                            rows = big_ref.at[pl.ds(b, n), :]
                            pltpu.make_async_remote_copy(
                                src_ref=rows, dst_ref=rows, send_sem=fssem.at[r - 1], recv_sem=frsem.at[r - 1],
                                device_id=sibling, device_id_type=MESH).start()
        for r in (1, 2, 3):
            half_wait(fssem.at[r - 1], frsem.at[r - 1]).wait_recv()
        for r in (1, 2, 3):
            half_wait(ssem.at[r - 1], rsem.at[r - 1]).wait_send()
            half_wait(fssem.at[r - 1], frsem.at[r - 1]).wait_send()
        for cp in meta_copies:
            cp.wait_send()
            cp.wait_recv()
        own_meta.wait()

    dma3 = pltpu.SemaphoreType.DMA((3,))
    return pl.pallas_call(
        body, name="allgather_weights", in_specs=[ANY, ANY], out_specs=[ANY, ANY],
        out_shape=[jax.ShapeDtypeStruct((BIG_ROWS, D), BF16), jax.ShapeDtypeStruct((N_CHIPS,) + meta_loc.shape, F32)],
        scratch_shapes=[pltpu.SemaphoreType.DMA(()), dma3, dma3, dma3, dma3, dma3, dma3],
        input_output_aliases={0: 0},
        compiler_params=pltpu.CompilerParams(has_side_effects=True),
    )(big0, meta_loc)


RS_SPLIT = 2


def _rs_core_exchange(gpk):
    n = PK_HALF // RS_SPLIT

    def body(g_ref, land_ref, ssem, rsem):
        x, y, c = _mesh_pos()
        me, sibling = (x, y, c), (x, y, 1 - c)
        for s in range(N_CHIPS):
            for q in range(RS_SPLIT):
                src = pl.ds(pl.multiple_of((1 - c) * PK_HALF + q * n, 16), n)
                pltpu.make_async_remote_copy(
                    src_ref=g_ref.at[s, src, :], dst_ref=land_ref.at[s, pl.ds(q * n, n), :],
                    send_sem=ssem, recv_sem=rsem, device_id=sibling, device_id_type=MESH).start()
        whole = pltpu.make_async_remote_copy(src_ref=land_ref, dst_ref=land_ref, send_sem=ssem, recv_sem=rsem,
                                             device_id=me, device_id_type=MESH)
        whole.wait_recv()
        whole.wait_send()

    return pl.pallas_call(
        body, name="rs_core_exchange", in_specs=[ANY], out_specs=ANY,
        out_shape=jax.ShapeDtypeStruct((N_CHIPS, PK_HALF, D), BF16),
        scratch_shapes=[pltpu.SemaphoreType.DMA(())] * 2,
        compiler_params=pltpu.CompilerParams(has_side_effects=True),
    )(gpk)


HBM = pl.BlockSpec(memory_space=pltpu.HBM)
SEM = pl.BlockSpec(memory_space=pltpu.SEMAPHORE)
EFFECT = pltpu.SideEffectType.DATAFLOW_SIDE_EFFECTING


def _rs_chip_start(cp):
    def body(cp_ref, land_ref, s1, s2, s3, r1, r2, r3, cp_thru, land_thru, token):
        x, y, c = _mesh_pos()
        j = 2 * x + y
        for r, s_sem, r_sem in zip((1, 2, 3), (s1, s2, s3), (r1, r2, r3)):
            pltpu.make_async_remote_copy(
                src_ref=cp_ref.at[jnp.bitwise_xor(j, r)], dst_ref=land_ref.at[r - 1], send_sem=s_sem, recv_sem=r_sem,
                device_id=_chip_peer(x, y, c, r), device_id_type=MESH).start()
        token[...] = jnp.zeros_like(token)

    land_shape = (3, PK_HALF, D)
    sem = pltpu.SemaphoreType.DMA(())
    out = pl.pallas_call(
        body, name="rs_chip_start",
        out_shape=(sem,) * 6 + (pltpu.HBM(cp.shape, cp.dtype), pltpu.HBM(land_shape, BF16),
                                jax.ShapeDtypeStruct((8, 128), F32)),
        in_specs=(HBM, HBM), out_specs=(SEM,) * 6 + (HBM, HBM, pl.BlockSpec(memory_space=pltpu.VMEM)),
        input_output_aliases={0: 6, 1: 7},
        compiler_params=pltpu.CompilerParams(has_side_effects=EFFECT),
    )(pltpu.with_memory_space_constraint(cp, pltpu.HBM),
      pltpu.with_memory_space_constraint(lax.empty(land_shape, BF16), pltpu.HBM))
    return out[:6], out[6], out[7], out[8]


def _rs_chip_wait(sems, cp_thru, land_thru, after):
    def body(cp_ref, land_ref, s1, s2, s3, r1, r2, r3, after_ref, cp_out, land_out):
        me = _mesh_pos()
        for r, s_sem, r_sem in zip((1, 2, 3), (s1, s2, s3), (r1, r2, r3)):
            copy = pltpu.make_async_remote_copy(src_ref=cp_ref.at[0], dst_ref=land_ref.at[r - 1], send_sem=s_sem,
                                                recv_sem=r_sem, device_id=me, device_id_type=MESH)
            copy.wait_send()
            copy.wait_recv()

    return pl.pallas_call(
        body, name="rs_chip_wait",
        out_shape=(pltpu.HBM(cp_thru.shape, cp_thru.dtype), pltpu.HBM(land_thru.shape, land_thru.dtype)),
        in_specs=(HBM, HBM) + (SEM,) * 6 + (pl.BlockSpec(memory_space=pl.ANY),), out_specs=(HBM, HBM),
        input_output_aliases={0: 0, 1: 1},
        compiler_params=pltpu.CompilerParams(has_side_effects=EFFECT),
    )(cp_thru, land_thru, *sems, after)


def _rs_finish(full, small):
    n = PK_HALF // RS_SPLIT

    def body(full_in_ref, sm_ref, full_ref, all_ref, lsem, ssem, rsem, sm_s, sm_r):
        del full_in_ref
        x, y, c = _mesh_pos()
        me, sibling = (x, y, c), (x, y, 1 - c)
        my_id = 4 * x + 2 * y + c
        own_sm = pltpu.make_async_copy(sm_ref, all_ref.at[my_id], lsem)
        own_sm.start()
        for q in range(RS_SPLIT):
            rows = full_ref.at[c, pl.ds(q * n, n), :]
            pltpu.make_async_remote_copy(src_ref=rows, dst_ref=rows, send_sem=ssem, recv_sem=rsem,
                                         device_id=sibling, device_id_type=MESH).start()
        half = pltpu.make_async_remote_copy(src_ref=full_ref.at[c], dst_ref=full_ref.at[c], send_sem=ssem,
                                            recv_sem=rsem, device_id=me, device_id_type=MESH)
        smalls = [pltpu.make_async_remote_copy(
            src_ref=sm_ref, dst_ref=all_ref.at[my_id], send_sem=sm_s.at[r - 1], recv_sem=sm_r.at[r - 1],
            device_id=(jnp.bitwise_xor(x, r >> 2), jnp.bitwise_xor(y, (r >> 1) & 1), jnp.bitwise_xor(c, r & 1)),
            device_id_type=MESH) for r in range(1, 8)]
        for cpy in smalls:
            cpy.start()
        half.wait_recv()
        half.wait_send()
        for cpy in smalls:
            cpy.wait_recv()
        for cpy in smalls:
            cpy.wait_send()
        own_sm.wait()

    dma7 = pltpu.SemaphoreType.DMA((7,))
    return pl.pallas_call(
        body, name="rs_finish", in_specs=[ANY, ANY], out_specs=[ANY, ANY],
        out_shape=[jax.ShapeDtypeStruct((2, PK_HALF, D), F32), jax.ShapeDtypeStruct((8, SM_ROWS, 128), F32)],
        scratch_shapes=[pltpu.SemaphoreType.DMA(()), pltpu.SemaphoreType.DMA(()), pltpu.SemaphoreType.DMA(()), dma7, dma7],
        input_output_aliases={0: 0},
        compiler_params=pltpu.CompilerParams(has_side_effects=True),
    )(full, small)


def _rope_tables(s):
    pos = jnp.arange(s, dtype=F32) - PAD_FRONT

    def cs(d):
        inv = ROPE_BASE ** (-jnp.arange(0, d, 2, dtype=F32) / d)
        ang = pos[:, None] * inv[None, :]
        return jnp.cos(ang), jnp.sin(ang)

    c, sn = cs(ROPE_D)
    z = jnp.zeros_like(c)
    ktab = (jnp.concatenate([c, c, z, z], axis=1), jnp.concatenate([-sn, z, z, z], axis=1),
            jnp.concatenate([z, sn, z, z], axis=1))
    one = jnp.ones_like(c)
    qtab = tuple(ATT_SCALE * jnp.concatenate(parts, axis=1) for parts in (
        [one, one, one, one, c, c, z, z], [z, z, z, z, -sn, z, z, z], [z, z, z, z, z, sn, z, z]))
    c, sn = cs(DH)
    return qtab, ktab, jnp.concatenate([c, c], axis=1), jnp.concatenate([-sn, sn], axis=1)


def _grad_mm(a, b, m, n, s, name):
    return _matmul(a, b, "tn", m, n, s, min(m, 512), min(n, 1024), s, BF16, name, n_outer=True)


def _local_step(x2, target2, hp0, big, small_w, reduce_start=None, reduce_wait=None):
    norm_w, qn_w, kvn_w, gn_w, gn_b, fin_w = small_w
    s = x2.shape[0] + BLK
    tm = _row_tile(s, 1408)
    uq = big[BIG_UQ:BIG_UQ + 384].reshape(HEADS, DH + ROPE_D, Q_RANK)
    wq = jnp.pad(uq, ((0, 0), (0, HD - DH - ROPE_D), (0, 0))).reshape(HEADS * HD, Q_RANK)
    wukv = big[BIG_UKV:BIG_UKV + 256].reshape(2 * WIDTH, KV_RANK)
    wbm = big[BIG_BM:BIG_BM + 1024].reshape(D, WIDTH)
    wbr = big[BIG_BR:BIG_BR + 1024].reshape(D, WIDTH)
    qtab, ktab, c128, s128 = _rope_tables(s)
    consts, gamma = _ret_consts()

    h, xn, r1 = _rms_in(x2, hp0, norm_w, s)
    proj = _matmul(xn, big, "nt", s, PROJ_W, D, tm, 512, D, BF16, "proj", b_off=(BIG_IN // 512, 0))
    cqn, rq, ckvn, rkv, qq, kk, vv = _mla_prep(proj, qn_w, kvn_w, wq, wukv, qtab, ktab, s)
    o_mla, y_mla, lse = _attn_fwd(qq, kk, vv, proj, s)
    y_ret, on, rstd, qr, kr, rall = _ret_fwd(proj, gn_w, gn_b, c128, s128, consts, gamma, s)
    u_mla, u_ret, merged = _merge_fwd(y_mla, y_ret, wbm, wbr, proj, s)
    h2 = _out_fwd(merged, big, h, s)
    dh2, dh2b, loss_blk, dfin = _loss_bwd(h2, target2, fin_w, s)

    dproj, du_mla, du_ret = _merge_bwd(dh2b, big, u_mla, u_ret, proj, s)
    g_out = _grad_mm(merged, dh2b, D, D, s, "grad_w_out")
    dproj, do_mla, delta = _branch_mla_bwd(dproj, du_mla, wbm, o_mla, proj, s)
    g_bm = _grad_mm(du_mla, y_mla, D, WIDTH, s, "grad_w_branch_mla")
    dproj, do_ret, dgw, dgb = _branch_ret_bwd(dproj, du_ret, wbr, on, rstd, gn_w, gn_b, proj, s)
    g_br = _grad_mm(du_ret, y_ret, D, WIDTH, s, "grad_w_branch_ret")
    dproj = _ret_bwd(dproj, qr, kr, proj, rall, do_ret, c128, s128, consts, gamma, s)
    dqq, dkk, dvv = _attn_bwd(qq, kk, vv, do_mla, lse, delta.reshape(lse.shape), s)
    dproj, dq, dkv, dqnw, dkvnw = _mla_prep_bwd(dproj, dqq, dkk, dvv, proj, rq, rkv, qn_w, kvn_w, wq, wukv,
                                                qtab, ktab, s)
    g_q = _grad_mm(dq, cqn, HEADS * HD, Q_RANK, s, "grad_w_uq")
    g_ukv = _grad_mm(dkv, ckvn, 2 * WIDTH, KV_RANK, s, "grad_w_ukv")
    g_in = _grad_mm(dproj, xn, PROJ_W, D, s, "grad_w_in")

    g_uq = g_q.reshape(HEADS, HD, Q_RANK)[:, :DH + ROPE_D]
    parts = [(g_out, 512), (g_bm.reshape(1024, D), 256), (g_br.reshape(1024, D), 256), (g_uq.reshape(384, D), 96),
             (g_ukv.reshape(256, D), 64)]

    def shard_rows(sh):
        out = [p[n * sh:n * (sh + 1)] for p, n in parts]
        out += [g_in[b - BIG_IN:b - BIG_IN + n] for p, b, n in _pieces(sh) if PK_IN <= p < PK_PAD]
        return jnp.concatenate(out + [jnp.zeros((PK_ROWS - PK_PAD, D), BF16)], axis=0)

    gbig = jnp.stack([shard_rows(sh) for sh in range(N_CHIPS)])
    token, travelling = (None, gbig) if reduce_start is None else reduce_start(gbig)
    dxn = _matmul(dproj, big, "nn", s, D, PROJ_W, tm, 1024, 1024, F32, "dxn", b_off=(BIG_IN // 1024, 0), after=token)
    grad_x, gmeta_blk, dnorm = _rms_in_bwd(dxn, h, r1, dh2, norm_w, s)
    small = _small_rows((dnorm, dqnw, dkvnw, dgw, dgb, dfin), loss_blk, gmeta_blk[PAD_FRONT:])
    return grad_x, travelling if reduce_wait is None else reduce_wait(travelling, dnorm), small


def _small_rows(ws, first=None, last=None):
    def part(a, rows):
        a = a.reshape(-1, 128)
        return a if a.shape[0] == rows else jnp.pad(a, ((0, rows - a.shape[0]), (0, 0)))

    bounds = (SM_NORM, SM_QN, SM_KVN, SM_GNW, SM_GNB, SM_FIN, SM_META)
    first = jnp.zeros((SM_NORM, 128), F32) if first is None else first
    last = jnp.zeros((SM_ROWS - SM_META, 128), F32) if last is None else last
    return jnp.concatenate([part(first, SM_NORM)] + [part(w, bounds[k + 1] - bounds[k]) for k, w in enumerate(ws)]
                           + [part(last, SM_ROWS - SM_META)], axis=0)


def kernel(x, meta, norm_w, w_in, mla_q_norm_w, mla_w_uq, mla_kv_norm_w, mla_w_ukv, ret_gn_w, ret_gn_b, w_branch_mla, w_branch_ret, w_out, final_norm_w, loss_target, m_meta, m_norm_w, m_w_in, m_mla_q_norm_w, m_mla_w_uq, m_mla_kv_norm_w, m_mla_w_ukv, m_ret_gn_w, m_ret_gn_b, m_w_branch_mla, m_w_branch_ret, m_w_out, m_final_norm_w, v_meta, v_norm_w, v_w_in, v_mla_q_norm_w, v_mla_w_uq, v_mla_kv_norm_w, v_mla_w_ukv, v_ret_gn_w, v_ret_gn_b, v_w_branch_mla, v_w_branch_ret, v_w_out, v_final_norm_w):
    j = 2 * lax.axis_index("x") + lax.axis_index("y")
    tr = lambda w: w[0].T.reshape(-1, D).astype(BF16)
    pos = jnp.stack([j, lax.axis_index("c")]).astype(jnp.int32)
    put = lambda buf, rows, off: lax.dynamic_update_slice(buf, rows, (off, 0))
    big0 = lax.empty((BIG_ROWS, D), BF16)
    big0 = put(big0, w_out[0].astype(BF16), BIG_OUT + 512 * j)
    big0 = put(big0, tr(w_branch_mla), BIG_BM + 256 * j)
    big0 = put(big0, tr(w_branch_ret), BIG_BR + 256 * j)
    big0 = put(big0, tr(mla_w_uq), BIG_UQ + 96 * j)
    big0 = put(big0, tr(mla_w_ukv), BIG_UKV + 64 * j)
    big0 = put(big0, jnp.zeros((PK_ROWS - PK_PAD, D), BF16), BIG_JUNK + 16 * j)
    cuts, dest = _in_cuts()
    w_in_t = tr(w_in)
    dest = jnp.asarray(dest, jnp.int32)[j]
    for k in range(len(cuts) - 1):
        big0 = put(big0, w_in_t[cuts[k]:cuts[k + 1]], dest[k])
    big0 = put(big0, jnp.zeros((ZERO_ROWS, D), BF16), BIG_IN + IN_WIDTH)
    big, meta_all = _allgather(big0, meta)
    meta_full = meta_all.transpose(1, 0, 2).reshape(N_META, D)
    hp0 = jnp.concatenate([jnp.zeros((PAD_FRONT, D), F32), meta_full], axis=0)
    small_w = (norm_w, mla_q_norm_w, mla_kv_norm_w, ret_gn_w, ret_gn_b, final_norm_w.reshape(1, D))

    def reduce_start(gbig):
        chip_part = _rs_core_add(gbig, _rs_core_exchange(gbig), pos)
        sems, part_thru, land_thru, token = _rs_chip_start(chip_part)
        return token, (sems, part_thru, land_thru)

    def reduce_wait(state, after):
        return _rs_chip_wait(*state, after)

    grad_x, (chip_part, land2), small = _local_step(x[0], loss_target[0], hp0, big, small_w, reduce_start, reduce_wait)
    full, small_all = _rs_finish(_rs_chip_add(chip_part, land2, pos), small)
    full = full.reshape(PK_ROWS, D)

    untr = lambda lo, hi, rows: full[lo:hi].reshape(rows, -1).T
    grads = {
        "w_out": full[PK_OUT:PK_BM], "w_branch_mla": untr(PK_BM, PK_BR, 512), "w_branch_ret": untr(PK_BR, PK_UQ, 512),
        "mla_w_uq": untr(PK_UQ, PK_UKV, 384), "mla_w_ukv": untr(PK_UKV, PK_IN, 512), "w_in": untr(PK_IN, PK_PAD, IN_SHARD),
    }
    big_w = {"w_in": (w_in, m_w_in, v_w_in), "mla_w_uq": (mla_w_uq, m_mla_w_uq, v_mla_w_uq),
             "mla_w_ukv": (mla_w_ukv, m_mla_w_ukv, v_mla_w_ukv),
             "w_branch_mla": (w_branch_mla, m_w_branch_mla, v_w_branch_mla),
             "w_branch_ret": (w_branch_ret, m_w_branch_ret, v_w_branch_ret), "w_out": (w_out, m_w_out, v_w_out)}
    res = {}
    for name, (w, m, v) in big_w.items():
        d, nm, nv = _adamw(w[0], grads[name], m[0], v[0], "adamw_" + name)
        res[name] = (grads[name][None], d[None], nm[None], nv[None])

    small_m = (m_norm_w, m_mla_q_norm_w, m_mla_kv_norm_w, m_ret_gn_w, m_ret_gn_b, m_final_norm_w.reshape(1, D))
    small_v = (v_norm_w, v_mla_q_norm_w, v_mla_kv_norm_w, v_ret_gn_w, v_ret_gn_b, v_final_norm_w.reshape(1, D))
    gs, ds, ms, vs = _adamw_small(_small_rows(small_w), small_all, _small_rows(small_m), _small_rows(small_v))
    names = ["norm_w", "mla_q_norm_w", "mla_kv_norm_w", "ret_gn_w", "ret_gn_b", "final_norm_w"]
    bounds = [SM_NORM, SM_QN, SM_KVN, SM_GNW, SM_GNB, SM_FIN]
    for k, name in enumerate(names):
        shape = (D,) if name == "final_norm_w" else (1, -1)
        rows = small_w[k].size // 128
        res[name] = tuple(a[bounds[k]:bounds[k] + rows].reshape(shape) for a in (gs, ds, ms, vs))
    g_meta = lax.dynamic_slice_in_dim(gs[SM_META:SM_META + 256].reshape(N_META, D), j * (D // N_CHIPS), D // N_CHIPS, axis=1)
    res["meta"] = (g_meta,) + tuple(_adamw(meta, g_meta, m_meta, v_meta, "adamw_meta"))

    order = ["meta", "norm_w", "w_in", "mla_q_norm_w", "mla_w_uq", "mla_kv_norm_w", "mla_w_ukv", "ret_gn_w", "ret_gn_b",
             "w_branch_mla", "w_branch_ret", "w_out", "final_norm_w"]
    return (gs[0, 0], grad_x[None]) + tuple(res[n][k] for k in range(4) for n in order)
```

```python
import functools
import math

import numpy as np
import jax
import jax.numpy as jnp
from jax import lax
from jax.experimental import pallas as pl
from jax.experimental.pallas import tpu as pltpu

F32 = jnp.float32
BF16 = jnp.bfloat16
MESH = pl.DeviceIdType.MESH

D = 2048
N_META = 16
BLK = 128
PAD_FRONT = BLK - N_META
HEADS = 8
DH = 128
ROPE_D = 64
Q_RANK = 512
KV_RANK = 256
WIDTH = HEADS * DH
ROPE_BASE = 10000.0
NORM_EPS = 1e-6
GN_EPS = 1e-5
NEG_INF = -1e30
ATT_SCALE = (DH + ROPE_D) ** -0.5
RET_SCALE = DH ** -0.5
IN_WIDTH = 10048
N_CHIPS = 4
IN_SHARD = IN_WIDTH // N_CHIPS
ADAM_LR, ADAM_B1, ADAM_B2, ADAM_EPS, ADAM_WD, ADAM_STEP = 0.001, 0.9, 0.999, 1e-08, 0.01, 10

R_Q, R_K, R_V, Z_RET, Z_MLA, GATE0, GATE1 = 0, 1024, 2048, 3072, 4096, 5120, 5120 + D
C_Q = 5120 + 2 * D
C_KV = C_Q + Q_RANK
K_PE = C_KV + KV_RANK
PROJ_W = 10240
IN_RUNS = ((0, 832, C_Q), (832, 1856, Z_MLA), (1856, 4928, R_Q), (4928, 5952, Z_RET), (5952, IN_WIDTH, GATE0))

BIG_OUT, BIG_BM, BIG_BR, BIG_UQ, BIG_UKV, BIG_JUNK, BIG_IN = 0, 2048, 3072, 4096, 4480, 4736, 5120
BIG_ROWS = BIG_IN + PROJ_W
ZERO_ROWS = PROJ_W - IN_WIDTH
PK_OUT, PK_BM, PK_BR, PK_UQ, PK_UKV, PK_IN, PK_PAD, PK_ROWS = 0, 512, 768, 1024, 1120, 1184, 3696, 3712
PK_HALF = PK_ROWS // 2
SM_LOSS, SM_NORM, SM_QN, SM_KVN, SM_GNW, SM_GNB, SM_FIN, SM_META, SM_ROWS = 0, 8, 24, 32, 40, 48, 56, 72, 328

VMEM_LIMIT = 56 * 1024 * 1024


def _pieces(shard):
    j = shard
    out = [(PK_OUT, BIG_OUT + 512 * j, 512), (PK_BM, BIG_BM + 256 * j, 256), (PK_BR, BIG_BR + 256 * j, 256),
           (PK_UQ, BIG_UQ + 96 * j, 96), (PK_UKV, BIG_UKV + 64 * j, 64)]
    for lo, hi, new in IN_RUNS:
        a, b = max(lo, IN_SHARD * j), min(hi, IN_SHARD * (j + 1))
        if b > a:
            out.append((PK_IN + a - IN_SHARD * j, BIG_IN + new + a - lo, b - a))
    out += [(PK_PAD, BIG_JUNK + 16 * j, 16)]
    return out


def _in_cuts():
    cuts = sorted({0, IN_SHARD} | {b - IN_SHARD * j for j in range(N_CHIPS) for lo, hi, _ in IN_RUNS for b in (lo, hi)
                                   if 0 < b - IN_SHARD * j < IN_SHARD})

    def new_row(col):
        return next(BIG_IN + new + col - lo for lo, hi, new in IN_RUNS if lo <= col < hi)

    return cuts, [[new_row(IN_SHARD * j + c) for c in cuts[:-1]] for j in range(N_CHIPS)]


AG_JUNK_REST, AG_JUNK_IN, AG_JUNK_END = 4736, 4864, 4928
AG_REST_HALF, AG_IN_HALF = 608, 1264


def _ag_half(shard, w_in, half):
    pieces = [(b, n) for p, b, n in _pieces(shard) if p < PK_PAD and (p >= PK_IN) == w_in]
    pieces.append((AG_JUNK_IN + 16 * shard, 16) if w_in else (AG_JUNK_REST + 32 * shard, 32))
    size = AG_IN_HALF if w_in else AG_REST_HALF
    out, pos = [], 0
    for b, n in pieces:
        s, e = max(pos, half * size), min(pos + n, (half + 1) * size)
        if e > s:
            out.append((b + s - pos, e - s))
        pos += n
    assert pos == 2 * size
    return out


def _half_pieces(shard, half):
    lo, hi = half * PK_HALF, (half + 1) * PK_HALF
    out = []
    for p, b, n in _pieces(shard):
        s, e = max(p, lo), min(p + n, hi)
        if e > s:
            out.append((s, b + (s - p), e - s))
    return out


def _row_tile(rows, cap):
    best = BLK
    for t in range(BLK, cap + 1, BLK):
        if rows % t == 0:
            best = t
    return best


def _cparams(sem):
    return pltpu.CompilerParams(dimension_semantics=sem, vmem_limit_bytes=VMEM_LIMIT)


def _dot(a, b, form):
    dn = {"nt": (((1,), (1,)), ((), ())), "nn": (((1,), (0,)), ((), ())), "tn": (((0,), (0,)), ((), ()))}[form]
    return lax.dot_general(a, b, dn, preferred_element_type=F32)


def _sigmoid(v):
    return 1.0 / (1.0 + jnp.exp(-v))


def _matmul(a, b, form, m, n, k, tm, tn, tk, out_dtype, name, a_off=(0, 0), b_off=(0, 0), n_outer=False, after=None):
    nk = k // tk
    gi, gj = m // tm, n // tn

    def ij(g0, g1):
        return (g1, g0) if n_outer else (g0, g1)

    if form == "nt":
        a_spec = pl.BlockSpec((tm, tk), lambda g0, g1, kk: (ij(g0, g1)[0] + a_off[0], kk + a_off[1]))
        b_spec = pl.BlockSpec((tn, tk), lambda g0, g1, kk: (ij(g0, g1)[1] + b_off[0], kk + b_off[1]))
    elif form == "nn":
        a_spec = pl.BlockSpec((tm, tk), lambda g0, g1, kk: (ij(g0, g1)[0] + a_off[0], kk + a_off[1]))
        b_spec = pl.BlockSpec((tk, tn), lambda g0, g1, kk: (kk + b_off[0], ij(g0, g1)[1] + b_off[1]))
    else:
        a_spec = pl.BlockSpec((tk, tm), lambda g0, g1, kk: (kk + a_off[0], ij(g0, g1)[0] + a_off[1]))
        b_spec = pl.BlockSpec((tk, tn), lambda g0, g1, kk: (kk + b_off[0], ij(g0, g1)[1] + b_off[1]))
    o_spec = pl.BlockSpec((tm, tn), lambda g0, g1, kk: ij(g0, g1))

    def body(a_ref, b_ref, *rest):
        o_ref, *acc = rest[0 if after is None else 1:]
        p = _dot(a_ref[...], b_ref[...], form)
        if nk == 1:
            o_ref[...] = p.astype(o_ref.dtype)
        else:
            acc_ref, = acc
            kk = pl.program_id(2)

            @pl.when(kk == 0)
            def _():
                acc_ref[...] = p

            @pl.when(kk > 0)
            def _():
                acc_ref[...] += p

            @pl.when(kk == nk - 1)
            def _():
                o_ref[...] = acc_ref[...].astype(o_ref.dtype)

    extra = [] if after is None else [after]
    return pl.pallas_call(
        body, name=name, grid=(gj, gi, nk) if n_outer else (gi, gj, nk),
        in_specs=[a_spec, b_spec] + [pl.BlockSpec(memory_space=pl.ANY)] * len(extra), out_specs=o_spec,
        out_shape=jax.ShapeDtypeStruct((m, n), out_dtype),
        scratch_shapes=[] if nk == 1 else [pltpu.VMEM((tm, tn), F32)],
        compiler_params=_cparams(("parallel", "parallel", "arbitrary")),
    )(a, b, *extra)


def _rms_in(x, hp0, norm_w, s):
    def body(x_ref, hp0_ref, w_ref, h_ref, xn_ref, r_ref):
        def run(hv):
            r = lax.rsqrt(jnp.mean(hv * hv, axis=-1, keepdims=True) + NORM_EPS)
            h_ref[...] = hv
            xn_ref[...] = (hv * r * w_ref[...]).astype(BF16)
            r_ref[...] = r

        @pl.when(pl.program_id(0) == 0)
        def _():
            run(hp0_ref[...])

        @pl.when(pl.program_id(0) > 0)
        def _():
            run(x_ref[...])

    return pl.pallas_call(
        body, name="rms_in", grid=(s // BLK,),
        in_specs=[pl.BlockSpec((BLK, D), lambda i: (jnp.maximum(i - 1, 0), 0)),
                  pl.BlockSpec((BLK, D), lambda i: (0, 0)), pl.BlockSpec((1, D), lambda i: (0, 0))],
        out_specs=[pl.BlockSpec((BLK, D), lambda i: (i, 0)), pl.BlockSpec((BLK, D), lambda i: (i, 0)),
                   pl.BlockSpec((BLK, 1), lambda i: (i, 0))],
        out_shape=[jax.ShapeDtypeStruct((s, D), F32), jax.ShapeDtypeStruct((s, D), BF16),
                   jax.ShapeDtypeStruct((s, 1), F32)],
        compiler_params=_cparams(("arbitrary",)),
    )(x, hp0, norm_w)


def _rope64(t, c, sa, sb):
    return t * c + pltpu.roll(t, t.shape[1] - ROPE_D // 2, 1) * sa + pltpu.roll(t, ROPE_D // 2, 1) * sb


def _rope128(t, c, sg):
    return t * c + pltpu.roll(t, DH // 2, 1) * sg


HD = 2 * DH


def _mla_prep(proj, qn_w, kvn_w, wq, wukv, qtab, ktab, s):
    tm = _row_tile(s, 384)

    def body(cq_ref, ckv_ref, kpe_ref, qnw_ref, kvnw_ref, wq_ref, wukv_ref, qc_ref, qa_ref, qb_ref,
             kc_ref, ka_ref, kb_ref, cqn_ref, rq_ref, ckvn_ref, rkv_ref, qq_ref, kk_ref, vv_ref):
        cq = cq_ref[...].astype(F32)
        rq = lax.rsqrt(jnp.mean(cq * cq, axis=-1, keepdims=True) + NORM_EPS)
        cqn = (cq * rq * qnw_ref[...]).astype(BF16)
        ckv = ckv_ref[...].astype(F32)
        rkv = lax.rsqrt(jnp.mean(ckv * ckv, axis=-1, keepdims=True) + NORM_EPS)
        ckvn = (ckv * rkv * kvnw_ref[...]).astype(BF16)
        cqn_ref[...] = cqn
        rq_ref[...] = rq
        ckvn_ref[...] = ckvn
        rkv_ref[...] = rkv
        q = _dot(cqn, wq_ref[...], "nt")
        kv = _dot(ckvn, wukv_ref[...], "nt")
        kp = _rope64(kpe_ref[...].astype(F32), kc_ref[...], ka_ref[...], kb_ref[...]).astype(BF16)
        qc, qa, qb = qc_ref[...], qa_ref[...], qb_ref[...]
        ones = jnp.ones((tm, DH), BF16)
        for h in range(HEADS):
            lo, mid, hi = h * HD, h * HD + DH, (h + 1) * HD
            qq_ref[:, lo:hi] = _rope64(q[:, lo:hi], qc, qa, qb).astype(BF16)
            kk_ref[:, lo:mid] = kv[:, lo:mid].astype(BF16)
            kk_ref[:, mid:hi] = kp
            vv_ref[:, lo:mid] = kv[:, mid:hi].astype(BF16)
            vv_ref[:, mid:hi] = ones

    row = lambda w, cb: pl.BlockSpec((tm, w), lambda i: (i, cb))
    full = lambda a: pl.BlockSpec(a.shape, lambda i: (0, 0))
    wide = jax.ShapeDtypeStruct((s, HEADS * HD), BF16)
    return pl.pallas_call(
        body, name="mla_prep", grid=(s // tm,),
        in_specs=[row(Q_RANK, C_Q // Q_RANK), row(KV_RANK, C_KV // KV_RANK), row(DH, K_PE // DH),
                  full(qn_w), full(kvn_w), full(wq), full(wukv), row(HD, 0), row(HD, 0), row(HD, 0),
                  row(DH, 0), row(DH, 0), row(DH, 0)],
        out_specs=[row(Q_RANK, 0), row(1, 0), row(KV_RANK, 0), row(1, 0), row(HEADS * HD, 0), row(HEADS * HD, 0),
                   row(HEADS * HD, 0)],
        out_shape=[jax.ShapeDtypeStruct((s, Q_RANK), BF16), jax.ShapeDtypeStruct((s, 1), F32),
                   jax.ShapeDtypeStruct((s, KV_RANK), BF16), jax.ShapeDtypeStruct((s, 1), F32), wide, wide, wide],
        compiler_params=_cparams(("parallel",)),
    )(proj, proj, proj, qn_w, kvn_w, wq, wukv, *qtab, *ktab)


def _att_mask(qi, ki, t):
    key = ki * t + lax.broadcasted_iota(jnp.int32, (t, t), 0)
    qry = qi * t + lax.broadcasted_iota(jnp.int32, (t, t), 1)
    return (key <= qry) & (key >= PAD_FRONT)


def _silu(z):
    return z * _sigmoid(z)


def _attn_fwd(qq, kk, vv, proj, s):
    t = _row_tile(s, 384)
    n = s // t

    def body(q_ref, k_ref, v_ref, z_ref, o_ref, y_ref, lse_ref, acc_ref, m_ref):
        qi = pl.program_id(1)
        q = q_ref[...]
        m_ref[...] = jnp.full(m_ref.shape, NEG_INF, F32)
        acc_ref[...] = jnp.zeros(acc_ref.shape, F32)

        def keys(ki):
            return pl.ds(pl.multiple_of(ki * t, t), t)

        def scores(ki):
            return _dot(k_ref[keys(ki), :], q, "nt")

        def tile(ki, st, masked):
            if masked:
                st = jnp.where(_att_mask(qi, ki, t), st, NEG_INF)
            m_old = m_ref[...]
            m_new = jnp.maximum(m_old, jnp.max(st, axis=0, keepdims=True))
            pt = jnp.exp(st - m_new).astype(BF16)
            acc_ref[...] = jnp.exp(m_old - m_new) * acc_ref[...] + _dot(v_ref[keys(ki), :], pt, "tn")
            m_ref[...] = m_new

        st_first = scores(0)
        st_next = scores(jnp.minimum(1, n - 1))
        tile(0, st_first, True)

        def inner(ki, st):
            st_after = scores(ki + 1)
            tile(ki, st, False)
            return st_after

        st_diag = lax.fori_loop(1, qi, inner, st_next)

        @pl.when(qi > 0)
        def _():
            tile(qi, st_diag, True)

        l = acc_ref[DH:DH + 1, :]
        o = (acc_ref[:DH, :] / l).T
        o_ref[...] = o.astype(BF16)
        y_ref[...] = (o * _silu(z_ref[...].astype(F32))).astype(BF16)
        lse_ref[0, 0] = m_ref[...] + jnp.log(l)

    qtile = pl.BlockSpec((t, DH), lambda h, i: (i, h))
    head = pl.BlockSpec((s, HD), lambda h, i: (0, h))
    return pl.pallas_call(
        body, name="attn_fwd", grid=(HEADS, s // t),
        in_specs=[pl.BlockSpec((t, HD), lambda h, i: (i, h)), head, head,
                  pl.BlockSpec((t, DH), lambda h, i: (i, Z_MLA // DH + h))],
        out_specs=[qtile, qtile, pl.BlockSpec((1, 1, 1, t), lambda h, i: (h, i, 0, 0))],
        out_shape=[jax.ShapeDtypeStruct((s, WIDTH), BF16), jax.ShapeDtypeStruct((s, WIDTH), BF16),
                   jax.ShapeDtypeStruct((HEADS, s // t, 1, t), F32)],
        scratch_shapes=[pltpu.VMEM((HD, t), F32), pltpu.VMEM((1, t), F32)],
        compiler_params=_cparams(("parallel", "arbitrary")),
    )(qq, kk, vv, proj)


def _ret_consts():
    log_g = np.log1p(-(2.0 ** (-5.0 - np.arange(HEADS, dtype=np.float64))))
    n = np.arange(BLK, dtype=np.float64)
    diff = n[:, None] - n[None, :]
    decay = np.where(diff >= 0, np.exp(log_g[:, None, None] * np.maximum(diff, 0.0)), 0.0)
    zeta = np.exp(log_g[:, None] * (BLK - 1.0 - n))[:, :, None]
    xi = np.exp(log_g[:, None] * (n + 1.0))[:, :, None]
    gamma = [float(np.float32(np.exp(g * BLK))) for g in log_g]
    return (jnp.asarray(decay, F32), jnp.asarray(zeta, F32), jnp.asarray(xi, F32)), gamma


def _ret_fwd(proj, gn_w, gn_b, c128, s128, consts, gamma, s):
    nb = s // BLK
    decay, zeta, xi = consts

    def body(rq_ref, rk_ref, rv_ref, z_ref, gw_ref, gb_ref, c_ref, s_ref, dm_ref, zt_ref, xi_ref,
             y_ref, on_ref, rstd_ref, qr_ref, kr_ref, rall_ref, state):
        @pl.when(pl.program_id(0) == 0)
        def _():
            state[...] = jnp.zeros_like(state)

        c, sg = c_ref[...], s_ref[...]
        for h in range(HEADS):
            sl = slice(h * DH, (h + 1) * DH)
            q = _rope128(rq_ref[:, sl].astype(F32), c, sg).astype(BF16)
            kf = _rope128(rk_ref[:, sl].astype(F32), c, sg) * RET_SCALE
            k = kf.astype(BF16)
            v = rv_ref[:, sl]
            qr_ref[:, sl] = q
            kr_ref[:, sl] = k
            r_prev = state[h]
            rall_ref[0, h] = r_prev
            a = _dot(q, k, "nt") * dm_ref[h]
            o = _dot(a.astype(BF16), v, "nn") + _dot(q, r_prev.astype(BF16), "nn") * xi_ref[h]
            state[h] = r_prev * gamma[h] + _dot((kf * zt_ref[h]).astype(BF16), v, "tn")
            mu = jnp.mean(o, axis=-1, keepdims=True)
            var = jnp.mean(jnp.square(o - mu), axis=-1, keepdims=True)
            rstd = lax.rsqrt(var + GN_EPS)
            on = (o - mu) * rstd
            rstd_ref[h] = rstd
            on_ref[:, sl] = on.astype(BF16)
            ogn = on * gw_ref[:, sl] + gb_ref[:, sl]
            y_ref[:, sl] = (ogn * _silu(z_ref[:, sl].astype(F32))).astype(BF16)

    seg = lambda cb: pl.BlockSpec((BLK, WIDTH), lambda i: (i, cb))
    full = lambda a: pl.BlockSpec(a.shape, lambda i: (0,) * a.ndim)
    tab = pl.BlockSpec((BLK, DH), lambda i: (i, 0))
    return pl.pallas_call(
        body, name="ret_fwd", grid=(nb,),
        in_specs=[seg(R_Q // WIDTH), seg(R_K // WIDTH), seg(R_V // WIDTH), seg(Z_RET // WIDTH), full(gn_w), full(gn_b),
                  tab, tab, full(decay), full(zeta), full(xi)],
        out_specs=[seg(0), seg(0), pl.BlockSpec((HEADS, BLK, 1), lambda i: (0, i, 0)), seg(0), seg(0),
                   pl.BlockSpec((1, HEADS, DH, DH), lambda i: (i, 0, 0, 0))],
        out_shape=[jax.ShapeDtypeStruct((s, WIDTH), BF16), jax.ShapeDtypeStruct((s, WIDTH), BF16),
                   jax.ShapeDtypeStruct((HEADS, s, 1), F32), jax.ShapeDtypeStruct((s, WIDTH), BF16),
                   jax.ShapeDtypeStruct((s, WIDTH), BF16), jax.ShapeDtypeStruct((nb, HEADS, DH, DH), F32)],
        scratch_shapes=[pltpu.VMEM((HEADS, DH, DH), F32)],
        compiler_params=_cparams(("arbitrary",)),
    )(proj, proj, proj, proj, gn_w, gn_b, c128, s128, decay, zeta, xi)


def _merge_fwd(y_mla, y_ret, wbm, wbr, proj, s):
    tm, tn = _row_tile(s, 1408), 512

    def body(ym_ref, yr_ref, wm_ref, wr_ref, g0_ref, g1_ref, um_ref, ur_ref, mg_ref):
        um = _dot(ym_ref[...], wm_ref[...], "nt")
        ur = _dot(yr_ref[...], wr_ref[...], "nt")
        um_ref[...] = um.astype(BF16)
        ur_ref[...] = ur.astype(BF16)
        mg_ref[...] = (_sigmoid(g0_ref[...].astype(F32)) * um + _sigmoid(g1_ref[...].astype(F32)) * ur).astype(BF16)

    yspec = pl.BlockSpec((tm, WIDTH), lambda i, j: (i, 0))
    wspec = pl.BlockSpec((tn, WIDTH), lambda i, j: (j, 0))
    ospec = pl.BlockSpec((tm, tn), lambda i, j: (i, j))
    return pl.pallas_call(
        body, name="merge_fwd", grid=(s // tm, D // tn),
        in_specs=[yspec, yspec, wspec, wspec, pl.BlockSpec((tm, tn), lambda i, j: (i, GATE0 // tn + j)),
                  pl.BlockSpec((tm, tn), lambda i, j: (i, GATE1 // tn + j))],
        out_specs=[ospec, ospec, ospec],
        out_shape=[jax.ShapeDtypeStruct((s, D), BF16)] * 3,
        compiler_params=_cparams(("parallel", "parallel")),
    )(y_mla, y_ret, wbm, wbr, proj, proj)


def _out_fwd(merged, big, h, s):
    tm, tn = _row_tile(s, 1408), 512

    def body(m_ref, w_ref, h_ref, o_ref):
        o_ref[...] = h_ref[...] + _dot(m_ref[...], w_ref[...], "nn")

    return pl.pallas_call(
        body, name="out_fwd", grid=(s // tm, D // tn),
        in_specs=[pl.BlockSpec((tm, D), lambda i, j: (i, 0)), pl.BlockSpec((D, tn), lambda i, j: (BIG_OUT // D, j)),
                  pl.BlockSpec((tm, tn), lambda i, j: (i, j))],
        out_specs=pl.BlockSpec((tm, tn), lambda i, j: (i, j)),
        out_shape=jax.ShapeDtypeStruct((s, D), F32),
        compiler_params=_cparams(("parallel", "parallel")),
    )(merged, big, h)


def _loss_bwd(h2, target, fin_w, s):
    nb = s // BLK

    def body(h2_ref, t_ref, w_ref, dh_ref, dhb_ref, loss_ref, dw_ref):
        i = pl.program_id(0)

        @pl.when(i == 0)
        def _():
            dh_ref[...] = jnp.zeros_like(dh_ref)
            dhb_ref[...] = jnp.zeros_like(dhb_ref)
            loss_ref[...] = jnp.zeros_like(loss_ref)
            dw_ref[...] = jnp.zeros_like(dw_ref)

        @pl.when(i > 0)
        def _():
            hv = h2_ref[...]
            w = w_ref[...]
            r = lax.rsqrt(jnp.mean(hv * hv, axis=-1, keepdims=True) + NORM_EPS)
            nrm = hv * r
            e = nrm * w - t_ref[...]
            loss_ref[...] += jnp.full(loss_ref.shape, 0.5 / D, F32) * jnp.sum(e * e)
            dy = e * (1.0 / D)
            dw_ref[...] += jnp.sum(dy * nrm, axis=0, keepdims=True)
            g = dy * w
            dh = r * (g - nrm * jnp.mean(g * nrm, axis=-1, keepdims=True))
            dh_ref[...] = dh
            dhb_ref[...] = dh.astype(BF16)

    blk = pl.BlockSpec((BLK, D), lambda i: (i, 0))
    return pl.pallas_call(
        body, name="loss_bwd", grid=(nb,),
        in_specs=[blk, pl.BlockSpec((BLK, D), lambda i: (jnp.maximum(i - 1, 0), 0)), pl.BlockSpec((1, D), lambda i: (0, 0))],
        out_specs=[blk, blk, pl.BlockSpec((8, 128), lambda i: (0, 0)), pl.BlockSpec((1, D), lambda i: (0, 0))],
        out_shape=[jax.ShapeDtypeStruct((s, D), F32), jax.ShapeDtypeStruct((s, D), BF16),
                   jax.ShapeDtypeStruct((8, 128), F32), jax.ShapeDtypeStruct((1, D), F32)],
        compiler_params=_cparams(("arbitrary",)),
    )(h2, target, fin_w)


def _merge_bwd(dh2b, big, u_mla, u_ret, proj, s):
    tm, tn = _row_tile(s, 1408), 512

    def body(d_ref, w_ref, um_ref, ur_ref, gate_ref, dproj_ref, dum_ref, dur_ref, dm_ref):
        branch = pl.program_id(2)

        @pl.when(branch == 0)
        def _():
            dm_ref[...] = _dot(d_ref[...], w_ref[...], "nt")

        dm = dm_ref[...]
        gt = _sigmoid(gate_ref[...].astype(F32))

        @pl.when(branch == 0)
        def _():
            dproj_ref[...] = (dm * um_ref[...].astype(F32) * gt * (1.0 - gt)).astype(BF16)
            dum_ref[...] = (dm * gt).astype(BF16)

        @pl.when(branch == 1)
        def _():
            dproj_ref[...] = (dm * ur_ref[...].astype(F32) * gt * (1.0 - gt)).astype(BF16)
            dur_ref[...] = (dm * gt).astype(BF16)

    ospec = pl.BlockSpec((tm, tn), lambda i, j, b: (i, j))
    gate = pl.BlockSpec((tm, tn), lambda i, j, b: (i, GATE0 // tn + b * (D // tn) + j))
    return pl.pallas_call(
        body, name="merge_bwd", grid=(s // tm, D // tn, 2),
        in_specs=[pl.BlockSpec((tm, D), lambda i, j, b: (i, 0)),
                  pl.BlockSpec((tn, D), lambda i, j, b: (BIG_OUT // tn + j, 0)), ospec, ospec, gate],
        out_specs=[gate, ospec, ospec],
        out_shape=[jax.ShapeDtypeStruct((s, PROJ_W), BF16), jax.ShapeDtypeStruct((s, D), BF16),
                   jax.ShapeDtypeStruct((s, D), BF16)],
        scratch_shapes=[pltpu.VMEM((tm, tn), F32)],
        compiler_params=_cparams(("parallel", "parallel", "arbitrary")),
    )(dh2b, big, u_mla, u_ret, proj)


def _dsilu(z):
    sg = _sigmoid(z)
    return sg * (1.0 + z * (1.0 - sg))


def _branch_mla_bwd(dproj, du, wbm, o_mla, proj, s):
    tm = _row_tile(s, 384)

    def body(dproj_in, du_ref, w_ref, o_ref, z_ref, dz_ref, do_ref, delta_ref):
        del dproj_in
        dy = _dot(du_ref[...], w_ref[...], "nn")
        z = z_ref[...].astype(F32)
        o = o_ref[...].astype(F32)
        do = dy * _silu(z)
        do_ref[...] = do.astype(BF16)
        dz_ref[...] = (dy * o * _dsilu(z)).astype(BF16)
        prod = do * o
        for h in range(HEADS):
            delta_ref[h] = jnp.sum(prod[:, h * DH:(h + 1) * DH], axis=-1, keepdims=True)

    row = lambda w, cb: pl.BlockSpec((tm, w), lambda i: (i, cb))
    return pl.pallas_call(
        body, name="branch_mla_bwd", grid=(s // tm,),
        in_specs=[ANY, row(D, 0), pl.BlockSpec((D, WIDTH), lambda i: (0, 0)), row(WIDTH, 0),
                  row(WIDTH, Z_MLA // WIDTH)],
        out_specs=[row(WIDTH, Z_MLA // WIDTH), row(WIDTH, 0), pl.BlockSpec((HEADS, tm, 1), lambda i: (0, i, 0))],
        out_shape=[jax.ShapeDtypeStruct((s, PROJ_W), BF16), jax.ShapeDtypeStruct((s, WIDTH), BF16),
                   jax.ShapeDtypeStruct((HEADS, s, 1), F32)],
        input_output_aliases={0: 0},
        compiler_params=_cparams(("parallel",)),
    )(dproj, du, wbm, o_mla, proj)


def _branch_ret_bwd(dproj, du, wbr, on, rstd, gn_w, gn_b, proj, s):
    tm = BLK

    def body(dproj_in, du_ref, w_ref, on_ref, rstd_ref, gw_ref, gb_ref, z_ref, dz_ref, do_ref, dgw_ref, dgb_ref):
        del dproj_in

        @pl.when(pl.program_id(0) == 0)
        def _():
            dgw_ref[...] = jnp.zeros_like(dgw_ref)
            dgb_ref[...] = jnp.zeros_like(dgb_ref)

        dy = _dot(du_ref[...], w_ref[...], "nn")
        z = z_ref[...].astype(F32)
        on = on_ref[...].astype(F32)
        gw = gw_ref[...]
        dogn = dy * _silu(z)
        dz_ref[...] = (dy * (on * gw + gb_ref[...]) * _dsilu(z)).astype(BF16)
        dgw_ref[...] += jnp.sum(dogn * on, axis=0, keepdims=True)
        dgb_ref[...] += jnp.sum(dogn, axis=0, keepdims=True)
        don = dogn * gw
        for h in range(HEADS):
            sl = slice(h * DH, (h + 1) * DH)
            dn, nh = don[:, sl], on[:, sl]
            do = rstd_ref[h] * (dn - jnp.mean(dn, axis=-1, keepdims=True)
                                - nh * jnp.mean(dn * nh, axis=-1, keepdims=True))
            do_ref[:, sl] = do.astype(BF16)

    row = lambda w, cb: pl.BlockSpec((tm, w), lambda i: (i, cb))
    vec = pl.BlockSpec((1, WIDTH), lambda i: (0, 0))
    return pl.pallas_call(
        body, name="branch_ret_bwd", grid=(s // tm,),
        in_specs=[ANY, row(D, 0), pl.BlockSpec((D, WIDTH), lambda i: (0, 0)), row(WIDTH, 0),
                  pl.BlockSpec((HEADS, tm, 1), lambda i: (0, i, 0)), vec, vec, row(WIDTH, Z_RET // WIDTH)],
        out_specs=[row(WIDTH, Z_RET // WIDTH), row(WIDTH, 0), vec, vec],
        out_shape=[jax.ShapeDtypeStruct((s, PROJ_W), BF16), jax.ShapeDtypeStruct((s, WIDTH), BF16)]
        + [jax.ShapeDtypeStruct((1, WIDTH), F32)] * 2,
        input_output_aliases={0: 0},
        compiler_params=_cparams(("arbitrary",)),
    )(dproj, du, wbr, on, rstd, gn_w, gn_b, proj)


def _ret_bwd(dproj, qr, kr, proj, rall, do_ret, c128, s128, consts, gamma, s):
    nb = s // BLK
    decay, zeta, xi = consts

    def body(dproj_in, q_ref, k_ref, v_ref, r_ref, do_ref, c_ref, s_ref, dm_ref, zt_ref, xi_ref, out_ref, gstate):
        del dproj_in
        dq_ref, dk_ref, dv_ref = (out_ref.at[:, pl.ds(off, WIDTH)] for off in (R_Q, R_K, R_V))

        @pl.when(pl.program_id(0) == 0)
        def _():
            gstate[...] = jnp.zeros_like(gstate)

        c, sg = c_ref[...], s_ref[...]
        for h in range(HEADS):
            sl = slice(h * DH, (h + 1) * DH)
            q, k, v, do = q_ref[:, sl], k_ref[:, sl], v_ref[:, sl], do_ref[:, sl]
            dm = dm_ref[h]
            g_next = gstate[h]
            gb = g_next.astype(BF16)
            a = (_dot(q, k, "nt") * dm).astype(BF16)
            da = (_dot(do, v, "nt") * dm).astype(BF16)
            dox = (do.astype(F32) * xi_ref[h]).astype(BF16)
            dq = _dot(da, k, "nn") + _dot(dox, r_ref[0, h].astype(BF16), "nt")
            dk = _dot(da, q, "tn") + _dot(v, gb, "nt") * zt_ref[h]
            kz = (k.astype(F32) * zt_ref[h]).astype(BF16)
            dv = _dot(a, do, "tn") + _dot(kz, gb, "nn")
            gstate[h] = g_next * gamma[h] + _dot(q, dox, "tn")
            dk = dk * RET_SCALE
            dq_ref[:, sl] = _rope128(dq, c, -sg).astype(BF16)
            dk_ref[:, sl] = _rope128(dk, c, -sg).astype(BF16)
            dv_ref[:, sl] = dv.astype(BF16)

    rev = lambda cb: pl.BlockSpec((BLK, WIDTH), lambda i: (nb - 1 - i, cb))
    full = lambda a: pl.BlockSpec(a.shape, lambda i: (0,) * a.ndim)
    tab = pl.BlockSpec((BLK, DH), lambda i: (nb - 1 - i, 0))
    return pl.pallas_call(
        body, name="ret_bwd", grid=(nb,),
        in_specs=[ANY, rev(0), rev(0), rev(R_V // WIDTH),
                  pl.BlockSpec((1, HEADS, DH, DH), lambda i: (nb - 1 - i, 0, 0, 0)),
                  rev(0), tab, tab, full(decay), full(zeta), full(xi)],
        out_specs=pl.BlockSpec((BLK, 3 * WIDTH), lambda i: (nb - 1 - i, R_Q // (3 * WIDTH))),
        out_shape=jax.ShapeDtypeStruct((s, PROJ_W), BF16),
        scratch_shapes=[pltpu.VMEM((HEADS, DH, DH), F32)],
        input_output_aliases={0: 0},
        compiler_params=_cparams(("arbitrary",)),
    )(dproj, qr, kr, proj, rall, do_ret, c128, s128, decay, zeta, xi)


def _attn_bwd(qq, kk, vv, do, lse, delta, s):
    t = _row_tile(s, 384)
    n = s // t

    def body(q_ref, k_ref, v_ref, do_ref, lse_ref, delta_ref, dq_ref, dk_ref, dv_ref, dk_acc, dv_acc):
        ki = pl.program_id(1)

        @pl.when(ki == 0)
        def _():
            dq_ref[...] = jnp.zeros(dq_ref.shape, F32)

        k, v = k_ref[...], v_ref[...]
        dk_acc[...] = jnp.zeros(dk_acc.shape, F32)
        dv_acc[...] = jnp.zeros(dv_acc.shape, F32)

        def tile(qi, masked):
            qs = pl.ds(pl.multiple_of(qi * t, t), t)
            q, dov = q_ref[qs, :], do_ref[qs, :]
            st = _dot(k, q, "nt")
            if masked:
                st = jnp.where(_att_mask(qi, ki, t), st, NEG_INF)
            pt = jnp.exp(st - lse_ref[0, qi])
            dv_acc[...] += _dot(pt.astype(BF16), dov, "nn")
            dst = (pt * (_dot(v, dov, "nt") - delta_ref[0, qi])).astype(BF16)
            dk_acc[...] += _dot(dst, q, "nn")
            dq_ref[qs, :] += _dot(dst, k, "tn")

        def masked_tile(qi, carry):
            tile(qi, True)
            return carry

        def plain_tile(qi, carry):
            tile(qi, False)
            return carry

        tile(ki, True)

        @pl.when(ki == 0)
        def _():
            lax.fori_loop(1, n, masked_tile, 0)

        @pl.when(ki > 0)
        def _():
            lax.fori_loop(ki + 1, n, plain_tile, 0)

        dk_ref[...] = dk_acc[...].astype(BF16)
        dv_ref[...] = dv_acc[...].astype(BF16)

    head = lambda w: pl.BlockSpec((s, w), lambda h, i: (0, h))
    stat = pl.BlockSpec((1, n, 1, t), lambda h, i: (h, 0, 0, 0))
    return pl.pallas_call(
        body, name="attn_bwd", grid=(HEADS, n),
        in_specs=[head(HD), pl.BlockSpec((t, HD), lambda h, i: (i, h)), pl.BlockSpec((t, DH), lambda h, i: (i, 2 * h)),
                  head(DH), stat, stat],
        out_specs=[head(HD), pl.BlockSpec((t, HD), lambda h, i: (i, h)), pl.BlockSpec((t, DH), lambda h, i: (i, h))],
        out_shape=[jax.ShapeDtypeStruct((s, HEADS * HD), F32), jax.ShapeDtypeStruct((s, HEADS * HD), BF16),
                   jax.ShapeDtypeStruct((s, WIDTH), BF16)],
        scratch_shapes=[pltpu.VMEM((t, HD), F32), pltpu.VMEM((t, DH), F32)],
        compiler_params=_cparams(("parallel", "arbitrary")),
    )(qq, kk, vv, do, lse, delta)


def _rms_bwd(dn, nhat, r, w):
    g = dn * w
    return r * (g - nhat * jnp.mean(g * nhat, axis=-1, keepdims=True)), dn * nhat


def _mla_prep_bwd(dproj, dqq, dkk, dvv, proj, rq, rkv, qn_w, kvn_w, wq, wukv, qtab, ktab, s):
    tm = _row_tile(s, 384)
    tail = PROJ_W - C_Q

    def body(dproj_in, dqq_ref, dkk_ref, dvv_ref, cq_ref, ckv_ref, rq_ref, rkv_ref, qnw_ref, kvnw_ref, wq_ref,
             wukv_ref, qc_ref, qa_ref, qb_ref, kc_ref, ka_ref, kb_ref,
             tail_ref, dq_ref, dkv_ref, dqnw_ref, dkvnw_ref):
        del dproj_in
        dcq_ref = tail_ref.at[:, pl.ds(0, Q_RANK)]
        dckv_ref = tail_ref.at[:, pl.ds(C_KV - C_Q, KV_RANK)]
        dkpe_ref = tail_ref.at[:, pl.ds(K_PE - C_Q, 2 * DH)]

        @pl.when(pl.program_id(0) == 0)
        def _():
            dqnw_ref[...] = jnp.zeros_like(dqnw_ref)
            dkvnw_ref[...] = jnp.zeros_like(dkvnw_ref)

        qc, qa, qb = qc_ref[...], qa_ref[...], qb_ref[...]
        dkp = jnp.zeros((tm, DH), F32)
        for h in range(HEADS):
            lo, mid, hi = h * HD, h * HD + DH, (h + 1) * HD
            dq_ref[:, lo:hi] = _rope64(dqq_ref[:, lo:hi], qc, -qa, -qb).astype(BF16)
            dkv_ref[:, lo:mid] = dkk_ref[:, lo:mid]
            dkv_ref[:, mid:hi] = dvv_ref[:, h * DH:(h + 1) * DH]
            dkp = dkp + dkk_ref[:, mid:hi].astype(F32)
        dcqn = _dot(dq_ref[...], wq_ref[...], "nn")
        rq_v = rq_ref[...]
        dcq, prod = _rms_bwd(dcqn, cq_ref[...].astype(F32) * rq_v, rq_v, qnw_ref[...])
        dcq_ref[...] = dcq.astype(BF16)
        dqnw_ref[...] += jnp.sum(prod, axis=0, keepdims=True)
        dckvn = _dot(dkv_ref[...], wukv_ref[...], "nn")
        rkv_v = rkv_ref[...]
        dckv, prod = _rms_bwd(dckvn, ckv_ref[...].astype(F32) * rkv_v, rkv_v, kvnw_ref[...])
        dckv_ref[...] = dckv.astype(BF16)
        dkvnw_ref[...] += jnp.sum(prod, axis=0, keepdims=True)
        dkpe_ref[:, :DH] = _rope64(dkp, kc_ref[...], -ka_ref[...], -kb_ref[...]).astype(BF16)
        dkpe_ref[:, DH:] = jnp.zeros((tm, DH), BF16)

    row = lambda w, cb: pl.BlockSpec((tm, w), lambda i: (i, cb))
    full = lambda a: pl.BlockSpec(a.shape, lambda i: (0, 0))
    wide = jax.ShapeDtypeStruct((s, HEADS * HD), BF16)
    return pl.pallas_call(
        body, name="mla_prep_bwd", grid=(s // tm,),
        in_specs=[ANY, row(HEADS * HD, 0), row(HEADS * HD, 0), row(WIDTH, 0),
                  row(Q_RANK, C_Q // Q_RANK), row(KV_RANK, C_KV // KV_RANK), row(1, 0), row(1, 0),
                  full(qn_w), full(kvn_w), full(wq), full(wukv), row(HD, 0), row(HD, 0), row(HD, 0),
                  row(DH, 0), row(DH, 0), row(DH, 0)],
        out_specs=[row(tail, C_Q // tail), row(HEADS * HD, 0), row(HEADS * HD, 0),
                   pl.BlockSpec((1, Q_RANK), lambda i: (0, 0)), pl.BlockSpec((1, KV_RANK), lambda i: (0, 0))],
        out_shape=[jax.ShapeDtypeStruct((s, PROJ_W), BF16), wide, wide,
                   jax.ShapeDtypeStruct((1, Q_RANK), F32), jax.ShapeDtypeStruct((1, KV_RANK), F32)],
        input_output_aliases={0: 0},
        compiler_params=_cparams(("arbitrary",)),
    )(dproj, dqq, dkk, dvv, proj, proj, rq, rkv, qn_w, kvn_w, wq, wukv, *qtab, *ktab)


def _rms_in_bwd(dxn, h, r, dh2, norm_w, s):
    def body(dxn_ref, h_ref, r_ref, dh2_ref, w_ref, gx_ref, gm_ref, dw_ref):
        i = pl.program_id(0)
        r_v = r_ref[...]
        dx, prod = _rms_bwd(dxn_ref[...], h_ref[...] * r_v, r_v, w_ref[...])
        dh = dh2_ref[...] + dx

        @pl.when(i == 0)
        def _():
            gm_ref[...] = dh
            dw_ref[...] = jnp.sum(prod, axis=0, keepdims=True)

        @pl.when(i > 0)
        def _():
            gx_ref[...] = dh
            dw_ref[...] += jnp.sum(prod, axis=0, keepdims=True)

    blk = pl.BlockSpec((BLK, D), lambda i: (i, 0))
    return pl.pallas_call(
        body, name="rms_in_bwd", grid=(s // BLK,),
        in_specs=[blk, blk, pl.BlockSpec((BLK, 1), lambda i: (i, 0)), blk, pl.BlockSpec((1, D), lambda i: (0, 0))],
        out_specs=[pl.BlockSpec((BLK, D), lambda i: (jnp.maximum(i - 1, 0), 0)), pl.BlockSpec((BLK, D), lambda i: (0, 0)),
                   pl.BlockSpec((1, D), lambda i: (0, 0))],
        out_shape=[jax.ShapeDtypeStruct((s - BLK, D), F32), jax.ShapeDtypeStruct((BLK, D), F32),
                   jax.ShapeDtypeStruct((1, D), F32)],
        compiler_params=_cparams(("arbitrary",)),
    )(dxn, h, r, dh2, norm_w)


def _adam_math(w, g, m, v):
    m = ADAM_B1 * m + (1.0 - ADAM_B1) * g
    v = ADAM_B2 * v + (1.0 - ADAM_B2) * (g * g)
    m_hat = m / (1.0 - ADAM_B1 ** ADAM_STEP)
    v_hat = v / (1.0 - ADAM_B2 ** ADAM_STEP)
    return -ADAM_LR * (m_hat / (jnp.sqrt(v_hat) + ADAM_EPS) + ADAM_WD * w), m, v


def _adamw(w, g, m, v, name):
    rows, cols = w.shape
    tr = rows
    for cand in (128, 64, 32, 16, 8):
        if rows % cand == 0:
            tr = cand
            break

    def body(w_ref, g_ref, m_ref, v_ref, d_ref, nm_ref, nv_ref):
        d_ref[...], nm_ref[...], nv_ref[...] = _adam_math(w_ref[...], g_ref[...], m_ref[...], v_ref[...])

    spec = pl.BlockSpec((tr, cols), lambda i: (i, 0))
    return pl.pallas_call(
        body, name=name, grid=(rows // tr,), in_specs=[spec] * 4, out_specs=[spec] * 3,
        out_shape=[jax.ShapeDtypeStruct((rows, cols), F32)] * 3,
        compiler_params=_cparams(("parallel",)),
    )(w, g, m, v)


def _adamw_small(w, gall, m, v):
    def body(w_ref, g_ref, m_ref, v_ref, gs_ref, d_ref, nm_ref, nv_ref):
        g = g_ref[0]
        for dev in range(1, 8):
            g = g + g_ref[dev]
        gs_ref[...] = g
        d_ref[...], nm_ref[...], nv_ref[...] = _adam_math(w_ref[...], g, m_ref[...], v_ref[...])

    return pl.pallas_call(
        body, name="adamw_small", out_shape=[jax.ShapeDtypeStruct((SM_ROWS, 128), F32)] * 4,
        compiler_params=pltpu.CompilerParams(vmem_limit_bytes=VMEM_LIMIT),
    )(w, gall, m, v)


ADD_ROWS = 464
HALF_BLOCKS = PK_HALF // ADD_ROWS


def _rs_core_add(gpk, land, pos):
    def body(pos_ref, a_ref, b_ref, o_ref):
        o_ref[...] = (a_ref[...].astype(F32) + b_ref[...].astype(F32)).astype(BF16)

    return pl.pallas_call(
        body, name="rs_core_add",
        grid_spec=pltpu.PrefetchScalarGridSpec(
            num_scalar_prefetch=1, grid=(N_CHIPS, HALF_BLOCKS),
            in_specs=[pl.BlockSpec((1, ADD_ROWS, D), lambda s, i, pos_ref: (s, pos_ref[1] * HALF_BLOCKS + i, 0)),
                      pl.BlockSpec((1, ADD_ROWS, D), lambda s, i, pos_ref: (s, i, 0))],
            out_specs=pl.BlockSpec((1, ADD_ROWS, D), lambda s, i, pos_ref: (s, i, 0))),
        out_shape=jax.ShapeDtypeStruct((N_CHIPS, PK_HALF, D), BF16),
        compiler_params=_cparams(("parallel", "parallel")),
    )(pos, gpk, land)


def _rs_chip_add(cp, land, pos):
    def body(pos_ref, a_ref, l_ref, o_ref):
        o_ref[0] = ((a_ref[0].astype(F32) + l_ref[0].astype(F32)) + l_ref[1].astype(F32)) + l_ref[2].astype(F32)

    return pl.pallas_call(
        body, name="rs_chip_add",
        grid_spec=pltpu.PrefetchScalarGridSpec(
            num_scalar_prefetch=1, grid=(HALF_BLOCKS,),
            in_specs=[pl.BlockSpec((1, ADD_ROWS, D), lambda i, pos_ref: (pos_ref[0], i, 0)),
                      pl.BlockSpec((3, ADD_ROWS, D), lambda i, pos_ref: (0, i, 0))],
            out_specs=pl.BlockSpec((1, ADD_ROWS, D), lambda i, pos_ref: (pos_ref[1], i, 0))),
        out_shape=jax.ShapeDtypeStruct((2, PK_HALF, D), F32),
        compiler_params=_cparams(("parallel",)),
    )(pos, cp, land)


ANY = pl.BlockSpec(memory_space=pl.ANY)


def _mesh_pos():
    return lax.axis_index("x"), lax.axis_index("y"), lax.axis_index("c")


def _chip_peer(x, y, c, r):
    return (jnp.bitwise_xor(x, r >> 1), jnp.bitwise_xor(y, r & 1), c)


def _allgather_w_in(big0, meta_loc):
    def body(big0_ref, meta_ref, big_ref, metaf_ref, lsem, ssem, rsem, fssem, frsem, msem_s, msem_r):
        del big0_ref
        x, y, c = _mesh_pos()
        j = 2 * x + y
        me, sibling = (x, y, c), (x, y, 1 - c)

        def half_wait(s_sem, r_sem):
            rows = big_ref.at[pl.ds(0, AG_IN_HALF), :]
            return pltpu.make_async_remote_copy(src_ref=rows, dst_ref=rows, send_sem=s_sem, recv_sem=r_sem,
                                                device_id=me, device_id_type=MESH)

        own_meta = pltpu.make_async_copy(meta_ref, metaf_ref.at[j], lsem)
        own_meta.start()
        for s in range(N_CHIPS):
            for hf in range(2):
                @pl.when((j == s) & (c == hf))
                def _():
                    for r in (1, 2, 3):
                        for b, n in _ag_half(s, True, hf):
                            rows = big_ref.at[pl.ds(b, n), :]
                            pltpu.make_async_remote_copy(
                                src_ref=rows, dst_ref=rows, send_sem=ssem.at[r - 1], recv_sem=rsem.at[r - 1],
                                device_id=_chip_peer(x, y, c, r), device_id_type=MESH).start()

        meta_copies = [pltpu.make_async_remote_copy(
            src_ref=meta_ref, dst_ref=metaf_ref.at[j], send_sem=msem_s.at[r - 1], recv_sem=msem_r.at[r - 1],
            device_id=_chip_peer(x, y, c, r), device_id_type=MESH) for r in (1, 2, 3)]
        for cp in meta_copies:
            cp.start()
        for r in (1, 2, 3):
            half_wait(ssem.at[r - 1], rsem.at[r - 1]).wait_recv()
            src_shard = jnp.bitwise_xor(j, r)
            for s in range(N_CHIPS):
                for hf in range(2):
                    @pl.when((src_shard == s) & (c == hf))
                    def _():
                        for b, n in _ag_half(s, True, hf):
                            rows = big_ref.at[pl.ds(b, n), :]
                            pltpu.make_async_remote_copy(
                                src_ref=rows, dst_ref=rows, send_sem=fssem.at[r - 1], recv_sem=frsem.at[r - 1],
                                device_id=sibling, device_id_type=MESH).start()
        for r in (1, 2, 3):
            half_wait(fssem.at[r - 1], frsem.at[r - 1]).wait_recv()
        for r in (1, 2, 3):
            half_wait(ssem.at[r - 1], rsem.at[r - 1]).wait_send()
            half_wait(fssem.at[r - 1], frsem.at[r - 1]).wait_send()
        for cp in meta_copies:
            cp.wait_send()
            cp.wait_recv()
        own_meta.wait()

    dma3 = pltpu.SemaphoreType.DMA((3,))
    return pl.pallas_call(
        body, name="allgather_w_in", in_specs=[ANY, ANY], out_specs=[ANY, ANY],
        out_shape=[jax.ShapeDtypeStruct((BIG_ROWS, D), BF16), jax.ShapeDtypeStruct((N_CHIPS,) + meta_loc.shape, F32)],
        scratch_shapes=[pltpu.SemaphoreType.DMA(()), dma3, dma3, dma3, dma3, dma3, dma3],
        input_output_aliases={0: 0},
        compiler_params=pltpu.CompilerParams(has_side_effects=True),
    )(big0, meta_loc)


HBM = pl.BlockSpec(memory_space=pltpu.HBM)
SEM = pl.BlockSpec(memory_space=pltpu.SEMAPHORE)
EFFECT = pltpu.SideEffectType.DATAFLOW_SIDE_EFFECTING
REST_PEERS = [(r, dc) for r in (1, 2, 3) for dc in (0, 1)]


def _allgather_rest_start(big):
    def body(big_ref, *rest):
        s_sems, r_sems = rest[:6], rest[6:12]
        x, y, c = _mesh_pos()
        j = 2 * x + y
        for s in range(N_CHIPS):
            for hf in range(2):
                @pl.when((j == s) & (c == hf))
                def _():
                    for k, (r, dc) in enumerate(REST_PEERS):
                        for b, n in _ag_half(s, False, hf):
                            rows = big_ref.at[pl.ds(b, n), :]
                            pltpu.make_async_remote_copy(
                                src_ref=rows, dst_ref=rows, send_sem=s_sems[k], recv_sem=r_sems[2 * (r - 1) + hf],
                                device_id=_chip_peer(x, y, dc, r), device_id_type=MESH).start()

    sem = pltpu.SemaphoreType.DMA(())
    out = pl.pallas_call(
        body, name="allgather_rest_start", out_shape=(sem,) * 12 + (pltpu.HBM(big.shape, big.dtype),),
        in_specs=(HBM,), out_specs=(SEM,) * 12 + (HBM,), input_output_aliases={0: 12},
        compiler_params=pltpu.CompilerParams(has_side_effects=EFFECT),
    )(pltpu.with_memory_space_constraint(big, pltpu.HBM))
    return out[:12], out[12]


def _allgather_rest_wait(sems, big, after):
    def body(big_ref, *rest):
        s_sems, r_sems = rest[:6], rest[6:12]
        me = _mesh_pos()
        rows = big_ref.at[pl.ds(0, AG_REST_HALF), :]
        for k in range(6):
            copy = pltpu.make_async_remote_copy(src_ref=rows, dst_ref=rows, send_sem=s_sems[k], recv_sem=r_sems[k],
                                                device_id=me, device_id_type=MESH)
            copy.wait_send()
            copy.wait_recv()

    return pl.pallas_call(
        body, name="allgather_rest_wait", out_shape=pltpu.HBM(big.shape, big.dtype),
        in_specs=(HBM,) + (SEM,) * 12 + (pl.BlockSpec(memory_space=pl.ANY),), out_specs=HBM,
        input_output_aliases={0: 0}, compiler_params=pltpu.CompilerParams(has_side_effects=EFFECT),
    )(big, *sems, after)


RS_SPLIT = 2


def _rs_core_exchange(gpk):
    n = PK_HALF // RS_SPLIT

    def body(g_ref, land_ref, ssem, rsem):
        x, y, c = _mesh_pos()
        me, sibling = (x, y, c), (x, y, 1 - c)
        for s in range(N_CHIPS):
            for q in range(RS_SPLIT):
                src = pl.ds(pl.multiple_of((1 - c) * PK_HALF + q * n, 16), n)
                pltpu.make_async_remote_copy(
                    src_ref=g_ref.at[s, src, :], dst_ref=land_ref.at[s, pl.ds(q * n, n), :],
                    send_sem=ssem, recv_sem=rsem, device_id=sibling, device_id_type=MESH).start()
        whole = pltpu.make_async_remote_copy(src_ref=land_ref, dst_ref=land_ref, send_sem=ssem, recv_sem=rsem,
                                             device_id=me, device_id_type=MESH)
        whole.wait_recv()
        whole.wait_send()

    return pl.pallas_call(
        body, name="rs_core_exchange", in_specs=[ANY], out_specs=ANY,
        out_shape=jax.ShapeDtypeStruct((N_CHIPS, PK_HALF, D), BF16),
        scratch_shapes=[pltpu.SemaphoreType.DMA(())] * 2,
        compiler_params=pltpu.CompilerParams(has_side_effects=True),
    )(gpk)


def _rs_chip_start(cp):
    def body(cp_ref, land_ref, s1, s2, s3, r1, r2, r3, cp_thru, land_thru, token):
        x, y, c = _mesh_pos()
        j = 2 * x + y
        for r, s_sem, r_sem in zip((1, 2, 3), (s1, s2, s3), (r1, r2, r3)):
            pltpu.make_async_remote_copy(
                src_ref=cp_ref.at[jnp.bitwise_xor(j, r)], dst_ref=land_ref.at[r - 1], send_sem=s_sem, recv_sem=r_sem,
                device_id=_chip_peer(x, y, c, r), device_id_type=MESH).start()
        token[...] = jnp.zeros_like(token)

    land_shape = (3, PK_HALF, D)
    sem = pltpu.SemaphoreType.DMA(())
    out = pl.pallas_call(
        body, name="rs_chip_start",
        out_shape=(sem,) * 6 + (pltpu.HBM(cp.shape, cp.dtype), pltpu.HBM(land_shape, BF16),
                                jax.ShapeDtypeStruct((8, 128), F32)),
        in_specs=(HBM, HBM), out_specs=(SEM,) * 6 + (HBM, HBM, pl.BlockSpec(memory_space=pltpu.VMEM)),
        input_output_aliases={0: 6, 1: 7},
        compiler_params=pltpu.CompilerParams(has_side_effects=EFFECT),
    )(pltpu.with_memory_space_constraint(cp, pltpu.HBM),
      pltpu.with_memory_space_constraint(lax.empty(land_shape, BF16), pltpu.HBM))
    return out[:6], out[6], out[7], out[8]


def _rs_chip_wait(sems, cp_thru, land_thru, after):
    def body(cp_ref, land_ref, s1, s2, s3, r1, r2, r3, after_ref, cp_out, land_out):
        me = _mesh_pos()
        for r, s_sem, r_sem in zip((1, 2, 3), (s1, s2, s3), (r1, r2, r3)):
            copy = pltpu.make_async_remote_copy(src_ref=cp_ref.at[0], dst_ref=land_ref.at[r - 1], send_sem=s_sem,
                                                recv_sem=r_sem, device_id=me, device_id_type=MESH)
            copy.wait_send()
            copy.wait_recv()

    return pl.pallas_call(
        body, name="rs_chip_wait",
        out_shape=(pltpu.HBM(cp_thru.shape, cp_thru.dtype), pltpu.HBM(land_thru.shape, land_thru.dtype)),
        in_specs=(HBM, HBM) + (SEM,) * 6 + (pl.BlockSpec(memory_space=pl.ANY),), out_specs=(HBM, HBM),
        input_output_aliases={0: 0, 1: 1},
        compiler_params=pltpu.CompilerParams(has_side_effects=EFFECT),
    )(cp_thru, land_thru, *sems, after)


def _rs_finish(full, small):
    n = PK_HALF // RS_SPLIT

    def body(full_in_ref, sm_ref, full_ref, all_ref, lsem, ssem, rsem, sm_s, sm_r):
        del full_in_ref
        x, y, c = _mesh_pos()
        me, sibling = (x, y, c), (x, y, 1 - c)
        my_id = 4 * x + 2 * y + c
        own_sm = pltpu.make_async_copy(sm_ref, all_ref.at[my_id], lsem)
        own_sm.start()
        for q in range(RS_SPLIT):
            rows = full_ref.at[c, pl.ds(q * n, n), :]
            pltpu.make_async_remote_copy(src_ref=rows, dst_ref=rows, send_sem=ssem, recv_sem=rsem,
                                         device_id=sibling, device_id_type=MESH).start()
        half = pltpu.make_async_remote_copy(src_ref=full_ref.at[c], dst_ref=full_ref.at[c], send_sem=ssem,
                                            recv_sem=rsem, device_id=me, device_id_type=MESH)
        smalls = [pltpu.make_async_remote_copy(
            src_ref=sm_ref, dst_ref=all_ref.at[my_id], send_sem=sm_s.at[r - 1], recv_sem=sm_r.at[r - 1],
            device_id=(jnp.bitwise_xor(x, r >> 2), jnp.bitwise_xor(y, (r >> 1) & 1), jnp.bitwise_xor(c, r & 1)),
            device_id_type=MESH) for r in range(1, 8)]
        for cpy in smalls:
            cpy.start()
        half.wait_recv()
        half.wait_send()
        for cpy in smalls:
            cpy.wait_recv()
        for cpy in smalls:
            cpy.wait_send()
        own_sm.wait()

    dma7 = pltpu.SemaphoreType.DMA((7,))
    return pl.pallas_call(
        body, name="rs_finish", in_specs=[ANY, ANY], out_specs=[ANY, ANY],
        out_shape=[jax.ShapeDtypeStruct((2, PK_HALF, D), F32), jax.ShapeDtypeStruct((8, SM_ROWS, 128), F32)],
        scratch_shapes=[pltpu.SemaphoreType.DMA(()), pltpu.SemaphoreType.DMA(()), pltpu.SemaphoreType.DMA(()), dma7, dma7],
        input_output_aliases={0: 0},
        compiler_params=pltpu.CompilerParams(has_side_effects=True),
    )(full, small)


def _rope_tables(s):
    pos = jnp.arange(s, dtype=F32) - PAD_FRONT

    def cs(d):
        inv = ROPE_BASE ** (-jnp.arange(0, d, 2, dtype=F32) / d)
        ang = pos[:, None] * inv[None, :]
        return jnp.cos(ang), jnp.sin(ang)

    c, sn = cs(ROPE_D)
    z = jnp.zeros_like(c)
    ktab = (jnp.concatenate([c, c, z, z], axis=1), jnp.concatenate([-sn, z, z, z], axis=1),
            jnp.concatenate([z, sn, z, z], axis=1))
    one = jnp.ones_like(c)
    qtab = tuple(ATT_SCALE * jnp.concatenate(parts, axis=1) for parts in (
        [one, one, one, one, c, c, z, z], [z, z, z, z, -sn, z, z, z], [z, z, z, z, z, sn, z, z]))
    c, sn = cs(DH)
    return qtab, ktab, jnp.concatenate([c, c], axis=1), jnp.concatenate([-sn, sn], axis=1)


def _grad_mm(a, b, m, n, s, name):
    return _matmul(a, b, "tn", m, n, s, min(m, 512), min(n, 1024), s, BF16, name, n_outer=True)


def _local_step(x2, target2, hp0, big, small_w, gather_wait=None, reduce_start=None, reduce_wait=None):
    norm_w, qn_w, kvn_w, gn_w, gn_b, fin_w = small_w
    s = x2.shape[0] + BLK
    tm = _row_tile(s, 1408)
    qtab, ktab, c128, s128 = _rope_tables(s)
    consts, gamma = _ret_consts()

    h, xn, r1 = _rms_in(x2, hp0, norm_w, s)
    proj = _matmul(xn, big, "nt", s, PROJ_W, D, tm, 512, D, BF16, "proj", b_off=(BIG_IN // 512, 0))
    if gather_wait is not None:
        big = gather_wait(big, proj)
    uq = big[BIG_UQ:BIG_UQ + 384].reshape(HEADS, DH + ROPE_D, Q_RANK)
    wq = jnp.pad(uq, ((0, 0), (0, HD - DH - ROPE_D), (0, 0))).reshape(HEADS * HD, Q_RANK)
    wukv = big[BIG_UKV:BIG_UKV + 256].reshape(2 * WIDTH, KV_RANK)
    wbm = big[BIG_BM:BIG_BM + 1024].reshape(D, WIDTH)
    wbr = big[BIG_BR:BIG_BR + 1024].reshape(D, WIDTH)
    cqn, rq, ckvn, rkv, qq, kk, vv = _mla_prep(proj, qn_w, kvn_w, wq, wukv, qtab, ktab, s)
    o_mla, y_mla, lse = _attn_fwd(qq, kk, vv, proj, s)
    y_ret, on, rstd, qr, kr, rall = _ret_fwd(proj, gn_w, gn_b, c128, s128, consts, gamma, s)
    u_mla, u_ret, merged = _merge_fwd(y_mla, y_ret, wbm, wbr, proj, s)
    h2 = _out_fwd(merged, big, h, s)
    dh2, dh2b, loss_blk, dfin = _loss_bwd(h2, target2, fin_w, s)

    dproj, du_mla, du_ret = _merge_bwd(dh2b, big, u_mla, u_ret, proj, s)
    g_out = _grad_mm(merged, dh2b, D, D, s, "grad_w_out")
    dproj, do_mla, delta = _branch_mla_bwd(dproj, du_mla, wbm, o_mla, proj, s)
    g_bm = _grad_mm(du_mla, y_mla, D, WIDTH, s, "grad_w_branch_mla")
    dproj, do_ret, dgw, dgb = _branch_ret_bwd(dproj, du_ret, wbr, on, rstd, gn_w, gn_b, proj, s)
    g_br = _grad_mm(du_ret, y_ret, D, WIDTH, s, "grad_w_branch_ret")
    dproj = _ret_bwd(dproj, qr, kr, proj, rall, do_ret, c128, s128, consts, gamma, s)
    dqq, dkk, dvv = _attn_bwd(qq, kk, vv, do_mla, lse, delta.reshape(lse.shape), s)
    dproj, dq, dkv, dqnw, dkvnw = _mla_prep_bwd(dproj, dqq, dkk, dvv, proj, rq, rkv, qn_w, kvn_w, wq, wukv,
                                                qtab, ktab, s)
    g_q = _grad_mm(dq, cqn, HEADS * HD, Q_RANK, s, "grad_w_uq")
    g_ukv = _grad_mm(dkv, ckvn, 2 * WIDTH, KV_RANK, s, "grad_w_ukv")
    g_in = _grad_mm(dproj, xn, PROJ_W, D, s, "grad_w_in")

    g_uq = g_q.reshape(HEADS, HD, Q_RANK)[:, :DH + ROPE_D]
    parts = [(g_out, 512), (g_bm.reshape(1024, D), 256), (g_br.reshape(1024, D), 256), (g_uq.reshape(384, D), 96),
             (g_ukv.reshape(256, D), 64)]

    def shard_rows(sh):
        out = [p[n * sh:n * (sh + 1)] for p, n in parts]
        out += [g_in[b - BIG_IN:b - BIG_IN + n] for p, b, n in _pieces(sh) if PK_IN <= p < PK_PAD]
        return jnp.concatenate(out + [jnp.zeros((PK_ROWS - PK_PAD, D), BF16)], axis=0)

    gbig = jnp.stack([shard_rows(sh) for sh in range(N_CHIPS)])
    token, travelling = (None, gbig) if reduce_start is None else reduce_start(gbig)
    dxn = _matmul(dproj, big, "nn", s, D, PROJ_W, tm, 1024, 1024, F32, "dxn", b_off=(BIG_IN // 1024, 0), after=token)
    grad_x, gmeta_blk, dnorm = _rms_in_bwd(dxn, h, r1, dh2, norm_w, s)
    small = _small_rows((dnorm, dqnw, dkvnw, dgw, dgb, dfin), loss_blk, gmeta_blk[PAD_FRONT:])
    return grad_x, travelling if reduce_wait is None else reduce_wait(travelling, dnorm), small


def _small_rows(ws, first=None, last=None):
    def part(a, rows):
        a = a.reshape(-1, 128)
        return a if a.shape[0] == rows else jnp.pad(a, ((0, rows - a.shape[0]), (0, 0)))

    bounds = (SM_NORM, SM_QN, SM_KVN, SM_GNW, SM_GNB, SM_FIN, SM_META)
    first = jnp.zeros((SM_NORM, 128), F32) if first is None else first
    last = jnp.zeros((SM_ROWS - SM_META, 128), F32) if last is None else last
    return jnp.concatenate([part(first, SM_NORM)] + [part(w, bounds[k + 1] - bounds[k]) for k, w in enumerate(ws)]
                           + [part(last, SM_ROWS - SM_META)], axis=0)


def kernel(x, meta, norm_w, w_in, mla_q_norm_w, mla_w_uq, mla_kv_norm_w, mla_w_ukv, ret_gn_w, ret_gn_b, w_branch_mla, w_branch_ret, w_out, final_norm_w, loss_target, m_meta, m_norm_w, m_w_in, m_mla_q_norm_w, m_mla_w_uq, m_mla_kv_norm_w, m_mla_w_ukv, m_ret_gn_w, m_ret_gn_b, m_w_branch_mla, m_w_branch_ret, m_w_out, m_final_norm_w, v_meta, v_norm_w, v_w_in, v_mla_q_norm_w, v_mla_w_uq, v_mla_kv_norm_w, v_mla_w_ukv, v_ret_gn_w, v_ret_gn_b, v_w_branch_mla, v_w_branch_ret, v_w_out, v_final_norm_w):
    j = 2 * lax.axis_index("x") + lax.axis_index("y")
    tr = lambda w: w[0].T.reshape(-1, D).astype(BF16)
    pos = jnp.stack([j, lax.axis_index("c")]).astype(jnp.int32)
    put = lambda buf, rows, off: lax.dynamic_update_slice(buf, rows, (off, 0))
    big0 = lax.empty((BIG_ROWS, D), BF16)
    big0 = put(big0, w_out[0].astype(BF16), BIG_OUT + 512 * j)
    big0 = put(big0, tr(w_branch_mla), BIG_BM + 256 * j)
    big0 = put(big0, tr(w_branch_ret), BIG_BR + 256 * j)
    big0 = put(big0, tr(mla_w_uq), BIG_UQ + 96 * j)
    big0 = put(big0, tr(mla_w_ukv), BIG_UKV + 64 * j)
    big0 = put(big0, jnp.zeros((AG_JUNK_END - AG_JUNK_REST, D), BF16), AG_JUNK_REST)
    cuts, dest = _in_cuts()
    w_in_t = tr(w_in)
    dest = jnp.asarray(dest, jnp.int32)[j]
    for k in range(len(cuts) - 1):
        big0 = put(big0, w_in_t[cuts[k]:cuts[k + 1]], dest[k])
    big0 = put(big0, jnp.zeros((ZERO_ROWS, D), BF16), BIG_IN + IN_WIDTH)
    big, meta_all = _allgather_w_in(big0, meta)
    gather_sems, big = _allgather_rest_start(big)

    def gather_wait(big_travelling, after):
        return _allgather_rest_wait(gather_sems, big_travelling, after)

    meta_full = meta_all.transpose(1, 0, 2).reshape(N_META, D)
    hp0 = jnp.concatenate([jnp.zeros((PAD_FRONT, D), F32), meta_full], axis=0)
    small_w = (norm_w, mla_q_norm_w, mla_kv_norm_w, ret_gn_w, ret_gn_b, final_norm_w.reshape(1, D))

    def reduce_start(gbig):
        chip_part = _rs_core_add(gbig, _rs_core_exchange(gbig), pos)
        sems, part_thru, land_thru, token = _rs_chip_start(chip_part)
        return token, (sems, part_thru, land_thru)

    def reduce_wait(state, after):
        return _rs_chip_wait(*state, after)

    grad_x, (chip_part, land2), small = _local_step(x[0], loss_target[0], hp0, big, small_w, gather_wait,
                                                    reduce_start, reduce_wait)
    full, small_all = _rs_finish(_rs_chip_add(chip_part, land2, pos), small)
    full = full.reshape(PK_ROWS, D)

    untr = lambda lo, hi, rows: full[lo:hi].reshape(rows, -1).T
    grads = {
        "w_out": full[PK_OUT:PK_BM], "w_branch_mla": untr(PK_BM, PK_BR, 512), "w_branch_ret": untr(PK_BR, PK_UQ, 512),
        "mla_w_uq": untr(PK_UQ, PK_UKV, 384), "mla_w_ukv": untr(PK_UKV, PK_IN, 512), "w_in": untr(PK_IN, PK_PAD, IN_SHARD),
    }
    big_w = {"w_in": (w_in, m_w_in, v_w_in), "mla_w_uq": (mla_w_uq, m_mla_w_uq, v_mla_w_uq),
             "mla_w_ukv": (mla_w_ukv, m_mla_w_ukv, v_mla_w_ukv),
             "w_branch_mla": (w_branch_mla, m_w_branch_mla, v_w_branch_mla),
             "w_branch_ret": (w_branch_ret, m_w_branch_ret, v_w_branch_ret), "w_out": (w_out, m_w_out, v_w_out)}
    res = {}
    for name, (w, m, v) in big_w.items():
        d, nm, nv = _adamw(w[0], grads[name], m[0], v[0], "adamw_" + name)
        res[name] = (grads[name][None], d[None], nm[None], nv[None])

    small_m = (m_norm_w, m_mla_q_norm_w, m_mla_kv_norm_w, m_ret_gn_w, m_ret_gn_b, m_final_norm_w.reshape(1, D))
    small_v = (v_norm_w, v_mla_q_norm_w, v_mla_kv_norm_w, v_ret_gn_w, v_ret_gn_b, v_final_norm_w.reshape(1, D))
    gs, ds, ms, vs = _adamw_small(_small_rows(small_w), small_all, _small_rows(small_m), _small_rows(small_v))
    names = ["norm_w", "mla_q_norm_w", "mla_kv_norm_w", "ret_gn_w", "ret_gn_b", "final_norm_w"]
    bounds = [SM_NORM, SM_QN, SM_KVN, SM_GNW, SM_GNB, SM_FIN]
    for k, name in enumerate(names):
        shape = (D,) if name == "final_norm_w" else (1, -1)
        rows = small_w[k].size // 128
        res[name] = tuple(a[bounds[k]:bounds[k] + rows].reshape(shape) for a in (gs, ds, ms, vs))
    g_meta = lax.dynamic_slice_in_dim(gs[SM_META:SM_META + 256].reshape(N_META, D), j * (D // N_CHIPS), D // N_CHIPS, axis=1)
    res["meta"] = (g_meta,) + tuple(_adamw(meta, g_meta, m_meta, v_meta, "adamw_meta"))

    order = ["meta", "norm_w", "w_in", "mla_q_norm_w", "mla_w_uq", "mla_kv_norm_w", "mla_w_ukv", "ret_gn_w", "ret_gn_b",
             "w_branch_mla", "w_branch_ret", "w_out", "final_norm_w"]
    return (gs[0, 0], grad_x[None]) + tuple(res[n][k] for k in range(4) for n in order)
```

```python
import functools
import math

import numpy as np
import jax
import jax.numpy as jnp
from jax import lax
from jax.experimental import pallas as pl
from jax.experimental.pallas import tpu as pltpu

F32 = jnp.float32
BF16 = jnp.bfloat16
MESH = pl.DeviceIdType.MESH

D = 2048
N_META = 16
BLK = 128
PAD_FRONT = BLK - N_META
HEADS = 8
DH = 128
ROPE_D = 64
Q_RANK = 512
KV_RANK = 256
WIDTH = HEADS * DH
ROPE_BASE = 10000.0
NORM_EPS = 1e-6
GN_EPS = 1e-5
NEG_INF = -1e30
ATT_SCALE = (DH + ROPE_D) ** -0.5
RET_SCALE = DH ** -0.5
IN_WIDTH = 10048
N_CHIPS = 4
IN_SHARD = IN_WIDTH // N_CHIPS
ADAM_LR, ADAM_B1, ADAM_B2, ADAM_EPS, ADAM_WD, ADAM_STEP = 0.001, 0.9, 0.999, 1e-08, 0.01, 10

R_Q, R_K, R_V, Z_RET, Z_MLA, GATE0, GATE1 = 0, 1024, 2048, 3072, 4096, 5120, 5120 + D
C_Q = 5120 + 2 * D
C_KV = C_Q + Q_RANK
K_PE = C_KV + KV_RANK
PROJ_W = 10240
IN_RUNS = ((0, 832, C_Q), (832, 1856, Z_MLA), (1856, 4928, R_Q), (4928, 5952, Z_RET), (5952, IN_WIDTH, GATE0))

BIG_OUT, BIG_BM, BIG_BR, BIG_UQ, BIG_UKV, BIG_JUNK, BIG_IN = 0, 2048, 3072, 4096, 4480, 4736, 5120
BIG_ROWS = BIG_IN + PROJ_W
ZERO_ROWS = PROJ_W - IN_WIDTH
PK_IN, PK_OUT, PK_BM, PK_BR, PK_UQ, PK_UKV, PK_PAD, PK_ROWS = 0, 2512, 3024, 3280, 3536, 3632, 3696, 3712
PK_HALF = PK_ROWS // 2
SM_LOSS, SM_NORM, SM_QN, SM_KVN, SM_GNW, SM_GNB, SM_FIN, SM_META, SM_ROWS = 0, 8, 24, 32, 40, 48, 56, 72, 328

VMEM_LIMIT = 56 * 1024 * 1024


def _pieces(shard):
    j = shard
    out = [(PK_OUT, BIG_OUT + 512 * j, 512), (PK_BM, BIG_BM + 256 * j, 256), (PK_BR, BIG_BR + 256 * j, 256),
           (PK_UQ, BIG_UQ + 96 * j, 96), (PK_UKV, BIG_UKV + 64 * j, 64)]
    for lo, hi, new in IN_RUNS:
        a, b = max(lo, IN_SHARD * j), min(hi, IN_SHARD * (j + 1))
        if b > a:
            out.append((PK_IN + a - IN_SHARD * j, BIG_IN + new + a - lo, b - a))
    out += [(PK_PAD, BIG_JUNK + 16 * j, 16)]
    return out


def _in_cuts():
    cuts = sorted({0, IN_SHARD} | {b - IN_SHARD * j for j in range(N_CHIPS) for lo, hi, _ in IN_RUNS for b in (lo, hi)
                                   if 0 < b - IN_SHARD * j < IN_SHARD})

    def new_row(col):
        return next(BIG_IN + new + col - lo for lo, hi, new in IN_RUNS if lo <= col < hi)

    return cuts, [[new_row(IN_SHARD * j + c) for c in cuts[:-1]] for j in range(N_CHIPS)]


AG_JUNK_REST, AG_JUNK_IN, AG_JUNK_END = 4736, 4864, 4928
AG_REST_HALF, AG_IN_HALF = 608, 1264


def _ag_half(shard, w_in, half):
    pieces = [(b, n) for p, b, n in _pieces(shard) if p < PK_PAD and (p < PK_OUT) == w_in]
    pieces.append((AG_JUNK_IN + 16 * shard, 16) if w_in else (AG_JUNK_REST + 32 * shard, 32))
    size = AG_IN_HALF if w_in else AG_REST_HALF
    out, pos = [], 0
    for b, n in pieces:
        s, e = max(pos, half * size), min(pos + n, (half + 1) * size)
        if e > s:
            out.append((b + s - pos, e - s))
        pos += n
    assert pos == 2 * size
    return out


def _half_pieces(shard, half):
    lo, hi = half * PK_HALF, (half + 1) * PK_HALF
    out = []
    for p, b, n in _pieces(shard):
        s, e = max(p, lo), min(p + n, hi)
        if e > s:
            out.append((s, b + (s - p), e - s))
    return out


def _row_tile(rows, cap):
    best = BLK
    for t in range(BLK, cap + 1, BLK):
        if rows % t == 0:
            best = t
    return best


def _cparams(sem):
    return pltpu.CompilerParams(dimension_semantics=sem, vmem_limit_bytes=VMEM_LIMIT)


def _dot(a, b, form):
    dn = {"nt": (((1,), (1,)), ((), ())), "nn": (((1,), (0,)), ((), ())), "tn": (((0,), (0,)), ((), ()))}[form]
    return lax.dot_general(a, b, dn, preferred_element_type=F32)


def _sigmoid(v):
    return 1.0 / (1.0 + jnp.exp(-v))


def _matmul(a, b, form, m, n, k, tm, tn, tk, out_dtype, name, a_off=(0, 0), b_off=(0, 0), n_outer=False, after=None):
    nk = k // tk
    gi, gj = m // tm, n // tn

    def ij(g0, g1):
        return (g1, g0) if n_outer else (g0, g1)

    if form == "nt":
        a_spec = pl.BlockSpec((tm, tk), lambda g0, g1, kk: (ij(g0, g1)[0] + a_off[0], kk + a_off[1]))
        b_spec = pl.BlockSpec((tn, tk), lambda g0, g1, kk: (ij(g0, g1)[1] + b_off[0], kk + b_off[1]))
    elif form == "nn":
        a_spec = pl.BlockSpec((tm, tk), lambda g0, g1, kk: (ij(g0, g1)[0] + a_off[0], kk + a_off[1]))
        b_spec = pl.BlockSpec((tk, tn), lambda g0, g1, kk: (kk + b_off[0], ij(g0, g1)[1] + b_off[1]))
    else:
        a_spec = pl.BlockSpec((tk, tm), lambda g0, g1, kk: (kk + a_off[0], ij(g0, g1)[0] + a_off[1]))
        b_spec = pl.BlockSpec((tk, tn), lambda g0, g1, kk: (kk + b_off[0], ij(g0, g1)[1] + b_off[1]))
    o_spec = pl.BlockSpec((tm, tn), lambda g0, g1, kk: ij(g0, g1))

    def body(a_ref, b_ref, *rest):
        o_ref, *acc = rest[0 if after is None else 1:]
        p = _dot(a_ref[...], b_ref[...], form)
        if nk == 1:
            o_ref[...] = p.astype(o_ref.dtype)
        else:
            acc_ref, = acc
            kk = pl.program_id(2)

            @pl.when(kk == 0)
            def _():
                acc_ref[...] = p

            @pl.when(kk > 0)
            def _():
                acc_ref[...] += p

            @pl.when(kk == nk - 1)
            def _():
                o_ref[...] = acc_ref[...].astype(o_ref.dtype)

    extra = [] if after is None else [after]
    return pl.pallas_call(
        body, name=name, grid=(gj, gi, nk) if n_outer else (gi, gj, nk),
        in_specs=[a_spec, b_spec] + [pl.BlockSpec(memory_space=pl.ANY)] * len(extra), out_specs=o_spec,
        out_shape=jax.ShapeDtypeStruct((m, n), out_dtype),
        scratch_shapes=[] if nk == 1 else [pltpu.VMEM((tm, tn), F32)],
        compiler_params=_cparams(("parallel", "parallel", "arbitrary")),
    )(a, b, *extra)


def _rms_in(x, hp0, norm_w, s):
    def body(x_ref, hp0_ref, w_ref, h_ref, xn_ref, r_ref):
        def run(hv):
            r = lax.rsqrt(jnp.mean(hv * hv, axis=-1, keepdims=True) + NORM_EPS)
            h_ref[...] = hv
            xn_ref[...] = (hv * r * w_ref[...]).astype(BF16)
            r_ref[...] = r

        @pl.when(pl.program_id(0) == 0)
        def _():
            run(hp0_ref[...])

        @pl.when(pl.program_id(0) > 0)
        def _():
            run(x_ref[...])

    return pl.pallas_call(
        body, name="rms_in", grid=(s // BLK,),
        in_specs=[pl.BlockSpec((BLK, D), lambda i: (jnp.maximum(i - 1, 0), 0)),
                  pl.BlockSpec((BLK, D), lambda i: (0, 0)), pl.BlockSpec((1, D), lambda i: (0, 0))],
        out_specs=[pl.BlockSpec((BLK, D), lambda i: (i, 0)), pl.BlockSpec((BLK, D), lambda i: (i, 0)),
                   pl.BlockSpec((BLK, 1), lambda i: (i, 0))],
        out_shape=[jax.ShapeDtypeStruct((s, D), F32), jax.ShapeDtypeStruct((s, D), BF16),
                   jax.ShapeDtypeStruct((s, 1), F32)],
        compiler_params=_cparams(("arbitrary",)),
    )(x, hp0, norm_w)


def _rope64(t, c, sa, sb):
    return t * c + pltpu.roll(t, t.shape[1] - ROPE_D // 2, 1) * sa + pltpu.roll(t, ROPE_D // 2, 1) * sb


def _rope128(t, c, sg):
    return t * c + pltpu.roll(t, DH // 2, 1) * sg


HD = 2 * DH


def _mla_prep(proj, qn_w, kvn_w, wq, wukv, qtab, ktab, qbias, kbias, s):
    tm = _row_tile(s, 384)

    def body(cq_ref, ckv_ref, kpe_ref, qnw_ref, kvnw_ref, wq_ref, wukv_ref, qc_ref, qa_ref, qb_ref,
             kc_ref, ka_ref, kb_ref, qbias_ref, kbias_ref, cqn_ref, rq_ref, ckvn_ref, rkv_ref, qq_ref, kk_ref, vv_ref):
        cq = cq_ref[...].astype(F32)
        rq = lax.rsqrt(jnp.mean(cq * cq, axis=-1, keepdims=True) + NORM_EPS)
        cqn = (cq * rq * qnw_ref[...]).astype(BF16)
        ckv = ckv_ref[...].astype(F32)
        rkv = lax.rsqrt(jnp.mean(ckv * ckv, axis=-1, keepdims=True) + NORM_EPS)
        ckvn = (ckv * rkv * kvnw_ref[...]).astype(BF16)
        cqn_ref[...] = cqn
        rq_ref[...] = rq
        ckvn_ref[...] = ckvn
        rkv_ref[...] = rkv
        q = _dot(cqn, wq_ref[...], "nt")
        kv = _dot(ckvn, wukv_ref[...], "nt")
        kp = (_rope64(kpe_ref[...].astype(F32), kc_ref[...], ka_ref[...], kb_ref[...]) + kbias_ref[...]).astype(BF16)
        qc, qa, qb, qbias = qc_ref[...], qa_ref[...], qb_ref[...], qbias_ref[...]
        ones = jnp.ones((tm, DH), BF16)
        for h in range(HEADS):
            lo, mid, hi = h * HD, h * HD + DH, (h + 1) * HD
            qq_ref[:, lo:hi] = (_rope64(q[:, lo:hi], qc, qa, qb) + qbias).astype(BF16)
            kk_ref[:, lo:mid] = kv[:, lo:mid].astype(BF16)
            kk_ref[:, mid:hi] = kp
            vv_ref[:, lo:mid] = kv[:, mid:hi].astype(BF16)
            vv_ref[:, mid:hi] = ones

    row = lambda w, cb: pl.BlockSpec((tm, w), lambda i: (i, cb))
    full = lambda a: pl.BlockSpec(a.shape, lambda i: (0, 0))
    wide = jax.ShapeDtypeStruct((s, HEADS * HD), BF16)
    return pl.pallas_call(
        body, name="mla_prep", grid=(s // tm,),
        in_specs=[row(Q_RANK, C_Q // Q_RANK), row(KV_RANK, C_KV // KV_RANK), row(DH, K_PE // DH),
                  full(qn_w), full(kvn_w), full(wq), full(wukv), row(HD, 0), row(HD, 0), row(HD, 0),
                  row(DH, 0), row(DH, 0), row(DH, 0), full(qbias), row(DH, 0)],
        out_specs=[row(Q_RANK, 0), row(1, 0), row(KV_RANK, 0), row(1, 0), row(HEADS * HD, 0), row(HEADS * HD, 0),
                   row(HEADS * HD, 0)],
        out_shape=[jax.ShapeDtypeStruct((s, Q_RANK), BF16), jax.ShapeDtypeStruct((s, 1), F32),
                   jax.ShapeDtypeStruct((s, KV_RANK), BF16), jax.ShapeDtypeStruct((s, 1), F32), wide, wide, wide],
        compiler_params=_cparams(("parallel",)),
    )(proj, proj, proj, qn_w, kvn_w, wq, wukv, *qtab, *ktab, qbias, kbias)


def _diag_mask(t):
    return lax.broadcasted_iota(jnp.int32, (t, t), 0) <= lax.broadcasted_iota(jnp.int32, (t, t), 1)


def _silu(z):
    return z * _sigmoid(z)


def _attn_fwd(qq, kk, vv, proj, s):
    t = _row_tile(s, 384)
    n = s // t

    def body(q_ref, k_ref, v_ref, z_ref, o_ref, y_ref, lse_ref, acc_ref, m_ref):
        qi = pl.program_id(1)
        q = q_ref[...]
        m_ref[...] = jnp.full(m_ref.shape, NEG_INF, F32)
        acc_ref[...] = jnp.zeros(acc_ref.shape, F32)

        def keys(ki):
            return pl.ds(pl.multiple_of(ki * t, t), t)

        def scores(ki):
            return _dot(k_ref[keys(ki), :], q, "nt")

        def tile(ki, st):
            m_old = m_ref[...]
            m_new = jnp.maximum(m_old, jnp.max(st, axis=0, keepdims=True))
            pt = jnp.exp(st - m_new).astype(BF16)
            acc_ref[...] = jnp.exp(m_old - m_new) * acc_ref[...] + _dot(v_ref[keys(ki), :], pt, "tn")
            m_ref[...] = m_new

        def inner(ki, st):
            st_after = scores(ki + 1)
            tile(ki, st)
            return st_after

        st_diag = lax.fori_loop(0, qi, inner, scores(0))
        tile(qi, jnp.where(_diag_mask(t), st_diag, NEG_INF))

        l = acc_ref[DH:DH + 1, :]
        o = (acc_ref[:DH, :] / l).T
        o_ref[...] = o.astype(BF16)
        y_ref[...] = (o * _silu(z_ref[...].astype(F32))).astype(BF16)
        lse_ref[0, 0] = m_ref[...] + jnp.log(l)

    qtile = pl.BlockSpec((t, DH), lambda h, i: (i, h))
    head = pl.BlockSpec((s, HD), lambda h, i: (0, h))
    return pl.pallas_call(
        body, name="attn_fwd", grid=(HEADS, s // t),
        in_specs=[pl.BlockSpec((t, HD), lambda h, i: (i, h)), head, head,
                  pl.BlockSpec((t, DH), lambda h, i: (i, Z_MLA // DH + h))],
        out_specs=[qtile, qtile, pl.BlockSpec((1, 1, 1, t), lambda h, i: (h, i, 0, 0))],
        out_shape=[jax.ShapeDtypeStruct((s, WIDTH), BF16), jax.ShapeDtypeStruct((s, WIDTH), BF16),
                   jax.ShapeDtypeStruct((HEADS, s // t, 1, t), F32)],
        scratch_shapes=[pltpu.VMEM((HD, t), F32), pltpu.VMEM((1, t), F32)],
        compiler_params=_cparams(("parallel", "arbitrary")),
    )(qq, kk, vv, proj)


def _ret_consts():
    log_g = np.log1p(-(2.0 ** (-5.0 - np.arange(HEADS, dtype=np.float64))))
    n = np.arange(BLK, dtype=np.float64)
    diff = n[:, None] - n[None, :]
    decay = np.where(diff >= 0, np.exp(log_g[:, None, None] * np.maximum(diff, 0.0)), 0.0)
    zeta = np.exp(log_g[:, None] * (BLK - 1.0 - n))[:, :, None]
    xi = np.exp(log_g[:, None] * (n + 1.0))[:, :, None]
    gamma = [float(np.float32(np.exp(g * BLK))) for g in log_g]
    return (jnp.asarray(decay, F32), jnp.asarray(zeta, F32), jnp.asarray(xi, F32)), gamma


def _ret_fwd(proj, gn_w, gn_b, c128, s128, consts, gamma, s):
    nb = s // BLK
    decay, zeta, xi = consts

    def body(rq_ref, rk_ref, rv_ref, z_ref, gw_ref, gb_ref, c_ref, s_ref, dm_ref, zt_ref, xi_ref,
             y_ref, on_ref, rstd_ref, qr_ref, kr_ref, rall_ref, state):
        @pl.when(pl.program_id(0) == 0)
        def _():
            state[...] = jnp.zeros_like(state)

        c, sg = c_ref[...], s_ref[...]
        for h in range(HEADS):
            sl = slice(h * DH, (h + 1) * DH)
            q = _rope128(rq_ref[:, sl].astype(F32), c, sg).astype(BF16)
            kf = _rope128(rk_ref[:, sl].astype(F32), c, sg) * RET_SCALE
            k = kf.astype(BF16)
            v = rv_ref[:, sl]
            qr_ref[:, sl] = q
            kr_ref[:, sl] = k
            r_prev = state[h]
            rall_ref[0, h] = r_prev
            a = _dot(q, k, "nt") * dm_ref[h]
            o = _dot(a.astype(BF16), v, "nn") + _dot(q, r_prev.astype(BF16), "nn") * xi_ref[h]
            state[h] = r_prev * gamma[h] + _dot((kf * zt_ref[h]).astype(BF16), v, "tn")
            mu = jnp.mean(o, axis=-1, keepdims=True)
            var = jnp.mean(jnp.square(o - mu), axis=-1, keepdims=True)
            rstd = lax.rsqrt(var + GN_EPS)
            on = (o - mu) * rstd
            rstd_ref[h] = rstd
            on_ref[:, sl] = on.astype(BF16)
            ogn = on * gw_ref[:, sl] + gb_ref[:, sl]
            y_ref[:, sl] = (ogn * _silu(z_ref[:, sl].astype(F32))).astype(BF16)

    seg = lambda cb: pl.BlockSpec((BLK, WIDTH), lambda i: (i, cb))
    full = lambda a: pl.BlockSpec(a.shape, lambda i: (0,) * a.ndim)
    tab = pl.BlockSpec((BLK, DH), lambda i: (i, 0))
    return pl.pallas_call(
        body, name="ret_fwd", grid=(nb,),
        in_specs=[seg(R_Q // WIDTH), seg(R_K // WIDTH), seg(R_V // WIDTH), seg(Z_RET // WIDTH), full(gn_w), full(gn_b),
                  tab, tab, full(decay), full(zeta), full(xi)],
        out_specs=[seg(0), seg(0), pl.BlockSpec((HEADS, BLK, 1), lambda i: (0, i, 0)), seg(0), seg(0),
                   pl.BlockSpec((1, HEADS, DH, DH), lambda i: (i, 0, 0, 0))],
        out_shape=[jax.ShapeDtypeStruct((s, WIDTH), BF16), jax.ShapeDtypeStruct((s, WIDTH), BF16),
                   jax.ShapeDtypeStruct((HEADS, s, 1), F32), jax.ShapeDtypeStruct((s, WIDTH), BF16),
                   jax.ShapeDtypeStruct((s, WIDTH), BF16), jax.ShapeDtypeStruct((nb, HEADS, DH, DH), F32)],
        scratch_shapes=[pltpu.VMEM((HEADS, DH, DH), F32)],
        compiler_params=_cparams(("arbitrary",)),
    )(proj, proj, proj, proj, gn_w, gn_b, c128, s128, decay, zeta, xi)


def _merge_fwd(y_mla, y_ret, wbm, wbr, proj, s):
    tm, tn = _row_tile(s, 1408), 512

    def body(ym_ref, yr_ref, wm_ref, wr_ref, g0_ref, g1_ref, um_ref, ur_ref, mg_ref):
        um = _dot(ym_ref[...], wm_ref[...], "nt")
        ur = _dot(yr_ref[...], wr_ref[...], "nt")
        um_ref[...] = um.astype(BF16)
        ur_ref[...] = ur.astype(BF16)
        mg_ref[...] = (_sigmoid(g0_ref[...].astype(F32)) * um + _sigmoid(g1_ref[...].astype(F32)) * ur).astype(BF16)

    yspec = pl.BlockSpec((tm, WIDTH), lambda i, j: (i, 0))
    wspec = pl.BlockSpec((tn, WIDTH), lambda i, j: (j, 0))
    ospec = pl.BlockSpec((tm, tn), lambda i, j: (i, j))
    return pl.pallas_call(
        body, name="merge_fwd", grid=(s // tm, D // tn),
        in_specs=[yspec, yspec, wspec, wspec, pl.BlockSpec((tm, tn), lambda i, j: (i, GATE0 // tn + j)),
                  pl.BlockSpec((tm, tn), lambda i, j: (i, GATE1 // tn + j))],
        out_specs=[ospec, ospec, ospec],
        out_shape=[jax.ShapeDtypeStruct((s, D), BF16)] * 3,
        compiler_params=_cparams(("parallel", "parallel")),
    )(y_mla, y_ret, wbm, wbr, proj, proj)


def _out_fwd(merged, big, h, s):
    tm, tn = _row_tile(s, 1408), 512

    def body(m_ref, w_ref, h_ref, o_ref):
        o_ref[...] = h_ref[...] + _dot(m_ref[...], w_ref[...], "nn")

    return pl.pallas_call(
        body, name="out_fwd", grid=(s // tm, D // tn),
        in_specs=[pl.BlockSpec((tm, D), lambda i, j: (i, 0)), pl.BlockSpec((D, tn), lambda i, j: (BIG_OUT // D, j)),
                  pl.BlockSpec((tm, tn), lambda i, j: (i, j))],
        out_specs=pl.BlockSpec((tm, tn), lambda i, j: (i, j)),
        out_shape=jax.ShapeDtypeStruct((s, D), F32),
        compiler_params=_cparams(("parallel", "parallel")),
    )(merged, big, h)


def _loss_bwd(h2, target, fin_w, s):
    nb = s // BLK

    def body(h2_ref, t_ref, w_ref, dh_ref, dhb_ref, loss_ref, dw_ref):
        i = pl.program_id(0)

        @pl.when(i == 0)
        def _():
            dh_ref[...] = jnp.zeros_like(dh_ref)
            dhb_ref[...] = jnp.zeros_like(dhb_ref)
            loss_ref[...] = jnp.zeros_like(loss_ref)
            dw_ref[...] = jnp.zeros_like(dw_ref)

        @pl.when(i > 0)
        def _():
            hv = h2_ref[...]
            w = w_ref[...]
            r = lax.rsqrt(jnp.mean(hv * hv, axis=-1, keepdims=True) + NORM_EPS)
            nrm = hv * r
            e = nrm * w - t_ref[...]
            loss_ref[...] += jnp.full(loss_ref.shape, 0.5 / D, F32) * jnp.sum(e * e)
            dy = e * (1.0 / D)
            dw_ref[...] += jnp.sum(dy * nrm, axis=0, keepdims=True)
            g = dy * w
            dh = r * (g - nrm * jnp.mean(g * nrm, axis=-1, keepdims=True))
            dh_ref[...] = dh
            dhb_ref[...] = dh.astype(BF16)

    blk = pl.BlockSpec((BLK, D), lambda i: (i, 0))
    return pl.pallas_call(
        body, name="loss_bwd", grid=(nb,),
        in_specs=[blk, pl.BlockSpec((BLK, D), lambda i: (jnp.maximum(i - 1, 0), 0)), pl.BlockSpec((1, D), lambda i: (0, 0))],
        out_specs=[blk, blk, pl.BlockSpec((8, 128), lambda i: (0, 0)), pl.BlockSpec((1, D), lambda i: (0, 0))],
        out_shape=[jax.ShapeDtypeStruct((s, D), F32), jax.ShapeDtypeStruct((s, D), BF16),
                   jax.ShapeDtypeStruct((8, 128), F32), jax.ShapeDtypeStruct((1, D), F32)],
        compiler_params=_cparams(("arbitrary",)),
    )(h2, target, fin_w)


def _merge_bwd(dh2b, big, u_mla, u_ret, proj, s):
    tm, tn = _row_tile(s, 1408), 512

    def body(d_ref, w_ref, um_ref, ur_ref, gate_ref, dproj_ref, dum_ref, dur_ref, dm_ref):
        branch = pl.program_id(2)

        @pl.when(branch == 0)
        def _():
            dm_ref[...] = _dot(d_ref[...], w_ref[...], "nt")

        dm = dm_ref[...]
        gt = _sigmoid(gate_ref[...].astype(F32))

        @pl.when(branch == 0)
        def _():
            dproj_ref[...] = (dm * um_ref[...].astype(F32) * gt * (1.0 - gt)).astype(BF16)
            dum_ref[...] = (dm * gt).astype(BF16)

        @pl.when(branch == 1)
        def _():
            dproj_ref[...] = (dm * ur_ref[...].astype(F32) * gt * (1.0 - gt)).astype(BF16)
            dur_ref[...] = (dm * gt).astype(BF16)

    ospec = pl.BlockSpec((tm, tn), lambda i, j, b: (i, j))
    gate = pl.BlockSpec((tm, tn), lambda i, j, b: (i, GATE0 // tn + b * (D // tn) + j))
    return pl.pallas_call(
        body, name="merge_bwd", grid=(s // tm, D // tn, 2),
        in_specs=[pl.BlockSpec((tm, D), lambda i, j, b: (i, 0)),
                  pl.BlockSpec((tn, D), lambda i, j, b: (BIG_OUT // tn + j, 0)), ospec, ospec, gate],
        out_specs=[gate, ospec, ospec],
        out_shape=[jax.ShapeDtypeStruct((s, PROJ_W), BF16), jax.ShapeDtypeStruct((s, D), BF16),
                   jax.ShapeDtypeStruct((s, D), BF16)],
        scratch_shapes=[pltpu.VMEM((tm, tn), F32)],
        compiler_params=_cparams(("parallel", "parallel", "arbitrary")),
    )(dh2b, big, u_mla, u_ret, proj)


def _dsilu(z):
    sg = _sigmoid(z)
    return sg * (1.0 + z * (1.0 - sg))


def _branch_mla_bwd(dproj, du, wbm, o_mla, proj, s):
    tm = _row_tile(s, 384)

    def body(dproj_in, du_ref, w_ref, o_ref, z_ref, dz_ref, do_ref, delta_ref):
        del dproj_in
        dy = _dot(du_ref[...], w_ref[...], "nn")
        z = z_ref[...].astype(F32)
        o = o_ref[...].astype(F32)
        do = dy * _silu(z)
        do_ref[...] = do.astype(BF16)
        dz_ref[...] = (dy * o * _dsilu(z)).astype(BF16)
        prod = do * o
        for h in range(HEADS):
            delta_ref[h] = jnp.sum(prod[:, h * DH:(h + 1) * DH], axis=-1, keepdims=True)

    row = lambda w, cb: pl.BlockSpec((tm, w), lambda i: (i, cb))
    return pl.pallas_call(
        body, name="branch_mla_bwd", grid=(s // tm,),
        in_specs=[ANY, row(D, 0), pl.BlockSpec((D, WIDTH), lambda i: (0, 0)), row(WIDTH, 0),
                  row(WIDTH, Z_MLA // WIDTH)],
        out_specs=[row(WIDTH, Z_MLA // WIDTH), row(WIDTH, 0), pl.BlockSpec((HEADS, tm, 1), lambda i: (0, i, 0))],
        out_shape=[jax.ShapeDtypeStruct((s, PROJ_W), BF16), jax.ShapeDtypeStruct((s, WIDTH), BF16),
                   jax.ShapeDtypeStruct((HEADS, s, 1), F32)],
        input_output_aliases={0: 0},
        compiler_params=_cparams(("parallel",)),
    )(dproj, du, wbm, o_mla, proj)


def _branch_ret_bwd(dproj, du, wbr, on, rstd, gn_w, gn_b, proj, s):
    tm = BLK

    def body(dproj_in, du_ref, w_ref, on_ref, rstd_ref, gw_ref, gb_ref, z_ref, dz_ref, do_ref, dgw_ref, dgb_ref):
        del dproj_in

        @pl.when(pl.program_id(0) == 0)
        def _():
            dgw_ref[...] = jnp.zeros_like(dgw_ref)
            dgb_ref[...] = jnp.zeros_like(dgb_ref)

        dy = _dot(du_ref[...], w_ref[...], "nn")
        z = z_ref[...].astype(F32)
        on = on_ref[...].astype(F32)
        gw = gw_ref[...]
        dogn = dy * _silu(z)
        dz_ref[...] = (dy * (on * gw + gb_ref[...]) * _dsilu(z)).astype(BF16)
        dgw_ref[...] += jnp.sum(dogn * on, axis=0, keepdims=True)
        dgb_ref[...] += jnp.sum(dogn, axis=0, keepdims=True)
        don = dogn * gw
        for h in range(HEADS):
            sl = slice(h * DH, (h + 1) * DH)
            dn, nh = don[:, sl], on[:, sl]
            do = rstd_ref[h] * (dn - jnp.mean(dn, axis=-1, keepdims=True)
                                - nh * jnp.mean(dn * nh, axis=-1, keepdims=True))
            do_ref[:, sl] = do.astype(BF16)

    row = lambda w, cb: pl.BlockSpec((tm, w), lambda i: (i, cb))
    vec = pl.BlockSpec((1, WIDTH), lambda i: (0, 0))
    return pl.pallas_call(
        body, name="branch_ret_bwd", grid=(s // tm,),
        in_specs=[ANY, row(D, 0), pl.BlockSpec((D, WIDTH), lambda i: (0, 0)), row(WIDTH, 0),
                  pl.BlockSpec((HEADS, tm, 1), lambda i: (0, i, 0)), vec, vec, row(WIDTH, Z_RET // WIDTH)],
        out_specs=[row(WIDTH, Z_RET // WIDTH), row(WIDTH, 0), vec, vec],
        out_shape=[jax.ShapeDtypeStruct((s, PROJ_W), BF16), jax.ShapeDtypeStruct((s, WIDTH), BF16)]
        + [jax.ShapeDtypeStruct((1, WIDTH), F32)] * 2,
        input_output_aliases={0: 0},
        compiler_params=_cparams(("arbitrary",)),
    )(dproj, du, wbr, on, rstd, gn_w, gn_b, proj)


def _ret_bwd(dproj, qr, kr, proj, rall, do_ret, c128, s128, consts, gamma, s):
    nb = s // BLK
    decay, zeta, xi = consts

    def body(dproj_in, q_ref, k_ref, v_ref, r_ref, do_ref, c_ref, s_ref, dm_ref, zt_ref, xi_ref, out_ref, gstate):
        del dproj_in
        dq_ref, dk_ref, dv_ref = (out_ref.at[:, pl.ds(off, WIDTH)] for off in (R_Q, R_K, R_V))

        @pl.when(pl.program_id(0) == 0)
        def _():
            gstate[...] = jnp.zeros_like(gstate)

        c, sg = c_ref[...], s_ref[...]
        for h in range(HEADS):
            sl = slice(h * DH, (h + 1) * DH)
            q, k, v, do = q_ref[:, sl], k_ref[:, sl], v_ref[:, sl], do_ref[:, sl]
            dm = dm_ref[h]
            g_next = gstate[h]
            gb = g_next.astype(BF16)
            a = (_dot(q, k, "nt") * dm).astype(BF16)
            da = (_dot(do, v, "nt") * dm).astype(BF16)
            dox = (do.astype(F32) * xi_ref[h]).astype(BF16)
            dq = _dot(da, k, "nn") + _dot(dox, r_ref[0, h].astype(BF16), "nt")
            dk = _dot(da, q, "tn") + _dot(v, gb, "nt") * zt_ref[h]
            kz = (k.astype(F32) * zt_ref[h]).astype(BF16)
            dv = _dot(a, do, "tn") + _dot(kz, gb, "nn")
            gstate[h] = g_next * gamma[h] + _dot(q, dox, "tn")
            dk = dk * RET_SCALE
            dq_ref[:, sl] = _rope128(dq, c, -sg).astype(BF16)
            dk_ref[:, sl] = _rope128(dk, c, -sg).astype(BF16)
            dv_ref[:, sl] = dv.astype(BF16)

    rev = lambda cb: pl.BlockSpec((BLK, WIDTH), lambda i: (nb - 1 - i, cb))
    full = lambda a: pl.BlockSpec(a.shape, lambda i: (0,) * a.ndim)
    tab = pl.BlockSpec((BLK, DH), lambda i: (nb - 1 - i, 0))
    return pl.pallas_call(
        body, name="ret_bwd", grid=(nb,),
        in_specs=[ANY, rev(0), rev(0), rev(R_V // WIDTH),
                  pl.BlockSpec((1, HEADS, DH, DH), lambda i: (nb - 1 - i, 0, 0, 0)),
                  rev(0), tab, tab, full(decay), full(zeta), full(xi)],
        out_specs=pl.BlockSpec((BLK, 3 * WIDTH), lambda i: (nb - 1 - i, R_Q // (3 * WIDTH))),
        out_shape=jax.ShapeDtypeStruct((s, PROJ_W), BF16),
        scratch_shapes=[pltpu.VMEM((HEADS, DH, DH), F32)],
        input_output_aliases={0: 0},
        compiler_params=_cparams(("arbitrary",)),
    )(dproj, qr, kr, proj, rall, do_ret, c128, s128, decay, zeta, xi)


def _attn_bwd(qq, kk, vv, do, lse, delta, s):
    t = _row_tile(s, 384)
    n = s // t

    def body(q_ref, k_ref, v_ref, do_ref, lse_ref, delta_ref, dq_ref, dk_ref, dv_ref, dk_acc, dv_acc):
        ki = pl.program_id(1)

        @pl.when(ki == 0)
        def _():
            dq_ref[...] = jnp.zeros(dq_ref.shape, F32)

        k, v = k_ref[...], v_ref[...]
        dk_acc[...] = jnp.zeros(dk_acc.shape, F32)
        dv_acc[...] = jnp.zeros(dv_acc.shape, F32)

        def rows(qi):
            return pl.ds(pl.multiple_of(qi * t, t), t)

        def products(qi):
            return _dot(k, q_ref[rows(qi), :], "nt"), _dot(v, do_ref[rows(qi), :], "nt")

        def tile(qi, st, dpt):
            q, dov = q_ref[rows(qi), :], do_ref[rows(qi), :]
            pt = jnp.exp(st - lse_ref[0, qi])
            dv_acc[...] += _dot(pt.astype(BF16), dov, "nn")
            dst = (pt * (dpt - delta_ref[0, qi])).astype(BF16)
            dk_acc[...] += _dot(dst, q, "nn")
            dq_ref[rows(qi), :] += _dot(dst, k, "tn")

        def inner(qi, carry):
            after = products(jnp.minimum(qi + 1, n - 1))
            tile(qi, *carry)
            return after

        st, dpt = products(ki)
        ahead = products(jnp.minimum(ki + 1, n - 1))
        tile(ki, jnp.where(_diag_mask(t), st, NEG_INF), dpt)
        lax.fori_loop(ki + 1, n, inner, ahead)

        dk_ref[...] = dk_acc[...].astype(BF16)
        dv_ref[...] = dv_acc[...].astype(BF16)

    head = lambda w: pl.BlockSpec((s, w), lambda h, i: (0, h))
    stat = pl.BlockSpec((1, n, 1, t), lambda h, i: (h, 0, 0, 0))
    return pl.pallas_call(
        body, name="attn_bwd", grid=(HEADS, n),
        in_specs=[head(HD), pl.BlockSpec((t, HD), lambda h, i: (i, h)), pl.BlockSpec((t, DH), lambda h, i: (i, 2 * h)),
                  head(DH), stat, stat],
        out_specs=[head(HD), pl.BlockSpec((t, HD), lambda h, i: (i, h)), pl.BlockSpec((t, DH), lambda h, i: (i, h))],
        out_shape=[jax.ShapeDtypeStruct((s, HEADS * HD), F32), jax.ShapeDtypeStruct((s, HEADS * HD), BF16),
                   jax.ShapeDtypeStruct((s, WIDTH), BF16)],
        scratch_shapes=[pltpu.VMEM((t, HD), F32), pltpu.VMEM((t, DH), F32)],
        compiler_params=_cparams(("parallel", "arbitrary")),
    )(qq, kk, vv, do, lse, delta)


def _rms_bwd(dn, nhat, r, w):
    g = dn * w
    return r * (g - nhat * jnp.mean(g * nhat, axis=-1, keepdims=True)), dn * nhat


def _mla_prep_bwd(dproj, dqq, dkk, dvv, proj, rq, rkv, qn_w, kvn_w, wq, wukv, qtab, ktab, s):
    tm = _row_tile(s, 384)
    tail = PROJ_W - C_Q

    def body(dproj_in, dqq_ref, dkk_ref, dvv_ref, cq_ref, ckv_ref, rq_ref, rkv_ref, qnw_ref, kvnw_ref, wq_ref,
             wukv_ref, qc_ref, qa_ref, qb_ref, kc_ref, ka_ref, kb_ref,
             tail_ref, dq_ref, dkv_ref, dqnw_ref, dkvnw_ref):
        del dproj_in
        dcq_ref = tail_ref.at[:, pl.ds(0, Q_RANK)]
        dckv_ref = tail_ref.at[:, pl.ds(C_KV - C_Q, KV_RANK)]
        dkpe_ref = tail_ref.at[:, pl.ds(K_PE - C_Q, 2 * DH)]

        @pl.when(pl.program_id(0) == 0)
        def _():
            dqnw_ref[...] = jnp.zeros_like(dqnw_ref)
            dkvnw_ref[...] = jnp.zeros_like(dkvnw_ref)

        qc, qa, qb = qc_ref[...], qa_ref[...], qb_ref[...]
        dkp = jnp.zeros((tm, DH), F32)
        for h in range(HEADS):
            lo, mid, hi = h * HD, h * HD + DH, (h + 1) * HD
            dq_ref[:, lo:hi] = _rope64(dqq_ref[:, lo:hi], qc, -qa, -qb).astype(BF16)
            dkv_ref[:, lo:mid] = dkk_ref[:, lo:mid]
            dkv_ref[:, mid:hi] = dvv_ref[:, h * DH:(h + 1) * DH]
            dkp = dkp + dkk_ref[:, mid:hi].astype(F32)
        dcqn = _dot(dq_ref[...], wq_ref[...], "nn")
        rq_v = rq_ref[...]
        dcq, prod = _rms_bwd(dcqn, cq_ref[...].astype(F32) * rq_v, rq_v, qnw_ref[...])
        dcq_ref[...] = dcq.astype(BF16)
        dqnw_ref[...] += jnp.sum(prod, axis=0, keepdims=True)
        dckvn = _dot(dkv_ref[...], wukv_ref[...], "nn")
        rkv_v = rkv_ref[...]
        dckv, prod = _rms_bwd(dckvn, ckv_ref[...].astype(F32) * rkv_v, rkv_v, kvnw_ref[...])
        dckv_ref[...] = dckv.astype(BF16)
        dkvnw_ref[...] += jnp.sum(prod, axis=0, keepdims=True)
        dkpe_ref[:, :DH] = _rope64(dkp, kc_ref[...], -ka_ref[...], -kb_ref[...]).astype(BF16)
        dkpe_ref[:, DH:] = jnp.zeros((tm, DH), BF16)

    row = lambda w, cb: pl.BlockSpec((tm, w), lambda i: (i, cb))
    full = lambda a: pl.BlockSpec(a.shape, lambda i: (0, 0))
    wide = jax.ShapeDtypeStruct((s, HEADS * HD), BF16)
    return pl.pallas_call(
        body, name="mla_prep_bwd", grid=(s // tm,),
        in_specs=[ANY, row(HEADS * HD, 0), row(HEADS * HD, 0), row(WIDTH, 0),
                  row(Q_RANK, C_Q // Q_RANK), row(KV_RANK, C_KV // KV_RANK), row(1, 0), row(1, 0),
                  full(qn_w), full(kvn_w), full(wq), full(wukv), row(HD, 0), row(HD, 0), row(HD, 0),
                  row(DH, 0), row(DH, 0), row(DH, 0)],
        out_specs=[row(tail, C_Q // tail), row(HEADS * HD, 0), row(HEADS * HD, 0),
                   pl.BlockSpec((1, Q_RANK), lambda i: (0, 0)), pl.BlockSpec((1, KV_RANK), lambda i: (0, 0))],
        out_shape=[jax.ShapeDtypeStruct((s, PROJ_W), BF16), wide, wide,
                   jax.ShapeDtypeStruct((1, Q_RANK), F32), jax.ShapeDtypeStruct((1, KV_RANK), F32)],
        input_output_aliases={0: 0},
        compiler_params=_cparams(("arbitrary",)),
    )(dproj, dqq, dkk, dvv, proj, proj, rq, rkv, qn_w, kvn_w, wq, wukv, *qtab, *ktab)


def _rms_in_bwd(dxn, h, r, dh2, norm_w, s):
    def body(dxn_ref, h_ref, r_ref, dh2_ref, w_ref, gx_ref, gm_ref, dw_ref):
        i = pl.program_id(0)
        r_v = r_ref[...]
        dx, prod = _rms_bwd(dxn_ref[...], h_ref[...] * r_v, r_v, w_ref[...])
        dh = dh2_ref[...] + dx

        @pl.when(i == 0)
        def _():
            gm_ref[...] = dh
            dw_ref[...] = jnp.sum(prod, axis=0, keepdims=True)

        @pl.when(i > 0)
        def _():
            gx_ref[...] = dh
            dw_ref[...] += jnp.sum(prod, axis=0, keepdims=True)

    blk = pl.BlockSpec((BLK, D), lambda i: (i, 0))
    return pl.pallas_call(
        body, name="rms_in_bwd", grid=(s // BLK,),
        in_specs=[blk, blk, pl.BlockSpec((BLK, 1), lambda i: (i, 0)), blk, pl.BlockSpec((1, D), lambda i: (0, 0))],
        out_specs=[pl.BlockSpec((BLK, D), lambda i: (jnp.maximum(i - 1, 0), 0)), pl.BlockSpec((BLK, D), lambda i: (0, 0)),
                   pl.BlockSpec((1, D), lambda i: (0, 0))],
        out_shape=[jax.ShapeDtypeStruct((s - BLK, D), F32), jax.ShapeDtypeStruct((BLK, D), F32),
                   jax.ShapeDtypeStruct((1, D), F32)],
        compiler_params=_cparams(("arbitrary",)),
    )(dxn, h, r, dh2, norm_w)


def _adam_math(w, g, m, v):
    m = ADAM_B1 * m + (1.0 - ADAM_B1) * g
    v = ADAM_B2 * v + (1.0 - ADAM_B2) * (g * g)
    m_hat = m / (1.0 - ADAM_B1 ** ADAM_STEP)
    v_hat = v / (1.0 - ADAM_B2 ** ADAM_STEP)
    return -ADAM_LR * (m_hat / (jnp.sqrt(v_hat) + ADAM_EPS) + ADAM_WD * w), m, v


def _adamw(w, g, m, v, name):
    rows, cols = w.shape
    tr = rows
    for cand in (128, 64, 32, 16, 8):
        if rows % cand == 0:
            tr = cand
            break

    def body(w_ref, g_ref, m_ref, v_ref, d_ref, nm_ref, nv_ref):
        d_ref[...], nm_ref[...], nv_ref[...] = _adam_math(w_ref[...], g_ref[...], m_ref[...], v_ref[...])

    spec = pl.BlockSpec((tr, cols), lambda i: (i, 0))
    return pl.pallas_call(
        body, name=name, grid=(rows // tr,), in_specs=[spec] * 4, out_specs=[spec] * 3,
        out_shape=[jax.ShapeDtypeStruct((rows, cols), F32)] * 3,
        compiler_params=_cparams(("parallel",)),
    )(w, g, m, v)


def _adamw_small(w, gall, m, v):
    def body(w_ref, g_ref, m_ref, v_ref, gs_ref, d_ref, nm_ref, nv_ref):
        g = g_ref[0]
        for dev in range(1, 8):
            g = g + g_ref[dev]
        gs_ref[...] = g
        d_ref[...], nm_ref[...], nv_ref[...] = _adam_math(w_ref[...], g, m_ref[...], v_ref[...])

    return pl.pallas_call(
        body, name="adamw_small", out_shape=[jax.ShapeDtypeStruct((SM_ROWS, 128), F32)] * 4,
        compiler_params=pltpu.CompilerParams(vmem_limit_bytes=VMEM_LIMIT),
    )(w, gall, m, v)


ADD_ROWS = 464
HALF_BLOCKS = PK_HALF // ADD_ROWS


def _rs_core_add(gpk, land, pos):
    def body(pos_ref, a_ref, b_ref, o_ref):
        o_ref[...] = (a_ref[...].astype(F32) + b_ref[...].astype(F32)).astype(BF16)

    return pl.pallas_call(
        body, name="rs_core_add",
        grid_spec=pltpu.PrefetchScalarGridSpec(
            num_scalar_prefetch=1, grid=(N_CHIPS, HALF_BLOCKS),
            in_specs=[pl.BlockSpec((1, ADD_ROWS, D), lambda s, i, pos_ref: (s, pos_ref[1] * HALF_BLOCKS + i, 0)),
                      pl.BlockSpec((1, ADD_ROWS, D), lambda s, i, pos_ref: (s, i, 0))],
            out_specs=pl.BlockSpec((1, ADD_ROWS, D), lambda s, i, pos_ref: (s, i, 0))),
        out_shape=jax.ShapeDtypeStruct((N_CHIPS, PK_HALF, D), BF16),
        compiler_params=_cparams(("parallel", "parallel")),
    )(pos, gpk, land)


def _rs_chip_add(cp, land, pos):
    def body(pos_ref, a_ref, l_ref, o_ref):
        o_ref[0] = ((a_ref[0].astype(F32) + l_ref[0].astype(F32)) + l_ref[1].astype(F32)) + l_ref[2].astype(F32)

    return pl.pallas_call(
        body, name="rs_chip_add",
        grid_spec=pltpu.PrefetchScalarGridSpec(
            num_scalar_prefetch=1, grid=(HALF_BLOCKS,),
            in_specs=[pl.BlockSpec((1, ADD_ROWS, D), lambda i, pos_ref: (pos_ref[0], i, 0)),
                      pl.BlockSpec((3, ADD_ROWS, D), lambda i, pos_ref: (0, i, 0))],
            out_specs=pl.BlockSpec((1, ADD_ROWS, D), lambda i, pos_ref: (pos_ref[1], i, 0))),
        out_shape=jax.ShapeDtypeStruct((2, PK_HALF, D), F32),
        compiler_params=_cparams(("parallel",)),
    )(pos, cp, land)


ANY = pl.BlockSpec(memory_space=pl.ANY)


def _mesh_pos():
    return lax.axis_index("x"), lax.axis_index("y"), lax.axis_index("c")


def _chip_peer(x, y, c, r):
    return (jnp.bitwise_xor(x, r >> 1), jnp.bitwise_xor(y, r & 1), c)


def _allgather_w_in(big0, meta_loc):
    def body(big0_ref, meta_ref, big_ref, metaf_ref, lsem, ssem, rsem, fssem, frsem, msem_s, msem_r):
        del big0_ref
        x, y, c = _mesh_pos()
        j = 2 * x + y
        me, sibling = (x, y, c), (x, y, 1 - c)

        def half_wait(s_sem, r_sem):
            rows = big_ref.at[pl.ds(0, AG_IN_HALF), :]
            return pltpu.make_async_remote_copy(src_ref=rows, dst_ref=rows, send_sem=s_sem, recv_sem=r_sem,
                                                device_id=me, device_id_type=MESH)

        own_meta = pltpu.make_async_copy(meta_ref, metaf_ref.at[j], lsem)
        own_meta.start()
        for s in range(N_CHIPS):
            for hf in range(2):
                @pl.when((j == s) & (c == hf))
                def _():
                    for r in (1, 2, 3):
                        for b, n in _ag_half(s, True, hf):
                            rows = big_ref.at[pl.ds(b, n), :]
                            pltpu.make_async_remote_copy(
                                src_ref=rows, dst_ref=rows, send_sem=ssem.at[r - 1], recv_sem=rsem.at[r - 1],
                                device_id=_chip_peer(x, y, c, r), device_id_type=MESH).start()

        meta_copies = [pltpu.make_async_remote_copy(
            src_ref=meta_ref, dst_ref=metaf_ref.at[j], send_sem=msem_s.at[r - 1], recv_sem=msem_r.at[r - 1],
            device_id=_chip_peer(x, y, c, r), device_id_type=MESH) for r in (1, 2, 3)]
        for cp in meta_copies:
            cp.start()
        for r in (1, 2, 3):
            half_wait(ssem.at[r - 1], rsem.at[r - 1]).wait_recv()
            src_shard = jnp.bitwise_xor(j, r)
            for s in range(N_CHIPS):
                for hf in range(2):
                    @pl.when((src_shard == s) & (c == hf))
                    def _():
                        for b, n in _ag_half(s, True, hf):
                            rows = big_ref.at[pl.ds(b, n), :]
                            pltpu.make_async_remote_copy(
                                src_ref=rows, dst_ref=rows, send_sem=fssem.at[r - 1], recv_sem=frsem.at[r - 1],
                                device_id=sibling, device_id_type=MESH).start()
        for r in (1, 2, 3):
            half_wait(fssem.at[r - 1], frsem.at[r - 1]).wait_recv()
        for r in (1, 2, 3):
            half_wait(ssem.at[r - 1], rsem.at[r - 1]).wait_send()
            half_wait(fssem.at[r - 1], frsem.at[r - 1]).wait_send()
        for cp in meta_copies:
            cp.wait_send()
            cp.wait_recv()
        own_meta.wait()

    dma3 = pltpu.SemaphoreType.DMA((3,))
    return pl.pallas_call(
        body, name="allgather_w_in", in_specs=[ANY, ANY], out_specs=[ANY, ANY],
        out_shape=[jax.ShapeDtypeStruct((BIG_ROWS, D), BF16), jax.ShapeDtypeStruct((N_CHIPS,) + meta_loc.shape, F32)],
        scratch_shapes=[pltpu.SemaphoreType.DMA(()), dma3, dma3, dma3, dma3, dma3, dma3],
        input_output_aliases={0: 0},
        compiler_params=pltpu.CompilerParams(has_side_effects=True),
    )(big0, meta_loc)


HBM = pl.BlockSpec(memory_space=pltpu.HBM)
SEM = pl.BlockSpec(memory_space=pltpu.SEMAPHORE)
EFFECT = pltpu.SideEffectType.DATAFLOW_SIDE_EFFECTING
REST_PEERS = [(r, dc) for r in (1, 2, 3) for dc in (0, 1)]


def _allgather_rest_start(big):
    def body(big_ref, *rest):
        s_sems, r_sems = rest[:6], rest[6:12]
        x, y, c = _mesh_pos()
        j = 2 * x + y
        for s in range(N_CHIPS):
            for hf in range(2):
                @pl.when((j == s) & (c == hf))
                def _():
                    for k, (r, dc) in enumerate(REST_PEERS):
                        for b, n in _ag_half(s, False, hf):
                            rows = big_ref.at[pl.ds(b, n), :]
                            pltpu.make_async_remote_copy(
                                src_ref=rows, dst_ref=rows, send_sem=s_sems[k], recv_sem=r_sems[2 * (r - 1) + hf],
                                device_id=_chip_peer(x, y, dc, r), device_id_type=MESH).start()

    sem = pltpu.SemaphoreType.DMA(())
    out = pl.pallas_call(
        body, name="allgather_rest_start", out_shape=(sem,) * 12 + (pltpu.HBM(big.shape, big.dtype),),
        in_specs=(HBM,), out_specs=(SEM,) * 12 + (HBM,), input_output_aliases={0: 12},
        compiler_params=pltpu.CompilerParams(has_side_effects=EFFECT),
    )(pltpu.with_memory_space_constraint(big, pltpu.HBM))
    return out[:12], out[12]


def _allgather_rest_wait(sems, big, after):
    def body(big_ref, *rest):
        s_sems, r_sems = rest[:6], rest[6:12]
        me = _mesh_pos()
        rows = big_ref.at[pl.ds(0, AG_REST_HALF), :]
        for k in range(6):
            copy = pltpu.make_async_remote_copy(src_ref=rows, dst_ref=rows, send_sem=s_sems[k], recv_sem=r_sems[k],
                                                device_id=me, device_id_type=MESH)
            copy.wait_send()
            copy.wait_recv()

    return pl.pallas_call(
        body, name="allgather_rest_wait", out_shape=pltpu.HBM(big.shape, big.dtype),
        in_specs=(HBM,) + (SEM,) * 12 + (pl.BlockSpec(memory_space=pl.ANY),), out_specs=HBM,
        input_output_aliases={0: 0}, compiler_params=pltpu.CompilerParams(has_side_effects=EFFECT),
    )(big, *sems, after)


RS_SPLIT = 2


def _rs_core_exchange(gpk):
    n = PK_HALF // RS_SPLIT

    def body(g_ref, land_ref, ssem, rsem):
        x, y, c = _mesh_pos()
        me, sibling = (x, y, c), (x, y, 1 - c)
        for s in range(N_CHIPS):
            for q in range(RS_SPLIT):
                src = pl.ds(pl.multiple_of((1 - c) * PK_HALF + q * n, 16), n)
                pltpu.make_async_remote_copy(
                    src_ref=g_ref.at[s, src, :], dst_ref=land_ref.at[s, pl.ds(q * n, n), :],
                    send_sem=ssem, recv_sem=rsem, device_id=sibling, device_id_type=MESH).start()
        whole = pltpu.make_async_remote_copy(src_ref=land_ref, dst_ref=land_ref, send_sem=ssem, recv_sem=rsem,
                                             device_id=me, device_id_type=MESH)
        whole.wait_recv()
        whole.wait_send()

    return pl.pallas_call(
        body, name="rs_core_exchange", in_specs=[ANY], out_specs=ANY,
        out_shape=jax.ShapeDtypeStruct((N_CHIPS, PK_HALF, D), BF16),
        scratch_shapes=[pltpu.SemaphoreType.DMA(())] * 2,
        compiler_params=pltpu.CompilerParams(has_side_effects=True),
    )(gpk)


def _rs_chip_start(cp):
    def body(cp_ref, land_ref, s1, s2, s3, r1, r2, r3, cp_thru, land_thru, token):
        x, y, c = _mesh_pos()
        j = 2 * x + y
        for r, s_sem, r_sem in zip((1, 2, 3), (s1, s2, s3), (r1, r2, r3)):
            pltpu.make_async_remote_copy(
                src_ref=cp_ref.at[jnp.bitwise_xor(j, r)], dst_ref=land_ref.at[r - 1], send_sem=s_sem, recv_sem=r_sem,
                device_id=_chip_peer(x, y, c, r), device_id_type=MESH).start()
        token[...] = jnp.zeros_like(token)

    land_shape = (3, PK_HALF, D)
    sem = pltpu.SemaphoreType.DMA(())
    out = pl.pallas_call(
        body, name="rs_chip_start",
        out_shape=(sem,) * 6 + (pltpu.HBM(cp.shape, cp.dtype), pltpu.HBM(land_shape, BF16),
                                jax.ShapeDtypeStruct((8, 128), F32)),
        in_specs=(HBM, HBM), out_specs=(SEM,) * 6 + (HBM, HBM, pl.BlockSpec(memory_space=pltpu.VMEM)),
        input_output_aliases={0: 6, 1: 7},
        compiler_params=pltpu.CompilerParams(has_side_effects=EFFECT),
    )(pltpu.with_memory_space_constraint(cp, pltpu.HBM),
      pltpu.with_memory_space_constraint(lax.empty(land_shape, BF16), pltpu.HBM))
    return out[:6], out[6], out[7], out[8]


def _rs_chip_wait(sems, cp_thru, land_thru, after):
    def body(cp_ref, land_ref, s1, s2, s3, r1, r2, r3, after_ref, cp_out, land_out):
        me = _mesh_pos()
        for r, s_sem, r_sem in zip((1, 2, 3), (s1, s2, s3), (r1, r2, r3)):
            copy = pltpu.make_async_remote_copy(src_ref=cp_ref.at[0], dst_ref=land_ref.at[r - 1], send_sem=s_sem,
                                                recv_sem=r_sem, device_id=me, device_id_type=MESH)
            copy.wait_send()
            copy.wait_recv()

    return pl.pallas_call(
        body, name="rs_chip_wait",
        out_shape=(pltpu.HBM(cp_thru.shape, cp_thru.dtype), pltpu.HBM(land_thru.shape, land_thru.dtype)),
        in_specs=(HBM, HBM) + (SEM,) * 6 + (pl.BlockSpec(memory_space=pl.ANY),), out_specs=(HBM, HBM),
        input_output_aliases={0: 0, 1: 1},
        compiler_params=pltpu.CompilerParams(has_side_effects=EFFECT),
    )(cp_thru, land_thru, *sems, after)


def _rs_finish(full, small):
    n = PK_HALF // RS_SPLIT

    def body(full_in_ref, sm_ref, full_ref, all_ref, lsem, ssem, rsem, sm_s, sm_r):
        del full_in_ref
        x, y, c = _mesh_pos()
        me, sibling = (x, y, c), (x, y, 1 - c)
        my_id = 4 * x + 2 * y + c
        own_sm = pltpu.make_async_copy(sm_ref, all_ref.at[my_id], lsem)
        own_sm.start()
        for q in range(RS_SPLIT):
            rows = full_ref.at[c, pl.ds(q * n, n), :]
            pltpu.make_async_remote_copy(src_ref=rows, dst_ref=rows, send_sem=ssem, recv_sem=rsem,
                                         device_id=sibling, device_id_type=MESH).start()
        half = pltpu.make_async_remote_copy(src_ref=full_ref.at[c], dst_ref=full_ref.at[c], send_sem=ssem,
                                            recv_sem=rsem, device_id=me, device_id_type=MESH)
        smalls = [pltpu.make_async_remote_copy(
            src_ref=sm_ref, dst_ref=all_ref.at[my_id], send_sem=sm_s.at[r - 1], recv_sem=sm_r.at[r - 1],
            device_id=(jnp.bitwise_xor(x, r >> 2), jnp.bitwise_xor(y, (r >> 1) & 1), jnp.bitwise_xor(c, r & 1)),
            device_id_type=MESH) for r in range(1, 8)]
        for cpy in smalls:
            cpy.start()
        half.wait_recv()
        half.wait_send()
        for cpy in smalls:
            cpy.wait_recv()
        for cpy in smalls:
            cpy.wait_send()
        own_sm.wait()

    dma7 = pltpu.SemaphoreType.DMA((7,))
    return pl.pallas_call(
        body, name="rs_finish", in_specs=[ANY, ANY], out_specs=[ANY, ANY],
        out_shape=[jax.ShapeDtypeStruct((2, PK_HALF, D), F32), jax.ShapeDtypeStruct((8, SM_ROWS, 128), F32)],
        scratch_shapes=[pltpu.SemaphoreType.DMA(()), pltpu.SemaphoreType.DMA(()), pltpu.SemaphoreType.DMA(()), dma7, dma7],
        input_output_aliases={0: 0},
        compiler_params=pltpu.CompilerParams(has_side_effects=True),
    )(full, small)


def _rope_tables(s):
    pos = jnp.arange(s, dtype=F32) - PAD_FRONT

    def cs(d):
        inv = ROPE_BASE ** (-jnp.arange(0, d, 2, dtype=F32) / d)
        ang = pos[:, None] * inv[None, :]
        return jnp.cos(ang), jnp.sin(ang)

    c, sn = cs(ROPE_D)
    z = jnp.zeros_like(c)
    ktab = (jnp.concatenate([c, c, z, z], axis=1), jnp.concatenate([-sn, z, z, z], axis=1),
            jnp.concatenate([z, sn, z, z], axis=1))
    one = jnp.ones_like(c)
    qtab = tuple(ATT_SCALE * jnp.concatenate(parts, axis=1) for parts in (
        [one, one, one, one, c, c, z, z], [z, z, z, z, -sn, z, z, z], [z, z, z, z, z, sn, z, z]))
    lane = jnp.arange(HD) == DH + ROPE_D
    qbias = lane.astype(F32)[None, :]
    kbias = jnp.where((pos < 0)[:, None] & lane[None, DH:], NEG_INF, 0.0).astype(F32)
    c, sn = cs(DH)
    return qtab, ktab, qbias, kbias, jnp.concatenate([c, c], axis=1), jnp.concatenate([-sn, sn], axis=1)


def _grad_mm(a, b, m, n, s, name):
    return _matmul(a, b, "tn", m, n, s, min(m, 512), min(n, 1024), s, BF16, name, n_outer=True)


def _local_step(x2, target2, hp0, big, small_w, gather_wait=None, reduce_start=None, reduce_wait=None):
    norm_w, qn_w, kvn_w, gn_w, gn_b, fin_w = small_w
    s = x2.shape[0] + BLK
    tm = _row_tile(s, 1408)
    qtab, ktab, qbias, kbias, c128, s128 = _rope_tables(s)
    consts, gamma = _ret_consts()

    h, xn, r1 = _rms_in(x2, hp0, norm_w, s)
    proj = _matmul(xn, big, "nt", s, PROJ_W, D, tm, 512, D, BF16, "proj", b_off=(BIG_IN // 512, 0))
    if gather_wait is not None:
        big = gather_wait(big, proj)
    uq = big[BIG_UQ:BIG_UQ + 384].reshape(HEADS, DH + ROPE_D, Q_RANK)
    wq = jnp.pad(uq, ((0, 0), (0, HD - DH - ROPE_D), (0, 0))).reshape(HEADS * HD, Q_RANK)
    wukv = big[BIG_UKV:BIG_UKV + 256].reshape(2 * WIDTH, KV_RANK)
    wbm = big[BIG_BM:BIG_BM + 1024].reshape(D, WIDTH)
    wbr = big[BIG_BR:BIG_BR + 1024].reshape(D, WIDTH)
    cqn, rq, ckvn, rkv, qq, kk, vv = _mla_prep(proj, qn_w, kvn_w, wq, wukv, qtab, ktab, qbias, kbias, s)
    o_mla, y_mla, lse = _attn_fwd(qq, kk, vv, proj, s)
    y_ret, on, rstd, qr, kr, rall = _ret_fwd(proj, gn_w, gn_b, c128, s128, consts, gamma, s)
    u_mla, u_ret, merged = _merge_fwd(y_mla, y_ret, wbm, wbr, proj, s)
    h2 = _out_fwd(merged, big, h, s)
    dh2, dh2b, loss_blk, dfin = _loss_bwd(h2, target2, fin_w, s)

    dproj, du_mla, du_ret = _merge_bwd(dh2b, big, u_mla, u_ret, proj, s)
    g_out = _grad_mm(merged, dh2b, D, D, s, "grad_w_out")
    dproj, do_mla, delta = _branch_mla_bwd(dproj, du_mla, wbm, o_mla, proj, s)
    g_bm = _grad_mm(du_mla, y_mla, D, WIDTH, s, "grad_w_branch_mla")
    dproj, do_ret, dgw, dgb = _branch_ret_bwd(dproj, du_ret, wbr, on, rstd, gn_w, gn_b, proj, s)
    g_br = _grad_mm(du_ret, y_ret, D, WIDTH, s, "grad_w_branch_ret")
    dproj = _ret_bwd(dproj, qr, kr, proj, rall, do_ret, c128, s128, consts, gamma, s)
    dqq, dkk, dvv = _attn_bwd(qq, kk, vv, do_mla, lse, delta.reshape(lse.shape), s)
    dproj, dq, dkv, dqnw, dkvnw = _mla_prep_bwd(dproj, dqq, dkk, dvv, proj, rq, rkv, qn_w, kvn_w, wq, wukv,
                                                qtab, ktab, s)
    g_q = _grad_mm(dq, cqn, HEADS * HD, Q_RANK, s, "grad_w_uq")
    g_ukv = _grad_mm(dkv, ckvn, 2 * WIDTH, KV_RANK, s, "grad_w_ukv")
    g_in = _grad_mm(dproj, xn, PROJ_W, D, s, "grad_w_in")

    g_uq = g_q.reshape(HEADS, HD, Q_RANK)[:, :DH + ROPE_D]
    parts = [(g_out, 512), (g_bm.reshape(1024, D), 256), (g_br.reshape(1024, D), 256), (g_uq.reshape(384, D), 96),
             (g_ukv.reshape(256, D), 64)]

    def shard_rows(sh):
        out = [g_in[b - BIG_IN:b - BIG_IN + n] for p, b, n in _pieces(sh) if p < PK_OUT]
        out += [p[n * sh:n * (sh + 1)] for p, n in parts]
        return jnp.concatenate(out + [jnp.zeros((PK_ROWS - PK_PAD, D), BF16)], axis=0)

    gbig = jnp.stack([shard_rows(sh) for sh in range(N_CHIPS)])
    token, travelling = (None, gbig) if reduce_start is None else reduce_start(gbig)
    dxn = _matmul(dproj, big, "nn", s, D, PROJ_W, tm, 1024, 1024, F32, "dxn", b_off=(BIG_IN // 1024, 0), after=token)
    grad_x, gmeta_blk, dnorm = _rms_in_bwd(dxn, h, r1, dh2, norm_w, s)
    small = _small_rows((dnorm, dqnw, dkvnw, dgw, dgb, dfin), loss_blk, gmeta_blk[PAD_FRONT:])
    return grad_x, travelling if reduce_wait is None else reduce_wait(travelling, dnorm), small


def _small_rows(ws, first=None, last=None):
    def part(a, rows):
        a = a.reshape(-1, 128)
        return a if a.shape[0] == rows else jnp.pad(a, ((0, rows - a.shape[0]), (0, 0)))

    bounds = (SM_NORM, SM_QN, SM_KVN, SM_GNW, SM_GNB, SM_FIN, SM_META)
    first = jnp.zeros((SM_NORM, 128), F32) if first is None else first
    last = jnp.zeros((SM_ROWS - SM_META, 128), F32) if last is None else last
    return jnp.concatenate([part(first, SM_NORM)] + [part(w, bounds[k + 1] - bounds[k]) for k, w in enumerate(ws)]
                           + [part(last, SM_ROWS - SM_META)], axis=0)


def kernel(x, meta, norm_w, w_in, mla_q_norm_w, mla_w_uq, mla_kv_norm_w, mla_w_ukv, ret_gn_w, ret_gn_b, w_branch_mla, w_branch_ret, w_out, final_norm_w, loss_target, m_meta, m_norm_w, m_w_in, m_mla_q_norm_w, m_mla_w_uq, m_mla_kv_norm_w, m_mla_w_ukv, m_ret_gn_w, m_ret_gn_b, m_w_branch_mla, m_w_branch_ret, m_w_out, m_final_norm_w, v_meta, v_norm_w, v_w_in, v_mla_q_norm_w, v_mla_w_uq, v_mla_kv_norm_w, v_mla_w_ukv, v_ret_gn_w, v_ret_gn_b, v_w_branch_mla, v_w_branch_ret, v_w_out, v_final_norm_w):
    j = 2 * lax.axis_index("x") + lax.axis_index("y")
    tr = lambda w: w[0].T.reshape(-1, D).astype(BF16)
    pos = jnp.stack([j, lax.axis_index("c")]).astype(jnp.int32)
    put = lambda buf, rows, off: lax.dynamic_update_slice(buf, rows, (off, 0))
    big0 = lax.empty((BIG_ROWS, D), BF16)
    big0 = put(big0, w_out[0].astype(BF16), BIG_OUT + 512 * j)
    big0 = put(big0, tr(w_branch_mla), BIG_BM + 256 * j)
    big0 = put(big0, tr(w_branch_ret), BIG_BR + 256 * j)
    big0 = put(big0, tr(mla_w_uq), BIG_UQ + 96 * j)
    big0 = put(big0, tr(mla_w_ukv), BIG_UKV + 64 * j)
    big0 = put(big0, jnp.zeros((AG_JUNK_END - AG_JUNK_REST, D), BF16), AG_JUNK_REST)
    cuts, dest = _in_cuts()
    w_in_t = tr(w_in)
    dest = jnp.asarray(dest, jnp.int32)[j]
    for k in range(len(cuts) - 1):
        big0 = put(big0, w_in_t[cuts[k]:cuts[k + 1]], dest[k])
    big0 = put(big0, jnp.zeros((ZERO_ROWS, D), BF16), BIG_IN + IN_WIDTH)
    big, meta_all = _allgather_w_in(big0, meta)
    gather_sems, big = _allgather_rest_start(big)

    def gather_wait(big_travelling, after):
        return _allgather_rest_wait(gather_sems, big_travelling, after)

    meta_full = meta_all.transpose(1, 0, 2).reshape(N_META, D)
    hp0 = jnp.concatenate([jnp.zeros((PAD_FRONT, D), F32), meta_full], axis=0)
    small_w = (norm_w, mla_q_norm_w, mla_kv_norm_w, ret_gn_w, ret_gn_b, final_norm_w.reshape(1, D))

    def reduce_start(gbig):
        chip_part = _rs_core_add(gbig, _rs_core_exchange(gbig), pos)
        sems, part_thru, land_thru, token = _rs_chip_start(chip_part)
        return token, (sems, part_thru, land_thru)

    def reduce_wait(state, after):
        return _rs_chip_wait(*state, after)

    grad_x, (chip_part, land2), small = _local_step(x[0], loss_target[0], hp0, big, small_w, gather_wait,
                                                    reduce_start, reduce_wait)
    full, small_all = _rs_finish(_rs_chip_add(chip_part, land2, pos), small)
    full = full.reshape(PK_ROWS, D)

    untr = lambda lo, hi, rows: full[lo:hi].reshape(rows, -1).T
    grads = {
        "w_out": full[PK_OUT:PK_BM], "w_branch_mla": untr(PK_BM, PK_BR, 512), "w_branch_ret": untr(PK_BR, PK_UQ, 512),
        "mla_w_uq": untr(PK_UQ, PK_UKV, 384), "mla_w_ukv": untr(PK_UKV, PK_PAD, 512), "w_in": untr(PK_IN, PK_OUT, IN_SHARD),
    }
    big_w = {"w_in": (w_in, m_w_in, v_w_in), "mla_w_uq": (mla_w_uq, m_mla_w_uq, v_mla_w_uq),
             "mla_w_ukv": (mla_w_ukv, m_mla_w_ukv, v_mla_w_ukv),
             "w_branch_mla": (w_branch_mla, m_w_branch_mla, v_w_branch_mla),
             "w_branch_ret": (w_branch_ret, m_w_branch_ret, v_w_branch_ret), "w_out": (w_out, m_w_out, v_w_out)}
    res = {}
    for name, (w, m, v) in big_w.items():
        d, nm, nv = _adamw(w[0], grads[name], m[0], v[0], "adamw_" + name)
        res[name] = (grads[name][None], d[None], nm[None], nv[None])

    small_m = (m_norm_w, m_mla_q_norm_w, m_mla_kv_norm_w, m_ret_gn_w, m_ret_gn_b, m_final_norm_w.reshape(1, D))
    small_v = (v_norm_w, v_mla_q_norm_w, v_mla_kv_norm_w, v_ret_gn_w, v_ret_gn_b, v_final_norm_w.reshape(1, D))
    gs, ds, ms, vs = _adamw_small(_small_rows(small_w), small_all, _small_rows(small_m), _small_rows(small_v))
    names = ["norm_w", "mla_q_norm_w", "mla_kv_norm_w", "ret_gn_w", "ret_gn_b", "final_norm_w"]
    bounds = [SM_NORM, SM_QN, SM_KVN, SM_GNW, SM_GNB, SM_FIN]
    for k, name in enumerate(names):
        shape = (D,) if name == "final_norm_w" else (1, -1)
        rows = small_w[k].size // 128
        res[name] = tuple(a[bounds[k]:bounds[k] + rows].reshape(shape) for a in (gs, ds, ms, vs))
    g_meta = lax.dynamic_slice_in_dim(gs[SM_META:SM_META + 256].reshape(N_META, D), j * (D // N_CHIPS), D // N_CHIPS, axis=1)
    res["meta"] = (g_meta,) + tuple(_adamw(meta, g_meta, m_meta, v_meta, "adamw_meta"))

    order = ["meta", "norm_w", "w_in", "mla_q_norm_w", "mla_w_uq", "mla_kv_norm_w", "mla_w_ukv", "ret_gn_w", "ret_gn_b",
             "w_branch_mla", "w_branch_ret", "w_out", "final_norm_w"]
    return (gs[0, 0], grad_x[None]) + tuple(res[n][k] for k in range(4) for n in order)
```

```python
import functools
import math

import numpy as np
import jax
import jax.numpy as jnp
from jax import lax
from jax.experimental import pallas as pl
from jax.experimental.pallas import tpu as pltpu

F32 = jnp.float32
BF16 = jnp.bfloat16
MESH = pl.DeviceIdType.MESH

D = 2048
N_META = 16
BLK = 128
PAD_FRONT = BLK - N_META
HEADS = 8
DH = 128
ROPE_D = 64
Q_RANK = 512
KV_RANK = 256
WIDTH = HEADS * DH
ROPE_BASE = 10000.0
NORM_EPS = 1e-6
GN_EPS = 1e-5
NEG_INF = -1e30
ATT_SCALE = (DH + ROPE_D) ** -0.5
RET_SCALE = DH ** -0.5
IN_WIDTH = 10048
N_CHIPS = 4
IN_SHARD = IN_WIDTH // N_CHIPS
ADAM_LR, ADAM_B1, ADAM_B2, ADAM_EPS, ADAM_WD, ADAM_STEP = 0.001, 0.9, 0.999, 1e-08, 0.01, 10

R_Q, R_K, R_V, Z_RET, Z_MLA, GATE0, GATE1 = 0, 1024, 2048, 3072, 4096, 5120, 5120 + D
C_Q = 5120 + 2 * D
C_KV = C_Q + Q_RANK
K_PE = C_KV + KV_RANK
PROJ_W = 10240
IN_RUNS = ((0, 832, C_Q), (832, 1856, Z_MLA), (1856, 4928, R_Q), (4928, 5952, Z_RET), (5952, IN_WIDTH, GATE0))

BIG_OUT, BIG_BM, BIG_BR, BIG_UQ, BIG_UKV, BIG_JUNK, BIG_IN = 0, 2048, 3072, 4096, 4480, 4736, 5120
BIG_ROWS = BIG_IN + PROJ_W
ZERO_ROWS = PROJ_W - IN_WIDTH
PK_IN, PK_OUT, PK_BM, PK_BR, PK_UQ, PK_UKV, PK_PAD, PK_ROWS = 0, 2512, 3024, 3280, 3536, 3632, 3696, 3712
PK_HALF = PK_ROWS // 2
SM_LOSS, SM_NORM, SM_QN, SM_KVN, SM_GNW, SM_GNB, SM_FIN, SM_META, SM_ROWS = 0, 8, 24, 32, 40, 48, 56, 72, 328

VMEM_LIMIT = 56 * 1024 * 1024


def _pieces(shard):
    j = shard
    out = [(PK_OUT, BIG_OUT + 512 * j, 512), (PK_BM, BIG_BM + 256 * j, 256), (PK_BR, BIG_BR + 256 * j, 256),
           (PK_UQ, BIG_UQ + 96 * j, 96), (PK_UKV, BIG_UKV + 64 * j, 64)]
    for lo, hi, new in IN_RUNS:
        a, b = max(lo, IN_SHARD * j), min(hi, IN_SHARD * (j + 1))
        if b > a:
            out.append((PK_IN + a - IN_SHARD * j, BIG_IN + new + a - lo, b - a))
    out += [(PK_PAD, BIG_JUNK + 16 * j, 16)]
    return out


def _in_cuts():
    cuts = sorted({0, IN_SHARD} | {b - IN_SHARD * j for j in range(N_CHIPS) for lo, hi, _ in IN_RUNS for b in (lo, hi)
                                   if 0 < b - IN_SHARD * j < IN_SHARD})

    def new_row(col):
        return next(BIG_IN + new + col - lo for lo, hi, new in IN_RUNS if lo <= col < hi)

    return cuts, [[new_row(IN_SHARD * j + c) for c in cuts[:-1]] for j in range(N_CHIPS)]


AG_JUNK_REST, AG_JUNK_IN, AG_JUNK_END = 4736, 4864, 4928
AG_REST_HALF, AG_IN_HALF = 608, 1264


def _ag_half(shard, w_in, half):
    pieces = [(b, n) for p, b, n in _pieces(shard) if p < PK_PAD and (p < PK_OUT) == w_in]
    pieces.append((AG_JUNK_IN + 16 * shard, 16) if w_in else (AG_JUNK_REST + 32 * shard, 32))
    size = AG_IN_HALF if w_in else AG_REST_HALF
    out, pos = [], 0
    for b, n in pieces:
        s, e = max(pos, half * size), min(pos + n, (half + 1) * size)
        if e > s:
            out.append((b + s - pos, e - s))
        pos += n
    assert pos == 2 * size
    return out


def _grad_half(shard, half):
    j = shard
    pieces = [(0, b - BIG_IN, p, n) for p, b, n in _pieces(j) if p < PK_OUT]
    pieces += [(1, 512 * j, PK_OUT, 512), (2, 256 * j, PK_BM, 256), (3, 256 * j, PK_BR, 256), (4, 96 * j, PK_UQ, 96),
               (5, 64 * j, PK_UKV, 64), (0, IN_WIDTH, PK_PAD, PK_ROWS - PK_PAD)]
    lo, hi = half * PK_HALF, (half + 1) * PK_HALF
    out = []
    for k, r0, p, n in pieces:
        s, e = max(p, lo), min(p + n, hi)
        if e > s:
            out.append((k, r0 + s - p, s - lo, e - s))
    assert sum(n for _, _, _, n in out) == PK_HALF
    return out


def _row_tile(rows, cap):
    best = BLK
    for t in range(BLK, cap + 1, BLK):
        if rows % t == 0:
            best = t
    return best


def _cparams(sem):
    return pltpu.CompilerParams(dimension_semantics=sem, vmem_limit_bytes=VMEM_LIMIT)


def _dot(a, b, form):
    dn = {"nt": (((1,), (1,)), ((), ())), "nn": (((1,), (0,)), ((), ())), "tn": (((0,), (0,)), ((), ()))}[form]
    return lax.dot_general(a, b, dn, preferred_element_type=F32)


def _sigmoid(v):
    return 1.0 / (1.0 + jnp.exp(-v))


def _matmul(a, b, form, m, n, k, tm, tn, tk, out_dtype, name, a_off=(0, 0), b_off=(0, 0), n_outer=False, after=None):
    nk = k // tk
    gi, gj = m // tm, n // tn

    def ij(g0, g1):
        return (g1, g0) if n_outer else (g0, g1)

    if form == "nt":
        a_spec = pl.BlockSpec((tm, tk), lambda g0, g1, kk: (ij(g0, g1)[0] + a_off[0], kk + a_off[1]))
        b_spec = pl.BlockSpec((tn, tk), lambda g0, g1, kk: (ij(g0, g1)[1] + b_off[0], kk + b_off[1]))
    elif form == "nn":
        a_spec = pl.BlockSpec((tm, tk), lambda g0, g1, kk: (ij(g0, g1)[0] + a_off[0], kk + a_off[1]))
        b_spec = pl.BlockSpec((tk, tn), lambda g0, g1, kk: (kk + b_off[0], ij(g0, g1)[1] + b_off[1]))
    else:
        a_spec = pl.BlockSpec((tk, tm), lambda g0, g1, kk: (kk + a_off[0], ij(g0, g1)[0] + a_off[1]))
        b_spec = pl.BlockSpec((tk, tn), lambda g0, g1, kk: (kk + b_off[0], ij(g0, g1)[1] + b_off[1]))
    o_spec = pl.BlockSpec((tm, tn), lambda g0, g1, kk: ij(g0, g1))

    def body(a_ref, b_ref, *rest):
        o_ref, *acc = rest[0 if after is None else 1:]
        p = _dot(a_ref[...], b_ref[...], form)
        if nk == 1:
            o_ref[...] = p.astype(o_ref.dtype)
        else:
            acc_ref, = acc
            kk = pl.program_id(2)

            @pl.when(kk == 0)
            def _():
                acc_ref[...] = p

            @pl.when(kk > 0)
            def _():
                acc_ref[...] += p

            @pl.when(kk == nk - 1)
            def _():
                o_ref[...] = acc_ref[...].astype(o_ref.dtype)

    extra = [] if after is None else [after]
    return pl.pallas_call(
        body, name=name, grid=(gj, gi, nk) if n_outer else (gi, gj, nk),
        in_specs=[a_spec, b_spec] + [pl.BlockSpec(memory_space=pl.ANY)] * len(extra), out_specs=o_spec,
        out_shape=jax.ShapeDtypeStruct((m, n), out_dtype),
        scratch_shapes=[] if nk == 1 else [pltpu.VMEM((tm, tn), F32)],
        compiler_params=_cparams(("parallel", "parallel", "arbitrary")),
    )(a, b, *extra)


def _rms_in(x, hp0, norm_w, s):
    def body(x_ref, hp0_ref, w_ref, h_ref, xn_ref, r_ref):
        def run(hv):
            r = lax.rsqrt(jnp.mean(hv * hv, axis=-1, keepdims=True) + NORM_EPS)
            h_ref[...] = hv
            xn_ref[...] = (hv * r * w_ref[...]).astype(BF16)
            r_ref[...] = r

        @pl.when(pl.program_id(0) == 0)
        def _():
            run(hp0_ref[...])

        @pl.when(pl.program_id(0) > 0)
        def _():
            run(x_ref[...])

    return pl.pallas_call(
        body, name="rms_in", grid=(s // BLK,),
        in_specs=[pl.BlockSpec((BLK, D), lambda i: (jnp.maximum(i - 1, 0), 0)),
                  pl.BlockSpec((BLK, D), lambda i: (0, 0)), pl.BlockSpec((1, D), lambda i: (0, 0))],
        out_specs=[pl.BlockSpec((BLK, D), lambda i: (i, 0)), pl.BlockSpec((BLK, D), lambda i: (i, 0)),
                   pl.BlockSpec((BLK, 1), lambda i: (i, 0))],
        out_shape=[jax.ShapeDtypeStruct((s, D), F32), jax.ShapeDtypeStruct((s, D), BF16),
                   jax.ShapeDtypeStruct((s, 1), F32)],
        compiler_params=_cparams(("arbitrary",)),
    )(x, hp0, norm_w)


def _rope64(t, c, sa, sb):
    return t * c + pltpu.roll(t, t.shape[1] - ROPE_D // 2, 1) * sa + pltpu.roll(t, ROPE_D // 2, 1) * sb


def _rope128(t, c, sg):
    return t * c + pltpu.roll(t, DH // 2, 1) * sg


HD = 2 * DH


def _mla_prep(proj, qn_w, kvn_w, wq, wukv, qtab, ktab, qbias, kbias, s):
    tm = _row_tile(s, 384)

    def body(cq_ref, ckv_ref, kpe_ref, qnw_ref, kvnw_ref, wq_ref, wukv_ref, qc_ref, qa_ref, qb_ref,
             kc_ref, ka_ref, kb_ref, qbias_ref, kbias_ref, cqn_ref, rq_ref, ckvn_ref, rkv_ref, qq_ref, kk_ref, vv_ref):
        cq = cq_ref[...].astype(F32)
        rq = lax.rsqrt(jnp.mean(cq * cq, axis=-1, keepdims=True) + NORM_EPS)
        cqn = (cq * rq * qnw_ref[...]).astype(BF16)
        ckv = ckv_ref[...].astype(F32)
        rkv = lax.rsqrt(jnp.mean(ckv * ckv, axis=-1, keepdims=True) + NORM_EPS)
        ckvn = (ckv * rkv * kvnw_ref[...]).astype(BF16)
        cqn_ref[...] = cqn
        rq_ref[...] = rq
        ckvn_ref[...] = ckvn
        rkv_ref[...] = rkv
        q = _dot(cqn, wq_ref[...], "nt")
        kv = _dot(ckvn, wukv_ref[...], "nt")
        kp = (_rope64(kpe_ref[...].astype(F32), kc_ref[...], ka_ref[...], kb_ref[...]) + kbias_ref[...]).astype(BF16)
        qc, qa, qb, qbias = qc_ref[...], qa_ref[...], qb_ref[...], qbias_ref[...]
        ones = jnp.ones((tm, DH), BF16)
        for h in range(HEADS):
            lo, mid, hi = h * HD, h * HD + DH, (h + 1) * HD
            qq_ref[:, lo:hi] = (_rope64(q[:, lo:hi], qc, qa, qb) + qbias).astype(BF16)
            kk_ref[:, lo:mid] = kv[:, lo:mid].astype(BF16)
            kk_ref[:, mid:hi] = kp
            vv_ref[:, lo:mid] = kv[:, mid:hi].astype(BF16)
            vv_ref[:, mid:hi] = ones

    row = lambda w, cb: pl.BlockSpec((tm, w), lambda i: (i, cb))
    full = lambda a: pl.BlockSpec(a.shape, lambda i: (0, 0))
    wide = jax.ShapeDtypeStruct((s, HEADS * HD), BF16)
    return pl.pallas_call(
        body, name="mla_prep", grid=(s // tm,),
        in_specs=[row(Q_RANK, C_Q // Q_RANK), row(KV_RANK, C_KV // KV_RANK), row(DH, K_PE // DH),
                  full(qn_w), full(kvn_w), full(wq), full(wukv), row(HD, 0), row(HD, 0), row(HD, 0),
                  row(DH, 0), row(DH, 0), row(DH, 0), full(qbias), row(DH, 0)],
        out_specs=[row(Q_RANK, 0), row(1, 0), row(KV_RANK, 0), row(1, 0), row(HEADS * HD, 0), row(HEADS * HD, 0),
                   row(HEADS * HD, 0)],
        out_shape=[jax.ShapeDtypeStruct((s, Q_RANK), BF16), jax.ShapeDtypeStruct((s, 1), F32),
                   jax.ShapeDtypeStruct((s, KV_RANK), BF16), jax.ShapeDtypeStruct((s, 1), F32), wide, wide, wide],
        compiler_params=_cparams(("parallel",)),
    )(proj, proj, proj, qn_w, kvn_w, wq, wukv, *qtab, *ktab, qbias, kbias)


def _diag_mask(t):
    return lax.broadcasted_iota(jnp.int32, (t, t), 0) <= lax.broadcasted_iota(jnp.int32, (t, t), 1)


def _silu(z):
    return z * _sigmoid(z)


def _attn_fwd(qq, kk, vv, proj, s):
    t = _row_tile(s, 384)
    n = s // t

    def body(q_ref, k_ref, v_ref, z_ref, o_ref, y_ref, lse_ref, acc_ref, m_ref):
        qi = pl.program_id(1)
        q = q_ref[...]
        m_ref[...] = jnp.full(m_ref.shape, NEG_INF, F32)
        acc_ref[...] = jnp.zeros(acc_ref.shape, F32)

        def keys(ki):
            return pl.ds(pl.multiple_of(ki * t, t), t)

        def scores(ki):
            return _dot(k_ref[keys(ki), :], q, "nt")

        def tile(ki, st):
            m_old = m_ref[...]
            m_new = jnp.maximum(m_old, jnp.max(st, axis=0, keepdims=True))
            pt = jnp.exp(st - m_new).astype(BF16)
            acc_ref[...] = jnp.exp(m_old - m_new) * acc_ref[...] + _dot(v_ref[keys(ki), :], pt, "tn")
            m_ref[...] = m_new

        def inner(ki, st):
            st_after = scores(ki + 1)
            tile(ki, st)
            return st_after

        st_diag = lax.fori_loop(0, qi, inner, scores(0))
        tile(qi, jnp.where(_diag_mask(t), st_diag, NEG_INF))

        l = acc_ref[DH:DH + 1, :]
        o = (acc_ref[:DH, :] / l).T
        o_ref[...] = o.astype(BF16)
        y_ref[...] = (o * _silu(z_ref[...].astype(F32))).astype(BF16)
        lse_ref[0, 0] = m_ref[...] + jnp.log(l)

    qtile = pl.BlockSpec((t, DH), lambda h, i: (i, h))
    head = pl.BlockSpec((s, HD), lambda h, i: (0, h))
    return pl.pallas_call(
        body, name="attn_fwd", grid=(HEADS, s // t),
        in_specs=[pl.BlockSpec((t, HD), lambda h, i: (i, h)), head, head,
                  pl.BlockSpec((t, DH), lambda h, i: (i, Z_MLA // DH + h))],
        out_specs=[qtile, qtile, pl.BlockSpec((1, 1, 1, t), lambda h, i: (h, i, 0, 0))],
        out_shape=[jax.ShapeDtypeStruct((s, WIDTH), BF16), jax.ShapeDtypeStruct((s, WIDTH), BF16),
                   jax.ShapeDtypeStruct((HEADS, s // t, 1, t), F32)],
        scratch_shapes=[pltpu.VMEM((HD, t), F32), pltpu.VMEM((1, t), F32)],
        compiler_params=_cparams(("parallel", "arbitrary")),
    )(qq, kk, vv, proj)


def _ret_consts():
    log_g = np.log1p(-(2.0 ** (-5.0 - np.arange(HEADS, dtype=np.float64))))
    n = np.arange(BLK, dtype=np.float64)
    diff = n[:, None] - n[None, :]
    decay = np.where(diff >= 0, np.exp(log_g[:, None, None] * np.maximum(diff, 0.0)), 0.0)
    zeta = np.exp(log_g[:, None] * (BLK - 1.0 - n))[:, :, None]
    xi = np.exp(log_g[:, None] * (n + 1.0))[:, :, None]
    gamma = [float(np.float32(np.exp(g * BLK))) for g in log_g]
    return (jnp.asarray(decay, F32), jnp.asarray(zeta, F32), jnp.asarray(xi, F32)), gamma


def _ret_fwd(proj, gn_w, gn_b, c128, s128, consts, gamma, s):
    nb = s // BLK
    decay, zeta, xi = consts

    def body(rq_ref, rk_ref, rv_ref, z_ref, gw_ref, gb_ref, c_ref, s_ref, dm_ref, zt_ref, xi_ref,
             y_ref, on_ref, rstd_ref, qr_ref, kr_ref, rall_ref, state):
        @pl.when(pl.program_id(0) == 0)
        def _():
            state[...] = jnp.zeros_like(state)

        c, sg = c_ref[...], s_ref[...]
        for h in range(HEADS):
            sl = slice(h * DH, (h + 1) * DH)
            q = _rope128(rq_ref[:, sl].astype(F32), c, sg).astype(BF16)
            kf = _rope128(rk_ref[:, sl].astype(F32), c, sg) * RET_SCALE
            k = kf.astype(BF16)
            v = rv_ref[:, sl]
            qr_ref[:, sl] = q
            kr_ref[:, sl] = k
            r_prev = state[h]
            rall_ref[0, h] = r_prev
            a = _dot(q, k, "nt") * dm_ref[h]
            o = _dot(a.astype(BF16), v, "nn") + _dot(q, r_prev.astype(BF16), "nn") * xi_ref[h]
            state[h] = r_prev * gamma[h] + _dot((kf * zt_ref[h]).astype(BF16), v, "tn")
            mu = jnp.mean(o, axis=-1, keepdims=True)
            var = jnp.mean(jnp.square(o - mu), axis=-1, keepdims=True)
            rstd = lax.rsqrt(var + GN_EPS)
            on = (o - mu) * rstd
            rstd_ref[h] = rstd
            on_ref[:, sl] = on.astype(BF16)
            ogn = on * gw_ref[:, sl] + gb_ref[:, sl]
            y_ref[:, sl] = (ogn * _silu(z_ref[:, sl].astype(F32))).astype(BF16)

    seg = lambda cb: pl.BlockSpec((BLK, WIDTH), lambda i: (i, cb))
    full = lambda a: pl.BlockSpec(a.shape, lambda i: (0,) * a.ndim)
    tab = pl.BlockSpec((BLK, DH), lambda i: (i, 0))
    return pl.pallas_call(
        body, name="ret_fwd", grid=(nb,),
        in_specs=[seg(R_Q // WIDTH), seg(R_K // WIDTH), seg(R_V // WIDTH), seg(Z_RET // WIDTH), full(gn_w), full(gn_b),
                  tab, tab, full(decay), full(zeta), full(xi)],
        out_specs=[seg(0), seg(0), pl.BlockSpec((HEADS, BLK, 1), lambda i: (0, i, 0)), seg(0), seg(0),
                   pl.BlockSpec((1, HEADS, DH, DH), lambda i: (i, 0, 0, 0))],
        out_shape=[jax.ShapeDtypeStruct((s, WIDTH), BF16), jax.ShapeDtypeStruct((s, WIDTH), BF16),
                   jax.ShapeDtypeStruct((HEADS, s, 1), F32), jax.ShapeDtypeStruct((s, WIDTH), BF16),
                   jax.ShapeDtypeStruct((s, WIDTH), BF16), jax.ShapeDtypeStruct((nb, HEADS, DH, DH), F32)],
        scratch_shapes=[pltpu.VMEM((HEADS, DH, DH), F32)],
        compiler_params=_cparams(("arbitrary",)),
    )(proj, proj, proj, proj, gn_w, gn_b, c128, s128, decay, zeta, xi)


def _merge_fwd(y_mla, y_ret, wbm, wbr, proj, s):
    tm, tn = _row_tile(s, 1408), 512

    def body(ym_ref, yr_ref, wm_ref, wr_ref, g0_ref, g1_ref, um_ref, ur_ref, mg_ref):
        um = _dot(ym_ref[...], wm_ref[...], "nt")
        ur = _dot(yr_ref[...], wr_ref[...], "nt")
        um_ref[...] = um.astype(BF16)
        ur_ref[...] = ur.astype(BF16)
        mg_ref[...] = (_sigmoid(g0_ref[...].astype(F32)) * um + _sigmoid(g1_ref[...].astype(F32)) * ur).astype(BF16)

    yspec = pl.BlockSpec((tm, WIDTH), lambda i, j: (i, 0))
    wspec = pl.BlockSpec((tn, WIDTH), lambda i, j: (j, 0))
    ospec = pl.BlockSpec((tm, tn), lambda i, j: (i, j))
    return pl.pallas_call(
        body, name="merge_fwd", grid=(s // tm, D // tn),
        in_specs=[yspec, yspec, wspec, wspec, pl.BlockSpec((tm, tn), lambda i, j: (i, GATE0 // tn + j)),
                  pl.BlockSpec((tm, tn), lambda i, j: (i, GATE1 // tn + j))],
        out_specs=[ospec, ospec, ospec],
        out_shape=[jax.ShapeDtypeStruct((s, D), BF16)] * 3,
        compiler_params=_cparams(("parallel", "parallel")),
    )(y_mla, y_ret, wbm, wbr, proj, proj)


def _out_fwd(merged, big, h, s):
    tm, tn = _row_tile(s, 1408), 512

    def body(m_ref, w_ref, h_ref, o_ref):
        o_ref[...] = h_ref[...] + _dot(m_ref[...], w_ref[...], "nn")

    return pl.pallas_call(
        body, name="out_fwd", grid=(s // tm, D // tn),
        in_specs=[pl.BlockSpec((tm, D), lambda i, j: (i, 0)), pl.BlockSpec((D, tn), lambda i, j: (BIG_OUT // D, j)),
                  pl.BlockSpec((tm, tn), lambda i, j: (i, j))],
        out_specs=pl.BlockSpec((tm, tn), lambda i, j: (i, j)),
        out_shape=jax.ShapeDtypeStruct((s, D), F32),
        compiler_params=_cparams(("parallel", "parallel")),
    )(merged, big, h)


def _loss_bwd(h2, target, fin_w, s):
    nb = s // BLK

    def body(h2_ref, t_ref, w_ref, dh_ref, dhb_ref, loss_ref, dw_ref):
        i = pl.program_id(0)

        @pl.when(i == 0)
        def _():
            dh_ref[...] = jnp.zeros_like(dh_ref)
            dhb_ref[...] = jnp.zeros_like(dhb_ref)
            loss_ref[...] = jnp.zeros_like(loss_ref)
            dw_ref[...] = jnp.zeros_like(dw_ref)

        @pl.when(i > 0)
        def _():
            hv = h2_ref[...]
            w = w_ref[...]
            r = lax.rsqrt(jnp.mean(hv * hv, axis=-1, keepdims=True) + NORM_EPS)
            nrm = hv * r
            e = nrm * w - t_ref[...]
            loss_ref[...] += jnp.full(loss_ref.shape, 0.5 / D, F32) * jnp.sum(e * e)
            dy = e * (1.0 / D)
            dw_ref[...] += jnp.sum(dy * nrm, axis=0, keepdims=True)
            g = dy * w
            dh = r * (g - nrm * jnp.mean(g * nrm, axis=-1, keepdims=True))
            dh_ref[...] = dh
            dhb_ref[...] = dh.astype(BF16)

    blk = pl.BlockSpec((BLK, D), lambda i: (i, 0))
    return pl.pallas_call(
        body, name="loss_bwd", grid=(nb,),
        in_specs=[blk, pl.BlockSpec((BLK, D), lambda i: (jnp.maximum(i - 1, 0), 0)), pl.BlockSpec((1, D), lambda i: (0, 0))],
        out_specs=[blk, blk, pl.BlockSpec((8, 128), lambda i: (0, 0)), pl.BlockSpec((1, D), lambda i: (0, 0))],
        out_shape=[jax.ShapeDtypeStruct((s, D), F32), jax.ShapeDtypeStruct((s, D), BF16),
                   jax.ShapeDtypeStruct((8, 128), F32), jax.ShapeDtypeStruct((1, D), F32)],
        compiler_params=_cparams(("arbitrary",)),
    )(h2, target, fin_w)


def _merge_bwd(dh2b, big, u_mla, u_ret, proj, s):
    tm, tn = _row_tile(s, 1408), 512

    def body(d_ref, w_ref, um_ref, ur_ref, gate_ref, dproj_ref, dum_ref, dur_ref, dm_ref):
        branch = pl.program_id(2)

        @pl.when(branch == 0)
        def _():
            dm_ref[...] = _dot(d_ref[...], w_ref[...], "nt")

        dm = dm_ref[...]
        gt = _sigmoid(gate_ref[...].astype(F32))

        @pl.when(branch == 0)
        def _():
            dproj_ref[...] = (dm * um_ref[...].astype(F32) * gt * (1.0 - gt)).astype(BF16)
            dum_ref[...] = (dm * gt).astype(BF16)

        @pl.when(branch == 1)
        def _():
            dproj_ref[...] = (dm * ur_ref[...].astype(F32) * gt * (1.0 - gt)).astype(BF16)
            dur_ref[...] = (dm * gt).astype(BF16)

    ospec = pl.BlockSpec((tm, tn), lambda i, j, b: (i, j))
    gate = pl.BlockSpec((tm, tn), lambda i, j, b: (i, GATE0 // tn + b * (D // tn) + j))
    return pl.pallas_call(
        body, name="merge_bwd", grid=(s // tm, D // tn, 2),
        in_specs=[pl.BlockSpec((tm, D), lambda i, j, b: (i, 0)),
                  pl.BlockSpec((tn, D), lambda i, j, b: (BIG_OUT // tn + j, 0)), ospec, ospec, gate],
        out_specs=[gate, ospec, ospec],
        out_shape=[jax.ShapeDtypeStruct((s, PROJ_W), BF16), jax.ShapeDtypeStruct((s, D), BF16),
                   jax.ShapeDtypeStruct((s, D), BF16)],
        scratch_shapes=[pltpu.VMEM((tm, tn), F32)],
        compiler_params=_cparams(("parallel", "parallel", "arbitrary")),
    )(dh2b, big, u_mla, u_ret, proj)


def _dsilu(z):
    sg = _sigmoid(z)
    return sg * (1.0 + z * (1.0 - sg))


def _branch_mla_bwd(dproj, du, wbm, o_mla, proj, s):
    tm = _row_tile(s, 384)

    def body(dproj_in, du_ref, w_ref, o_ref, z_ref, dz_ref, do_ref, delta_ref):
        del dproj_in
        dy = _dot(du_ref[...], w_ref[...], "nn")
        z = z_ref[...].astype(F32)
        o = o_ref[...].astype(F32)
        do = dy * _silu(z)
        do_ref[...] = do.astype(BF16)
        dz_ref[...] = (dy * o * _dsilu(z)).astype(BF16)
        prod = do * o
        for h in range(HEADS):
            delta_ref[h] = jnp.sum(prod[:, h * DH:(h + 1) * DH], axis=-1, keepdims=True)

    row = lambda w, cb: pl.BlockSpec((tm, w), lambda i: (i, cb))
    return pl.pallas_call(
        body, name="branch_mla_bwd", grid=(s // tm,),
        in_specs=[ANY, row(D, 0), pl.BlockSpec((D, WIDTH), lambda i: (0, 0)), row(WIDTH, 0),
                  row(WIDTH, Z_MLA // WIDTH)],
        out_specs=[row(WIDTH, Z_MLA // WIDTH), row(WIDTH, 0), pl.BlockSpec((HEADS, tm, 1), lambda i: (0, i, 0))],
        out_shape=[jax.ShapeDtypeStruct((s, PROJ_W), BF16), jax.ShapeDtypeStruct((s, WIDTH), BF16),
                   jax.ShapeDtypeStruct((HEADS, s, 1), F32)],
        input_output_aliases={0: 0},
        compiler_params=_cparams(("parallel",)),
    )(dproj, du, wbm, o_mla, proj)


def _branch_ret_bwd(dproj, du, wbr, on, rstd, gn_w, gn_b, proj, s):
    tm = BLK

    def body(dproj_in, du_ref, w_ref, on_ref, rstd_ref, gw_ref, gb_ref, z_ref, dz_ref, do_ref, dgw_ref, dgb_ref):
        del dproj_in

        @pl.when(pl.program_id(0) == 0)
        def _():
            dgw_ref[...] = jnp.zeros_like(dgw_ref)
            dgb_ref[...] = jnp.zeros_like(dgb_ref)

        dy = _dot(du_ref[...], w_ref[...], "nn")
        z = z_ref[...].astype(F32)
        on = on_ref[...].astype(F32)
        gw = gw_ref[...]
        dogn = dy * _silu(z)
        dz_ref[...] = (dy * (on * gw + gb_ref[...]) * _dsilu(z)).astype(BF16)
        dgw_ref[...] += jnp.sum(dogn * on, axis=0, keepdims=True)
        dgb_ref[...] += jnp.sum(dogn, axis=0, keepdims=True)
        don = dogn * gw
        for h in range(HEADS):
            sl = slice(h * DH, (h + 1) * DH)
            dn, nh = don[:, sl], on[:, sl]
            do = rstd_ref[h] * (dn - jnp.mean(dn, axis=-1, keepdims=True)
                                - nh * jnp.mean(dn * nh, axis=-1, keepdims=True))
            do_ref[:, sl] = do.astype(BF16)

    row = lambda w, cb: pl.BlockSpec((tm, w), lambda i: (i, cb))
    vec = pl.BlockSpec((1, WIDTH), lambda i: (0, 0))
    return pl.pallas_call(
        body, name="branch_ret_bwd", grid=(s // tm,),
        in_specs=[ANY, row(D, 0), pl.BlockSpec((D, WIDTH), lambda i: (0, 0)), row(WIDTH, 0),
                  pl.BlockSpec((HEADS, tm, 1), lambda i: (0, i, 0)), vec, vec, row(WIDTH, Z_RET // WIDTH)],
        out_specs=[row(WIDTH, Z_RET // WIDTH), row(WIDTH, 0), vec, vec],
        out_shape=[jax.ShapeDtypeStruct((s, PROJ_W), BF16), jax.ShapeDtypeStruct((s, WIDTH), BF16)]
        + [jax.ShapeDtypeStruct((1, WIDTH), F32)] * 2,
        input_output_aliases={0: 0},
        compiler_params=_cparams(("arbitrary",)),
    )(dproj, du, wbr, on, rstd, gn_w, gn_b, proj)


def _ret_bwd(dproj, qr, kr, proj, rall, do_ret, c128, s128, consts, gamma, s):
    nb = s // BLK
    decay, zeta, xi = consts

    def body(dproj_in, q_ref, k_ref, v_ref, r_ref, do_ref, c_ref, s_ref, dm_ref, zt_ref, xi_ref, out_ref, gstate):
        del dproj_in
        dq_ref, dk_ref, dv_ref = (out_ref.at[:, pl.ds(off, WIDTH)] for off in (R_Q, R_K, R_V))

        @pl.when(pl.program_id(0) == 0)
        def _():
            gstate[...] = jnp.zeros_like(gstate)

        c, sg = c_ref[...], s_ref[...]
        for h in range(HEADS):
            sl = slice(h * DH, (h + 1) * DH)
            q, k, v, do = q_ref[:, sl], k_ref[:, sl], v_ref[:, sl], do_ref[:, sl]
            dm = dm_ref[h]
            g_next = gstate[h]
            gb = g_next.astype(BF16)
            a = (_dot(q, k, "nt") * dm).astype(BF16)
            da = (_dot(do, v, "nt") * dm).astype(BF16)
            dox = (do.astype(F32) * xi_ref[h]).astype(BF16)
            dq = _dot(da, k, "nn") + _dot(dox, r_ref[0, h].astype(BF16), "nt")
            dk = _dot(da, q, "tn") + _dot(v, gb, "nt") * zt_ref[h]
            kz = (k.astype(F32) * zt_ref[h]).astype(BF16)
            dv = _dot(a, do, "tn") + _dot(kz, gb, "nn")
            gstate[h] = g_next * gamma[h] + _dot(q, dox, "tn")
            dk = dk * RET_SCALE
            dq_ref[:, sl] = _rope128(dq, c, -sg).astype(BF16)
            dk_ref[:, sl] = _rope128(dk, c, -sg).astype(BF16)
            dv_ref[:, sl] = dv.astype(BF16)

    rev = lambda cb: pl.BlockSpec((BLK, WIDTH), lambda i: (nb - 1 - i, cb))
    full = lambda a: pl.BlockSpec(a.shape, lambda i: (0,) * a.ndim)
    tab = pl.BlockSpec((BLK, DH), lambda i: (nb - 1 - i, 0))
    return pl.pallas_call(
        body, name="ret_bwd", grid=(nb,),
        in_specs=[ANY, rev(0), rev(0), rev(R_V // WIDTH),
                  pl.BlockSpec((1, HEADS, DH, DH), lambda i: (nb - 1 - i, 0, 0, 0)),
                  rev(0), tab, tab, full(decay), full(zeta), full(xi)],
        out_specs=pl.BlockSpec((BLK, 3 * WIDTH), lambda i: (nb - 1 - i, R_Q // (3 * WIDTH))),
        out_shape=jax.ShapeDtypeStruct((s, PROJ_W), BF16),
        scratch_shapes=[pltpu.VMEM((HEADS, DH, DH), F32)],
        input_output_aliases={0: 0},
        compiler_params=_cparams(("arbitrary",)),
    )(dproj, qr, kr, proj, rall, do_ret, c128, s128, decay, zeta, xi)


def _attn_bwd(qq, kk, vv, do, lse, delta, s):
    t = _row_tile(s, 384)
    n = s // t

    def body(q_ref, k_ref, v_ref, do_ref, lse_ref, delta_ref, dq_ref, dk_ref, dv_ref, dk_acc, dv_acc):
        ki = pl.program_id(1)

        @pl.when(ki == 0)
        def _():
            dq_ref[...] = jnp.zeros(dq_ref.shape, F32)

        k, v = k_ref[...], v_ref[...]
        dk_acc[...] = jnp.zeros(dk_acc.shape, F32)
        dv_acc[...] = jnp.zeros(dv_acc.shape, F32)

        def rows(qi):
            return pl.ds(pl.multiple_of(qi * t, t), t)

        def products(qi):
            return _dot(k, q_ref[rows(qi), :], "nt"), _dot(v, do_ref[rows(qi), :], "nt")

        def tile(qi, st, dpt):
            q, dov = q_ref[rows(qi), :], do_ref[rows(qi), :]
            pt = jnp.exp(st - lse_ref[0, qi])
            dv_acc[...] += _dot(pt.astype(BF16), dov, "nn")
            dst = (pt * (dpt - delta_ref[0, qi])).astype(BF16)
            dk_acc[...] += _dot(dst, q, "nn")
            dq_ref[rows(qi), :] += _dot(dst, k, "tn")

        def inner(qi, carry):
            after = products(jnp.minimum(qi + 1, n - 1))
            tile(qi, *carry)
            return after

        st, dpt = products(ki)
        ahead = products(jnp.minimum(ki + 1, n - 1))
        tile(ki, jnp.where(_diag_mask(t), st, NEG_INF), dpt)
        lax.fori_loop(ki + 1, n, inner, ahead)

        dk_ref[...] = dk_acc[...].astype(BF16)
        dv_ref[...] = dv_acc[...].astype(BF16)

    head = lambda w: pl.BlockSpec((s, w), lambda h, i: (0, h))
    stat = pl.BlockSpec((1, n, 1, t), lambda h, i: (h, 0, 0, 0))
    return pl.pallas_call(
        body, name="attn_bwd", grid=(HEADS, n),
        in_specs=[head(HD), pl.BlockSpec((t, HD), lambda h, i: (i, h)), pl.BlockSpec((t, DH), lambda h, i: (i, 2 * h)),
                  head(DH), stat, stat],
        out_specs=[head(HD), pl.BlockSpec((t, HD), lambda h, i: (i, h)), pl.BlockSpec((t, DH), lambda h, i: (i, h))],
        out_shape=[jax.ShapeDtypeStruct((s, HEADS * HD), F32), jax.ShapeDtypeStruct((s, HEADS * HD), BF16),
                   jax.ShapeDtypeStruct((s, WIDTH), BF16)],
        scratch_shapes=[pltpu.VMEM((t, HD), F32), pltpu.VMEM((t, DH), F32)],
        compiler_params=_cparams(("parallel", "arbitrary")),
    )(qq, kk, vv, do, lse, delta)


def _rms_bwd(dn, nhat, r, w):
    g = dn * w
    return r * (g - nhat * jnp.mean(g * nhat, axis=-1, keepdims=True)), dn * nhat


def _mla_prep_bwd(dproj, dqq, dkk, dvv, proj, rq, rkv, qn_w, kvn_w, wq, wukv, qtab, ktab, s):
    tm = _row_tile(s, 384)
    tail = PROJ_W - C_Q

    def body(dproj_in, dqq_ref, dkk_ref, dvv_ref, cq_ref, ckv_ref, rq_ref, rkv_ref, qnw_ref, kvnw_ref, wq_ref,
             wukv_ref, qc_ref, qa_ref, qb_ref, kc_ref, ka_ref, kb_ref,
             tail_ref, dq_ref, dkv_ref, dqnw_ref, dkvnw_ref):
        del dproj_in
        dcq_ref = tail_ref.at[:, pl.ds(0, Q_RANK)]
        dckv_ref = tail_ref.at[:, pl.ds(C_KV - C_Q, KV_RANK)]
        dkpe_ref = tail_ref.at[:, pl.ds(K_PE - C_Q, 2 * DH)]

        @pl.when(pl.program_id(0) == 0)
        def _():
            dqnw_ref[...] = jnp.zeros_like(dqnw_ref)
            dkvnw_ref[...] = jnp.zeros_like(dkvnw_ref)

        qc, qa, qb = qc_ref[...], qa_ref[...], qb_ref[...]
        dkp = jnp.zeros((tm, DH), F32)
        for h in range(HEADS):
            lo, mid, hi = h * HD, h * HD + DH, (h + 1) * HD
            dq_ref[:, lo:hi] = _rope64(dqq_ref[:, lo:hi], qc, -qa, -qb).astype(BF16)
            dkv_ref[:, lo:mid] = dkk_ref[:, lo:mid]
            dkv_ref[:, mid:hi] = dvv_ref[:, h * DH:(h + 1) * DH]
            dkp = dkp + dkk_ref[:, mid:hi].astype(F32)
        dcqn = _dot(dq_ref[...], wq_ref[...], "nn")
        rq_v = rq_ref[...]
        dcq, prod = _rms_bwd(dcqn, cq_ref[...].astype(F32) * rq_v, rq_v, qnw_ref[...])
        dcq_ref[...] = dcq.astype(BF16)
        dqnw_ref[...] += jnp.sum(prod, axis=0, keepdims=True)
        dckvn = _dot(dkv_ref[...], wukv_ref[...], "nn")
        rkv_v = rkv_ref[...]
        dckv, prod = _rms_bwd(dckvn, ckv_ref[...].astype(F32) * rkv_v, rkv_v, kvnw_ref[...])
        dckv_ref[...] = dckv.astype(BF16)
        dkvnw_ref[...] += jnp.sum(prod, axis=0, keepdims=True)
        dkpe_ref[:, :DH] = _rope64(dkp, kc_ref[...], -ka_ref[...], -kb_ref[...]).astype(BF16)
        dkpe_ref[:, DH:] = jnp.zeros((tm, DH), BF16)

    row = lambda w, cb: pl.BlockSpec((tm, w), lambda i: (i, cb))
    full = lambda a: pl.BlockSpec(a.shape, lambda i: (0, 0))
    wide = jax.ShapeDtypeStruct((s, HEADS * HD), BF16)
    return pl.pallas_call(
        body, name="mla_prep_bwd", grid=(s // tm,),
        in_specs=[ANY, row(HEADS * HD, 0), row(HEADS * HD, 0), row(WIDTH, 0),
                  row(Q_RANK, C_Q // Q_RANK), row(KV_RANK, C_KV // KV_RANK), row(1, 0), row(1, 0),
                  full(qn_w), full(kvn_w), full(wq), full(wukv), row(HD, 0), row(HD, 0), row(HD, 0),
                  row(DH, 0), row(DH, 0), row(DH, 0)],
        out_specs=[row(tail, C_Q // tail), row(HEADS * HD, 0), row(HEADS * HD, 0),
                   pl.BlockSpec((1, Q_RANK), lambda i: (0, 0)), pl.BlockSpec((1, KV_RANK), lambda i: (0, 0))],
        out_shape=[jax.ShapeDtypeStruct((s, PROJ_W), BF16), wide, wide,
                   jax.ShapeDtypeStruct((1, Q_RANK), F32), jax.ShapeDtypeStruct((1, KV_RANK), F32)],
        input_output_aliases={0: 0},
        compiler_params=_cparams(("arbitrary",)),
    )(dproj, dqq, dkk, dvv, proj, proj, rq, rkv, qn_w, kvn_w, wq, wukv, *qtab, *ktab)


def _rms_in_bwd(dxn, h, r, dh2, norm_w, s):
    def body(dxn_ref, h_ref, r_ref, dh2_ref, w_ref, gx_ref, gm_ref, dw_ref):
        i = pl.program_id(0)
        r_v = r_ref[...]
        dx, prod = _rms_bwd(dxn_ref[...], h_ref[...] * r_v, r_v, w_ref[...])
        dh = dh2_ref[...] + dx

        @pl.when(i == 0)
        def _():
            gm_ref[...] = dh
            dw_ref[...] = jnp.sum(prod, axis=0, keepdims=True)

        @pl.when(i > 0)
        def _():
            gx_ref[...] = dh
            dw_ref[...] += jnp.sum(prod, axis=0, keepdims=True)

    blk = pl.BlockSpec((BLK, D), lambda i: (i, 0))
    return pl.pallas_call(
        body, name="rms_in_bwd", grid=(s // BLK,),
        in_specs=[blk, blk, pl.BlockSpec((BLK, 1), lambda i: (i, 0)), blk, pl.BlockSpec((1, D), lambda i: (0, 0))],
        out_specs=[pl.BlockSpec((BLK, D), lambda i: (jnp.maximum(i - 1, 0), 0)), pl.BlockSpec((BLK, D), lambda i: (0, 0)),
                   pl.BlockSpec((1, D), lambda i: (0, 0))],
        out_shape=[jax.ShapeDtypeStruct((s - BLK, D), F32), jax.ShapeDtypeStruct((BLK, D), F32),
                   jax.ShapeDtypeStruct((1, D), F32)],
        compiler_params=_cparams(("arbitrary",)),
    )(dxn, h, r, dh2, norm_w)


def _adam_math(w, g, m, v):
    m = ADAM_B1 * m + (1.0 - ADAM_B1) * g
    v = ADAM_B2 * v + (1.0 - ADAM_B2) * (g * g)
    m_hat = m / (1.0 - ADAM_B1 ** ADAM_STEP)
    v_hat = v / (1.0 - ADAM_B2 ** ADAM_STEP)
    return -ADAM_LR * (m_hat / (jnp.sqrt(v_hat) + ADAM_EPS) + ADAM_WD * w), m, v


def _adamw(w, g, m, v, name):
    rows, cols = w.shape
    tr = rows
    for cand in (128, 64, 32, 16, 8):
        if rows % cand == 0:
            tr = cand
            break

    def body(w_ref, g_ref, m_ref, v_ref, d_ref, nm_ref, nv_ref):
        d_ref[...], nm_ref[...], nv_ref[...] = _adam_math(w_ref[...], g_ref[...], m_ref[...], v_ref[...])

    spec = pl.BlockSpec((tr, cols), lambda i: (i, 0))
    return pl.pallas_call(
        body, name=name, grid=(rows // tr,), in_specs=[spec] * 4, out_specs=[spec] * 3,
        out_shape=[jax.ShapeDtypeStruct((rows, cols), F32)] * 3,
        compiler_params=_cparams(("parallel",)),
    )(w, g, m, v)


def _adamw_t(w, g_t, m, v, name):
    rows, cols = w.shape

    def body(w_ref, g_ref, m_ref, v_ref, go_ref, d_ref, nm_ref, nv_ref):
        g = g_ref[...].T
        go_ref[...] = g
        d_ref[...], nm_ref[...], nv_ref[...] = _adam_math(w_ref[...], g, m_ref[...], v_ref[...])

    nat = pl.BlockSpec((rows, 128), lambda k: (0, k))
    return pl.pallas_call(
        body, name=name, grid=(pl.cdiv(cols, 128),),
        in_specs=[nat, pl.BlockSpec((128, rows), lambda k: (k, 0)), nat, nat], out_specs=[nat] * 4,
        out_shape=[jax.ShapeDtypeStruct((rows, cols), F32)] * 4, compiler_params=_cparams(("parallel",)),
    )(w, g_t, m, v)


def _adamw_small(w, gall, m, v):
    def body(w_ref, g_ref, m_ref, v_ref, gs_ref, d_ref, nm_ref, nv_ref):
        g = g_ref[0]
        for dev in range(1, 8):
            g = g + g_ref[dev]
        gs_ref[...] = g
        d_ref[...], nm_ref[...], nv_ref[...] = _adam_math(w_ref[...], g, m_ref[...], v_ref[...])

    return pl.pallas_call(
        body, name="adamw_small", out_shape=[jax.ShapeDtypeStruct((SM_ROWS, 128), F32)] * 4,
        compiler_params=pltpu.CompilerParams(vmem_limit_bytes=VMEM_LIMIT),
    )(w, gall, m, v)


ADD_ROWS = 464
HALF_BLOCKS = PK_HALF // ADD_ROWS


def _rs_core_add(srcs, land, pos):
    def body(pos_ref, *refs):
        src_refs, (land_ref, out_ref, own, sem) = refs[:len(srcs)], refs[len(srcs):]
        for s in range(N_CHIPS):
            for hf in range(2):
                @pl.when((pl.program_id(0) == s) & (pos_ref[1] == hf))
                def _():
                    for k, r0, p, n in _grad_half(s, hf):
                        pltpu.make_async_copy(src_refs[k].at[pl.ds(r0, n), :], own.at[pl.ds(p, n), :], sem).start()
        pltpu.make_async_copy(own, own, sem).wait()
        out_ref[0] = (own[...].astype(F32) + land_ref[0].astype(F32)).astype(BF16)

    blk = pl.BlockSpec((1, PK_HALF, D), lambda s, pos_ref: (s, 0, 0))
    return pl.pallas_call(
        body, name="rs_core_add",
        grid_spec=pltpu.PrefetchScalarGridSpec(
            num_scalar_prefetch=1, grid=(N_CHIPS,), in_specs=[ANY] * len(srcs) + [blk], out_specs=blk,
            scratch_shapes=[pltpu.VMEM((PK_HALF, D), BF16), pltpu.SemaphoreType.DMA(())]),
        out_shape=jax.ShapeDtypeStruct((N_CHIPS, PK_HALF, D), BF16),
        compiler_params=_cparams(("arbitrary",)),
    )(pos, *srcs, land)


def _rs_chip_add(cp, land, pos):
    def body(pos_ref, a_ref, l_ref, o_ref):
        o_ref[0] = ((a_ref[0].astype(F32) + l_ref[0].astype(F32)) + l_ref[1].astype(F32)) + l_ref[2].astype(F32)

    return pl.pallas_call(
        body, name="rs_chip_add",
        grid_spec=pltpu.PrefetchScalarGridSpec(
            num_scalar_prefetch=1, grid=(HALF_BLOCKS,),
            in_specs=[pl.BlockSpec((1, ADD_ROWS, D), lambda i, pos_ref: (pos_ref[0], i, 0)),
                      pl.BlockSpec((3, ADD_ROWS, D), lambda i, pos_ref: (0, i, 0))],
            out_specs=pl.BlockSpec((1, ADD_ROWS, D), lambda i, pos_ref: (pos_ref[1], i, 0))),
        out_shape=jax.ShapeDtypeStruct((2, PK_HALF, D), F32),
        compiler_params=_cparams(("parallel",)),
    )(pos, cp, land)


ANY = pl.BlockSpec(memory_space=pl.ANY)


def _mesh_pos():
    return lax.axis_index("x"), lax.axis_index("y"), lax.axis_index("c")


def _chip_peer(x, y, c, r):
    return (jnp.bitwise_xor(x, r >> 1), jnp.bitwise_xor(y, r & 1), c)


def _allgather_w_in(big0, meta_loc):
    def body(big0_ref, meta_ref, big_ref, metaf_ref, lsem, ssem, rsem, fssem, frsem, msem_s, msem_r):
        del big0_ref
        x, y, c = _mesh_pos()
        j = 2 * x + y
        me, sibling = (x, y, c), (x, y, 1 - c)

        def half_wait(s_sem, r_sem):
            rows = big_ref.at[pl.ds(0, AG_IN_HALF), :]
            return pltpu.make_async_remote_copy(src_ref=rows, dst_ref=rows, send_sem=s_sem, recv_sem=r_sem,
                                                device_id=me, device_id_type=MESH)

        own_meta = pltpu.make_async_copy(meta_ref, metaf_ref.at[j], lsem)
        own_meta.start()
        for s in range(N_CHIPS):
            for hf in range(2):
                @pl.when((j == s) & (c == hf))
                def _():
                    for r in (1, 2, 3):
                        for b, n in _ag_half(s, True, hf):
                            rows = big_ref.at[pl.ds(b, n), :]
                            pltpu.make_async_remote_copy(
                                src_ref=rows, dst_ref=rows, send_sem=ssem.at[r - 1], recv_sem=rsem.at[r - 1],
                                device_id=_chip_peer(x, y, c, r), device_id_type=MESH).start()

        meta_copies = [pltpu.make_async_remote_copy(
            src_ref=meta_ref, dst_ref=metaf_ref.at[j], send_sem=msem_s.at[r - 1], recv_sem=msem_r.at[r - 1],
            device_id=_chip_peer(x, y, c, r), device_id_type=MESH) for r in (1, 2, 3)]
        for cp in meta_copies:
            cp.start()
        for r in (1, 2, 3):
            half_wait(ssem.at[r - 1], rsem.at[r - 1]).wait_recv()
            src_shard = jnp.bitwise_xor(j, r)
            for s in range(N_CHIPS):
                for hf in range(2):
                    @pl.when((src_shard == s) & (c == hf))
                    def _():
                        for b, n in _ag_half(s, True, hf):
                            rows = big_ref.at[pl.ds(b, n), :]
                            pltpu.make_async_remote_copy(
                                src_ref=rows, dst_ref=rows, send_sem=fssem.at[r - 1], recv_sem=frsem.at[r - 1],
                                device_id=sibling, device_id_type=MESH).start()
        for r in (1, 2, 3):
            half_wait(fssem.at[r - 1], frsem.at[r - 1]).wait_recv()
        for r in (1, 2, 3):
            half_wait(ssem.at[r - 1], rsem.at[r - 1]).wait_send()
            half_wait(fssem.at[r - 1], frsem.at[r - 1]).wait_send()
        for cp in meta_copies:
            cp.wait_send()
            cp.wait_recv()
        own_meta.wait()

    dma3 = pltpu.SemaphoreType.DMA((3,))
    return pl.pallas_call(
        body, name="allgather_w_in", in_specs=[ANY, ANY], out_specs=[ANY, ANY],
        out_shape=[jax.ShapeDtypeStruct((BIG_ROWS, D), BF16), jax.ShapeDtypeStruct((N_CHIPS,) + meta_loc.shape, F32)],
        scratch_shapes=[pltpu.SemaphoreType.DMA(()), dma3, dma3, dma3, dma3, dma3, dma3],
        input_output_aliases={0: 0},
        compiler_params=pltpu.CompilerParams(has_side_effects=True),
    )(big0, meta_loc)


HBM = pl.BlockSpec(memory_space=pltpu.HBM)
SEM = pl.BlockSpec(memory_space=pltpu.SEMAPHORE)
EFFECT = pltpu.SideEffectType.DATAFLOW_SIDE_EFFECTING
REST_PEERS = [(r, dc) for r in (1, 2, 3) for dc in (0, 1)]


def _allgather_rest_start(big):
    def body(big_ref, *rest):
        s_sems, r_sems = rest[:6], rest[6:12]
        x, y, c = _mesh_pos()
        j = 2 * x + y
        for s in range(N_CHIPS):
            for hf in range(2):
                @pl.when((j == s) & (c == hf))
                def _():
                    for k, (r, dc) in enumerate(REST_PEERS):
                        for b, n in _ag_half(s, False, hf):
                            rows = big_ref.at[pl.ds(b, n), :]
                            pltpu.make_async_remote_copy(
                                src_ref=rows, dst_ref=rows, send_sem=s_sems[k], recv_sem=r_sems[2 * (r - 1) + hf],
                                device_id=_chip_peer(x, y, dc, r), device_id_type=MESH).start()

    sem = pltpu.SemaphoreType.DMA(())
    out = pl.pallas_call(
        body, name="allgather_rest_start", out_shape=(sem,) * 12 + (pltpu.HBM(big.shape, big.dtype),),
        in_specs=(HBM,), out_specs=(SEM,) * 12 + (HBM,), input_output_aliases={0: 12},
        compiler_params=pltpu.CompilerParams(has_side_effects=EFFECT),
    )(pltpu.with_memory_space_constraint(big, pltpu.HBM))
    return out[:12], out[12]


def _allgather_rest_wait(sems, big, after):
    def body(big_ref, *rest):
        s_sems, r_sems = rest[:6], rest[6:12]
        me = _mesh_pos()
        rows = big_ref.at[pl.ds(0, AG_REST_HALF), :]
        for k in range(6):
            copy = pltpu.make_async_remote_copy(src_ref=rows, dst_ref=rows, send_sem=s_sems[k], recv_sem=r_sems[k],
                                                device_id=me, device_id_type=MESH)
            copy.wait_send()
            copy.wait_recv()

    return pl.pallas_call(
        body, name="allgather_rest_wait", out_shape=pltpu.HBM(big.shape, big.dtype),
        in_specs=(HBM,) + (SEM,) * 12 + (pl.BlockSpec(memory_space=pl.ANY),), out_specs=HBM,
        input_output_aliases={0: 0}, compiler_params=pltpu.CompilerParams(has_side_effects=EFFECT),
    )(big, *sems, after)


RS_SPLIT = 2


def _rs_core_exchange(srcs):
    def body(*refs):
        src_refs, (land_ref, ssem, rsem) = refs[:len(srcs)], refs[len(srcs):]
        x, y, c = _mesh_pos()
        me, sibling = (x, y, c), (x, y, 1 - c)
        for hf in range(2):
            @pl.when(c == hf)
            def _():
                for s in range(N_CHIPS):
                    for k, r0, p, n in _grad_half(s, 1 - hf):
                        pltpu.make_async_remote_copy(
                            src_ref=src_refs[k].at[pl.ds(r0, n), :], dst_ref=land_ref.at[s, pl.ds(p, n), :],
                            send_sem=ssem, recv_sem=rsem, device_id=sibling, device_id_type=MESH).start()
        whole = pltpu.make_async_remote_copy(src_ref=land_ref, dst_ref=land_ref, send_sem=ssem, recv_sem=rsem,
                                             device_id=me, device_id_type=MESH)
        whole.wait_recv()
        whole.wait_send()

    return pl.pallas_call(
        body, name="rs_core_exchange", in_specs=[ANY] * len(srcs), out_specs=ANY,
        out_shape=jax.ShapeDtypeStruct((N_CHIPS, PK_HALF, D), BF16),
        scratch_shapes=[pltpu.SemaphoreType.DMA(())] * 2,
        compiler_params=pltpu.CompilerParams(has_side_effects=True),
    )(*srcs)


def _rs_chip_start(cp):
    def body(cp_ref, land_ref, s1, s2, s3, r1, r2, r3, cp_thru, land_thru, token):
        x, y, c = _mesh_pos()
        j = 2 * x + y
        for r, s_sem, r_sem in zip((1, 2, 3), (s1, s2, s3), (r1, r2, r3)):
            pltpu.make_async_remote_copy(
                src_ref=cp_ref.at[jnp.bitwise_xor(j, r)], dst_ref=land_ref.at[r - 1], send_sem=s_sem, recv_sem=r_sem,
                device_id=_chip_peer(x, y, c, r), device_id_type=MESH).start()
        token[...] = jnp.zeros_like(token)

    land_shape = (3, PK_HALF, D)
    sem = pltpu.SemaphoreType.DMA(())
    out = pl.pallas_call(
        body, name="rs_chip_start",
        out_shape=(sem,) * 6 + (pltpu.HBM(cp.shape, cp.dtype), pltpu.HBM(land_shape, BF16),
                                jax.ShapeDtypeStruct((8, 128), F32)),
        in_specs=(HBM, HBM), out_specs=(SEM,) * 6 + (HBM, HBM, pl.BlockSpec(memory_space=pltpu.VMEM)),
        input_output_aliases={0: 6, 1: 7},
        compiler_params=pltpu.CompilerParams(has_side_effects=EFFECT),
    )(pltpu.with_memory_space_constraint(cp, pltpu.HBM),
      pltpu.with_memory_space_constraint(lax.empty(land_shape, BF16), pltpu.HBM))
    return out[:6], out[6], out[7], out[8]


def _rs_chip_wait(sems, cp_thru, land_thru, after):
    def body(cp_ref, land_ref, s1, s2, s3, r1, r2, r3, after_ref, cp_out, land_out):
        me = _mesh_pos()
        for r, s_sem, r_sem in zip((1, 2, 3), (s1, s2, s3), (r1, r2, r3)):
            copy = pltpu.make_async_remote_copy(src_ref=cp_ref.at[0], dst_ref=land_ref.at[r - 1], send_sem=s_sem,
                                                recv_sem=r_sem, device_id=me, device_id_type=MESH)
            copy.wait_send()
            copy.wait_recv()

    return pl.pallas_call(
        body, name="rs_chip_wait",
        out_shape=(pltpu.HBM(cp_thru.shape, cp_thru.dtype), pltpu.HBM(land_thru.shape, land_thru.dtype)),
        in_specs=(HBM, HBM) + (SEM,) * 6 + (pl.BlockSpec(memory_space=pl.ANY),), out_specs=(HBM, HBM),
        input_output_aliases={0: 0, 1: 1},
        compiler_params=pltpu.CompilerParams(has_side_effects=EFFECT),
    )(cp_thru, land_thru, *sems, after)


def _rs_finish(full, small):
    n = PK_HALF // RS_SPLIT

    def body(full_in_ref, sm_ref, full_ref, all_ref, lsem, ssem, rsem, sm_s, sm_r):
        del full_in_ref
        x, y, c = _mesh_pos()
        me, sibling = (x, y, c), (x, y, 1 - c)
        my_id = 4 * x + 2 * y + c
        own_sm = pltpu.make_async_copy(sm_ref, all_ref.at[my_id], lsem)
        own_sm.start()
        for q in range(RS_SPLIT):
            rows = full_ref.at[c, pl.ds(q * n, n), :]
            pltpu.make_async_remote_copy(src_ref=rows, dst_ref=rows, send_sem=ssem, recv_sem=rsem,
                                         device_id=sibling, device_id_type=MESH).start()
        half = pltpu.make_async_remote_copy(src_ref=full_ref.at[c], dst_ref=full_ref.at[c], send_sem=ssem,
                                            recv_sem=rsem, device_id=me, device_id_type=MESH)
        smalls = [pltpu.make_async_remote_copy(
            src_ref=sm_ref, dst_ref=all_ref.at[my_id], send_sem=sm_s.at[r - 1], recv_sem=sm_r.at[r - 1],
            device_id=(jnp.bitwise_xor(x, r >> 2), jnp.bitwise_xor(y, (r >> 1) & 1), jnp.bitwise_xor(c, r & 1)),
            device_id_type=MESH) for r in range(1, 8)]
        for cpy in smalls:
            cpy.start()
        half.wait_recv()
        half.wait_send()
        for cpy in smalls:
            cpy.wait_recv()
        for cpy in smalls:
            cpy.wait_send()
        own_sm.wait()

    dma7 = pltpu.SemaphoreType.DMA((7,))
    return pl.pallas_call(
        body, name="rs_finish", in_specs=[ANY, ANY], out_specs=[ANY, ANY],
        out_shape=[jax.ShapeDtypeStruct((2, PK_HALF, D), F32), jax.ShapeDtypeStruct((8, SM_ROWS, 128), F32)],
        scratch_shapes=[pltpu.SemaphoreType.DMA(()), pltpu.SemaphoreType.DMA(()), pltpu.SemaphoreType.DMA(()), dma7, dma7],
        input_output_aliases={0: 0},
        compiler_params=pltpu.CompilerParams(has_side_effects=True),
    )(full, small)


def _rope_tables(s):
    pos = jnp.arange(s, dtype=F32) - PAD_FRONT

    def cs(d):
        inv = ROPE_BASE ** (-jnp.arange(0, d, 2, dtype=F32) / d)
        ang = pos[:, None] * inv[None, :]
        return jnp.cos(ang), jnp.sin(ang)

    c, sn = cs(ROPE_D)
    z = jnp.zeros_like(c)
    ktab = (jnp.concatenate([c, c, z, z], axis=1), jnp.concatenate([-sn, z, z, z], axis=1),
            jnp.concatenate([z, sn, z, z], axis=1))
    one = jnp.ones_like(c)
    qtab = tuple(ATT_SCALE * jnp.concatenate(parts, axis=1) for parts in (
        [one, one, one, one, c, c, z, z], [z, z, z, z, -sn, z, z, z], [z, z, z, z, z, sn, z, z]))
    lane = jnp.arange(HD) == DH + ROPE_D
    qbias = lane.astype(F32)[None, :]
    kbias = jnp.where((pos < 0)[:, None] & lane[None, DH:], NEG_INF, 0.0).astype(F32)
    c, sn = cs(DH)
    return qtab, ktab, qbias, kbias, jnp.concatenate([c, c], axis=1), jnp.concatenate([-sn, sn], axis=1)


def _grad_mm(a, b, m, n, s, name):
    return _matmul(a, b, "tn", m, n, s, min(m, 512), min(n, 1024), s, BF16, name, n_outer=True)


def _local_step(x2, target2, hp0, big, small_w, gather_wait=None, reduce_start=None, reduce_wait=None):
    norm_w, qn_w, kvn_w, gn_w, gn_b, fin_w = small_w
    s = x2.shape[0] + BLK
    tm = _row_tile(s, 1408)
    qtab, ktab, qbias, kbias, c128, s128 = _rope_tables(s)
    consts, gamma = _ret_consts()

    h, xn, r1 = _rms_in(x2, hp0, norm_w, s)
    proj = _matmul(xn, big, "nt", s, PROJ_W, D, tm, 512, D, BF16, "proj", b_off=(BIG_IN // 512, 0))
    if gather_wait is not None:
        big = gather_wait(big, proj)
    uq = big[BIG_UQ:BIG_UQ + 384].reshape(HEADS, DH + ROPE_D, Q_RANK)
    wq = jnp.pad(uq, ((0, 0), (0, HD - DH - ROPE_D), (0, 0))).reshape(HEADS * HD, Q_RANK)
    wukv = big[BIG_UKV:BIG_UKV + 256].reshape(2 * WIDTH, KV_RANK)
    wbm = big[BIG_BM:BIG_BM + 1024].reshape(D, WIDTH)
    wbr = big[BIG_BR:BIG_BR + 1024].reshape(D, WIDTH)
    cqn, rq, ckvn, rkv, qq, kk, vv = _mla_prep(proj, qn_w, kvn_w, wq, wukv, qtab, ktab, qbias, kbias, s)
    o_mla, y_mla, lse = _attn_fwd(qq, kk, vv, proj, s)
    y_ret, on, rstd, qr, kr, rall = _ret_fwd(proj, gn_w, gn_b, c128, s128, consts, gamma, s)
    u_mla, u_ret, merged = _merge_fwd(y_mla, y_ret, wbm, wbr, proj, s)
    h2 = _out_fwd(merged, big, h, s)
    dh2, dh2b, loss_blk, dfin = _loss_bwd(h2, target2, fin_w, s)

    dproj, du_mla, du_ret = _merge_bwd(dh2b, big, u_mla, u_ret, proj, s)
    g_out = _grad_mm(merged, dh2b, D, D, s, "grad_w_out")
    dproj, do_mla, delta = _branch_mla_bwd(dproj, du_mla, wbm, o_mla, proj, s)
    g_bm = _grad_mm(du_mla, y_mla, D, WIDTH, s, "grad_w_branch_mla")
    dproj, do_ret, dgw, dgb = _branch_ret_bwd(dproj, du_ret, wbr, on, rstd, gn_w, gn_b, proj, s)
    g_br = _grad_mm(du_ret, y_ret, D, WIDTH, s, "grad_w_branch_ret")
    dproj = _ret_bwd(dproj, qr, kr, proj, rall, do_ret, c128, s128, consts, gamma, s)
    dqq, dkk, dvv = _attn_bwd(qq, kk, vv, do_mla, lse, delta.reshape(lse.shape), s)
    dproj, dq, dkv, dqnw, dkvnw = _mla_prep_bwd(dproj, dqq, dkk, dvv, proj, rq, rkv, qn_w, kvn_w, wq, wukv,
                                                qtab, ktab, s)
    g_q = _grad_mm(dq, cqn, HEADS * HD, Q_RANK, s, "grad_w_uq")
    g_ukv = _grad_mm(dkv, ckvn, 2 * WIDTH, KV_RANK, s, "grad_w_ukv")
    g_in = _grad_mm(dproj, xn, PROJ_W, D, s, "grad_w_in")

    g_uq = g_q.reshape(HEADS, HD, Q_RANK)[:, :DH + ROPE_D]
    grads = (g_in, g_out, g_bm.reshape(1024, D), g_br.reshape(1024, D), g_uq.reshape(384, D), g_ukv.reshape(256, D))
    token, travelling = (None, grads) if reduce_start is None else reduce_start(grads)
    dxn = _matmul(dproj, big, "nn", s, D, PROJ_W, tm, 1024, 1024, F32, "dxn", b_off=(BIG_IN // 1024, 0), after=token)
    grad_x, gmeta_blk, dnorm = _rms_in_bwd(dxn, h, r1, dh2, norm_w, s)
    small = _small_rows((dnorm, dqnw, dkvnw, dgw, dgb, dfin), loss_blk, gmeta_blk[PAD_FRONT:])
    return grad_x, travelling if reduce_wait is None else reduce_wait(travelling, dnorm), small


def _small_rows(ws, first=None, last=None):
    def part(a, rows):
        a = a.reshape(-1, 128)
        return a if a.shape[0] == rows else jnp.pad(a, ((0, rows - a.shape[0]), (0, 0)))

    bounds = (SM_NORM, SM_QN, SM_KVN, SM_GNW, SM_GNB, SM_FIN, SM_META)
    first = jnp.zeros((SM_NORM, 128), F32) if first is None else first
    last = jnp.zeros((SM_ROWS - SM_META, 128), F32) if last is None else last
    return jnp.concatenate([part(first, SM_NORM)] + [part(w, bounds[k + 1] - bounds[k]) for k, w in enumerate(ws)]
                           + [part(last, SM_ROWS - SM_META)], axis=0)


def kernel(x, meta, norm_w, w_in, mla_q_norm_w, mla_w_uq, mla_kv_norm_w, mla_w_ukv, ret_gn_w, ret_gn_b, w_branch_mla, w_branch_ret, w_out, final_norm_w, loss_target, m_meta, m_norm_w, m_w_in, m_mla_q_norm_w, m_mla_w_uq, m_mla_kv_norm_w, m_mla_w_ukv, m_ret_gn_w, m_ret_gn_b, m_w_branch_mla, m_w_branch_ret, m_w_out, m_final_norm_w, v_meta, v_norm_w, v_w_in, v_mla_q_norm_w, v_mla_w_uq, v_mla_kv_norm_w, v_mla_w_ukv, v_ret_gn_w, v_ret_gn_b, v_w_branch_mla, v_w_branch_ret, v_w_out, v_final_norm_w):
    j = 2 * lax.axis_index("x") + lax.axis_index("y")
    tr = lambda w: w[0].T.reshape(-1, D).astype(BF16)
    pos = jnp.stack([j, lax.axis_index("c")]).astype(jnp.int32)
    put = lambda buf, rows, off: lax.dynamic_update_slice(buf, rows, (off, 0))
    big0 = lax.empty((BIG_ROWS, D), BF16)
    big0 = put(big0, w_out[0].astype(BF16), BIG_OUT + 512 * j)
    big0 = put(big0, tr(w_branch_mla), BIG_BM + 256 * j)
    big0 = put(big0, tr(w_branch_ret), BIG_BR + 256 * j)
    big0 = put(big0, tr(mla_w_uq), BIG_UQ + 96 * j)
    big0 = put(big0, tr(mla_w_ukv), BIG_UKV + 64 * j)
    big0 = put(big0, jnp.zeros((AG_JUNK_END - AG_JUNK_REST, D), BF16), AG_JUNK_REST)
    cuts, dest = _in_cuts()
    w_in_t = tr(w_in)
    dest = jnp.asarray(dest, jnp.int32)[j]
    for k in range(len(cuts) - 1):
        big0 = put(big0, w_in_t[cuts[k]:cuts[k + 1]], dest[k])
    big0 = put(big0, jnp.zeros((ZERO_ROWS, D), BF16), BIG_IN + IN_WIDTH)
    big, meta_all = _allgather_w_in(big0, meta)
    gather_sems, big = _allgather_rest_start(big)

    def gather_wait(big_travelling, after):
        return _allgather_rest_wait(gather_sems, big_travelling, after)

    meta_full = meta_all.transpose(1, 0, 2).reshape(N_META, D)
    hp0 = jnp.concatenate([jnp.zeros((PAD_FRONT, D), F32), meta_full], axis=0)
    small_w = (norm_w, mla_q_norm_w, mla_kv_norm_w, ret_gn_w, ret_gn_b, final_norm_w.reshape(1, D))

    def reduce_start(grads):
        chip_part = _rs_core_add(grads, _rs_core_exchange(grads), pos)
        sems, part_thru, land_thru, token = _rs_chip_start(chip_part)
        return token, (sems, part_thru, land_thru)

    def reduce_wait(state, after):
        return _rs_chip_wait(*state, after)

    grad_x, (chip_part, land2), small = _local_step(x[0], loss_target[0], hp0, big, small_w, gather_wait,
                                                    reduce_start, reduce_wait)
    full, small_all = _rs_finish(_rs_chip_add(chip_part, land2, pos), small)
    full = full.reshape(PK_ROWS, D)

    untr = lambda lo, hi, rows: full[lo:hi].reshape(rows, -1).T
    grads = {
        "w_out": full[PK_OUT:PK_BM], "w_branch_mla": untr(PK_BM, PK_BR, 512), "w_branch_ret": untr(PK_BR, PK_UQ, 512),
        "mla_w_uq": untr(PK_UQ, PK_UKV, 384), "mla_w_ukv": untr(PK_UKV, PK_PAD, 512),
    }
    big_w = {"mla_w_uq": (mla_w_uq, m_mla_w_uq, v_mla_w_uq),
             "mla_w_ukv": (mla_w_ukv, m_mla_w_ukv, v_mla_w_ukv),
             "w_branch_mla": (w_branch_mla, m_w_branch_mla, v_w_branch_mla),
             "w_branch_ret": (w_branch_ret, m_w_branch_ret, v_w_branch_ret), "w_out": (w_out, m_w_out, v_w_out)}
    res = {"w_in": tuple(a[None] for a in _adamw_t(w_in[0], full, m_w_in[0], v_w_in[0], "adamw_w_in"))}
    for name, (w, m, v) in big_w.items():
        d, nm, nv = _adamw(w[0], grads[name], m[0], v[0], "adamw_" + name)
        res[name] = (grads[name][None], d[None], nm[None], nv[None])

    small_m = (m_norm_w, m_mla_q_norm_w, m_mla_kv_norm_w, m_ret_gn_w, m_ret_gn_b, m_final_norm_w.reshape(1, D))
    small_v = (v_norm_w, v_mla_q_norm_w, v_mla_kv_norm_w, v_ret_gn_w, v_ret_gn_b, v_final_norm_w.reshape(1, D))
    gs, ds, ms, vs = _adamw_small(_small_rows(small_w), small_all, _small_rows(small_m), _small_rows(small_v))
    names = ["norm_w", "mla_q_norm_w", "mla_kv_norm_w", "ret_gn_w", "ret_gn_b", "final_norm_w"]
    bounds = [SM_NORM, SM_QN, SM_KVN, SM_GNW, SM_GNB, SM_FIN]
    for k, name in enumerate(names):
        shape = (D,) if name == "final_norm_w" else (1, -1)
        rows = small_w[k].size // 128
        res[name] = tuple(a[bounds[k]:bounds[k] + rows].reshape(shape) for a in (gs, ds, ms, vs))
    g_meta = lax.dynamic_slice_in_dim(gs[SM_META:SM_META + 256].reshape(N_META, D), j * (D // N_CHIPS), D // N_CHIPS, axis=1)
    res["meta"] = (g_meta,) + tuple(_adamw(meta, g_meta, m_meta, v_meta, "adamw_meta"))

    order = ["meta", "norm_w", "w_in", "mla_q_norm_w", "mla_w_uq", "mla_kv_norm_w", "mla_w_ukv", "ret_gn_w", "ret_gn_b",
             "w_branch_mla", "w_branch_ret", "w_out", "final_norm_w"]
    return (gs[0, 0], grad_x[None]) + tuple(res[n][k] for k in range(4) for n in order)
```

```python
import functools
import math

import numpy as np
import jax
import jax.numpy as jnp
from jax import lax
from jax.experimental import pallas as pl
from jax.experimental.pallas import tpu as pltpu

F32 = jnp.float32
BF16 = jnp.bfloat16
MESH = pl.DeviceIdType.MESH

D = 2048
N_META = 16
BLK = 128
PAD_FRONT = BLK - N_META
HEADS = 8
DH = 128
ROPE_D = 64
Q_RANK = 512
KV_RANK = 256
WIDTH = HEADS * DH
ROPE_BASE = 10000.0
NORM_EPS = 1e-6
GN_EPS = 1e-5
NEG_INF = -1e30
ATT_SCALE = (DH + ROPE_D) ** -0.5
RET_SCALE = DH ** -0.5
IN_WIDTH = 10048
N_CHIPS = 4
IN_SHARD = IN_WIDTH // N_CHIPS
ADAM_LR, ADAM_B1, ADAM_B2, ADAM_EPS, ADAM_WD, ADAM_STEP = 0.001, 0.9, 0.999, 1e-08, 0.01, 10

R_Q, R_K, R_V, Z_RET, Z_MLA, GATE0, GATE1 = 0, 1024, 2048, 3072, 4096, 5120, 5120 + D
C_Q = 5120 + 2 * D
C_KV = C_Q + Q_RANK
K_PE = C_KV + KV_RANK
PROJ_W = 10240
IN_RUNS = ((0, 832, C_Q), (832, 1856, Z_MLA), (1856, 4928, R_Q), (4928, 5952, Z_RET), (5952, IN_WIDTH, GATE0))

BIG_OUT, BIG_BM, BIG_BR, BIG_UQ, BIG_UKV, BIG_JUNK, BIG_IN = 0, 2048, 3072, 4096, 4480, 4736, 5120
BIG_ROWS = BIG_IN + PROJ_W
ZERO_ROWS = PROJ_W - IN_WIDTH
PK_IN, PK_OUT, PK_BM, PK_BR, PK_UQ, PK_UKV, PK_PAD, PK_ROWS = 0, 2512, 3024, 3280, 3536, 3632, 3696, 3712
PK_HALF = PK_ROWS // 2
SM_LOSS, SM_NORM, SM_QN, SM_KVN, SM_GNW, SM_GNB, SM_FIN, SM_META, SM_ROWS = 0, 8, 24, 32, 40, 48, 56, 72, 328

VMEM_LIMIT = 56 * 1024 * 1024


def _pieces(shard):
    j = shard
    out = [(PK_OUT, BIG_OUT + 512 * j, 512), (PK_BM, BIG_BM + 256 * j, 256), (PK_BR, BIG_BR + 256 * j, 256),
           (PK_UQ, BIG_UQ + 96 * j, 96), (PK_UKV, BIG_UKV + 64 * j, 64)]
    for lo, hi, new in IN_RUNS:
        a, b = max(lo, IN_SHARD * j), min(hi, IN_SHARD * (j + 1))
        if b > a:
            out.append((PK_IN + a - IN_SHARD * j, BIG_IN + new + a - lo, b - a))
    out += [(PK_PAD, BIG_JUNK + 16 * j, 16)]
    return out


AG_JUNK_REST, AG_JUNK_IN, AG_JUNK_END = 4736, 4864, 4928
AG_REST_HALF, AG_IN_HALF = 608, 1264


def _ag_half(shard, w_in, half):
    pieces = [(b, n) for p, b, n in _pieces(shard) if p < PK_PAD and (p < PK_OUT) == w_in]
    pieces.append((AG_JUNK_IN + 16 * shard, 16) if w_in else (AG_JUNK_REST + 32 * shard, 32))
    size = AG_IN_HALF if w_in else AG_REST_HALF
    out, pos = [], 0
    for b, n in pieces:
        s, e = max(pos, half * size), min(pos + n, (half + 1) * size)
        if e > s:
            out.append((s, b + s - pos, e - s))
        pos += n
    assert pos == 2 * size
    return out


def _grad_half(shard, half):
    j = shard
    pieces = [(0, b - BIG_IN, p, n) for p, b, n in _pieces(j) if p < PK_OUT]
    pieces += [(1, 512 * j, PK_OUT, 512), (2, 256 * j, PK_BM, 256), (3, 256 * j, PK_BR, 256), (4, 96 * j, PK_UQ, 96),
               (5, 64 * j, PK_UKV, 64), (0, IN_WIDTH, PK_PAD, PK_ROWS - PK_PAD)]
    lo, hi = half * PK_HALF, (half + 1) * PK_HALF
    out = []
    for k, r0, p, n in pieces:
        s, e = max(p, lo), min(p + n, hi)
        if e > s:
            out.append((k, r0 + s - p, s - lo, e - s))
    assert sum(n for _, _, _, n in out) == PK_HALF
    return out


def _row_tile(rows, cap):
    best = BLK
    for t in range(BLK, cap + 1, BLK):
        if rows % t == 0:
            best = t
    return best


def _cparams(sem):
    return pltpu.CompilerParams(dimension_semantics=sem, vmem_limit_bytes=VMEM_LIMIT)


def _dot(a, b, form):
    dn = {"nt": (((1,), (1,)), ((), ())), "nn": (((1,), (0,)), ((), ())), "tn": (((0,), (0,)), ((), ()))}[form]
    return lax.dot_general(a, b, dn, preferred_element_type=F32)


def _sigmoid(v):
    return 1.0 / (1.0 + jnp.exp(-v))


def _matmul(a, b, form, m, n, k, tm, tn, tk, out_dtype, name, a_off=(0, 0), b_off=(0, 0), n_outer=False, after=None):
    nk = k // tk
    gi, gj = m // tm, n // tn

    def ij(g0, g1):
        return (g1, g0) if n_outer else (g0, g1)

    if form == "nt":
        a_spec = pl.BlockSpec((tm, tk), lambda g0, g1, kk: (ij(g0, g1)[0] + a_off[0], kk + a_off[1]))
        b_spec = pl.BlockSpec((tn, tk), lambda g0, g1, kk: (ij(g0, g1)[1] + b_off[0], kk + b_off[1]))
    elif form == "nn":
        a_spec = pl.BlockSpec((tm, tk), lambda g0, g1, kk: (ij(g0, g1)[0] + a_off[0], kk + a_off[1]))
        b_spec = pl.BlockSpec((tk, tn), lambda g0, g1, kk: (kk + b_off[0], ij(g0, g1)[1] + b_off[1]))
    else:
        a_spec = pl.BlockSpec((tk, tm), lambda g0, g1, kk: (kk + a_off[0], ij(g0, g1)[0] + a_off[1]))
        b_spec = pl.BlockSpec((tk, tn), lambda g0, g1, kk: (kk + b_off[0], ij(g0, g1)[1] + b_off[1]))
    o_spec = pl.BlockSpec((tm, tn), lambda g0, g1, kk: ij(g0, g1))

    def body(a_ref, b_ref, *rest):
        o_ref, *acc = rest[0 if after is None else 1:]
        p = _dot(a_ref[...], b_ref[...], form)
        if nk == 1:
            o_ref[...] = p.astype(o_ref.dtype)
        else:
            acc_ref, = acc
            kk = pl.program_id(2)

            @pl.when(kk == 0)
            def _():
                acc_ref[...] = p

            @pl.when(kk > 0)
            def _():
                acc_ref[...] += p

            @pl.when(kk == nk - 1)
            def _():
                o_ref[...] = acc_ref[...].astype(o_ref.dtype)

    extra = [] if after is None else [after]
    return pl.pallas_call(
        body, name=name, grid=(gj, gi, nk) if n_outer else (gi, gj, nk),
        in_specs=[a_spec, b_spec] + [pl.BlockSpec(memory_space=pl.ANY)] * len(extra), out_specs=o_spec,
        out_shape=jax.ShapeDtypeStruct((m, n), out_dtype),
        scratch_shapes=[] if nk == 1 else [pltpu.VMEM((tm, tn), F32)],
        compiler_params=_cparams(("parallel", "parallel", "arbitrary")),
    )(a, b, *extra)


def _rms_in(x, hp0, norm_w, s):
    def body(x_ref, hp0_ref, w_ref, h_ref, xn_ref, r_ref):
        def run(hv):
            r = lax.rsqrt(jnp.mean(hv * hv, axis=-1, keepdims=True) + NORM_EPS)
            h_ref[...] = hv
            xn_ref[...] = (hv * r * w_ref[...]).astype(BF16)
            r_ref[...] = r

        @pl.when(pl.program_id(0) == 0)
        def _():
            run(hp0_ref[...])

        @pl.when(pl.program_id(0) > 0)
        def _():
            run(x_ref[...])

    return pl.pallas_call(
        body, name="rms_in", grid=(s // BLK,),
        in_specs=[pl.BlockSpec((BLK, D), lambda i: (jnp.maximum(i - 1, 0), 0)),
                  pl.BlockSpec((BLK, D), lambda i: (0, 0)), pl.BlockSpec((1, D), lambda i: (0, 0))],
        out_specs=[pl.BlockSpec((BLK, D), lambda i: (i, 0)), pl.BlockSpec((BLK, D), lambda i: (i, 0)),
                   pl.BlockSpec((BLK, 1), lambda i: (i, 0))],
        out_shape=[jax.ShapeDtypeStruct((s, D), F32), jax.ShapeDtypeStruct((s, D), BF16),
                   jax.ShapeDtypeStruct((s, 1), F32)],
        compiler_params=_cparams(("arbitrary",)),
    )(x, hp0, norm_w)


def _rope64(t, c, sa, sb):
    return t * c + pltpu.roll(t, t.shape[1] - ROPE_D // 2, 1) * sa + pltpu.roll(t, ROPE_D // 2, 1) * sb


def _rope128(t, c, sg):
    return t * c + pltpu.roll(t, DH // 2, 1) * sg


HD = 2 * DH


def _mla_prep(proj, qn_w, kvn_w, wq, wukv, qtab, ktab, qbias, kbias, s):
    tm = _row_tile(s, 384)

    def body(cq_ref, ckv_ref, kpe_ref, qnw_ref, kvnw_ref, wq_ref, wukv_ref, qc_ref, qa_ref, qb_ref,
             kc_ref, ka_ref, kb_ref, qbias_ref, kbias_ref, cqn_ref, rq_ref, ckvn_ref, rkv_ref, qq_ref, kk_ref, vv_ref):
        cq = cq_ref[...].astype(F32)
        rq = lax.rsqrt(jnp.mean(cq * cq, axis=-1, keepdims=True) + NORM_EPS)
        cqn = (cq * rq * qnw_ref[...]).astype(BF16)
        ckv = ckv_ref[...].astype(F32)
        rkv = lax.rsqrt(jnp.mean(ckv * ckv, axis=-1, keepdims=True) + NORM_EPS)
        ckvn = (ckv * rkv * kvnw_ref[...]).astype(BF16)
        cqn_ref[...] = cqn
        rq_ref[...] = rq
        ckvn_ref[...] = ckvn
        rkv_ref[...] = rkv
        q = _dot(cqn, wq_ref[...], "nt")
        kv = _dot(ckvn, wukv_ref[...], "nt")
        kp = (_rope64(kpe_ref[...].astype(F32), kc_ref[...], ka_ref[...], kb_ref[...]) + kbias_ref[...]).astype(BF16)
        qc, qa, qb, qbias = qc_ref[...], qa_ref[...], qb_ref[...], qbias_ref[...]
        ones = jnp.ones((tm, DH), BF16)
        for h in range(HEADS):
            lo, mid, hi = h * HD, h * HD + DH, (h + 1) * HD
            qq_ref[:, lo:hi] = (_rope64(q[:, lo:hi], qc, qa, qb) + qbias).astype(BF16)
            kk_ref[:, lo:mid] = kv[:, lo:mid].astype(BF16)
            kk_ref[:, mid:hi] = kp
            vv_ref[:, lo:mid] = kv[:, mid:hi].astype(BF16)
            vv_ref[:, mid:hi] = ones

    row = lambda w, cb: pl.BlockSpec((tm, w), lambda i: (i, cb))
    full = lambda a: pl.BlockSpec(a.shape, lambda i: (0, 0))
    wide = jax.ShapeDtypeStruct((s, HEADS * HD), BF16)
    return pl.pallas_call(
        body, name="mla_prep", grid=(s // tm,),
        in_specs=[row(Q_RANK, C_Q // Q_RANK), row(KV_RANK, C_KV // KV_RANK), row(DH, K_PE // DH),
                  full(qn_w), full(kvn_w), full(wq), full(wukv), row(HD, 0), row(HD, 0), row(HD, 0),
                  row(DH, 0), row(DH, 0), row(DH, 0), full(qbias), row(DH, 0)],
        out_specs=[row(Q_RANK, 0), row(1, 0), row(KV_RANK, 0), row(1, 0), row(HEADS * HD, 0), row(HEADS * HD, 0),
                   row(HEADS * HD, 0)],
        out_shape=[jax.ShapeDtypeStruct((s, Q_RANK), BF16), jax.ShapeDtypeStruct((s, 1), F32),
                   jax.ShapeDtypeStruct((s, KV_RANK), BF16), jax.ShapeDtypeStruct((s, 1), F32), wide, wide, wide],
        compiler_params=_cparams(("parallel",)),
    )(proj, proj, proj, qn_w, kvn_w, wq, wukv, *qtab, *ktab, qbias, kbias)


def _diag_mask(t):
    return lax.broadcasted_iota(jnp.int32, (t, t), 0) <= lax.broadcasted_iota(jnp.int32, (t, t), 1)


def _silu(z):
    return z * _sigmoid(z)


def _attn_fwd(qq, kk, vv, proj, s):
    t = _row_tile(s, 384)
    n = s // t

    def body(q_ref, k_ref, v_ref, z_ref, o_ref, y_ref, lse_ref, acc_ref, m_ref):
        qi = pl.program_id(1)
        q = q_ref[...]
        m_ref[...] = jnp.full(m_ref.shape, NEG_INF, F32)
        acc_ref[...] = jnp.zeros(acc_ref.shape, F32)

        def keys(ki):
            return pl.ds(pl.multiple_of(ki * t, t), t)

        def scores(ki):
            return _dot(k_ref[keys(ki), :], q, "nt")

        def tile(ki, st):
            m_old = m_ref[...]
            m_new = jnp.maximum(m_old, jnp.max(st, axis=0, keepdims=True))
            pt = jnp.exp(st - m_new).astype(BF16)
            acc_ref[...] = jnp.exp(m_old - m_new) * acc_ref[...] + _dot(v_ref[keys(ki), :], pt, "tn")
            m_ref[...] = m_new

        def inner(ki, st):
            st_after = scores(ki + 1)
            tile(ki, st)
            return st_after

        st_diag = lax.fori_loop(0, qi, inner, scores(0))
        tile(qi, jnp.where(_diag_mask(t), st_diag, NEG_INF))

        l = acc_ref[DH:DH + 1, :]
        o = (acc_ref[:DH, :] / l).T
        o_ref[...] = o.astype(BF16)
        y_ref[...] = (o * _silu(z_ref[...].astype(F32))).astype(BF16)
        lse_ref[0, 0] = m_ref[...] + jnp.log(l)

    qtile = pl.BlockSpec((t, DH), lambda h, i: (i, h))
    head = pl.BlockSpec((s, HD), lambda h, i: (0, h))
    return pl.pallas_call(
        body, name="attn_fwd", grid=(HEADS, s // t),
        in_specs=[pl.BlockSpec((t, HD), lambda h, i: (i, h)), head, head,
                  pl.BlockSpec((t, DH), lambda h, i: (i, Z_MLA // DH + h))],
        out_specs=[qtile, qtile, pl.BlockSpec((1, 1, 1, t), lambda h, i: (h, i, 0, 0))],
        out_shape=[jax.ShapeDtypeStruct((s, WIDTH), BF16), jax.ShapeDtypeStruct((s, WIDTH), BF16),
                   jax.ShapeDtypeStruct((HEADS, s // t, 1, t), F32)],
        scratch_shapes=[pltpu.VMEM((HD, t), F32), pltpu.VMEM((1, t), F32)],
        compiler_params=_cparams(("parallel", "arbitrary")),
    )(qq, kk, vv, proj)


def _ret_consts():
    log_g = np.log1p(-(2.0 ** (-5.0 - np.arange(HEADS, dtype=np.float64))))
    n = np.arange(BLK, dtype=np.float64)
    diff = n[:, None] - n[None, :]
    decay = np.where(diff >= 0, np.exp(log_g[:, None, None] * np.maximum(diff, 0.0)), 0.0)
    zeta = np.exp(log_g[:, None] * (BLK - 1.0 - n))[:, :, None]
    xi = np.exp(log_g[:, None] * (n + 1.0))[:, :, None]
    gamma = [float(np.float32(np.exp(g * BLK))) for g in log_g]
    return (jnp.asarray(decay, F32), jnp.asarray(zeta, F32), jnp.asarray(xi, F32)), gamma


def _ret_fwd(proj, gn_w, gn_b, c128, s128, consts, gamma, s):
    nb = s // BLK
    decay, zeta, xi = consts

    def body(rq_ref, rk_ref, rv_ref, z_ref, gw_ref, gb_ref, c_ref, s_ref, dm_ref, zt_ref, xi_ref,
             y_ref, on_ref, rstd_ref, qr_ref, kr_ref, rall_ref, state):
        @pl.when(pl.program_id(0) == 0)
        def _():
            state[...] = jnp.zeros_like(state)

        c, sg = c_ref[...], s_ref[...]
        for h in range(HEADS):
            sl = slice(h * DH, (h + 1) * DH)
            q = _rope128(rq_ref[:, sl].astype(F32), c, sg).astype(BF16)
            kf = _rope128(rk_ref[:, sl].astype(F32), c, sg) * RET_SCALE
            k = kf.astype(BF16)
            v = rv_ref[:, sl]
            qr_ref[:, sl] = q
            kr_ref[:, sl] = k
            r_prev = state[h]
            rall_ref[0, h] = r_prev
            a = _dot(q, k, "nt") * dm_ref[h]
            o = _dot(a.astype(BF16), v, "nn") + _dot(q, r_prev.astype(BF16), "nn") * xi_ref[h]
            state[h] = r_prev * gamma[h] + _dot((kf * zt_ref[h]).astype(BF16), v, "tn")
            mu = jnp.mean(o, axis=-1, keepdims=True)
            var = jnp.mean(jnp.square(o - mu), axis=-1, keepdims=True)
            rstd = lax.rsqrt(var + GN_EPS)
            on = (o - mu) * rstd
            rstd_ref[h] = rstd
            on_ref[:, sl] = on.astype(BF16)
            ogn = on * gw_ref[:, sl] + gb_ref[:, sl]
            y_ref[:, sl] = (ogn * _silu(z_ref[:, sl].astype(F32))).astype(BF16)

    seg = lambda cb: pl.BlockSpec((BLK, WIDTH), lambda i: (i, cb))
    full = lambda a: pl.BlockSpec(a.shape, lambda i: (0,) * a.ndim)
    tab = pl.BlockSpec((BLK, DH), lambda i: (i, 0))
    return pl.pallas_call(
        body, name="ret_fwd", grid=(nb,),
        in_specs=[seg(R_Q // WIDTH), seg(R_K // WIDTH), seg(R_V // WIDTH), seg(Z_RET // WIDTH), full(gn_w), full(gn_b),
                  tab, tab, full(decay), full(zeta), full(xi)],
        out_specs=[seg(0), seg(0), pl.BlockSpec((HEADS, BLK, 1), lambda i: (0, i, 0)), seg(0), seg(0),
                   pl.BlockSpec((1, HEADS, DH, DH), lambda i: (i, 0, 0, 0))],
        out_shape=[jax.ShapeDtypeStruct((s, WIDTH), BF16), jax.ShapeDtypeStruct((s, WIDTH), BF16),
                   jax.ShapeDtypeStruct((HEADS, s, 1), F32), jax.ShapeDtypeStruct((s, WIDTH), BF16),
                   jax.ShapeDtypeStruct((s, WIDTH), BF16), jax.ShapeDtypeStruct((nb, HEADS, DH, DH), F32)],
        scratch_shapes=[pltpu.VMEM((HEADS, DH, DH), F32)],
        compiler_params=_cparams(("arbitrary",)),
    )(proj, proj, proj, proj, gn_w, gn_b, c128, s128, decay, zeta, xi)


def _merge_fwd(y_mla, y_ret, wbm, wbr, proj, s):
    tm, tn = _row_tile(s, 1408), 512

    def body(ym_ref, yr_ref, wm_ref, wr_ref, g0_ref, g1_ref, um_ref, ur_ref, mg_ref):
        um = _dot(ym_ref[...], wm_ref[...], "nt")
        ur = _dot(yr_ref[...], wr_ref[...], "nt")
        um_ref[...] = um.astype(BF16)
        ur_ref[...] = ur.astype(BF16)
        mg_ref[...] = (_sigmoid(g0_ref[...].astype(F32)) * um + _sigmoid(g1_ref[...].astype(F32)) * ur).astype(BF16)

    yspec = pl.BlockSpec((tm, WIDTH), lambda i, j: (i, 0))
    wspec = pl.BlockSpec((tn, WIDTH), lambda i, j: (j, 0))
    ospec = pl.BlockSpec((tm, tn), lambda i, j: (i, j))
    return pl.pallas_call(
        body, name="merge_fwd", grid=(s // tm, D // tn),
        in_specs=[yspec, yspec, wspec, wspec, pl.BlockSpec((tm, tn), lambda i, j: (i, GATE0 // tn + j)),
                  pl.BlockSpec((tm, tn), lambda i, j: (i, GATE1 // tn + j))],
        out_specs=[ospec, ospec, ospec],
        out_shape=[jax.ShapeDtypeStruct((s, D), BF16)] * 3,
        compiler_params=_cparams(("parallel", "parallel")),
    )(y_mla, y_ret, wbm, wbr, proj, proj)


def _out_fwd(merged, big, h, s):
    tm, tn = _row_tile(s, 1408), 512

    def body(m_ref, w_ref, h_ref, o_ref):
        o_ref[...] = h_ref[...] + _dot(m_ref[...], w_ref[...], "nn")

    return pl.pallas_call(
        body, name="out_fwd", grid=(s // tm, D // tn),
        in_specs=[pl.BlockSpec((tm, D), lambda i, j: (i, 0)), pl.BlockSpec((D, tn), lambda i, j: (BIG_OUT // D, j)),
                  pl.BlockSpec((tm, tn), lambda i, j: (i, j))],
        out_specs=pl.BlockSpec((tm, tn), lambda i, j: (i, j)),
        out_shape=jax.ShapeDtypeStruct((s, D), F32),
        compiler_params=_cparams(("parallel", "parallel")),
    )(merged, big, h)


def _loss_bwd(h2, target, fin_w, s):
    nb = s // BLK

    def body(h2_ref, t_ref, w_ref, dh_ref, dhb_ref, loss_ref, dw_ref):
        i = pl.program_id(0)

        @pl.when(i == 0)
        def _():
            dh_ref[...] = jnp.zeros_like(dh_ref)
            dhb_ref[...] = jnp.zeros_like(dhb_ref)
            loss_ref[...] = jnp.zeros_like(loss_ref)
            dw_ref[...] = jnp.zeros_like(dw_ref)

        @pl.when(i > 0)
        def _():
            hv = h2_ref[...]
            w = w_ref[...]
            r = lax.rsqrt(jnp.mean(hv * hv, axis=-1, keepdims=True) + NORM_EPS)
            nrm = hv * r
            e = nrm * w - t_ref[...]
            loss_ref[...] += jnp.full(loss_ref.shape, 0.5 / D, F32) * jnp.sum(e * e)
            dy = e * (1.0 / D)
            dw_ref[...] += jnp.sum(dy * nrm, axis=0, keepdims=True)
            g = dy * w
            dh = r * (g - nrm * jnp.mean(g * nrm, axis=-1, keepdims=True))
            dh_ref[...] = dh
            dhb_ref[...] = dh.astype(BF16)

    blk = pl.BlockSpec((BLK, D), lambda i: (i, 0))
    return pl.pallas_call(
        body, name="loss_bwd", grid=(nb,),
        in_specs=[blk, pl.BlockSpec((BLK, D), lambda i: (jnp.maximum(i - 1, 0), 0)), pl.BlockSpec((1, D), lambda i: (0, 0))],
        out_specs=[blk, blk, pl.BlockSpec((8, 128), lambda i: (0, 0)), pl.BlockSpec((1, D), lambda i: (0, 0))],
        out_shape=[jax.ShapeDtypeStruct((s, D), F32), jax.ShapeDtypeStruct((s, D), BF16),
                   jax.ShapeDtypeStruct((8, 128), F32), jax.ShapeDtypeStruct((1, D), F32)],
        compiler_params=_cparams(("arbitrary",)),
    )(h2, target, fin_w)


def _merge_bwd(dh2b, big, u_mla, u_ret, proj, s):
    tm, tn = _row_tile(s, 1408), 512

    def body(d_ref, w_ref, um_ref, ur_ref, gate_ref, dproj_ref, dum_ref, dur_ref, dm_ref):
        branch = pl.program_id(2)

        @pl.when(branch == 0)
        def _():
            dm_ref[...] = _dot(d_ref[...], w_ref[...], "nt")

        dm = dm_ref[...]
        gt = _sigmoid(gate_ref[...].astype(F32))

        @pl.when(branch == 0)
        def _():
            dproj_ref[...] = (dm * um_ref[...].astype(F32) * gt * (1.0 - gt)).astype(BF16)
            dum_ref[...] = (dm * gt).astype(BF16)

        @pl.when(branch == 1)
        def _():
            dproj_ref[...] = (dm * ur_ref[...].astype(F32) * gt * (1.0 - gt)).astype(BF16)
            dur_ref[...] = (dm * gt).astype(BF16)

    ospec = pl.BlockSpec((tm, tn), lambda i, j, b: (i, j))
    gate = pl.BlockSpec((tm, tn), lambda i, j, b: (i, GATE0 // tn + b * (D // tn) + j))
    return pl.pallas_call(
        body, name="merge_bwd", grid=(s // tm, D // tn, 2),
        in_specs=[pl.BlockSpec((tm, D), lambda i, j, b: (i, 0)),
                  pl.BlockSpec((tn, D), lambda i, j, b: (BIG_OUT // tn + j, 0)), ospec, ospec, gate],
        out_specs=[gate, ospec, ospec],
        out_shape=[jax.ShapeDtypeStruct((s, PROJ_W), BF16), jax.ShapeDtypeStruct((s, D), BF16),
                   jax.ShapeDtypeStruct((s, D), BF16)],
        scratch_shapes=[pltpu.VMEM((tm, tn), F32)],
        compiler_params=_cparams(("parallel", "parallel", "arbitrary")),
    )(dh2b, big, u_mla, u_ret, proj)


def _dsilu(z):
    sg = _sigmoid(z)
    return sg * (1.0 + z * (1.0 - sg))


def _branch_mla_bwd(dproj, du, wbm, o_mla, proj, s):
    tm = _row_tile(s, 384)

    def body(dproj_in, du_ref, w_ref, o_ref, z_ref, dz_ref, do_ref, delta_ref):
        del dproj_in
        dy = _dot(du_ref[...], w_ref[...], "nn")
        z = z_ref[...].astype(F32)
        o = o_ref[...].astype(F32)
        do = dy * _silu(z)
        do_ref[...] = do.astype(BF16)
        dz_ref[...] = (dy * o * _dsilu(z)).astype(BF16)
        prod = do * o
        for h in range(HEADS):
            delta_ref[h] = jnp.sum(prod[:, h * DH:(h + 1) * DH], axis=-1, keepdims=True)

    row = lambda w, cb: pl.BlockSpec((tm, w), lambda i: (i, cb))
    return pl.pallas_call(
        body, name="branch_mla_bwd", grid=(s // tm,),
        in_specs=[ANY, row(D, 0), pl.BlockSpec((D, WIDTH), lambda i: (0, 0)), row(WIDTH, 0),
                  row(WIDTH, Z_MLA // WIDTH)],
        out_specs=[row(WIDTH, Z_MLA // WIDTH), row(WIDTH, 0), pl.BlockSpec((HEADS, tm, 1), lambda i: (0, i, 0))],
        out_shape=[jax.ShapeDtypeStruct((s, PROJ_W), BF16), jax.ShapeDtypeStruct((s, WIDTH), BF16),
                   jax.ShapeDtypeStruct((HEADS, s, 1), F32)],
        input_output_aliases={0: 0},
        compiler_params=_cparams(("parallel",)),
    )(dproj, du, wbm, o_mla, proj)


def _branch_ret_bwd(dproj, du, wbr, on, rstd, gn_w, gn_b, proj, s):
    tm = BLK

    def body(dproj_in, du_ref, w_ref, on_ref, rstd_ref, gw_ref, gb_ref, z_ref, dz_ref, do_ref, dgw_ref, dgb_ref):
        del dproj_in

        @pl.when(pl.program_id(0) == 0)
        def _():
            dgw_ref[...] = jnp.zeros_like(dgw_ref)
            dgb_ref[...] = jnp.zeros_like(dgb_ref)

        dy = _dot(du_ref[...], w_ref[...], "nn")
        z = z_ref[...].astype(F32)
        on = on_ref[...].astype(F32)
        gw = gw_ref[...]
        dogn = dy * _silu(z)
        dz_ref[...] = (dy * (on * gw + gb_ref[...]) * _dsilu(z)).astype(BF16)
        dgw_ref[...] += jnp.sum(dogn * on, axis=0, keepdims=True)
        dgb_ref[...] += jnp.sum(dogn, axis=0, keepdims=True)
        don = dogn * gw
        for h in range(HEADS):
            sl = slice(h * DH, (h + 1) * DH)
            dn, nh = don[:, sl], on[:, sl]
            do = rstd_ref[h] * (dn - jnp.mean(dn, axis=-1, keepdims=True)
                                - nh * jnp.mean(dn * nh, axis=-1, keepdims=True))
            do_ref[:, sl] = do.astype(BF16)

    row = lambda w, cb: pl.BlockSpec((tm, w), lambda i: (i, cb))
    vec = pl.BlockSpec((1, WIDTH), lambda i: (0, 0))
    return pl.pallas_call(
        body, name="branch_ret_bwd", grid=(s // tm,),
        in_specs=[ANY, row(D, 0), pl.BlockSpec((D, WIDTH), lambda i: (0, 0)), row(WIDTH, 0),
                  pl.BlockSpec((HEADS, tm, 1), lambda i: (0, i, 0)), vec, vec, row(WIDTH, Z_RET // WIDTH)],
        out_specs=[row(WIDTH, Z_RET // WIDTH), row(WIDTH, 0), vec, vec],
        out_shape=[jax.ShapeDtypeStruct((s, PROJ_W), BF16), jax.ShapeDtypeStruct((s, WIDTH), BF16)]
        + [jax.ShapeDtypeStruct((1, WIDTH), F32)] * 2,
        input_output_aliases={0: 0},
        compiler_params=_cparams(("arbitrary",)),
    )(dproj, du, wbr, on, rstd, gn_w, gn_b, proj)


def _ret_bwd(dproj, qr, kr, proj, rall, do_ret, c128, s128, consts, gamma, s):
    nb = s // BLK
    decay, zeta, xi = consts

    def body(dproj_in, q_ref, k_ref, v_ref, r_ref, do_ref, c_ref, s_ref, dm_ref, zt_ref, xi_ref, out_ref, gstate):
        del dproj_in
        dq_ref, dk_ref, dv_ref = (out_ref.at[:, pl.ds(off, WIDTH)] for off in (R_Q, R_K, R_V))

        @pl.when(pl.program_id(0) == 0)
        def _():
            gstate[...] = jnp.zeros_like(gstate)

        c, sg = c_ref[...], s_ref[...]
        for h in range(HEADS):
            sl = slice(h * DH, (h + 1) * DH)
            q, k, v, do = q_ref[:, sl], k_ref[:, sl], v_ref[:, sl], do_ref[:, sl]
            dm = dm_ref[h]
            g_next = gstate[h]
            gb = g_next.astype(BF16)
            a = (_dot(q, k, "nt") * dm).astype(BF16)
            da = (_dot(do, v, "nt") * dm).astype(BF16)
            dox = (do.astype(F32) * xi_ref[h]).astype(BF16)
            dq = _dot(da, k, "nn") + _dot(dox, r_ref[0, h].astype(BF16), "nt")
            dk = _dot(da, q, "tn") + _dot(v, gb, "nt") * zt_ref[h]
            kz = (k.astype(F32) * zt_ref[h]).astype(BF16)
            dv = _dot(a, do, "tn") + _dot(kz, gb, "nn")
            gstate[h] = g_next * gamma[h] + _dot(q, dox, "tn")
            dk = dk * RET_SCALE
            dq_ref[:, sl] = _rope128(dq, c, -sg).astype(BF16)
            dk_ref[:, sl] = _rope128(dk, c, -sg).astype(BF16)
            dv_ref[:, sl] = dv.astype(BF16)

    rev = lambda cb: pl.BlockSpec((BLK, WIDTH), lambda i: (nb - 1 - i, cb))
    full = lambda a: pl.BlockSpec(a.shape, lambda i: (0,) * a.ndim)
    tab = pl.BlockSpec((BLK, DH), lambda i: (nb - 1 - i, 0))
    return pl.pallas_call(
        body, name="ret_bwd", grid=(nb,),
        in_specs=[ANY, rev(0), rev(0), rev(R_V // WIDTH),
                  pl.BlockSpec((1, HEADS, DH, DH), lambda i: (nb - 1 - i, 0, 0, 0)),
                  rev(0), tab, tab, full(decay), full(zeta), full(xi)],
        out_specs=pl.BlockSpec((BLK, 3 * WIDTH), lambda i: (nb - 1 - i, R_Q // (3 * WIDTH))),
        out_shape=jax.ShapeDtypeStruct((s, PROJ_W), BF16),
        scratch_shapes=[pltpu.VMEM((HEADS, DH, DH), F32)],
        input_output_aliases={0: 0},
        compiler_params=_cparams(("arbitrary",)),
    )(dproj, qr, kr, proj, rall, do_ret, c128, s128, decay, zeta, xi)


def _attn_bwd(qq, kk, vv, do, lse, delta, s):
    t = _row_tile(s, 384)
    n = s // t

    def body(q_ref, k_ref, v_ref, do_ref, lse_ref, delta_ref, dq_ref, dk_ref, dv_ref, dk_acc, dv_acc):
        ki = pl.program_id(1)

        @pl.when(ki == 0)
        def _():
            dq_ref[...] = jnp.zeros(dq_ref.shape, F32)

        k, v = k_ref[...], v_ref[...]
        dk_acc[...] = jnp.zeros(dk_acc.shape, F32)
        dv_acc[...] = jnp.zeros(dv_acc.shape, F32)

        def rows(qi):
            return pl.ds(pl.multiple_of(qi * t, t), t)

        def products(qi):
            return _dot(k, q_ref[rows(qi), :], "nt"), _dot(v, do_ref[rows(qi), :], "nt")

        def tile(qi, st, dpt):
            q, dov = q_ref[rows(qi), :], do_ref[rows(qi), :]
            pt = jnp.exp(st - lse_ref[0, qi])
            dv_acc[...] += _dot(pt.astype(BF16), dov, "nn")
            dst = (pt * (dpt - delta_ref[0, qi])).astype(BF16)
            dk_acc[...] += _dot(dst, q, "nn")
            dq_ref[rows(qi), :] += _dot(dst, k, "tn")

        def inner(qi, carry):
            after = products(jnp.minimum(qi + 1, n - 1))
            tile(qi, *carry)
            return after

        st, dpt = products(ki)
        ahead = products(jnp.minimum(ki + 1, n - 1))
        tile(ki, jnp.where(_diag_mask(t), st, NEG_INF), dpt)
        lax.fori_loop(ki + 1, n, inner, ahead)

        dk_ref[...] = dk_acc[...].astype(BF16)
        dv_ref[...] = dv_acc[...].astype(BF16)

    head = lambda w: pl.BlockSpec((s, w), lambda h, i: (0, h))
    stat = pl.BlockSpec((1, n, 1, t), lambda h, i: (h, 0, 0, 0))
    return pl.pallas_call(
        body, name="attn_bwd", grid=(HEADS, n),
        in_specs=[head(HD), pl.BlockSpec((t, HD), lambda h, i: (i, h)), pl.BlockSpec((t, DH), lambda h, i: (i, 2 * h)),
                  head(DH), stat, stat],
        out_specs=[head(HD), pl.BlockSpec((t, HD), lambda h, i: (i, h)), pl.BlockSpec((t, DH), lambda h, i: (i, h))],
        out_shape=[jax.ShapeDtypeStruct((s, HEADS * HD), F32), jax.ShapeDtypeStruct((s, HEADS * HD), BF16),
                   jax.ShapeDtypeStruct((s, WIDTH), BF16)],
        scratch_shapes=[pltpu.VMEM((t, HD), F32), pltpu.VMEM((t, DH), F32)],
        compiler_params=_cparams(("parallel", "arbitrary")),
    )(qq, kk, vv, do, lse, delta)


def _rms_bwd(dn, nhat, r, w):
    g = dn * w
    return r * (g - nhat * jnp.mean(g * nhat, axis=-1, keepdims=True)), dn * nhat


def _mla_prep_bwd(dproj, dqq, dkk, dvv, proj, rq, rkv, qn_w, kvn_w, wq, wukv, qtab, ktab, s):
    tm = _row_tile(s, 384)
    tail = PROJ_W - C_Q

    def body(dproj_in, dqq_ref, dkk_ref, dvv_ref, cq_ref, ckv_ref, rq_ref, rkv_ref, qnw_ref, kvnw_ref, wq_ref,
             wukv_ref, qc_ref, qa_ref, qb_ref, kc_ref, ka_ref, kb_ref,
             tail_ref, dq_ref, dkv_ref, dqnw_ref, dkvnw_ref):
        del dproj_in
        dcq_ref = tail_ref.at[:, pl.ds(0, Q_RANK)]
        dckv_ref = tail_ref.at[:, pl.ds(C_KV - C_Q, KV_RANK)]
        dkpe_ref = tail_ref.at[:, pl.ds(K_PE - C_Q, 2 * DH)]

        @pl.when(pl.program_id(0) == 0)
        def _():
            dqnw_ref[...] = jnp.zeros_like(dqnw_ref)
            dkvnw_ref[...] = jnp.zeros_like(dkvnw_ref)

        qc, qa, qb = qc_ref[...], qa_ref[...], qb_ref[...]
        dkp = jnp.zeros((tm, DH), F32)
        for h in range(HEADS):
            lo, mid, hi = h * HD, h * HD + DH, (h + 1) * HD
            dq_ref[:, lo:hi] = _rope64(dqq_ref[:, lo:hi], qc, -qa, -qb).astype(BF16)
            dkv_ref[:, lo:mid] = dkk_ref[:, lo:mid]
            dkv_ref[:, mid:hi] = dvv_ref[:, h * DH:(h + 1) * DH]
            dkp = dkp + dkk_ref[:, mid:hi].astype(F32)
        dcqn = _dot(dq_ref[...], wq_ref[...], "nn")
        rq_v = rq_ref[...]
        dcq, prod = _rms_bwd(dcqn, cq_ref[...].astype(F32) * rq_v, rq_v, qnw_ref[...])
        dcq_ref[...] = dcq.astype(BF16)
        dqnw_ref[...] += jnp.sum(prod, axis=0, keepdims=True)
        dckvn = _dot(dkv_ref[...], wukv_ref[...], "nn")
        rkv_v = rkv_ref[...]
        dckv, prod = _rms_bwd(dckvn, ckv_ref[...].astype(F32) * rkv_v, rkv_v, kvnw_ref[...])
        dckv_ref[...] = dckv.astype(BF16)
        dkvnw_ref[...] += jnp.sum(prod, axis=0, keepdims=True)
        dkpe_ref[:, :DH] = _rope64(dkp, kc_ref[...], -ka_ref[...], -kb_ref[...]).astype(BF16)
        dkpe_ref[:, DH:] = jnp.zeros((tm, DH), BF16)

    row = lambda w, cb: pl.BlockSpec((tm, w), lambda i: (i, cb))
    full = lambda a: pl.BlockSpec(a.shape, lambda i: (0, 0))
    wide = jax.ShapeDtypeStruct((s, HEADS * HD), BF16)
    return pl.pallas_call(
        body, name="mla_prep_bwd", grid=(s // tm,),
        in_specs=[ANY, row(HEADS * HD, 0), row(HEADS * HD, 0), row(WIDTH, 0),
                  row(Q_RANK, C_Q // Q_RANK), row(KV_RANK, C_KV // KV_RANK), row(1, 0), row(1, 0),
                  full(qn_w), full(kvn_w), full(wq), full(wukv), row(HD, 0), row(HD, 0), row(HD, 0),
                  row(DH, 0), row(DH, 0), row(DH, 0)],
        out_specs=[row(tail, C_Q // tail), row(HEADS * HD, 0), row(HEADS * HD, 0),
                   pl.BlockSpec((1, Q_RANK), lambda i: (0, 0)), pl.BlockSpec((1, KV_RANK), lambda i: (0, 0))],
        out_shape=[jax.ShapeDtypeStruct((s, PROJ_W), BF16), wide, wide,
                   jax.ShapeDtypeStruct((1, Q_RANK), F32), jax.ShapeDtypeStruct((1, KV_RANK), F32)],
        input_output_aliases={0: 0},
        compiler_params=_cparams(("arbitrary",)),
    )(dproj, dqq, dkk, dvv, proj, proj, rq, rkv, qn_w, kvn_w, wq, wukv, *qtab, *ktab)


def _rms_in_bwd(dxn, h, r, dh2, norm_w, s):
    def body(dxn_ref, h_ref, r_ref, dh2_ref, w_ref, gx_ref, gm_ref, dw_ref):
        i = pl.program_id(0)
        r_v = r_ref[...]
        dx, prod = _rms_bwd(dxn_ref[...], h_ref[...] * r_v, r_v, w_ref[...])
        dh = dh2_ref[...] + dx

        @pl.when(i == 0)
        def _():
            gm_ref[...] = dh
            dw_ref[...] = jnp.sum(prod, axis=0, keepdims=True)

        @pl.when(i > 0)
        def _():
            gx_ref[...] = dh
            dw_ref[...] += jnp.sum(prod, axis=0, keepdims=True)

    blk = pl.BlockSpec((BLK, D), lambda i: (i, 0))
    return pl.pallas_call(
        body, name="rms_in_bwd", grid=(s // BLK,),
        in_specs=[blk, blk, pl.BlockSpec((BLK, 1), lambda i: (i, 0)), blk, pl.BlockSpec((1, D), lambda i: (0, 0))],
        out_specs=[pl.BlockSpec((BLK, D), lambda i: (jnp.maximum(i - 1, 0), 0)), pl.BlockSpec((BLK, D), lambda i: (0, 0)),
                   pl.BlockSpec((1, D), lambda i: (0, 0))],
        out_shape=[jax.ShapeDtypeStruct((s - BLK, D), F32), jax.ShapeDtypeStruct((BLK, D), F32),
                   jax.ShapeDtypeStruct((1, D), F32)],
        compiler_params=_cparams(("arbitrary",)),
    )(dxn, h, r, dh2, norm_w)


def _adam_math(w, g, m, v):
    m = ADAM_B1 * m + (1.0 - ADAM_B1) * g
    v = ADAM_B2 * v + (1.0 - ADAM_B2) * (g * g)
    m_hat = m / (1.0 - ADAM_B1 ** ADAM_STEP)
    v_hat = v / (1.0 - ADAM_B2 ** ADAM_STEP)
    return -ADAM_LR * (m_hat / (jnp.sqrt(v_hat) + ADAM_EPS) + ADAM_WD * w), m, v


def _adamw(w, g, m, v, name):
    rows, cols = w.shape
    tr = rows
    for cand in (128, 64, 32, 16, 8):
        if rows % cand == 0:
            tr = cand
            break

    def body(w_ref, g_ref, m_ref, v_ref, d_ref, nm_ref, nv_ref):
        d_ref[...], nm_ref[...], nv_ref[...] = _adam_math(w_ref[...], g_ref[...], m_ref[...], v_ref[...])

    spec = pl.BlockSpec((tr, cols), lambda i: (i, 0))
    return pl.pallas_call(
        body, name=name, grid=(rows // tr,), in_specs=[spec] * 4, out_specs=[spec] * 3,
        out_shape=[jax.ShapeDtypeStruct((rows, cols), F32)] * 3,
        compiler_params=_cparams(("parallel",)),
    )(w, g, m, v)


def _adamw_t(w_t, g_t, m_t, v_t, name):
    cols, rows = w_t.shape

    def body(w_ref, g_ref, m_ref, v_ref, go_ref, d_ref, nm_ref, nv_ref):
        g = g_ref[...]
        go_ref[...] = g
        d_ref[...], nm_ref[...], nv_ref[...] = _adam_math(w_ref[...], g, m_ref[...], v_ref[...])

    blk = pl.BlockSpec((128, rows), lambda k: (k, 0))
    return pl.pallas_call(
        body, name=name, grid=(pl.cdiv(cols, 128),), in_specs=[blk] * 4, out_specs=[blk] * 4,
        out_shape=[jax.ShapeDtypeStruct((cols, rows), F32)] * 4, compiler_params=_cparams(("parallel",)),
    )(w_t, g_t, m_t, v_t)


def _adamw_small(w, gall, m, v):
    def body(w_ref, g_ref, m_ref, v_ref, gs_ref, d_ref, nm_ref, nv_ref):
        g = g_ref[0]
        for dev in range(1, 8):
            g = g + g_ref[dev]
        gs_ref[...] = g
        d_ref[...], nm_ref[...], nv_ref[...] = _adam_math(w_ref[...], g, m_ref[...], v_ref[...])

    return pl.pallas_call(
        body, name="adamw_small", out_shape=[jax.ShapeDtypeStruct((SM_ROWS, 128), F32)] * 4,
        compiler_params=pltpu.CompilerParams(vmem_limit_bytes=VMEM_LIMIT),
    )(w, gall, m, v)


ADD_ROWS = 464
HALF_BLOCKS = PK_HALF // ADD_ROWS


def _rs_core_add(srcs, land, pos):
    def body(pos_ref, *refs):
        src_refs, (land_ref, out_ref, own, sem) = refs[:len(srcs)], refs[len(srcs):]
        for s in range(N_CHIPS):
            for hf in range(2):
                @pl.when((pl.program_id(0) == s) & (pos_ref[1] == hf))
                def _():
                    for k, r0, p, n in _grad_half(s, hf):
                        pltpu.make_async_copy(src_refs[k].at[pl.ds(r0, n), :], own.at[pl.ds(p, n), :], sem).start()
        pltpu.make_async_copy(own, own, sem).wait()
        out_ref[0] = (own[...].astype(F32) + land_ref[0].astype(F32)).astype(BF16)

    blk = pl.BlockSpec((1, PK_HALF, D), lambda s, pos_ref: (s, 0, 0))
    return pl.pallas_call(
        body, name="rs_core_add",
        grid_spec=pltpu.PrefetchScalarGridSpec(
            num_scalar_prefetch=1, grid=(N_CHIPS,), in_specs=[ANY] * len(srcs) + [blk], out_specs=blk,
            scratch_shapes=[pltpu.VMEM((PK_HALF, D), BF16), pltpu.SemaphoreType.DMA(())]),
        out_shape=jax.ShapeDtypeStruct((N_CHIPS, PK_HALF, D), BF16),
        compiler_params=_cparams(("arbitrary",)),
    )(pos, *srcs, land)


def _rs_chip_add(cp, land, pos):
    def body(pos_ref, a_ref, l_ref, o_ref):
        o_ref[0] = ((a_ref[0].astype(F32) + l_ref[0].astype(F32)) + l_ref[1].astype(F32)) + l_ref[2].astype(F32)

    return pl.pallas_call(
        body, name="rs_chip_add",
        grid_spec=pltpu.PrefetchScalarGridSpec(
            num_scalar_prefetch=1, grid=(HALF_BLOCKS,),
            in_specs=[pl.BlockSpec((1, ADD_ROWS, D), lambda i, pos_ref: (pos_ref[0], i, 0)),
                      pl.BlockSpec((3, ADD_ROWS, D), lambda i, pos_ref: (0, i, 0))],
            out_specs=pl.BlockSpec((1, ADD_ROWS, D), lambda i, pos_ref: (pos_ref[1], i, 0))),
        out_shape=jax.ShapeDtypeStruct((2, PK_HALF, D), F32),
        compiler_params=_cparams(("parallel",)),
    )(pos, cp, land)


ANY = pl.BlockSpec(memory_space=pl.ANY)


def _mesh_pos():
    return lax.axis_index("x"), lax.axis_index("y"), lax.axis_index("c")


def _chip_peer(x, y, c, r):
    return (jnp.bitwise_xor(x, r >> 1), jnp.bitwise_xor(y, r & 1), c)


def _allgather_w_in(big0, wt, meta_loc):
    def body(big0_ref, wt_ref, meta_ref, big_ref, metaf_ref, lsem, ssem, rsem, fssem, frsem, msem_s, msem_r, osem):
        del big0_ref
        x, y, c = _mesh_pos()
        j = 2 * x + y
        me, sibling = (x, y, c), (x, y, 1 - c)

        def half_wait(s_sem, r_sem, halves=1):
            rows = big_ref.at[pl.ds(0, halves * AG_IN_HALF), :]
            return pltpu.make_async_remote_copy(src_ref=rows, dst_ref=rows, send_sem=s_sem, recv_sem=r_sem,
                                                device_id=me, device_id_type=MESH)

        own_meta = pltpu.make_async_copy(meta_ref, metaf_ref.at[j], lsem)
        own_meta.start()
        for s in range(N_CHIPS):
            for hf in range(2):
                @pl.when((j == s) & (c == hf))
                def _():
                    for r in (1, 2, 3):
                        for p, b, n in _ag_half(s, True, hf):
                            pltpu.make_async_remote_copy(
                                src_ref=wt_ref.at[pl.ds(p, n), :], dst_ref=big_ref.at[pl.ds(b, n), :],
                                send_sem=ssem.at[r - 1], recv_sem=rsem.at[r - 1],
                                device_id=_chip_peer(x, y, c, r), device_id_type=MESH).start()

            @pl.when(j == s)
            def _():
                for hf in range(2):
                    for p, b, n in _ag_half(s, True, hf):
                        pltpu.make_async_remote_copy(
                            src_ref=wt_ref.at[pl.ds(p, n), :], dst_ref=big_ref.at[pl.ds(b, n), :],
                            send_sem=osem.at[0], recv_sem=osem.at[1], device_id=sibling, device_id_type=MESH).start()

        meta_copies = [pltpu.make_async_remote_copy(
            src_ref=meta_ref, dst_ref=metaf_ref.at[j], send_sem=msem_s.at[r - 1], recv_sem=msem_r.at[r - 1],
            device_id=_chip_peer(x, y, c, r), device_id_type=MESH) for r in (1, 2, 3)]
        for cp in meta_copies:
            cp.start()
        for r in (1, 2, 3):
            half_wait(ssem.at[r - 1], rsem.at[r - 1]).wait_recv()
            src_shard = jnp.bitwise_xor(j, r)
            for s in range(N_CHIPS):
                for hf in range(2):
                    @pl.when((src_shard == s) & (c == hf))
                    def _():
                        for _, b, n in _ag_half(s, True, hf):
                            rows = big_ref.at[pl.ds(b, n), :]
                            pltpu.make_async_remote_copy(
                                src_ref=rows, dst_ref=rows, send_sem=fssem.at[r - 1], recv_sem=frsem.at[r - 1],
                                device_id=sibling, device_id_type=MESH).start()
        for r in (1, 2, 3):
            half_wait(fssem.at[r - 1], frsem.at[r - 1]).wait_recv()
        for r in (1, 2, 3):
            half_wait(ssem.at[r - 1], rsem.at[r - 1]).wait_send()
            half_wait(fssem.at[r - 1], frsem.at[r - 1]).wait_send()
        own_rows = half_wait(osem.at[0], osem.at[1], halves=2)
        own_rows.wait_recv()
        own_rows.wait_send()
        for cp in meta_copies:
            cp.wait_send()
            cp.wait_recv()
        own_meta.wait()

    dma3 = pltpu.SemaphoreType.DMA((3,))
    return pl.pallas_call(
        body, name="allgather_w_in", in_specs=[ANY, ANY, ANY], out_specs=[ANY, ANY],
        out_shape=[jax.ShapeDtypeStruct((BIG_ROWS, D), BF16), jax.ShapeDtypeStruct((N_CHIPS,) + meta_loc.shape, F32)],
        scratch_shapes=[pltpu.SemaphoreType.DMA(()), dma3, dma3, dma3, dma3, dma3, dma3, pltpu.SemaphoreType.DMA((2,))],
        input_output_aliases={0: 0},
        compiler_params=pltpu.CompilerParams(has_side_effects=True),
    )(big0, wt, meta_loc)


HBM = pl.BlockSpec(memory_space=pltpu.HBM)
SEM = pl.BlockSpec(memory_space=pltpu.SEMAPHORE)
EFFECT = pltpu.SideEffectType.DATAFLOW_SIDE_EFFECTING
REST_PEERS = [(r, dc) for r in (1, 2, 3) for dc in (0, 1)]


def _allgather_rest_start(big):
    def body(big_ref, *rest):
        s_sems, r_sems = rest[:6], rest[6:12]
        x, y, c = _mesh_pos()
        j = 2 * x + y
        for s in range(N_CHIPS):
            for hf in range(2):
                @pl.when((j == s) & (c == hf))
                def _():
                    for k, (r, dc) in enumerate(REST_PEERS):
                        for _, b, n in _ag_half(s, False, hf):
                            rows = big_ref.at[pl.ds(b, n), :]
                            pltpu.make_async_remote_copy(
                                src_ref=rows, dst_ref=rows, send_sem=s_sems[k], recv_sem=r_sems[2 * (r - 1) + hf],
                                device_id=_chip_peer(x, y, dc, r), device_id_type=MESH).start()

    sem = pltpu.SemaphoreType.DMA(())
    out = pl.pallas_call(
        body, name="allgather_rest_start", out_shape=(sem,) * 12 + (pltpu.HBM(big.shape, big.dtype),),
        in_specs=(HBM,), out_specs=(SEM,) * 12 + (HBM,), input_output_aliases={0: 12},
        compiler_params=pltpu.CompilerParams(has_side_effects=EFFECT),
    )(pltpu.with_memory_space_constraint(big, pltpu.HBM))
    return out[:12], out[12]


def _allgather_rest_wait(sems, big, after):
    def body(big_ref, *rest):
        s_sems, r_sems = rest[:6], rest[6:12]
        me = _mesh_pos()
        rows = big_ref.at[pl.ds(0, AG_REST_HALF), :]
        for k in range(6):
            copy = pltpu.make_async_remote_copy(src_ref=rows, dst_ref=rows, send_sem=s_sems[k], recv_sem=r_sems[k],
                                                device_id=me, device_id_type=MESH)
            copy.wait_send()
            copy.wait_recv()

    return pl.pallas_call(
        body, name="allgather_rest_wait", out_shape=pltpu.HBM(big.shape, big.dtype),
        in_specs=(HBM,) + (SEM,) * 12 + (pl.BlockSpec(memory_space=pl.ANY),), out_specs=HBM,
        input_output_aliases={0: 0}, compiler_params=pltpu.CompilerParams(has_side_effects=EFFECT),
    )(big, *sems, after)


RS_SPLIT = 2


def _rs_core_exchange(srcs):
    def body(*refs):
        src_refs, (land_ref, ssem, rsem) = refs[:len(srcs)], refs[len(srcs):]
        x, y, c = _mesh_pos()
        me, sibling = (x, y, c), (x, y, 1 - c)
        for hf in range(2):
            @pl.when(c == hf)
            def _():
                for s in range(N_CHIPS):
                    for k, r0, p, n in _grad_half(s, 1 - hf):
                        pltpu.make_async_remote_copy(
                            src_ref=src_refs[k].at[pl.ds(r0, n), :], dst_ref=land_ref.at[s, pl.ds(p, n), :],
                            send_sem=ssem, recv_sem=rsem, device_id=sibling, device_id_type=MESH).start()
        whole = pltpu.make_async_remote_copy(src_ref=land_ref, dst_ref=land_ref, send_sem=ssem, recv_sem=rsem,
                                             device_id=me, device_id_type=MESH)
        whole.wait_recv()
        whole.wait_send()

    return pl.pallas_call(
        body, name="rs_core_exchange", in_specs=[ANY] * len(srcs), out_specs=ANY,
        out_shape=jax.ShapeDtypeStruct((N_CHIPS, PK_HALF, D), BF16),
        scratch_shapes=[pltpu.SemaphoreType.DMA(())] * 2,
        compiler_params=pltpu.CompilerParams(has_side_effects=True),
    )(*srcs)


def _rs_chip_start(cp):
    def body(cp_ref, land_ref, s1, s2, s3, r1, r2, r3, cp_thru, land_thru, token):
        x, y, c = _mesh_pos()
        j = 2 * x + y
        for r, s_sem, r_sem in zip((1, 2, 3), (s1, s2, s3), (r1, r2, r3)):
            pltpu.make_async_remote_copy(
                src_ref=cp_ref.at[jnp.bitwise_xor(j, r)], dst_ref=land_ref.at[r - 1], send_sem=s_sem, recv_sem=r_sem,
                device_id=_chip_peer(x, y, c, r), device_id_type=MESH).start()
        token[...] = jnp.zeros_like(token)

    land_shape = (3, PK_HALF, D)
    sem = pltpu.SemaphoreType.DMA(())
    out = pl.pallas_call(
        body, name="rs_chip_start",
        out_shape=(sem,) * 6 + (pltpu.HBM(cp.shape, cp.dtype), pltpu.HBM(land_shape, BF16),
                                jax.ShapeDtypeStruct((8, 128), F32)),
        in_specs=(HBM, HBM), out_specs=(SEM,) * 6 + (HBM, HBM, pl.BlockSpec(memory_space=pltpu.VMEM)),
        input_output_aliases={0: 6, 1: 7},
        compiler_params=pltpu.CompilerParams(has_side_effects=EFFECT),
    )(pltpu.with_memory_space_constraint(cp, pltpu.HBM),
      pltpu.with_memory_space_constraint(lax.empty(land_shape, BF16), pltpu.HBM))
    return out[:6], out[6], out[7], out[8]


def _rs_chip_wait(sems, cp_thru, land_thru, after):
    def body(cp_ref, land_ref, s1, s2, s3, r1, r2, r3, after_ref, cp_out, land_out):
        me = _mesh_pos()
        for r, s_sem, r_sem in zip((1, 2, 3), (s1, s2, s3), (r1, r2, r3)):
            copy = pltpu.make_async_remote_copy(src_ref=cp_ref.at[0], dst_ref=land_ref.at[r - 1], send_sem=s_sem,
                                                recv_sem=r_sem, device_id=me, device_id_type=MESH)
            copy.wait_send()
            copy.wait_recv()

    return pl.pallas_call(
        body, name="rs_chip_wait",
        out_shape=(pltpu.HBM(cp_thru.shape, cp_thru.dtype), pltpu.HBM(land_thru.shape, land_thru.dtype)),
        in_specs=(HBM, HBM) + (SEM,) * 6 + (pl.BlockSpec(memory_space=pl.ANY),), out_specs=(HBM, HBM),
        input_output_aliases={0: 0, 1: 1},
        compiler_params=pltpu.CompilerParams(has_side_effects=EFFECT),
    )(cp_thru, land_thru, *sems, after)


def _rs_finish(full, small):
    n = PK_HALF // RS_SPLIT

    def body(full_in_ref, sm_ref, full_ref, all_ref, lsem, ssem, rsem, sm_s, sm_r):
        del full_in_ref
        x, y, c = _mesh_pos()
        me, sibling = (x, y, c), (x, y, 1 - c)
        my_id = 4 * x + 2 * y + c
        own_sm = pltpu.make_async_copy(sm_ref, all_ref.at[my_id], lsem)
        own_sm.start()
        for q in range(RS_SPLIT):
            rows = full_ref.at[c, pl.ds(q * n, n), :]
            pltpu.make_async_remote_copy(src_ref=rows, dst_ref=rows, send_sem=ssem, recv_sem=rsem,
                                         device_id=sibling, device_id_type=MESH).start()
        half = pltpu.make_async_remote_copy(src_ref=full_ref.at[c], dst_ref=full_ref.at[c], send_sem=ssem,
                                            recv_sem=rsem, device_id=me, device_id_type=MESH)
        smalls = [pltpu.make_async_remote_copy(
            src_ref=sm_ref, dst_ref=all_ref.at[my_id], send_sem=sm_s.at[r - 1], recv_sem=sm_r.at[r - 1],
            device_id=(jnp.bitwise_xor(x, r >> 2), jnp.bitwise_xor(y, (r >> 1) & 1), jnp.bitwise_xor(c, r & 1)),
            device_id_type=MESH) for r in range(1, 8)]
        for cpy in smalls:
            cpy.start()
        half.wait_recv()
        half.wait_send()
        for cpy in smalls:
            cpy.wait_recv()
        for cpy in smalls:
            cpy.wait_send()
        own_sm.wait()

    dma7 = pltpu.SemaphoreType.DMA((7,))
    return pl.pallas_call(
        body, name="rs_finish", in_specs=[ANY, ANY], out_specs=[ANY, ANY],
        out_shape=[jax.ShapeDtypeStruct((2, PK_HALF, D), F32), jax.ShapeDtypeStruct((8, SM_ROWS, 128), F32)],
        scratch_shapes=[pltpu.SemaphoreType.DMA(()), pltpu.SemaphoreType.DMA(()), pltpu.SemaphoreType.DMA(()), dma7, dma7],
        input_output_aliases={0: 0},
        compiler_params=pltpu.CompilerParams(has_side_effects=True),
    )(full, small)


def _rope_tables(s):
    pos = jnp.arange(s, dtype=F32) - PAD_FRONT

    def cs(d):
        inv = ROPE_BASE ** (-jnp.arange(0, d, 2, dtype=F32) / d)
        ang = pos[:, None] * inv[None, :]
        return jnp.cos(ang), jnp.sin(ang)

    c, sn = cs(ROPE_D)
    z = jnp.zeros_like(c)
    ktab = (jnp.concatenate([c, c, z, z], axis=1), jnp.concatenate([-sn, z, z, z], axis=1),
            jnp.concatenate([z, sn, z, z], axis=1))
    one = jnp.ones_like(c)
    qtab = tuple(ATT_SCALE * jnp.concatenate(parts, axis=1) for parts in (
        [one, one, one, one, c, c, z, z], [z, z, z, z, -sn, z, z, z], [z, z, z, z, z, sn, z, z]))
    lane = jnp.arange(HD) == DH + ROPE_D
    qbias = lane.astype(F32)[None, :]
    kbias = jnp.where((pos < 0)[:, None] & lane[None, DH:], NEG_INF, 0.0).astype(F32)
    c, sn = cs(DH)
    return qtab, ktab, qbias, kbias, jnp.concatenate([c, c], axis=1), jnp.concatenate([-sn, sn], axis=1)


def _grad_mm(a, b, m, n, s, name):
    return _matmul(a, b, "tn", m, n, s, min(m, 512), min(n, 1024), s, BF16, name, n_outer=True)


def _local_step(x2, target2, hp0, big, small_w, gather_wait=None, reduce_start=None, reduce_wait=None):
    norm_w, qn_w, kvn_w, gn_w, gn_b, fin_w = small_w
    s = x2.shape[0] + BLK
    tm = _row_tile(s, 1408)
    qtab, ktab, qbias, kbias, c128, s128 = _rope_tables(s)
    consts, gamma = _ret_consts()

    h, xn, r1 = _rms_in(x2, hp0, norm_w, s)
    proj = _matmul(xn, big, "nt", s, PROJ_W, D, tm, 512, D, BF16, "proj", b_off=(BIG_IN // 512, 0))
    if gather_wait is not None:
        big = gather_wait(big, proj)
    uq = big[BIG_UQ:BIG_UQ + 384].reshape(HEADS, DH + ROPE_D, Q_RANK)
    wq = jnp.pad(uq, ((0, 0), (0, HD - DH - ROPE_D), (0, 0))).reshape(HEADS * HD, Q_RANK)
    wukv = big[BIG_UKV:BIG_UKV + 256].reshape(2 * WIDTH, KV_RANK)
    wbm = big[BIG_BM:BIG_BM + 1024].reshape(D, WIDTH)
    wbr = big[BIG_BR:BIG_BR + 1024].reshape(D, WIDTH)
    cqn, rq, ckvn, rkv, qq, kk, vv = _mla_prep(proj, qn_w, kvn_w, wq, wukv, qtab, ktab, qbias, kbias, s)
    o_mla, y_mla, lse = _attn_fwd(qq, kk, vv, proj, s)
    y_ret, on, rstd, qr, kr, rall = _ret_fwd(proj, gn_w, gn_b, c128, s128, consts, gamma, s)
    u_mla, u_ret, merged = _merge_fwd(y_mla, y_ret, wbm, wbr, proj, s)
    h2 = _out_fwd(merged, big, h, s)
    dh2, dh2b, loss_blk, dfin = _loss_bwd(h2, target2, fin_w, s)

    dproj, du_mla, du_ret = _merge_bwd(dh2b, big, u_mla, u_ret, proj, s)
    g_out = _grad_mm(merged, dh2b, D, D, s, "grad_w_out")
    dproj, do_mla, delta = _branch_mla_bwd(dproj, du_mla, wbm, o_mla, proj, s)
    g_bm = _grad_mm(du_mla, y_mla, D, WIDTH, s, "grad_w_branch_mla")
    dproj, do_ret, dgw, dgb = _branch_ret_bwd(dproj, du_ret, wbr, on, rstd, gn_w, gn_b, proj, s)
    g_br = _grad_mm(du_ret, y_ret, D, WIDTH, s, "grad_w_branch_ret")
    dproj = _ret_bwd(dproj, qr, kr, proj, rall, do_ret, c128, s128, consts, gamma, s)
    dqq, dkk, dvv = _attn_bwd(qq, kk, vv, do_mla, lse, delta.reshape(lse.shape), s)
    dproj, dq, dkv, dqnw, dkvnw = _mla_prep_bwd(dproj, dqq, dkk, dvv, proj, rq, rkv, qn_w, kvn_w, wq, wukv,
                                                qtab, ktab, s)
    g_q = _grad_mm(dq, cqn, HEADS * HD, Q_RANK, s, "grad_w_uq")
    g_ukv = _grad_mm(dkv, ckvn, 2 * WIDTH, KV_RANK, s, "grad_w_ukv")
    g_in = _grad_mm(dproj, xn, PROJ_W, D, s, "grad_w_in")

    g_uq = g_q.reshape(HEADS, HD, Q_RANK)[:, :DH + ROPE_D]
    grads = (g_in, g_out, g_bm.reshape(1024, D), g_br.reshape(1024, D), g_uq.reshape(384, D), g_ukv.reshape(256, D))
    token, travelling = (None, grads) if reduce_start is None else reduce_start(grads)
    dxn = _matmul(dproj, big, "nn", s, D, PROJ_W, tm, 1024, 1024, F32, "dxn", b_off=(BIG_IN // 1024, 0), after=token)
    grad_x, gmeta_blk, dnorm = _rms_in_bwd(dxn, h, r1, dh2, norm_w, s)
    small = _small_rows((dnorm, dqnw, dkvnw, dgw, dgb, dfin), loss_blk, gmeta_blk[PAD_FRONT:])
    return grad_x, travelling if reduce_wait is None else reduce_wait(travelling, dnorm), small


def _small_rows(ws, first=None, last=None):
    def part(a, rows):
        a = a.reshape(-1, 128)
        return a if a.shape[0] == rows else jnp.pad(a, ((0, rows - a.shape[0]), (0, 0)))

    bounds = (SM_NORM, SM_QN, SM_KVN, SM_GNW, SM_GNB, SM_FIN, SM_META)
    first = jnp.zeros((SM_NORM, 128), F32) if first is None else first
    last = jnp.zeros((SM_ROWS - SM_META, 128), F32) if last is None else last
    return jnp.concatenate([part(first, SM_NORM)] + [part(w, bounds[k + 1] - bounds[k]) for k, w in enumerate(ws)]
                           + [part(last, SM_ROWS - SM_META)], axis=0)


def kernel(x, meta, norm_w, w_in, mla_q_norm_w, mla_w_uq, mla_kv_norm_w, mla_w_ukv, ret_gn_w, ret_gn_b, w_branch_mla, w_branch_ret, w_out, final_norm_w, loss_target, m_meta, m_norm_w, m_w_in, m_mla_q_norm_w, m_mla_w_uq, m_mla_kv_norm_w, m_mla_w_ukv, m_ret_gn_w, m_ret_gn_b, m_w_branch_mla, m_w_branch_ret, m_w_out, m_final_norm_w, v_meta, v_norm_w, v_w_in, v_mla_q_norm_w, v_mla_w_uq, v_mla_kv_norm_w, v_mla_w_ukv, v_ret_gn_w, v_ret_gn_b, v_w_branch_mla, v_w_branch_ret, v_w_out, v_final_norm_w):
    j = 2 * lax.axis_index("x") + lax.axis_index("y")
    tr = lambda w: w[0].T.reshape(-1, D).astype(BF16)
    pos = jnp.stack([j, lax.axis_index("c")]).astype(jnp.int32)
    put = lambda buf, rows, off: lax.dynamic_update_slice(buf, rows, (off, 0))
    big0 = lax.empty((BIG_ROWS, D), BF16)
    big0 = put(big0, w_out[0].astype(BF16), BIG_OUT + 512 * j)
    big0 = put(big0, tr(w_branch_mla), BIG_BM + 256 * j)
    big0 = put(big0, tr(w_branch_ret), BIG_BR + 256 * j)
    big0 = put(big0, tr(mla_w_uq), BIG_UQ + 96 * j)
    big0 = put(big0, tr(mla_w_ukv), BIG_UKV + 64 * j)
    big0 = put(big0, jnp.zeros((AG_JUNK_END - AG_JUNK_REST, D), BF16), AG_JUNK_REST)
    big0 = put(big0, jnp.zeros((ZERO_ROWS, D), BF16), BIG_IN + IN_WIDTH)
    wt = jnp.concatenate([tr(w_in), jnp.zeros((2 * AG_IN_HALF - IN_SHARD, D), BF16)], axis=0)
    big, meta_all = _allgather_w_in(big0, wt, meta)
    gather_sems, big = _allgather_rest_start(big)

    def gather_wait(big_travelling, after):
        return _allgather_rest_wait(gather_sems, big_travelling, after)

    meta_full = meta_all.transpose(1, 0, 2).reshape(N_META, D)
    hp0 = jnp.concatenate([jnp.zeros((PAD_FRONT, D), F32), meta_full], axis=0)
    small_w = (norm_w, mla_q_norm_w, mla_kv_norm_w, ret_gn_w, ret_gn_b, final_norm_w.reshape(1, D))

    def reduce_start(grads):
        chip_part = _rs_core_add(grads, _rs_core_exchange(grads), pos)
        sems, part_thru, land_thru, token = _rs_chip_start(chip_part)
        return token, (sems, part_thru, land_thru)

    def reduce_wait(state, after):
        return _rs_chip_wait(*state, after)

    grad_x, (chip_part, land2), small = _local_step(x[0], loss_target[0], hp0, big, small_w, gather_wait,
                                                    reduce_start, reduce_wait)
    full, small_all = _rs_finish(_rs_chip_add(chip_part, land2, pos), small)
    full = full.reshape(PK_ROWS, D)

    untr = lambda lo, hi, rows: full[lo:hi].reshape(rows, -1).T
    grads = {
        "w_out": full[PK_OUT:PK_BM], "w_branch_mla": untr(PK_BM, PK_BR, 512), "w_branch_ret": untr(PK_BR, PK_UQ, 512),
        "mla_w_uq": untr(PK_UQ, PK_UKV, 384), "mla_w_ukv": untr(PK_UKV, PK_PAD, 512),
    }
    big_w = {"mla_w_uq": (mla_w_uq, m_mla_w_uq, v_mla_w_uq),
             "mla_w_ukv": (mla_w_ukv, m_mla_w_ukv, v_mla_w_ukv),
             "w_branch_mla": (w_branch_mla, m_w_branch_mla, v_w_branch_mla),
             "w_branch_ret": (w_branch_ret, m_w_branch_ret, v_w_branch_ret), "w_out": (w_out, m_w_out, v_w_out)}
    res = {"w_in": tuple(a.T[None] for a in _adamw_t(w_in[0].T, full, m_w_in[0].T, v_w_in[0].T, "adamw_w_in"))}
    for name, (w, m, v) in big_w.items():
        d, nm, nv = _adamw(w[0], grads[name], m[0], v[0], "adamw_" + name)
        res[name] = (grads[name][None], d[None], nm[None], nv[None])

    small_m = (m_norm_w, m_mla_q_norm_w, m_mla_kv_norm_w, m_ret_gn_w, m_ret_gn_b, m_final_norm_w.reshape(1, D))
    small_v = (v_norm_w, v_mla_q_norm_w, v_mla_kv_norm_w, v_ret_gn_w, v_ret_gn_b, v_final_norm_w.reshape(1, D))
    gs, ds, ms, vs = _adamw_small(_small_rows(small_w), small_all, _small_rows(small_m), _small_rows(small_v))
    names = ["norm_w", "mla_q_norm_w", "mla_kv_norm_w", "ret_gn_w", "ret_gn_b", "final_norm_w"]
    bounds = [SM_NORM, SM_QN, SM_KVN, SM_GNW, SM_GNB, SM_FIN]
    for k, name in enumerate(names):
        shape = (D,) if name == "final_norm_w" else (1, -1)
        rows = small_w[k].size // 128
        res[name] = tuple(a[bounds[k]:bounds[k] + rows].reshape(shape) for a in (gs, ds, ms, vs))
    g_meta = lax.dynamic_slice_in_dim(gs[SM_META:SM_META + 256].reshape(N_META, D), j * (D // N_CHIPS), D // N_CHIPS, axis=1)
    res["meta"] = (g_meta,) + tuple(_adamw(meta, g_meta, m_meta, v_meta, "adamw_meta"))

    order = ["meta", "norm_w", "w_in", "mla_q_norm_w", "mla_w_uq", "mla_kv_norm_w", "mla_w_ukv", "ret_gn_w", "ret_gn_b",
             "w_branch_mla", "w_branch_ret", "w_out", "final_norm_w"]
    return (gs[0, 0], grad_x[None]) + tuple(res[n][k] for k in range(4) for n in order)
```

```python
import functools
import math

import numpy as np
import jax
import jax.numpy as jnp
from jax import lax
from jax.experimental import pallas as pl
from jax.experimental.pallas import tpu as pltpu

F32 = jnp.float32
BF16 = jnp.bfloat16
MESH = pl.DeviceIdType.MESH

D = 2048
N_META = 16
BLK = 128
PAD_FRONT = BLK - N_META
HEADS = 8
DH = 128
ROPE_D = 64
Q_RANK = 512
KV_RANK = 256
WIDTH = HEADS * DH
ROPE_BASE = 10000.0
NORM_EPS = 1e-6
GN_EPS = 1e-5
NEG_INF = -1e30
ATT_SCALE = (DH + ROPE_D) ** -0.5
RET_SCALE = DH ** -0.5
IN_WIDTH = 10048
N_CHIPS = 4
IN_SHARD = IN_WIDTH // N_CHIPS
ADAM_LR, ADAM_B1, ADAM_B2, ADAM_EPS, ADAM_WD, ADAM_STEP = 0.001, 0.9, 0.999, 1e-08, 0.01, 10

R_Q, R_K, R_V, Z_RET, Z_MLA, GATE0, GATE1 = 0, 1024, 2048, 3072, 4096, 5120, 5120 + D
C_Q = 5120 + 2 * D
C_KV = C_Q + Q_RANK
K_PE = C_KV + KV_RANK
PROJ_W = 10240
IN_RUNS = ((0, 832, C_Q), (832, 1856, Z_MLA), (1856, 4928, R_Q), (4928, 5952, Z_RET), (5952, IN_WIDTH, GATE0))

BIG_IN, BIG_OUT, BIG_BM, BIG_BR, BIG_UQ, BIG_UKV, BIG_JUNK, BIG_ROWS = 0, 10240, 12288, 13312, 14336, 14720, 14976, 15360
ZERO_ROWS = PROJ_W - IN_WIDTH
PK_IN, PK_OUT, PK_BM, PK_BR, PK_UQ, PK_UKV, PK_PAD, PK_ROWS = 0, 2512, 3024, 3280, 3536, 3632, 3696, 3712
PK_HALF = PK_ROWS // 2
SM_LOSS, SM_NORM, SM_QN, SM_KVN, SM_GNW, SM_GNB, SM_FIN, SM_META, SM_ROWS = 0, 8, 24, 32, 40, 48, 56, 72, 328

VMEM_LIMIT = 56 * 1024 * 1024


def _pieces(shard):
    j = shard
    out = [(PK_OUT, BIG_OUT + 512 * j, 512), (PK_BM, BIG_BM + 256 * j, 256), (PK_BR, BIG_BR + 256 * j, 256),
           (PK_UQ, BIG_UQ + 96 * j, 96), (PK_UKV, BIG_UKV + 64 * j, 64)]
    for lo, hi, new in IN_RUNS:
        a, b = max(lo, IN_SHARD * j), min(hi, IN_SHARD * (j + 1))
        if b > a:
            out.append((PK_IN + a - IN_SHARD * j, BIG_IN + new + a - lo, b - a))
    out += [(PK_PAD, BIG_JUNK + 16 * j, 16)]
    return out


AG_JUNK_REST, AG_JUNK_IN, AG_JUNK_END = BIG_JUNK, BIG_JUNK + 128, BIG_JUNK + 192
AG_REST_HALF, AG_IN_HALF = 608, 1264


def _ag_half(shard, w_in, half):
    pieces = [(b, n) for p, b, n in _pieces(shard) if p < PK_PAD and (p < PK_OUT) == w_in]
    pieces.append((AG_JUNK_IN + 16 * shard, 16) if w_in else (AG_JUNK_REST + 32 * shard, 32))
    size = AG_IN_HALF if w_in else AG_REST_HALF
    out, pos = [], 0
    for b, n in pieces:
        s, e = max(pos, half * size), min(pos + n, (half + 1) * size)
        if e > s:
            out.append((s, b + s - pos, e - s))
        pos += n
    assert pos == 2 * size
    return out


def _grad_half(shard, half):
    j = shard
    pieces = [(0, b - BIG_IN, p, n) for p, b, n in _pieces(j) if p < PK_OUT]
    pieces += [(1, 512 * j, PK_OUT, 512), (2, 256 * j, PK_BM, 256), (3, 256 * j, PK_BR, 256), (4, 96 * j, PK_UQ, 96),
               (5, 64 * j, PK_UKV, 64), (0, IN_WIDTH, PK_PAD, PK_ROWS - PK_PAD)]
    lo, hi = half * PK_HALF, (half + 1) * PK_HALF
    out = []
    for k, r0, p, n in pieces:
        s, e = max(p, lo), min(p + n, hi)
        if e > s:
            out.append((k, r0 + s - p, s - lo, e - s))
    assert sum(n for _, _, _, n in out) == PK_HALF
    return out


def _row_tile(rows, cap):
    best = BLK
    for t in range(BLK, cap + 1, BLK):
        if rows % t == 0:
            best = t
    return best


def _cparams(sem):
    return pltpu.CompilerParams(dimension_semantics=sem, vmem_limit_bytes=VMEM_LIMIT)


def _dot(a, b, form):
    dn = {"nt": (((1,), (1,)), ((), ())), "nn": (((1,), (0,)), ((), ())), "tn": (((0,), (0,)), ((), ()))}[form]
    return lax.dot_general(a, b, dn, preferred_element_type=F32)


def _sigmoid(v):
    return 1.0 / (1.0 + jnp.exp(-v))


def _matmul(a, b, form, m, n, k, tm, tn, tk, out_dtype, name, a_off=(0, 0), b_off=(0, 0), n_outer=False, after=None):
    nk = k // tk
    gi, gj = m // tm, n // tn

    def ij(g0, g1):
        return (g1, g0) if n_outer else (g0, g1)

    if form == "nt":
        a_spec = pl.BlockSpec((tm, tk), lambda g0, g1, kk: (ij(g0, g1)[0] + a_off[0], kk + a_off[1]))
        b_spec = pl.BlockSpec((tn, tk), lambda g0, g1, kk: (ij(g0, g1)[1] + b_off[0], kk + b_off[1]))
    elif form == "nn":
        a_spec = pl.BlockSpec((tm, tk), lambda g0, g1, kk: (ij(g0, g1)[0] + a_off[0], kk + a_off[1]))
        b_spec = pl.BlockSpec((tk, tn), lambda g0, g1, kk: (kk + b_off[0], ij(g0, g1)[1] + b_off[1]))
    else:
        a_spec = pl.BlockSpec((tk, tm), lambda g0, g1, kk: (kk + a_off[0], ij(g0, g1)[0] + a_off[1]))
        b_spec = pl.BlockSpec((tk, tn), lambda g0, g1, kk: (kk + b_off[0], ij(g0, g1)[1] + b_off[1]))
    o_spec = pl.BlockSpec((tm, tn), lambda g0, g1, kk: ij(g0, g1))

    def body(a_ref, b_ref, *rest):
        o_ref, *acc = rest[0 if after is None else 1:]
        p = _dot(a_ref[...], b_ref[...], form)
        if nk == 1:
            o_ref[...] = p.astype(o_ref.dtype)
        else:
            acc_ref, = acc
            kk = pl.program_id(2)

            @pl.when(kk == 0)
            def _():
                acc_ref[...] = p

            @pl.when(kk > 0)
            def _():
                acc_ref[...] += p

            @pl.when(kk == nk - 1)
            def _():
                o_ref[...] = acc_ref[...].astype(o_ref.dtype)

    extra = [] if after is None else [after]
    return pl.pallas_call(
        body, name=name, grid=(gj, gi, nk) if n_outer else (gi, gj, nk),
        in_specs=[a_spec, b_spec] + [pl.BlockSpec(memory_space=pl.ANY)] * len(extra), out_specs=o_spec,
        out_shape=jax.ShapeDtypeStruct((m, n), out_dtype),
        scratch_shapes=[] if nk == 1 else [pltpu.VMEM((tm, tn), F32)],
        compiler_params=_cparams(("parallel", "parallel", "arbitrary")),
    )(a, b, *extra)


def _rms_in(x, hp0, norm_w, s):
    def body(x_ref, hp0_ref, w_ref, h_ref, xn_ref, r_ref):
        def run(hv):
            r = lax.rsqrt(jnp.mean(hv * hv, axis=-1, keepdims=True) + NORM_EPS)
            h_ref[...] = hv
            xn_ref[...] = (hv * r * w_ref[...]).astype(BF16)
            r_ref[...] = r

        @pl.when(pl.program_id(0) == 0)
        def _():
            run(hp0_ref[...])

        @pl.when(pl.program_id(0) > 0)
        def _():
            run(x_ref[...])

    return pl.pallas_call(
        body, name="rms_in", grid=(s // BLK,),
        in_specs=[pl.BlockSpec((BLK, D), lambda i: (jnp.maximum(i - 1, 0), 0)),
                  pl.BlockSpec((BLK, D), lambda i: (0, 0)), pl.BlockSpec((1, D), lambda i: (0, 0))],
        out_specs=[pl.BlockSpec((BLK, D), lambda i: (i, 0)), pl.BlockSpec((BLK, D), lambda i: (i, 0)),
                   pl.BlockSpec((BLK, 1), lambda i: (i, 0))],
        out_shape=[jax.ShapeDtypeStruct((s, D), F32), jax.ShapeDtypeStruct((s, D), BF16),
                   jax.ShapeDtypeStruct((s, 1), F32)],
        compiler_params=_cparams(("arbitrary",)),
    )(x, hp0, norm_w)


def _rope64(t, c, sa, sb):
    return t * c + pltpu.roll(t, t.shape[1] - ROPE_D // 2, 1) * sa + pltpu.roll(t, ROPE_D // 2, 1) * sb


def _rope128(t, c, sg):
    return t * c + pltpu.roll(t, DH // 2, 1) * sg


HD = 2 * DH


def _mla_prep(proj, qn_w, kvn_w, wq, wukv, qtab, ktab, qbias, kbias, s):
    tm = _row_tile(s, 384)

    def body(cq_ref, ckv_ref, kpe_ref, qnw_ref, kvnw_ref, wq_ref, wukv_ref, qc_ref, qa_ref, qb_ref,
             kc_ref, ka_ref, kb_ref, qbias_ref, kbias_ref, cqn_ref, rq_ref, ckvn_ref, rkv_ref, qq_ref, kk_ref, vv_ref):
        cq = cq_ref[...].astype(F32)
        rq = lax.rsqrt(jnp.mean(cq * cq, axis=-1, keepdims=True) + NORM_EPS)
        cqn = (cq * rq * qnw_ref[...]).astype(BF16)
        ckv = ckv_ref[...].astype(F32)
        rkv = lax.rsqrt(jnp.mean(ckv * ckv, axis=-1, keepdims=True) + NORM_EPS)
        ckvn = (ckv * rkv * kvnw_ref[...]).astype(BF16)
        cqn_ref[...] = cqn
        rq_ref[...] = rq
        ckvn_ref[...] = ckvn
        rkv_ref[...] = rkv
        q = _dot(cqn, wq_ref[...], "nt")
        kv = _dot(ckvn, wukv_ref[...], "nt")
        kp = (_rope64(kpe_ref[...].astype(F32), kc_ref[...], ka_ref[...], kb_ref[...]) + kbias_ref[...]).astype(BF16)
        qc, qa, qb, qbias = qc_ref[...], qa_ref[...], qb_ref[...], qbias_ref[...]
        ones = jnp.ones((tm, DH), BF16)
        for h in range(HEADS):
            lo, mid, hi = h * HD, h * HD + DH, (h + 1) * HD
            qq_ref[:, lo:hi] = (_rope64(q[:, lo:hi], qc, qa, qb) + qbias).astype(BF16)
            kk_ref[:, lo:mid] = kv[:, lo:mid].astype(BF16)
            kk_ref[:, mid:hi] = kp
            vv_ref[:, lo:mid] = kv[:, mid:hi].astype(BF16)
            vv_ref[:, mid:hi] = ones

    row = lambda w, cb: pl.BlockSpec((tm, w), lambda i: (i, cb))
    full = lambda a: pl.BlockSpec(a.shape, lambda i: (0, 0))
    wide = jax.ShapeDtypeStruct((s, HEADS * HD), BF16)
    return pl.pallas_call(
        body, name="mla_prep", grid=(s // tm,),
        in_specs=[row(Q_RANK, C_Q // Q_RANK), row(KV_RANK, C_KV // KV_RANK), row(DH, K_PE // DH),
                  full(qn_w), full(kvn_w), full(wq), full(wukv), row(HD, 0), row(HD, 0), row(HD, 0),
                  row(DH, 0), row(DH, 0), row(DH, 0), full(qbias), row(DH, 0)],
        out_specs=[row(Q_RANK, 0), row(1, 0), row(KV_RANK, 0), row(1, 0), row(HEADS * HD, 0), row(HEADS * HD, 0),
                   row(HEADS * HD, 0)],
        out_shape=[jax.ShapeDtypeStruct((s, Q_RANK), BF16), jax.ShapeDtypeStruct((s, 1), F32),
                   jax.ShapeDtypeStruct((s, KV_RANK), BF16), jax.ShapeDtypeStruct((s, 1), F32), wide, wide, wide],
        compiler_params=_cparams(("parallel",)),
    )(proj, proj, proj, qn_w, kvn_w, wq, wukv, *qtab, *ktab, qbias, kbias)


def _diag_mask(t):
    return lax.broadcasted_iota(jnp.int32, (t, t), 0) <= lax.broadcasted_iota(jnp.int32, (t, t), 1)


def _silu(z):
    return z * _sigmoid(z)


def _attn_fwd(qq, kk, vv, proj, s):
    t = _row_tile(s, 384)
    n = s // t

    def body(q_ref, k_ref, v_ref, z_ref, o_ref, y_ref, lse_ref, acc_ref, m_ref):
        qi = pl.program_id(1)
        q = q_ref[...]
        m_ref[...] = jnp.full(m_ref.shape, NEG_INF, F32)
        acc_ref[...] = jnp.zeros(acc_ref.shape, F32)

        def keys(ki):
            return pl.ds(pl.multiple_of(ki * t, t), t)

        def scores(ki):
            return _dot(k_ref[keys(ki), :], q, "nt")

        def tile(ki, st):
            m_old = m_ref[...]
            m_new = jnp.maximum(m_old, jnp.max(st, axis=0, keepdims=True))
            pt = jnp.exp(st - m_new).astype(BF16)
            acc_ref[...] = jnp.exp(m_old - m_new) * acc_ref[...] + _dot(v_ref[keys(ki), :], pt, "tn")
            m_ref[...] = m_new

        def inner(ki, st):
            st_after = scores(ki + 1)
            tile(ki, st)
            return st_after

        st_diag = lax.fori_loop(0, qi, inner, scores(0))
        tile(qi, jnp.where(_diag_mask(t), st_diag, NEG_INF))

        l = acc_ref[DH:DH + 1, :]
        o = (acc_ref[:DH, :] / l).T
        o_ref[...] = o.astype(BF16)
        y_ref[...] = (o * _silu(z_ref[...].astype(F32))).astype(BF16)
        lse_ref[0, 0] = m_ref[...] + jnp.log(l)

    qtile = pl.BlockSpec((t, DH), lambda h, i: (i, h))
    head = pl.BlockSpec((s, HD), lambda h, i: (0, h))
    return pl.pallas_call(
        body, name="attn_fwd", grid=(HEADS, s // t),
        in_specs=[pl.BlockSpec((t, HD), lambda h, i: (i, h)), head, head,
                  pl.BlockSpec((t, DH), lambda h, i: (i, Z_MLA // DH + h))],
        out_specs=[qtile, qtile, pl.BlockSpec((1, 1, 1, t), lambda h, i: (h, i, 0, 0))],
        out_shape=[jax.ShapeDtypeStruct((s, WIDTH), BF16), jax.ShapeDtypeStruct((s, WIDTH), BF16),
                   jax.ShapeDtypeStruct((HEADS, s // t, 1, t), F32)],
        scratch_shapes=[pltpu.VMEM((HD, t), F32), pltpu.VMEM((1, t), F32)],
        compiler_params=_cparams(("parallel", "arbitrary")),
    )(qq, kk, vv, proj)


def _ret_consts():
    log_g = np.log1p(-(2.0 ** (-5.0 - np.arange(HEADS, dtype=np.float64))))
    n = np.arange(BLK, dtype=np.float64)
    diff = n[:, None] - n[None, :]
    decay = np.where(diff >= 0, np.exp(log_g[:, None, None] * np.maximum(diff, 0.0)), 0.0)
    zeta = np.exp(log_g[:, None] * (BLK - 1.0 - n))[:, :, None]
    xi = np.exp(log_g[:, None] * (n + 1.0))[:, :, None]
    gamma = [float(np.float32(np.exp(g * BLK))) for g in log_g]
    return (jnp.asarray(decay, F32), jnp.asarray(zeta, F32), jnp.asarray(xi, F32)), gamma


def _ret_fwd(proj, gn_w, gn_b, c128, s128, consts, gamma, s):
    nb = s // BLK
    decay, zeta, xi = consts

    def body(rq_ref, rk_ref, rv_ref, z_ref, gw_ref, gb_ref, c_ref, s_ref, dm_ref, zt_ref, xi_ref,
             y_ref, on_ref, rstd_ref, qr_ref, kr_ref, rall_ref, state):
        @pl.when(pl.program_id(0) == 0)
        def _():
            state[...] = jnp.zeros_like(state)

        c, sg = c_ref[...], s_ref[...]
        for h in range(HEADS):
            sl = slice(h * DH, (h + 1) * DH)
            q = _rope128(rq_ref[:, sl].astype(F32), c, sg).astype(BF16)
            kf = _rope128(rk_ref[:, sl].astype(F32), c, sg) * RET_SCALE
            k = kf.astype(BF16)
            v = rv_ref[:, sl]
            qr_ref[:, sl] = q
            kr_ref[:, sl] = k
            r_prev = state[h]
            rall_ref[0, h] = r_prev
            a = _dot(q, k, "nt") * dm_ref[h]
            o = _dot(a.astype(BF16), v, "nn") + _dot(q, r_prev.astype(BF16), "nn") * xi_ref[h]
            state[h] = r_prev * gamma[h] + _dot((kf * zt_ref[h]).astype(BF16), v, "tn")
            mu = jnp.mean(o, axis=-1, keepdims=True)
            var = jnp.mean(jnp.square(o - mu), axis=-1, keepdims=True)
            rstd = lax.rsqrt(var + GN_EPS)
            on = (o - mu) * rstd
            rstd_ref[h] = rstd
            on_ref[:, sl] = on.astype(BF16)
            ogn = on * gw_ref[:, sl] + gb_ref[:, sl]
            y_ref[:, sl] = (ogn * _silu(z_ref[:, sl].astype(F32))).astype(BF16)

    seg = lambda cb: pl.BlockSpec((BLK, WIDTH), lambda i: (i, cb))
    full = lambda a: pl.BlockSpec(a.shape, lambda i: (0,) * a.ndim)
    tab = pl.BlockSpec((BLK, DH), lambda i: (i, 0))
    return pl.pallas_call(
        body, name="ret_fwd", grid=(nb,),
        in_specs=[seg(R_Q // WIDTH), seg(R_K // WIDTH), seg(R_V // WIDTH), seg(Z_RET // WIDTH), full(gn_w), full(gn_b),
                  tab, tab, full(decay), full(zeta), full(xi)],
        out_specs=[seg(0), seg(0), pl.BlockSpec((HEADS, BLK, 1), lambda i: (0, i, 0)), seg(0), seg(0),
                   pl.BlockSpec((1, HEADS, DH, DH), lambda i: (i, 0, 0, 0))],
        out_shape=[jax.ShapeDtypeStruct((s, WIDTH), BF16), jax.ShapeDtypeStruct((s, WIDTH), BF16),
                   jax.ShapeDtypeStruct((HEADS, s, 1), F32), jax.ShapeDtypeStruct((s, WIDTH), BF16),
                   jax.ShapeDtypeStruct((s, WIDTH), BF16), jax.ShapeDtypeStruct((nb, HEADS, DH, DH), F32)],
        scratch_shapes=[pltpu.VMEM((HEADS, DH, DH), F32)],
        compiler_params=_cparams(("arbitrary",)),
    )(proj, proj, proj, proj, gn_w, gn_b, c128, s128, decay, zeta, xi)


def _merge_fwd(y_mla, y_ret, wbm, wbr, proj, s):
    tm, tn = _row_tile(s, 1408), 512

    def body(ym_ref, yr_ref, wm_ref, wr_ref, g0_ref, g1_ref, um_ref, ur_ref, mg_ref):
        um = _dot(ym_ref[...], wm_ref[...], "nt")
        ur = _dot(yr_ref[...], wr_ref[...], "nt")
        um_ref[...] = um.astype(BF16)
        ur_ref[...] = ur.astype(BF16)
        mg_ref[...] = (_sigmoid(g0_ref[...].astype(F32)) * um + _sigmoid(g1_ref[...].astype(F32)) * ur).astype(BF16)

    yspec = pl.BlockSpec((tm, WIDTH), lambda i, j: (i, 0))
    wspec = pl.BlockSpec((tn, WIDTH), lambda i, j: (j, 0))
    ospec = pl.BlockSpec((tm, tn), lambda i, j: (i, j))
    return pl.pallas_call(
        body, name="merge_fwd", grid=(s // tm, D // tn),
        in_specs=[yspec, yspec, wspec, wspec, pl.BlockSpec((tm, tn), lambda i, j: (i, GATE0 // tn + j)),
                  pl.BlockSpec((tm, tn), lambda i, j: (i, GATE1 // tn + j))],
        out_specs=[ospec, ospec, ospec],
        out_shape=[jax.ShapeDtypeStruct((s, D), BF16)] * 3,
        compiler_params=_cparams(("parallel", "parallel")),
    )(y_mla, y_ret, wbm, wbr, proj, proj)


def _out_fwd(merged, big, h, s):
    tm, tn = _row_tile(s, 1408), 512

    def body(m_ref, w_ref, h_ref, o_ref):
        o_ref[...] = h_ref[...] + _dot(m_ref[...], w_ref[...], "nn")

    return pl.pallas_call(
        body, name="out_fwd", grid=(s // tm, D // tn),
        in_specs=[pl.BlockSpec((tm, D), lambda i, j: (i, 0)), pl.BlockSpec((D, tn), lambda i, j: (BIG_OUT // D, j)),
                  pl.BlockSpec((tm, tn), lambda i, j: (i, j))],
        out_specs=pl.BlockSpec((tm, tn), lambda i, j: (i, j)),
        out_shape=jax.ShapeDtypeStruct((s, D), F32),
        compiler_params=_cparams(("parallel", "parallel")),
    )(merged, big, h)


def _loss_bwd(h2, target, fin_w, s):
    nb = s // BLK

    def body(h2_ref, t_ref, w_ref, dh_ref, dhb_ref, loss_ref, dw_ref):
        i = pl.program_id(0)

        @pl.when(i == 0)
        def _():
            dh_ref[...] = jnp.zeros_like(dh_ref)
            dhb_ref[...] = jnp.zeros_like(dhb_ref)
            loss_ref[...] = jnp.zeros_like(loss_ref)
            dw_ref[...] = jnp.zeros_like(dw_ref)

        @pl.when(i > 0)
        def _():
            hv = h2_ref[...]
            w = w_ref[...]
            r = lax.rsqrt(jnp.mean(hv * hv, axis=-1, keepdims=True) + NORM_EPS)
            nrm = hv * r
            e = nrm * w - t_ref[...]
            loss_ref[...] += jnp.full(loss_ref.shape, 0.5 / D, F32) * jnp.sum(e * e)
            dy = e * (1.0 / D)
            dw_ref[...] += jnp.sum(dy * nrm, axis=0, keepdims=True)
            g = dy * w
            dh = r * (g - nrm * jnp.mean(g * nrm, axis=-1, keepdims=True))
            dh_ref[...] = dh
            dhb_ref[...] = dh.astype(BF16)

    blk = pl.BlockSpec((BLK, D), lambda i: (i, 0))
    return pl.pallas_call(
        body, name="loss_bwd", grid=(nb,),
        in_specs=[blk, pl.BlockSpec((BLK, D), lambda i: (jnp.maximum(i - 1, 0), 0)), pl.BlockSpec((1, D), lambda i: (0, 0))],
        out_specs=[blk, blk, pl.BlockSpec((8, 128), lambda i: (0, 0)), pl.BlockSpec((1, D), lambda i: (0, 0))],
        out_shape=[jax.ShapeDtypeStruct((s, D), F32), jax.ShapeDtypeStruct((s, D), BF16),
                   jax.ShapeDtypeStruct((8, 128), F32), jax.ShapeDtypeStruct((1, D), F32)],
        compiler_params=_cparams(("arbitrary",)),
    )(h2, target, fin_w)


def _merge_bwd(dh2b, big, u_mla, u_ret, proj, s):
    tm, tn = _row_tile(s, 1408), 512

    def body(d_ref, w_ref, um_ref, ur_ref, gate_ref, dproj_ref, dum_ref, dur_ref, dm_ref):
        branch = pl.program_id(2)

        @pl.when(branch == 0)
        def _():
            dm_ref[...] = _dot(d_ref[...], w_ref[...], "nt")

        dm = dm_ref[...]
        gt = _sigmoid(gate_ref[...].astype(F32))

        @pl.when(branch == 0)
        def _():
            dproj_ref[...] = (dm * um_ref[...].astype(F32) * gt * (1.0 - gt)).astype(BF16)
            dum_ref[...] = (dm * gt).astype(BF16)

        @pl.when(branch == 1)
        def _():
            dproj_ref[...] = (dm * ur_ref[...].astype(F32) * gt * (1.0 - gt)).astype(BF16)
            dur_ref[...] = (dm * gt).astype(BF16)

    ospec = pl.BlockSpec((tm, tn), lambda i, j, b: (i, j))
    gate = pl.BlockSpec((tm, tn), lambda i, j, b: (i, GATE0 // tn + b * (D // tn) + j))
    return pl.pallas_call(
        body, name="merge_bwd", grid=(s // tm, D // tn, 2),
        in_specs=[pl.BlockSpec((tm, D), lambda i, j, b: (i, 0)),
                  pl.BlockSpec((tn, D), lambda i, j, b: (BIG_OUT // tn + j, 0)), ospec, ospec, gate],
        out_specs=[gate, ospec, ospec],
        out_shape=[jax.ShapeDtypeStruct((s, PROJ_W), BF16), jax.ShapeDtypeStruct((s, D), BF16),
                   jax.ShapeDtypeStruct((s, D), BF16)],
        scratch_shapes=[pltpu.VMEM((tm, tn), F32)],
        compiler_params=_cparams(("parallel", "parallel", "arbitrary")),
    )(dh2b, big, u_mla, u_ret, proj)


def _dsilu(z):
    sg = _sigmoid(z)
    return sg * (1.0 + z * (1.0 - sg))


def _branch_mla_bwd(dproj, du, wbm, o_mla, proj, s):
    tm = _row_tile(s, 384)

    def body(dproj_in, du_ref, w_ref, o_ref, z_ref, dz_ref, do_ref, delta_ref):
        del dproj_in
        dy = _dot(du_ref[...], w_ref[...], "nn")
        z = z_ref[...].astype(F32)
        o = o_ref[...].astype(F32)
        do = dy * _silu(z)
        do_ref[...] = do.astype(BF16)
        dz_ref[...] = (dy * o * _dsilu(z)).astype(BF16)
        prod = do * o
        for h in range(HEADS):
            delta_ref[h] = jnp.sum(prod[:, h * DH:(h + 1) * DH], axis=-1, keepdims=True)

    row = lambda w, cb: pl.BlockSpec((tm, w), lambda i: (i, cb))
    return pl.pallas_call(
        body, name="branch_mla_bwd", grid=(s // tm,),
        in_specs=[ANY, row(D, 0), pl.BlockSpec((D, WIDTH), lambda i: (0, 0)), row(WIDTH, 0),
                  row(WIDTH, Z_MLA // WIDTH)],
        out_specs=[row(WIDTH, Z_MLA // WIDTH), row(WIDTH, 0), pl.BlockSpec((HEADS, tm, 1), lambda i: (0, i, 0))],
        out_shape=[jax.ShapeDtypeStruct((s, PROJ_W), BF16), jax.ShapeDtypeStruct((s, WIDTH), BF16),
                   jax.ShapeDtypeStruct((HEADS, s, 1), F32)],
        input_output_aliases={0: 0},
        compiler_params=_cparams(("parallel",)),
    )(dproj, du, wbm, o_mla, proj)


def _branch_ret_bwd(dproj, du, wbr, on, rstd, gn_w, gn_b, proj, s):
    tm = BLK

    def body(dproj_in, du_ref, w_ref, on_ref, rstd_ref, gw_ref, gb_ref, z_ref, dz_ref, do_ref, dgw_ref, dgb_ref):
        del dproj_in

        @pl.when(pl.program_id(0) == 0)
        def _():
            dgw_ref[...] = jnp.zeros_like(dgw_ref)
            dgb_ref[...] = jnp.zeros_like(dgb_ref)

        dy = _dot(du_ref[...], w_ref[...], "nn")
        z = z_ref[...].astype(F32)
        on = on_ref[...].astype(F32)
        gw = gw_ref[...]
        dogn = dy * _silu(z)
        dz_ref[...] = (dy * (on * gw + gb_ref[...]) * _dsilu(z)).astype(BF16)
        dgw_ref[...] += jnp.sum(dogn * on, axis=0, keepdims=True)
        dgb_ref[...] += jnp.sum(dogn, axis=0, keepdims=True)
        don = dogn * gw
        for h in range(HEADS):
            sl = slice(h * DH, (h + 1) * DH)
            dn, nh = don[:, sl], on[:, sl]
            do = rstd_ref[h] * (dn - jnp.mean(dn, axis=-1, keepdims=True)
                                - nh * jnp.mean(dn * nh, axis=-1, keepdims=True))
            do_ref[:, sl] = do.astype(BF16)

    row = lambda w, cb: pl.BlockSpec((tm, w), lambda i: (i, cb))
    vec = pl.BlockSpec((1, WIDTH), lambda i: (0, 0))
    return pl.pallas_call(
        body, name="branch_ret_bwd", grid=(s // tm,),
        in_specs=[ANY, row(D, 0), pl.BlockSpec((D, WIDTH), lambda i: (0, 0)), row(WIDTH, 0),
                  pl.BlockSpec((HEADS, tm, 1), lambda i: (0, i, 0)), vec, vec, row(WIDTH, Z_RET // WIDTH)],
        out_specs=[row(WIDTH, Z_RET // WIDTH), row(WIDTH, 0), vec, vec],
        out_shape=[jax.ShapeDtypeStruct((s, PROJ_W), BF16), jax.ShapeDtypeStruct((s, WIDTH), BF16)]
        + [jax.ShapeDtypeStruct((1, WIDTH), F32)] * 2,
        input_output_aliases={0: 0},
        compiler_params=_cparams(("arbitrary",)),
    )(dproj, du, wbr, on, rstd, gn_w, gn_b, proj)


def _ret_bwd(dproj, qr, kr, proj, rall, do_ret, c128, s128, consts, gamma, s):
    nb = s // BLK
    decay, zeta, xi = consts

    def body(dproj_in, q_ref, k_ref, v_ref, r_ref, do_ref, c_ref, s_ref, dm_ref, zt_ref, xi_ref, out_ref, gstate):
        del dproj_in
        dq_ref, dk_ref, dv_ref = (out_ref.at[:, pl.ds(off, WIDTH)] for off in (R_Q, R_K, R_V))

        @pl.when(pl.program_id(0) == 0)
        def _():
            gstate[...] = jnp.zeros_like(gstate)

        c, sg = c_ref[...], s_ref[...]
        for h in range(HEADS):
            sl = slice(h * DH, (h + 1) * DH)
            q, k, v, do = q_ref[:, sl], k_ref[:, sl], v_ref[:, sl], do_ref[:, sl]
            dm = dm_ref[h]
            g_next = gstate[h]
            gb = g_next.astype(BF16)
            a = (_dot(q, k, "nt") * dm).astype(BF16)
            da = (_dot(do, v, "nt") * dm).astype(BF16)
            dox = (do.astype(F32) * xi_ref[h]).astype(BF16)
            dq = _dot(da, k, "nn") + _dot(dox, r_ref[0, h].astype(BF16), "nt")
            dk = _dot(da, q, "tn") + _dot(v, gb, "nt") * zt_ref[h]
            kz = (k.astype(F32) * zt_ref[h]).astype(BF16)
            dv = _dot(a, do, "tn") + _dot(kz, gb, "nn")
            gstate[h] = g_next * gamma[h] + _dot(q, dox, "tn")
            dk = dk * RET_SCALE
            dq_ref[:, sl] = _rope128(dq, c, -sg).astype(BF16)
            dk_ref[:, sl] = _rope128(dk, c, -sg).astype(BF16)
            dv_ref[:, sl] = dv.astype(BF16)

    rev = lambda cb: pl.BlockSpec((BLK, WIDTH), lambda i: (nb - 1 - i, cb))
    full = lambda a: pl.BlockSpec(a.shape, lambda i: (0,) * a.ndim)
    tab = pl.BlockSpec((BLK, DH), lambda i: (nb - 1 - i, 0))
    return pl.pallas_call(
        body, name="ret_bwd", grid=(nb,),
        in_specs=[ANY, rev(0), rev(0), rev(R_V // WIDTH),
                  pl.BlockSpec((1, HEADS, DH, DH), lambda i: (nb - 1 - i, 0, 0, 0)),
                  rev(0), tab, tab, full(decay), full(zeta), full(xi)],
        out_specs=pl.BlockSpec((BLK, 3 * WIDTH), lambda i: (nb - 1 - i, R_Q // (3 * WIDTH))),
        out_shape=jax.ShapeDtypeStruct((s, PROJ_W), BF16),
        scratch_shapes=[pltpu.VMEM((HEADS, DH, DH), F32)],
        input_output_aliases={0: 0},
        compiler_params=_cparams(("arbitrary",)),
    )(dproj, qr, kr, proj, rall, do_ret, c128, s128, decay, zeta, xi)


def _attn_bwd(qq, kk, vv, do, lse, delta, s):
    t = _row_tile(s, 384)
    n = s // t

    def body(q_ref, k_ref, v_ref, do_ref, lse_ref, delta_ref, dq_ref, dk_ref, dv_ref, dk_acc, dv_acc):
        ki = pl.program_id(1)

        @pl.when(ki == 0)
        def _():
            dq_ref[...] = jnp.zeros(dq_ref.shape, F32)

        k, v = k_ref[...], v_ref[...]
        dk_acc[...] = jnp.zeros(dk_acc.shape, F32)
        dv_acc[...] = jnp.zeros(dv_acc.shape, F32)

        def rows(qi):
            return pl.ds(pl.multiple_of(qi * t, t), t)

        def products(qi):
            return _dot(k, q_ref[rows(qi), :], "nt"), _dot(v, do_ref[rows(qi), :], "nt")

        def tile(qi, st, dpt):
            q, dov = q_ref[rows(qi), :], do_ref[rows(qi), :]
            pt = jnp.exp(st - lse_ref[0, qi])
            dv_acc[...] += _dot(pt.astype(BF16), dov, "nn")
            dst = (pt * (dpt - delta_ref[0, qi])).astype(BF16)
            dk_acc[...] += _dot(dst, q, "nn")
            dq_ref[rows(qi), :] += _dot(dst, k, "tn")

        def inner(qi, carry):
            after = products(jnp.minimum(qi + 1, n - 1))
            tile(qi, *carry)
            return after

        st, dpt = products(ki)
        ahead = products(jnp.minimum(ki + 1, n - 1))
        tile(ki, jnp.where(_diag_mask(t), st, NEG_INF), dpt)
        lax.fori_loop(ki + 1, n, inner, ahead)

        dk_ref[...] = dk_acc[...].astype(BF16)
        dv_ref[...] = dv_acc[...].astype(BF16)

    head = lambda w: pl.BlockSpec((s, w), lambda h, i: (0, h))
    stat = pl.BlockSpec((1, n, 1, t), lambda h, i: (h, 0, 0, 0))
    return pl.pallas_call(
        body, name="attn_bwd", grid=(HEADS, n),
        in_specs=[head(HD), pl.BlockSpec((t, HD), lambda h, i: (i, h)), pl.BlockSpec((t, DH), lambda h, i: (i, 2 * h)),
                  head(DH), stat, stat],
        out_specs=[head(HD), pl.BlockSpec((t, HD), lambda h, i: (i, h)), pl.BlockSpec((t, DH), lambda h, i: (i, h))],
        out_shape=[jax.ShapeDtypeStruct((s, HEADS * HD), F32), jax.ShapeDtypeStruct((s, HEADS * HD), BF16),
                   jax.ShapeDtypeStruct((s, WIDTH), BF16)],
        scratch_shapes=[pltpu.VMEM((t, HD), F32), pltpu.VMEM((t, DH), F32)],
        compiler_params=_cparams(("parallel", "arbitrary")),
    )(qq, kk, vv, do, lse, delta)


def _rms_bwd(dn, nhat, r, w):
    g = dn * w
    return r * (g - nhat * jnp.mean(g * nhat, axis=-1, keepdims=True)), dn * nhat


def _mla_prep_bwd(dproj, dqq, dkk, dvv, proj, rq, rkv, qn_w, kvn_w, wq, wukv, qtab, ktab, s):
    tm = _row_tile(s, 384)
    tail = PROJ_W - C_Q

    def body(dproj_in, dqq_ref, dkk_ref, dvv_ref, cq_ref, ckv_ref, rq_ref, rkv_ref, qnw_ref, kvnw_ref, wq_ref,
             wukv_ref, qc_ref, qa_ref, qb_ref, kc_ref, ka_ref, kb_ref,
             tail_ref, dq_ref, dkv_ref, dqnw_ref, dkvnw_ref):
        del dproj_in
        dcq_ref = tail_ref.at[:, pl.ds(0, Q_RANK)]
        dckv_ref = tail_ref.at[:, pl.ds(C_KV - C_Q, KV_RANK)]
        dkpe_ref = tail_ref.at[:, pl.ds(K_PE - C_Q, 2 * DH)]

        @pl.when(pl.program_id(0) == 0)
        def _():
            dqnw_ref[...] = jnp.zeros_like(dqnw_ref)
            dkvnw_ref[...] = jnp.zeros_like(dkvnw_ref)

        qc, qa, qb = qc_ref[...], qa_ref[...], qb_ref[...]
        dkp = jnp.zeros((tm, DH), F32)
        for h in range(HEADS):
            lo, mid, hi = h * HD, h * HD + DH, (h + 1) * HD
            dq_ref[:, lo:hi] = _rope64(dqq_ref[:, lo:hi], qc, -qa, -qb).astype(BF16)
            dkv_ref[:, lo:mid] = dkk_ref[:, lo:mid]
            dkv_ref[:, mid:hi] = dvv_ref[:, h * DH:(h + 1) * DH]
            dkp = dkp + dkk_ref[:, mid:hi].astype(F32)
        dcqn = _dot(dq_ref[...], wq_ref[...], "nn")
        rq_v = rq_ref[...]
        dcq, prod = _rms_bwd(dcqn, cq_ref[...].astype(F32) * rq_v, rq_v, qnw_ref[...])
        dcq_ref[...] = dcq.astype(BF16)
        dqnw_ref[...] += jnp.sum(prod, axis=0, keepdims=True)
        dckvn = _dot(dkv_ref[...], wukv_ref[...], "nn")
        rkv_v = rkv_ref[...]
        dckv, prod = _rms_bwd(dckvn, ckv_ref[...].astype(F32) * rkv_v, rkv_v, kvnw_ref[...])
        dckv_ref[...] = dckv.astype(BF16)
        dkvnw_ref[...] += jnp.sum(prod, axis=0, keepdims=True)
        dkpe_ref[:, :DH] = _rope64(dkp, kc_ref[...], -ka_ref[...], -kb_ref[...]).astype(BF16)
        dkpe_ref[:, DH:] = jnp.zeros((tm, DH), BF16)

    row = lambda w, cb: pl.BlockSpec((tm, w), lambda i: (i, cb))
    full = lambda a: pl.BlockSpec(a.shape, lambda i: (0, 0))
    wide = jax.ShapeDtypeStruct((s, HEADS * HD), BF16)
    return pl.pallas_call(
        body, name="mla_prep_bwd", grid=(s // tm,),
        in_specs=[ANY, row(HEADS * HD, 0), row(HEADS * HD, 0), row(WIDTH, 0),
                  row(Q_RANK, C_Q // Q_RANK), row(KV_RANK, C_KV // KV_RANK), row(1, 0), row(1, 0),
                  full(qn_w), full(kvn_w), full(wq), full(wukv), row(HD, 0), row(HD, 0), row(HD, 0),
                  row(DH, 0), row(DH, 0), row(DH, 0)],
        out_specs=[row(tail, C_Q // tail), row(HEADS * HD, 0), row(HEADS * HD, 0),
                   pl.BlockSpec((1, Q_RANK), lambda i: (0, 0)), pl.BlockSpec((1, KV_RANK), lambda i: (0, 0))],
        out_shape=[jax.ShapeDtypeStruct((s, PROJ_W), BF16), wide, wide,
                   jax.ShapeDtypeStruct((1, Q_RANK), F32), jax.ShapeDtypeStruct((1, KV_RANK), F32)],
        input_output_aliases={0: 0},
        compiler_params=_cparams(("arbitrary",)),
    )(dproj, dqq, dkk, dvv, proj, proj, rq, rkv, qn_w, kvn_w, wq, wukv, *qtab, *ktab)


def _rms_in_bwd(dxn, h, r, dh2, norm_w, s):
    def body(dxn_ref, h_ref, r_ref, dh2_ref, w_ref, gx_ref, gm_ref, dw_ref):
        i = pl.program_id(0)
        r_v = r_ref[...]
        dx, prod = _rms_bwd(dxn_ref[...], h_ref[...] * r_v, r_v, w_ref[...])
        dh = dh2_ref[...] + dx

        @pl.when(i == 0)
        def _():
            gm_ref[...] = dh
            dw_ref[...] = jnp.sum(prod, axis=0, keepdims=True)

        @pl.when(i > 0)
        def _():
            gx_ref[...] = dh
            dw_ref[...] += jnp.sum(prod, axis=0, keepdims=True)

    blk = pl.BlockSpec((BLK, D), lambda i: (i, 0))
    return pl.pallas_call(
        body, name="rms_in_bwd", grid=(s // BLK,),
        in_specs=[blk, blk, pl.BlockSpec((BLK, 1), lambda i: (i, 0)), blk, pl.BlockSpec((1, D), lambda i: (0, 0))],
        out_specs=[pl.BlockSpec((BLK, D), lambda i: (jnp.maximum(i - 1, 0), 0)), pl.BlockSpec((BLK, D), lambda i: (0, 0)),
                   pl.BlockSpec((1, D), lambda i: (0, 0))],
        out_shape=[jax.ShapeDtypeStruct((s - BLK, D), F32), jax.ShapeDtypeStruct((BLK, D), F32),
                   jax.ShapeDtypeStruct((1, D), F32)],
        compiler_params=_cparams(("arbitrary",)),
    )(dxn, h, r, dh2, norm_w)


def _adam_math(w, g, m, v):
    m = ADAM_B1 * m + (1.0 - ADAM_B1) * g
    v = ADAM_B2 * v + (1.0 - ADAM_B2) * (g * g)
    m_hat = m / (1.0 - ADAM_B1 ** ADAM_STEP)
    v_hat = v / (1.0 - ADAM_B2 ** ADAM_STEP)
    return -ADAM_LR * (m_hat / (jnp.sqrt(v_hat) + ADAM_EPS) + ADAM_WD * w), m, v


def _adamw(w, g, m, v, name):
    rows, cols = w.shape
    tr = rows
    for cand in (128, 64, 32, 16, 8):
        if rows % cand == 0:
            tr = cand
            break

    def body(w_ref, g_ref, m_ref, v_ref, d_ref, nm_ref, nv_ref):
        d_ref[...], nm_ref[...], nv_ref[...] = _adam_math(w_ref[...], g_ref[...], m_ref[...], v_ref[...])

    spec = pl.BlockSpec((tr, cols), lambda i: (i, 0))
    return pl.pallas_call(
        body, name=name, grid=(rows // tr,), in_specs=[spec] * 4, out_specs=[spec] * 3,
        out_shape=[jax.ShapeDtypeStruct((rows, cols), F32)] * 3,
        compiler_params=_cparams(("parallel",)),
    )(w, g, m, v)


def _adamw_t(w_t, g_t, m_t, v_t, name):
    cols, rows = w_t.shape

    def body(w_ref, g_ref, m_ref, v_ref, go_ref, d_ref, nm_ref, nv_ref):
        g = g_ref[...]
        go_ref[...] = g
        d_ref[...], nm_ref[...], nv_ref[...] = _adam_math(w_ref[...], g, m_ref[...], v_ref[...])

    blk = pl.BlockSpec((128, rows), lambda k: (k, 0))
    return pl.pallas_call(
        body, name=name, grid=(pl.cdiv(cols, 128),), in_specs=[blk] * 4, out_specs=[blk] * 4,
        out_shape=[jax.ShapeDtypeStruct((cols, rows), F32)] * 4, compiler_params=_cparams(("parallel",)),
    )(w_t, g_t, m_t, v_t)


def _adamw_small(w, gall, m, v):
    def body(w_ref, g_ref, m_ref, v_ref, gs_ref, d_ref, nm_ref, nv_ref):
        g = g_ref[0]
        for dev in range(1, 8):
            g = g + g_ref[dev]
        gs_ref[...] = g
        d_ref[...], nm_ref[...], nv_ref[...] = _adam_math(w_ref[...], g, m_ref[...], v_ref[...])

    return pl.pallas_call(
        body, name="adamw_small", out_shape=[jax.ShapeDtypeStruct((SM_ROWS, 128), F32)] * 4,
        compiler_params=pltpu.CompilerParams(vmem_limit_bytes=VMEM_LIMIT),
    )(w, gall, m, v)


ADD_ROWS = 464
HALF_BLOCKS = PK_HALF // ADD_ROWS


def _rs_core_add(srcs, land, pos):
    def body(pos_ref, *refs):
        src_refs, (land_ref, out_ref, own, sem) = refs[:len(srcs)], refs[len(srcs):]
        for s in range(N_CHIPS):
            for hf in range(2):
                @pl.when((pl.program_id(0) == s) & (pos_ref[1] == hf))
                def _():
                    for k, r0, p, n in _grad_half(s, hf):
                        pltpu.make_async_copy(src_refs[k].at[pl.ds(r0, n), :], own.at[pl.ds(p, n), :], sem).start()
        pltpu.make_async_copy(own, own, sem).wait()
        out_ref[0] = (own[...].astype(F32) + land_ref[0].astype(F32)).astype(BF16)

    blk = pl.BlockSpec((1, PK_HALF, D), lambda s, pos_ref: (s, 0, 0))
    return pl.pallas_call(
        body, name="rs_core_add",
        grid_spec=pltpu.PrefetchScalarGridSpec(
            num_scalar_prefetch=1, grid=(N_CHIPS,), in_specs=[ANY] * len(srcs) + [blk], out_specs=blk,
            scratch_shapes=[pltpu.VMEM((PK_HALF, D), BF16), pltpu.SemaphoreType.DMA(())]),
        out_shape=jax.ShapeDtypeStruct((N_CHIPS, PK_HALF, D), BF16),
        compiler_params=_cparams(("arbitrary",)),
    )(pos, *srcs, land)


def _rs_chip_add(cp, land, pos):
    def body(pos_ref, a_ref, l_ref, o_ref):
        o_ref[0] = ((a_ref[0].astype(F32) + l_ref[0].astype(F32)) + l_ref[1].astype(F32)) + l_ref[2].astype(F32)

    return pl.pallas_call(
        body, name="rs_chip_add",
        grid_spec=pltpu.PrefetchScalarGridSpec(
            num_scalar_prefetch=1, grid=(HALF_BLOCKS,),
            in_specs=[pl.BlockSpec((1, ADD_ROWS, D), lambda i, pos_ref: (pos_ref[0], i, 0)),
                      pl.BlockSpec((3, ADD_ROWS, D), lambda i, pos_ref: (0, i, 0))],
            out_specs=pl.BlockSpec((1, ADD_ROWS, D), lambda i, pos_ref: (pos_ref[1], i, 0))),
        out_shape=jax.ShapeDtypeStruct((2, PK_HALF, D), F32),
        compiler_params=_cparams(("parallel",)),
    )(pos, cp, land)


ANY = pl.BlockSpec(memory_space=pl.ANY)


def _mesh_pos():
    return lax.axis_index("x"), lax.axis_index("y"), lax.axis_index("c")


def _chip_peer(x, y, c, r):
    return (jnp.bitwise_xor(x, r >> 1), jnp.bitwise_xor(y, r & 1), c)


def _allgather_w_in(big0, wt, meta_loc):
    def body(big0_ref, wt_ref, meta_ref, big_ref, metaf_ref, lsem, ssem, rsem, fssem, frsem, msem_s, msem_r, osem):
        del big0_ref
        x, y, c = _mesh_pos()
        j = 2 * x + y
        me, sibling = (x, y, c), (x, y, 1 - c)

        def half_wait(s_sem, r_sem, halves=1):
            rows = big_ref.at[pl.ds(0, halves * AG_IN_HALF), :]
            return pltpu.make_async_remote_copy(src_ref=rows, dst_ref=rows, send_sem=s_sem, recv_sem=r_sem,
                                                device_id=me, device_id_type=MESH)

        own_meta = pltpu.make_async_copy(meta_ref, metaf_ref.at[j], lsem)
        own_meta.start()
        for s in range(N_CHIPS):
            for hf in range(2):
                @pl.when((j == s) & (c == hf))
                def _():
                    for r in (1, 2, 3):
                        for p, b, n in _ag_half(s, True, hf):
                            pltpu.make_async_remote_copy(
                                src_ref=wt_ref.at[pl.ds(p, n), :], dst_ref=big_ref.at[pl.ds(b, n), :],
                                send_sem=ssem.at[r - 1], recv_sem=rsem.at[r - 1],
                                device_id=_chip_peer(x, y, c, r), device_id_type=MESH).start()

            @pl.when(j == s)
            def _():
                for hf in range(2):
                    for p, b, n in _ag_half(s, True, hf):
                        pltpu.make_async_remote_copy(
                            src_ref=wt_ref.at[pl.ds(p, n), :], dst_ref=big_ref.at[pl.ds(b, n), :],
                            send_sem=osem.at[0], recv_sem=osem.at[1], device_id=sibling, device_id_type=MESH).start()

        meta_copies = [pltpu.make_async_remote_copy(
            src_ref=meta_ref, dst_ref=metaf_ref.at[j], send_sem=msem_s.at[r - 1], recv_sem=msem_r.at[r - 1],
            device_id=_chip_peer(x, y, c, r), device_id_type=MESH) for r in (1, 2, 3)]
        for cp in meta_copies:
            cp.start()
        for r in (1, 2, 3):
            half_wait(ssem.at[r - 1], rsem.at[r - 1]).wait_recv()
            src_shard = jnp.bitwise_xor(j, r)
            for s in range(N_CHIPS):
                for hf in range(2):
                    @pl.when((src_shard == s) & (c == hf))
                    def _():
                        for _, b, n in _ag_half(s, True, hf):
                            rows = big_ref.at[pl.ds(b, n), :]
                            pltpu.make_async_remote_copy(
                                src_ref=rows, dst_ref=rows, send_sem=fssem.at[r - 1], recv_sem=frsem.at[r - 1],
                                device_id=sibling, device_id_type=MESH).start()
        for r in (1, 2, 3):
            half_wait(fssem.at[r - 1], frsem.at[r - 1]).wait_recv()
        for r in (1, 2, 3):
            half_wait(ssem.at[r - 1], rsem.at[r - 1]).wait_send()
            half_wait(fssem.at[r - 1], frsem.at[r - 1]).wait_send()
        own_rows = half_wait(osem.at[0], osem.at[1], halves=2)
        own_rows.wait_recv()
        own_rows.wait_send()
        for cp in meta_copies:
            cp.wait_send()
            cp.wait_recv()
        own_meta.wait()

    dma3 = pltpu.SemaphoreType.DMA((3,))
    return pl.pallas_call(
        body, name="allgather_w_in", in_specs=[ANY, ANY, ANY], out_specs=[ANY, ANY],
        out_shape=[jax.ShapeDtypeStruct((BIG_ROWS, D), BF16), jax.ShapeDtypeStruct((N_CHIPS,) + meta_loc.shape, F32)],
        scratch_shapes=[pltpu.SemaphoreType.DMA(()), dma3, dma3, dma3, dma3, dma3, dma3, pltpu.SemaphoreType.DMA((2,))],
        input_output_aliases={0: 0},
        compiler_params=pltpu.CompilerParams(has_side_effects=True),
    )(big0, wt, meta_loc)


HBM = pl.BlockSpec(memory_space=pltpu.HBM)
SEM = pl.BlockSpec(memory_space=pltpu.SEMAPHORE)
EFFECT = pltpu.SideEffectType.DATAFLOW_SIDE_EFFECTING
REST_PEERS = [(r, dc) for r in (1, 2, 3) for dc in (0, 1)]


def _allgather_rest_start(big):
    def body(big_ref, *rest):
        s_sems, r_sems = rest[:6], rest[6:12]
        x, y, c = _mesh_pos()
        j = 2 * x + y
        for s in range(N_CHIPS):
            for hf in range(2):
                @pl.when((j == s) & (c == hf))
                def _():
                    for k, (r, dc) in enumerate(REST_PEERS):
                        for _, b, n in _ag_half(s, False, hf):
                            rows = big_ref.at[pl.ds(b, n), :]
                            pltpu.make_async_remote_copy(
                                src_ref=rows, dst_ref=rows, send_sem=s_sems[k], recv_sem=r_sems[2 * (r - 1) + hf],
                                device_id=_chip_peer(x, y, dc, r), device_id_type=MESH).start()

    sem = pltpu.SemaphoreType.DMA(())
    out = pl.pallas_call(
        body, name="allgather_rest_start", out_shape=(sem,) * 12 + (pltpu.HBM(big.shape, big.dtype),),
        in_specs=(HBM,), out_specs=(SEM,) * 12 + (HBM,), input_output_aliases={0: 12},
        compiler_params=pltpu.CompilerParams(has_side_effects=EFFECT),
    )(pltpu.with_memory_space_constraint(big, pltpu.HBM))
    return out[:12], out[12]


def _allgather_rest_wait(sems, big, after):
    def body(big_ref, *rest):
        s_sems, r_sems = rest[:6], rest[6:12]
        me = _mesh_pos()
        rows = big_ref.at[pl.ds(0, AG_REST_HALF), :]
        for k in range(6):
            copy = pltpu.make_async_remote_copy(src_ref=rows, dst_ref=rows, send_sem=s_sems[k], recv_sem=r_sems[k],
                                                device_id=me, device_id_type=MESH)
            copy.wait_send()
            copy.wait_recv()

    return pl.pallas_call(
        body, name="allgather_rest_wait", out_shape=pltpu.HBM(big.shape, big.dtype),
        in_specs=(HBM,) + (SEM,) * 12 + (pl.BlockSpec(memory_space=pl.ANY),), out_specs=HBM,
        input_output_aliases={0: 0}, compiler_params=pltpu.CompilerParams(has_side_effects=EFFECT),
    )(big, *sems, after)


RS_SPLIT = 2


def _rs_core_exchange(srcs):
    def body(*refs):
        src_refs, (land_ref, ssem, rsem) = refs[:len(srcs)], refs[len(srcs):]
        x, y, c = _mesh_pos()
        me, sibling = (x, y, c), (x, y, 1 - c)
        for hf in range(2):
            @pl.when(c == hf)
            def _():
                for s in range(N_CHIPS):
                    for k, r0, p, n in _grad_half(s, 1 - hf):
                        pltpu.make_async_remote_copy(
                            src_ref=src_refs[k].at[pl.ds(r0, n), :], dst_ref=land_ref.at[s, pl.ds(p, n), :],
                            send_sem=ssem, recv_sem=rsem, device_id=sibling, device_id_type=MESH).start()
        whole = pltpu.make_async_remote_copy(src_ref=land_ref, dst_ref=land_ref, send_sem=ssem, recv_sem=rsem,
                                             device_id=me, device_id_type=MESH)
        whole.wait_recv()
        whole.wait_send()

    return pl.pallas_call(
        body, name="rs_core_exchange", in_specs=[ANY] * len(srcs), out_specs=ANY,
        out_shape=jax.ShapeDtypeStruct((N_CHIPS, PK_HALF, D), BF16),
        scratch_shapes=[pltpu.SemaphoreType.DMA(())] * 2,
        compiler_params=pltpu.CompilerParams(has_side_effects=True),
    )(*srcs)


def _rs_chip_start(cp):
    def body(cp_ref, land_ref, s1, s2, s3, r1, r2, r3, cp_thru, land_thru, token):
        x, y, c = _mesh_pos()
        j = 2 * x + y
        for r, s_sem, r_sem in zip((1, 2, 3), (s1, s2, s3), (r1, r2, r3)):
            pltpu.make_async_remote_copy(
                src_ref=cp_ref.at[jnp.bitwise_xor(j, r)], dst_ref=land_ref.at[r - 1], send_sem=s_sem, recv_sem=r_sem,
                device_id=_chip_peer(x, y, c, r), device_id_type=MESH).start()
        token[...] = jnp.zeros_like(token)

    land_shape = (3, PK_HALF, D)
    sem = pltpu.SemaphoreType.DMA(())
    out = pl.pallas_call(
        body, name="rs_chip_start",
        out_shape=(sem,) * 6 + (pltpu.HBM(cp.shape, cp.dtype), pltpu.HBM(land_shape, BF16),
                                jax.ShapeDtypeStruct((8, 128), F32)),
        in_specs=(HBM, HBM), out_specs=(SEM,) * 6 + (HBM, HBM, pl.BlockSpec(memory_space=pltpu.VMEM)),
        input_output_aliases={0: 6, 1: 7},
        compiler_params=pltpu.CompilerParams(has_side_effects=EFFECT),
    )(pltpu.with_memory_space_constraint(cp, pltpu.HBM),
      pltpu.with_memory_space_constraint(lax.empty(land_shape, BF16), pltpu.HBM))
    return out[:6], out[6], out[7], out[8]


def _rs_chip_wait(sems, cp_thru, land_thru, after):
    def body(cp_ref, land_ref, s1, s2, s3, r1, r2, r3, after_ref, cp_out, land_out):
        me = _mesh_pos()
        for r, s_sem, r_sem in zip((1, 2, 3), (s1, s2, s3), (r1, r2, r3)):
            copy = pltpu.make_async_remote_copy(src_ref=cp_ref.at[0], dst_ref=land_ref.at[r - 1], send_sem=s_sem,
                                                recv_sem=r_sem, device_id=me, device_id_type=MESH)
            copy.wait_send()
            copy.wait_recv()

    return pl.pallas_call(
        body, name="rs_chip_wait",
        out_shape=(pltpu.HBM(cp_thru.shape, cp_thru.dtype), pltpu.HBM(land_thru.shape, land_thru.dtype)),
        in_specs=(HBM, HBM) + (SEM,) * 6 + (pl.BlockSpec(memory_space=pl.ANY),), out_specs=(HBM, HBM),
        input_output_aliases={0: 0, 1: 1},
        compiler_params=pltpu.CompilerParams(has_side_effects=EFFECT),
    )(cp_thru, land_thru, *sems, after)


def _rs_finish(full, small):
    n = PK_HALF // RS_SPLIT

    def body(full_in_ref, sm_ref, full_ref, all_ref, lsem, ssem, rsem, sm_s, sm_r):
        del full_in_ref
        x, y, c = _mesh_pos()
        me, sibling = (x, y, c), (x, y, 1 - c)
        my_id = 4 * x + 2 * y + c
        own_sm = pltpu.make_async_copy(sm_ref, all_ref.at[my_id], lsem)
        own_sm.start()
        for q in range(RS_SPLIT):
            rows = full_ref.at[c, pl.ds(q * n, n), :]
            pltpu.make_async_remote_copy(src_ref=rows, dst_ref=rows, send_sem=ssem, recv_sem=rsem,
                                         device_id=sibling, device_id_type=MESH).start()
        half = pltpu.make_async_remote_copy(src_ref=full_ref.at[c], dst_ref=full_ref.at[c], send_sem=ssem,
                                            recv_sem=rsem, device_id=me, device_id_type=MESH)
        smalls = [pltpu.make_async_remote_copy(
            src_ref=sm_ref, dst_ref=all_ref.at[my_id], send_sem=sm_s.at[r - 1], recv_sem=sm_r.at[r - 1],
            device_id=(jnp.bitwise_xor(x, r >> 2), jnp.bitwise_xor(y, (r >> 1) & 1), jnp.bitwise_xor(c, r & 1)),
            device_id_type=MESH) for r in range(1, 8)]
        for cpy in smalls:
            cpy.start()
        half.wait_recv()
        half.wait_send()
        for cpy in smalls:
            cpy.wait_recv()
        for cpy in smalls:
            cpy.wait_send()
        own_sm.wait()

    dma7 = pltpu.SemaphoreType.DMA((7,))
    return pl.pallas_call(
        body, name="rs_finish", in_specs=[ANY, ANY], out_specs=[ANY, ANY],
        out_shape=[jax.ShapeDtypeStruct((2, PK_HALF, D), F32), jax.ShapeDtypeStruct((8, SM_ROWS, 128), F32)],
        scratch_shapes=[pltpu.SemaphoreType.DMA(()), pltpu.SemaphoreType.DMA(()), pltpu.SemaphoreType.DMA(()), dma7, dma7],
        input_output_aliases={0: 0},
        compiler_params=pltpu.CompilerParams(has_side_effects=True),
    )(full, small)


def _rope_tables(s):
    pos = jnp.arange(s, dtype=F32) - PAD_FRONT

    def cs(d):
        inv = ROPE_BASE ** (-jnp.arange(0, d, 2, dtype=F32) / d)
        ang = pos[:, None] * inv[None, :]
        return jnp.cos(ang), jnp.sin(ang)

    c, sn = cs(ROPE_D)
    z = jnp.zeros_like(c)
    ktab = (jnp.concatenate([c, c, z, z], axis=1), jnp.concatenate([-sn, z, z, z], axis=1),
            jnp.concatenate([z, sn, z, z], axis=1))
    one = jnp.ones_like(c)
    qtab = tuple(ATT_SCALE * jnp.concatenate(parts, axis=1) for parts in (
        [one, one, one, one, c, c, z, z], [z, z, z, z, -sn, z, z, z], [z, z, z, z, z, sn, z, z]))
    lane = jnp.arange(HD) == DH + ROPE_D
    qbias = lane.astype(F32)[None, :]
    kbias = jnp.where((pos < 0)[:, None] & lane[None, DH:], NEG_INF, 0.0).astype(F32)
    c, sn = cs(DH)
    return qtab, ktab, qbias, kbias, jnp.concatenate([c, c], axis=1), jnp.concatenate([-sn, sn], axis=1)


def _grad_mm(a, b, m, n, s, name):
    return _matmul(a, b, "tn", m, n, s, min(m, 512), min(n, 1024), s, BF16, name, n_outer=True)


def _local_step(x2, target2, hp0, big, small_w, gather_wait=None, reduce_start=None, reduce_wait=None):
    norm_w, qn_w, kvn_w, gn_w, gn_b, fin_w = small_w
    s = x2.shape[0] + BLK
    tm = _row_tile(s, 1408)
    qtab, ktab, qbias, kbias, c128, s128 = _rope_tables(s)
    consts, gamma = _ret_consts()

    h, xn, r1 = _rms_in(x2, hp0, norm_w, s)
    proj = _matmul(xn, big, "nt", s, PROJ_W, D, tm, 512, D, BF16, "proj", b_off=(BIG_IN // 512, 0))
    if gather_wait is not None:
        big = gather_wait(big, proj)
    uq = big[BIG_UQ:BIG_UQ + 384].reshape(HEADS, DH + ROPE_D, Q_RANK)
    wq = jnp.pad(uq, ((0, 0), (0, HD - DH - ROPE_D), (0, 0))).reshape(HEADS * HD, Q_RANK)
    wukv = big[BIG_UKV:BIG_UKV + 256].reshape(2 * WIDTH, KV_RANK)
    wbm = big[BIG_BM:BIG_BM + 1024].reshape(D, WIDTH)
    wbr = big[BIG_BR:BIG_BR + 1024].reshape(D, WIDTH)
    cqn, rq, ckvn, rkv, qq, kk, vv = _mla_prep(proj, qn_w, kvn_w, wq, wukv, qtab, ktab, qbias, kbias, s)
    o_mla, y_mla, lse = _attn_fwd(qq, kk, vv, proj, s)
    y_ret, on, rstd, qr, kr, rall = _ret_fwd(proj, gn_w, gn_b, c128, s128, consts, gamma, s)
    u_mla, u_ret, merged = _merge_fwd(y_mla, y_ret, wbm, wbr, proj, s)
    h2 = _out_fwd(merged, big, h, s)
    dh2, dh2b, loss_blk, dfin = _loss_bwd(h2, target2, fin_w, s)

    dproj, du_mla, du_ret = _merge_bwd(dh2b, big, u_mla, u_ret, proj, s)
    g_out = _grad_mm(merged, dh2b, D, D, s, "grad_w_out")
    dproj, do_mla, delta = _branch_mla_bwd(dproj, du_mla, wbm, o_mla, proj, s)
    g_bm = _grad_mm(du_mla, y_mla, D, WIDTH, s, "grad_w_branch_mla")
    dproj, do_ret, dgw, dgb = _branch_ret_bwd(dproj, du_ret, wbr, on, rstd, gn_w, gn_b, proj, s)
    g_br = _grad_mm(du_ret, y_ret, D, WIDTH, s, "grad_w_branch_ret")
    dproj = _ret_bwd(dproj, qr, kr, proj, rall, do_ret, c128, s128, consts, gamma, s)
    dqq, dkk, dvv = _attn_bwd(qq, kk, vv, do_mla, lse, delta.reshape(lse.shape), s)
    dproj, dq, dkv, dqnw, dkvnw = _mla_prep_bwd(dproj, dqq, dkk, dvv, proj, rq, rkv, qn_w, kvn_w, wq, wukv,
                                                qtab, ktab, s)
    g_q = _grad_mm(dq, cqn, HEADS * HD, Q_RANK, s, "grad_w_uq")
    g_ukv = _grad_mm(dkv, ckvn, 2 * WIDTH, KV_RANK, s, "grad_w_ukv")
    g_in = _grad_mm(dproj, xn, PROJ_W, D, s, "grad_w_in")

    g_uq = g_q.reshape(HEADS, HD, Q_RANK)[:, :DH + ROPE_D]
    grads = (g_in, g_out, g_bm.reshape(1024, D), g_br.reshape(1024, D), g_uq.reshape(384, D), g_ukv.reshape(256, D))
    token, travelling = (None, grads) if reduce_start is None else reduce_start(grads)
    dxn = _matmul(dproj, big, "nn", s, D, PROJ_W, _row_tile(s, 384), 512, PROJ_W, F32, "dxn",
                  b_off=(BIG_IN // PROJ_W, 0), n_outer=True, after=token)
    grad_x, gmeta_blk, dnorm = _rms_in_bwd(dxn, h, r1, dh2, norm_w, s)
    small = _small_rows((dnorm, dqnw, dkvnw, dgw, dgb, dfin), loss_blk, gmeta_blk[PAD_FRONT:])
    return grad_x, travelling if reduce_wait is None else reduce_wait(travelling, dnorm), small


def _small_rows(ws, first=None, last=None):
    def part(a, rows):
        a = a.reshape(-1, 128)
        return a if a.shape[0] == rows else jnp.pad(a, ((0, rows - a.shape[0]), (0, 0)))

    bounds = (SM_NORM, SM_QN, SM_KVN, SM_GNW, SM_GNB, SM_FIN, SM_META)
    first = jnp.zeros((SM_NORM, 128), F32) if first is None else first
    last = jnp.zeros((SM_ROWS - SM_META, 128), F32) if last is None else last
    return jnp.concatenate([part(first, SM_NORM)] + [part(w, bounds[k + 1] - bounds[k]) for k, w in enumerate(ws)]
                           + [part(last, SM_ROWS - SM_META)], axis=0)


def kernel(x, meta, norm_w, w_in, mla_q_norm_w, mla_w_uq, mla_kv_norm_w, mla_w_ukv, ret_gn_w, ret_gn_b, w_branch_mla, w_branch_ret, w_out, final_norm_w, loss_target, m_meta, m_norm_w, m_w_in, m_mla_q_norm_w, m_mla_w_uq, m_mla_kv_norm_w, m_mla_w_ukv, m_ret_gn_w, m_ret_gn_b, m_w_branch_mla, m_w_branch_ret, m_w_out, m_final_norm_w, v_meta, v_norm_w, v_w_in, v_mla_q_norm_w, v_mla_w_uq, v_mla_kv_norm_w, v_mla_w_ukv, v_ret_gn_w, v_ret_gn_b, v_w_branch_mla, v_w_branch_ret, v_w_out, v_final_norm_w):
    j = 2 * lax.axis_index("x") + lax.axis_index("y")
    tr = lambda w: w[0].T.reshape(-1, D).astype(BF16)
    pos = jnp.stack([j, lax.axis_index("c")]).astype(jnp.int32)
    put = lambda buf, rows, off: lax.dynamic_update_slice(buf, rows, (off, 0))
    big0 = lax.empty((BIG_ROWS, D), BF16)
    big0 = put(big0, w_out[0].astype(BF16), BIG_OUT + 512 * j)
    big0 = put(big0, tr(w_branch_mla), BIG_BM + 256 * j)
    big0 = put(big0, tr(w_branch_ret), BIG_BR + 256 * j)
    big0 = put(big0, tr(mla_w_uq), BIG_UQ + 96 * j)
    big0 = put(big0, tr(mla_w_ukv), BIG_UKV + 64 * j)
    big0 = put(big0, jnp.zeros((AG_JUNK_END - AG_JUNK_REST, D), BF16), AG_JUNK_REST)
    big0 = put(big0, jnp.zeros((ZERO_ROWS, D), BF16), BIG_IN + IN_WIDTH)
    wt = jnp.concatenate([tr(w_in), jnp.zeros((2 * AG_IN_HALF - IN_SHARD, D), BF16)], axis=0)
    big, meta_all = _allgather_w_in(big0, wt, meta)
    gather_sems, big = _allgather_rest_start(big)

    def gather_wait(big_travelling, after):
        return _allgather_rest_wait(gather_sems, big_travelling, after)

    meta_full = meta_all.transpose(1, 0, 2).reshape(N_META, D)
    hp0 = jnp.concatenate([jnp.zeros((PAD_FRONT, D), F32), meta_full], axis=0)
    small_w = (norm_w, mla_q_norm_w, mla_kv_norm_w, ret_gn_w, ret_gn_b, final_norm_w.reshape(1, D))

    def reduce_start(grads):
        chip_part = _rs_core_add(grads, _rs_core_exchange(grads), pos)
        sems, part_thru, land_thru, token = _rs_chip_start(chip_part)
        return token, (sems, part_thru, land_thru)

    def reduce_wait(state, after):
        return _rs_chip_wait(*state, after)

    grad_x, (chip_part, land2), small = _local_step(x[0], loss_target[0], hp0, big, small_w, gather_wait,
                                                    reduce_start, reduce_wait)
    full, small_all = _rs_finish(_rs_chip_add(chip_part, land2, pos), small)
    full = full.reshape(PK_ROWS, D)

    untr = lambda lo, hi, rows: full[lo:hi].reshape(rows, -1).T
    grads = {
        "w_out": full[PK_OUT:PK_BM], "w_branch_mla": untr(PK_BM, PK_BR, 512), "w_branch_ret": untr(PK_BR, PK_UQ, 512),
        "mla_w_uq": untr(PK_UQ, PK_UKV, 384), "mla_w_ukv": untr(PK_UKV, PK_PAD, 512),
    }
    big_w = {"mla_w_uq": (mla_w_uq, m_mla_w_uq, v_mla_w_uq),
             "mla_w_ukv": (mla_w_ukv, m_mla_w_ukv, v_mla_w_ukv),
             "w_branch_mla": (w_branch_mla, m_w_branch_mla, v_w_branch_mla),
             "w_branch_ret": (w_branch_ret, m_w_branch_ret, v_w_branch_ret), "w_out": (w_out, m_w_out, v_w_out)}
    res = {"w_in": tuple(a.T[None] for a in _adamw_t(w_in[0].T, full, m_w_in[0].T, v_w_in[0].T, "adamw_w_in"))}
    for name, (w, m, v) in big_w.items():
        d, nm, nv = _adamw(w[0], grads[name], m[0], v[0], "adamw_" + name)
        res[name] = (grads[name][None], d[None], nm[None], nv[None])

    small_m = (m_norm_w, m_mla_q_norm_w, m_mla_kv_norm_w, m_ret_gn_w, m_ret_gn_b, m_final_norm_w.reshape(1, D))
    small_v = (v_norm_w, v_mla_q_norm_w, v_mla_kv_norm_w, v_ret_gn_w, v_ret_gn_b, v_final_norm_w.reshape(1, D))
    gs, ds, ms, vs = _adamw_small(_small_rows(small_w), small_all, _small_rows(small_m), _small_rows(small_v))
    names = ["norm_w", "mla_q_norm_w", "mla_kv_norm_w", "ret_gn_w", "ret_gn_b", "final_norm_w"]
    bounds = [SM_NORM, SM_QN, SM_KVN, SM_GNW, SM_GNB, SM_FIN]
    for k, name in enumerate(names):
        shape = (D,) if name == "final_norm_w" else (1, -1)
        rows = small_w[k].size // 128
        res[name] = tuple(a[bounds[k]:bounds[k] + rows].reshape(shape) for a in (gs, ds, ms, vs))
    g_meta = lax.dynamic_slice_in_dim(gs[SM_META:SM_META + 256].reshape(N_META, D), j * (D // N_CHIPS), D // N_CHIPS, axis=1)
    res["meta"] = (g_meta,) + tuple(_adamw(meta, g_meta, m_meta, v_meta, "adamw_meta"))

    order = ["meta", "norm_w", "w_in", "mla_q_norm_w", "mla_w_uq", "mla_kv_norm_w", "mla_w_ukv", "ret_gn_w", "ret_gn_b",
             "w_branch_mla", "w_branch_ret", "w_out", "final_norm_w"]
    return (gs[0, 0], grad_x[None]) + tuple(res[n][k] for k in range(4) for n in order)
```

```python
import functools
import math

import numpy as np
import jax
import jax.numpy as jnp
from jax import lax
from jax.experimental import pallas as pl
from jax.experimental.pallas import tpu as pltpu

F32 = jnp.float32
BF16 = jnp.bfloat16
MESH = pl.DeviceIdType.MESH

D = 2048
N_META = 16
BLK = 128
PAD_FRONT = BLK - N_META
HEADS = 8
DH = 128
ROPE_D = 64
Q_RANK = 512
KV_RANK = 256
WIDTH = HEADS * DH
ROPE_BASE = 10000.0
NORM_EPS = 1e-6
GN_EPS = 1e-5
NEG_INF = -1e30
ATT_SCALE = (DH + ROPE_D) ** -0.5
RET_SCALE = DH ** -0.5
IN_WIDTH = 10048
N_CHIPS = 4
IN_SHARD = IN_WIDTH // N_CHIPS
ADAM_LR, ADAM_B1, ADAM_B2, ADAM_EPS, ADAM_WD, ADAM_STEP = 0.001, 0.9, 0.999, 1e-08, 0.01, 10

R_Q, R_K, R_V, Z_RET, Z_MLA, GATE0, GATE1 = 0, 1024, 2048, 3072, 4096, 5120, 5120 + D
C_Q = 5120 + 2 * D
C_KV = C_Q + Q_RANK
K_PE = C_KV + KV_RANK
PROJ_W = 10240
IN_RUNS = ((0, 832, C_Q), (832, 1856, Z_MLA), (1856, 4928, R_Q), (4928, 5952, Z_RET), (5952, IN_WIDTH, GATE0))

BIG_IN, BIG_OUT, BIG_BM, BIG_BR, BIG_UQ, BIG_UKV, BIG_JUNK, BIG_ROWS = 0, 10240, 12288, 13312, 14336, 14720, 14976, 15360
ZERO_ROWS = PROJ_W - IN_WIDTH
PK_IN, PK_OUT, PK_BM, PK_BR, PK_UQ, PK_UKV, PK_PAD, PK_ROWS = 0, 2512, 3024, 3280, 3536, 3632, 3696, 3712
PK_HALF = PK_ROWS // 2
SM_LOSS, SM_NORM, SM_QN, SM_KVN, SM_GNW, SM_GNB, SM_FIN, SM_META, SM_ROWS = 0, 8, 24, 32, 40, 48, 56, 72, 328

VMEM_LIMIT = 56 * 1024 * 1024


def _pieces(shard):
    j = shard
    out = [(PK_OUT, BIG_OUT + 512 * j, 512), (PK_BM, BIG_BM + 256 * j, 256), (PK_BR, BIG_BR + 256 * j, 256),
           (PK_UQ, BIG_UQ + 96 * j, 96), (PK_UKV, BIG_UKV + 64 * j, 64)]
    for lo, hi, new in IN_RUNS:
        a, b = max(lo, IN_SHARD * j), min(hi, IN_SHARD * (j + 1))
        if b > a:
            out.append((PK_IN + a - IN_SHARD * j, BIG_IN + new + a - lo, b - a))
    out += [(PK_PAD, BIG_JUNK + 16 * j, 16)]
    return out


AG_JUNK_REST, AG_JUNK_IN, AG_JUNK_END = BIG_JUNK, BIG_JUNK + 128, BIG_JUNK + 192
AG_REST_HALF, AG_IN_HALF = 608, 1264


def _ag_half(shard, w_in, half):
    pieces = [(b, n) for p, b, n in _pieces(shard) if p < PK_PAD and (p < PK_OUT) == w_in]
    pieces.append((AG_JUNK_IN + 16 * shard, 16) if w_in else (AG_JUNK_REST + 32 * shard, 32))
    size = AG_IN_HALF if w_in else AG_REST_HALF
    out, pos = [], 0
    for b, n in pieces:
        s, e = max(pos, half * size), min(pos + n, (half + 1) * size)
        if e > s:
            out.append((s, b + s - pos, e - s))
        pos += n
    assert pos == 2 * size
    return out


def _grad_half(shard, half):
    j = shard
    pieces = [(0, b - BIG_IN, p, n) for p, b, n in _pieces(j) if p < PK_OUT]
    pieces += [(1, 512 * j, PK_OUT, 512), (2, 256 * j, PK_BM, 256), (3, 256 * j, PK_BR, 256), (4, 96 * j, PK_UQ, 96),
               (5, 64 * j, PK_UKV, 64), (0, IN_WIDTH, PK_PAD, PK_ROWS - PK_PAD)]
    lo, hi = half * PK_HALF, (half + 1) * PK_HALF
    out = []
    for k, r0, p, n in pieces:
        s, e = max(p, lo), min(p + n, hi)
        if e > s:
            out.append((k, r0 + s - p, s - lo, e - s))
    assert sum(n for _, _, _, n in out) == PK_HALF
    return out


def _row_tile(rows, cap):
    best = BLK
    for t in range(BLK, cap + 1, BLK):
        if rows % t == 0:
            best = t
    return best


def _cparams(sem):
    return pltpu.CompilerParams(dimension_semantics=sem, vmem_limit_bytes=VMEM_LIMIT)


def _dot(a, b, form):
    dn = {"nt": (((1,), (1,)), ((), ())), "nn": (((1,), (0,)), ((), ())), "tn": (((0,), (0,)), ((), ()))}[form]
    return lax.dot_general(a, b, dn, preferred_element_type=F32)


def _sigmoid(v):
    return 1.0 / (1.0 + jnp.exp(-v))


def _matmul(a, b, form, m, n, k, tm, tn, tk, out_dtype, name, a_off=(0, 0), b_off=(0, 0), n_outer=False, after=None):
    nk = k // tk
    gi, gj = m // tm, n // tn

    def ij(g0, g1):
        return (g1, g0) if n_outer else (g0, g1)

    if form == "nt":
        a_spec = pl.BlockSpec((tm, tk), lambda g0, g1, kk: (ij(g0, g1)[0] + a_off[0], kk + a_off[1]))
        b_spec = pl.BlockSpec((tn, tk), lambda g0, g1, kk: (ij(g0, g1)[1] + b_off[0], kk + b_off[1]))
    elif form == "nn":
        a_spec = pl.BlockSpec((tm, tk), lambda g0, g1, kk: (ij(g0, g1)[0] + a_off[0], kk + a_off[1]))
        b_spec = pl.BlockSpec((tk, tn), lambda g0, g1, kk: (kk + b_off[0], ij(g0, g1)[1] + b_off[1]))
    else:
        a_spec = pl.BlockSpec((tk, tm), lambda g0, g1, kk: (kk + a_off[0], ij(g0, g1)[0] + a_off[1]))
        b_spec = pl.BlockSpec((tk, tn), lambda g0, g1, kk: (kk + b_off[0], ij(g0, g1)[1] + b_off[1]))
    o_spec = pl.BlockSpec((tm, tn), lambda g0, g1, kk: ij(g0, g1))

    def body(a_ref, b_ref, *rest):
        o_ref, *acc = rest[0 if after is None else 1:]
        p = _dot(a_ref[...], b_ref[...], form)
        if nk == 1:
            o_ref[...] = p.astype(o_ref.dtype)
        else:
            acc_ref, = acc
            kk = pl.program_id(2)

            @pl.when(kk == 0)
            def _():
                acc_ref[...] = p

            @pl.when(kk > 0)
            def _():
                acc_ref[...] += p

            @pl.when(kk == nk - 1)
            def _():
                o_ref[...] = acc_ref[...].astype(o_ref.dtype)

    extra = [] if after is None else [after]
    return pl.pallas_call(
        body, name=name, grid=(gj, gi, nk) if n_outer else (gi, gj, nk),
        in_specs=[a_spec, b_spec] + [pl.BlockSpec(memory_space=pl.ANY)] * len(extra), out_specs=o_spec,
        out_shape=jax.ShapeDtypeStruct((m, n), out_dtype),
        scratch_shapes=[] if nk == 1 else [pltpu.VMEM((tm, tn), F32)],
        compiler_params=_cparams(("parallel", "parallel", "arbitrary")),
    )(a, b, *extra)


def _rms_in(x, hp0, norm_w, s):
    def body(x_ref, hp0_ref, w_ref, h_ref, xn_ref, r_ref):
        def run(hv):
            r = lax.rsqrt(jnp.mean(hv * hv, axis=-1, keepdims=True) + NORM_EPS)
            h_ref[...] = hv
            xn_ref[...] = (hv * r * w_ref[...]).astype(BF16)
            r_ref[...] = r

        @pl.when(pl.program_id(0) == 0)
        def _():
            run(hp0_ref[...])

        @pl.when(pl.program_id(0) > 0)
        def _():
            run(x_ref[...])

    return pl.pallas_call(
        body, name="rms_in", grid=(s // BLK,),
        in_specs=[pl.BlockSpec((BLK, D), lambda i: (jnp.maximum(i - 1, 0), 0)),
                  pl.BlockSpec((BLK, D), lambda i: (0, 0)), pl.BlockSpec((1, D), lambda i: (0, 0))],
        out_specs=[pl.BlockSpec((BLK, D), lambda i: (i, 0)), pl.BlockSpec((BLK, D), lambda i: (i, 0)),
                   pl.BlockSpec((BLK, 1), lambda i: (i, 0))],
        out_shape=[jax.ShapeDtypeStruct((s, D), F32), jax.ShapeDtypeStruct((s, D), BF16),
                   jax.ShapeDtypeStruct((s, 1), F32)],
        compiler_params=_cparams(("arbitrary",)),
    )(x, hp0, norm_w)


def _rope64(t, c, sa, sb):
    return t * c + pltpu.roll(t, t.shape[1] - ROPE_D // 2, 1) * sa + pltpu.roll(t, ROPE_D // 2, 1) * sb


def _rope128(t, c, sg):
    return t * c + pltpu.roll(t, DH // 2, 1) * sg


HD = 2 * DH


def _mla_prep(proj, qn_w, kvn_w, wq, wukv, qtab, ktab, qbias, kbias, s):
    tm = _row_tile(s, 384)

    def body(cq_ref, ckv_ref, kpe_ref, qnw_ref, kvnw_ref, wq_ref, wukv_ref, qc_ref, qa_ref, qb_ref,
             kc_ref, ka_ref, kb_ref, qbias_ref, kbias_ref, cqn_ref, rq_ref, ckvn_ref, rkv_ref, qq_ref, kk_ref, vv_ref):
        cq = cq_ref[...].astype(F32)
        rq = lax.rsqrt(jnp.mean(cq * cq, axis=-1, keepdims=True) + NORM_EPS)
        cqn = (cq * rq * qnw_ref[...]).astype(BF16)
        ckv = ckv_ref[...].astype(F32)
        rkv = lax.rsqrt(jnp.mean(ckv * ckv, axis=-1, keepdims=True) + NORM_EPS)
        ckvn = (ckv * rkv * kvnw_ref[...]).astype(BF16)
        cqn_ref[...] = cqn
        rq_ref[...] = rq
        ckvn_ref[...] = ckvn
        rkv_ref[...] = rkv
        q = _dot(cqn, wq_ref[...], "nt")
        kv = _dot(ckvn, wukv_ref[...], "nt")
        kp = (_rope64(kpe_ref[...].astype(F32), kc_ref[...], ka_ref[...], kb_ref[...]) + kbias_ref[...]).astype(BF16)
        qc, qa, qb, qbias = qc_ref[...], qa_ref[...], qb_ref[...], qbias_ref[...]
        ones = jnp.ones((tm, DH), BF16)
        for h in range(HEADS):
            lo, mid, hi = h * HD, h * HD + DH, (h + 1) * HD
            qq_ref[:, lo:hi] = (_rope64(q[:, lo:hi], qc, qa, qb) + qbias).astype(BF16)
            kk_ref[:, lo:mid] = kv[:, lo:mid].astype(BF16)
            kk_ref[:, mid:hi] = kp
            vv_ref[:, lo:mid] = kv[:, mid:hi].astype(BF16)
            vv_ref[:, mid:hi] = ones

    row = lambda w, cb: pl.BlockSpec((tm, w), lambda i: (i, cb))
    full = lambda a: pl.BlockSpec(a.shape, lambda i: (0, 0))
    wide = jax.ShapeDtypeStruct((s, HEADS * HD), BF16)
    return pl.pallas_call(
        body, name="mla_prep", grid=(s // tm,),
        in_specs=[row(Q_RANK, C_Q // Q_RANK), row(KV_RANK, C_KV // KV_RANK), row(DH, K_PE // DH),
                  full(qn_w), full(kvn_w), full(wq), full(wukv), row(HD, 0), row(HD, 0), row(HD, 0),
                  row(DH, 0), row(DH, 0), row(DH, 0), full(qbias), row(DH, 0)],
        out_specs=[row(Q_RANK, 0), row(1, 0), row(KV_RANK, 0), row(1, 0), row(HEADS * HD, 0), row(HEADS * HD, 0),
                   row(HEADS * HD, 0)],
        out_shape=[jax.ShapeDtypeStruct((s, Q_RANK), BF16), jax.ShapeDtypeStruct((s, 1), F32),
                   jax.ShapeDtypeStruct((s, KV_RANK), BF16), jax.ShapeDtypeStruct((s, 1), F32), wide, wide, wide],
        compiler_params=_cparams(("parallel",)),
    )(proj, proj, proj, qn_w, kvn_w, wq, wukv, *qtab, *ktab, qbias, kbias)


def _diag_mask(t):
    return lax.broadcasted_iota(jnp.int32, (t, t), 0) <= lax.broadcasted_iota(jnp.int32, (t, t), 1)


def _silu(z):
    return z * _sigmoid(z)


def _attn_fwd(qq, kk, vv, proj, s):
    t = _row_tile(s, 384)
    n = s // t

    def body(q_ref, k_ref, v_ref, z_ref, o_ref, y_ref, lse_ref, acc_ref, m_ref):
        qi = pl.program_id(1)
        q = q_ref[...]
        m_ref[...] = jnp.full(m_ref.shape, NEG_INF, F32)
        acc_ref[...] = jnp.zeros(acc_ref.shape, F32)

        def keys(ki):
            return pl.ds(pl.multiple_of(ki * t, t), t)

        def scores(ki):
            return _dot(k_ref[keys(ki), :], q, "nt")

        def tile(ki, st):
            m_old = m_ref[...]
            m_new = jnp.maximum(m_old, jnp.max(st, axis=0, keepdims=True))
            pt = jnp.exp(st - m_new).astype(BF16)
            acc_ref[...] = jnp.exp(m_old - m_new) * acc_ref[...] + _dot(v_ref[keys(ki), :], pt, "tn")
            m_ref[...] = m_new

        def inner(ki, st):
            st_after = scores(ki + 1)
            tile(ki, st)
            return st_after

        st_diag = lax.fori_loop(0, qi, inner, scores(0))
        tile(qi, jnp.where(_diag_mask(t), st_diag, NEG_INF))

        l = acc_ref[DH:DH + 1, :]
        o = (acc_ref[:DH, :] / l).T
        o_ref[...] = o.astype(BF16)
        y_ref[...] = (o * _silu(z_ref[...].astype(F32))).astype(BF16)
        lse_ref[0, 0] = m_ref[...] + jnp.log(l)

    qtile = pl.BlockSpec((t, DH), lambda h, i: (i, h))
    head = pl.BlockSpec((s, HD), lambda h, i: (0, h))
    return pl.pallas_call(
        body, name="attn_fwd", grid=(HEADS, s // t),
        in_specs=[pl.BlockSpec((t, HD), lambda h, i: (i, h)), head, head,
                  pl.BlockSpec((t, DH), lambda h, i: (i, Z_MLA // DH + h))],
        out_specs=[qtile, qtile, pl.BlockSpec((1, 1, 1, t), lambda h, i: (h, i, 0, 0))],
        out_shape=[jax.ShapeDtypeStruct((s, WIDTH), BF16), jax.ShapeDtypeStruct((s, WIDTH), BF16),
                   jax.ShapeDtypeStruct((HEADS, s // t, 1, t), F32)],
        scratch_shapes=[pltpu.VMEM((HD, t), F32), pltpu.VMEM((1, t), F32)],
        compiler_params=_cparams(("parallel", "arbitrary")),
    )(qq, kk, vv, proj)


def _ret_consts():
    log_g = np.log1p(-(2.0 ** (-5.0 - np.arange(HEADS, dtype=np.float64))))
    n = np.arange(BLK, dtype=np.float64)
    diff = n[:, None] - n[None, :]
    decay = np.where(diff >= 0, np.exp(log_g[:, None, None] * np.maximum(diff, 0.0)), 0.0)
    zeta = np.exp(log_g[:, None] * (BLK - 1.0 - n))[:, :, None]
    xi = np.exp(log_g[:, None] * (n + 1.0))[:, :, None]
    gamma = [float(np.float32(np.exp(g * BLK))) for g in log_g]
    return (jnp.asarray(decay, F32), jnp.asarray(zeta, F32), jnp.asarray(xi, F32)), gamma


def _ret_fwd(proj, gn_w, gn_b, c128, s128, consts, gamma, s):
    nb = s // BLK
    decay, zeta, xi = consts

    def body(rq_ref, rk_ref, rv_ref, z_ref, gw_ref, gb_ref, c_ref, s_ref, dm_ref, zt_ref, xi_ref,
             y_ref, on_ref, rstd_ref, qr_ref, kr_ref, rall_ref, state):
        @pl.when(pl.program_id(0) == 0)
        def _():
            state[...] = jnp.zeros_like(state)

        c, sg = c_ref[...], s_ref[...]
        for h in range(HEADS):
            sl = slice(h * DH, (h + 1) * DH)
            q = _rope128(rq_ref[:, sl].astype(F32), c, sg).astype(BF16)
            kf = _rope128(rk_ref[:, sl].astype(F32), c, sg) * RET_SCALE
            k = kf.astype(BF16)
            v = rv_ref[:, sl]
            qr_ref[:, sl] = q
            kr_ref[:, sl] = k
            r_prev = state[h]
            rall_ref[0, h] = r_prev
            a = _dot(q, k, "nt") * dm_ref[h]
            o = _dot(a.astype(BF16), v, "nn") + _dot(q, r_prev.astype(BF16), "nn") * xi_ref[h]
            state[h] = r_prev * gamma[h] + _dot((kf * zt_ref[h]).astype(BF16), v, "tn")
            mu = jnp.mean(o, axis=-1, keepdims=True)
            var = jnp.mean(jnp.square(o - mu), axis=-1, keepdims=True)
            rstd = lax.rsqrt(var + GN_EPS)
            on = (o - mu) * rstd
            rstd_ref[h] = rstd
            on_ref[:, sl] = on.astype(BF16)
            ogn = on * gw_ref[:, sl] + gb_ref[:, sl]
            y_ref[:, sl] = (ogn * _silu(z_ref[:, sl].astype(F32))).astype(BF16)

    seg = lambda cb: pl.BlockSpec((BLK, WIDTH), lambda i: (i, cb))
    full = lambda a: pl.BlockSpec(a.shape, lambda i: (0,) * a.ndim)
    tab = pl.BlockSpec((BLK, DH), lambda i: (i, 0))
    return pl.pallas_call(
        body, name="ret_fwd", grid=(nb,),
        in_specs=[seg(R_Q // WIDTH), seg(R_K // WIDTH), seg(R_V // WIDTH), seg(Z_RET // WIDTH), full(gn_w), full(gn_b),
                  tab, tab, full(decay), full(zeta), full(xi)],
        out_specs=[seg(0), seg(0), pl.BlockSpec((HEADS, BLK, 1), lambda i: (0, i, 0)), seg(0), seg(0),
                   pl.BlockSpec((1, HEADS, DH, DH), lambda i: (i, 0, 0, 0))],
        out_shape=[jax.ShapeDtypeStruct((s, WIDTH), BF16), jax.ShapeDtypeStruct((s, WIDTH), BF16),
                   jax.ShapeDtypeStruct((HEADS, s, 1), F32), jax.ShapeDtypeStruct((s, WIDTH), BF16),
                   jax.ShapeDtypeStruct((s, WIDTH), BF16), jax.ShapeDtypeStruct((nb, HEADS, DH, DH), F32)],
        scratch_shapes=[pltpu.VMEM((HEADS, DH, DH), F32)],
        compiler_params=_cparams(("arbitrary",)),
    )(proj, proj, proj, proj, gn_w, gn_b, c128, s128, decay, zeta, xi)


def _merge_fwd(y_mla, y_ret, wbm, wbr, proj, s):
    tm, tn = _row_tile(s, 1408), 512

    def body(ym_ref, yr_ref, wm_ref, wr_ref, g0_ref, g1_ref, um_ref, ur_ref, mg_ref):
        um = _dot(ym_ref[...], wm_ref[...], "nt")
        ur = _dot(yr_ref[...], wr_ref[...], "nt")
        um_ref[...] = um.astype(BF16)
        ur_ref[...] = ur.astype(BF16)
        mg_ref[...] = (_sigmoid(g0_ref[...].astype(F32)) * um + _sigmoid(g1_ref[...].astype(F32)) * ur).astype(BF16)

    yspec = pl.BlockSpec((tm, WIDTH), lambda i, j: (i, 0))
    wspec = pl.BlockSpec((tn, WIDTH), lambda i, j: (j, 0))
    ospec = pl.BlockSpec((tm, tn), lambda i, j: (i, j))
    return pl.pallas_call(
        body, name="merge_fwd", grid=(s // tm, D // tn),
        in_specs=[yspec, yspec, wspec, wspec, pl.BlockSpec((tm, tn), lambda i, j: (i, GATE0 // tn + j)),
                  pl.BlockSpec((tm, tn), lambda i, j: (i, GATE1 // tn + j))],
        out_specs=[ospec, ospec, ospec],
        out_shape=[jax.ShapeDtypeStruct((s, D), BF16)] * 3,
        compiler_params=_cparams(("parallel", "parallel")),
    )(y_mla, y_ret, wbm, wbr, proj, proj)


def _out_fwd(merged, big, h, s):
    tm, tn = _row_tile(s, 1408), 512

    def body(m_ref, w_ref, h_ref, o_ref):
        o_ref[...] = h_ref[...] + _dot(m_ref[...], w_ref[...], "nn")

    return pl.pallas_call(
        body, name="out_fwd", grid=(s // tm, D // tn),
        in_specs=[pl.BlockSpec((tm, D), lambda i, j: (i, 0)), pl.BlockSpec((D, tn), lambda i, j: (BIG_OUT // D, j)),
                  pl.BlockSpec((tm, tn), lambda i, j: (i, j))],
        out_specs=pl.BlockSpec((tm, tn), lambda i, j: (i, j)),
        out_shape=jax.ShapeDtypeStruct((s, D), F32),
        compiler_params=_cparams(("parallel", "parallel")),
    )(merged, big, h)


def _loss_bwd(h2, target, fin_w, s):
    nb = s // BLK

    def body(h2_ref, t_ref, w_ref, dh_ref, dhb_ref, loss_ref, dw_ref):
        i = pl.program_id(0)

        @pl.when(i == 0)
        def _():
            dh_ref[...] = jnp.zeros_like(dh_ref)
            dhb_ref[...] = jnp.zeros_like(dhb_ref)
            loss_ref[...] = jnp.zeros_like(loss_ref)
            dw_ref[...] = jnp.zeros_like(dw_ref)

        @pl.when(i > 0)
        def _():
            hv = h2_ref[...]
            w = w_ref[...]
            r = lax.rsqrt(jnp.mean(hv * hv, axis=-1, keepdims=True) + NORM_EPS)
            nrm = hv * r
            e = nrm * w - t_ref[...]
            loss_ref[...] += jnp.full(loss_ref.shape, 0.5 / D, F32) * jnp.sum(e * e)
            dy = e * (1.0 / D)
            dw_ref[...] += jnp.sum(dy * nrm, axis=0, keepdims=True)
            g = dy * w
            dh = r * (g - nrm * jnp.mean(g * nrm, axis=-1, keepdims=True))
            dh_ref[...] = dh
            dhb_ref[...] = dh.astype(BF16)

    blk = pl.BlockSpec((BLK, D), lambda i: (i, 0))
    return pl.pallas_call(
        body, name="loss_bwd", grid=(nb,),
        in_specs=[blk, pl.BlockSpec((BLK, D), lambda i: (jnp.maximum(i - 1, 0), 0)), pl.BlockSpec((1, D), lambda i: (0, 0))],
        out_specs=[blk, blk, pl.BlockSpec((8, 128), lambda i: (0, 0)), pl.BlockSpec((1, D), lambda i: (0, 0))],
        out_shape=[jax.ShapeDtypeStruct((s, D), F32), jax.ShapeDtypeStruct((s, D), BF16),
                   jax.ShapeDtypeStruct((8, 128), F32), jax.ShapeDtypeStruct((1, D), F32)],
        compiler_params=_cparams(("arbitrary",)),
    )(h2, target, fin_w)


def _merge_bwd(dh2b, big, u_mla, u_ret, proj, s):
    tm, tn = _row_tile(s, 1408), 512

    def body(d_ref, w_ref, um_ref, ur_ref, gate_ref, dproj_ref, dum_ref, dur_ref, dm_ref):
        branch, j = pl.program_id(1), pl.program_id(2)
        cols = pl.ds(pl.multiple_of(j * tn, tn), tn)

        @pl.when(branch == 0)
        def _():
            dm_ref[:, cols] = _dot(d_ref[...], w_ref[...], "nt")

        dm = dm_ref[:, cols]
        gt = _sigmoid(gate_ref[...].astype(F32))

        @pl.when(branch == 0)
        def _():
            dproj_ref[...] = (dm * um_ref[...].astype(F32) * gt * (1.0 - gt)).astype(BF16)
            dum_ref[...] = (dm * gt).astype(BF16)

        @pl.when(branch == 1)
        def _():
            dproj_ref[...] = (dm * ur_ref[...].astype(F32) * gt * (1.0 - gt)).astype(BF16)
            dur_ref[...] = (dm * gt).astype(BF16)

    last = D // tn - 1
    mla = pl.BlockSpec((tm, tn), lambda i, b, j: (i, jnp.where(b == 0, j, last)))
    ret = pl.BlockSpec((tm, tn), lambda i, b, j: (i, jnp.where(b == 0, 0, j)))
    gate = pl.BlockSpec((tm, tn), lambda i, b, j: (i, GATE0 // tn + b * (D // tn) + j))
    return pl.pallas_call(
        body, name="merge_bwd", grid=(s // tm, 2, D // tn),
        in_specs=[pl.BlockSpec((tm, D), lambda i, b, j: (i, 0)),
                  pl.BlockSpec((tn, D), lambda i, b, j: (BIG_OUT // tn + jnp.where(b == 0, j, last), 0)), mla, ret, gate],
        out_specs=[gate, mla, ret],
        out_shape=[jax.ShapeDtypeStruct((s, PROJ_W), BF16), jax.ShapeDtypeStruct((s, D), BF16),
                   jax.ShapeDtypeStruct((s, D), BF16)],
        scratch_shapes=[pltpu.VMEM((tm, D), F32)],
        compiler_params=_cparams(("parallel", "arbitrary", "arbitrary")),
    )(dh2b, big, u_mla, u_ret, proj)


def _dsilu(z):
    sg = _sigmoid(z)
    return sg * (1.0 + z * (1.0 - sg))


def _branch_mla_bwd(dproj, du, wbm, o_mla, proj, s):
    tm = _row_tile(s, 384)

    def body(dproj_in, du_ref, w_ref, o_ref, z_ref, dz_ref, do_ref, delta_ref):
        del dproj_in
        dy = _dot(du_ref[...], w_ref[...], "nn")
        z = z_ref[...].astype(F32)
        o = o_ref[...].astype(F32)
        do = dy * _silu(z)
        do_ref[...] = do.astype(BF16)
        dz_ref[...] = (dy * o * _dsilu(z)).astype(BF16)
        prod = do * o
        for h in range(HEADS):
            delta_ref[h] = jnp.sum(prod[:, h * DH:(h + 1) * DH], axis=-1, keepdims=True)

    row = lambda w, cb: pl.BlockSpec((tm, w), lambda i: (i, cb))
    return pl.pallas_call(
        body, name="branch_mla_bwd", grid=(s // tm,),
        in_specs=[ANY, row(D, 0), pl.BlockSpec((D, WIDTH), lambda i: (0, 0)), row(WIDTH, 0),
                  row(WIDTH, Z_MLA // WIDTH)],
        out_specs=[row(WIDTH, Z_MLA // WIDTH), row(WIDTH, 0), pl.BlockSpec((HEADS, tm, 1), lambda i: (0, i, 0))],
        out_shape=[jax.ShapeDtypeStruct((s, PROJ_W), BF16), jax.ShapeDtypeStruct((s, WIDTH), BF16),
                   jax.ShapeDtypeStruct((HEADS, s, 1), F32)],
        input_output_aliases={0: 0},
        compiler_params=_cparams(("parallel",)),
    )(dproj, du, wbm, o_mla, proj)


def _branch_ret_bwd(dproj, du, wbr, on, rstd, gn_w, gn_b, proj, s):
    tm = BLK

    def body(dproj_in, du_ref, w_ref, on_ref, rstd_ref, gw_ref, gb_ref, z_ref, dz_ref, do_ref, dgw_ref, dgb_ref):
        del dproj_in

        @pl.when(pl.program_id(0) == 0)
        def _():
            dgw_ref[...] = jnp.zeros_like(dgw_ref)
            dgb_ref[...] = jnp.zeros_like(dgb_ref)

        dy = _dot(du_ref[...], w_ref[...], "nn")
        z = z_ref[...].astype(F32)
        on = on_ref[...].astype(F32)
        gw = gw_ref[...]
        dogn = dy * _silu(z)
        dz_ref[...] = (dy * (on * gw + gb_ref[...]) * _dsilu(z)).astype(BF16)
        dgw_ref[...] += jnp.sum(dogn * on, axis=0, keepdims=True)
        dgb_ref[...] += jnp.sum(dogn, axis=0, keepdims=True)
        don = dogn * gw
        for h in range(HEADS):
            sl = slice(h * DH, (h + 1) * DH)
            dn, nh = don[:, sl], on[:, sl]
            do = rstd_ref[h] * (dn - jnp.mean(dn, axis=-1, keepdims=True)
                                - nh * jnp.mean(dn * nh, axis=-1, keepdims=True))
            do_ref[:, sl] = do.astype(BF16)

    row = lambda w, cb: pl.BlockSpec((tm, w), lambda i: (i, cb))
    vec = pl.BlockSpec((1, WIDTH), lambda i: (0, 0))
    return pl.pallas_call(
        body, name="branch_ret_bwd", grid=(s // tm,),
        in_specs=[ANY, row(D, 0), pl.BlockSpec((D, WIDTH), lambda i: (0, 0)), row(WIDTH, 0),
                  pl.BlockSpec((HEADS, tm, 1), lambda i: (0, i, 0)), vec, vec, row(WIDTH, Z_RET // WIDTH)],
        out_specs=[row(WIDTH, Z_RET // WIDTH), row(WIDTH, 0), vec, vec],
        out_shape=[jax.ShapeDtypeStruct((s, PROJ_W), BF16), jax.ShapeDtypeStruct((s, WIDTH), BF16)]
        + [jax.ShapeDtypeStruct((1, WIDTH), F32)] * 2,
        input_output_aliases={0: 0},
        compiler_params=_cparams(("arbitrary",)),
    )(dproj, du, wbr, on, rstd, gn_w, gn_b, proj)


def _ret_bwd(dproj, qr, kr, proj, rall, do_ret, c128, s128, consts, gamma, s):
    nb = s // BLK
    decay, zeta, xi = consts

    def body(dproj_in, q_ref, k_ref, v_ref, r_ref, do_ref, c_ref, s_ref, dm_ref, zt_ref, xi_ref, out_ref, gstate):
        del dproj_in
        dq_ref, dk_ref, dv_ref = (out_ref.at[:, pl.ds(off, WIDTH)] for off in (R_Q, R_K, R_V))

        @pl.when(pl.program_id(0) == 0)
        def _():
            gstate[...] = jnp.zeros_like(gstate)

        c, sg = c_ref[...], s_ref[...]
        for h in range(HEADS):
            sl = slice(h * DH, (h + 1) * DH)
            q, k, v, do = q_ref[:, sl], k_ref[:, sl], v_ref[:, sl], do_ref[:, sl]
            dm = dm_ref[h]
            g_next = gstate[h]
            gb = g_next.astype(BF16)
            a = (_dot(q, k, "nt") * dm).astype(BF16)
            da = (_dot(do, v, "nt") * dm).astype(BF16)
            dox = (do.astype(F32) * xi_ref[h]).astype(BF16)
            dq = _dot(da, k, "nn") + _dot(dox, r_ref[0, h].astype(BF16), "nt")
            dk = _dot(da, q, "tn") + _dot(v, gb, "nt") * zt_ref[h]
            kz = (k.astype(F32) * zt_ref[h]).astype(BF16)
            dv = _dot(a, do, "tn") + _dot(kz, gb, "nn")
            gstate[h] = g_next * gamma[h] + _dot(q, dox, "tn")
            dk = dk * RET_SCALE
            dq_ref[:, sl] = _rope128(dq, c, -sg).astype(BF16)
            dk_ref[:, sl] = _rope128(dk, c, -sg).astype(BF16)
            dv_ref[:, sl] = dv.astype(BF16)

    rev = lambda cb: pl.BlockSpec((BLK, WIDTH), lambda i: (nb - 1 - i, cb))
    full = lambda a: pl.BlockSpec(a.shape, lambda i: (0,) * a.ndim)
    tab = pl.BlockSpec((BLK, DH), lambda i: (nb - 1 - i, 0))
    return pl.pallas_call(
        body, name="ret_bwd", grid=(nb,),
        in_specs=[ANY, rev(0), rev(0), rev(R_V // WIDTH),
                  pl.BlockSpec((1, HEADS, DH, DH), lambda i: (nb - 1 - i, 0, 0, 0)),
                  rev(0), tab, tab, full(decay), full(zeta), full(xi)],
        out_specs=pl.BlockSpec((BLK, 3 * WIDTH), lambda i: (nb - 1 - i, R_Q // (3 * WIDTH))),
        out_shape=jax.ShapeDtypeStruct((s, PROJ_W), BF16),
        scratch_shapes=[pltpu.VMEM((HEADS, DH, DH), F32)],
        input_output_aliases={0: 0},
        compiler_params=_cparams(("arbitrary",)),
    )(dproj, qr, kr, proj, rall, do_ret, c128, s128, decay, zeta, xi)


def _attn_bwd(qq, kk, vv, do, lse, delta, s):
    t = _row_tile(s, 384)
    n = s // t

    def body(q_ref, k_ref, v_ref, do_ref, lse_ref, delta_ref, dq_ref, dk_ref, dv_ref, dk_acc, dv_acc):
        ki = pl.program_id(1)

        @pl.when(ki == 0)
        def _():
            dq_ref[...] = jnp.zeros(dq_ref.shape, F32)

        k, v = k_ref[...], v_ref[...]
        dk_acc[...] = jnp.zeros(dk_acc.shape, F32)
        dv_acc[...] = jnp.zeros(dv_acc.shape, F32)

        def rows(qi):
            return pl.ds(pl.multiple_of(qi * t, t), t)

        def products(qi):
            return _dot(k, q_ref[rows(qi), :], "nt"), _dot(v, do_ref[rows(qi), :], "nt")

        def tile(qi, st, dpt):
            q, dov = q_ref[rows(qi), :], do_ref[rows(qi), :]
            pt = jnp.exp(st - lse_ref[0, qi])
            dv_acc[...] += _dot(pt.astype(BF16), dov, "nn")
            dst = (pt * (dpt - delta_ref[0, qi])).astype(BF16)
            dk_acc[...] += _dot(dst, q, "nn")
            dq_ref[rows(qi), :] += _dot(dst, k, "tn")

        def inner(qi, carry):
            after = products(jnp.minimum(qi + 1, n - 1))
            tile(qi, *carry)
            return after

        st, dpt = products(ki)
        ahead = products(jnp.minimum(ki + 1, n - 1))
        tile(ki, jnp.where(_diag_mask(t), st, NEG_INF), dpt)
        lax.fori_loop(ki + 1, n, inner, ahead)

        dk_ref[...] = dk_acc[...].astype(BF16)
        dv_ref[...] = dv_acc[...].astype(BF16)

    head = lambda w: pl.BlockSpec((s, w), lambda h, i: (0, h))
    stat = pl.BlockSpec((1, n, 1, t), lambda h, i: (h, 0, 0, 0))
    return pl.pallas_call(
        body, name="attn_bwd", grid=(HEADS, n),
        in_specs=[head(HD), pl.BlockSpec((t, HD), lambda h, i: (i, h)), pl.BlockSpec((t, DH), lambda h, i: (i, 2 * h)),
                  head(DH), stat, stat],
        out_specs=[head(HD), pl.BlockSpec((t, HD), lambda h, i: (i, h)), pl.BlockSpec((t, DH), lambda h, i: (i, h))],
        out_shape=[jax.ShapeDtypeStruct((s, HEADS * HD), F32), jax.ShapeDtypeStruct((s, HEADS * HD), BF16),
                   jax.ShapeDtypeStruct((s, WIDTH), BF16)],
        scratch_shapes=[pltpu.VMEM((t, HD), F32), pltpu.VMEM((t, DH), F32)],
        compiler_params=_cparams(("parallel", "arbitrary")),
    )(qq, kk, vv, do, lse, delta)


def _rms_bwd(dn, nhat, r, w):
    g = dn * w
    return r * (g - nhat * jnp.mean(g * nhat, axis=-1, keepdims=True)), dn * nhat


def _mla_prep_bwd(dproj, dqq, dkk, dvv, proj, rq, rkv, qn_w, kvn_w, wq, wukv, qtab, ktab, s):
    tm = _row_tile(s, 384)
    tail = PROJ_W - C_Q

    def body(dproj_in, dqq_ref, dkk_ref, dvv_ref, cq_ref, ckv_ref, rq_ref, rkv_ref, qnw_ref, kvnw_ref, wq_ref,
             wukv_ref, qc_ref, qa_ref, qb_ref, kc_ref, ka_ref, kb_ref,
             tail_ref, dq_ref, dkv_ref, dqnw_ref, dkvnw_ref):
        del dproj_in
        dcq_ref = tail_ref.at[:, pl.ds(0, Q_RANK)]
        dckv_ref = tail_ref.at[:, pl.ds(C_KV - C_Q, KV_RANK)]
        dkpe_ref = tail_ref.at[:, pl.ds(K_PE - C_Q, 2 * DH)]

        @pl.when(pl.program_id(0) == 0)
        def _():
            dqnw_ref[...] = jnp.zeros_like(dqnw_ref)
            dkvnw_ref[...] = jnp.zeros_like(dkvnw_ref)

        qc, qa, qb = qc_ref[...], qa_ref[...], qb_ref[...]
        dkp = jnp.zeros((tm, DH), F32)
        for h in range(HEADS):
            lo, mid, hi = h * HD, h * HD + DH, (h + 1) * HD
            dq_ref[:, lo:hi] = _rope64(dqq_ref[:, lo:hi], qc, -qa, -qb).astype(BF16)
            dkv_ref[:, lo:mid] = dkk_ref[:, lo:mid]
            dkv_ref[:, mid:hi] = dvv_ref[:, h * DH:(h + 1) * DH]
            dkp = dkp + dkk_ref[:, mid:hi].astype(F32)
        dcqn = _dot(dq_ref[...], wq_ref[...], "nn")
        rq_v = rq_ref[...]
        dcq, prod = _rms_bwd(dcqn, cq_ref[...].astype(F32) * rq_v, rq_v, qnw_ref[...])
        dcq_ref[...] = dcq.astype(BF16)
        dqnw_ref[...] += jnp.sum(prod, axis=0, keepdims=True)
        dckvn = _dot(dkv_ref[...], wukv_ref[...], "nn")
        rkv_v = rkv_ref[...]
        dckv, prod = _rms_bwd(dckvn, ckv_ref[...].astype(F32) * rkv_v, rkv_v, kvnw_ref[...])
        dckv_ref[...] = dckv.astype(BF16)
        dkvnw_ref[...] += jnp.sum(prod, axis=0, keepdims=True)
        dkpe_ref[:, :DH] = _rope64(dkp, kc_ref[...], -ka_ref[...], -kb_ref[...]).astype(BF16)
        dkpe_ref[:, DH:] = jnp.zeros((tm, DH), BF16)

    row = lambda w, cb: pl.BlockSpec((tm, w), lambda i: (i, cb))
    full = lambda a: pl.BlockSpec(a.shape, lambda i: (0, 0))
    wide = jax.ShapeDtypeStruct((s, HEADS * HD), BF16)
    return pl.pallas_call(
        body, name="mla_prep_bwd", grid=(s // tm,),
        in_specs=[ANY, row(HEADS * HD, 0), row(HEADS * HD, 0), row(WIDTH, 0),
                  row(Q_RANK, C_Q // Q_RANK), row(KV_RANK, C_KV // KV_RANK), row(1, 0), row(1, 0),
                  full(qn_w), full(kvn_w), full(wq), full(wukv), row(HD, 0), row(HD, 0), row(HD, 0),
                  row(DH, 0), row(DH, 0), row(DH, 0)],
        out_specs=[row(tail, C_Q // tail), row(HEADS * HD, 0), row(HEADS * HD, 0),
                   pl.BlockSpec((1, Q_RANK), lambda i: (0, 0)), pl.BlockSpec((1, KV_RANK), lambda i: (0, 0))],
        out_shape=[jax.ShapeDtypeStruct((s, PROJ_W), BF16), wide, wide,
                   jax.ShapeDtypeStruct((1, Q_RANK), F32), jax.ShapeDtypeStruct((1, KV_RANK), F32)],
        input_output_aliases={0: 0},
        compiler_params=_cparams(("arbitrary",)),
    )(dproj, dqq, dkk, dvv, proj, proj, rq, rkv, qn_w, kvn_w, wq, wukv, *qtab, *ktab)


def _rms_in_bwd(dxn, h, r, dh2, norm_w, s):
    def body(dxn_ref, h_ref, r_ref, dh2_ref, w_ref, gx_ref, gm_ref, dw_ref):
        i = pl.program_id(0)
        r_v = r_ref[...]
        dx, prod = _rms_bwd(dxn_ref[...], h_ref[...] * r_v, r_v, w_ref[...])
        dh = dh2_ref[...] + dx

        @pl.when(i == 0)
        def _():
            gm_ref[...] = dh
            dw_ref[...] = jnp.sum(prod, axis=0, keepdims=True)

        @pl.when(i > 0)
        def _():
            gx_ref[...] = dh
            dw_ref[...] += jnp.sum(prod, axis=0, keepdims=True)

    blk = pl.BlockSpec((BLK, D), lambda i: (i, 0))
    return pl.pallas_call(
        body, name="rms_in_bwd", grid=(s // BLK,),
        in_specs=[blk, blk, pl.BlockSpec((BLK, 1), lambda i: (i, 0)), blk, pl.BlockSpec((1, D), lambda i: (0, 0))],
        out_specs=[pl.BlockSpec((BLK, D), lambda i: (jnp.maximum(i - 1, 0), 0)), pl.BlockSpec((BLK, D), lambda i: (0, 0)),
                   pl.BlockSpec((1, D), lambda i: (0, 0))],
        out_shape=[jax.ShapeDtypeStruct((s - BLK, D), F32), jax.ShapeDtypeStruct((BLK, D), F32),
                   jax.ShapeDtypeStruct((1, D), F32)],
        compiler_params=_cparams(("arbitrary",)),
    )(dxn, h, r, dh2, norm_w)


def _adam_math(w, g, m, v):
    m = ADAM_B1 * m + (1.0 - ADAM_B1) * g
    v = ADAM_B2 * v + (1.0 - ADAM_B2) * (g * g)
    m_hat = m / (1.0 - ADAM_B1 ** ADAM_STEP)
    v_hat = v / (1.0 - ADAM_B2 ** ADAM_STEP)
    return -ADAM_LR * (m_hat / (jnp.sqrt(v_hat) + ADAM_EPS) + ADAM_WD * w), m, v


def _adamw(w, g, m, v, name):
    rows, cols = w.shape
    tr = rows
    for cand in (128, 64, 32, 16, 8):
        if rows % cand == 0:
            tr = cand
            break

    def body(w_ref, g_ref, m_ref, v_ref, d_ref, nm_ref, nv_ref):
        d_ref[...], nm_ref[...], nv_ref[...] = _adam_math(w_ref[...], g_ref[...], m_ref[...], v_ref[...])

    spec = pl.BlockSpec((tr, cols), lambda i: (i, 0))
    return pl.pallas_call(
        body, name=name, grid=(rows // tr,), in_specs=[spec] * 4, out_specs=[spec] * 3,
        out_shape=[jax.ShapeDtypeStruct((rows, cols), F32)] * 3,
        compiler_params=_cparams(("parallel",)),
    )(w, g, m, v)


def _adamw_t(w_t, g_t, m_t, v_t, name):
    cols, rows = w_t.shape

    def body(w_ref, g_ref, m_ref, v_ref, go_ref, d_ref, nm_ref, nv_ref):
        g = g_ref[...]
        go_ref[...] = g
        d_ref[...], nm_ref[...], nv_ref[...] = _adam_math(w_ref[...], g, m_ref[...], v_ref[...])

    blk = pl.BlockSpec((128, rows), lambda k: (k, 0))
    return pl.pallas_call(
        body, name=name, grid=(pl.cdiv(cols, 128),), in_specs=[blk] * 4, out_specs=[blk] * 4,
        out_shape=[jax.ShapeDtypeStruct((cols, rows), F32)] * 4, compiler_params=_cparams(("parallel",)),
    )(w_t, g_t, m_t, v_t)


def _adamw_small(w, gall, m, v):
    def body(w_ref, g_ref, m_ref, v_ref, gs_ref, d_ref, nm_ref, nv_ref):
        g = g_ref[0]
        for dev in range(1, 8):
            g = g + g_ref[dev]
        gs_ref[...] = g
        d_ref[...], nm_ref[...], nv_ref[...] = _adam_math(w_ref[...], g, m_ref[...], v_ref[...])

    return pl.pallas_call(
        body, name="adamw_small", out_shape=[jax.ShapeDtypeStruct((SM_ROWS, 128), F32)] * 4,
        compiler_params=pltpu.CompilerParams(vmem_limit_bytes=VMEM_LIMIT),
    )(w, gall, m, v)


ADD_ROWS = 464
HALF_BLOCKS = PK_HALF // ADD_ROWS


def _rs_core_add(srcs, land, pos):
    def body(pos_ref, *refs):
        src_refs, (land_ref, out_ref, own, sem) = refs[:len(srcs)], refs[len(srcs):]
        for s in range(N_CHIPS):
            for hf in range(2):
                @pl.when((pl.program_id(0) == s) & (pos_ref[1] == hf))
                def _():
                    for k, r0, p, n in _grad_half(s, hf):
                        pltpu.make_async_copy(src_refs[k].at[pl.ds(r0, n), :], own.at[pl.ds(p, n), :], sem).start()
        pltpu.make_async_copy(own, own, sem).wait()
        out_ref[0] = (own[...].astype(F32) + land_ref[0].astype(F32)).astype(BF16)

    blk = pl.BlockSpec((1, PK_HALF, D), lambda s, pos_ref: (s, 0, 0))
    return pl.pallas_call(
        body, name="rs_core_add",
        grid_spec=pltpu.PrefetchScalarGridSpec(
            num_scalar_prefetch=1, grid=(N_CHIPS,), in_specs=[ANY] * len(srcs) + [blk], out_specs=blk,
            scratch_shapes=[pltpu.VMEM((PK_HALF, D), BF16), pltpu.SemaphoreType.DMA(())]),
        out_shape=jax.ShapeDtypeStruct((N_CHIPS, PK_HALF, D), BF16),
        compiler_params=_cparams(("arbitrary",)),
    )(pos, *srcs, land)


def _rs_chip_add(cp, land, pos):
    def body(pos_ref, a_ref, l_ref, o_ref):
        o_ref[0] = ((a_ref[0].astype(F32) + l_ref[0].astype(F32)) + l_ref[1].astype(F32)) + l_ref[2].astype(F32)

    return pl.pallas_call(
        body, name="rs_chip_add",
        grid_spec=pltpu.PrefetchScalarGridSpec(
            num_scalar_prefetch=1, grid=(HALF_BLOCKS,),
            in_specs=[pl.BlockSpec((1, ADD_ROWS, D), lambda i, pos_ref: (pos_ref[0], i, 0)),
                      pl.BlockSpec((3, ADD_ROWS, D), lambda i, pos_ref: (0, i, 0))],
            out_specs=pl.BlockSpec((1, ADD_ROWS, D), lambda i, pos_ref: (pos_ref[1], i, 0))),
        out_shape=jax.ShapeDtypeStruct((2, PK_HALF, D), F32),
        compiler_params=_cparams(("parallel",)),
    )(pos, cp, land)


ANY = pl.BlockSpec(memory_space=pl.ANY)


def _mesh_pos():
    return lax.axis_index("x"), lax.axis_index("y"), lax.axis_index("c")


def _chip_peer(x, y, c, r):
    return (jnp.bitwise_xor(x, r >> 1), jnp.bitwise_xor(y, r & 1), c)


def _allgather_w_in(big0, wt, meta_loc):
    def body(big0_ref, wt_ref, meta_ref, big_ref, metaf_ref, lsem, ssem, rsem, fssem, frsem, msem_s, msem_r, osem):
        del big0_ref
        x, y, c = _mesh_pos()
        j = 2 * x + y
        me, sibling = (x, y, c), (x, y, 1 - c)

        def half_wait(s_sem, r_sem, halves=1):
            rows = big_ref.at[pl.ds(0, halves * AG_IN_HALF), :]
            return pltpu.make_async_remote_copy(src_ref=rows, dst_ref=rows, send_sem=s_sem, recv_sem=r_sem,
                                                device_id=me, device_id_type=MESH)

        own_meta = pltpu.make_async_copy(meta_ref, metaf_ref.at[j], lsem)
        own_meta.start()
        for s in range(N_CHIPS):
            for hf in range(2):
                @pl.when((j == s) & (c == hf))
                def _():
                    for r in (1, 2, 3):
                        for p, b, n in _ag_half(s, True, hf):
                            pltpu.make_async_remote_copy(
                                src_ref=wt_ref.at[pl.ds(p, n), :], dst_ref=big_ref.at[pl.ds(b, n), :],
                                send_sem=ssem.at[r - 1], recv_sem=rsem.at[r - 1],
                                device_id=_chip_peer(x, y, c, r), device_id_type=MESH).start()

            @pl.when(j == s)
            def _():
                for hf in range(2):
                    for p, b, n in _ag_half(s, True, hf):
                        pltpu.make_async_remote_copy(
                            src_ref=wt_ref.at[pl.ds(p, n), :], dst_ref=big_ref.at[pl.ds(b, n), :],
                            send_sem=osem.at[0], recv_sem=osem.at[1], device_id=sibling, device_id_type=MESH).start()

        meta_copies = [pltpu.make_async_remote_copy(
            src_ref=meta_ref, dst_ref=metaf_ref.at[j], send_sem=msem_s.at[r - 1], recv_sem=msem_r.at[r - 1],
            device_id=_chip_peer(x, y, c, r), device_id_type=MESH) for r in (1, 2, 3)]
        for cp in meta_copies:
            cp.start()
        for r in (1, 2, 3):
            half_wait(ssem.at[r - 1], rsem.at[r - 1]).wait_recv()
            src_shard = jnp.bitwise_xor(j, r)
            for s in range(N_CHIPS):
                for hf in range(2):
                    @pl.when((src_shard == s) & (c == hf))
                    def _():
                        for _, b, n in _ag_half(s, True, hf):
                            rows = big_ref.at[pl.ds(b, n), :]
                            pltpu.make_async_remote_copy(
                                src_ref=rows, dst_ref=rows, send_sem=fssem.at[r - 1], recv_sem=frsem.at[r - 1],
                                device_id=sibling, device_id_type=MESH).start()
        for r in (1, 2, 3):
            half_wait(fssem.at[r - 1], frsem.at[r - 1]).wait_recv()
        for r in (1, 2, 3):
            half_wait(ssem.at[r - 1], rsem.at[r - 1]).wait_send()
            half_wait(fssem.at[r - 1], frsem.at[r - 1]).wait_send()
        own_rows = half_wait(osem.at[0], osem.at[1], halves=2)
        own_rows.wait_recv()
        own_rows.wait_send()
        for cp in meta_copies:
            cp.wait_send()
            cp.wait_recv()
        own_meta.wait()

    dma3 = pltpu.SemaphoreType.DMA((3,))
    return pl.pallas_call(
        body, name="allgather_w_in", in_specs=[ANY, ANY, ANY], out_specs=[ANY, ANY],
        out_shape=[jax.ShapeDtypeStruct((BIG_ROWS, D), BF16), jax.ShapeDtypeStruct((N_CHIPS,) + meta_loc.shape, F32)],
        scratch_shapes=[pltpu.SemaphoreType.DMA(()), dma3, dma3, dma3, dma3, dma3, dma3, pltpu.SemaphoreType.DMA((2,))],
        input_output_aliases={0: 0},
        compiler_params=pltpu.CompilerParams(has_side_effects=True),
    )(big0, wt, meta_loc)


HBM = pl.BlockSpec(memory_space=pltpu.HBM)
SEM = pl.BlockSpec(memory_space=pltpu.SEMAPHORE)
EFFECT = pltpu.SideEffectType.DATAFLOW_SIDE_EFFECTING
REST_PEERS = [(r, dc) for r in (1, 2, 3) for dc in (0, 1)]


def _allgather_rest_start(big):
    def body(big_ref, *rest):
        s_sems, r_sems = rest[:6], rest[6:12]
        x, y, c = _mesh_pos()
        j = 2 * x + y
        for s in range(N_CHIPS):
            for hf in range(2):
                @pl.when((j == s) & (c == hf))
                def _():
                    for k, (r, dc) in enumerate(REST_PEERS):
                        for _, b, n in _ag_half(s, False, hf):
                            rows = big_ref.at[pl.ds(b, n), :]
                            pltpu.make_async_remote_copy(
                                src_ref=rows, dst_ref=rows, send_sem=s_sems[k], recv_sem=r_sems[2 * (r - 1) + hf],
                                device_id=_chip_peer(x, y, dc, r), device_id_type=MESH).start()

    sem = pltpu.SemaphoreType.DMA(())
    out = pl.pallas_call(
        body, name="allgather_rest_start", out_shape=(sem,) * 12 + (pltpu.HBM(big.shape, big.dtype),),
        in_specs=(HBM,), out_specs=(SEM,) * 12 + (HBM,), input_output_aliases={0: 12},
        compiler_params=pltpu.CompilerParams(has_side_effects=EFFECT),
    )(pltpu.with_memory_space_constraint(big, pltpu.HBM))
    return out[:12], out[12]


def _allgather_rest_wait(sems, big, after):
    def body(big_ref, *rest):
        s_sems, r_sems = rest[:6], rest[6:12]
        me = _mesh_pos()
        rows = big_ref.at[pl.ds(0, AG_REST_HALF), :]
        for k in range(6):
            copy = pltpu.make_async_remote_copy(src_ref=rows, dst_ref=rows, send_sem=s_sems[k], recv_sem=r_sems[k],
                                                device_id=me, device_id_type=MESH)
            copy.wait_send()
            copy.wait_recv()

    return pl.pallas_call(
        body, name="allgather_rest_wait", out_shape=pltpu.HBM(big.shape, big.dtype),
        in_specs=(HBM,) + (SEM,) * 12 + (pl.BlockSpec(memory_space=pl.ANY),), out_specs=HBM,
        input_output_aliases={0: 0}, compiler_params=pltpu.CompilerParams(has_side_effects=EFFECT),
    )(big, *sems, after)


RS_SPLIT = 2


def _rs_core_exchange(srcs):
    def body(*refs):
        src_refs, (land_ref, ssem, rsem) = refs[:len(srcs)], refs[len(srcs):]
        x, y, c = _mesh_pos()
        me, sibling = (x, y, c), (x, y, 1 - c)
        for hf in range(2):
            @pl.when(c == hf)
            def _():
                for s in range(N_CHIPS):
                    for k, r0, p, n in _grad_half(s, 1 - hf):
                        pltpu.make_async_remote_copy(
                            src_ref=src_refs[k].at[pl.ds(r0, n), :], dst_ref=land_ref.at[s, pl.ds(p, n), :],
                            send_sem=ssem, recv_sem=rsem, device_id=sibling, device_id_type=MESH).start()
        whole = pltpu.make_async_remote_copy(src_ref=land_ref, dst_ref=land_ref, send_sem=ssem, recv_sem=rsem,
                                             device_id=me, device_id_type=MESH)
        whole.wait_recv()
        whole.wait_send()

    return pl.pallas_call(
        body, name="rs_core_exchange", in_specs=[ANY] * len(srcs), out_specs=ANY,
        out_shape=jax.ShapeDtypeStruct((N_CHIPS, PK_HALF, D), BF16),
        scratch_shapes=[pltpu.SemaphoreType.DMA(())] * 2,
        compiler_params=pltpu.CompilerParams(has_side_effects=True),
    )(*srcs)


def _rs_chip_start(cp):
    def body(cp_ref, land_ref, s1, s2, s3, r1, r2, r3, cp_thru, land_thru, token):
        x, y, c = _mesh_pos()
        j = 2 * x + y
        for r, s_sem, r_sem in zip((1, 2, 3), (s1, s2, s3), (r1, r2, r3)):
            pltpu.make_async_remote_copy(
                src_ref=cp_ref.at[jnp.bitwise_xor(j, r)], dst_ref=land_ref.at[r - 1], send_sem=s_sem, recv_sem=r_sem,
                device_id=_chip_peer(x, y, c, r), device_id_type=MESH).start()
        token[...] = jnp.zeros_like(token)

    land_shape = (3, PK_HALF, D)
    sem = pltpu.SemaphoreType.DMA(())
    out = pl.pallas_call(
        body, name="rs_chip_start",
        out_shape=(sem,) * 6 + (pltpu.HBM(cp.shape, cp.dtype), pltpu.HBM(land_shape, BF16),
                                jax.ShapeDtypeStruct((8, 128), F32)),
        in_specs=(HBM, HBM), out_specs=(SEM,) * 6 + (HBM, HBM, pl.BlockSpec(memory_space=pltpu.VMEM)),
        input_output_aliases={0: 6, 1: 7},
        compiler_params=pltpu.CompilerParams(has_side_effects=EFFECT),
    )(pltpu.with_memory_space_constraint(cp, pltpu.HBM),
      pltpu.with_memory_space_constraint(lax.empty(land_shape, BF16), pltpu.HBM))
    return out[:6], out[6], out[7], out[8]


def _rs_chip_wait(sems, cp_thru, land_thru, after):
    def body(cp_ref, land_ref, s1, s2, s3, r1, r2, r3, after_ref, cp_out, land_out):
        me = _mesh_pos()
        for r, s_sem, r_sem in zip((1, 2, 3), (s1, s2, s3), (r1, r2, r3)):
            copy = pltpu.make_async_remote_copy(src_ref=cp_ref.at[0], dst_ref=land_ref.at[r - 1], send_sem=s_sem,
                                                recv_sem=r_sem, device_id=me, device_id_type=MESH)
            copy.wait_send()
            copy.wait_recv()

    return pl.pallas_call(
        body, name="rs_chip_wait",
        out_shape=(pltpu.HBM(cp_thru.shape, cp_thru.dtype), pltpu.HBM(land_thru.shape, land_thru.dtype)),
        in_specs=(HBM, HBM) + (SEM,) * 6 + (pl.BlockSpec(memory_space=pl.ANY),), out_specs=(HBM, HBM),
        input_output_aliases={0: 0, 1: 1},
        compiler_params=pltpu.CompilerParams(has_side_effects=EFFECT),
    )(cp_thru, land_thru, *sems, after)


def _rs_finish(full, small):
    n = PK_HALF // RS_SPLIT

    def body(full_in_ref, sm_ref, full_ref, all_ref, lsem, ssem, rsem, sm_s, sm_r):
        del full_in_ref
        x, y, c = _mesh_pos()
        me, sibling = (x, y, c), (x, y, 1 - c)
        my_id = 4 * x + 2 * y + c
        own_sm = pltpu.make_async_copy(sm_ref, all_ref.at[my_id], lsem)
        own_sm.start()
        for q in range(RS_SPLIT):
            rows = full_ref.at[c, pl.ds(q * n, n), :]
            pltpu.make_async_remote_copy(src_ref=rows, dst_ref=rows, send_sem=ssem, recv_sem=rsem,
                                         device_id=sibling, device_id_type=MESH).start()
        half = pltpu.make_async_remote_copy(src_ref=full_ref.at[c], dst_ref=full_ref.at[c], send_sem=ssem,
                                            recv_sem=rsem, device_id=me, device_id_type=MESH)
        smalls = [pltpu.make_async_remote_copy(
            src_ref=sm_ref, dst_ref=all_ref.at[my_id], send_sem=sm_s.at[r - 1], recv_sem=sm_r.at[r - 1],
            device_id=(jnp.bitwise_xor(x, r >> 2), jnp.bitwise_xor(y, (r >> 1) & 1), jnp.bitwise_xor(c, r & 1)),
            device_id_type=MESH) for r in range(1, 8)]
        for cpy in smalls:
            cpy.start()
        half.wait_recv()
        half.wait_send()
        for cpy in smalls:
            cpy.wait_recv()
        for cpy in smalls:
            cpy.wait_send()
        own_sm.wait()

    dma7 = pltpu.SemaphoreType.DMA((7,))
    return pl.pallas_call(
        body, name="rs_finish", in_specs=[ANY, ANY], out_specs=[ANY, ANY],
        out_shape=[jax.ShapeDtypeStruct((2, PK_HALF, D), F32), jax.ShapeDtypeStruct((8, SM_ROWS, 128), F32)],
        scratch_shapes=[pltpu.SemaphoreType.DMA(()), pltpu.SemaphoreType.DMA(()), pltpu.SemaphoreType.DMA(()), dma7, dma7],
        input_output_aliases={0: 0},
        compiler_params=pltpu.CompilerParams(has_side_effects=True),
    )(full, small)


def _rope_tables(s):
    pos = jnp.arange(s, dtype=F32) - PAD_FRONT

    def cs(d):
        inv = ROPE_BASE ** (-jnp.arange(0, d, 2, dtype=F32) / d)
        ang = pos[:, None] * inv[None, :]
        return jnp.cos(ang), jnp.sin(ang)

    c, sn = cs(ROPE_D)
    z = jnp.zeros_like(c)
    ktab = (jnp.concatenate([c, c, z, z], axis=1), jnp.concatenate([-sn, z, z, z], axis=1),
            jnp.concatenate([z, sn, z, z], axis=1))
    one = jnp.ones_like(c)
    qtab = tuple(ATT_SCALE * jnp.concatenate(parts, axis=1) for parts in (
        [one, one, one, one, c, c, z, z], [z, z, z, z, -sn, z, z, z], [z, z, z, z, z, sn, z, z]))
    lane = jnp.arange(HD) == DH + ROPE_D
    qbias = lane.astype(F32)[None, :]
    kbias = jnp.where((pos < 0)[:, None] & lane[None, DH:], NEG_INF, 0.0).astype(F32)
    c, sn = cs(DH)
    return qtab, ktab, qbias, kbias, jnp.concatenate([c, c], axis=1), jnp.concatenate([-sn, sn], axis=1)


def _grad_mm(a, b, m, n, s, name):
    return _matmul(a, b, "tn", m, n, s, min(m, 512), min(n, 1024), s, BF16, name, n_outer=True)


def _local_step(x2, target2, hp0, big, small_w, gather_wait=None, reduce_start=None, reduce_wait=None):
    norm_w, qn_w, kvn_w, gn_w, gn_b, fin_w = small_w
    s = x2.shape[0] + BLK
    tm = _row_tile(s, 1408)
    qtab, ktab, qbias, kbias, c128, s128 = _rope_tables(s)
    consts, gamma = _ret_consts()

    h, xn, r1 = _rms_in(x2, hp0, norm_w, s)
    proj = _matmul(xn, big, "nt", s, PROJ_W, D, tm, 512, D, BF16, "proj", b_off=(BIG_IN // 512, 0))
    if gather_wait is not None:
        big = gather_wait(big, proj)
    uq = big[BIG_UQ:BIG_UQ + 384].reshape(HEADS, DH + ROPE_D, Q_RANK)
    wq = jnp.pad(uq, ((0, 0), (0, HD - DH - ROPE_D), (0, 0))).reshape(HEADS * HD, Q_RANK)
    wukv = big[BIG_UKV:BIG_UKV + 256].reshape(2 * WIDTH, KV_RANK)
    wbm = big[BIG_BM:BIG_BM + 1024].reshape(D, WIDTH)
    wbr = big[BIG_BR:BIG_BR + 1024].reshape(D, WIDTH)
    cqn, rq, ckvn, rkv, qq, kk, vv = _mla_prep(proj, qn_w, kvn_w, wq, wukv, qtab, ktab, qbias, kbias, s)
    o_mla, y_mla, lse = _attn_fwd(qq, kk, vv, proj, s)
    y_ret, on, rstd, qr, kr, rall = _ret_fwd(proj, gn_w, gn_b, c128, s128, consts, gamma, s)
    u_mla, u_ret, merged = _merge_fwd(y_mla, y_ret, wbm, wbr, proj, s)
    h2 = _out_fwd(merged, big, h, s)
    dh2, dh2b, loss_blk, dfin = _loss_bwd(h2, target2, fin_w, s)

    dproj, du_mla, du_ret = _merge_bwd(dh2b, big, u_mla, u_ret, proj, s)
    g_out = _grad_mm(merged, dh2b, D, D, s, "grad_w_out")
    dproj, do_mla, delta = _branch_mla_bwd(dproj, du_mla, wbm, o_mla, proj, s)
    g_bm = _grad_mm(du_mla, y_mla, D, WIDTH, s, "grad_w_branch_mla")
    dproj, do_ret, dgw, dgb = _branch_ret_bwd(dproj, du_ret, wbr, on, rstd, gn_w, gn_b, proj, s)
    g_br = _grad_mm(du_ret, y_ret, D, WIDTH, s, "grad_w_branch_ret")
    dproj = _ret_bwd(dproj, qr, kr, proj, rall, do_ret, c128, s128, consts, gamma, s)
    dqq, dkk, dvv = _attn_bwd(qq, kk, vv, do_mla, lse, delta.reshape(lse.shape), s)
    dproj, dq, dkv, dqnw, dkvnw = _mla_prep_bwd(dproj, dqq, dkk, dvv, proj, rq, rkv, qn_w, kvn_w, wq, wukv,
                                                qtab, ktab, s)
    g_q = _grad_mm(dq, cqn, HEADS * HD, Q_RANK, s, "grad_w_uq")
    g_ukv = _grad_mm(dkv, ckvn, 2 * WIDTH, KV_RANK, s, "grad_w_ukv")
    g_in = _grad_mm(dproj, xn, PROJ_W, D, s, "grad_w_in")

    g_uq = g_q.reshape(HEADS, HD, Q_RANK)[:, :DH + ROPE_D]
    grads = (g_in, g_out, g_bm.reshape(1024, D), g_br.reshape(1024, D), g_uq.reshape(384, D), g_ukv.reshape(256, D))
    token, travelling = (None, grads) if reduce_start is None else reduce_start(grads)
    dxn = _matmul(dproj, big, "nn", s, D, PROJ_W, _row_tile(s, 384), 512, PROJ_W, F32, "dxn",
                  b_off=(BIG_IN // PROJ_W, 0), n_outer=True, after=token)
    grad_x, gmeta_blk, dnorm = _rms_in_bwd(dxn, h, r1, dh2, norm_w, s)
    small = _small_rows((dnorm, dqnw, dkvnw, dgw, dgb, dfin), loss_blk, gmeta_blk[PAD_FRONT:])
    return grad_x, travelling if reduce_wait is None else reduce_wait(travelling, dnorm), small


def _small_rows(ws, first=None, last=None):
    def part(a, rows):
        a = a.reshape(-1, 128)
        return a if a.shape[0] == rows else jnp.pad(a, ((0, rows - a.shape[0]), (0, 0)))

    bounds = (SM_NORM, SM_QN, SM_KVN, SM_GNW, SM_GNB, SM_FIN, SM_META)
    first = jnp.zeros((SM_NORM, 128), F32) if first is None else first
    last = jnp.zeros((SM_ROWS - SM_META, 128), F32) if last is None else last
    return jnp.concatenate([part(first, SM_NORM)] + [part(w, bounds[k + 1] - bounds[k]) for k, w in enumerate(ws)]
                           + [part(last, SM_ROWS - SM_META)], axis=0)


def kernel(x, meta, norm_w, w_in, mla_q_norm_w, mla_w_uq, mla_kv_norm_w, mla_w_ukv, ret_gn_w, ret_gn_b, w_branch_mla, w_branch_ret, w_out, final_norm_w, loss_target, m_meta, m_norm_w, m_w_in, m_mla_q_norm_w, m_mla_w_uq, m_mla_kv_norm_w, m_mla_w_ukv, m_ret_gn_w, m_ret_gn_b, m_w_branch_mla, m_w_branch_ret, m_w_out, m_final_norm_w, v_meta, v_norm_w, v_w_in, v_mla_q_norm_w, v_mla_w_uq, v_mla_kv_norm_w, v_mla_w_ukv, v_ret_gn_w, v_ret_gn_b, v_w_branch_mla, v_w_branch_ret, v_w_out, v_final_norm_w):
    j = 2 * lax.axis_index("x") + lax.axis_index("y")
    tr = lambda w: w[0].T.reshape(-1, D).astype(BF16)
    pos = jnp.stack([j, lax.axis_index("c")]).astype(jnp.int32)
    put = lambda buf, rows, off: lax.dynamic_update_slice(buf, rows, (off, 0))
    big0 = lax.empty((BIG_ROWS, D), BF16)
    big0 = put(big0, w_out[0].astype(BF16), BIG_OUT + 512 * j)
    big0 = put(big0, tr(w_branch_mla), BIG_BM + 256 * j)
    big0 = put(big0, tr(w_branch_ret), BIG_BR + 256 * j)
    big0 = put(big0, tr(mla_w_uq), BIG_UQ + 96 * j)
    big0 = put(big0, tr(mla_w_ukv), BIG_UKV + 64 * j)
    big0 = put(big0, jnp.zeros((AG_JUNK_END - AG_JUNK_REST, D), BF16), AG_JUNK_REST)
    big0 = put(big0, jnp.zeros((ZERO_ROWS, D), BF16), BIG_IN + IN_WIDTH)
    wt = jnp.concatenate([tr(w_in), jnp.zeros((2 * AG_IN_HALF - IN_SHARD, D), BF16)], axis=0)
    big, meta_all = _allgather_w_in(big0, wt, meta)
    gather_sems, big = _allgather_rest_start(big)

    def gather_wait(big_travelling, after):
        return _allgather_rest_wait(gather_sems, big_travelling, after)

    meta_full = meta_all.transpose(1, 0, 2).reshape(N_META, D)
    hp0 = jnp.concatenate([jnp.zeros((PAD_FRONT, D), F32), meta_full], axis=0)
    small_w = (norm_w, mla_q_norm_w, mla_kv_norm_w, ret_gn_w, ret_gn_b, final_norm_w.reshape(1, D))

    def reduce_start(grads):
        chip_part = _rs_core_add(grads, _rs_core_exchange(grads), pos)
        sems, part_thru, land_thru, token = _rs_chip_start(chip_part)
        return token, (sems, part_thru, land_thru)

    def reduce_wait(state, after):
        return _rs_chip_wait(*state, after)

    grad_x, (chip_part, land2), small = _local_step(x[0], loss_target[0], hp0, big, small_w, gather_wait,
                                                    reduce_start, reduce_wait)
    full, small_all = _rs_finish(_rs_chip_add(chip_part, land2, pos), small)
    full = full.reshape(PK_ROWS, D)

    untr = lambda lo, hi, rows: full[lo:hi].reshape(rows, -1).T
    grads = {
        "w_out": full[PK_OUT:PK_BM], "w_branch_mla": untr(PK_BM, PK_BR, 512), "w_branch_ret": untr(PK_BR, PK_UQ, 512),
        "mla_w_uq": untr(PK_UQ, PK_UKV, 384), "mla_w_ukv": untr(PK_UKV, PK_PAD, 512),
    }
    big_w = {"mla_w_uq": (mla_w_uq, m_mla_w_uq, v_mla_w_uq),
             "mla_w_ukv": (mla_w_ukv, m_mla_w_ukv, v_mla_w_ukv),
             "w_branch_mla": (w_branch_mla, m_w_branch_mla, v_w_branch_mla),
             "w_branch_ret": (w_branch_ret, m_w_branch_ret, v_w_branch_ret), "w_out": (w_out, m_w_out, v_w_out)}
    res = {"w_in": tuple(a.T[None] for a in _adamw_t(w_in[0].T, full, m_w_in[0].T, v_w_in[0].T, "adamw_w_in"))}
    for name, (w, m, v) in big_w.items():
        d, nm, nv = _adamw(w[0], grads[name], m[0], v[0], "adamw_" + name)
        res[name] = (grads[name][None], d[None], nm[None], nv[None])

    small_m = (m_norm_w, m_mla_q_norm_w, m_mla_kv_norm_w, m_ret_gn_w, m_ret_gn_b, m_final_norm_w.reshape(1, D))
    small_v = (v_norm_w, v_mla_q_norm_w, v_mla_kv_norm_w, v_ret_gn_w, v_ret_gn_b, v_final_norm_w.reshape(1, D))
    gs, ds, ms, vs = _adamw_small(_small_rows(small_w), small_all, _small_rows(small_m), _small_rows(small_v))
    names = ["norm_w", "mla_q_norm_w", "mla_kv_norm_w", "ret_gn_w", "ret_gn_b", "final_norm_w"]
    bounds = [SM_NORM, SM_QN, SM_KVN, SM_GNW, SM_GNB, SM_FIN]
    for k, name in enumerate(names):
        shape = (D,) if name == "final_norm_w" else (1, -1)
        rows = small_w[k].size // 128
        res[name] = tuple(a[bounds[k]:bounds[k] + rows].reshape(shape) for a in (gs, ds, ms, vs))
    g_meta = lax.dynamic_slice_in_dim(gs[SM_META:SM_META + 256].reshape(N_META, D), j * (D // N_CHIPS), D // N_CHIPS, axis=1)
    res["meta"] = (g_meta,) + tuple(_adamw(meta, g_meta, m_meta, v_meta, "adamw_meta"))

    order = ["meta", "norm_w", "w_in", "mla_q_norm_w", "mla_w_uq", "mla_kv_norm_w", "mla_w_ukv", "ret_gn_w", "ret_gn_b",
             "w_branch_mla", "w_branch_ret", "w_out", "final_norm_w"]
    return (gs[0, 0], grad_x[None]) + tuple(res[n][k] for k in range(4) for n in order)
```

```python
import functools
import math

import numpy as np
import jax
import jax.numpy as jnp
from jax import lax
from jax.experimental import pallas as pl
from jax.experimental.pallas import tpu as pltpu

F32 = jnp.float32
BF16 = jnp.bfloat16
MESH = pl.DeviceIdType.MESH

D = 2048
N_META = 16
BLK = 128
PAD_FRONT = BLK - N_META
HEADS = 8
DH = 128
ROPE_D = 64
Q_RANK = 512
KV_RANK = 256
WIDTH = HEADS * DH
ROPE_BASE = 10000.0
NORM_EPS = 1e-6
GN_EPS = 1e-5
NEG_INF = -1e30
ATT_SCALE = (DH + ROPE_D) ** -0.5
RET_SCALE = DH ** -0.5
IN_WIDTH = 10048
N_CHIPS = 4
IN_SHARD = IN_WIDTH // N_CHIPS
ADAM_LR, ADAM_B1, ADAM_B2, ADAM_EPS, ADAM_WD, ADAM_STEP = 0.001, 0.9, 0.999, 1e-08, 0.01, 10

R_Q, R_K, R_V, Z_RET, Z_MLA, GATE0, GATE1 = 0, 1024, 2048, 3072, 4096, 5120, 5120 + D
C_Q = 5120 + 2 * D
C_KV = C_Q + Q_RANK
K_PE = C_KV + KV_RANK
PROJ_W = 10240
IN_RUNS = ((0, 832, C_Q), (832, 1856, Z_MLA), (1856, 4928, R_Q), (4928, 5952, Z_RET), (5952, IN_WIDTH, GATE0))

BIG_IN, BIG_OUT, BIG_BM, BIG_BR, BIG_UQ, BIG_UKV, BIG_JUNK, BIG_ROWS = 0, 10240, 12288, 13312, 14336, 14720, 14976, 15360
ZERO_ROWS = PROJ_W - IN_WIDTH
PK_IN, PK_OUT, PK_BM, PK_BR, PK_UQ, PK_UKV, PK_PAD, PK_ROWS = 0, 2512, 3024, 3280, 3536, 3632, 3696, 3712
PK_HALF = PK_ROWS // 2
SM_LOSS, SM_NORM, SM_QN, SM_KVN, SM_GNW, SM_GNB, SM_FIN, SM_META, SM_ROWS = 0, 8, 24, 32, 40, 48, 56, 72, 328

VMEM_LIMIT = 56 * 1024 * 1024


def _pieces(shard):
    j = shard
    out = [(PK_OUT, BIG_OUT + 512 * j, 512), (PK_BM, BIG_BM + 256 * j, 256), (PK_BR, BIG_BR + 256 * j, 256),
           (PK_UQ, BIG_UQ + 96 * j, 96), (PK_UKV, BIG_UKV + 64 * j, 64)]
    for lo, hi, new in IN_RUNS:
        a, b = max(lo, IN_SHARD * j), min(hi, IN_SHARD * (j + 1))
        if b > a:
            out.append((PK_IN + a - IN_SHARD * j, BIG_IN + new + a - lo, b - a))
    out += [(PK_PAD, BIG_JUNK + 16 * j, 16)]
    return out


AG_JUNK_REST, AG_JUNK_IN, AG_JUNK_END = BIG_JUNK, BIG_JUNK + 128, BIG_JUNK + 192
AG_REST_HALF, AG_IN_HALF = 608, 1264


def _ag_half(shard, w_in, half):
    pieces = [(b, n) for p, b, n in _pieces(shard) if p < PK_PAD and (p < PK_OUT) == w_in]
    pieces.append((AG_JUNK_IN + 16 * shard, 16) if w_in else (AG_JUNK_REST + 32 * shard, 32))
    size = AG_IN_HALF if w_in else AG_REST_HALF
    out, pos = [], 0
    for b, n in pieces:
        s, e = max(pos, half * size), min(pos + n, (half + 1) * size)
        if e > s:
            out.append((s, b + s - pos, e - s))
        pos += n
    assert pos == 2 * size
    return out


def _grad_half(shard, half):
    j = shard
    pieces = [(0, b - BIG_IN, p, n) for p, b, n in _pieces(j) if p < PK_OUT]
    pieces += [(1, 512 * j, PK_OUT, 512), (2, 256 * j, PK_BM, 256), (3, 256 * j, PK_BR, 256), (4, 96 * j, PK_UQ, 96),
               (5, 64 * j, PK_UKV, 64), (0, IN_WIDTH, PK_PAD, PK_ROWS - PK_PAD)]
    lo, hi = half * PK_HALF, (half + 1) * PK_HALF
    out = []
    for k, r0, p, n in pieces:
        s, e = max(p, lo), min(p + n, hi)
        if e > s:
            out.append((k, r0 + s - p, s - lo, e - s))
    assert sum(n for _, _, _, n in out) == PK_HALF
    return out


def _row_tile(rows, cap):
    best = BLK
    for t in range(BLK, cap + 1, BLK):
        if rows % t == 0:
            best = t
    return best


def _cparams(sem):
    return pltpu.CompilerParams(dimension_semantics=sem, vmem_limit_bytes=VMEM_LIMIT)


def _dot(a, b, form):
    dn = {"nt": (((1,), (1,)), ((), ())), "nn": (((1,), (0,)), ((), ())), "tn": (((0,), (0,)), ((), ()))}[form]
    return lax.dot_general(a, b, dn, preferred_element_type=F32)


def _sigmoid(v):
    return 1.0 / (1.0 + jnp.exp(-v))


def _matmul(a, b, form, m, n, k, tm, tn, tk, out_dtype, name, a_off=(0, 0), b_off=(0, 0), n_outer=False, after=None):
    nk = k // tk
    gi, gj = m // tm, n // tn

    def ij(g0, g1):
        return (g1, g0) if n_outer else (g0, g1)

    if form == "nt":
        a_spec = pl.BlockSpec((tm, tk), lambda g0, g1, kk: (ij(g0, g1)[0] + a_off[0], kk + a_off[1]))
        b_spec = pl.BlockSpec((tn, tk), lambda g0, g1, kk: (ij(g0, g1)[1] + b_off[0], kk + b_off[1]))
    elif form == "nn":
        a_spec = pl.BlockSpec((tm, tk), lambda g0, g1, kk: (ij(g0, g1)[0] + a_off[0], kk + a_off[1]))
        b_spec = pl.BlockSpec((tk, tn), lambda g0, g1, kk: (kk + b_off[0], ij(g0, g1)[1] + b_off[1]))
    else:
        a_spec = pl.BlockSpec((tk, tm), lambda g0, g1, kk: (kk + a_off[0], ij(g0, g1)[0] + a_off[1]))
        b_spec = pl.BlockSpec((tk, tn), lambda g0, g1, kk: (kk + b_off[0], ij(g0, g1)[1] + b_off[1]))
    o_spec = pl.BlockSpec((tm, tn), lambda g0, g1, kk: ij(g0, g1))

    def body(a_ref, b_ref, *rest):
        o_ref, *acc = rest[0 if after is None else 1:]
        p = _dot(a_ref[...], b_ref[...], form)
        if nk == 1:
            o_ref[...] = p.astype(o_ref.dtype)
        else:
            acc_ref, = acc
            kk = pl.program_id(2)

            @pl.when(kk == 0)
            def _():
                acc_ref[...] = p

            @pl.when(kk > 0)
            def _():
                acc_ref[...] += p

            @pl.when(kk == nk - 1)
            def _():
                o_ref[...] = acc_ref[...].astype(o_ref.dtype)

    extra = [] if after is None else [after]
    return pl.pallas_call(
        body, name=name, grid=(gj, gi, nk) if n_outer else (gi, gj, nk),
        in_specs=[a_spec, b_spec] + [pl.BlockSpec(memory_space=pl.ANY)] * len(extra), out_specs=o_spec,
        out_shape=jax.ShapeDtypeStruct((m, n), out_dtype),
        scratch_shapes=[] if nk == 1 else [pltpu.VMEM((tm, tn), F32)],
        compiler_params=_cparams(("parallel", "parallel", "arbitrary")),
    )(a, b, *extra)


def _rms_in(x, hp0, norm_w, s):
    def body(x_ref, hp0_ref, w_ref, h_ref, xn_ref, r_ref):
        def run(hv):
            r = lax.rsqrt(jnp.mean(hv * hv, axis=-1, keepdims=True) + NORM_EPS)
            h_ref[...] = hv
            xn_ref[...] = (hv * r * w_ref[...]).astype(BF16)
            r_ref[...] = r

        @pl.when(pl.program_id(0) == 0)
        def _():
            run(hp0_ref[...])

        @pl.when(pl.program_id(0) > 0)
        def _():
            run(x_ref[...])

    return pl.pallas_call(
        body, name="rms_in", grid=(s // BLK,),
        in_specs=[pl.BlockSpec((BLK, D), lambda i: (jnp.maximum(i - 1, 0), 0)),
                  pl.BlockSpec((BLK, D), lambda i: (0, 0)), pl.BlockSpec((1, D), lambda i: (0, 0))],
        out_specs=[pl.BlockSpec((BLK, D), lambda i: (i, 0)), pl.BlockSpec((BLK, D), lambda i: (i, 0)),
                   pl.BlockSpec((BLK, 1), lambda i: (i, 0))],
        out_shape=[jax.ShapeDtypeStruct((s, D), F32), jax.ShapeDtypeStruct((s, D), BF16),
                   jax.ShapeDtypeStruct((s, 1), F32)],
        compiler_params=_cparams(("arbitrary",)),
    )(x, hp0, norm_w)


def _rope64(t, c, sa, sb):
    return t * c + pltpu.roll(t, t.shape[1] - ROPE_D // 2, 1) * sa + pltpu.roll(t, ROPE_D // 2, 1) * sb


def _rope128(t, c, sg):
    return t * c + pltpu.roll(t, DH // 2, 1) * sg


HD = 2 * DH


def _mla_prep(proj, qn_w, kvn_w, wq, wukv, qtab, ktab, qbias, kbias, s):
    tm = _row_tile(s, 384)

    def body(cq_ref, ckv_ref, kpe_ref, qnw_ref, kvnw_ref, wq_ref, wukv_ref, qc_ref, qa_ref, qb_ref,
             kc_ref, ka_ref, kb_ref, qbias_ref, kbias_ref, cqn_ref, rq_ref, ckvn_ref, rkv_ref, qq_ref, kk_ref, vv_ref):
        cq = cq_ref[...].astype(F32)
        rq = lax.rsqrt(jnp.mean(cq * cq, axis=-1, keepdims=True) + NORM_EPS)
        cqn = (cq * rq * qnw_ref[...]).astype(BF16)
        ckv = ckv_ref[...].astype(F32)
        rkv = lax.rsqrt(jnp.mean(ckv * ckv, axis=-1, keepdims=True) + NORM_EPS)
        ckvn = (ckv * rkv * kvnw_ref[...]).astype(BF16)
        cqn_ref[...] = cqn
        rq_ref[...] = rq
        ckvn_ref[...] = ckvn
        rkv_ref[...] = rkv
        q = _dot(cqn, wq_ref[...], "nt")
        kv = _dot(ckvn, wukv_ref[...], "nt")
        kp = (_rope64(kpe_ref[...].astype(F32), kc_ref[...], ka_ref[...], kb_ref[...]) + kbias_ref[...]).astype(BF16)
        qc, qa, qb, qbias = qc_ref[...], qa_ref[...], qb_ref[...], qbias_ref[...]
        ones = jnp.ones((tm, DH), BF16)
        for h in range(HEADS):
            lo, mid, hi = h * HD, h * HD + DH, (h + 1) * HD
            qq_ref[:, lo:hi] = (_rope64(q[:, lo:hi], qc, qa, qb) + qbias).astype(BF16)
            kk_ref[:, lo:mid] = kv[:, lo:mid].astype(BF16)
            kk_ref[:, mid:hi] = kp
            vv_ref[:, lo:mid] = kv[:, mid:hi].astype(BF16)
            vv_ref[:, mid:hi] = ones

    row = lambda w, cb: pl.BlockSpec((tm, w), lambda i: (i, cb))
    full = lambda a: pl.BlockSpec(a.shape, lambda i: (0, 0))
    wide = jax.ShapeDtypeStruct((s, HEADS * HD), BF16)
    return pl.pallas_call(
        body, name="mla_prep", grid=(s // tm,),
        in_specs=[row(Q_RANK, C_Q // Q_RANK), row(KV_RANK, C_KV // KV_RANK), row(DH, K_PE // DH),
                  full(qn_w), full(kvn_w), full(wq), full(wukv), row(HD, 0), row(HD, 0), row(HD, 0),
                  row(DH, 0), row(DH, 0), row(DH, 0), full(qbias), row(DH, 0)],
        out_specs=[row(Q_RANK, 0), row(1, 0), row(KV_RANK, 0), row(1, 0), row(HEADS * HD, 0), row(HEADS * HD, 0),
                   row(HEADS * HD, 0)],
        out_shape=[jax.ShapeDtypeStruct((s, Q_RANK), BF16), jax.ShapeDtypeStruct((s, 1), F32),
                   jax.ShapeDtypeStruct((s, KV_RANK), BF16), jax.ShapeDtypeStruct((s, 1), F32), wide, wide, wide],
        compiler_params=_cparams(("parallel",)),
    )(proj, proj, proj, qn_w, kvn_w, wq, wukv, *qtab, *ktab, qbias, kbias)


def _diag_mask(t):
    return lax.broadcasted_iota(jnp.int32, (t, t), 0) <= lax.broadcasted_iota(jnp.int32, (t, t), 1)


def _silu(z):
    return z * _sigmoid(z)


def _attn_fwd(qq, kk, vv, proj, s):
    t = _row_tile(s, 384)
    n = s // t

    def body(q_ref, k_ref, v_ref, z_ref, o_ref, y_ref, lse_ref, acc_ref, m_ref):
        qi = pl.program_id(1)
        q = q_ref[...]
        m_ref[...] = jnp.full(m_ref.shape, NEG_INF, F32)
        acc_ref[...] = jnp.zeros(acc_ref.shape, F32)

        def keys(ki):
            return pl.ds(pl.multiple_of(ki * t, t), t)

        def scores(ki):
            return _dot(k_ref[keys(ki), :], q, "nt")

        def tile(ki, st):
            m_old = m_ref[...]
            m_new = jnp.maximum(m_old, jnp.max(st, axis=0, keepdims=True))
            pt = jnp.exp(st - m_new).astype(BF16)
            acc_ref[...] = jnp.exp(m_old - m_new) * acc_ref[...] + _dot(v_ref[keys(ki), :], pt, "tn")
            m_ref[...] = m_new

        def inner(ki, st):
            st_after = scores(ki + 1)
            tile(ki, st)
            return st_after

        st_diag = lax.fori_loop(0, qi, inner, scores(0))
        tile(qi, jnp.where(_diag_mask(t), st_diag, NEG_INF))

        l = acc_ref[DH:DH + 1, :]
        o = (acc_ref[:DH, :] / l).T
        o_ref[...] = o.astype(BF16)
        y_ref[...] = (o * _silu(z_ref[...].astype(F32))).astype(BF16)
        lse_ref[0, 0] = m_ref[...] + jnp.log(l)

    qtile = pl.BlockSpec((t, DH), lambda h, i: (i, h))
    head = pl.BlockSpec((s, HD), lambda h, i: (0, h))
    return pl.pallas_call(
        body, name="attn_fwd", grid=(HEADS, s // t),
        in_specs=[pl.BlockSpec((t, HD), lambda h, i: (i, h)), head, head,
                  pl.BlockSpec((t, DH), lambda h, i: (i, Z_MLA // DH + h))],
        out_specs=[qtile, qtile, pl.BlockSpec((1, 1, 1, t), lambda h, i: (h, i, 0, 0))],
        out_shape=[jax.ShapeDtypeStruct((s, WIDTH), BF16), jax.ShapeDtypeStruct((s, WIDTH), BF16),
                   jax.ShapeDtypeStruct((HEADS, s // t, 1, t), F32)],
        scratch_shapes=[pltpu.VMEM((HD, t), F32), pltpu.VMEM((1, t), F32)],
        compiler_params=_cparams(("parallel", "arbitrary")),
    )(qq, kk, vv, proj)


def _ret_consts():
    log_g = np.log1p(-(2.0 ** (-5.0 - np.arange(HEADS, dtype=np.float64))))
    n = np.arange(BLK, dtype=np.float64)
    diff = n[:, None] - n[None, :]
    decay = np.where(diff >= 0, np.exp(log_g[:, None, None] * np.maximum(diff, 0.0)), 0.0)
    zeta = np.exp(log_g[:, None] * (BLK - 1.0 - n))[:, :, None]
    xi = np.exp(log_g[:, None] * (n + 1.0))[:, :, None]
    gamma = [float(np.float32(np.exp(g * BLK))) for g in log_g]
    return (jnp.asarray(decay, F32), jnp.asarray(zeta, F32), jnp.asarray(xi, F32)), gamma


def _ret_fwd(proj, gn_w, gn_b, c128, s128, consts, gamma, s):
    nb = s // BLK
    decay, zeta, xi = consts

    def body(rq_ref, rk_ref, rv_ref, z_ref, gw_ref, gb_ref, c_ref, s_ref, dm_ref, zt_ref, xi_ref,
             y_ref, on_ref, rstd_ref, qr_ref, kr_ref, rall_ref, state):
        @pl.when(pl.program_id(0) == 0)
        def _():
            state[...] = jnp.zeros_like(state)

        c, sg = c_ref[...], s_ref[...]
        for h in range(HEADS):
            sl = slice(h * DH, (h + 1) * DH)
            q = _rope128(rq_ref[:, sl].astype(F32), c, sg).astype(BF16)
            kf = _rope128(rk_ref[:, sl].astype(F32), c, sg) * RET_SCALE
            k = kf.astype(BF16)
            v = rv_ref[:, sl]
            qr_ref[:, sl] = q
            kr_ref[:, sl] = k
            r_prev = state[h]
            rall_ref[0, h] = r_prev
            a = _dot(q, k, "nt") * dm_ref[h]
            o = _dot(a.astype(BF16), v, "nn") + _dot(q, r_prev.astype(BF16), "nn") * xi_ref[h]
            state[h] = r_prev * gamma[h] + _dot((kf * zt_ref[h]).astype(BF16), v, "tn")
            mu = jnp.mean(o, axis=-1, keepdims=True)
            var = jnp.mean(jnp.square(o - mu), axis=-1, keepdims=True)
            rstd = lax.rsqrt(var + GN_EPS)
            on = (o - mu) * rstd
            rstd_ref[h] = rstd
            on_ref[:, sl] = on.astype(BF16)
            ogn = on * gw_ref[:, sl] + gb_ref[:, sl]
            y_ref[:, sl] = (ogn * _silu(z_ref[:, sl].astype(F32))).astype(BF16)

    seg = lambda cb: pl.BlockSpec((BLK, WIDTH), lambda i: (i, cb))
    full = lambda a: pl.BlockSpec(a.shape, lambda i: (0,) * a.ndim)
    tab = pl.BlockSpec((BLK, DH), lambda i: (i, 0))
    return pl.pallas_call(
        body, name="ret_fwd", grid=(nb,),
        in_specs=[seg(R_Q // WIDTH), seg(R_K // WIDTH), seg(R_V // WIDTH), seg(Z_RET // WIDTH), full(gn_w), full(gn_b),
                  tab, tab, full(decay), full(zeta), full(xi)],
        out_specs=[seg(0), seg(0), pl.BlockSpec((HEADS, BLK, 1), lambda i: (0, i, 0)), seg(0), seg(0),
                   pl.BlockSpec((1, HEADS, DH, DH), lambda i: (i, 0, 0, 0))],
        out_shape=[jax.ShapeDtypeStruct((s, WIDTH), BF16), jax.ShapeDtypeStruct((s, WIDTH), BF16),
                   jax.ShapeDtypeStruct((HEADS, s, 1), F32), jax.ShapeDtypeStruct((s, WIDTH), BF16),
                   jax.ShapeDtypeStruct((s, WIDTH), BF16), jax.ShapeDtypeStruct((nb, HEADS, DH, DH), F32)],
        scratch_shapes=[pltpu.VMEM((HEADS, DH, DH), F32)],
        compiler_params=_cparams(("arbitrary",)),
    )(proj, proj, proj, proj, gn_w, gn_b, c128, s128, decay, zeta, xi)


def _merge_fwd(y_mla, y_ret, wbm, wbr, proj, s):
    tm, tn = _row_tile(s, 1408), 512

    def body(ym_ref, yr_ref, wm_ref, wr_ref, g0_ref, g1_ref, um_ref, ur_ref, mg_ref):
        um = _dot(ym_ref[...], wm_ref[...], "nt")
        ur = _dot(yr_ref[...], wr_ref[...], "nt")
        um_ref[...] = um.astype(BF16)
        ur_ref[...] = ur.astype(BF16)
        mg_ref[...] = (_sigmoid(g0_ref[...].astype(F32)) * um + _sigmoid(g1_ref[...].astype(F32)) * ur).astype(BF16)

    yspec = pl.BlockSpec((tm, WIDTH), lambda i, j: (i, 0))
    wspec = pl.BlockSpec((tn, WIDTH), lambda i, j: (j, 0))
    ospec = pl.BlockSpec((tm, tn), lambda i, j: (i, j))
    return pl.pallas_call(
        body, name="merge_fwd", grid=(s // tm, D // tn),
        in_specs=[yspec, yspec, wspec, wspec, pl.BlockSpec((tm, tn), lambda i, j: (i, GATE0 // tn + j)),
                  pl.BlockSpec((tm, tn), lambda i, j: (i, GATE1 // tn + j))],
        out_specs=[ospec, ospec, ospec],
        out_shape=[jax.ShapeDtypeStruct((s, D), BF16)] * 3,
        compiler_params=_cparams(("parallel", "parallel")),
    )(y_mla, y_ret, wbm, wbr, proj, proj)


def _out_fwd(merged, big, h, s):
    tm, tn = _row_tile(s, 1408), 512

    def body(m_ref, w_ref, h_ref, o_ref):
        o_ref[...] = h_ref[...] + _dot(m_ref[...], w_ref[...], "nn")

    return pl.pallas_call(
        body, name="out_fwd", grid=(s // tm, D // tn),
        in_specs=[pl.BlockSpec((tm, D), lambda i, j: (i, 0)), pl.BlockSpec((D, tn), lambda i, j: (BIG_OUT // D, j)),
                  pl.BlockSpec((tm, tn), lambda i, j: (i, j))],
        out_specs=pl.BlockSpec((tm, tn), lambda i, j: (i, j)),
        out_shape=jax.ShapeDtypeStruct((s, D), F32),
        compiler_params=_cparams(("parallel", "parallel")),
    )(merged, big, h)


def _loss_bwd(h2, target, fin_w, s):
    nb = s // BLK

    def body(h2_ref, t_ref, w_ref, dh_ref, dhb_ref, loss_ref, dw_ref):
        i = pl.program_id(0)

        @pl.when(i == 0)
        def _():
            dh_ref[...] = jnp.zeros_like(dh_ref)
            dhb_ref[...] = jnp.zeros_like(dhb_ref)
            loss_ref[...] = jnp.zeros_like(loss_ref)
            dw_ref[...] = jnp.zeros_like(dw_ref)

        @pl.when(i > 0)
        def _():
            hv = h2_ref[...]
            w = w_ref[...]
            r = lax.rsqrt(jnp.mean(hv * hv, axis=-1, keepdims=True) + NORM_EPS)
            nrm = hv * r
            e = nrm * w - t_ref[...]
            loss_ref[...] += jnp.full(loss_ref.shape, 0.5 / D, F32) * jnp.sum(e * e)
            dy = e * (1.0 / D)
            dw_ref[...] += jnp.sum(dy * nrm, axis=0, keepdims=True)
            g = dy * w
            dh = r * (g - nrm * jnp.mean(g * nrm, axis=-1, keepdims=True))
            dh_ref[...] = dh
            dhb_ref[...] = dh.astype(BF16)

    blk = pl.BlockSpec((BLK, D), lambda i: (i, 0))
    return pl.pallas_call(
        body, name="loss_bwd", grid=(nb,),
        in_specs=[blk, pl.BlockSpec((BLK, D), lambda i: (jnp.maximum(i - 1, 0), 0)), pl.BlockSpec((1, D), lambda i: (0, 0))],
        out_specs=[blk, blk, pl.BlockSpec((8, 128), lambda i: (0, 0)), pl.BlockSpec((1, D), lambda i: (0, 0))],
        out_shape=[jax.ShapeDtypeStruct((s, D), F32), jax.ShapeDtypeStruct((s, D), BF16),
                   jax.ShapeDtypeStruct((8, 128), F32), jax.ShapeDtypeStruct((1, D), F32)],
        compiler_params=_cparams(("arbitrary",)),
    )(h2, target, fin_w)


def _merge_bwd(dh2b, big, u_mla, u_ret, proj, s):
    tm, tn = _row_tile(s, 1408), 512

    def body(d_ref, w_ref, um_ref, ur_ref, gate_ref, dproj_ref, dum_ref, dur_ref, dm_ref):
        branch, j = pl.program_id(1), pl.program_id(2)
        cols = pl.ds(pl.multiple_of(j * tn, tn), tn)

        @pl.when(branch == 0)
        def _():
            dm_ref[:, cols] = _dot(d_ref[...], w_ref[...], "nt")

        dm = dm_ref[:, cols]
        gt = _sigmoid(gate_ref[...].astype(F32))

        @pl.when(branch == 0)
        def _():
            dproj_ref[...] = (dm * um_ref[...].astype(F32) * gt * (1.0 - gt)).astype(BF16)
            dum_ref[...] = (dm * gt).astype(BF16)

        @pl.when(branch == 1)
        def _():
            dproj_ref[...] = (dm * ur_ref[...].astype(F32) * gt * (1.0 - gt)).astype(BF16)
            dur_ref[...] = (dm * gt).astype(BF16)

    last = D // tn - 1
    mla = pl.BlockSpec((tm, tn), lambda i, b, j: (i, jnp.where(b == 0, j, last)))
    ret = pl.BlockSpec((tm, tn), lambda i, b, j: (i, jnp.where(b == 0, 0, j)))
    gate = pl.BlockSpec((tm, tn), lambda i, b, j: (i, GATE0 // tn + b * (D // tn) + j))
    return pl.pallas_call(
        body, name="merge_bwd", grid=(s // tm, 2, D // tn),
        in_specs=[pl.BlockSpec((tm, D), lambda i, b, j: (i, 0)),
                  pl.BlockSpec((tn, D), lambda i, b, j: (BIG_OUT // tn + jnp.where(b == 0, j, last), 0)), mla, ret, gate],
        out_specs=[gate, mla, ret],
        out_shape=[jax.ShapeDtypeStruct((s, PROJ_W), BF16), jax.ShapeDtypeStruct((s, D), BF16),
                   jax.ShapeDtypeStruct((s, D), BF16)],
        scratch_shapes=[pltpu.VMEM((tm, D), F32)],
        compiler_params=_cparams(("parallel", "arbitrary", "arbitrary")),
    )(dh2b, big, u_mla, u_ret, proj)


def _dsilu(z):
    sg = _sigmoid(z)
    return sg * (1.0 + z * (1.0 - sg))


def _branch_mla_bwd(dproj, du, wbm, o_mla, proj, s):
    tm = _row_tile(s, 384)

    def body(dproj_in, du_ref, w_ref, o_ref, z_ref, dz_ref, do_ref, delta_ref):
        del dproj_in
        dy = _dot(du_ref[...], w_ref[...], "nn")
        z = z_ref[...].astype(F32)
        o = o_ref[...].astype(F32)
        do = dy * _silu(z)
        do_ref[...] = do.astype(BF16)
        dz_ref[...] = (dy * o * _dsilu(z)).astype(BF16)
        prod = do * o
        for h in range(HEADS):
            delta_ref[h] = jnp.sum(prod[:, h * DH:(h + 1) * DH], axis=-1, keepdims=True)

    row = lambda w, cb: pl.BlockSpec((tm, w), lambda i: (i, cb))
    return pl.pallas_call(
        body, name="branch_mla_bwd", grid=(s // tm,),
        in_specs=[ANY, row(D, 0), pl.BlockSpec((D, WIDTH), lambda i: (0, 0)), row(WIDTH, 0),
                  row(WIDTH, Z_MLA // WIDTH)],
        out_specs=[row(WIDTH, Z_MLA // WIDTH), row(WIDTH, 0), pl.BlockSpec((HEADS, tm, 1), lambda i: (0, i, 0))],
        out_shape=[jax.ShapeDtypeStruct((s, PROJ_W), BF16), jax.ShapeDtypeStruct((s, WIDTH), BF16),
                   jax.ShapeDtypeStruct((HEADS, s, 1), F32)],
        input_output_aliases={0: 0},
        compiler_params=_cparams(("parallel",)),
    )(dproj, du, wbm, o_mla, proj)


def _branch_ret_bwd(dproj, du, wbr, on, rstd, gn_w, gn_b, proj, s):
    tm = BLK

    def body(dproj_in, du_ref, w_ref, on_ref, rstd_ref, gw_ref, gb_ref, z_ref, dz_ref, do_ref, dgw_ref, dgb_ref):
        del dproj_in

        @pl.when(pl.program_id(0) == 0)
        def _():
            dgw_ref[...] = jnp.zeros_like(dgw_ref)
            dgb_ref[...] = jnp.zeros_like(dgb_ref)

        dy = _dot(du_ref[...], w_ref[...], "nn")
        z = z_ref[...].astype(F32)
        on = on_ref[...].astype(F32)
        gw = gw_ref[...]
        dogn = dy * _silu(z)
        dz_ref[...] = (dy * (on * gw + gb_ref[...]) * _dsilu(z)).astype(BF16)
        dgw_ref[...] += jnp.sum(dogn * on, axis=0, keepdims=True)
        dgb_ref[...] += jnp.sum(dogn, axis=0, keepdims=True)
        don = dogn * gw
        for h in range(HEADS):
            sl = slice(h * DH, (h + 1) * DH)
            dn, nh = don[:, sl], on[:, sl]
            do = rstd_ref[h] * (dn - jnp.mean(dn, axis=-1, keepdims=True)
                                - nh * jnp.mean(dn * nh, axis=-1, keepdims=True))
            do_ref[:, sl] = do.astype(BF16)

    row = lambda w, cb: pl.BlockSpec((tm, w), lambda i: (i, cb))
    vec = pl.BlockSpec((1, WIDTH), lambda i: (0, 0))
    return pl.pallas_call(
        body, name="branch_ret_bwd", grid=(s // tm,),
        in_specs=[ANY, row(D, 0), pl.BlockSpec((D, WIDTH), lambda i: (0, 0)), row(WIDTH, 0),
                  pl.BlockSpec((HEADS, tm, 1), lambda i: (0, i, 0)), vec, vec, row(WIDTH, Z_RET // WIDTH)],
        out_specs=[row(WIDTH, Z_RET // WIDTH), row(WIDTH, 0), vec, vec],
        out_shape=[jax.ShapeDtypeStruct((s, PROJ_W), BF16), jax.ShapeDtypeStruct((s, WIDTH), BF16)]
        + [jax.ShapeDtypeStruct((1, WIDTH), F32)] * 2,
        input_output_aliases={0: 0},
        compiler_params=_cparams(("arbitrary",)),
    )(dproj, du, wbr, on, rstd, gn_w, gn_b, proj)


def _ret_bwd(dproj, qr, kr, proj, rall, do_ret, c128, s128, consts, gamma, s):
    nb = s // BLK
    decay, zeta, xi = consts

    def body(dproj_in, q_ref, k_ref, v_ref, r_ref, do_ref, c_ref, s_ref, dm_ref, zt_ref, xi_ref, out_ref, gstate):
        del dproj_in
        dq_ref, dk_ref, dv_ref = (out_ref.at[:, pl.ds(off, WIDTH)] for off in (R_Q, R_K, R_V))

        @pl.when(pl.program_id(0) == 0)
        def _():
            gstate[...] = jnp.zeros_like(gstate)

        c, sg = c_ref[...], s_ref[...]
        for h in range(HEADS):
            sl = slice(h * DH, (h + 1) * DH)
            q, k, v, do = q_ref[:, sl], k_ref[:, sl], v_ref[:, sl], do_ref[:, sl]
            dm = dm_ref[h]
            g_next = gstate[h]
            gb = g_next.astype(BF16)
            a = (_dot(q, k, "nt") * dm).astype(BF16)
            da = (_dot(do, v, "nt") * dm).astype(BF16)
            dox = (do.astype(F32) * xi_ref[h]).astype(BF16)
            dq = _dot(da, k, "nn") + _dot(dox, r_ref[0, h].astype(BF16), "nt")
            dk = _dot(da, q, "tn") + _dot(v, gb, "nt") * zt_ref[h]
            kz = (k.astype(F32) * zt_ref[h]).astype(BF16)
            dv = _dot(a, do, "tn") + _dot(kz, gb, "nn")
            gstate[h] = g_next * gamma[h] + _dot(q, dox, "tn")
            dk = dk * RET_SCALE
            dq_ref[:, sl] = _rope128(dq, c, -sg).astype(BF16)
            dk_ref[:, sl] = _rope128(dk, c, -sg).astype(BF16)
            dv_ref[:, sl] = dv.astype(BF16)

    rev = lambda cb: pl.BlockSpec((BLK, WIDTH), lambda i: (nb - 1 - i, cb))
    full = lambda a: pl.BlockSpec(a.shape, lambda i: (0,) * a.ndim)
    tab = pl.BlockSpec((BLK, DH), lambda i: (nb - 1 - i, 0))
    return pl.pallas_call(
        body, name="ret_bwd", grid=(nb,),
        in_specs=[ANY, rev(0), rev(0), rev(R_V // WIDTH),
                  pl.BlockSpec((1, HEADS, DH, DH), lambda i: (nb - 1 - i, 0, 0, 0)),
                  rev(0), tab, tab, full(decay), full(zeta), full(xi)],
        out_specs=pl.BlockSpec((BLK, 3 * WIDTH), lambda i: (nb - 1 - i, R_Q // (3 * WIDTH))),
        out_shape=jax.ShapeDtypeStruct((s, PROJ_W), BF16),
        scratch_shapes=[pltpu.VMEM((HEADS, DH, DH), F32)],
        input_output_aliases={0: 0},
        compiler_params=_cparams(("arbitrary",)),
    )(dproj, qr, kr, proj, rall, do_ret, c128, s128, decay, zeta, xi)


def _attn_bwd(qq, kk, vv, do, lse, delta, s):
    t = _row_tile(s, 384)
    n = s // t

    def body(q_ref, k_ref, v_ref, do_ref, lse_ref, delta_ref, dq_ref, dk_ref, dv_ref, dk_acc, dv_acc):
        ki = pl.program_id(1)

        @pl.when(ki == 0)
        def _():
            dq_ref[...] = jnp.zeros(dq_ref.shape, F32)

        k, v = k_ref[...], v_ref[...]
        dk_acc[...] = jnp.zeros(dk_acc.shape, F32)
        dv_acc[...] = jnp.zeros(dv_acc.shape, F32)

        def rows(qi):
            return pl.ds(pl.multiple_of(qi * t, t), t)

        def products(qi):
            return _dot(k, q_ref[rows(qi), :], "nt"), _dot(v, do_ref[rows(qi), :], "nt")

        def tile(qi, st, dpt):
            q, dov = q_ref[rows(qi), :], do_ref[rows(qi), :]
            pt = jnp.exp(st - lse_ref[0, qi])
            dv_acc[...] += _dot(pt.astype(BF16), dov, "nn")
            dst = (pt * (dpt - delta_ref[0, qi])).astype(BF16)
            dk_acc[...] += _dot(dst, q, "nn")
            dq_ref[rows(qi), :] += _dot(dst, k, "tn")

        def inner(qi, carry):
            after = products(jnp.minimum(qi + 1, n - 1))
            tile(qi, *carry)
            return after

        st, dpt = products(ki)
        ahead = products(jnp.minimum(ki + 1, n - 1))
        tile(ki, jnp.where(_diag_mask(t), st, NEG_INF), dpt)
        lax.fori_loop(ki + 1, n, inner, ahead)

        dk_ref[...] = dk_acc[...].astype(BF16)
        dv_ref[...] = dv_acc[...].astype(BF16)

    head = lambda w: pl.BlockSpec((s, w), lambda h, i: (0, h))
    stat = pl.BlockSpec((1, n, 1, t), lambda h, i: (h, 0, 0, 0))
    return pl.pallas_call(
        body, name="attn_bwd", grid=(HEADS, n),
        in_specs=[head(HD), pl.BlockSpec((t, HD), lambda h, i: (i, h)), pl.BlockSpec((t, DH), lambda h, i: (i, 2 * h)),
                  head(DH), stat, stat],
        out_specs=[head(HD), pl.BlockSpec((t, HD), lambda h, i: (i, h)), pl.BlockSpec((t, DH), lambda h, i: (i, h))],
        out_shape=[jax.ShapeDtypeStruct((s, HEADS * HD), F32), jax.ShapeDtypeStruct((s, HEADS * HD), BF16),
                   jax.ShapeDtypeStruct((s, WIDTH), BF16)],
        scratch_shapes=[pltpu.VMEM((t, HD), F32), pltpu.VMEM((t, DH), F32)],
        compiler_params=_cparams(("parallel", "arbitrary")),
    )(qq, kk, vv, do, lse, delta)


def _rms_bwd(dn, nhat, r, w):
    g = dn * w
    return r * (g - nhat * jnp.mean(g * nhat, axis=-1, keepdims=True)), dn * nhat


def _mla_prep_bwd(dproj, dqq, dkk, dvv, proj, rq, rkv, qn_w, kvn_w, wq, wukv, qtab, ktab, s):
    tm = _row_tile(s, 384)
    tail = PROJ_W - C_Q

    def body(dproj_in, dqq_ref, dkk_ref, dvv_ref, cq_ref, ckv_ref, rq_ref, rkv_ref, qnw_ref, kvnw_ref, wq_ref,
             wukv_ref, qc_ref, qa_ref, qb_ref, kc_ref, ka_ref, kb_ref,
             tail_ref, dq_ref, dkv_ref, dqnw_ref, dkvnw_ref):
        del dproj_in
        dcq_ref = tail_ref.at[:, pl.ds(0, Q_RANK)]
        dckv_ref = tail_ref.at[:, pl.ds(C_KV - C_Q, KV_RANK)]
        dkpe_ref = tail_ref.at[:, pl.ds(K_PE - C_Q, 2 * DH)]

        @pl.when(pl.program_id(0) == 0)
        def _():
            dqnw_ref[...] = jnp.zeros_like(dqnw_ref)
            dkvnw_ref[...] = jnp.zeros_like(dkvnw_ref)

        qc, qa, qb = qc_ref[...], qa_ref[...], qb_ref[...]
        dkp = jnp.zeros((tm, DH), F32)
        for h in range(HEADS):
            lo, mid, hi = h * HD, h * HD + DH, (h + 1) * HD
            dq_ref[:, lo:hi] = _rope64(dqq_ref[:, lo:hi], qc, -qa, -qb).astype(BF16)
            dkv_ref[:, lo:mid] = dkk_ref[:, lo:mid]
            dkv_ref[:, mid:hi] = dvv_ref[:, h * DH:(h + 1) * DH]
            dkp = dkp + dkk_ref[:, mid:hi].astype(F32)
        dcqn = _dot(dq_ref[...], wq_ref[...], "nn")
        rq_v = rq_ref[...]
        dcq, prod = _rms_bwd(dcqn, cq_ref[...].astype(F32) * rq_v, rq_v, qnw_ref[...])
        dcq_ref[...] = dcq.astype(BF16)
        dqnw_ref[...] += jnp.sum(prod, axis=0, keepdims=True)
        dckvn = _dot(dkv_ref[...], wukv_ref[...], "nn")
        rkv_v = rkv_ref[...]
        dckv, prod = _rms_bwd(dckvn, ckv_ref[...].astype(F32) * rkv_v, rkv_v, kvnw_ref[...])
        dckv_ref[...] = dckv.astype(BF16)
        dkvnw_ref[...] += jnp.sum(prod, axis=0, keepdims=True)
        dkpe_ref[:, :DH] = _rope64(dkp, kc_ref[...], -ka_ref[...], -kb_ref[...]).astype(BF16)
        dkpe_ref[:, DH:] = jnp.zeros((tm, DH), BF16)

    row = lambda w, cb: pl.BlockSpec((tm, w), lambda i: (i, cb))
    full = lambda a: pl.BlockSpec(a.shape, lambda i: (0, 0))
    wide = jax.ShapeDtypeStruct((s, HEADS * HD), BF16)
    return pl.pallas_call(
        body, name="mla_prep_bwd", grid=(s // tm,),
        in_specs=[ANY, row(HEADS * HD, 0), row(HEADS * HD, 0), row(WIDTH, 0),
                  row(Q_RANK, C_Q // Q_RANK), row(KV_RANK, C_KV // KV_RANK), row(1, 0), row(1, 0),
                  full(qn_w), full(kvn_w), full(wq), full(wukv), row(HD, 0), row(HD, 0), row(HD, 0),
                  row(DH, 0), row(DH, 0), row(DH, 0)],
        out_specs=[row(tail, C_Q // tail), row(HEADS * HD, 0), row(HEADS * HD, 0),
                   pl.BlockSpec((1, Q_RANK), lambda i: (0, 0)), pl.BlockSpec((1, KV_RANK), lambda i: (0, 0))],
        out_shape=[jax.ShapeDtypeStruct((s, PROJ_W), BF16), wide, wide,
                   jax.ShapeDtypeStruct((1, Q_RANK), F32), jax.ShapeDtypeStruct((1, KV_RANK), F32)],
        input_output_aliases={0: 0},
        compiler_params=_cparams(("arbitrary",)),
    )(dproj, dqq, dkk, dvv, proj, proj, rq, rkv, qn_w, kvn_w, wq, wukv, *qtab, *ktab)


def _rms_in_bwd(dxn, h, r, dh2, norm_w, s):
    def body(dxn_ref, h_ref, r_ref, dh2_ref, w_ref, gx_ref, gm_ref, dw_ref):
        i = pl.program_id(0)
        r_v = r_ref[...]
        dx, prod = _rms_bwd(dxn_ref[...], h_ref[...] * r_v, r_v, w_ref[...])
        dh = dh2_ref[...] + dx

        @pl.when(i == 0)
        def _():
            gm_ref[...] = dh
            dw_ref[...] = jnp.sum(prod, axis=0, keepdims=True)

        @pl.when(i > 0)
        def _():
            gx_ref[...] = dh
            dw_ref[...] += jnp.sum(prod, axis=0, keepdims=True)

    blk = pl.BlockSpec((BLK, D), lambda i: (i, 0))
    return pl.pallas_call(
        body, name="rms_in_bwd", grid=(s // BLK,),
        in_specs=[blk, blk, pl.BlockSpec((BLK, 1), lambda i: (i, 0)), blk, pl.BlockSpec((1, D), lambda i: (0, 0))],
        out_specs=[pl.BlockSpec((BLK, D), lambda i: (jnp.maximum(i - 1, 0), 0)), pl.BlockSpec((BLK, D), lambda i: (0, 0)),
                   pl.BlockSpec((1, D), lambda i: (0, 0))],
        out_shape=[jax.ShapeDtypeStruct((s - BLK, D), F32), jax.ShapeDtypeStruct((BLK, D), F32),
                   jax.ShapeDtypeStruct((1, D), F32)],
        compiler_params=_cparams(("arbitrary",)),
    )(dxn, h, r, dh2, norm_w)


def _adam_math(w, g, m, v):
    m = ADAM_B1 * m + (1.0 - ADAM_B1) * g
    v = ADAM_B2 * v + (1.0 - ADAM_B2) * (g * g)
    m_hat = m / (1.0 - ADAM_B1 ** ADAM_STEP)
    v_hat = v / (1.0 - ADAM_B2 ** ADAM_STEP)
    return -ADAM_LR * (m_hat / (jnp.sqrt(v_hat) + ADAM_EPS) + ADAM_WD * w), m, v


def _adamw(w, g, m, v, name):
    rows, cols = w.shape
    tr = rows
    for cand in (128, 64, 32, 16, 8):
        if rows % cand == 0:
            tr = cand
            break

    def body(w_ref, g_ref, m_ref, v_ref, d_ref, nm_ref, nv_ref):
        d_ref[...], nm_ref[...], nv_ref[...] = _adam_math(w_ref[...], g_ref[...], m_ref[...], v_ref[...])

    spec = pl.BlockSpec((tr, cols), lambda i: (i, 0))
    return pl.pallas_call(
        body, name=name, grid=(rows // tr,), in_specs=[spec] * 4, out_specs=[spec] * 3,
        out_shape=[jax.ShapeDtypeStruct((rows, cols), F32)] * 3,
        compiler_params=_cparams(("parallel",)),
    )(w, g, m, v)


def _adamw_t(w_t, g_t, m_t, v_t, name):
    cols, rows = w_t.shape

    def body(w_ref, g_ref, m_ref, v_ref, go_ref, d_ref, nm_ref, nv_ref):
        g = g_ref[...]
        go_ref[...] = g
        d_ref[...], nm_ref[...], nv_ref[...] = _adam_math(w_ref[...], g, m_ref[...], v_ref[...])

    blk = pl.BlockSpec((128, rows), lambda k: (k, 0))
    return pl.pallas_call(
        body, name=name, grid=(pl.cdiv(cols, 128),), in_specs=[blk] * 4, out_specs=[blk] * 4,
        out_shape=[jax.ShapeDtypeStruct((cols, rows), F32)] * 4, compiler_params=_cparams(("parallel",)),
    )(w_t, g_t, m_t, v_t)


def _adamw_small(w, gall, m, v):
    def body(w_ref, g_ref, m_ref, v_ref, gs_ref, d_ref, nm_ref, nv_ref):
        g = g_ref[0]
        for dev in range(1, 8):
            g = g + g_ref[dev]
        gs_ref[...] = g
        d_ref[...], nm_ref[...], nv_ref[...] = _adam_math(w_ref[...], g, m_ref[...], v_ref[...])

    return pl.pallas_call(
        body, name="adamw_small", out_shape=[jax.ShapeDtypeStruct((SM_ROWS, 128), F32)] * 4,
        compiler_params=pltpu.CompilerParams(vmem_limit_bytes=VMEM_LIMIT),
    )(w, gall, m, v)


ADD_ROWS = 464
HALF_BLOCKS = PK_HALF // ADD_ROWS


def _rs_core_add(srcs, land, pos):
    def body(pos_ref, *refs):
        src_refs, (land_ref, out_ref, own, sem) = refs[:len(srcs)], refs[len(srcs):]
        for s in range(N_CHIPS):
            for hf in range(2):
                @pl.when((pl.program_id(0) == s) & (pos_ref[1] == hf))
                def _():
                    for k, r0, p, n in _grad_half(s, hf):
                        pltpu.make_async_copy(src_refs[k].at[pl.ds(r0, n), :], own.at[pl.ds(p, n), :], sem).start()
        pltpu.make_async_copy(own, own, sem).wait()
        out_ref[0] = (own[...].astype(F32) + land_ref[0].astype(F32)).astype(BF16)

    blk = pl.BlockSpec((1, PK_HALF, D), lambda s, pos_ref: (s, 0, 0))
    return pl.pallas_call(
        body, name="rs_core_add",
        grid_spec=pltpu.PrefetchScalarGridSpec(
            num_scalar_prefetch=1, grid=(N_CHIPS,), in_specs=[ANY] * len(srcs) + [blk], out_specs=blk,
            scratch_shapes=[pltpu.VMEM((PK_HALF, D), BF16), pltpu.SemaphoreType.DMA(())]),
        out_shape=jax.ShapeDtypeStruct((N_CHIPS, PK_HALF, D), BF16),
        compiler_params=_cparams(("arbitrary",)),
    )(pos, *srcs, land)


def _rs_chip_add(cp, land, pos):
    def body(pos_ref, a_ref, l_ref, o_ref):
        o_ref[0] = ((a_ref[0].astype(F32) + l_ref[0].astype(F32)) + l_ref[1].astype(F32)) + l_ref[2].astype(F32)

    return pl.pallas_call(
        body, name="rs_chip_add",
        grid_spec=pltpu.PrefetchScalarGridSpec(
            num_scalar_prefetch=1, grid=(HALF_BLOCKS,),
            in_specs=[pl.BlockSpec((1, ADD_ROWS, D), lambda i, pos_ref: (pos_ref[0], i, 0)),
                      pl.BlockSpec((3, ADD_ROWS, D), lambda i, pos_ref: (0, i, 0))],
            out_specs=pl.BlockSpec((1, ADD_ROWS, D), lambda i, pos_ref: (pos_ref[1], i, 0))),
        out_shape=jax.ShapeDtypeStruct((2, PK_HALF, D), F32),
        compiler_params=_cparams(("parallel",)),
    )(pos, cp, land)


ANY = pl.BlockSpec(memory_space=pl.ANY)


def _mesh_pos():
    return lax.axis_index("x"), lax.axis_index("y"), lax.axis_index("c")


def _chip_peer(x, y, c, r):
    return (jnp.bitwise_xor(x, r >> 1), jnp.bitwise_xor(y, r & 1), c)


def _allgather_w_in(big0, wt, meta_loc, x2, norm_w):
    s_rows = x2.shape[0] + BLK

    def body(big0_ref, wt_ref, meta_ref, x_ref, nw_ref, big_ref, metaf_ref, h_ref, xn_ref, r_ref,
             lsem, ssem, rsem, fssem, frsem, msem_s, msem_r, osem, xbuf, xnbuf, rbuf, mbuf):
        del big0_ref
        x, y, c = _mesh_pos()
        j = 2 * x + y
        me, sibling = (x, y, c), (x, y, 1 - c)

        def half_wait(s_sem, r_sem, halves=1):
            rows = big_ref.at[pl.ds(0, halves * AG_IN_HALF), :]
            return pltpu.make_async_remote_copy(src_ref=rows, dst_ref=rows, send_sem=s_sem, recv_sem=r_sem,
                                                device_id=me, device_id_type=MESH)

        own_meta = pltpu.make_async_copy(meta_ref, metaf_ref.at[j], lsem)
        own_meta.start()
        for s in range(N_CHIPS):
            for hf in range(2):
                @pl.when((j == s) & (c == hf))
                def _():
                    for r in (1, 2, 3):
                        for p, b, n in _ag_half(s, True, hf):
                            pltpu.make_async_remote_copy(
                                src_ref=wt_ref.at[pl.ds(p, n), :], dst_ref=big_ref.at[pl.ds(b, n), :],
                                send_sem=ssem.at[r - 1], recv_sem=rsem.at[r - 1],
                                device_id=_chip_peer(x, y, c, r), device_id_type=MESH).start()

            @pl.when(j == s)
            def _():
                for hf in range(2):
                    for p, b, n in _ag_half(s, True, hf):
                        pltpu.make_async_remote_copy(
                            src_ref=wt_ref.at[pl.ds(p, n), :], dst_ref=big_ref.at[pl.ds(b, n), :],
                            send_sem=osem.at[0], recv_sem=osem.at[1], device_id=sibling, device_id_type=MESH).start()

        meta_copies = [pltpu.make_async_remote_copy(
            src_ref=meta_ref, dst_ref=metaf_ref.at[j], send_sem=msem_s.at[r - 1], recv_sem=msem_r.at[r - 1],
            device_id=_chip_peer(x, y, c, r), device_id_type=MESH) for r in (1, 2, 3)]
        for cp in meta_copies:
            cp.start()

        nw = nw_ref[...]

        def norm_rows(i):
            hv = xbuf[...]
            r = lax.rsqrt(jnp.mean(hv * hv, axis=-1, keepdims=True) + NORM_EPS)
            xnbuf[...] = (hv * r * nw).astype(BF16)
            rbuf[...] = r
            rows = pl.ds(pl.multiple_of(i * BLK, BLK), BLK)
            pltpu.sync_copy(xbuf, h_ref.at[rows, :])
            pltpu.sync_copy(xnbuf, xn_ref.at[rows, :])
            pltpu.sync_copy(rbuf, r_ref.at[rows, :])

        def token_rows(i, carry):
            pltpu.sync_copy(x_ref.at[pl.ds(pl.multiple_of((i - 1) * BLK, BLK), BLK), :], xbuf)
            norm_rows(i)
            return carry

        lax.fori_loop(1, s_rows // BLK, token_rows, 0)
        for cp in meta_copies:
            cp.wait_recv()
        own_meta.wait()
        xbuf[...] = jnp.zeros_like(xbuf)
        for sh in range(N_CHIPS):
            pltpu.sync_copy(metaf_ref.at[sh], mbuf)
            xbuf[PAD_FRONT:, sh * (D // N_CHIPS):(sh + 1) * (D // N_CHIPS)] = mbuf[...]
        norm_rows(0)

        for r in (1, 2, 3):
            half_wait(ssem.at[r - 1], rsem.at[r - 1]).wait_recv()
            src_shard = jnp.bitwise_xor(j, r)
            for s in range(N_CHIPS):
                for hf in range(2):
                    @pl.when((src_shard == s) & (c == hf))
                    def _():
                        for _, b, n in _ag_half(s, True, hf):
                            rows = big_ref.at[pl.ds(b, n), :]
                            pltpu.make_async_remote_copy(
                                src_ref=rows, dst_ref=rows, send_sem=fssem.at[r - 1], recv_sem=frsem.at[r - 1],
                                device_id=sibling, device_id_type=MESH).start()
        for r in (1, 2, 3):
            half_wait(fssem.at[r - 1], frsem.at[r - 1]).wait_recv()
        for r in (1, 2, 3):
            half_wait(ssem.at[r - 1], rsem.at[r - 1]).wait_send()
            half_wait(fssem.at[r - 1], frsem.at[r - 1]).wait_send()
        own_rows = half_wait(osem.at[0], osem.at[1], halves=2)
        own_rows.wait_recv()
        own_rows.wait_send()
        for cp in meta_copies:
            cp.wait_send()

    dma3 = pltpu.SemaphoreType.DMA((3,))
    big, _, h, xn, r = pl.pallas_call(
        body, name="allgather_w_in", in_specs=[ANY, ANY, ANY, ANY, pl.BlockSpec(memory_space=pltpu.VMEM)],
        out_specs=[ANY] * 5,
        out_shape=[jax.ShapeDtypeStruct((BIG_ROWS, D), BF16), jax.ShapeDtypeStruct((N_CHIPS,) + meta_loc.shape, F32),
                   jax.ShapeDtypeStruct((s_rows, D), F32), jax.ShapeDtypeStruct((s_rows, D), BF16),
                   jax.ShapeDtypeStruct((s_rows, 1), F32)],
        scratch_shapes=[pltpu.SemaphoreType.DMA(()), dma3, dma3, dma3, dma3, dma3, dma3, pltpu.SemaphoreType.DMA((2,)),
                        pltpu.VMEM((BLK, D), F32), pltpu.VMEM((BLK, D), BF16), pltpu.VMEM((BLK, 1), F32),
                        pltpu.VMEM(meta_loc.shape, F32)],
        input_output_aliases={0: 0},
        compiler_params=pltpu.CompilerParams(has_side_effects=True),
    )(big0, wt, meta_loc, x2, norm_w)
    return big, h, xn, r


HBM = pl.BlockSpec(memory_space=pltpu.HBM)
SEM = pl.BlockSpec(memory_space=pltpu.SEMAPHORE)
EFFECT = pltpu.SideEffectType.DATAFLOW_SIDE_EFFECTING
REST_PEERS = [(r, dc) for r in (1, 2, 3) for dc in (0, 1)]


def _allgather_rest_start(big):
    def body(big_ref, *rest):
        s_sems, r_sems = rest[:6], rest[6:12]
        x, y, c = _mesh_pos()
        j = 2 * x + y
        for s in range(N_CHIPS):
            for hf in range(2):
                @pl.when((j == s) & (c == hf))
                def _():
                    for k, (r, dc) in enumerate(REST_PEERS):
                        for _, b, n in _ag_half(s, False, hf):
                            rows = big_ref.at[pl.ds(b, n), :]
                            pltpu.make_async_remote_copy(
                                src_ref=rows, dst_ref=rows, send_sem=s_sems[k], recv_sem=r_sems[2 * (r - 1) + hf],
                                device_id=_chip_peer(x, y, dc, r), device_id_type=MESH).start()

    sem = pltpu.SemaphoreType.DMA(())
    out = pl.pallas_call(
        body, name="allgather_rest_start", out_shape=(sem,) * 12 + (pltpu.HBM(big.shape, big.dtype),),
        in_specs=(HBM,), out_specs=(SEM,) * 12 + (HBM,), input_output_aliases={0: 12},
        compiler_params=pltpu.CompilerParams(has_side_effects=EFFECT),
    )(pltpu.with_memory_space_constraint(big, pltpu.HBM))
    return out[:12], out[12]


def _allgather_rest_wait(sems, big, after):
    def body(big_ref, *rest):
        s_sems, r_sems = rest[:6], rest[6:12]
        me = _mesh_pos()
        rows = big_ref.at[pl.ds(0, AG_REST_HALF), :]
        for k in range(6):
            copy = pltpu.make_async_remote_copy(src_ref=rows, dst_ref=rows, send_sem=s_sems[k], recv_sem=r_sems[k],
                                                device_id=me, device_id_type=MESH)
            copy.wait_send()
            copy.wait_recv()

    return pl.pallas_call(
        body, name="allgather_rest_wait", out_shape=pltpu.HBM(big.shape, big.dtype),
        in_specs=(HBM,) + (SEM,) * 12 + (pl.BlockSpec(memory_space=pl.ANY),), out_specs=HBM,
        input_output_aliases={0: 0}, compiler_params=pltpu.CompilerParams(has_side_effects=EFFECT),
    )(big, *sems, after)


RS_SPLIT = 2


def _rs_core_exchange(srcs):
    def body(*refs):
        src_refs, (land_ref, ssem, rsem) = refs[:len(srcs)], refs[len(srcs):]
        x, y, c = _mesh_pos()
        me, sibling = (x, y, c), (x, y, 1 - c)
        for hf in range(2):
            @pl.when(c == hf)
            def _():
                for s in range(N_CHIPS):
                    for k, r0, p, n in _grad_half(s, 1 - hf):
                        pltpu.make_async_remote_copy(
                            src_ref=src_refs[k].at[pl.ds(r0, n), :], dst_ref=land_ref.at[s, pl.ds(p, n), :],
                            send_sem=ssem, recv_sem=rsem, device_id=sibling, device_id_type=MESH).start()
        whole = pltpu.make_async_remote_copy(src_ref=land_ref, dst_ref=land_ref, send_sem=ssem, recv_sem=rsem,
                                             device_id=me, device_id_type=MESH)
        whole.wait_recv()
        whole.wait_send()

    return pl.pallas_call(
        body, name="rs_core_exchange", in_specs=[ANY] * len(srcs), out_specs=ANY,
        out_shape=jax.ShapeDtypeStruct((N_CHIPS, PK_HALF, D), BF16),
        scratch_shapes=[pltpu.SemaphoreType.DMA(())] * 2,
        compiler_params=pltpu.CompilerParams(has_side_effects=True),
    )(*srcs)


def _rs_chip_start(cp):
    def body(cp_ref, land_ref, s1, s2, s3, r1, r2, r3, cp_thru, land_thru, token):
        x, y, c = _mesh_pos()
        j = 2 * x + y
        for r, s_sem, r_sem in zip((1, 2, 3), (s1, s2, s3), (r1, r2, r3)):
            pltpu.make_async_remote_copy(
                src_ref=cp_ref.at[jnp.bitwise_xor(j, r)], dst_ref=land_ref.at[r - 1], send_sem=s_sem, recv_sem=r_sem,
                device_id=_chip_peer(x, y, c, r), device_id_type=MESH).start()
        token[...] = jnp.zeros_like(token)

    land_shape = (3, PK_HALF, D)
    sem = pltpu.SemaphoreType.DMA(())
    out = pl.pallas_call(
        body, name="rs_chip_start",
        out_shape=(sem,) * 6 + (pltpu.HBM(cp.shape, cp.dtype), pltpu.HBM(land_shape, BF16),
                                jax.ShapeDtypeStruct((8, 128), F32)),
        in_specs=(HBM, HBM), out_specs=(SEM,) * 6 + (HBM, HBM, pl.BlockSpec(memory_space=pltpu.VMEM)),
        input_output_aliases={0: 6, 1: 7},
        compiler_params=pltpu.CompilerParams(has_side_effects=EFFECT),
    )(pltpu.with_memory_space_constraint(cp, pltpu.HBM),
      pltpu.with_memory_space_constraint(lax.empty(land_shape, BF16), pltpu.HBM))
    return out[:6], out[6], out[7], out[8]


def _rs_chip_wait(sems, cp_thru, land_thru, after):
    def body(cp_ref, land_ref, s1, s2, s3, r1, r2, r3, after_ref, cp_out, land_out):
        me = _mesh_pos()
        for r, s_sem, r_sem in zip((1, 2, 3), (s1, s2, s3), (r1, r2, r3)):
            copy = pltpu.make_async_remote_copy(src_ref=cp_ref.at[0], dst_ref=land_ref.at[r - 1], send_sem=s_sem,
                                                recv_sem=r_sem, device_id=me, device_id_type=MESH)
            copy.wait_send()
            copy.wait_recv()

    return pl.pallas_call(
        body, name="rs_chip_wait",
        out_shape=(pltpu.HBM(cp_thru.shape, cp_thru.dtype), pltpu.HBM(land_thru.shape, land_thru.dtype)),
        in_specs=(HBM, HBM) + (SEM,) * 6 + (pl.BlockSpec(memory_space=pl.ANY),), out_specs=(HBM, HBM),
        input_output_aliases={0: 0, 1: 1},
        compiler_params=pltpu.CompilerParams(has_side_effects=EFFECT),
    )(cp_thru, land_thru, *sems, after)


def _rs_finish(full, small):
    n = PK_HALF // RS_SPLIT

    def body(full_in_ref, sm_ref, full_ref, all_ref, lsem, ssem, rsem, sm_s, sm_r):
        del full_in_ref
        x, y, c = _mesh_pos()
        me, sibling = (x, y, c), (x, y, 1 - c)
        my_id = 4 * x + 2 * y + c
        own_sm = pltpu.make_async_copy(sm_ref, all_ref.at[my_id], lsem)
        own_sm.start()
        for q in range(RS_SPLIT):
            rows = full_ref.at[c, pl.ds(q * n, n), :]
            pltpu.make_async_remote_copy(src_ref=rows, dst_ref=rows, send_sem=ssem, recv_sem=rsem,
                                         device_id=sibling, device_id_type=MESH).start()
        half = pltpu.make_async_remote_copy(src_ref=full_ref.at[c], dst_ref=full_ref.at[c], send_sem=ssem,
                                            recv_sem=rsem, device_id=me, device_id_type=MESH)
        smalls = [pltpu.make_async_remote_copy(
            src_ref=sm_ref, dst_ref=all_ref.at[my_id], send_sem=sm_s.at[r - 1], recv_sem=sm_r.at[r - 1],
            device_id=(jnp.bitwise_xor(x, r >> 2), jnp.bitwise_xor(y, (r >> 1) & 1), jnp.bitwise_xor(c, r & 1)),
            device_id_type=MESH) for r in range(1, 8)]
        for cpy in smalls:
            cpy.start()
        half.wait_recv()
        half.wait_send()
        for cpy in smalls:
            cpy.wait_recv()
        for cpy in smalls:
            cpy.wait_send()
        own_sm.wait()

    dma7 = pltpu.SemaphoreType.DMA((7,))
    return pl.pallas_call(
        body, name="rs_finish", in_specs=[ANY, ANY], out_specs=[ANY, ANY],
        out_shape=[jax.ShapeDtypeStruct((2, PK_HALF, D), F32), jax.ShapeDtypeStruct((8, SM_ROWS, 128), F32)],
        scratch_shapes=[pltpu.SemaphoreType.DMA(()), pltpu.SemaphoreType.DMA(()), pltpu.SemaphoreType.DMA(()), dma7, dma7],
        input_output_aliases={0: 0},
        compiler_params=pltpu.CompilerParams(has_side_effects=True),
    )(full, small)


def _rope_tables(s):
    pos = jnp.arange(s, dtype=F32) - PAD_FRONT

    def cs(d):
        inv = ROPE_BASE ** (-jnp.arange(0, d, 2, dtype=F32) / d)
        ang = pos[:, None] * inv[None, :]
        return jnp.cos(ang), jnp.sin(ang)

    c, sn = cs(ROPE_D)
    z = jnp.zeros_like(c)
    ktab = (jnp.concatenate([c, c, z, z], axis=1), jnp.concatenate([-sn, z, z, z], axis=1),
            jnp.concatenate([z, sn, z, z], axis=1))
    one = jnp.ones_like(c)
    qtab = tuple(ATT_SCALE * jnp.concatenate(parts, axis=1) for parts in (
        [one, one, one, one, c, c, z, z], [z, z, z, z, -sn, z, z, z], [z, z, z, z, z, sn, z, z]))
    lane = jnp.arange(HD) == DH + ROPE_D
    qbias = lane.astype(F32)[None, :]
    kbias = jnp.where((pos < 0)[:, None] & lane[None, DH:], NEG_INF, 0.0).astype(F32)
    c, sn = cs(DH)
    return qtab, ktab, qbias, kbias, jnp.concatenate([c, c], axis=1), jnp.concatenate([-sn, sn], axis=1)


def _grad_mm(a, b, m, n, s, name):
    return _matmul(a, b, "tn", m, n, s, min(m, 512), min(n, 1024), s, BF16, name, n_outer=True)


def _local_step(x2, target2, hp0, big, small_w, normed=None, gather_wait=None, reduce_start=None, reduce_wait=None):
    norm_w, qn_w, kvn_w, gn_w, gn_b, fin_w = small_w
    s = x2.shape[0] + BLK
    tm = _row_tile(s, 1408)
    qtab, ktab, qbias, kbias, c128, s128 = _rope_tables(s)
    consts, gamma = _ret_consts()

    h, xn, r1 = _rms_in(x2, hp0, norm_w, s) if normed is None else normed
    proj = _matmul(xn, big, "nt", s, PROJ_W, D, tm, 512, D, BF16, "proj", b_off=(BIG_IN // 512, 0))
    if gather_wait is not None:
        big = gather_wait(big, proj)
    uq = big[BIG_UQ:BIG_UQ + 384].reshape(HEADS, DH + ROPE_D, Q_RANK)
    wq = jnp.pad(uq, ((0, 0), (0, HD - DH - ROPE_D), (0, 0))).reshape(HEADS * HD, Q_RANK)
    wukv = big[BIG_UKV:BIG_UKV + 256].reshape(2 * WIDTH, KV_RANK)
    wbm = big[BIG_BM:BIG_BM + 1024].reshape(D, WIDTH)
    wbr = big[BIG_BR:BIG_BR + 1024].reshape(D, WIDTH)
    cqn, rq, ckvn, rkv, qq, kk, vv = _mla_prep(proj, qn_w, kvn_w, wq, wukv, qtab, ktab, qbias, kbias, s)
    o_mla, y_mla, lse = _attn_fwd(qq, kk, vv, proj, s)
    y_ret, on, rstd, qr, kr, rall = _ret_fwd(proj, gn_w, gn_b, c128, s128, consts, gamma, s)
    u_mla, u_ret, merged = _merge_fwd(y_mla, y_ret, wbm, wbr, proj, s)
    h2 = _out_fwd(merged, big, h, s)
    dh2, dh2b, loss_blk, dfin = _loss_bwd(h2, target2, fin_w, s)

    dproj, du_mla, du_ret = _merge_bwd(dh2b, big, u_mla, u_ret, proj, s)
    g_out = _grad_mm(merged, dh2b, D, D, s, "grad_w_out")
    dproj, do_mla, delta = _branch_mla_bwd(dproj, du_mla, wbm, o_mla, proj, s)
    g_bm = _grad_mm(du_mla, y_mla, D, WIDTH, s, "grad_w_branch_mla")
    dproj, do_ret, dgw, dgb = _branch_ret_bwd(dproj, du_ret, wbr, on, rstd, gn_w, gn_b, proj, s)
    g_br = _grad_mm(du_ret, y_ret, D, WIDTH, s, "grad_w_branch_ret")
    dproj = _ret_bwd(dproj, qr, kr, proj, rall, do_ret, c128, s128, consts, gamma, s)
    dqq, dkk, dvv = _attn_bwd(qq, kk, vv, do_mla, lse, delta.reshape(lse.shape), s)
    dproj, dq, dkv, dqnw, dkvnw = _mla_prep_bwd(dproj, dqq, dkk, dvv, proj, rq, rkv, qn_w, kvn_w, wq, wukv,
                                                qtab, ktab, s)
    g_q = _grad_mm(dq, cqn, HEADS * HD, Q_RANK, s, "grad_w_uq")
    g_ukv = _grad_mm(dkv, ckvn, 2 * WIDTH, KV_RANK, s, "grad_w_ukv")
    g_in = _grad_mm(dproj, xn, PROJ_W, D, s, "grad_w_in")

    g_uq = g_q.reshape(HEADS, HD, Q_RANK)[:, :DH + ROPE_D]
    grads = (g_in, g_out, g_bm.reshape(1024, D), g_br.reshape(1024, D), g_uq.reshape(384, D), g_ukv.reshape(256, D))
    token, travelling = (None, grads) if reduce_start is None else reduce_start(grads)
    dxn = _matmul(dproj, big, "nn", s, D, PROJ_W, _row_tile(s, 384), 512, PROJ_W, F32, "dxn",
                  b_off=(BIG_IN // PROJ_W, 0), n_outer=True, after=token)
    grad_x, gmeta_blk, dnorm = _rms_in_bwd(dxn, h, r1, dh2, norm_w, s)
    small = _small_rows((dnorm, dqnw, dkvnw, dgw, dgb, dfin), loss_blk, gmeta_blk[PAD_FRONT:])
    return grad_x, travelling if reduce_wait is None else reduce_wait(travelling, dnorm), small


def _small_rows(ws, first=None, last=None):
    def part(a, rows):
        a = a.reshape(-1, 128)
        return a if a.shape[0] == rows else jnp.pad(a, ((0, rows - a.shape[0]), (0, 0)))

    bounds = (SM_NORM, SM_QN, SM_KVN, SM_GNW, SM_GNB, SM_FIN, SM_META)
    first = jnp.zeros((SM_NORM, 128), F32) if first is None else first
    last = jnp.zeros((SM_ROWS - SM_META, 128), F32) if last is None else last
    return jnp.concatenate([part(first, SM_NORM)] + [part(w, bounds[k + 1] - bounds[k]) for k, w in enumerate(ws)]
                           + [part(last, SM_ROWS - SM_META)], axis=0)


def kernel(x, meta, norm_w, w_in, mla_q_norm_w, mla_w_uq, mla_kv_norm_w, mla_w_ukv, ret_gn_w, ret_gn_b, w_branch_mla, w_branch_ret, w_out, final_norm_w, loss_target, m_meta, m_norm_w, m_w_in, m_mla_q_norm_w, m_mla_w_uq, m_mla_kv_norm_w, m_mla_w_ukv, m_ret_gn_w, m_ret_gn_b, m_w_branch_mla, m_w_branch_ret, m_w_out, m_final_norm_w, v_meta, v_norm_w, v_w_in, v_mla_q_norm_w, v_mla_w_uq, v_mla_kv_norm_w, v_mla_w_ukv, v_ret_gn_w, v_ret_gn_b, v_w_branch_mla, v_w_branch_ret, v_w_out, v_final_norm_w):
    j = 2 * lax.axis_index("x") + lax.axis_index("y")
    tr = lambda w: w[0].T.reshape(-1, D).astype(BF16)
    pos = jnp.stack([j, lax.axis_index("c")]).astype(jnp.int32)
    put = lambda buf, rows, off: lax.dynamic_update_slice(buf, rows, (off, 0))
    big0 = lax.empty((BIG_ROWS, D), BF16)
    big0 = put(big0, w_out[0].astype(BF16), BIG_OUT + 512 * j)
    big0 = put(big0, tr(w_branch_mla), BIG_BM + 256 * j)
    big0 = put(big0, tr(w_branch_ret), BIG_BR + 256 * j)
    big0 = put(big0, tr(mla_w_uq), BIG_UQ + 96 * j)
    big0 = put(big0, tr(mla_w_ukv), BIG_UKV + 64 * j)
    big0 = put(big0, jnp.zeros((AG_JUNK_END - AG_JUNK_REST, D), BF16), AG_JUNK_REST)
    big0 = put(big0, jnp.zeros((ZERO_ROWS, D), BF16), BIG_IN + IN_WIDTH)
    wt = jnp.concatenate([tr(w_in), jnp.zeros((2 * AG_IN_HALF - IN_SHARD, D), BF16)], axis=0)
    big, h, xn, r1 = _allgather_w_in(big0, wt, meta, x[0], norm_w)
    gather_sems, big = _allgather_rest_start(big)

    def gather_wait(big_travelling, after):
        return _allgather_rest_wait(gather_sems, big_travelling, after)

    small_w = (norm_w, mla_q_norm_w, mla_kv_norm_w, ret_gn_w, ret_gn_b, final_norm_w.reshape(1, D))

    def reduce_start(grads):
        chip_part = _rs_core_add(grads, _rs_core_exchange(grads), pos)
        sems, part_thru, land_thru, token = _rs_chip_start(chip_part)
        return token, (sems, part_thru, land_thru)

    def reduce_wait(state, after):
        return _rs_chip_wait(*state, after)

    grad_x, (chip_part, land2), small = _local_step(x[0], loss_target[0], None, big, small_w, (h, xn, r1), gather_wait,
                                                    reduce_start, reduce_wait)
    full, small_all = _rs_finish(_rs_chip_add(chip_part, land2, pos), small)
    full = full.reshape(PK_ROWS, D)

    untr = lambda lo, hi, rows: full[lo:hi].reshape(rows, -1).T
    grads = {
        "w_out": full[PK_OUT:PK_BM], "w_branch_mla": untr(PK_BM, PK_BR, 512), "w_branch_ret": untr(PK_BR, PK_UQ, 512),
        "mla_w_uq": untr(PK_UQ, PK_UKV, 384), "mla_w_ukv": untr(PK_UKV, PK_PAD, 512),
    }
    big_w = {"mla_w_uq": (mla_w_uq, m_mla_w_uq, v_mla_w_uq),
             "mla_w_ukv": (mla_w_ukv, m_mla_w_ukv, v_mla_w_ukv),
             "w_branch_mla": (w_branch_mla, m_w_branch_mla, v_w_branch_mla),
             "w_branch_ret": (w_branch_ret, m_w_branch_ret, v_w_branch_ret), "w_out": (w_out, m_w_out, v_w_out)}
    res = {"w_in": tuple(a.T[None] for a in _adamw_t(w_in[0].T, full, m_w_in[0].T, v_w_in[0].T, "adamw_w_in"))}
    for name, (w, m, v) in big_w.items():
        d, nm, nv = _adamw(w[0], grads[name], m[0], v[0], "adamw_" + name)
        res[name] = (grads[name][None], d[None], nm[None], nv[None])

    small_m = (m_norm_w, m_mla_q_norm_w, m_mla_kv_norm_w, m_ret_gn_w, m_ret_gn_b, m_final_norm_w.reshape(1, D))
    small_v = (v_norm_w, v_mla_q_norm_w, v_mla_kv_norm_w, v_ret_gn_w, v_ret_gn_b, v_final_norm_w.reshape(1, D))
    gs, ds, ms, vs = _adamw_small(_small_rows(small_w), small_all, _small_rows(small_m), _small_rows(small_v))
    names = ["norm_w", "mla_q_norm_w", "mla_kv_norm_w", "ret_gn_w", "ret_gn_b", "final_norm_w"]
    bounds = [SM_NORM, SM_QN, SM_KVN, SM_GNW, SM_GNB, SM_FIN]
    for k, name in enumerate(names):
        shape = (D,) if name == "final_norm_w" else (1, -1)
        rows = small_w[k].size // 128
        res[name] = tuple(a[bounds[k]:bounds[k] + rows].reshape(shape) for a in (gs, ds, ms, vs))
    g_meta = lax.dynamic_slice_in_dim(gs[SM_META:SM_META + 256].reshape(N_META, D), j * (D // N_CHIPS), D // N_CHIPS, axis=1)
    res["meta"] = (g_meta,) + tuple(_adamw(meta, g_meta, m_meta, v_meta, "adamw_meta"))

    order = ["meta", "norm_w", "w_in", "mla_q_norm_w", "mla_w_uq", "mla_kv_norm_w", "mla_w_ukv", "ret_gn_w", "ret_gn_b",
             "w_branch_mla", "w_branch_ret", "w_out", "final_norm_w"]
    return (gs[0, 0], grad_x[None]) + tuple(res[n][k] for k in range(4) for n in order)
```

```python
import functools
import math

import numpy as np
import jax
import jax.numpy as jnp
from jax import lax
from jax.experimental import pallas as pl
from jax.experimental.pallas import tpu as pltpu

F32 = jnp.float32
BF16 = jnp.bfloat16
MESH = pl.DeviceIdType.MESH

D = 2048
N_META = 16
BLK = 128
PAD_FRONT = BLK - N_META
HEADS = 8
DH = 128
ROPE_D = 64
Q_RANK = 512
KV_RANK = 256
WIDTH = HEADS * DH
ROPE_BASE = 10000.0
NORM_EPS = 1e-6
GN_EPS = 1e-5
NEG_INF = -1e30
ATT_SCALE = (DH + ROPE_D) ** -0.5
RET_SCALE = DH ** -0.5
IN_WIDTH = 10048
N_CHIPS = 4
IN_SHARD = IN_WIDTH // N_CHIPS
ADAM_LR, ADAM_B1, ADAM_B2, ADAM_EPS, ADAM_WD, ADAM_STEP = 0.001, 0.9, 0.999, 1e-08, 0.01, 10

R_Q, R_K, R_V, Z_RET, Z_MLA, GATE0, GATE1 = 0, 1024, 2048, 3072, 4096, 5120, 5120 + D
C_Q = 5120 + 2 * D
C_KV = C_Q + Q_RANK
K_PE = C_KV + KV_RANK
PROJ_W = 10240
IN_RUNS = ((0, 832, C_Q), (832, 1856, Z_MLA), (1856, 4928, R_Q), (4928, 5952, Z_RET), (5952, IN_WIDTH, GATE0))

BIG_IN, BIG_OUT, BIG_BM, BIG_BR, BIG_UQ, BIG_UKV, BIG_JUNK, BIG_ROWS = 0, 10240, 12288, 13312, 14336, 14720, 14976, 15360
ZERO_ROWS = PROJ_W - IN_WIDTH
PK_IN, PK_OUT, PK_BM, PK_BR, PK_UQ, PK_UKV, PK_PAD, PK_ROWS = 0, 2512, 3024, 3280, 3536, 3632, 3696, 3712
PK_HALF = PK_ROWS // 2
SM_LOSS, SM_NORM, SM_QN, SM_KVN, SM_GNW, SM_GNB, SM_FIN, SM_META, SM_ROWS = 0, 8, 24, 32, 40, 48, 56, 72, 328

VMEM_LIMIT = 56 * 1024 * 1024


def _pieces(shard):
    j = shard
    out = [(PK_OUT, BIG_OUT + 512 * j, 512), (PK_BM, BIG_BM + 256 * j, 256), (PK_BR, BIG_BR + 256 * j, 256),
           (PK_UQ, BIG_UQ + 96 * j, 96), (PK_UKV, BIG_UKV + 64 * j, 64)]
    for lo, hi, new in IN_RUNS:
        a, b = max(lo, IN_SHARD * j), min(hi, IN_SHARD * (j + 1))
        if b > a:
            out.append((PK_IN + a - IN_SHARD * j, BIG_IN + new + a - lo, b - a))
    out += [(PK_PAD, BIG_JUNK + 16 * j, 16)]
    return out


AG_JUNK_REST, AG_JUNK_IN, AG_JUNK_END = BIG_JUNK, BIG_JUNK + 128, BIG_JUNK + 192
AG_REST_HALF, AG_IN_HALF = 608, 1264


def _ag_half(shard, w_in, half):
    pieces = [(b, n) for p, b, n in _pieces(shard) if p < PK_PAD and (p < PK_OUT) == w_in]
    pieces.append((AG_JUNK_IN + 16 * shard, 16) if w_in else (AG_JUNK_REST + 32 * shard, 32))
    size = AG_IN_HALF if w_in else AG_REST_HALF
    out, pos = [], 0
    for b, n in pieces:
        s, e = max(pos, half * size), min(pos + n, (half + 1) * size)
        if e > s:
            out.append((s, b + s - pos, e - s))
        pos += n
    assert pos == 2 * size
    return out


def _grad_half(shard, half):
    j = shard
    pieces = [(0, b - BIG_IN, p, n) for p, b, n in _pieces(j) if p < PK_OUT]
    pieces += [(1, 512 * j, PK_OUT, 512), (2, 256 * j, PK_BM, 256), (3, 256 * j, PK_BR, 256), (4, 96 * j, PK_UQ, 96),
               (5, 64 * j, PK_UKV, 64), (0, IN_WIDTH, PK_PAD, PK_ROWS - PK_PAD)]
    lo, hi = half * PK_HALF, (half + 1) * PK_HALF
    out = []
    for k, r0, p, n in pieces:
        s, e = max(p, lo), min(p + n, hi)
        if e > s:
            out.append((k, r0 + s - p, s - lo, e - s))
    assert sum(n for _, _, _, n in out) == PK_HALF
    return out


def _row_tile(rows, cap):
    best = BLK
    for t in range(BLK, cap + 1, BLK):
        if rows % t == 0:
            best = t
    return best


def _cparams(sem):
    return pltpu.CompilerParams(dimension_semantics=sem, vmem_limit_bytes=VMEM_LIMIT)


def _dot(a, b, form):
    dn = {"nt": (((1,), (1,)), ((), ())), "nn": (((1,), (0,)), ((), ())), "tn": (((0,), (0,)), ((), ()))}[form]
    return lax.dot_general(a, b, dn, preferred_element_type=F32)


def _sigmoid(v):
    return 1.0 / (1.0 + jnp.exp(-v))


def _matmul(a, b, form, m, n, k, tm, tn, tk, out_dtype, name, a_off=(0, 0), b_off=(0, 0), n_outer=False, after=None):
    nk = k // tk
    gi, gj = m // tm, n // tn

    def ij(g0, g1):
        return (g1, g0) if n_outer else (g0, g1)

    if form == "nt":
        a_spec = pl.BlockSpec((tm, tk), lambda g0, g1, kk: (ij(g0, g1)[0] + a_off[0], kk + a_off[1]))
        b_spec = pl.BlockSpec((tn, tk), lambda g0, g1, kk: (ij(g0, g1)[1] + b_off[0], kk + b_off[1]))
    elif form == "nn":
        a_spec = pl.BlockSpec((tm, tk), lambda g0, g1, kk: (ij(g0, g1)[0] + a_off[0], kk + a_off[1]))
        b_spec = pl.BlockSpec((tk, tn), lambda g0, g1, kk: (kk + b_off[0], ij(g0, g1)[1] + b_off[1]))
    else:
        a_spec = pl.BlockSpec((tk, tm), lambda g0, g1, kk: (kk + a_off[0], ij(g0, g1)[0] + a_off[1]))
        b_spec = pl.BlockSpec((tk, tn), lambda g0, g1, kk: (kk + b_off[0], ij(g0, g1)[1] + b_off[1]))
    o_spec = pl.BlockSpec((tm, tn), lambda g0, g1, kk: ij(g0, g1))

    def body(a_ref, b_ref, *rest):
        o_ref, *acc = rest[0 if after is None else 1:]
        p = _dot(a_ref[...], b_ref[...], form)
        if nk == 1:
            o_ref[...] = p.astype(o_ref.dtype)
        else:
            acc_ref, = acc
            kk = pl.program_id(2)

            @pl.when(kk == 0)
            def _():
                acc_ref[...] = p

            @pl.when(kk > 0)
            def _():
                acc_ref[...] += p

            @pl.when(kk == nk - 1)
            def _():
                o_ref[...] = acc_ref[...].astype(o_ref.dtype)

    extra = [] if after is None else [after]
    return pl.pallas_call(
        body, name=name, grid=(gj, gi, nk) if n_outer else (gi, gj, nk),
        in_specs=[a_spec, b_spec] + [pl.BlockSpec(memory_space=pl.ANY)] * len(extra), out_specs=o_spec,
        out_shape=jax.ShapeDtypeStruct((m, n), out_dtype),
        scratch_shapes=[] if nk == 1 else [pltpu.VMEM((tm, tn), F32)],
        compiler_params=_cparams(("parallel", "parallel", "arbitrary")),
    )(a, b, *extra)


def _rms_in(x, hp0, norm_w, s):
    def body(x_ref, hp0_ref, w_ref, h_ref, xn_ref, r_ref):
        def run(hv):
            r = lax.rsqrt(jnp.mean(hv * hv, axis=-1, keepdims=True) + NORM_EPS)
            h_ref[...] = hv
            xn_ref[...] = (hv * r * w_ref[...]).astype(BF16)
            r_ref[...] = r

        @pl.when(pl.program_id(0) == 0)
        def _():
            run(hp0_ref[...])

        @pl.when(pl.program_id(0) > 0)
        def _():
            run(x_ref[...])

    return pl.pallas_call(
        body, name="rms_in", grid=(s // BLK,),
        in_specs=[pl.BlockSpec((BLK, D), lambda i: (jnp.maximum(i - 1, 0), 0)),
                  pl.BlockSpec((BLK, D), lambda i: (0, 0)), pl.BlockSpec((1, D), lambda i: (0, 0))],
        out_specs=[pl.BlockSpec((BLK, D), lambda i: (i, 0)), pl.BlockSpec((BLK, D), lambda i: (i, 0)),
                   pl.BlockSpec((BLK, 1), lambda i: (i, 0))],
        out_shape=[jax.ShapeDtypeStruct((s, D), F32), jax.ShapeDtypeStruct((s, D), BF16),
                   jax.ShapeDtypeStruct((s, 1), F32)],
        compiler_params=_cparams(("arbitrary",)),
    )(x, hp0, norm_w)


def _rope64(t, c, sa, sb):
    return t * c + pltpu.roll(t, t.shape[1] - ROPE_D // 2, 1) * sa + pltpu.roll(t, ROPE_D // 2, 1) * sb


def _rope128(t, c, sg):
    return t * c + pltpu.roll(t, DH // 2, 1) * sg


HD = 2 * DH


def _mla_prep(proj, qn_w, kvn_w, wq, wukv, qtab, ktab, qbias, kbias, s):
    tm = _row_tile(s, 384)

    def body(cq_ref, ckv_ref, kpe_ref, qnw_ref, kvnw_ref, wq_ref, wukv_ref, qc_ref, qa_ref, qb_ref,
             kc_ref, ka_ref, kb_ref, qbias_ref, kbias_ref, cqn_ref, rq_ref, ckvn_ref, rkv_ref, qq_ref, kk_ref, vv_ref):
        cq = cq_ref[...].astype(F32)
        rq = lax.rsqrt(jnp.mean(cq * cq, axis=-1, keepdims=True) + NORM_EPS)
        cqn = (cq * rq * qnw_ref[...]).astype(BF16)
        ckv = ckv_ref[...].astype(F32)
        rkv = lax.rsqrt(jnp.mean(ckv * ckv, axis=-1, keepdims=True) + NORM_EPS)
        ckvn = (ckv * rkv * kvnw_ref[...]).astype(BF16)
        cqn_ref[...] = cqn
        rq_ref[...] = rq
        ckvn_ref[...] = ckvn
        rkv_ref[...] = rkv
        q = _dot(cqn, wq_ref[...], "nt")
        kv = _dot(ckvn, wukv_ref[...], "nt")
        kp = (_rope64(kpe_ref[...].astype(F32), kc_ref[...], ka_ref[...], kb_ref[...]) + kbias_ref[...]).astype(BF16)
        qc, qa, qb, qbias = qc_ref[...], qa_ref[...], qb_ref[...], qbias_ref[...]
        ones = jnp.ones((tm, DH), BF16)
        for h in range(HEADS):
            lo, mid, hi = h * HD, h * HD + DH, (h + 1) * HD
            qq_ref[:, lo:hi] = (_rope64(q[:, lo:hi], qc, qa, qb) + qbias).astype(BF16)
            kk_ref[:, lo:mid] = kv[:, lo:mid].astype(BF16)
            kk_ref[:, mid:hi] = kp
            vv_ref[:, lo:mid] = kv[:, mid:hi].astype(BF16)
            vv_ref[:, mid:hi] = ones

    row = lambda w, cb: pl.BlockSpec((tm, w), lambda i: (i, cb))
    full = lambda a: pl.BlockSpec(a.shape, lambda i: (0, 0))
    wide = jax.ShapeDtypeStruct((s, HEADS * HD), BF16)
    return pl.pallas_call(
        body, name="mla_prep", grid=(s // tm,),
        in_specs=[row(Q_RANK, C_Q // Q_RANK), row(KV_RANK, C_KV // KV_RANK), row(DH, K_PE // DH),
                  full(qn_w), full(kvn_w), full(wq), full(wukv), row(HD, 0), row(HD, 0), row(HD, 0),
                  row(DH, 0), row(DH, 0), row(DH, 0), full(qbias), row(DH, 0)],
        out_specs=[row(Q_RANK, 0), row(1, 0), row(KV_RANK, 0), row(1, 0), row(HEADS * HD, 0), row(HEADS * HD, 0),
                   row(HEADS * HD, 0)],
        out_shape=[jax.ShapeDtypeStruct((s, Q_RANK), BF16), jax.ShapeDtypeStruct((s, 1), F32),
                   jax.ShapeDtypeStruct((s, KV_RANK), BF16), jax.ShapeDtypeStruct((s, 1), F32), wide, wide, wide],
        compiler_params=_cparams(("parallel",)),
    )(proj, proj, proj, qn_w, kvn_w, wq, wukv, *qtab, *ktab, qbias, kbias)


def _diag_mask(t):
    return lax.broadcasted_iota(jnp.int32, (t, t), 0) <= lax.broadcasted_iota(jnp.int32, (t, t), 1)


def _silu(z):
    return z * _sigmoid(z)


def _attn_fwd(qq, kk, vv, proj, s):
    t = _row_tile(s, 384)
    n = s // t

    def body(q_ref, k_ref, v_ref, z_ref, o_ref, y_ref, lse_ref, acc_ref, m_ref):
        qi = pl.program_id(1)
        q = q_ref[...]
        m_ref[...] = jnp.full(m_ref.shape, NEG_INF, F32)
        acc_ref[...] = jnp.zeros(acc_ref.shape, F32)

        def keys(ki):
            return pl.ds(pl.multiple_of(ki * t, t), t)

        def scores(ki):
            return _dot(k_ref[keys(ki), :], q, "nt")

        def tile(ki, st):
            m_old = m_ref[...]
            m_new = jnp.maximum(m_old, jnp.max(st, axis=0, keepdims=True))
            pt = jnp.exp(st - m_new).astype(BF16)
            acc_ref[...] = jnp.exp(m_old - m_new) * acc_ref[...] + _dot(v_ref[keys(ki), :], pt, "tn")
            m_ref[...] = m_new

        def inner(ki, st):
            st_after = scores(ki + 1)
            tile(ki, st)
            return st_after

        st_diag = lax.fori_loop(0, qi, inner, scores(0))
        tile(qi, jnp.where(_diag_mask(t), st_diag, NEG_INF))

        l = acc_ref[DH:DH + 1, :]
        o = (acc_ref[:DH, :] / l).T
        o_ref[...] = o.astype(BF16)
        y_ref[...] = (o * _silu(z_ref[...].astype(F32))).astype(BF16)
        lse_ref[0, 0] = m_ref[...] + jnp.log(l)

    qtile = pl.BlockSpec((t, DH), lambda h, i: (i, h))
    head = pl.BlockSpec((s, HD), lambda h, i: (0, h))
    return pl.pallas_call(
        body, name="attn_fwd", grid=(HEADS, s // t),
        in_specs=[pl.BlockSpec((t, HD), lambda h, i: (i, h)), head, head,
                  pl.BlockSpec((t, DH), lambda h, i: (i, Z_MLA // DH + h))],
        out_specs=[qtile, qtile, pl.BlockSpec((1, 1, 1, t), lambda h, i: (h, i, 0, 0))],
        out_shape=[jax.ShapeDtypeStruct((s, WIDTH), BF16), jax.ShapeDtypeStruct((s, WIDTH), BF16),
                   jax.ShapeDtypeStruct((HEADS, s // t, 1, t), F32)],
        scratch_shapes=[pltpu.VMEM((HD, t), F32), pltpu.VMEM((1, t), F32)],
        compiler_params=_cparams(("parallel", "arbitrary")),
    )(qq, kk, vv, proj)


def _ret_consts():
    log_g = np.log1p(-(2.0 ** (-5.0 - np.arange(HEADS, dtype=np.float64))))
    n = np.arange(BLK, dtype=np.float64)
    diff = n[:, None] - n[None, :]
    decay = np.where(diff >= 0, np.exp(log_g[:, None, None] * np.maximum(diff, 0.0)), 0.0)
    zeta = np.exp(log_g[:, None] * (BLK - 1.0 - n))[:, :, None]
    xi = np.exp(log_g[:, None] * (n + 1.0))[:, :, None]
    gamma = [float(np.float32(np.exp(g * BLK))) for g in log_g]
    return (jnp.asarray(decay, F32), jnp.asarray(zeta, F32), jnp.asarray(xi, F32)), gamma


def _ret_fwd(proj, gn_w, gn_b, c128, s128, consts, gamma, s):
    nb = s // BLK
    decay, zeta, xi = consts

    def body(rq_ref, rk_ref, rv_ref, z_ref, gw_ref, gb_ref, c_ref, s_ref, dm_ref, zt_ref, xi_ref,
             y_ref, on_ref, rstd_ref, qr_ref, kr_ref, rall_ref, state):
        @pl.when(pl.program_id(0) == 0)
        def _():
            state[...] = jnp.zeros_like(state)

        c, sg = c_ref[...], s_ref[...]
        for h in range(HEADS):
            sl = slice(h * DH, (h + 1) * DH)
            q = _rope128(rq_ref[:, sl].astype(F32), c, sg).astype(BF16)
            kf = _rope128(rk_ref[:, sl].astype(F32), c, sg) * RET_SCALE
            k = kf.astype(BF16)
            v = rv_ref[:, sl]
            qr_ref[:, sl] = q
            kr_ref[:, sl] = k
            r_prev = state[h]
            rall_ref[0, h] = r_prev
            a = _dot(q, k, "nt") * dm_ref[h]
            o = _dot(a.astype(BF16), v, "nn") + _dot(q, r_prev.astype(BF16), "nn") * xi_ref[h]
            state[h] = r_prev * gamma[h] + _dot((kf * zt_ref[h]).astype(BF16), v, "tn")
            mu = jnp.mean(o, axis=-1, keepdims=True)
            var = jnp.mean(jnp.square(o - mu), axis=-1, keepdims=True)
            rstd = lax.rsqrt(var + GN_EPS)
            on = (o - mu) * rstd
            rstd_ref[h] = rstd
            on_ref[:, sl] = on.astype(BF16)
            ogn = on * gw_ref[:, sl] + gb_ref[:, sl]
            y_ref[:, sl] = (ogn * _silu(z_ref[:, sl].astype(F32))).astype(BF16)

    seg = lambda cb: pl.BlockSpec((BLK, WIDTH), lambda i: (i, cb))
    full = lambda a: pl.BlockSpec(a.shape, lambda i: (0,) * a.ndim)
    tab = pl.BlockSpec((BLK, DH), lambda i: (i, 0))
    return pl.pallas_call(
        body, name="ret_fwd", grid=(nb,),
        in_specs=[seg(R_Q // WIDTH), seg(R_K // WIDTH), seg(R_V // WIDTH), seg(Z_RET // WIDTH), full(gn_w), full(gn_b),
                  tab, tab, full(decay), full(zeta), full(xi)],
        out_specs=[seg(0), seg(0), pl.BlockSpec((HEADS, BLK, 1), lambda i: (0, i, 0)), seg(0), seg(0),
                   pl.BlockSpec((1, HEADS, DH, DH), lambda i: (i, 0, 0, 0))],
        out_shape=[jax.ShapeDtypeStruct((s, WIDTH), BF16), jax.ShapeDtypeStruct((s, WIDTH), BF16),
                   jax.ShapeDtypeStruct((HEADS, s, 1), F32), jax.ShapeDtypeStruct((s, WIDTH), BF16),
                   jax.ShapeDtypeStruct((s, WIDTH), BF16), jax.ShapeDtypeStruct((nb, HEADS, DH, DH), F32)],
        scratch_shapes=[pltpu.VMEM((HEADS, DH, DH), F32)],
        compiler_params=_cparams(("arbitrary",)),
    )(proj, proj, proj, proj, gn_w, gn_b, c128, s128, decay, zeta, xi)


def _merge_fwd(y_mla, y_ret, wbm, wbr, proj, s):
    tm, tn = _row_tile(s, 1408), 512

    def body(ym_ref, yr_ref, wm_ref, wr_ref, g0_ref, g1_ref, um_ref, ur_ref, mg_ref):
        um = _dot(ym_ref[...], wm_ref[...], "nt")
        ur = _dot(yr_ref[...], wr_ref[...], "nt")
        um_ref[...] = um.astype(BF16)
        ur_ref[...] = ur.astype(BF16)
        mg_ref[...] = (_sigmoid(g0_ref[...].astype(F32)) * um + _sigmoid(g1_ref[...].astype(F32)) * ur).astype(BF16)

    yspec = pl.BlockSpec((tm, WIDTH), lambda i, j: (i, 0))
    wspec = pl.BlockSpec((tn, WIDTH), lambda i, j: (j, 0))
    ospec = pl.BlockSpec((tm, tn), lambda i, j: (i, j))
    return pl.pallas_call(
        body, name="merge_fwd", grid=(s // tm, D // tn),
        in_specs=[yspec, yspec, wspec, wspec, pl.BlockSpec((tm, tn), lambda i, j: (i, GATE0 // tn + j)),
                  pl.BlockSpec((tm, tn), lambda i, j: (i, GATE1 // tn + j))],
        out_specs=[ospec, ospec, ospec],
        out_shape=[jax.ShapeDtypeStruct((s, D), BF16)] * 3,
        compiler_params=_cparams(("parallel", "parallel")),
    )(y_mla, y_ret, wbm, wbr, proj, proj)


def _out_fwd(merged, big, h, s):
    tm, tn = _row_tile(s, 1408), 512

    def body(m_ref, w_ref, h_ref, o_ref):
        o_ref[...] = h_ref[...] + _dot(m_ref[...], w_ref[...], "nn")

    return pl.pallas_call(
        body, name="out_fwd", grid=(s // tm, D // tn),
        in_specs=[pl.BlockSpec((tm, D), lambda i, j: (i, 0)), pl.BlockSpec((D, tn), lambda i, j: (BIG_OUT // D, j)),
                  pl.BlockSpec((tm, tn), lambda i, j: (i, j))],
        out_specs=pl.BlockSpec((tm, tn), lambda i, j: (i, j)),
        out_shape=jax.ShapeDtypeStruct((s, D), F32),
        compiler_params=_cparams(("parallel", "parallel")),
    )(merged, big, h)


def _loss_bwd(h2, target, fin_w, s):
    nb = s // BLK

    def body(h2_ref, t_ref, w_ref, dh_ref, dhb_ref, loss_ref, dw_ref):
        i = pl.program_id(0)

        @pl.when(i == 0)
        def _():
            dh_ref[...] = jnp.zeros_like(dh_ref)
            dhb_ref[...] = jnp.zeros_like(dhb_ref)
            loss_ref[...] = jnp.zeros_like(loss_ref)
            dw_ref[...] = jnp.zeros_like(dw_ref)

        @pl.when(i > 0)
        def _():
            hv = h2_ref[...]
            w = w_ref[...]
            r = lax.rsqrt(jnp.mean(hv * hv, axis=-1, keepdims=True) + NORM_EPS)
            nrm = hv * r
            e = nrm * w - t_ref[...]
            loss_ref[...] += jnp.full(loss_ref.shape, 0.5 / D, F32) * jnp.sum(e * e)
            dy = e * (1.0 / D)
            dw_ref[...] += jnp.sum(dy * nrm, axis=0, keepdims=True)
            g = dy * w
            dh = r * (g - nrm * jnp.mean(g * nrm, axis=-1, keepdims=True))
            dh_ref[...] = dh
            dhb_ref[...] = dh.astype(BF16)

    blk = pl.BlockSpec((BLK, D), lambda i: (i, 0))
    return pl.pallas_call(
        body, name="loss_bwd", grid=(nb,),
        in_specs=[blk, pl.BlockSpec((BLK, D), lambda i: (jnp.maximum(i - 1, 0), 0)), pl.BlockSpec((1, D), lambda i: (0, 0))],
        out_specs=[blk, blk, pl.BlockSpec((8, 128), lambda i: (0, 0)), pl.BlockSpec((1, D), lambda i: (0, 0))],
        out_shape=[jax.ShapeDtypeStruct((s, D), F32), jax.ShapeDtypeStruct((s, D), BF16),
                   jax.ShapeDtypeStruct((8, 128), F32), jax.ShapeDtypeStruct((1, D), F32)],
        compiler_params=_cparams(("arbitrary",)),
    )(h2, target, fin_w)


def _merge_bwd(dh2b, big, u_mla, u_ret, proj, s):
    tm, tn = _row_tile(s, 1408), 512

    def body(d_ref, w_ref, um_ref, ur_ref, gate_ref, dproj_ref, dum_ref, dur_ref, dm_ref):
        branch, j = pl.program_id(1), pl.program_id(2)
        cols = pl.ds(pl.multiple_of(j * tn, tn), tn)

        @pl.when(branch == 0)
        def _():
            dm_ref[:, cols] = _dot(d_ref[...], w_ref[...], "nt")

        dm = dm_ref[:, cols]
        gt = _sigmoid(gate_ref[...].astype(F32))

        @pl.when(branch == 0)
        def _():
            dproj_ref[...] = (dm * um_ref[...].astype(F32) * gt * (1.0 - gt)).astype(BF16)
            dum_ref[...] = (dm * gt).astype(BF16)

        @pl.when(branch == 1)
        def _():
            dproj_ref[...] = (dm * ur_ref[...].astype(F32) * gt * (1.0 - gt)).astype(BF16)
            dur_ref[...] = (dm * gt).astype(BF16)

    last = D // tn - 1
    mla = pl.BlockSpec((tm, tn), lambda i, b, j: (i, jnp.where(b == 0, j, last)))
    ret = pl.BlockSpec((tm, tn), lambda i, b, j: (i, jnp.where(b == 0, 0, j)))
    gate = pl.BlockSpec((tm, tn), lambda i, b, j: (i, GATE0 // tn + b * (D // tn) + j))
    return pl.pallas_call(
        body, name="merge_bwd", grid=(s // tm, 2, D // tn),
        in_specs=[pl.BlockSpec((tm, D), lambda i, b, j: (i, 0)),
                  pl.BlockSpec((tn, D), lambda i, b, j: (BIG_OUT // tn + jnp.where(b == 0, j, last), 0)), mla, ret, gate],
        out_specs=[gate, mla, ret],
        out_shape=[jax.ShapeDtypeStruct((s, PROJ_W), BF16), jax.ShapeDtypeStruct((s, D), BF16),
                   jax.ShapeDtypeStruct((s, D), BF16)],
        scratch_shapes=[pltpu.VMEM((tm, D), F32)],
        compiler_params=_cparams(("parallel", "arbitrary", "arbitrary")),
    )(dh2b, big, u_mla, u_ret, proj)


def _dsilu(z):
    sg = _sigmoid(z)
    return sg * (1.0 + z * (1.0 - sg))


def _branch_mla_bwd(dproj, du, wbm, o_mla, proj, s):
    tm = _row_tile(s, 384)

    def body(dproj_in, du_ref, w_ref, o_ref, z_ref, dz_ref, do_ref, delta_ref):
        del dproj_in
        dy = _dot(du_ref[...], w_ref[...], "nn")
        z = z_ref[...].astype(F32)
        o = o_ref[...].astype(F32)
        do = dy * _silu(z)
        do_ref[...] = do.astype(BF16)
        dz_ref[...] = (dy * o * _dsilu(z)).astype(BF16)
        prod = do * o
        for h in range(HEADS):
            delta_ref[h] = jnp.sum(prod[:, h * DH:(h + 1) * DH], axis=-1, keepdims=True)

    row = lambda w, cb: pl.BlockSpec((tm, w), lambda i: (i, cb))
    return pl.pallas_call(
        body, name="branch_mla_bwd", grid=(s // tm,),
        in_specs=[ANY, row(D, 0), pl.BlockSpec((D, WIDTH), lambda i: (0, 0)), row(WIDTH, 0),
                  row(WIDTH, Z_MLA // WIDTH)],
        out_specs=[row(WIDTH, Z_MLA // WIDTH), row(WIDTH, 0), pl.BlockSpec((HEADS, tm, 1), lambda i: (0, i, 0))],
        out_shape=[jax.ShapeDtypeStruct((s, PROJ_W), BF16), jax.ShapeDtypeStruct((s, WIDTH), BF16),
                   jax.ShapeDtypeStruct((HEADS, s, 1), F32)],
        input_output_aliases={0: 0},
        compiler_params=_cparams(("parallel",)),
    )(dproj, du, wbm, o_mla, proj)


def _branch_ret_bwd(dproj, du, wbr, on, rstd, gn_w, gn_b, proj, s):
    tm = BLK

    def body(dproj_in, du_ref, w_ref, on_ref, rstd_ref, gw_ref, gb_ref, z_ref, dz_ref, do_ref, dgw_ref, dgb_ref):
        del dproj_in

        @pl.when(pl.program_id(0) == 0)
        def _():
            dgw_ref[...] = jnp.zeros_like(dgw_ref)
            dgb_ref[...] = jnp.zeros_like(dgb_ref)

        dy = _dot(du_ref[...], w_ref[...], "nn")
        z = z_ref[...].astype(F32)
        on = on_ref[...].astype(F32)
        gw = gw_ref[...]
        dogn = dy * _silu(z)
        dz_ref[...] = (dy * (on * gw + gb_ref[...]) * _dsilu(z)).astype(BF16)
        dgw_ref[...] += jnp.sum(dogn * on, axis=0, keepdims=True)
        dgb_ref[...] += jnp.sum(dogn, axis=0, keepdims=True)
        don = dogn * gw
        for h in range(HEADS):
            sl = slice(h * DH, (h + 1) * DH)
            dn, nh = don[:, sl], on[:, sl]
            do = rstd_ref[h] * (dn - jnp.mean(dn, axis=-1, keepdims=True)
                                - nh * jnp.mean(dn * nh, axis=-1, keepdims=True))
            do_ref[:, sl] = do.astype(BF16)

    row = lambda w, cb: pl.BlockSpec((tm, w), lambda i: (i, cb))
    vec = pl.BlockSpec((1, WIDTH), lambda i: (0, 0))
    return pl.pallas_call(
        body, name="branch_ret_bwd", grid=(s // tm,),
        in_specs=[ANY, row(D, 0), pl.BlockSpec((D, WIDTH), lambda i: (0, 0)), row(WIDTH, 0),
                  pl.BlockSpec((HEADS, tm, 1), lambda i: (0, i, 0)), vec, vec, row(WIDTH, Z_RET // WIDTH)],
        out_specs=[row(WIDTH, Z_RET // WIDTH), row(WIDTH, 0), vec, vec],
        out_shape=[jax.ShapeDtypeStruct((s, PROJ_W), BF16), jax.ShapeDtypeStruct((s, WIDTH), BF16)]
        + [jax.ShapeDtypeStruct((1, WIDTH), F32)] * 2,
        input_output_aliases={0: 0},
        compiler_params=_cparams(("arbitrary",)),
    )(dproj, du, wbr, on, rstd, gn_w, gn_b, proj)


def _ret_bwd(dproj, qr, kr, proj, rall, do_ret, c128, s128, consts, gamma, s):
    nb = s // BLK
    decay, zeta, xi = consts

    def body(dproj_in, q_ref, k_ref, v_ref, r_ref, do_ref, c_ref, s_ref, dm_ref, zt_ref, xi_ref, out_ref, gstate):
        del dproj_in
        dq_ref, dk_ref, dv_ref = (out_ref.at[:, pl.ds(off, WIDTH)] for off in (R_Q, R_K, R_V))

        @pl.when(pl.program_id(0) == 0)
        def _():
            gstate[...] = jnp.zeros_like(gstate)

        c, sg = c_ref[...], s_ref[...]
        for h in range(HEADS):
            sl = slice(h * DH, (h + 1) * DH)
            q, k, v, do = q_ref[:, sl], k_ref[:, sl], v_ref[:, sl], do_ref[:, sl]
            dm = dm_ref[h]
            g_next = gstate[h]
            gb = g_next.astype(BF16)
            a = (_dot(q, k, "nt") * dm).astype(BF16)
            da = (_dot(do, v, "nt") * dm).astype(BF16)
            dox = (do.astype(F32) * xi_ref[h]).astype(BF16)
            dq = _dot(da, k, "nn") + _dot(dox, r_ref[0, h].astype(BF16), "nt")
            dk = _dot(da, q, "tn") + _dot(v, gb, "nt") * zt_ref[h]
            kz = (k.astype(F32) * zt_ref[h]).astype(BF16)
            dv = _dot(a, do, "tn") + _dot(kz, gb, "nn")
            gstate[h] = g_next * gamma[h] + _dot(q, dox, "tn")
            dk = dk * RET_SCALE
            dq_ref[:, sl] = _rope128(dq, c, -sg).astype(BF16)
            dk_ref[:, sl] = _rope128(dk, c, -sg).astype(BF16)
            dv_ref[:, sl] = dv.astype(BF16)

    rev = lambda cb: pl.BlockSpec((BLK, WIDTH), lambda i: (nb - 1 - i, cb))
    full = lambda a: pl.BlockSpec(a.shape, lambda i: (0,) * a.ndim)
    tab = pl.BlockSpec((BLK, DH), lambda i: (nb - 1 - i, 0))
    return pl.pallas_call(
        body, name="ret_bwd", grid=(nb,),
        in_specs=[ANY, rev(0), rev(0), rev(R_V // WIDTH),
                  pl.BlockSpec((1, HEADS, DH, DH), lambda i: (nb - 1 - i, 0, 0, 0)),
                  rev(0), tab, tab, full(decay), full(zeta), full(xi)],
        out_specs=pl.BlockSpec((BLK, 3 * WIDTH), lambda i: (nb - 1 - i, R_Q // (3 * WIDTH))),
        out_shape=jax.ShapeDtypeStruct((s, PROJ_W), BF16),
        scratch_shapes=[pltpu.VMEM((HEADS, DH, DH), F32)],
        input_output_aliases={0: 0},
        compiler_params=_cparams(("arbitrary",)),
    )(dproj, qr, kr, proj, rall, do_ret, c128, s128, decay, zeta, xi)


def _attn_bwd(qq, kk, vv, do, lse, delta, s):
    t = _row_tile(s, 384)
    n = s // t

    def body(q_ref, k_ref, v_ref, do_ref, lse_ref, delta_ref, dq_ref, dk_ref, dv_ref, dk_acc, dv_acc):
        ki = pl.program_id(1)

        @pl.when(ki == 0)
        def _():
            dq_ref[...] = jnp.zeros(dq_ref.shape, F32)

        k, v = k_ref[...], v_ref[...]
        dk_acc[...] = jnp.zeros(dk_acc.shape, F32)
        dv_acc[...] = jnp.zeros(dv_acc.shape, F32)

        def rows(qi):
            return pl.ds(pl.multiple_of(qi * t, t), t)

        def products(qi):
            return _dot(k, q_ref[rows(qi), :], "nt"), _dot(v, do_ref[rows(qi), :], "nt")

        def tile(qi, st, dpt):
            q, dov = q_ref[rows(qi), :], do_ref[rows(qi), :]
            pt = jnp.exp(st - lse_ref[0, qi])
            dv_acc[...] += _dot(pt.astype(BF16), dov, "nn")
            dst = (pt * (dpt - delta_ref[0, qi])).astype(BF16)
            dk_acc[...] += _dot(dst, q, "nn")
            dq_ref[rows(qi), :] += _dot(dst, k, "tn")

        def inner(qi, carry):
            after = products(jnp.minimum(qi + 1, n - 1))
            tile(qi, *carry)
            return after

        st, dpt = products(ki)
        ahead = products(jnp.minimum(ki + 1, n - 1))
        tile(ki, jnp.where(_diag_mask(t), st, NEG_INF), dpt)
        lax.fori_loop(ki + 1, n, inner, ahead)

        dk_ref[...] = dk_acc[...].astype(BF16)
        dv_ref[...] = dv_acc[...].astype(BF16)

    head = lambda w: pl.BlockSpec((s, w), lambda h, i: (0, h))
    stat = pl.BlockSpec((1, n, 1, t), lambda h, i: (h, 0, 0, 0))
    return pl.pallas_call(
        body, name="attn_bwd", grid=(HEADS, n),
        in_specs=[head(HD), pl.BlockSpec((t, HD), lambda h, i: (i, h)), pl.BlockSpec((t, DH), lambda h, i: (i, 2 * h)),
                  head(DH), stat, stat],
        out_specs=[head(HD), pl.BlockSpec((t, HD), lambda h, i: (i, h)), pl.BlockSpec((t, DH), lambda h, i: (i, h))],
        out_shape=[jax.ShapeDtypeStruct((s, HEADS * HD), F32), jax.ShapeDtypeStruct((s, HEADS * HD), BF16),
                   jax.ShapeDtypeStruct((s, WIDTH), BF16)],
        scratch_shapes=[pltpu.VMEM((t, HD), F32), pltpu.VMEM((t, DH), F32)],
        compiler_params=_cparams(("parallel", "arbitrary")),
    )(qq, kk, vv, do, lse, delta)


def _rms_bwd(dn, nhat, r, w):
    g = dn * w
    return r * (g - nhat * jnp.mean(g * nhat, axis=-1, keepdims=True)), dn * nhat


def _mla_prep_bwd(dproj, dqq, dkk, dvv, proj, rq, rkv, qn_w, kvn_w, wq, wukv, qtab, ktab, s):
    tm = _row_tile(s, 384)
    tail = PROJ_W - C_Q

    def body(dproj_in, dqq_ref, dkk_ref, dvv_ref, cq_ref, ckv_ref, rq_ref, rkv_ref, qnw_ref, kvnw_ref, wq_ref,
             wukv_ref, qc_ref, qa_ref, qb_ref, kc_ref, ka_ref, kb_ref,
             tail_ref, dq_ref, dkv_ref, dqnw_ref, dkvnw_ref):
        del dproj_in
        dcq_ref = tail_ref.at[:, pl.ds(0, Q_RANK)]
        dckv_ref = tail_ref.at[:, pl.ds(C_KV - C_Q, KV_RANK)]
        dkpe_ref = tail_ref.at[:, pl.ds(K_PE - C_Q, 2 * DH)]

        @pl.when(pl.program_id(0) == 0)
        def _():
            dqnw_ref[...] = jnp.zeros_like(dqnw_ref)
            dkvnw_ref[...] = jnp.zeros_like(dkvnw_ref)

        qc, qa, qb = qc_ref[...], qa_ref[...], qb_ref[...]
        dkp = jnp.zeros((tm, DH), F32)
        for h in range(HEADS):
            lo, mid, hi = h * HD, h * HD + DH, (h + 1) * HD
            dq_ref[:, lo:hi] = _rope64(dqq_ref[:, lo:hi], qc, -qa, -qb).astype(BF16)
            dkv_ref[:, lo:mid] = dkk_ref[:, lo:mid]
            dkv_ref[:, mid:hi] = dvv_ref[:, h * DH:(h + 1) * DH]
            dkp = dkp + dkk_ref[:, mid:hi].astype(F32)
        dcqn = _dot(dq_ref[...], wq_ref[...], "nn")
        rq_v = rq_ref[...]
        dcq, prod = _rms_bwd(dcqn, cq_ref[...].astype(F32) * rq_v, rq_v, qnw_ref[...])
        dcq_ref[...] = dcq.astype(BF16)
        dqnw_ref[...] += jnp.sum(prod, axis=0, keepdims=True)
        dckvn = _dot(dkv_ref[...], wukv_ref[...], "nn")
        rkv_v = rkv_ref[...]
        dckv, prod = _rms_bwd(dckvn, ckv_ref[...].astype(F32) * rkv_v, rkv_v, kvnw_ref[...])
        dckv_ref[...] = dckv.astype(BF16)
        dkvnw_ref[...] += jnp.sum(prod, axis=0, keepdims=True)
        dkpe_ref[:, :DH] = _rope64(dkp, kc_ref[...], -ka_ref[...], -kb_ref[...]).astype(BF16)
        dkpe_ref[:, DH:] = jnp.zeros((tm, DH), BF16)

    row = lambda w, cb: pl.BlockSpec((tm, w), lambda i: (i, cb))
    full = lambda a: pl.BlockSpec(a.shape, lambda i: (0, 0))
    wide = jax.ShapeDtypeStruct((s, HEADS * HD), BF16)
    return pl.pallas_call(
        body, name="mla_prep_bwd", grid=(s // tm,),
        in_specs=[ANY, row(HEADS * HD, 0), row(HEADS * HD, 0), row(WIDTH, 0),
                  row(Q_RANK, C_Q // Q_RANK), row(KV_RANK, C_KV // KV_RANK), row(1, 0), row(1, 0),
                  full(qn_w), full(kvn_w), full(wq), full(wukv), row(HD, 0), row(HD, 0), row(HD, 0),
                  row(DH, 0), row(DH, 0), row(DH, 0)],
        out_specs=[row(tail, C_Q // tail), row(HEADS * HD, 0), row(HEADS * HD, 0),
                   pl.BlockSpec((1, Q_RANK), lambda i: (0, 0)), pl.BlockSpec((1, KV_RANK), lambda i: (0, 0))],
        out_shape=[jax.ShapeDtypeStruct((s, PROJ_W), BF16), wide, wide,
                   jax.ShapeDtypeStruct((1, Q_RANK), F32), jax.ShapeDtypeStruct((1, KV_RANK), F32)],
        input_output_aliases={0: 0},
        compiler_params=_cparams(("arbitrary",)),
    )(dproj, dqq, dkk, dvv, proj, proj, rq, rkv, qn_w, kvn_w, wq, wukv, *qtab, *ktab)


def _rms_in_bwd(dxn, h, r, dh2, norm_w, s):
    def body(dxn_ref, h_ref, r_ref, dh2_ref, w_ref, gx_ref, gm_ref, dw_ref):
        i = pl.program_id(0)
        r_v = r_ref[...]
        dx, prod = _rms_bwd(dxn_ref[...], h_ref[...] * r_v, r_v, w_ref[...])
        dh = dh2_ref[...] + dx

        @pl.when(i == 0)
        def _():
            gm_ref[...] = dh
            dw_ref[...] = jnp.sum(prod, axis=0, keepdims=True)

        @pl.when(i > 0)
        def _():
            gx_ref[...] = dh
            dw_ref[...] += jnp.sum(prod, axis=0, keepdims=True)

    blk = pl.BlockSpec((BLK, D), lambda i: (i, 0))
    return pl.pallas_call(
        body, name="rms_in_bwd", grid=(s // BLK,),
        in_specs=[blk, blk, pl.BlockSpec((BLK, 1), lambda i: (i, 0)), blk, pl.BlockSpec((1, D), lambda i: (0, 0))],
        out_specs=[pl.BlockSpec((BLK, D), lambda i: (jnp.maximum(i - 1, 0), 0)), pl.BlockSpec((BLK, D), lambda i: (0, 0)),
                   pl.BlockSpec((1, D), lambda i: (0, 0))],
        out_shape=[jax.ShapeDtypeStruct((s - BLK, D), F32), jax.ShapeDtypeStruct((BLK, D), F32),
                   jax.ShapeDtypeStruct((1, D), F32)],
        compiler_params=_cparams(("arbitrary",)),
    )(dxn, h, r, dh2, norm_w)


def _adam_math(w, g, m, v):
    m = ADAM_B1 * m + (1.0 - ADAM_B1) * g
    v = ADAM_B2 * v + (1.0 - ADAM_B2) * (g * g)
    m_hat = m / (1.0 - ADAM_B1 ** ADAM_STEP)
    v_hat = v / (1.0 - ADAM_B2 ** ADAM_STEP)
    return -ADAM_LR * (m_hat / (jnp.sqrt(v_hat) + ADAM_EPS) + ADAM_WD * w), m, v


def _adamw(w, g, m, v, name):
    rows, cols = w.shape
    tr = rows
    for cand in (128, 64, 32, 16, 8):
        if rows % cand == 0:
            tr = cand
            break

    def body(w_ref, g_ref, m_ref, v_ref, d_ref, nm_ref, nv_ref):
        d_ref[...], nm_ref[...], nv_ref[...] = _adam_math(w_ref[...], g_ref[...], m_ref[...], v_ref[...])

    spec = pl.BlockSpec((tr, cols), lambda i: (i, 0))
    return pl.pallas_call(
        body, name=name, grid=(rows // tr,), in_specs=[spec] * 4, out_specs=[spec] * 3,
        out_shape=[jax.ShapeDtypeStruct((rows, cols), F32)] * 3,
        compiler_params=_cparams(("parallel",)),
    )(w, g, m, v)


def _adamw_t(w_t, g_t, m_t, v_t, name):
    cols, rows = w_t.shape

    def body(w_ref, g_ref, m_ref, v_ref, go_ref, d_ref, nm_ref, nv_ref):
        g = g_ref[...]
        go_ref[...] = g
        d_ref[...], nm_ref[...], nv_ref[...] = _adam_math(w_ref[...], g, m_ref[...], v_ref[...])

    blk = pl.BlockSpec((128, rows), lambda k: (k, 0))
    return pl.pallas_call(
        body, name=name, grid=(pl.cdiv(cols, 128),), in_specs=[blk] * 4, out_specs=[blk] * 4,
        out_shape=[jax.ShapeDtypeStruct((cols, rows), F32)] * 4, compiler_params=_cparams(("parallel",)),
    )(w_t, g_t, m_t, v_t)


def _adamw_small(w, gall, m, v):
    def body(w_ref, g_ref, m_ref, v_ref, gs_ref, d_ref, nm_ref, nv_ref):
        g = g_ref[0]
        for dev in range(1, 8):
            g = g + g_ref[dev]
        gs_ref[...] = g
        d_ref[...], nm_ref[...], nv_ref[...] = _adam_math(w_ref[...], g, m_ref[...], v_ref[...])

    return pl.pallas_call(
        body, name="adamw_small", out_shape=[jax.ShapeDtypeStruct((SM_ROWS, 128), F32)] * 4,
        compiler_params=pltpu.CompilerParams(vmem_limit_bytes=VMEM_LIMIT),
    )(w, gall, m, v)


ADD_ROWS = 464
HALF_BLOCKS = PK_HALF // ADD_ROWS


def _rs_core_add(srcs, land, pos):
    def body(pos_ref, *refs):
        src_refs, (land_ref, out_ref, own, sems) = refs[:len(srcs)], refs[len(srcs):]
        step = pl.program_id(0)

        def fetch(s):
            for hf in range(2):
                @pl.when(pos_ref[1] == hf)
                def _():
                    for k, r0, p, n in _grad_half(s, hf):
                        pltpu.make_async_copy(src_refs[k].at[pl.ds(r0, n), :], own.at[s % 2, pl.ds(p, n), :],
                                              sems.at[s % 2]).start()

        @pl.when(step == 0)
        def _():
            fetch(0)

        for s in range(N_CHIPS):
            @pl.when(step == s)
            def _():
                if s + 1 < N_CHIPS:
                    fetch(s + 1)
                slot = own.at[s % 2]
                pltpu.make_async_copy(slot, slot, sems.at[s % 2]).wait()
                out_ref[0] = (slot[...].astype(F32) + land_ref[0].astype(F32)).astype(BF16)

    blk = pl.BlockSpec((1, PK_HALF, D), lambda s, pos_ref: (s, 0, 0))
    return pl.pallas_call(
        body, name="rs_core_add",
        grid_spec=pltpu.PrefetchScalarGridSpec(
            num_scalar_prefetch=1, grid=(N_CHIPS,), in_specs=[ANY] * len(srcs) + [blk], out_specs=blk,
            scratch_shapes=[pltpu.VMEM((2, PK_HALF, D), BF16), pltpu.SemaphoreType.DMA((2,))]),
        out_shape=jax.ShapeDtypeStruct((N_CHIPS, PK_HALF, D), BF16),
        compiler_params=_cparams(("arbitrary",)),
    )(pos, *srcs, land)


def _rs_chip_add(cp, land, pos):
    def body(pos_ref, a_ref, l_ref, o_ref):
        o_ref[0] = ((a_ref[0].astype(F32) + l_ref[0].astype(F32)) + l_ref[1].astype(F32)) + l_ref[2].astype(F32)

    return pl.pallas_call(
        body, name="rs_chip_add",
        grid_spec=pltpu.PrefetchScalarGridSpec(
            num_scalar_prefetch=1, grid=(HALF_BLOCKS,),
            in_specs=[pl.BlockSpec((1, ADD_ROWS, D), lambda i, pos_ref: (pos_ref[0], i, 0)),
                      pl.BlockSpec((3, ADD_ROWS, D), lambda i, pos_ref: (0, i, 0))],
            out_specs=pl.BlockSpec((1, ADD_ROWS, D), lambda i, pos_ref: (pos_ref[1], i, 0))),
        out_shape=jax.ShapeDtypeStruct((2, PK_HALF, D), F32),
        compiler_params=_cparams(("parallel",)),
    )(pos, cp, land)


ANY = pl.BlockSpec(memory_space=pl.ANY)


def _mesh_pos():
    return lax.axis_index("x"), lax.axis_index("y"), lax.axis_index("c")


def _chip_peer(x, y, c, r):
    return (jnp.bitwise_xor(x, r >> 1), jnp.bitwise_xor(y, r & 1), c)


def _allgather_w_in(big0, wt, meta_loc, x2, norm_w):
    s_rows = x2.shape[0] + BLK
    tok = next(t for t in (512, 256, 128) if x2.shape[0] % t == 0)

    def body(big0_ref, wt_ref, meta_ref, x_ref, nw_ref, big_ref, metaf_ref, h_ref, xn_ref, r_ref,
             lsem, ssem, rsem, fssem, frsem, msem_s, msem_r, osem, xbuf, xnbuf, rbuf, mbuf):
        del big0_ref
        x, y, c = _mesh_pos()
        j = 2 * x + y
        me, sibling = (x, y, c), (x, y, 1 - c)

        def half_wait(s_sem, r_sem, halves=1):
            rows = big_ref.at[pl.ds(0, halves * AG_IN_HALF), :]
            return pltpu.make_async_remote_copy(src_ref=rows, dst_ref=rows, send_sem=s_sem, recv_sem=r_sem,
                                                device_id=me, device_id_type=MESH)

        own_meta = pltpu.make_async_copy(meta_ref, metaf_ref.at[j], lsem)
        own_meta.start()
        for s in range(N_CHIPS):
            for hf in range(2):
                @pl.when((j == s) & (c == hf))
                def _():
                    for r in (1, 2, 3):
                        for p, b, n in _ag_half(s, True, hf):
                            pltpu.make_async_remote_copy(
                                src_ref=wt_ref.at[pl.ds(p, n), :], dst_ref=big_ref.at[pl.ds(b, n), :],
                                send_sem=ssem.at[r - 1], recv_sem=rsem.at[r - 1],
                                device_id=_chip_peer(x, y, c, r), device_id_type=MESH).start()

            @pl.when(j == s)
            def _():
                for hf in range(2):
                    for p, b, n in _ag_half(s, True, hf):
                        pltpu.make_async_remote_copy(
                            src_ref=wt_ref.at[pl.ds(p, n), :], dst_ref=big_ref.at[pl.ds(b, n), :],
                            send_sem=osem.at[0], recv_sem=osem.at[1], device_id=sibling, device_id_type=MESH).start()

        meta_copies = [pltpu.make_async_remote_copy(
            src_ref=meta_ref, dst_ref=metaf_ref.at[j], send_sem=msem_s.at[r - 1], recv_sem=msem_r.at[r - 1],
            device_id=_chip_peer(x, y, c, r), device_id_type=MESH) for r in (1, 2, 3)]
        for cp in meta_copies:
            cp.start()

        nw = nw_ref[...]

        def norm_rows(first, n):
            hv = xbuf[pl.ds(0, n), :]
            r = lax.rsqrt(jnp.mean(hv * hv, axis=-1, keepdims=True) + NORM_EPS)
            xnbuf[pl.ds(0, n), :] = (hv * r * nw).astype(BF16)
            rbuf[pl.ds(0, n), :] = r
            rows = pl.ds(first if isinstance(first, int) else pl.multiple_of(first, BLK), n)
            pltpu.sync_copy(xbuf.at[pl.ds(0, n), :], h_ref.at[rows, :])
            pltpu.sync_copy(xnbuf.at[pl.ds(0, n), :], xn_ref.at[rows, :])
            pltpu.sync_copy(rbuf.at[pl.ds(0, n), :], r_ref.at[rows, :])

        def token_rows(i, carry):
            pltpu.sync_copy(x_ref.at[pl.ds(pl.multiple_of(i * tok, tok), tok), :], xbuf)
            norm_rows(BLK + i * tok, tok)
            return carry

        lax.fori_loop(0, (s_rows - BLK) // tok, token_rows, 0)
        for cp in meta_copies:
            cp.wait_recv()
        own_meta.wait()
        xbuf[pl.ds(0, BLK), :] = jnp.zeros((BLK, D), F32)
        for sh in range(N_CHIPS):
            pltpu.sync_copy(metaf_ref.at[sh], mbuf)
            xbuf[PAD_FRONT:BLK, sh * (D // N_CHIPS):(sh + 1) * (D // N_CHIPS)] = mbuf[...]
        norm_rows(0, BLK)

        for r in (1, 2, 3):
            half_wait(ssem.at[r - 1], rsem.at[r - 1]).wait_recv()
            src_shard = jnp.bitwise_xor(j, r)
            for s in range(N_CHIPS):
                for hf in range(2):
                    @pl.when((src_shard == s) & (c == hf))
                    def _():
                        for _, b, n in _ag_half(s, True, hf):
                            rows = big_ref.at[pl.ds(b, n), :]
                            pltpu.make_async_remote_copy(
                                src_ref=rows, dst_ref=rows, send_sem=fssem.at[r - 1], recv_sem=frsem.at[r - 1],
                                device_id=sibling, device_id_type=MESH).start()
        for r in (1, 2, 3):
            half_wait(fssem.at[r - 1], frsem.at[r - 1]).wait_recv()
        for r in (1, 2, 3):
            half_wait(ssem.at[r - 1], rsem.at[r - 1]).wait_send()
            half_wait(fssem.at[r - 1], frsem.at[r - 1]).wait_send()
        own_rows = half_wait(osem.at[0], osem.at[1], halves=2)
        own_rows.wait_recv()
        own_rows.wait_send()
        for cp in meta_copies:
            cp.wait_send()

    dma3 = pltpu.SemaphoreType.DMA((3,))
    big, _, h, xn, r = pl.pallas_call(
        body, name="allgather_w_in", in_specs=[ANY, ANY, ANY, ANY, pl.BlockSpec(memory_space=pltpu.VMEM)],
        out_specs=[ANY] * 5,
        out_shape=[jax.ShapeDtypeStruct((BIG_ROWS, D), BF16), jax.ShapeDtypeStruct((N_CHIPS,) + meta_loc.shape, F32),
                   jax.ShapeDtypeStruct((s_rows, D), F32), jax.ShapeDtypeStruct((s_rows, D), BF16),
                   jax.ShapeDtypeStruct((s_rows, 1), F32)],
        scratch_shapes=[pltpu.SemaphoreType.DMA(()), dma3, dma3, dma3, dma3, dma3, dma3, pltpu.SemaphoreType.DMA((2,)),
                        pltpu.VMEM((tok, D), F32), pltpu.VMEM((tok, D), BF16), pltpu.VMEM((tok, 1), F32),
                        pltpu.VMEM(meta_loc.shape, F32)],
        input_output_aliases={0: 0},
        compiler_params=pltpu.CompilerParams(has_side_effects=True),
    )(big0, wt, meta_loc, x2, norm_w)
    return big, h, xn, r


HBM = pl.BlockSpec(memory_space=pltpu.HBM)
SEM = pl.BlockSpec(memory_space=pltpu.SEMAPHORE)
EFFECT = pltpu.SideEffectType.DATAFLOW_SIDE_EFFECTING
REST_PEERS = [(r, dc) for r in (1, 2, 3) for dc in (0, 1)]


def _allgather_rest_start(big):
    def body(big_ref, *rest):
        s_sems, r_sems = rest[:6], rest[6:12]
        x, y, c = _mesh_pos()
        j = 2 * x + y
        for s in range(N_CHIPS):
            for hf in range(2):
                @pl.when((j == s) & (c == hf))
                def _():
                    for k, (r, dc) in enumerate(REST_PEERS):
                        for _, b, n in _ag_half(s, False, hf):
                            rows = big_ref.at[pl.ds(b, n), :]
                            pltpu.make_async_remote_copy(
                                src_ref=rows, dst_ref=rows, send_sem=s_sems[k], recv_sem=r_sems[2 * (r - 1) + hf],
                                device_id=_chip_peer(x, y, dc, r), device_id_type=MESH).start()

    sem = pltpu.SemaphoreType.DMA(())
    out = pl.pallas_call(
        body, name="allgather_rest_start", out_shape=(sem,) * 12 + (pltpu.HBM(big.shape, big.dtype),),
        in_specs=(HBM,), out_specs=(SEM,) * 12 + (HBM,), input_output_aliases={0: 12},
        compiler_params=pltpu.CompilerParams(has_side_effects=EFFECT),
    )(pltpu.with_memory_space_constraint(big, pltpu.HBM))
    return out[:12], out[12]


def _allgather_rest_wait(sems, big, after):
    def body(big_ref, *rest):
        s_sems, r_sems = rest[:6], rest[6:12]
        me = _mesh_pos()
        rows = big_ref.at[pl.ds(0, AG_REST_HALF), :]
        for k in range(6):
            copy = pltpu.make_async_remote_copy(src_ref=rows, dst_ref=rows, send_sem=s_sems[k], recv_sem=r_sems[k],
                                                device_id=me, device_id_type=MESH)
            copy.wait_send()
            copy.wait_recv()

    return pl.pallas_call(
        body, name="allgather_rest_wait", out_shape=pltpu.HBM(big.shape, big.dtype),
        in_specs=(HBM,) + (SEM,) * 12 + (pl.BlockSpec(memory_space=pl.ANY),), out_specs=HBM,
        input_output_aliases={0: 0}, compiler_params=pltpu.CompilerParams(has_side_effects=EFFECT),
    )(big, *sems, after)


RS_SPLIT = 2


def _rs_core_exchange(srcs):
    def body(*refs):
        src_refs, (land_ref, ssem, rsem) = refs[:len(srcs)], refs[len(srcs):]
        x, y, c = _mesh_pos()
        me, sibling = (x, y, c), (x, y, 1 - c)
        for hf in range(2):
            @pl.when(c == hf)
            def _():
                for s in range(N_CHIPS):
                    for k, r0, p, n in _grad_half(s, 1 - hf):
                        pltpu.make_async_remote_copy(
                            src_ref=src_refs[k].at[pl.ds(r0, n), :], dst_ref=land_ref.at[s, pl.ds(p, n), :],
                            send_sem=ssem, recv_sem=rsem, device_id=sibling, device_id_type=MESH).start()
        whole = pltpu.make_async_remote_copy(src_ref=land_ref, dst_ref=land_ref, send_sem=ssem, recv_sem=rsem,
                                             device_id=me, device_id_type=MESH)
        whole.wait_recv()
        whole.wait_send()

    return pl.pallas_call(
        body, name="rs_core_exchange", in_specs=[ANY] * len(srcs), out_specs=ANY,
        out_shape=jax.ShapeDtypeStruct((N_CHIPS, PK_HALF, D), BF16),
        scratch_shapes=[pltpu.SemaphoreType.DMA(())] * 2,
        compiler_params=pltpu.CompilerParams(has_side_effects=True),
    )(*srcs)


def _rs_chip_start(cp):
    def body(cp_ref, land_ref, s1, s2, s3, r1, r2, r3, cp_thru, land_thru, token):
        x, y, c = _mesh_pos()
        j = 2 * x + y
        for r, s_sem, r_sem in zip((1, 2, 3), (s1, s2, s3), (r1, r2, r3)):
            pltpu.make_async_remote_copy(
                src_ref=cp_ref.at[jnp.bitwise_xor(j, r)], dst_ref=land_ref.at[r - 1], send_sem=s_sem, recv_sem=r_sem,
                device_id=_chip_peer(x, y, c, r), device_id_type=MESH).start()
        token[...] = jnp.zeros_like(token)

    land_shape = (3, PK_HALF, D)
    sem = pltpu.SemaphoreType.DMA(())
    out = pl.pallas_call(
        body, name="rs_chip_start",
        out_shape=(sem,) * 6 + (pltpu.HBM(cp.shape, cp.dtype), pltpu.HBM(land_shape, BF16),
                                jax.ShapeDtypeStruct((8, 128), F32)),
        in_specs=(HBM, HBM), out_specs=(SEM,) * 6 + (HBM, HBM, pl.BlockSpec(memory_space=pltpu.VMEM)),
        input_output_aliases={0: 6, 1: 7},
        compiler_params=pltpu.CompilerParams(has_side_effects=EFFECT),
    )(pltpu.with_memory_space_constraint(cp, pltpu.HBM),
      pltpu.with_memory_space_constraint(lax.empty(land_shape, BF16), pltpu.HBM))
    return out[:6], out[6], out[7], out[8]


def _rs_chip_wait(sems, cp_thru, land_thru, after):
    def body(cp_ref, land_ref, s1, s2, s3, r1, r2, r3, after_ref, cp_out, land_out):
        me = _mesh_pos()
        for r, s_sem, r_sem in zip((1, 2, 3), (s1, s2, s3), (r1, r2, r3)):
            copy = pltpu.make_async_remote_copy(src_ref=cp_ref.at[0], dst_ref=land_ref.at[r - 1], send_sem=s_sem,
                                                recv_sem=r_sem, device_id=me, device_id_type=MESH)
            copy.wait_send()
            copy.wait_recv()

    return pl.pallas_call(
        body, name="rs_chip_wait",
        out_shape=(pltpu.HBM(cp_thru.shape, cp_thru.dtype), pltpu.HBM(land_thru.shape, land_thru.dtype)),
        in_specs=(HBM, HBM) + (SEM,) * 6 + (pl.BlockSpec(memory_space=pl.ANY),), out_specs=(HBM, HBM),
        input_output_aliases={0: 0, 1: 1},
        compiler_params=pltpu.CompilerParams(has_side_effects=EFFECT),
    )(cp_thru, land_thru, *sems, after)


def _rs_finish(full, small):
    n = PK_HALF // RS_SPLIT

    def body(full_in_ref, sm_ref, full_ref, all_ref, lsem, ssem, rsem, sm_s, sm_r):
        del full_in_ref
        x, y, c = _mesh_pos()
        me, sibling = (x, y, c), (x, y, 1 - c)
        my_id = 4 * x + 2 * y + c
        own_sm = pltpu.make_async_copy(sm_ref, all_ref.at[my_id], lsem)
        own_sm.start()
        for q in range(RS_SPLIT):
            rows = full_ref.at[c, pl.ds(q * n, n), :]
            pltpu.make_async_remote_copy(src_ref=rows, dst_ref=rows, send_sem=ssem, recv_sem=rsem,
                                         device_id=sibling, device_id_type=MESH).start()
        half = pltpu.make_async_remote_copy(src_ref=full_ref.at[c], dst_ref=full_ref.at[c], send_sem=ssem,
                                            recv_sem=rsem, device_id=me, device_id_type=MESH)
        smalls = [pltpu.make_async_remote_copy(
            src_ref=sm_ref, dst_ref=all_ref.at[my_id], send_sem=sm_s.at[r - 1], recv_sem=sm_r.at[r - 1],
            device_id=(jnp.bitwise_xor(x, r >> 2), jnp.bitwise_xor(y, (r >> 1) & 1), jnp.bitwise_xor(c, r & 1)),
            device_id_type=MESH) for r in range(1, 8)]
        for cpy in smalls:
            cpy.start()
        half.wait_recv()
        half.wait_send()
        for cpy in smalls:
            cpy.wait_recv()
        for cpy in smalls:
            cpy.wait_send()
        own_sm.wait()

    dma7 = pltpu.SemaphoreType.DMA((7,))
    return pl.pallas_call(
        body, name="rs_finish", in_specs=[ANY, ANY], out_specs=[ANY, ANY],
        out_shape=[jax.ShapeDtypeStruct((2, PK_HALF, D), F32), jax.ShapeDtypeStruct((8, SM_ROWS, 128), F32)],
        scratch_shapes=[pltpu.SemaphoreType.DMA(()), pltpu.SemaphoreType.DMA(()), pltpu.SemaphoreType.DMA(()), dma7, dma7],
        input_output_aliases={0: 0},
        compiler_params=pltpu.CompilerParams(has_side_effects=True),
    )(full, small)


def _rope_tables(s):
    pos = jnp.arange(s, dtype=F32) - PAD_FRONT

    def cs(d):
        inv = ROPE_BASE ** (-jnp.arange(0, d, 2, dtype=F32) / d)
        ang = pos[:, None] * inv[None, :]
        return jnp.cos(ang), jnp.sin(ang)

    c, sn = cs(ROPE_D)
    z = jnp.zeros_like(c)
    ktab = (jnp.concatenate([c, c, z, z], axis=1), jnp.concatenate([-sn, z, z, z], axis=1),
            jnp.concatenate([z, sn, z, z], axis=1))
    one = jnp.ones_like(c)
    qtab = tuple(ATT_SCALE * jnp.concatenate(parts, axis=1) for parts in (
        [one, one, one, one, c, c, z, z], [z, z, z, z, -sn, z, z, z], [z, z, z, z, z, sn, z, z]))
    lane = jnp.arange(HD) == DH + ROPE_D
    qbias = lane.astype(F32)[None, :]
    kbias = jnp.where((pos < 0)[:, None] & lane[None, DH:], NEG_INF, 0.0).astype(F32)
    c, sn = cs(DH)
    return qtab, ktab, qbias, kbias, jnp.concatenate([c, c], axis=1), jnp.concatenate([-sn, sn], axis=1)


def _grad_mm(a, b, m, n, s, name):
    return _matmul(a, b, "tn", m, n, s, min(m, 512), min(n, 1024), s, BF16, name, n_outer=True)


def _local_step(x2, target2, hp0, big, small_w, normed=None, gather_wait=None, reduce_start=None, reduce_wait=None):
    norm_w, qn_w, kvn_w, gn_w, gn_b, fin_w = small_w
    s = x2.shape[0] + BLK
    tm = _row_tile(s, 1408)
    qtab, ktab, qbias, kbias, c128, s128 = _rope_tables(s)
    consts, gamma = _ret_consts()

    h, xn, r1 = _rms_in(x2, hp0, norm_w, s) if normed is None else normed
    proj = _matmul(xn, big, "nt", s, PROJ_W, D, tm, 512, D, BF16, "proj", b_off=(BIG_IN // 512, 0))
    if gather_wait is not None:
        big = gather_wait(big, proj)
    uq = big[BIG_UQ:BIG_UQ + 384].reshape(HEADS, DH + ROPE_D, Q_RANK)
    wq = jnp.pad(uq, ((0, 0), (0, HD - DH - ROPE_D), (0, 0))).reshape(HEADS * HD, Q_RANK)
    wukv = big[BIG_UKV:BIG_UKV + 256].reshape(2 * WIDTH, KV_RANK)
    wbm = big[BIG_BM:BIG_BM + 1024].reshape(D, WIDTH)
    wbr = big[BIG_BR:BIG_BR + 1024].reshape(D, WIDTH)
    cqn, rq, ckvn, rkv, qq, kk, vv = _mla_prep(proj, qn_w, kvn_w, wq, wukv, qtab, ktab, qbias, kbias, s)
    o_mla, y_mla, lse = _attn_fwd(qq, kk, vv, proj, s)
    y_ret, on, rstd, qr, kr, rall = _ret_fwd(proj, gn_w, gn_b, c128, s128, consts, gamma, s)
    u_mla, u_ret, merged = _merge_fwd(y_mla, y_ret, wbm, wbr, proj, s)
    h2 = _out_fwd(merged, big, h, s)
    dh2, dh2b, loss_blk, dfin = _loss_bwd(h2, target2, fin_w, s)

    dproj, du_mla, du_ret = _merge_bwd(dh2b, big, u_mla, u_ret, proj, s)
    g_out = _grad_mm(merged, dh2b, D, D, s, "grad_w_out")
    dproj, do_mla, delta = _branch_mla_bwd(dproj, du_mla, wbm, o_mla, proj, s)
    g_bm = _grad_mm(du_mla, y_mla, D, WIDTH, s, "grad_w_branch_mla")
    dproj, do_ret, dgw, dgb = _branch_ret_bwd(dproj, du_ret, wbr, on, rstd, gn_w, gn_b, proj, s)
    g_br = _grad_mm(du_ret, y_ret, D, WIDTH, s, "grad_w_branch_ret")
    dproj = _ret_bwd(dproj, qr, kr, proj, rall, do_ret, c128, s128, consts, gamma, s)
    dqq, dkk, dvv = _attn_bwd(qq, kk, vv, do_mla, lse, delta.reshape(lse.shape), s)
    dproj, dq, dkv, dqnw, dkvnw = _mla_prep_bwd(dproj, dqq, dkk, dvv, proj, rq, rkv, qn_w, kvn_w, wq, wukv,
                                                qtab, ktab, s)
    g_q = _grad_mm(dq, cqn, HEADS * HD, Q_RANK, s, "grad_w_uq")
    g_ukv = _grad_mm(dkv, ckvn, 2 * WIDTH, KV_RANK, s, "grad_w_ukv")
    g_in = _grad_mm(dproj, xn, PROJ_W, D, s, "grad_w_in")

    g_uq = g_q.reshape(HEADS, HD, Q_RANK)[:, :DH + ROPE_D]
    grads = (g_in, g_out, g_bm.reshape(1024, D), g_br.reshape(1024, D), g_uq.reshape(384, D), g_ukv.reshape(256, D))
    token, travelling = (None, grads) if reduce_start is None else reduce_start(grads)
    dxn = _matmul(dproj, big, "nn", s, D, PROJ_W, _row_tile(s, 384), 512, PROJ_W, F32, "dxn",
                  b_off=(BIG_IN // PROJ_W, 0), n_outer=True, after=token)
    grad_x, gmeta_blk, dnorm = _rms_in_bwd(dxn, h, r1, dh2, norm_w, s)
    small = _small_rows((dnorm, dqnw, dkvnw, dgw, dgb, dfin), loss_blk, gmeta_blk[PAD_FRONT:])
    return grad_x, travelling if reduce_wait is None else reduce_wait(travelling, dnorm), small


def _small_rows(ws, first=None, last=None):
    def part(a, rows):
        a = a.reshape(-1, 128)
        return a if a.shape[0] == rows else jnp.pad(a, ((0, rows - a.shape[0]), (0, 0)))

    bounds = (SM_NORM, SM_QN, SM_KVN, SM_GNW, SM_GNB, SM_FIN, SM_META)
    first = jnp.zeros((SM_NORM, 128), F32) if first is None else first
    last = jnp.zeros((SM_ROWS - SM_META, 128), F32) if last is None else last
    return jnp.concatenate([part(first, SM_NORM)] + [part(w, bounds[k + 1] - bounds[k]) for k, w in enumerate(ws)]
                           + [part(last, SM_ROWS - SM_META)], axis=0)


def kernel(x, meta, norm_w, w_in, mla_q_norm_w, mla_w_uq, mla_kv_norm_w, mla_w_ukv, ret_gn_w, ret_gn_b, w_branch_mla, w_branch_ret, w_out, final_norm_w, loss_target, m_meta, m_norm_w, m_w_in, m_mla_q_norm_w, m_mla_w_uq, m_mla_kv_norm_w, m_mla_w_ukv, m_ret_gn_w, m_ret_gn_b, m_w_branch_mla, m_w_branch_ret, m_w_out, m_final_norm_w, v_meta, v_norm_w, v_w_in, v_mla_q_norm_w, v_mla_w_uq, v_mla_kv_norm_w, v_mla_w_ukv, v_ret_gn_w, v_ret_gn_b, v_w_branch_mla, v_w_branch_ret, v_w_out, v_final_norm_w):
    j = 2 * lax.axis_index("x") + lax.axis_index("y")
    tr = lambda w: w[0].T.reshape(-1, D).astype(BF16)
    pos = jnp.stack([j, lax.axis_index("c")]).astype(jnp.int32)
    put = lambda buf, rows, off: lax.dynamic_update_slice(buf, rows, (off, 0))
    big0 = lax.empty((BIG_ROWS, D), BF16)
    big0 = put(big0, w_out[0].astype(BF16), BIG_OUT + 512 * j)
    big0 = put(big0, tr(w_branch_mla), BIG_BM + 256 * j)
    big0 = put(big0, tr(w_branch_ret), BIG_BR + 256 * j)
    big0 = put(big0, tr(mla_w_uq), BIG_UQ + 96 * j)
    big0 = put(big0, tr(mla_w_ukv), BIG_UKV + 64 * j)
    big0 = put(big0, jnp.zeros((AG_JUNK_END - AG_JUNK_REST, D), BF16), AG_JUNK_REST)
    big0 = put(big0, jnp.zeros((ZERO_ROWS, D), BF16), BIG_IN + IN_WIDTH)
    wt = jnp.concatenate([tr(w_in), jnp.zeros((2 * AG_IN_HALF - IN_SHARD, D), BF16)], axis=0)
    big, h, xn, r1 = _allgather_w_in(big0, wt, meta, x[0], norm_w)
    gather_sems, big = _allgather_rest_start(big)

    def gather_wait(big_travelling, after):
        return _allgather_rest_wait(gather_sems, big_travelling, after)

    small_w = (norm_w, mla_q_norm_w, mla_kv_norm_w, ret_gn_w, ret_gn_b, final_norm_w.reshape(1, D))

    def reduce_start(grads):
        chip_part = _rs_core_add(grads, _rs_core_exchange(grads), pos)
        sems, part_thru, land_thru, token = _rs_chip_start(chip_part)
        return token, (sems, part_thru, land_thru)

    def reduce_wait(state, after):
        return _rs_chip_wait(*state, after)

    grad_x, (chip_part, land2), small = _local_step(x[0], loss_target[0], None, big, small_w, (h, xn, r1), gather_wait,
                                                    reduce_start, reduce_wait)
    full, small_all = _rs_finish(_rs_chip_add(chip_part, land2, pos), small)
    full = full.reshape(PK_ROWS, D)

    untr = lambda lo, hi, rows: full[lo:hi].reshape(rows, -1).T
    grads = {
        "w_out": full[PK_OUT:PK_BM], "w_branch_mla": untr(PK_BM, PK_BR, 512), "w_branch_ret": untr(PK_BR, PK_UQ, 512),
        "mla_w_uq": untr(PK_UQ, PK_UKV, 384), "mla_w_ukv": untr(PK_UKV, PK_PAD, 512),
    }
    big_w = {"mla_w_uq": (mla_w_uq, m_mla_w_uq, v_mla_w_uq),
             "mla_w_ukv": (mla_w_ukv, m_mla_w_ukv, v_mla_w_ukv),
             "w_branch_mla": (w_branch_mla, m_w_branch_mla, v_w_branch_mla),
             "w_branch_ret": (w_branch_ret, m_w_branch_ret, v_w_branch_ret), "w_out": (w_out, m_w_out, v_w_out)}
    res = {"w_in": tuple(a.T[None] for a in _adamw_t(w_in[0].T, full, m_w_in[0].T, v_w_in[0].T, "adamw_w_in"))}
    for name, (w, m, v) in big_w.items():
        d, nm, nv = _adamw(w[0], grads[name], m[0], v[0], "adamw_" + name)
        res[name] = (grads[name][None], d[None], nm[None], nv[None])

    small_m = (m_norm_w, m_mla_q_norm_w, m_mla_kv_norm_w, m_ret_gn_w, m_ret_gn_b, m_final_norm_w.reshape(1, D))
    small_v = (v_norm_w, v_mla_q_norm_w, v_mla_kv_norm_w, v_ret_gn_w, v_ret_gn_b, v_final_norm_w.reshape(1, D))
    gs, ds, ms, vs = _adamw_small(_small_rows(small_w), small_all, _small_rows(small_m), _small_rows(small_v))
    names = ["norm_w", "mla_q_norm_w", "mla_kv_norm_w", "ret_gn_w", "ret_gn_b", "final_norm_w"]
    bounds = [SM_NORM, SM_QN, SM_KVN, SM_GNW, SM_GNB, SM_FIN]
    for k, name in enumerate(names):
        shape = (D,) if name == "final_norm_w" else (1, -1)
        rows = small_w[k].size // 128
        res[name] = tuple(a[bounds[k]:bounds[k] + rows].reshape(shape) for a in (gs, ds, ms, vs))
    g_meta = lax.dynamic_slice_in_dim(gs[SM_META:SM_META + 256].reshape(N_META, D), j * (D // N_CHIPS), D // N_CHIPS, axis=1)
    res["meta"] = (g_meta,) + tuple(_adamw(meta, g_meta, m_meta, v_meta, "adamw_meta"))

    order = ["meta", "norm_w", "w_in", "mla_q_norm_w", "mla_w_uq", "mla_kv_norm_w", "mla_w_ukv", "ret_gn_w", "ret_gn_b",
             "w_branch_mla", "w_branch_ret", "w_out", "final_norm_w"]
    return (gs[0, 0], grad_x[None]) + tuple(res[n][k] for k in range(4) for n in order)
```

```python
import functools
import math

import numpy as np
import jax
import jax.numpy as jnp
from jax import lax
from jax.experimental import pallas as pl
from jax.experimental.pallas import tpu as pltpu

F32 = jnp.float32
BF16 = jnp.bfloat16
MESH = pl.DeviceIdType.MESH

D = 2048
N_META = 16
BLK = 128
PAD_FRONT = BLK - N_META
HEADS = 8
DH = 128
ROPE_D = 64
Q_RANK = 512
KV_RANK = 256
WIDTH = HEADS * DH
ROPE_BASE = 10000.0
NORM_EPS = 1e-6
GN_EPS = 1e-5
NEG_INF = -1e30
ATT_SCALE = (DH + ROPE_D) ** -0.5
RET_SCALE = DH ** -0.5
IN_WIDTH = 10048
N_CHIPS = 4
IN_SHARD = IN_WIDTH // N_CHIPS
ADAM_LR, ADAM_B1, ADAM_B2, ADAM_EPS, ADAM_WD, ADAM_STEP = 0.001, 0.9, 0.999, 1e-08, 0.01, 10

R_Q, R_K, R_V, Z_RET, Z_MLA, GATE0, GATE1 = 0, 1024, 2048, 3072, 4096, 5120, 5120 + D
C_Q = 5120 + 2 * D
C_KV = C_Q + Q_RANK
K_PE = C_KV + KV_RANK
PROJ_W = 10240
IN_RUNS = ((0, 832, C_Q), (832, 1856, Z_MLA), (1856, 4928, R_Q), (4928, 5952, Z_RET), (5952, IN_WIDTH, GATE0))

BIG_IN, BIG_OUT, BIG_BM, BIG_BR, BIG_UQ, BIG_UKV, BIG_JUNK, BIG_ROWS = 0, 10240, 12288, 13312, 14336, 14720, 14976, 15360
ZERO_ROWS = PROJ_W - IN_WIDTH
PK_IN, PK_OUT, PK_BM, PK_BR, PK_UQ, PK_UKV, PK_PAD, PK_ROWS = 0, 2512, 3024, 3280, 3536, 3632, 3696, 3712
PK_HALF = PK_ROWS // 2
SM_LOSS, SM_NORM, SM_QN, SM_KVN, SM_GNW, SM_GNB, SM_FIN, SM_META, SM_ROWS = 0, 8, 24, 32, 40, 48, 56, 72, 328

VMEM_LIMIT = 56 * 1024 * 1024


def _pieces(shard):
    j = shard
    out = [(PK_OUT, BIG_OUT + 512 * j, 512), (PK_BM, BIG_BM + 256 * j, 256), (PK_BR, BIG_BR + 256 * j, 256),
           (PK_UQ, BIG_UQ + 96 * j, 96), (PK_UKV, BIG_UKV + 64 * j, 64)]
    for lo, hi, new in IN_RUNS:
        a, b = max(lo, IN_SHARD * j), min(hi, IN_SHARD * (j + 1))
        if b > a:
            out.append((PK_IN + a - IN_SHARD * j, BIG_IN + new + a - lo, b - a))
    out += [(PK_PAD, BIG_JUNK + 16 * j, 16)]
    return out


AG_JUNK_REST, AG_JUNK_IN, AG_JUNK_END = BIG_JUNK, BIG_JUNK + 128, BIG_JUNK + 192
AG_REST_HALF, AG_IN_HALF = 608, 1264


def _ag_half(shard, w_in, half):
    pieces = [(b, n) for p, b, n in _pieces(shard) if p < PK_PAD and (p < PK_OUT) == w_in]
    pieces.append((AG_JUNK_IN + 16 * shard, 16) if w_in else (AG_JUNK_REST + 32 * shard, 32))
    size = AG_IN_HALF if w_in else AG_REST_HALF
    out, pos = [], 0
    for b, n in pieces:
        s, e = max(pos, half * size), min(pos + n, (half + 1) * size)
        if e > s:
            out.append((s, b + s - pos, e - s))
        pos += n
    assert pos == 2 * size
    return out


def _grad_half(shard, half):
    j = shard
    pieces = [(0, b - BIG_IN, p, n) for p, b, n in _pieces(j) if p < PK_OUT]
    pieces += [(1, 512 * j, PK_OUT, 512), (2, 256 * j, PK_BM, 256), (3, 256 * j, PK_BR, 256), (4, 96 * j, PK_UQ, 96),
               (5, 64 * j, PK_UKV, 64), (0, IN_WIDTH, PK_PAD, PK_ROWS - PK_PAD)]
    lo, hi = half * PK_HALF, (half + 1) * PK_HALF
    out = []
    for k, r0, p, n in pieces:
        s, e = max(p, lo), min(p + n, hi)
        if e > s:
            out.append((k, r0 + s - p, s - lo, e - s))
    assert sum(n for _, _, _, n in out) == PK_HALF
    return out


def _row_tile(rows, cap):
    best = BLK
    for t in range(BLK, cap + 1, BLK):
        if rows % t == 0:
            best = t
    return best


def _cparams(sem):
    return pltpu.CompilerParams(dimension_semantics=sem, vmem_limit_bytes=VMEM_LIMIT)


def _dot(a, b, form):
    dn = {"nt": (((1,), (1,)), ((), ())), "nn": (((1,), (0,)), ((), ())), "tn": (((0,), (0,)), ((), ()))}[form]
    return lax.dot_general(a, b, dn, preferred_element_type=F32)


def _sigmoid(v):
    return 1.0 / (1.0 + jnp.exp(-v))


def _matmul(a, b, form, m, n, k, tm, tn, tk, out_dtype, name, a_off=(0, 0), b_off=(0, 0), n_outer=False, after=None):
    nk = k // tk
    gi, gj = m // tm, n // tn

    def ij(g0, g1):
        return (g1, g0) if n_outer else (g0, g1)

    if form == "nt":
        a_spec = pl.BlockSpec((tm, tk), lambda g0, g1, kk: (ij(g0, g1)[0] + a_off[0], kk + a_off[1]))
        b_spec = pl.BlockSpec((tn, tk), lambda g0, g1, kk: (ij(g0, g1)[1] + b_off[0], kk + b_off[1]))
    elif form == "nn":
        a_spec = pl.BlockSpec((tm, tk), lambda g0, g1, kk: (ij(g0, g1)[0] + a_off[0], kk + a_off[1]))
        b_spec = pl.BlockSpec((tk, tn), lambda g0, g1, kk: (kk + b_off[0], ij(g0, g1)[1] + b_off[1]))
    else:
        a_spec = pl.BlockSpec((tk, tm), lambda g0, g1, kk: (kk + a_off[0], ij(g0, g1)[0] + a_off[1]))
        b_spec = pl.BlockSpec((tk, tn), lambda g0, g1, kk: (kk + b_off[0], ij(g0, g1)[1] + b_off[1]))
    o_spec = pl.BlockSpec((tm, tn), lambda g0, g1, kk: ij(g0, g1))

    def body(a_ref, b_ref, *rest):
        o_ref, *acc = rest[0 if after is None else 1:]
        p = _dot(a_ref[...], b_ref[...], form)
        if nk == 1:
            o_ref[...] = p.astype(o_ref.dtype)
        else:
            acc_ref, = acc
            kk = pl.program_id(2)

            @pl.when(kk == 0)
            def _():
                acc_ref[...] = p

            @pl.when(kk > 0)
            def _():
                acc_ref[...] += p

            @pl.when(kk == nk - 1)
            def _():
                o_ref[...] = acc_ref[...].astype(o_ref.dtype)

    extra = [] if after is None else [after]
    return pl.pallas_call(
        body, name=name, grid=(gj, gi, nk) if n_outer else (gi, gj, nk),
        in_specs=[a_spec, b_spec] + [pl.BlockSpec(memory_space=pl.ANY)] * len(extra), out_specs=o_spec,
        out_shape=jax.ShapeDtypeStruct((m, n), out_dtype),
        scratch_shapes=[] if nk == 1 else [pltpu.VMEM((tm, tn), F32)],
        compiler_params=_cparams(("parallel", "parallel", "arbitrary")),
    )(a, b, *extra)


def _rms_in(x, hp0, norm_w, s):
    def body(x_ref, hp0_ref, w_ref, h_ref, xn_ref, r_ref):
        def run(hv):
            r = lax.rsqrt(jnp.mean(hv * hv, axis=-1, keepdims=True) + NORM_EPS)
            h_ref[...] = hv
            xn_ref[...] = (hv * r * w_ref[...]).astype(BF16)
            r_ref[...] = r

        @pl.when(pl.program_id(0) == 0)
        def _():
            run(hp0_ref[...])

        @pl.when(pl.program_id(0) > 0)
        def _():
            run(x_ref[...])

    return pl.pallas_call(
        body, name="rms_in", grid=(s // BLK,),
        in_specs=[pl.BlockSpec((BLK, D), lambda i: (jnp.maximum(i - 1, 0), 0)),
                  pl.BlockSpec((BLK, D), lambda i: (0, 0)), pl.BlockSpec((1, D), lambda i: (0, 0))],
        out_specs=[pl.BlockSpec((BLK, D), lambda i: (i, 0)), pl.BlockSpec((BLK, D), lambda i: (i, 0)),
                   pl.BlockSpec((BLK, 1), lambda i: (i, 0))],
        out_shape=[jax.ShapeDtypeStruct((s, D), F32), jax.ShapeDtypeStruct((s, D), BF16),
                   jax.ShapeDtypeStruct((s, 1), F32)],
        compiler_params=_cparams(("arbitrary",)),
    )(x, hp0, norm_w)


def _rope64(t, c, sa, sb):
    return t * c + pltpu.roll(t, t.shape[1] - ROPE_D // 2, 1) * sa + pltpu.roll(t, ROPE_D // 2, 1) * sb


def _rope128(t, c, sg):
    return t * c + pltpu.roll(t, DH // 2, 1) * sg


HD = 2 * DH


def _mla_prep(proj, qn_w, kvn_w, wq, wukv, qtab, ktab, qbias, kbias, s):
    tm = _row_tile(s, 384)

    def body(cq_ref, ckv_ref, kpe_ref, qnw_ref, kvnw_ref, wq_ref, wukv_ref, qc_ref, qa_ref, qb_ref,
             kc_ref, ka_ref, kb_ref, qbias_ref, kbias_ref, cqn_ref, rq_ref, ckvn_ref, rkv_ref, qq_ref, kk_ref, vv_ref):
        cq = cq_ref[...].astype(F32)
        rq = lax.rsqrt(jnp.mean(cq * cq, axis=-1, keepdims=True) + NORM_EPS)
        cqn = (cq * rq * qnw_ref[...]).astype(BF16)
        ckv = ckv_ref[...].astype(F32)
        rkv = lax.rsqrt(jnp.mean(ckv * ckv, axis=-1, keepdims=True) + NORM_EPS)
        ckvn = (ckv * rkv * kvnw_ref[...]).astype(BF16)
        cqn_ref[...] = cqn
        rq_ref[...] = rq
        ckvn_ref[...] = ckvn
        rkv_ref[...] = rkv
        q = _dot(cqn, wq_ref[...], "nt")
        kv = _dot(ckvn, wukv_ref[...], "nt")
        kp = (_rope64(kpe_ref[...].astype(F32), kc_ref[...], ka_ref[...], kb_ref[...]) + kbias_ref[...]).astype(BF16)
        qc, qa, qb, qbias = qc_ref[...], qa_ref[...], qb_ref[...], qbias_ref[...]
        ones = jnp.ones((tm, DH), BF16)
        for h in range(HEADS):
            lo, mid, hi = h * HD, h * HD + DH, (h + 1) * HD
            qq_ref[:, lo:hi] = (_rope64(q[:, lo:hi], qc, qa, qb) + qbias).astype(BF16)
            kk_ref[:, lo:mid] = kv[:, lo:mid].astype(BF16)
            kk_ref[:, mid:hi] = kp
            vv_ref[:, lo:mid] = kv[:, mid:hi].astype(BF16)
            vv_ref[:, mid:hi] = ones

    row = lambda w, cb: pl.BlockSpec((tm, w), lambda i: (i, cb))
    full = lambda a: pl.BlockSpec(a.shape, lambda i: (0, 0))
    wide = jax.ShapeDtypeStruct((s, HEADS * HD), BF16)
    return pl.pallas_call(
        body, name="mla_prep", grid=(s // tm,),
        in_specs=[row(Q_RANK, C_Q // Q_RANK), row(KV_RANK, C_KV // KV_RANK), row(DH, K_PE // DH),
                  full(qn_w), full(kvn_w), full(wq), full(wukv), row(HD, 0), row(HD, 0), row(HD, 0),
                  row(DH, 0), row(DH, 0), row(DH, 0), full(qbias), row(DH, 0)],
        out_specs=[row(Q_RANK, 0), row(1, 0), row(KV_RANK, 0), row(1, 0), row(HEADS * HD, 0), row(HEADS * HD, 0),
                   row(HEADS * HD, 0)],
        out_shape=[jax.ShapeDtypeStruct((s, Q_RANK), BF16), jax.ShapeDtypeStruct((s, 1), F32),
                   jax.ShapeDtypeStruct((s, KV_RANK), BF16), jax.ShapeDtypeStruct((s, 1), F32), wide, wide, wide],
        compiler_params=_cparams(("parallel",)),
    )(proj, proj, proj, qn_w, kvn_w, wq, wukv, *qtab, *ktab, qbias, kbias)


def _diag_mask(t):
    return lax.broadcasted_iota(jnp.int32, (t, t), 0) <= lax.broadcasted_iota(jnp.int32, (t, t), 1)


def _silu(z):
    return z * _sigmoid(z)


def _attn_fwd(qq, kk, vv, proj, s):
    t = _row_tile(s, 384)
    n = s // t

    def body(q_ref, k_ref, v_ref, z_ref, o_ref, y_ref, lse_ref, acc_ref, m_ref):
        qi = pl.program_id(1)
        q = q_ref[...]
        m_ref[...] = jnp.full(m_ref.shape, NEG_INF, F32)
        acc_ref[...] = jnp.zeros(acc_ref.shape, F32)

        def keys(ki):
            return pl.ds(pl.multiple_of(ki * t, t), t)

        def scores(ki):
            return _dot(k_ref[keys(ki), :], q, "nt")

        def tile(ki, st):
            m_old = m_ref[...]
            m_new = jnp.maximum(m_old, jnp.max(st, axis=0, keepdims=True))
            pt = jnp.exp(st - m_new).astype(BF16)
            acc_ref[...] = jnp.exp(m_old - m_new) * acc_ref[...] + _dot(v_ref[keys(ki), :], pt, "tn")
            m_ref[...] = m_new

        def inner(ki, st):
            st_after = scores(ki + 1)
            tile(ki, st)
            return st_after

        st_diag = lax.fori_loop(0, qi, inner, scores(0))
        tile(qi, jnp.where(_diag_mask(t), st_diag, NEG_INF))

        l = acc_ref[DH:DH + 1, :]
        o = (acc_ref[:DH, :] / l).T
        o_ref[...] = o.astype(BF16)
        y_ref[...] = (o * _silu(z_ref[...].astype(F32))).astype(BF16)
        lse_ref[0, 0] = m_ref[...] + jnp.log(l)

    qtile = pl.BlockSpec((t, DH), lambda h, i: (i, h))
    head = pl.BlockSpec((s, HD), lambda h, i: (0, h))
    return pl.pallas_call(
        body, name="attn_fwd", grid=(HEADS, s // t),
        in_specs=[pl.BlockSpec((t, HD), lambda h, i: (i, h)), head, head,
                  pl.BlockSpec((t, DH), lambda h, i: (i, Z_MLA // DH + h))],
        out_specs=[qtile, qtile, pl.BlockSpec((1, 1, 1, t), lambda h, i: (h, i, 0, 0))],
        out_shape=[jax.ShapeDtypeStruct((s, WIDTH), BF16), jax.ShapeDtypeStruct((s, WIDTH), BF16),
                   jax.ShapeDtypeStruct((HEADS, s // t, 1, t), F32)],
        scratch_shapes=[pltpu.VMEM((HD, t), F32), pltpu.VMEM((1, t), F32)],
        compiler_params=_cparams(("parallel", "arbitrary")),
    )(qq, kk, vv, proj)


RET_CHUNKS = 3


def _ret_consts():
    log_g = np.log1p(-(2.0 ** (-5.0 - np.arange(HEADS, dtype=np.float64))))
    n = np.arange(BLK, dtype=np.float64)
    diff = n[:, None] - n[None, :]
    decay = np.where(diff >= 0, np.exp(log_g[:, None, None] * np.maximum(diff, 0.0)), 0.0)
    zeta = np.exp(log_g[:, None] * (BLK - 1.0 - n))[:, :, None]
    xi = np.exp(log_g[:, None] * (n + 1.0))[:, :, None]
    gamma = [float(np.float32(np.exp(g * BLK))) for g in log_g]
    return (jnp.asarray(decay, F32), jnp.asarray(zeta, F32), jnp.asarray(xi, F32)), gamma


def _ret_fwd(proj, gn_w, gn_b, c128, s128, consts, gamma, s):
    nb = s // BLK
    cps = RET_CHUNKS if nb % RET_CHUNKS == 0 else 1
    decay, zeta, xi = consts

    def body(rq_ref, rk_ref, rv_ref, z_ref, gw_ref, gb_ref, c_ref, s_ref, dm_ref, zt_ref, xi_ref,
             y_ref, on_ref, rstd_ref, qr_ref, kr_ref, rall_ref, state):
        @pl.when(pl.program_id(0) == 0)
        def _():
            state[...] = jnp.zeros_like(state)

        for sub in range(cps):
            rows = slice(sub * BLK, (sub + 1) * BLK)
            c, sg = c_ref[rows, :], s_ref[rows, :]
            for h in range(HEADS):
                sl = slice(h * DH, (h + 1) * DH)
                q = _rope128(rq_ref[rows, sl].astype(F32), c, sg).astype(BF16)
                kf = _rope128(rk_ref[rows, sl].astype(F32), c, sg) * RET_SCALE
                k = kf.astype(BF16)
                v = rv_ref[rows, sl]
                qr_ref[rows, sl] = q
                kr_ref[rows, sl] = k
                r_prev = state[h]
                rall_ref[sub, h] = r_prev
                a = _dot(q, k, "nt") * dm_ref[h]
                o = _dot(a.astype(BF16), v, "nn") + _dot(q, r_prev.astype(BF16), "nn") * xi_ref[h]
                state[h] = r_prev * gamma[h] + _dot((kf * zt_ref[h]).astype(BF16), v, "tn")
                mu = jnp.mean(o, axis=-1, keepdims=True)
                var = jnp.mean(jnp.square(o - mu), axis=-1, keepdims=True)
                rstd = lax.rsqrt(var + GN_EPS)
                on = (o - mu) * rstd
                rstd_ref[h, rows, :] = rstd
                on_ref[rows, sl] = on.astype(BF16)
                ogn = on * gw_ref[:, sl] + gb_ref[:, sl]
                y_ref[rows, sl] = (ogn * _silu(z_ref[rows, sl].astype(F32))).astype(BF16)

    seg = lambda cb: pl.BlockSpec((cps * BLK, WIDTH), lambda i: (i, cb))
    full = lambda a: pl.BlockSpec(a.shape, lambda i: (0,) * a.ndim)
    tab = pl.BlockSpec((cps * BLK, DH), lambda i: (i, 0))
    return pl.pallas_call(
        body, name="ret_fwd", grid=(nb // cps,),
        in_specs=[seg(R_Q // WIDTH), seg(R_K // WIDTH), seg(R_V // WIDTH), seg(Z_RET // WIDTH), full(gn_w), full(gn_b),
                  tab, tab, full(decay), full(zeta), full(xi)],
        out_specs=[seg(0), seg(0), pl.BlockSpec((HEADS, cps * BLK, 1), lambda i: (0, i, 0)), seg(0), seg(0),
                   pl.BlockSpec((cps, HEADS, DH, DH), lambda i: (i, 0, 0, 0))],
        out_shape=[jax.ShapeDtypeStruct((s, WIDTH), BF16), jax.ShapeDtypeStruct((s, WIDTH), BF16),
                   jax.ShapeDtypeStruct((HEADS, s, 1), F32), jax.ShapeDtypeStruct((s, WIDTH), BF16),
                   jax.ShapeDtypeStruct((s, WIDTH), BF16), jax.ShapeDtypeStruct((nb, HEADS, DH, DH), F32)],
        scratch_shapes=[pltpu.VMEM((HEADS, DH, DH), F32)],
        compiler_params=_cparams(("arbitrary",)),
    )(proj, proj, proj, proj, gn_w, gn_b, c128, s128, decay, zeta, xi)


def _merge_fwd(y_mla, y_ret, wbm, wbr, proj, s):
    tm, tn = _row_tile(s, 1408), 512

    def body(ym_ref, yr_ref, wm_ref, wr_ref, g0_ref, g1_ref, um_ref, ur_ref, mg_ref):
        um = _dot(ym_ref[...], wm_ref[...], "nt")
        ur = _dot(yr_ref[...], wr_ref[...], "nt")
        um_ref[...] = um.astype(BF16)
        ur_ref[...] = ur.astype(BF16)
        mg_ref[...] = (_sigmoid(g0_ref[...].astype(F32)) * um + _sigmoid(g1_ref[...].astype(F32)) * ur).astype(BF16)

    yspec = pl.BlockSpec((tm, WIDTH), lambda i, j: (i, 0))
    wspec = pl.BlockSpec((tn, WIDTH), lambda i, j: (j, 0))
    ospec = pl.BlockSpec((tm, tn), lambda i, j: (i, j))
    return pl.pallas_call(
        body, name="merge_fwd", grid=(s // tm, D // tn),
        in_specs=[yspec, yspec, wspec, wspec, pl.BlockSpec((tm, tn), lambda i, j: (i, GATE0 // tn + j)),
                  pl.BlockSpec((tm, tn), lambda i, j: (i, GATE1 // tn + j))],
        out_specs=[ospec, ospec, ospec],
        out_shape=[jax.ShapeDtypeStruct((s, D), BF16)] * 3,
        compiler_params=_cparams(("parallel", "parallel")),
    )(y_mla, y_ret, wbm, wbr, proj, proj)


def _out_fwd(merged, big, h, s):
    tm, tn = _row_tile(s, 1408), 512

    def body(m_ref, w_ref, h_ref, o_ref):
        o_ref[...] = h_ref[...] + _dot(m_ref[...], w_ref[...], "nn")

    return pl.pallas_call(
        body, name="out_fwd", grid=(s // tm, D // tn),
        in_specs=[pl.BlockSpec((tm, D), lambda i, j: (i, 0)), pl.BlockSpec((D, tn), lambda i, j: (BIG_OUT // D, j)),
                  pl.BlockSpec((tm, tn), lambda i, j: (i, j))],
        out_specs=pl.BlockSpec((tm, tn), lambda i, j: (i, j)),
        out_shape=jax.ShapeDtypeStruct((s, D), F32),
        compiler_params=_cparams(("parallel", "parallel")),
    )(merged, big, h)


def _loss_bwd(h2, target, fin_w, s):
    nb = s // BLK

    def body(h2_ref, t_ref, w_ref, dh_ref, dhb_ref, loss_ref, dw_ref):
        i = pl.program_id(0)

        @pl.when(i == 0)
        def _():
            dh_ref[...] = jnp.zeros_like(dh_ref)
            dhb_ref[...] = jnp.zeros_like(dhb_ref)
            loss_ref[...] = jnp.zeros_like(loss_ref)
            dw_ref[...] = jnp.zeros_like(dw_ref)

        @pl.when(i > 0)
        def _():
            hv = h2_ref[...]
            w = w_ref[...]
            r = lax.rsqrt(jnp.mean(hv * hv, axis=-1, keepdims=True) + NORM_EPS)
            nrm = hv * r
            e = nrm * w - t_ref[...]
            loss_ref[...] += jnp.full(loss_ref.shape, 0.5 / D, F32) * jnp.sum(e * e)
            dy = e * (1.0 / D)
            dw_ref[...] += jnp.sum(dy * nrm, axis=0, keepdims=True)
            g = dy * w
            dh = r * (g - nrm * jnp.mean(g * nrm, axis=-1, keepdims=True))
            dh_ref[...] = dh
            dhb_ref[...] = dh.astype(BF16)

    blk = pl.BlockSpec((BLK, D), lambda i: (i, 0))
    return pl.pallas_call(
        body, name="loss_bwd", grid=(nb,),
        in_specs=[blk, pl.BlockSpec((BLK, D), lambda i: (jnp.maximum(i - 1, 0), 0)), pl.BlockSpec((1, D), lambda i: (0, 0))],
        out_specs=[blk, blk, pl.BlockSpec((8, 128), lambda i: (0, 0)), pl.BlockSpec((1, D), lambda i: (0, 0))],
        out_shape=[jax.ShapeDtypeStruct((s, D), F32), jax.ShapeDtypeStruct((s, D), BF16),
                   jax.ShapeDtypeStruct((8, 128), F32), jax.ShapeDtypeStruct((1, D), F32)],
        compiler_params=_cparams(("arbitrary",)),
    )(h2, target, fin_w)


def _merge_bwd(dh2b, big, u_mla, u_ret, proj, s):
    tm, tn = _row_tile(s, 1408), 512

    def body(d_ref, w_ref, um_ref, ur_ref, gate_ref, dproj_ref, dum_ref, dur_ref, dm_ref):
        branch, j = pl.program_id(1), pl.program_id(2)
        cols = pl.ds(pl.multiple_of(j * tn, tn), tn)

        @pl.when(branch == 0)
        def _():
            dm_ref[:, cols] = _dot(d_ref[...], w_ref[...], "nt")

        dm = dm_ref[:, cols]
        gt = _sigmoid(gate_ref[...].astype(F32))

        @pl.when(branch == 0)
        def _():
            dproj_ref[...] = (dm * um_ref[...].astype(F32) * gt * (1.0 - gt)).astype(BF16)
            dum_ref[...] = (dm * gt).astype(BF16)

        @pl.when(branch == 1)
        def _():
            dproj_ref[...] = (dm * ur_ref[...].astype(F32) * gt * (1.0 - gt)).astype(BF16)
            dur_ref[...] = (dm * gt).astype(BF16)

    last = D // tn - 1
    mla = pl.BlockSpec((tm, tn), lambda i, b, j: (i, jnp.where(b == 0, j, last)))
    ret = pl.BlockSpec((tm, tn), lambda i, b, j: (i, jnp.where(b == 0, 0, j)))
    gate = pl.BlockSpec((tm, tn), lambda i, b, j: (i, GATE0 // tn + b * (D // tn) + j))
    return pl.pallas_call(
        body, name="merge_bwd", grid=(s // tm, 2, D // tn),
        in_specs=[pl.BlockSpec((tm, D), lambda i, b, j: (i, 0)),
                  pl.BlockSpec((tn, D), lambda i, b, j: (BIG_OUT // tn + jnp.where(b == 0, j, last), 0)), mla, ret, gate],
        out_specs=[gate, mla, ret],
        out_shape=[jax.ShapeDtypeStruct((s, PROJ_W), BF16), jax.ShapeDtypeStruct((s, D), BF16),
                   jax.ShapeDtypeStruct((s, D), BF16)],
        scratch_shapes=[pltpu.VMEM((tm, D), F32)],
        compiler_params=_cparams(("parallel", "arbitrary", "arbitrary")),
    )(dh2b, big, u_mla, u_ret, proj)


def _dsilu(z):
    sg = _sigmoid(z)
    return sg * (1.0 + z * (1.0 - sg))


def _branch_mla_bwd(dproj, du, wbm, o_mla, proj, s):
    tm = _row_tile(s, 384)

    def body(dproj_in, du_ref, w_ref, o_ref, z_ref, dz_ref, do_ref, delta_ref):
        del dproj_in
        dy = _dot(du_ref[...], w_ref[...], "nn")
        z = z_ref[...].astype(F32)
        o = o_ref[...].astype(F32)
        do = dy * _silu(z)
        do_ref[...] = do.astype(BF16)
        dz_ref[...] = (dy * o * _dsilu(z)).astype(BF16)
        prod = do * o
        for h in range(HEADS):
            delta_ref[h] = jnp.sum(prod[:, h * DH:(h + 1) * DH], axis=-1, keepdims=True)

    row = lambda w, cb: pl.BlockSpec((tm, w), lambda i: (i, cb))
    return pl.pallas_call(
        body, name="branch_mla_bwd", grid=(s // tm,),
        in_specs=[ANY, row(D, 0), pl.BlockSpec((D, WIDTH), lambda i: (0, 0)), row(WIDTH, 0),
                  row(WIDTH, Z_MLA // WIDTH)],
        out_specs=[row(WIDTH, Z_MLA // WIDTH), row(WIDTH, 0), pl.BlockSpec((HEADS, tm, 1), lambda i: (0, i, 0))],
        out_shape=[jax.ShapeDtypeStruct((s, PROJ_W), BF16), jax.ShapeDtypeStruct((s, WIDTH), BF16),
                   jax.ShapeDtypeStruct((HEADS, s, 1), F32)],
        input_output_aliases={0: 0},
        compiler_params=_cparams(("parallel",)),
    )(dproj, du, wbm, o_mla, proj)


def _branch_ret_bwd(dproj, du, wbr, on, rstd, gn_w, gn_b, proj, s):
    tm = _row_tile(s, 384)

    def body(dproj_in, du_ref, w_ref, on_ref, rstd_ref, gw_ref, gb_ref, z_ref, dz_ref, do_ref, dgw_ref, dgb_ref):
        del dproj_in

        @pl.when(pl.program_id(0) == 0)
        def _():
            dgw_ref[...] = jnp.zeros_like(dgw_ref)
            dgb_ref[...] = jnp.zeros_like(dgb_ref)

        dy = _dot(du_ref[...], w_ref[...], "nn")
        z = z_ref[...].astype(F32)
        on = on_ref[...].astype(F32)
        gw = gw_ref[...]
        dogn = dy * _silu(z)
        dz_ref[...] = (dy * (on * gw + gb_ref[...]) * _dsilu(z)).astype(BF16)
        dgw_ref[...] += jnp.sum(dogn * on, axis=0, keepdims=True)
        dgb_ref[...] += jnp.sum(dogn, axis=0, keepdims=True)
        don = dogn * gw
        for h in range(HEADS):
            sl = slice(h * DH, (h + 1) * DH)
            dn, nh = don[:, sl], on[:, sl]
            do = rstd_ref[h] * (dn - jnp.mean(dn, axis=-1, keepdims=True)
                                - nh * jnp.mean(dn * nh, axis=-1, keepdims=True))
            do_ref[:, sl] = do.astype(BF16)

    row = lambda w, cb: pl.BlockSpec((tm, w), lambda i: (i, cb))
    vec = pl.BlockSpec((1, WIDTH), lambda i: (0, 0))
    return pl.pallas_call(
        body, name="branch_ret_bwd", grid=(s // tm,),
        in_specs=[ANY, row(D, 0), pl.BlockSpec((D, WIDTH), lambda i: (0, 0)), row(WIDTH, 0),
                  pl.BlockSpec((HEADS, tm, 1), lambda i: (0, i, 0)), vec, vec, row(WIDTH, Z_RET // WIDTH)],
        out_specs=[row(WIDTH, Z_RET // WIDTH), row(WIDTH, 0), vec, vec],
        out_shape=[jax.ShapeDtypeStruct((s, PROJ_W), BF16), jax.ShapeDtypeStruct((s, WIDTH), BF16)]
        + [jax.ShapeDtypeStruct((1, WIDTH), F32)] * 2,
        input_output_aliases={0: 0},
        compiler_params=_cparams(("arbitrary",)),
    )(dproj, du, wbr, on, rstd, gn_w, gn_b, proj)


def _ret_bwd(dproj, qr, kr, proj, rall, do_ret, c128, s128, consts, gamma, s):
    cps = RET_CHUNKS if (s // BLK) % RET_CHUNKS == 0 else 1
    nb = s // (cps * BLK)
    decay, zeta, xi = consts

    def body(dproj_in, q_ref, k_ref, v_ref, r_ref, do_ref, c_ref, s_ref, dm_ref, zt_ref, xi_ref, out_ref, gstate):
        del dproj_in
        dq_ref, dk_ref, dv_ref = (out_ref.at[:, pl.ds(off, WIDTH)] for off in (R_Q, R_K, R_V))

        @pl.when(pl.program_id(0) == 0)
        def _():
            gstate[...] = jnp.zeros_like(gstate)

        for sub in reversed(range(cps)):
            rows = slice(sub * BLK, (sub + 1) * BLK)
            c, sg = c_ref[rows, :], s_ref[rows, :]
            for h in range(HEADS):
                sl = slice(h * DH, (h + 1) * DH)
                q, k, v, do = q_ref[rows, sl], k_ref[rows, sl], v_ref[rows, sl], do_ref[rows, sl]
                dm = dm_ref[h]
                g_next = gstate[h]
                gb = g_next.astype(BF16)
                a = (_dot(q, k, "nt") * dm).astype(BF16)
                da = (_dot(do, v, "nt") * dm).astype(BF16)
                dox = (do.astype(F32) * xi_ref[h]).astype(BF16)
                dq = _dot(da, k, "nn") + _dot(dox, r_ref[sub, h].astype(BF16), "nt")
                dk = _dot(da, q, "tn") + _dot(v, gb, "nt") * zt_ref[h]
                kz = (k.astype(F32) * zt_ref[h]).astype(BF16)
                dv = _dot(a, do, "tn") + _dot(kz, gb, "nn")
                gstate[h] = g_next * gamma[h] + _dot(q, dox, "tn")
                dk = dk * RET_SCALE
                dq_ref[rows, sl] = _rope128(dq, c, -sg).astype(BF16)
                dk_ref[rows, sl] = _rope128(dk, c, -sg).astype(BF16)
                dv_ref[rows, sl] = dv.astype(BF16)

    rev = lambda cb: pl.BlockSpec((cps * BLK, WIDTH), lambda i: (nb - 1 - i, cb))
    full = lambda a: pl.BlockSpec(a.shape, lambda i: (0,) * a.ndim)
    tab = pl.BlockSpec((cps * BLK, DH), lambda i: (nb - 1 - i, 0))
    return pl.pallas_call(
        body, name="ret_bwd", grid=(nb,),
        in_specs=[ANY, rev(0), rev(0), rev(R_V // WIDTH),
                  pl.BlockSpec((cps, HEADS, DH, DH), lambda i: (nb - 1 - i, 0, 0, 0)),
                  rev(0), tab, tab, full(decay), full(zeta), full(xi)],
        out_specs=pl.BlockSpec((cps * BLK, 3 * WIDTH), lambda i: (nb - 1 - i, R_Q // (3 * WIDTH))),
        out_shape=jax.ShapeDtypeStruct((s, PROJ_W), BF16),
        scratch_shapes=[pltpu.VMEM((HEADS, DH, DH), F32)],
        input_output_aliases={0: 0},
        compiler_params=_cparams(("arbitrary",)),
    )(dproj, qr, kr, proj, rall, do_ret, c128, s128, decay, zeta, xi)


def _attn_bwd(qq, kk, vv, do, lse, delta, s):
    t = _row_tile(s, 384)
    n = s // t

    def body(q_ref, k_ref, v_ref, do_ref, lse_ref, delta_ref, dq_ref, dk_ref, dv_ref, dk_acc, dv_acc):
        ki = pl.program_id(1)

        @pl.when(ki == 0)
        def _():
            dq_ref[...] = jnp.zeros(dq_ref.shape, F32)

        k, v = k_ref[...], v_ref[...]
        dk_acc[...] = jnp.zeros(dk_acc.shape, F32)
        dv_acc[...] = jnp.zeros(dv_acc.shape, F32)

        def rows(qi):
            return pl.ds(pl.multiple_of(qi * t, t), t)

        def products(qi):
            return _dot(k, q_ref[rows(qi), :], "nt"), _dot(v, do_ref[rows(qi), :], "nt")

        def tile(qi, st, dpt):
            q, dov = q_ref[rows(qi), :], do_ref[rows(qi), :]
            pt = jnp.exp(st - lse_ref[0, qi])
            dv_acc[...] += _dot(pt.astype(BF16), dov, "nn")
            dst = (pt * (dpt - delta_ref[0, qi])).astype(BF16)
            dk_acc[...] += _dot(dst, q, "nn")
            dq_ref[rows(qi), :] += _dot(dst, k, "tn")

        def inner(qi, carry):
            after = products(jnp.minimum(qi + 1, n - 1))
            tile(qi, *carry)
            return after

        st, dpt = products(ki)
        ahead = products(jnp.minimum(ki + 1, n - 1))
        tile(ki, jnp.where(_diag_mask(t), st, NEG_INF), dpt)
        lax.fori_loop(ki + 1, n, inner, ahead)

        dk_ref[...] = dk_acc[...].astype(BF16)
        dv_ref[...] = dv_acc[...].astype(BF16)

    head = lambda w: pl.BlockSpec((s, w), lambda h, i: (0, h))
    stat = pl.BlockSpec((1, n, 1, t), lambda h, i: (h, 0, 0, 0))
    return pl.pallas_call(
        body, name="attn_bwd", grid=(HEADS, n),
        in_specs=[head(HD), pl.BlockSpec((t, HD), lambda h, i: (i, h)), pl.BlockSpec((t, DH), lambda h, i: (i, 2 * h)),
                  head(DH), stat, stat],
        out_specs=[head(HD), pl.BlockSpec((t, HD), lambda h, i: (i, h)), pl.BlockSpec((t, DH), lambda h, i: (i, h))],
        out_shape=[jax.ShapeDtypeStruct((s, HEADS * HD), F32), jax.ShapeDtypeStruct((s, HEADS * HD), BF16),
                   jax.ShapeDtypeStruct((s, WIDTH), BF16)],
        scratch_shapes=[pltpu.VMEM((t, HD), F32), pltpu.VMEM((t, DH), F32)],
        compiler_params=_cparams(("parallel", "arbitrary")),
    )(qq, kk, vv, do, lse, delta)


def _rms_bwd(dn, nhat, r, w):
    g = dn * w
    return r * (g - nhat * jnp.mean(g * nhat, axis=-1, keepdims=True)), dn * nhat


def _mla_prep_bwd(dproj, dqq, dkk, dvv, proj, rq, rkv, qn_w, kvn_w, wq, wukv, qtab, ktab, s):
    tm = _row_tile(s, 384)
    tail = PROJ_W - C_Q

    def body(dproj_in, dqq_ref, dkk_ref, dvv_ref, cq_ref, ckv_ref, rq_ref, rkv_ref, qnw_ref, kvnw_ref, wq_ref,
             wukv_ref, qc_ref, qa_ref, qb_ref, kc_ref, ka_ref, kb_ref,
             tail_ref, dq_ref, dkv_ref, dqnw_ref, dkvnw_ref):
        del dproj_in
        dcq_ref = tail_ref.at[:, pl.ds(0, Q_RANK)]
        dckv_ref = tail_ref.at[:, pl.ds(C_KV - C_Q, KV_RANK)]
        dkpe_ref = tail_ref.at[:, pl.ds(K_PE - C_Q, 2 * DH)]

        @pl.when(pl.program_id(0) == 0)
        def _():
            dqnw_ref[...] = jnp.zeros_like(dqnw_ref)
            dkvnw_ref[...] = jnp.zeros_like(dkvnw_ref)

        qc, qa, qb = qc_ref[...], qa_ref[...], qb_ref[...]
        dkp = jnp.zeros((tm, DH), F32)
        for h in range(HEADS):
            lo, mid, hi = h * HD, h * HD + DH, (h + 1) * HD
            dq_ref[:, lo:hi] = _rope64(dqq_ref[:, lo:hi], qc, -qa, -qb).astype(BF16)
            dkv_ref[:, lo:mid] = dkk_ref[:, lo:mid]
            dkv_ref[:, mid:hi] = dvv_ref[:, h * DH:(h + 1) * DH]
            dkp = dkp + dkk_ref[:, mid:hi].astype(F32)
        dcqn = _dot(dq_ref[...], wq_ref[...], "nn")
        rq_v = rq_ref[...]
        dcq, prod = _rms_bwd(dcqn, cq_ref[...].astype(F32) * rq_v, rq_v, qnw_ref[...])
        dcq_ref[...] = dcq.astype(BF16)
        dqnw_ref[...] += jnp.sum(prod, axis=0, keepdims=True)
        dckvn = _dot(dkv_ref[...], wukv_ref[...], "nn")
        rkv_v = rkv_ref[...]
        dckv, prod = _rms_bwd(dckvn, ckv_ref[...].astype(F32) * rkv_v, rkv_v, kvnw_ref[...])
        dckv_ref[...] = dckv.astype(BF16)
        dkvnw_ref[...] += jnp.sum(prod, axis=0, keepdims=True)
        dkpe_ref[:, :DH] = _rope64(dkp, kc_ref[...], -ka_ref[...], -kb_ref[...]).astype(BF16)
        dkpe_ref[:, DH:] = jnp.zeros((tm, DH), BF16)

    row = lambda w, cb: pl.BlockSpec((tm, w), lambda i: (i, cb))
    full = lambda a: pl.BlockSpec(a.shape, lambda i: (0, 0))
    wide = jax.ShapeDtypeStruct((s, HEADS * HD), BF16)
    return pl.pallas_call(
        body, name="mla_prep_bwd", grid=(s // tm,),
        in_specs=[ANY, row(HEADS * HD, 0), row(HEADS * HD, 0), row(WIDTH, 0),
                  row(Q_RANK, C_Q // Q_RANK), row(KV_RANK, C_KV // KV_RANK), row(1, 0), row(1, 0),
                  full(qn_w), full(kvn_w), full(wq), full(wukv), row(HD, 0), row(HD, 0), row(HD, 0),
                  row(DH, 0), row(DH, 0), row(DH, 0)],
        out_specs=[row(tail, C_Q // tail), row(HEADS * HD, 0), row(HEADS * HD, 0),
                   pl.BlockSpec((1, Q_RANK), lambda i: (0, 0)), pl.BlockSpec((1, KV_RANK), lambda i: (0, 0))],
        out_shape=[jax.ShapeDtypeStruct((s, PROJ_W), BF16), wide, wide,
                   jax.ShapeDtypeStruct((1, Q_RANK), F32), jax.ShapeDtypeStruct((1, KV_RANK), F32)],
        input_output_aliases={0: 0},
        compiler_params=_cparams(("arbitrary",)),
    )(dproj, dqq, dkk, dvv, proj, proj, rq, rkv, qn_w, kvn_w, wq, wukv, *qtab, *ktab)


def _rms_in_bwd(dxn, h, r, dh2, norm_w, s):
    def body(dxn_ref, h_ref, r_ref, dh2_ref, w_ref, gx_ref, gm_ref, dw_ref):
        i = pl.program_id(0)
        r_v = r_ref[...]
        dx, prod = _rms_bwd(dxn_ref[...], h_ref[...] * r_v, r_v, w_ref[...])
        dh = dh2_ref[...] + dx

        @pl.when(i == 0)
        def _():
            gm_ref[...] = dh
            dw_ref[...] = jnp.sum(prod, axis=0, keepdims=True)

        @pl.when(i > 0)
        def _():
            gx_ref[...] = dh
            dw_ref[...] += jnp.sum(prod, axis=0, keepdims=True)

    blk = pl.BlockSpec((BLK, D), lambda i: (i, 0))
    return pl.pallas_call(
        body, name="rms_in_bwd", grid=(s // BLK,),
        in_specs=[blk, blk, pl.BlockSpec((BLK, 1), lambda i: (i, 0)), blk, pl.BlockSpec((1, D), lambda i: (0, 0))],
        out_specs=[pl.BlockSpec((BLK, D), lambda i: (jnp.maximum(i - 1, 0), 0)), pl.BlockSpec((BLK, D), lambda i: (0, 0)),
                   pl.BlockSpec((1, D), lambda i: (0, 0))],
        out_shape=[jax.ShapeDtypeStruct((s - BLK, D), F32), jax.ShapeDtypeStruct((BLK, D), F32),
                   jax.ShapeDtypeStruct((1, D), F32)],
        compiler_params=_cparams(("arbitrary",)),
    )(dxn, h, r, dh2, norm_w)


def _adam_math(w, g, m, v):
    m = ADAM_B1 * m + (1.0 - ADAM_B1) * g
    v = ADAM_B2 * v + (1.0 - ADAM_B2) * (g * g)
    m_hat = m / (1.0 - ADAM_B1 ** ADAM_STEP)
    v_hat = v / (1.0 - ADAM_B2 ** ADAM_STEP)
    return -ADAM_LR * (m_hat / (jnp.sqrt(v_hat) + ADAM_EPS) + ADAM_WD * w), m, v


def _adamw(w, g, m, v, name):
    rows, cols = w.shape
    tr = rows
    for cand in (128, 64, 32, 16, 8):
        if rows % cand == 0:
            tr = cand
            break

    def body(w_ref, g_ref, m_ref, v_ref, d_ref, nm_ref, nv_ref):
        d_ref[...], nm_ref[...], nv_ref[...] = _adam_math(w_ref[...], g_ref[...], m_ref[...], v_ref[...])

    spec = pl.BlockSpec((tr, cols), lambda i: (i, 0))
    return pl.pallas_call(
        body, name=name, grid=(rows // tr,), in_specs=[spec] * 4, out_specs=[spec] * 3,
        out_shape=[jax.ShapeDtypeStruct((rows, cols), F32)] * 3,
        compiler_params=_cparams(("parallel",)),
    )(w, g, m, v)


def _adamw_t(w_t, g_t, m_t, v_t, name):
    cols, rows = w_t.shape

    def body(w_ref, g_ref, m_ref, v_ref, go_ref, d_ref, nm_ref, nv_ref):
        g = g_ref[...]
        go_ref[...] = g
        d_ref[...], nm_ref[...], nv_ref[...] = _adam_math(w_ref[...], g, m_ref[...], v_ref[...])

    blk = pl.BlockSpec((128, rows), lambda k: (k, 0))
    return pl.pallas_call(
        body, name=name, grid=(pl.cdiv(cols, 128),), in_specs=[blk] * 4, out_specs=[blk] * 4,
        out_shape=[jax.ShapeDtypeStruct((cols, rows), F32)] * 4, compiler_params=_cparams(("parallel",)),
    )(w_t, g_t, m_t, v_t)


def _adamw_small(w, gall, m, v):
    def body(w_ref, g_ref, m_ref, v_ref, gs_ref, d_ref, nm_ref, nv_ref):
        g = g_ref[0]
        for dev in range(1, 8):
            g = g + g_ref[dev]
        gs_ref[...] = g
        d_ref[...], nm_ref[...], nv_ref[...] = _adam_math(w_ref[...], g, m_ref[...], v_ref[...])

    return pl.pallas_call(
        body, name="adamw_small", out_shape=[jax.ShapeDtypeStruct((SM_ROWS, 128), F32)] * 4,
        compiler_params=pltpu.CompilerParams(vmem_limit_bytes=VMEM_LIMIT),
    )(w, gall, m, v)


ADD_ROWS = 464
HALF_BLOCKS = PK_HALF // ADD_ROWS


def _rs_core_add(srcs, land, pos):
    def body(pos_ref, *refs):
        src_refs, (land_ref, out_ref, own, sems) = refs[:len(srcs)], refs[len(srcs):]
        step = pl.program_id(0)

        def fetch(s):
            for hf in range(2):
                @pl.when(pos_ref[1] == hf)
                def _():
                    for k, r0, p, n in _grad_half(s, hf):
                        pltpu.make_async_copy(src_refs[k].at[pl.ds(r0, n), :], own.at[s % 2, pl.ds(p, n), :],
                                              sems.at[s % 2]).start()

        @pl.when(step == 0)
        def _():
            fetch(0)

        for s in range(N_CHIPS):
            @pl.when(step == s)
            def _():
                if s + 1 < N_CHIPS:
                    fetch(s + 1)
                slot = own.at[s % 2]
                pltpu.make_async_copy(slot, slot, sems.at[s % 2]).wait()
                out_ref[0] = (slot[...].astype(F32) + land_ref[0].astype(F32)).astype(BF16)

    blk = pl.BlockSpec((1, PK_HALF, D), lambda s, pos_ref: (s, 0, 0))
    return pl.pallas_call(
        body, name="rs_core_add",
        grid_spec=pltpu.PrefetchScalarGridSpec(
            num_scalar_prefetch=1, grid=(N_CHIPS,), in_specs=[ANY] * len(srcs) + [blk], out_specs=blk,
            scratch_shapes=[pltpu.VMEM((2, PK_HALF, D), BF16), pltpu.SemaphoreType.DMA((2,))]),
        out_shape=jax.ShapeDtypeStruct((N_CHIPS, PK_HALF, D), BF16),
        compiler_params=_cparams(("arbitrary",)),
    )(pos, *srcs, land)


def _rs_chip_add(cp, land, pos):
    def body(pos_ref, a_ref, l_ref, o_ref):
        o_ref[0] = ((a_ref[0].astype(F32) + l_ref[0].astype(F32)) + l_ref[1].astype(F32)) + l_ref[2].astype(F32)

    return pl.pallas_call(
        body, name="rs_chip_add",
        grid_spec=pltpu.PrefetchScalarGridSpec(
            num_scalar_prefetch=1, grid=(HALF_BLOCKS,),
            in_specs=[pl.BlockSpec((1, ADD_ROWS, D), lambda i, pos_ref: (pos_ref[0], i, 0)),
                      pl.BlockSpec((3, ADD_ROWS, D), lambda i, pos_ref: (0, i, 0))],
            out_specs=pl.BlockSpec((1, ADD_ROWS, D), lambda i, pos_ref: (pos_ref[1], i, 0))),
        out_shape=jax.ShapeDtypeStruct((2, PK_HALF, D), F32),
        compiler_params=_cparams(("parallel",)),
    )(pos, cp, land)


ANY = pl.BlockSpec(memory_space=pl.ANY)


def _mesh_pos():
    return lax.axis_index("x"), lax.axis_index("y"), lax.axis_index("c")


def _chip_peer(x, y, c, r):
    return (jnp.bitwise_xor(x, r >> 1), jnp.bitwise_xor(y, r & 1), c)


def _allgather_w_in(big0, wt, meta_loc, x2, norm_w):
    s_rows = x2.shape[0] + BLK
    tok = next(t for t in (512, 256, 128) if x2.shape[0] % t == 0)

    def body(big0_ref, wt_ref, meta_ref, x_ref, nw_ref, big_ref, metaf_ref, h_ref, xn_ref, r_ref,
             lsem, ssem, rsem, fssem, frsem, msem_s, msem_r, osem, xbuf, xnbuf, rbuf, mbuf):
        del big0_ref
        x, y, c = _mesh_pos()
        j = 2 * x + y
        me, sibling = (x, y, c), (x, y, 1 - c)

        def half_wait(s_sem, r_sem, halves=1):
            rows = big_ref.at[pl.ds(0, halves * AG_IN_HALF), :]
            return pltpu.make_async_remote_copy(src_ref=rows, dst_ref=rows, send_sem=s_sem, recv_sem=r_sem,
                                                device_id=me, device_id_type=MESH)

        own_meta = pltpu.make_async_copy(meta_ref, metaf_ref.at[j], lsem)
        own_meta.start()
        for s in range(N_CHIPS):
            for hf in range(2):
                @pl.when((j == s) & (c == hf))
                def _():
                    for r in (1, 2, 3):
                        for p, b, n in _ag_half(s, True, hf):
                            pltpu.make_async_remote_copy(
                                src_ref=wt_ref.at[pl.ds(p, n), :], dst_ref=big_ref.at[pl.ds(b, n), :],
                                send_sem=ssem.at[r - 1], recv_sem=rsem.at[r - 1],
                                device_id=_chip_peer(x, y, c, r), device_id_type=MESH).start()

            @pl.when(j == s)
            def _():
                for hf in range(2):
                    for p, b, n in _ag_half(s, True, hf):
                        pltpu.make_async_remote_copy(
                            src_ref=wt_ref.at[pl.ds(p, n), :], dst_ref=big_ref.at[pl.ds(b, n), :],
                            send_sem=osem.at[0], recv_sem=osem.at[1], device_id=sibling, device_id_type=MESH).start()

        meta_copies = [pltpu.make_async_remote_copy(
            src_ref=meta_ref, dst_ref=metaf_ref.at[j], send_sem=msem_s.at[r - 1], recv_sem=msem_r.at[r - 1],
            device_id=_chip_peer(x, y, c, r), device_id_type=MESH) for r in (1, 2, 3)]
        for cp in meta_copies:
            cp.start()

        nw = nw_ref[...]

        def norm_rows(first, n):
            hv = xbuf[pl.ds(0, n), :]
            r = lax.rsqrt(jnp.mean(hv * hv, axis=-1, keepdims=True) + NORM_EPS)
            xnbuf[pl.ds(0, n), :] = (hv * r * nw).astype(BF16)
            rbuf[pl.ds(0, n), :] = r
            rows = pl.ds(first if isinstance(first, int) else pl.multiple_of(first, BLK), n)
            pltpu.sync_copy(xbuf.at[pl.ds(0, n), :], h_ref.at[rows, :])
            pltpu.sync_copy(xnbuf.at[pl.ds(0, n), :], xn_ref.at[rows, :])
            pltpu.sync_copy(rbuf.at[pl.ds(0, n), :], r_ref.at[rows, :])

        def token_rows(i, carry):
            pltpu.sync_copy(x_ref.at[pl.ds(pl.multiple_of(i * tok, tok), tok), :], xbuf)
            norm_rows(BLK + i * tok, tok)
            return carry

        lax.fori_loop(0, (s_rows - BLK) // tok, token_rows, 0)
        for cp in meta_copies:
            cp.wait_recv()
        own_meta.wait()
        xbuf[pl.ds(0, BLK), :] = jnp.zeros((BLK, D), F32)
        for sh in range(N_CHIPS):
            pltpu.sync_copy(metaf_ref.at[sh], mbuf)
            xbuf[PAD_FRONT:BLK, sh * (D // N_CHIPS):(sh + 1) * (D // N_CHIPS)] = mbuf[...]
        norm_rows(0, BLK)

        for r in (1, 2, 3):
            half_wait(ssem.at[r - 1], rsem.at[r - 1]).wait_recv()
            src_shard = jnp.bitwise_xor(j, r)
            for s in range(N_CHIPS):
                for hf in range(2):
                    @pl.when((src_shard == s) & (c == hf))
                    def _():
                        for _, b, n in _ag_half(s, True, hf):
                            rows = big_ref.at[pl.ds(b, n), :]
                            pltpu.make_async_remote_copy(
                                src_ref=rows, dst_ref=rows, send_sem=fssem.at[r - 1], recv_sem=frsem.at[r - 1],
                                device_id=sibling, device_id_type=MESH).start()
        for r in (1, 2, 3):
            half_wait(fssem.at[r - 1], frsem.at[r - 1]).wait_recv()
        for r in (1, 2, 3):
            half_wait(ssem.at[r - 1], rsem.at[r - 1]).wait_send()
            half_wait(fssem.at[r - 1], frsem.at[r - 1]).wait_send()
        own_rows = half_wait(osem.at[0], osem.at[1], halves=2)
        own_rows.wait_recv()
        own_rows.wait_send()
        for cp in meta_copies:
            cp.wait_send()

    dma3 = pltpu.SemaphoreType.DMA((3,))
    big, _, h, xn, r = pl.pallas_call(
        body, name="allgather_w_in", in_specs=[ANY, ANY, ANY, ANY, pl.BlockSpec(memory_space=pltpu.VMEM)],
        out_specs=[ANY] * 5,
        out_shape=[jax.ShapeDtypeStruct((BIG_ROWS, D), BF16), jax.ShapeDtypeStruct((N_CHIPS,) + meta_loc.shape, F32),
                   jax.ShapeDtypeStruct((s_rows, D), F32), jax.ShapeDtypeStruct((s_rows, D), BF16),
                   jax.ShapeDtypeStruct((s_rows, 1), F32)],
        scratch_shapes=[pltpu.SemaphoreType.DMA(()), dma3, dma3, dma3, dma3, dma3, dma3, pltpu.SemaphoreType.DMA((2,)),
                        pltpu.VMEM((tok, D), F32), pltpu.VMEM((tok, D), BF16), pltpu.VMEM((tok, 1), F32),
                        pltpu.VMEM(meta_loc.shape, F32)],
        input_output_aliases={0: 0},
        compiler_params=pltpu.CompilerParams(has_side_effects=True),
    )(big0, wt, meta_loc, x2, norm_w)
    return big, h, xn, r


HBM = pl.BlockSpec(memory_space=pltpu.HBM)
SEM = pl.BlockSpec(memory_space=pltpu.SEMAPHORE)
EFFECT = pltpu.SideEffectType.DATAFLOW_SIDE_EFFECTING
REST_PEERS = [(r, dc) for r in (1, 2, 3) for dc in (0, 1)]


def _allgather_rest_start(big):
    def body(big_ref, *rest):
        s_sems, r_sems = rest[:6], rest[6:12]
        x, y, c = _mesh_pos()
        j = 2 * x + y
        for s in range(N_CHIPS):
            for hf in range(2):
                @pl.when((j == s) & (c == hf))
                def _():
                    for k, (r, dc) in enumerate(REST_PEERS):
                        for _, b, n in _ag_half(s, False, hf):
                            rows = big_ref.at[pl.ds(b, n), :]
                            pltpu.make_async_remote_copy(
                                src_ref=rows, dst_ref=rows, send_sem=s_sems[k], recv_sem=r_sems[2 * (r - 1) + hf],
                                device_id=_chip_peer(x, y, dc, r), device_id_type=MESH).start()

    sem = pltpu.SemaphoreType.DMA(())
    out = pl.pallas_call(
        body, name="allgather_rest_start", out_shape=(sem,) * 12 + (pltpu.HBM(big.shape, big.dtype),),
        in_specs=(HBM,), out_specs=(SEM,) * 12 + (HBM,), input_output_aliases={0: 12},
        compiler_params=pltpu.CompilerParams(has_side_effects=EFFECT),
    )(pltpu.with_memory_space_constraint(big, pltpu.HBM))
    return out[:12], out[12]


def _allgather_rest_wait(sems, big, after):
    def body(big_ref, *rest):
        s_sems, r_sems = rest[:6], rest[6:12]
        me = _mesh_pos()
        rows = big_ref.at[pl.ds(0, AG_REST_HALF), :]
        for k in range(6):
            copy = pltpu.make_async_remote_copy(src_ref=rows, dst_ref=rows, send_sem=s_sems[k], recv_sem=r_sems[k],
                                                device_id=me, device_id_type=MESH)
            copy.wait_send()
            copy.wait_recv()

    return pl.pallas_call(
        body, name="allgather_rest_wait", out_shape=pltpu.HBM(big.shape, big.dtype),
        in_specs=(HBM,) + (SEM,) * 12 + (pl.BlockSpec(memory_space=pl.ANY),), out_specs=HBM,
        input_output_aliases={0: 0}, compiler_params=pltpu.CompilerParams(has_side_effects=EFFECT),
    )(big, *sems, after)


RS_SPLIT = 2


def _rs_core_exchange(srcs):
    def body(*refs):
        src_refs, (land_ref, ssem, rsem) = refs[:len(srcs)], refs[len(srcs):]
        x, y, c = _mesh_pos()
        me, sibling = (x, y, c), (x, y, 1 - c)
        for hf in range(2):
            @pl.when(c == hf)
            def _():
                for s in range(N_CHIPS):
                    for k, r0, p, n in _grad_half(s, 1 - hf):
                        pltpu.make_async_remote_copy(
                            src_ref=src_refs[k].at[pl.ds(r0, n), :], dst_ref=land_ref.at[s, pl.ds(p, n), :],
                            send_sem=ssem, recv_sem=rsem, device_id=sibling, device_id_type=MESH).start()
        whole = pltpu.make_async_remote_copy(src_ref=land_ref, dst_ref=land_ref, send_sem=ssem, recv_sem=rsem,
                                             device_id=me, device_id_type=MESH)
        whole.wait_recv()
        whole.wait_send()

    return pl.pallas_call(
        body, name="rs_core_exchange", in_specs=[ANY] * len(srcs), out_specs=ANY,
        out_shape=jax.ShapeDtypeStruct((N_CHIPS, PK_HALF, D), BF16),
        scratch_shapes=[pltpu.SemaphoreType.DMA(())] * 2,
        compiler_params=pltpu.CompilerParams(has_side_effects=True),
    )(*srcs)


def _rs_chip_start(cp):
    def body(cp_ref, land_ref, s1, s2, s3, r1, r2, r3, cp_thru, land_thru, token):
        x, y, c = _mesh_pos()
        j = 2 * x + y
        for r, s_sem, r_sem in zip((1, 2, 3), (s1, s2, s3), (r1, r2, r3)):
            pltpu.make_async_remote_copy(
                src_ref=cp_ref.at[jnp.bitwise_xor(j, r)], dst_ref=land_ref.at[r - 1], send_sem=s_sem, recv_sem=r_sem,
                device_id=_chip_peer(x, y, c, r), device_id_type=MESH).start()
        token[...] = jnp.zeros_like(token)

    land_shape = (3, PK_HALF, D)
    sem = pltpu.SemaphoreType.DMA(())
    out = pl.pallas_call(
        body, name="rs_chip_start",
        out_shape=(sem,) * 6 + (pltpu.HBM(cp.shape, cp.dtype), pltpu.HBM(land_shape, BF16),
                                jax.ShapeDtypeStruct((8, 128), F32)),
        in_specs=(HBM, HBM), out_specs=(SEM,) * 6 + (HBM, HBM, pl.BlockSpec(memory_space=pltpu.VMEM)),
        input_output_aliases={0: 6, 1: 7},
        compiler_params=pltpu.CompilerParams(has_side_effects=EFFECT),
    )(pltpu.with_memory_space_constraint(cp, pltpu.HBM),
      pltpu.with_memory_space_constraint(lax.empty(land_shape, BF16), pltpu.HBM))
    return out[:6], out[6], out[7], out[8]


def _rs_chip_wait(sems, cp_thru, land_thru, after):
    def body(cp_ref, land_ref, s1, s2, s3, r1, r2, r3, after_ref, cp_out, land_out):
        me = _mesh_pos()
        for r, s_sem, r_sem in zip((1, 2, 3), (s1, s2, s3), (r1, r2, r3)):
            copy = pltpu.make_async_remote_copy(src_ref=cp_ref.at[0], dst_ref=land_ref.at[r - 1], send_sem=s_sem,
                                                recv_sem=r_sem, device_id=me, device_id_type=MESH)
            copy.wait_send()
            copy.wait_recv()

    return pl.pallas_call(
        body, name="rs_chip_wait",
        out_shape=(pltpu.HBM(cp_thru.shape, cp_thru.dtype), pltpu.HBM(land_thru.shape, land_thru.dtype)),
        in_specs=(HBM, HBM) + (SEM,) * 6 + (pl.BlockSpec(memory_space=pl.ANY),), out_specs=(HBM, HBM),
        input_output_aliases={0: 0, 1: 1},
        compiler_params=pltpu.CompilerParams(has_side_effects=EFFECT),
    )(cp_thru, land_thru, *sems, after)


def _rs_finish(full, small):
    n = PK_HALF // RS_SPLIT

    def body(full_in_ref, sm_ref, full_ref, all_ref, lsem, ssem, rsem, sm_s, sm_r):
        del full_in_ref
        x, y, c = _mesh_pos()
        me, sibling = (x, y, c), (x, y, 1 - c)
        my_id = 4 * x + 2 * y + c
        own_sm = pltpu.make_async_copy(sm_ref, all_ref.at[my_id], lsem)
        own_sm.start()
        for q in range(RS_SPLIT):
            rows = full_ref.at[c, pl.ds(q * n, n), :]
            pltpu.make_async_remote_copy(src_ref=rows, dst_ref=rows, send_sem=ssem, recv_sem=rsem,
                                         device_id=sibling, device_id_type=MESH).start()
        half = pltpu.make_async_remote_copy(src_ref=full_ref.at[c], dst_ref=full_ref.at[c], send_sem=ssem,
                                            recv_sem=rsem, device_id=me, device_id_type=MESH)
        smalls = [pltpu.make_async_remote_copy(
            src_ref=sm_ref, dst_ref=all_ref.at[my_id], send_sem=sm_s.at[r - 1], recv_sem=sm_r.at[r - 1],
            device_id=(jnp.bitwise_xor(x, r >> 2), jnp.bitwise_xor(y, (r >> 1) & 1), jnp.bitwise_xor(c, r & 1)),
            device_id_type=MESH) for r in range(1, 8)]
        for cpy in smalls:
            cpy.start()
        half.wait_recv()
        half.wait_send()
        for cpy in smalls:
            cpy.wait_recv()
        for cpy in smalls:
            cpy.wait_send()
        own_sm.wait()

    dma7 = pltpu.SemaphoreType.DMA((7,))
    return pl.pallas_call(
        body, name="rs_finish", in_specs=[ANY, ANY], out_specs=[ANY, ANY],
        out_shape=[jax.ShapeDtypeStruct((2, PK_HALF, D), F32), jax.ShapeDtypeStruct((8, SM_ROWS, 128), F32)],
        scratch_shapes=[pltpu.SemaphoreType.DMA(()), pltpu.SemaphoreType.DMA(()), pltpu.SemaphoreType.DMA(()), dma7, dma7],
        input_output_aliases={0: 0},
        compiler_params=pltpu.CompilerParams(has_side_effects=True),
    )(full, small)


def _rope_tables(s):
    pos = jnp.arange(s, dtype=F32) - PAD_FRONT

    def cs(d):
        inv = ROPE_BASE ** (-jnp.arange(0, d, 2, dtype=F32) / d)
        ang = pos[:, None] * inv[None, :]
        return jnp.cos(ang), jnp.sin(ang)

    c, sn = cs(ROPE_D)
    z = jnp.zeros_like(c)
    ktab = (jnp.concatenate([c, c, z, z], axis=1), jnp.concatenate([-sn, z, z, z], axis=1),
            jnp.concatenate([z, sn, z, z], axis=1))
    one = jnp.ones_like(c)
    qtab = tuple(ATT_SCALE * jnp.concatenate(parts, axis=1) for parts in (
        [one, one, one, one, c, c, z, z], [z, z, z, z, -sn, z, z, z], [z, z, z, z, z, sn, z, z]))
    lane = jnp.arange(HD) == DH + ROPE_D
    qbias = lane.astype(F32)[None, :]
    kbias = jnp.where((pos < 0)[:, None] & lane[None, DH:], NEG_INF, 0.0).astype(F32)
    c, sn = cs(DH)
    return qtab, ktab, qbias, kbias, jnp.concatenate([c, c], axis=1), jnp.concatenate([-sn, sn], axis=1)


def _grad_mm(a, b, m, n, s, name):
    return _matmul(a, b, "tn", m, n, s, min(m, 512), min(n, 1024), s, BF16, name, n_outer=True)


def _local_step(x2, target2, hp0, big, small_w, normed=None, gather_wait=None, reduce_start=None, reduce_wait=None):
    norm_w, qn_w, kvn_w, gn_w, gn_b, fin_w = small_w
    s = x2.shape[0] + BLK
    tm = _row_tile(s, 1408)
    qtab, ktab, qbias, kbias, c128, s128 = _rope_tables(s)
    consts, gamma = _ret_consts()

    h, xn, r1 = _rms_in(x2, hp0, norm_w, s) if normed is None else normed
    proj = _matmul(xn, big, "nt", s, PROJ_W, D, tm, 512, D, BF16, "proj", b_off=(BIG_IN // 512, 0))
    if gather_wait is not None:
        big = gather_wait(big, proj)
    uq = big[BIG_UQ:BIG_UQ + 384].reshape(HEADS, DH + ROPE_D, Q_RANK)
    wq = jnp.pad(uq, ((0, 0), (0, HD - DH - ROPE_D), (0, 0))).reshape(HEADS * HD, Q_RANK)
    wukv = big[BIG_UKV:BIG_UKV + 256].reshape(2 * WIDTH, KV_RANK)
    wbm = big[BIG_BM:BIG_BM + 1024].reshape(D, WIDTH)
    wbr = big[BIG_BR:BIG_BR + 1024].reshape(D, WIDTH)
    cqn, rq, ckvn, rkv, qq, kk, vv = _mla_prep(proj, qn_w, kvn_w, wq, wukv, qtab, ktab, qbias, kbias, s)
    o_mla, y_mla, lse = _attn_fwd(qq, kk, vv, proj, s)
    y_ret, on, rstd, qr, kr, rall = _ret_fwd(proj, gn_w, gn_b, c128, s128, consts, gamma, s)
    u_mla, u_ret, merged = _merge_fwd(y_mla, y_ret, wbm, wbr, proj, s)
    h2 = _out_fwd(merged, big, h, s)
    dh2, dh2b, loss_blk, dfin = _loss_bwd(h2, target2, fin_w, s)

    dproj, du_mla, du_ret = _merge_bwd(dh2b, big, u_mla, u_ret, proj, s)
    g_out = _grad_mm(merged, dh2b, D, D, s, "grad_w_out")
    dproj, do_mla, delta = _branch_mla_bwd(dproj, du_mla, wbm, o_mla, proj, s)
    g_bm = _grad_mm(du_mla, y_mla, D, WIDTH, s, "grad_w_branch_mla")
    dproj, do_ret, dgw, dgb = _branch_ret_bwd(dproj, du_ret, wbr, on, rstd, gn_w, gn_b, proj, s)
    g_br = _grad_mm(du_ret, y_ret, D, WIDTH, s, "grad_w_branch_ret")
    dproj = _ret_bwd(dproj, qr, kr, proj, rall, do_ret, c128, s128, consts, gamma, s)
    dqq, dkk, dvv = _attn_bwd(qq, kk, vv, do_mla, lse, delta.reshape(lse.shape), s)
    dproj, dq, dkv, dqnw, dkvnw = _mla_prep_bwd(dproj, dqq, dkk, dvv, proj, rq, rkv, qn_w, kvn_w, wq, wukv,
                                                qtab, ktab, s)
    g_q = _grad_mm(dq, cqn, HEADS * HD, Q_RANK, s, "grad_w_uq")
    g_ukv = _grad_mm(dkv, ckvn, 2 * WIDTH, KV_RANK, s, "grad_w_ukv")
    g_in = _grad_mm(dproj, xn, PROJ_W, D, s, "grad_w_in")

    g_uq = g_q.reshape(HEADS, HD, Q_RANK)[:, :DH + ROPE_D]
    grads = (g_in, g_out, g_bm.reshape(1024, D), g_br.reshape(1024, D), g_uq.reshape(384, D), g_ukv.reshape(256, D))
    token, travelling = (None, grads) if reduce_start is None else reduce_start(grads)
    dxn = _matmul(dproj, big, "nn", s, D, PROJ_W, _row_tile(s, 384), 512, PROJ_W, F32, "dxn",
                  b_off=(BIG_IN // PROJ_W, 0), n_outer=True, after=token)
    grad_x, gmeta_blk, dnorm = _rms_in_bwd(dxn, h, r1, dh2, norm_w, s)
    small = _small_rows((dnorm, dqnw, dkvnw, dgw, dgb, dfin), loss_blk, gmeta_blk[PAD_FRONT:])
    return grad_x, travelling if reduce_wait is None else reduce_wait(travelling, dnorm), small


def _small_rows(ws, first=None, last=None):
    def part(a, rows):
        a = a.reshape(-1, 128)
        return a if a.shape[0] == rows else jnp.pad(a, ((0, rows - a.shape[0]), (0, 0)))

    bounds = (SM_NORM, SM_QN, SM_KVN, SM_GNW, SM_GNB, SM_FIN, SM_META)
    first = jnp.zeros((SM_NORM, 128), F32) if first is None else first
    last = jnp.zeros((SM_ROWS - SM_META, 128), F32) if last is None else last
    return jnp.concatenate([part(first, SM_NORM)] + [part(w, bounds[k + 1] - bounds[k]) for k, w in enumerate(ws)]
                           + [part(last, SM_ROWS - SM_META)], axis=0)


def kernel(x, meta, norm_w, w_in, mla_q_norm_w, mla_w_uq, mla_kv_norm_w, mla_w_ukv, ret_gn_w, ret_gn_b, w_branch_mla, w_branch_ret, w_out, final_norm_w, loss_target, m_meta, m_norm_w, m_w_in, m_mla_q_norm_w, m_mla_w_uq, m_mla_kv_norm_w, m_mla_w_ukv, m_ret_gn_w, m_ret_gn_b, m_w_branch_mla, m_w_branch_ret, m_w_out, m_final_norm_w, v_meta, v_norm_w, v_w_in, v_mla_q_norm_w, v_mla_w_uq, v_mla_kv_norm_w, v_mla_w_ukv, v_ret_gn_w, v_ret_gn_b, v_w_branch_mla, v_w_branch_ret, v_w_out, v_final_norm_w):
    j = 2 * lax.axis_index("x") + lax.axis_index("y")
    tr = lambda w: w[0].T.reshape(-1, D).astype(BF16)
    pos = jnp.stack([j, lax.axis_index("c")]).astype(jnp.int32)
    put = lambda buf, rows, off: lax.dynamic_update_slice(buf, rows, (off, 0))
    big0 = lax.empty((BIG_ROWS, D), BF16)
    big0 = put(big0, w_out[0].astype(BF16), BIG_OUT + 512 * j)
    big0 = put(big0, tr(w_branch_mla), BIG_BM + 256 * j)
    big0 = put(big0, tr(w_branch_ret), BIG_BR + 256 * j)
    big0 = put(big0, tr(mla_w_uq), BIG_UQ + 96 * j)
    big0 = put(big0, tr(mla_w_ukv), BIG_UKV + 64 * j)
    big0 = put(big0, jnp.zeros((AG_JUNK_END - AG_JUNK_REST, D), BF16), AG_JUNK_REST)
    big0 = put(big0, jnp.zeros((ZERO_ROWS, D), BF16), BIG_IN + IN_WIDTH)
    wt = jnp.concatenate([tr(w_in), jnp.zeros((2 * AG_IN_HALF - IN_SHARD, D), BF16)], axis=0)
    big, h, xn, r1 = _allgather_w_in(big0, wt, meta, x[0], norm_w)
    gather_sems, big = _allgather_rest_start(big)

    def gather_wait(big_travelling, after):
        return _allgather_rest_wait(gather_sems, big_travelling, after)

    small_w = (norm_w, mla_q_norm_w, mla_kv_norm_w, ret_gn_w, ret_gn_b, final_norm_w.reshape(1, D))

    def reduce_start(grads):
        chip_part = _rs_core_add(grads, _rs_core_exchange(grads), pos)
        sems, part_thru, land_thru, token = _rs_chip_start(chip_part)
        return token, (sems, part_thru, land_thru)

    def reduce_wait(state, after):
        return _rs_chip_wait(*state, after)

    grad_x, (chip_part, land2), small = _local_step(x[0], loss_target[0], None, big, small_w, (h, xn, r1), gather_wait,
                                                    reduce_start, reduce_wait)
    full, small_all = _rs_finish(_rs_chip_add(chip_part, land2, pos), small)
    full = full.reshape(PK_ROWS, D)

    untr = lambda lo, hi, rows: full[lo:hi].reshape(rows, -1).T
    grads = {
        "w_out": full[PK_OUT:PK_BM], "w_branch_mla": untr(PK_BM, PK_BR, 512), "w_branch_ret": untr(PK_BR, PK_UQ, 512),
        "mla_w_uq": untr(PK_UQ, PK_UKV, 384), "mla_w_ukv": untr(PK_UKV, PK_PAD, 512),
    }
    big_w = {"mla_w_uq": (mla_w_uq, m_mla_w_uq, v_mla_w_uq),
             "mla_w_ukv": (mla_w_ukv, m_mla_w_ukv, v_mla_w_ukv),
             "w_branch_mla": (w_branch_mla, m_w_branch_mla, v_w_branch_mla),
             "w_branch_ret": (w_branch_ret, m_w_branch_ret, v_w_branch_ret), "w_out": (w_out, m_w_out, v_w_out)}
    res = {"w_in": tuple(a.T[None] for a in _adamw_t(w_in[0].T, full, m_w_in[0].T, v_w_in[0].T, "adamw_w_in"))}
    for name, (w, m, v) in big_w.items():
        d, nm, nv = _adamw(w[0], grads[name], m[0], v[0], "adamw_" + name)
        res[name] = (grads[name][None], d[None], nm[None], nv[None])

    small_m = (m_norm_w, m_mla_q_norm_w, m_mla_kv_norm_w, m_ret_gn_w, m_ret_gn_b, m_final_norm_w.reshape(1, D))
    small_v = (v_norm_w, v_mla_q_norm_w, v_mla_kv_norm_w, v_ret_gn_w, v_ret_gn_b, v_final_norm_w.reshape(1, D))
    gs, ds, ms, vs = _adamw_small(_small_rows(small_w), small_all, _small_rows(small_m), _small_rows(small_v))
    names = ["norm_w", "mla_q_norm_w", "mla_kv_norm_w", "ret_gn_w", "ret_gn_b", "final_norm_w"]
    bounds = [SM_NORM, SM_QN, SM_KVN, SM_GNW, SM_GNB, SM_FIN]
    for k, name in enumerate(names):
        shape = (D,) if name == "final_norm_w" else (1, -1)
        rows = small_w[k].size // 128
        res[name] = tuple(a[bounds[k]:bounds[k] + rows].reshape(shape) for a in (gs, ds, ms, vs))
    g_meta = lax.dynamic_slice_in_dim(gs[SM_META:SM_META + 256].reshape(N_META, D), j * (D // N_CHIPS), D // N_CHIPS, axis=1)
    res["meta"] = (g_meta,) + tuple(_adamw(meta, g_meta, m_meta, v_meta, "adamw_meta"))

    order = ["meta", "norm_w", "w_in", "mla_q_norm_w", "mla_w_uq", "mla_kv_norm_w", "mla_w_ukv", "ret_gn_w", "ret_gn_b",
             "w_branch_mla", "w_branch_ret", "w_out", "final_norm_w"]
    return (gs[0, 0], grad_x[None]) + tuple(res[n][k] for k in range(4) for n in order)
```

```python
import functools
import math

import numpy as np
import jax
import jax.numpy as jnp
from jax import lax
from jax.experimental import pallas as pl
from jax.experimental.pallas import tpu as pltpu

F32 = jnp.float32
BF16 = jnp.bfloat16
MESH = pl.DeviceIdType.MESH

D = 2048
N_META = 16
BLK = 128
PAD_FRONT = BLK - N_META
HEADS = 8
DH = 128
ROPE_D = 64
Q_RANK = 512
KV_RANK = 256
WIDTH = HEADS * DH
ROPE_BASE = 10000.0
NORM_EPS = 1e-6
GN_EPS = 1e-5
NEG_INF = -1e30
ATT_SCALE = (DH + ROPE_D) ** -0.5
RET_SCALE = DH ** -0.5
IN_WIDTH = 10048
N_CHIPS = 4
IN_SHARD = IN_WIDTH // N_CHIPS
ADAM_LR, ADAM_B1, ADAM_B2, ADAM_EPS, ADAM_WD, ADAM_STEP = 0.001, 0.9, 0.999, 1e-08, 0.01, 10

R_Q, R_K, R_V, Z_RET, Z_MLA, GATE0, GATE1 = 0, 1024, 2048, 3072, 4096, 5120, 5120 + D
C_Q = 5120 + 2 * D
C_KV = C_Q + Q_RANK
K_PE = C_KV + KV_RANK
PROJ_W = 10240
IN_RUNS = ((0, 832, C_Q), (832, 1856, Z_MLA), (1856, 4928, R_Q), (4928, 5952, Z_RET), (5952, IN_WIDTH, GATE0))

BIG_IN, BIG_OUT, BIG_BM, BIG_BR, BIG_UQ, BIG_UKV, BIG_JUNK, BIG_ROWS = 0, 10240, 12288, 13312, 14336, 14720, 14976, 15360
ZERO_ROWS = PROJ_W - IN_WIDTH
PK_IN, PK_OUT, PK_BM, PK_BR, PK_UQ, PK_UKV, PK_PAD, PK_ROWS = 0, 2512, 3024, 3280, 3536, 3632, 3696, 3712
PK_HALF = PK_ROWS // 2
SM_LOSS, SM_NORM, SM_QN, SM_KVN, SM_GNW, SM_GNB, SM_FIN, SM_META, SM_ROWS = 0, 8, 24, 32, 40, 48, 56, 72, 328

VMEM_LIMIT = 56 * 1024 * 1024


def _pieces(shard):
    j = shard
    out = [(PK_OUT, BIG_OUT + 512 * j, 512), (PK_BM, BIG_BM + 256 * j, 256), (PK_BR, BIG_BR + 256 * j, 256),
           (PK_UQ, BIG_UQ + 96 * j, 96), (PK_UKV, BIG_UKV + 64 * j, 64)]
    for lo, hi, new in IN_RUNS:
        a, b = max(lo, IN_SHARD * j), min(hi, IN_SHARD * (j + 1))
        if b > a:
            out.append((PK_IN + a - IN_SHARD * j, BIG_IN + new + a - lo, b - a))
    out += [(PK_PAD, BIG_JUNK + 16 * j, 16)]
    return out


AG_JUNK_REST, AG_JUNK_IN, AG_JUNK_END = BIG_JUNK, BIG_JUNK + 128, BIG_JUNK + 192
AG_REST_HALF, AG_IN_HALF = 608, 1264


def _ag_half(shard, w_in, half):
    pieces = [(b, n) for p, b, n in _pieces(shard) if p < PK_PAD and (p < PK_OUT) == w_in]
    pieces.append((AG_JUNK_IN + 16 * shard, 16) if w_in else (AG_JUNK_REST + 32 * shard, 32))
    size = AG_IN_HALF if w_in else AG_REST_HALF
    out, pos = [], 0
    for b, n in pieces:
        s, e = max(pos, half * size), min(pos + n, (half + 1) * size)
        if e > s:
            out.append((s, b + s - pos, e - s))
        pos += n
    assert pos == 2 * size
    return out


def _grad_half(shard, half):
    j = shard
    pieces = [(0, b - BIG_IN, p, n) for p, b, n in _pieces(j) if p < PK_OUT]
    pieces += [(1, 512 * j, PK_OUT, 512), (2, 256 * j, PK_BM, 256), (3, 256 * j, PK_BR, 256), (4, 96 * j, PK_UQ, 96),
               (5, 64 * j, PK_UKV, 64), (0, IN_WIDTH, PK_PAD, PK_ROWS - PK_PAD)]
    lo, hi = half * PK_HALF, (half + 1) * PK_HALF
    out = []
    for k, r0, p, n in pieces:
        s, e = max(p, lo), min(p + n, hi)
        if e > s:
            out.append((k, r0 + s - p, s - lo, e - s))
    assert sum(n for _, _, _, n in out) == PK_HALF
    return out


def _row_tile(rows, cap):
    best = BLK
    for t in range(BLK, cap + 1, BLK):
        if rows % t == 0:
            best = t
    return best


def _cparams(sem):
    return pltpu.CompilerParams(dimension_semantics=sem, vmem_limit_bytes=VMEM_LIMIT)


def _dot(a, b, form):
    dn = {"nt": (((1,), (1,)), ((), ())), "nn": (((1,), (0,)), ((), ())), "tn": (((0,), (0,)), ((), ()))}[form]
    return lax.dot_general(a, b, dn, preferred_element_type=F32)


def _sigmoid(v):
    return 1.0 / (1.0 + jnp.exp(-v))


def _matmul(a, b, form, m, n, k, tm, tn, tk, out_dtype, name, a_off=(0, 0), b_off=(0, 0), n_outer=False, after=None):
    nk = k // tk
    gi, gj = m // tm, n // tn

    def ij(g0, g1):
        return (g1, g0) if n_outer else (g0, g1)

    if form == "nt":
        a_spec = pl.BlockSpec((tm, tk), lambda g0, g1, kk: (ij(g0, g1)[0] + a_off[0], kk + a_off[1]))
        b_spec = pl.BlockSpec((tn, tk), lambda g0, g1, kk: (ij(g0, g1)[1] + b_off[0], kk + b_off[1]))
    elif form == "nn":
        a_spec = pl.BlockSpec((tm, tk), lambda g0, g1, kk: (ij(g0, g1)[0] + a_off[0], kk + a_off[1]))
        b_spec = pl.BlockSpec((tk, tn), lambda g0, g1, kk: (kk + b_off[0], ij(g0, g1)[1] + b_off[1]))
    else:
        a_spec = pl.BlockSpec((tk, tm), lambda g0, g1, kk: (kk + a_off[0], ij(g0, g1)[0] + a_off[1]))
        b_spec = pl.BlockSpec((tk, tn), lambda g0, g1, kk: (kk + b_off[0], ij(g0, g1)[1] + b_off[1]))
    o_spec = pl.BlockSpec((tm, tn), lambda g0, g1, kk: ij(g0, g1))

    def body(a_ref, b_ref, *rest):
        o_ref, *acc = rest[0 if after is None else 1:]
        p = _dot(a_ref[...], b_ref[...], form)
        if nk == 1:
            o_ref[...] = p.astype(o_ref.dtype)
        else:
            acc_ref, = acc
            kk = pl.program_id(2)

            @pl.when(kk == 0)
            def _():
                acc_ref[...] = p

            @pl.when(kk > 0)
            def _():
                acc_ref[...] += p

            @pl.when(kk == nk - 1)
            def _():
                o_ref[...] = acc_ref[...].astype(o_ref.dtype)

    extra = [] if after is None else [after]
    return pl.pallas_call(
        body, name=name, grid=(gj, gi, nk) if n_outer else (gi, gj, nk),
        in_specs=[a_spec, b_spec] + [pl.BlockSpec(memory_space=pl.ANY)] * len(extra), out_specs=o_spec,
        out_shape=jax.ShapeDtypeStruct((m, n), out_dtype),
        scratch_shapes=[] if nk == 1 else [pltpu.VMEM((tm, tn), F32)],
        compiler_params=_cparams(("parallel", "parallel", "arbitrary")),
    )(a, b, *extra)


def _rms_in(x, hp0, norm_w, s):
    def body(x_ref, hp0_ref, w_ref, h_ref, xn_ref, r_ref):
        def run(hv):
            r = lax.rsqrt(jnp.mean(hv * hv, axis=-1, keepdims=True) + NORM_EPS)
            h_ref[...] = hv
            xn_ref[...] = (hv * r * w_ref[...]).astype(BF16)
            r_ref[...] = r

        @pl.when(pl.program_id(0) == 0)
        def _():
            run(hp0_ref[...])

        @pl.when(pl.program_id(0) > 0)
        def _():
            run(x_ref[...])

    return pl.pallas_call(
        body, name="rms_in", grid=(s // BLK,),
        in_specs=[pl.BlockSpec((BLK, D), lambda i: (jnp.maximum(i - 1, 0), 0)),
                  pl.BlockSpec((BLK, D), lambda i: (0, 0)), pl.BlockSpec((1, D), lambda i: (0, 0))],
        out_specs=[pl.BlockSpec((BLK, D), lambda i: (i, 0)), pl.BlockSpec((BLK, D), lambda i: (i, 0)),
                   pl.BlockSpec((BLK, 1), lambda i: (i, 0))],
        out_shape=[jax.ShapeDtypeStruct((s, D), F32), jax.ShapeDtypeStruct((s, D), BF16),
                   jax.ShapeDtypeStruct((s, 1), F32)],
        compiler_params=_cparams(("arbitrary",)),
    )(x, hp0, norm_w)


def _rope64(t, c, sa, sb):
    return t * c + pltpu.roll(t, t.shape[1] - ROPE_D // 2, 1) * sa + pltpu.roll(t, ROPE_D // 2, 1) * sb


def _rope128(t, c, sg):
    return t * c + pltpu.roll(t, DH // 2, 1) * sg


HD = 2 * DH


def _mla_prep(proj, qn_w, kvn_w, wq, wukv, qtab, ktab, qbias, kbias, s):
    tm = _row_tile(s, 384)

    def body(cq_ref, ckv_ref, kpe_ref, qnw_ref, kvnw_ref, wq_ref, wukv_ref, qc_ref, qa_ref, qb_ref,
             kc_ref, ka_ref, kb_ref, qbias_ref, kbias_ref, cqn_ref, rq_ref, ckvn_ref, rkv_ref, qq_ref, kk_ref, vv_ref):
        cq = cq_ref[...].astype(F32)
        rq = lax.rsqrt(jnp.mean(cq * cq, axis=-1, keepdims=True) + NORM_EPS)
        cqn = (cq * rq * qnw_ref[...]).astype(BF16)
        ckv = ckv_ref[...].astype(F32)
        rkv = lax.rsqrt(jnp.mean(ckv * ckv, axis=-1, keepdims=True) + NORM_EPS)
        ckvn = (ckv * rkv * kvnw_ref[...]).astype(BF16)
        cqn_ref[...] = cqn
        rq_ref[...] = rq
        ckvn_ref[...] = ckvn
        rkv_ref[...] = rkv
        q = _dot(cqn, wq_ref[...], "nt")
        kv = _dot(ckvn, wukv_ref[...], "nt")
        kp = (_rope64(kpe_ref[...].astype(F32), kc_ref[...], ka_ref[...], kb_ref[...]) + kbias_ref[...]).astype(BF16)
        qc, qa, qb, qbias = qc_ref[...], qa_ref[...], qb_ref[...], qbias_ref[...]
        ones = jnp.ones((tm, DH), BF16)
        for h in range(HEADS):
            lo, mid, hi = h * HD, h * HD + DH, (h + 1) * HD
            qq_ref[:, lo:hi] = (_rope64(q[:, lo:hi], qc, qa, qb) + qbias).astype(BF16)
            kk_ref[:, lo:mid] = kv[:, lo:mid].astype(BF16)
            kk_ref[:, mid:hi] = kp
            vv_ref[:, lo:mid] = kv[:, mid:hi].astype(BF16)
            vv_ref[:, mid:hi] = ones

    row = lambda w, cb: pl.BlockSpec((tm, w), lambda i: (i, cb))
    full = lambda a: pl.BlockSpec(a.shape, lambda i: (0, 0))
    wide = jax.ShapeDtypeStruct((s, HEADS * HD), BF16)
    return pl.pallas_call(
        body, name="mla_prep", grid=(s // tm,),
        in_specs=[row(Q_RANK, C_Q // Q_RANK), row(KV_RANK, C_KV // KV_RANK), row(DH, K_PE // DH),
                  full(qn_w), full(kvn_w), full(wq), full(wukv), row(HD, 0), row(HD, 0), row(HD, 0),
                  row(DH, 0), row(DH, 0), row(DH, 0), full(qbias), row(DH, 0)],
        out_specs=[row(Q_RANK, 0), row(1, 0), row(KV_RANK, 0), row(1, 0), row(HEADS * HD, 0), row(HEADS * HD, 0),
                   row(HEADS * HD, 0)],
        out_shape=[jax.ShapeDtypeStruct((s, Q_RANK), BF16), jax.ShapeDtypeStruct((s, 1), F32),
                   jax.ShapeDtypeStruct((s, KV_RANK), BF16), jax.ShapeDtypeStruct((s, 1), F32), wide, wide, wide],
        compiler_params=_cparams(("parallel",)),
    )(proj, proj, proj, qn_w, kvn_w, wq, wukv, *qtab, *ktab, qbias, kbias)


def _diag_mask(t):
    return lax.broadcasted_iota(jnp.int32, (t, t), 0) <= lax.broadcasted_iota(jnp.int32, (t, t), 1)


def _silu(z):
    return z * _sigmoid(z)


def _attn_fwd(qq, kk, vv, proj, s):
    t = _row_tile(s, 384)
    n = s // t

    def body(q_ref, k_ref, v_ref, z_ref, o_ref, y_ref, lse_ref, acc_ref, m_ref):
        qi = pl.program_id(1)
        q = q_ref[...]
        m_ref[...] = jnp.full(m_ref.shape, NEG_INF, F32)
        acc_ref[...] = jnp.zeros(acc_ref.shape, F32)

        def keys(ki):
            return pl.ds(pl.multiple_of(ki * t, t), t)

        def scores(ki):
            return _dot(k_ref[keys(ki), :], q, "nt")

        def tile(ki, st):
            m_old = m_ref[...]
            m_new = jnp.maximum(m_old, jnp.max(st, axis=0, keepdims=True))
            pt = jnp.exp(st - m_new).astype(BF16)
            acc_ref[...] = jnp.exp(m_old - m_new) * acc_ref[...] + _dot(v_ref[keys(ki), :], pt, "tn")
            m_ref[...] = m_new

        def inner(ki, st):
            st_after = scores(ki + 1)
            tile(ki, st)
            return st_after

        st_diag = lax.fori_loop(0, qi, inner, scores(0))
        tile(qi, jnp.where(_diag_mask(t), st_diag, NEG_INF))

        l = acc_ref[DH:DH + 1, :]
        o = (acc_ref[:DH, :] / l).T
        o_ref[...] = o.astype(BF16)
        y_ref[...] = (o * _silu(z_ref[...].astype(F32))).astype(BF16)
        lse_ref[0, 0] = m_ref[...] + jnp.log(l)

    qtile = pl.BlockSpec((t, DH), lambda h, i: (i, h))
    head = pl.BlockSpec((s, HD), lambda h, i: (0, h))
    return pl.pallas_call(
        body, name="attn_fwd", grid=(HEADS, s // t),
        in_specs=[pl.BlockSpec((t, HD), lambda h, i: (i, h)), head, head,
                  pl.BlockSpec((t, DH), lambda h, i: (i, Z_MLA // DH + h))],
        out_specs=[qtile, qtile, pl.BlockSpec((1, 1, 1, t), lambda h, i: (h, i, 0, 0))],
        out_shape=[jax.ShapeDtypeStruct((s, WIDTH), BF16), jax.ShapeDtypeStruct((s, WIDTH), BF16),
                   jax.ShapeDtypeStruct((HEADS, s // t, 1, t), F32)],
        scratch_shapes=[pltpu.VMEM((HD, t), F32), pltpu.VMEM((1, t), F32)],
        compiler_params=_cparams(("parallel", "arbitrary")),
    )(qq, kk, vv, proj)


RET_CHUNKS = 3


def _ret_consts():
    log_g = np.log1p(-(2.0 ** (-5.0 - np.arange(HEADS, dtype=np.float64))))
    n = np.arange(BLK, dtype=np.float64)
    diff = n[:, None] - n[None, :]
    decay = np.where(diff >= 0, np.exp(log_g[:, None, None] * np.maximum(diff, 0.0)), 0.0)
    zeta = np.broadcast_to(np.exp(log_g[:, None] * (BLK - 1.0 - n))[:, :, None], (HEADS, BLK, DH))
    xi = np.broadcast_to(np.exp(log_g[:, None] * (n + 1.0))[:, :, None], (HEADS, BLK, DH))
    gamma = [float(np.float32(np.exp(g * BLK))) for g in log_g]
    return (jnp.asarray(decay, F32), jnp.asarray(zeta, F32), jnp.asarray(xi, F32)), gamma


def _ret_fwd(proj, gn_w, gn_b, c128, s128, consts, gamma, s):
    nb = s // BLK
    cps = RET_CHUNKS if nb % RET_CHUNKS == 0 else 1
    decay, zeta, xi = consts

    def body(rq_ref, rk_ref, rv_ref, z_ref, gw_ref, gb_ref, c_ref, s_ref, dm_ref, zt_ref, xi_ref,
             y_ref, on_ref, rstd_ref, qr_ref, kr_ref, rall_ref, state):
        @pl.when(pl.program_id(0) == 0)
        def _():
            state[...] = jnp.zeros_like(state)

        for sub in range(cps):
            rows = slice(sub * BLK, (sub + 1) * BLK)
            c, sg = c_ref[rows, :], s_ref[rows, :]
            for h in range(HEADS):
                sl = slice(h * DH, (h + 1) * DH)
                q = _rope128(rq_ref[rows, sl].astype(F32), c, sg).astype(BF16)
                kf = _rope128(rk_ref[rows, sl].astype(F32), c, sg) * RET_SCALE
                k = kf.astype(BF16)
                v = rv_ref[rows, sl]
                qr_ref[rows, sl] = q
                kr_ref[rows, sl] = k
                r_prev = state[h]
                rall_ref[sub, h] = r_prev
                a = _dot(q, k, "nt") * dm_ref[h]
                o = _dot(a.astype(BF16), v, "nn") + _dot(q, r_prev.astype(BF16), "nn") * xi_ref[h]
                state[h] = r_prev * gamma[h] + _dot((kf * zt_ref[h]).astype(BF16), v, "tn")
                mu = jnp.mean(o, axis=-1, keepdims=True)
                var = jnp.mean(jnp.square(o - mu), axis=-1, keepdims=True)
                rstd = lax.rsqrt(var + GN_EPS)
                on = (o - mu) * rstd
                rstd_ref[h, rows, :] = rstd
                on_ref[rows, sl] = on.astype(BF16)
                ogn = on * gw_ref[:, sl] + gb_ref[:, sl]
                y_ref[rows, sl] = (ogn * _silu(z_ref[rows, sl].astype(F32))).astype(BF16)

    seg = lambda cb: pl.BlockSpec((cps * BLK, WIDTH), lambda i: (i, cb))
    full = lambda a: pl.BlockSpec(a.shape, lambda i: (0,) * a.ndim)
    tab = pl.BlockSpec((cps * BLK, DH), lambda i: (i, 0))
    return pl.pallas_call(
        body, name="ret_fwd", grid=(nb // cps,),
        in_specs=[seg(R_Q // WIDTH), seg(R_K // WIDTH), seg(R_V // WIDTH), seg(Z_RET // WIDTH), full(gn_w), full(gn_b),
                  tab, tab, full(decay), full(zeta), full(xi)],
        out_specs=[seg(0), seg(0), pl.BlockSpec((HEADS, cps * BLK, 1), lambda i: (0, i, 0)), seg(0), seg(0),
                   pl.BlockSpec((cps, HEADS, DH, DH), lambda i: (i, 0, 0, 0))],
        out_shape=[jax.ShapeDtypeStruct((s, WIDTH), BF16), jax.ShapeDtypeStruct((s, WIDTH), BF16),
                   jax.ShapeDtypeStruct((HEADS, s, 1), F32), jax.ShapeDtypeStruct((s, WIDTH), BF16),
                   jax.ShapeDtypeStruct((s, WIDTH), BF16), jax.ShapeDtypeStruct((nb, HEADS, DH, DH), F32)],
        scratch_shapes=[pltpu.VMEM((HEADS, DH, DH), F32)],
        compiler_params=_cparams(("arbitrary",)),
    )(proj, proj, proj, proj, gn_w, gn_b, c128, s128, decay, zeta, xi)


def _merge_fwd(y_mla, y_ret, wbm, wbr, proj, s):
    tm, tn = _row_tile(s, 1408), 512

    def body(ym_ref, yr_ref, wm_ref, wr_ref, g0_ref, g1_ref, um_ref, ur_ref, mg_ref):
        um = _dot(ym_ref[...], wm_ref[...], "nt")
        ur = _dot(yr_ref[...], wr_ref[...], "nt")
        um_ref[...] = um.astype(BF16)
        ur_ref[...] = ur.astype(BF16)
        mg_ref[...] = (_sigmoid(g0_ref[...].astype(F32)) * um + _sigmoid(g1_ref[...].astype(F32)) * ur).astype(BF16)

    yspec = pl.BlockSpec((tm, WIDTH), lambda i, j: (i, 0))
    wspec = pl.BlockSpec((tn, WIDTH), lambda i, j: (j, 0))
    ospec = pl.BlockSpec((tm, tn), lambda i, j: (i, j))
    return pl.pallas_call(
        body, name="merge_fwd", grid=(s // tm, D // tn),
        in_specs=[yspec, yspec, wspec, wspec, pl.BlockSpec((tm, tn), lambda i, j: (i, GATE0 // tn + j)),
                  pl.BlockSpec((tm, tn), lambda i, j: (i, GATE1 // tn + j))],
        out_specs=[ospec, ospec, ospec],
        out_shape=[jax.ShapeDtypeStruct((s, D), BF16)] * 3,
        compiler_params=_cparams(("parallel", "parallel")),
    )(y_mla, y_ret, wbm, wbr, proj, proj)


def _out_fwd(merged, big, h, s):
    tm, tn = _row_tile(s, 1408), 512

    def body(m_ref, w_ref, h_ref, o_ref):
        o_ref[...] = h_ref[...] + _dot(m_ref[...], w_ref[...], "nn")

    return pl.pallas_call(
        body, name="out_fwd", grid=(s // tm, D // tn),
        in_specs=[pl.BlockSpec((tm, D), lambda i, j: (i, 0)), pl.BlockSpec((D, tn), lambda i, j: (BIG_OUT // D, j)),
                  pl.BlockSpec((tm, tn), lambda i, j: (i, j))],
        out_specs=pl.BlockSpec((tm, tn), lambda i, j: (i, j)),
        out_shape=jax.ShapeDtypeStruct((s, D), F32),
        compiler_params=_cparams(("parallel", "parallel")),
    )(merged, big, h)


def _loss_bwd(h2, target, fin_w, s):
    nb = s // BLK

    def body(h2_ref, t_ref, w_ref, dh_ref, dhb_ref, loss_ref, dw_ref):
        i = pl.program_id(0)

        @pl.when(i == 0)
        def _():
            dh_ref[...] = jnp.zeros_like(dh_ref)
            dhb_ref[...] = jnp.zeros_like(dhb_ref)
            loss_ref[...] = jnp.zeros_like(loss_ref)
            dw_ref[...] = jnp.zeros_like(dw_ref)

        @pl.when(i > 0)
        def _():
            hv = h2_ref[...]
            w = w_ref[...]
            r = lax.rsqrt(jnp.mean(hv * hv, axis=-1, keepdims=True) + NORM_EPS)
            nrm = hv * r
            e = nrm * w - t_ref[...]
            loss_ref[...] += jnp.full(loss_ref.shape, 0.5 / D, F32) * jnp.sum(e * e)
            dy = e * (1.0 / D)
            dw_ref[...] += jnp.sum(dy * nrm, axis=0, keepdims=True)
            g = dy * w
            dh = r * (g - nrm * jnp.mean(g * nrm, axis=-1, keepdims=True))
            dh_ref[...] = dh
            dhb_ref[...] = dh.astype(BF16)

    blk = pl.BlockSpec((BLK, D), lambda i: (i, 0))
    return pl.pallas_call(
        body, name="loss_bwd", grid=(nb,),
        in_specs=[blk, pl.BlockSpec((BLK, D), lambda i: (jnp.maximum(i - 1, 0), 0)), pl.BlockSpec((1, D), lambda i: (0, 0))],
        out_specs=[blk, blk, pl.BlockSpec((8, 128), lambda i: (0, 0)), pl.BlockSpec((1, D), lambda i: (0, 0))],
        out_shape=[jax.ShapeDtypeStruct((s, D), F32), jax.ShapeDtypeStruct((s, D), BF16),
                   jax.ShapeDtypeStruct((8, 128), F32), jax.ShapeDtypeStruct((1, D), F32)],
        compiler_params=_cparams(("arbitrary",)),
    )(h2, target, fin_w)


def _merge_bwd(dh2b, big, u_mla, u_ret, proj, s):
    tm, tn = _row_tile(s, 1408), 512

    def body(d_ref, w_ref, um_ref, ur_ref, gate_ref, dproj_ref, dum_ref, dur_ref, dm_ref):
        branch, j = pl.program_id(1), pl.program_id(2)
        cols = pl.ds(pl.multiple_of(j * tn, tn), tn)

        @pl.when(branch == 0)
        def _():
            dm_ref[:, cols] = _dot(d_ref[...], w_ref[...], "nt")

        dm = dm_ref[:, cols]
        gt = _sigmoid(gate_ref[...].astype(F32))

        @pl.when(branch == 0)
        def _():
            dproj_ref[...] = (dm * um_ref[...].astype(F32) * gt * (1.0 - gt)).astype(BF16)
            dum_ref[...] = (dm * gt).astype(BF16)

        @pl.when(branch == 1)
        def _():
            dproj_ref[...] = (dm * ur_ref[...].astype(F32) * gt * (1.0 - gt)).astype(BF16)
            dur_ref[...] = (dm * gt).astype(BF16)

    last = D // tn - 1
    mla = pl.BlockSpec((tm, tn), lambda i, b, j: (i, jnp.where(b == 0, j, last)))
    ret = pl.BlockSpec((tm, tn), lambda i, b, j: (i, jnp.where(b == 0, 0, j)))
    gate = pl.BlockSpec((tm, tn), lambda i, b, j: (i, GATE0 // tn + b * (D // tn) + j))
    return pl.pallas_call(
        body, name="merge_bwd", grid=(s // tm, 2, D // tn),
        in_specs=[pl.BlockSpec((tm, D), lambda i, b, j: (i, 0)),
                  pl.BlockSpec((tn, D), lambda i, b, j: (BIG_OUT // tn + jnp.where(b == 0, j, last), 0)), mla, ret, gate],
        out_specs=[gate, mla, ret],
        out_shape=[jax.ShapeDtypeStruct((s, PROJ_W), BF16), jax.ShapeDtypeStruct((s, D), BF16),
                   jax.ShapeDtypeStruct((s, D), BF16)],
        scratch_shapes=[pltpu.VMEM((tm, D), F32)],
        compiler_params=_cparams(("parallel", "arbitrary", "arbitrary")),
    )(dh2b, big, u_mla, u_ret, proj)


def _dsilu(z):
    sg = _sigmoid(z)
    return sg * (1.0 + z * (1.0 - sg))


def _branch_mla_bwd(dproj, du, wbm, o_mla, proj, s):
    tm = _row_tile(s, 384)

    def body(dproj_in, du_ref, w_ref, o_ref, z_ref, dz_ref, do_ref, delta_ref):
        del dproj_in
        dy = _dot(du_ref[...], w_ref[...], "nn")
        z = z_ref[...].astype(F32)
        o = o_ref[...].astype(F32)
        do = dy * _silu(z)
        do_ref[...] = do.astype(BF16)
        dz_ref[...] = (dy * o * _dsilu(z)).astype(BF16)
        prod = do * o
        for h in range(HEADS):
            delta_ref[h] = jnp.sum(prod[:, h * DH:(h + 1) * DH], axis=-1, keepdims=True)

    row = lambda w, cb: pl.BlockSpec((tm, w), lambda i: (i, cb))
    return pl.pallas_call(
        body, name="branch_mla_bwd", grid=(s // tm,),
        in_specs=[ANY, row(D, 0), pl.BlockSpec((D, WIDTH), lambda i: (0, 0)), row(WIDTH, 0),
                  row(WIDTH, Z_MLA // WIDTH)],
        out_specs=[row(WIDTH, Z_MLA // WIDTH), row(WIDTH, 0), pl.BlockSpec((HEADS, tm, 1), lambda i: (0, i, 0))],
        out_shape=[jax.ShapeDtypeStruct((s, PROJ_W), BF16), jax.ShapeDtypeStruct((s, WIDTH), BF16),
                   jax.ShapeDtypeStruct((HEADS, s, 1), F32)],
        input_output_aliases={0: 0},
        compiler_params=_cparams(("parallel",)),
    )(dproj, du, wbm, o_mla, proj)


def _branch_ret_bwd(dproj, du, wbr, on, rstd, gn_w, gn_b, proj, s):
    tm = _row_tile(s, 384)

    def body(dproj_in, du_ref, w_ref, on_ref, rstd_ref, gw_ref, gb_ref, z_ref, dz_ref, do_ref, dgw_ref, dgb_ref):
        del dproj_in

        @pl.when(pl.program_id(0) == 0)
        def _():
            dgw_ref[...] = jnp.zeros_like(dgw_ref)
            dgb_ref[...] = jnp.zeros_like(dgb_ref)

        dy = _dot(du_ref[...], w_ref[...], "nn")
        z = z_ref[...].astype(F32)
        on = on_ref[...].astype(F32)
        gw = gw_ref[...]
        dogn = dy * _silu(z)
        dz_ref[...] = (dy * (on * gw + gb_ref[...]) * _dsilu(z)).astype(BF16)
        dgw_ref[...] += jnp.sum(dogn * on, axis=0, keepdims=True)
        dgb_ref[...] += jnp.sum(dogn, axis=0, keepdims=True)
        don = dogn * gw
        for h in range(HEADS):
            sl = slice(h * DH, (h + 1) * DH)
            dn, nh = don[:, sl], on[:, sl]
            do = rstd_ref[h] * (dn - jnp.mean(dn, axis=-1, keepdims=True)
                                - nh * jnp.mean(dn * nh, axis=-1, keepdims=True))
            do_ref[:, sl] = do.astype(BF16)

    row = lambda w, cb: pl.BlockSpec((tm, w), lambda i: (i, cb))
    vec = pl.BlockSpec((1, WIDTH), lambda i: (0, 0))
    return pl.pallas_call(
        body, name="branch_ret_bwd", grid=(s // tm,),
        in_specs=[ANY, row(D, 0), pl.BlockSpec((D, WIDTH), lambda i: (0, 0)), row(WIDTH, 0),
                  pl.BlockSpec((HEADS, tm, 1), lambda i: (0, i, 0)), vec, vec, row(WIDTH, Z_RET // WIDTH)],
        out_specs=[row(WIDTH, Z_RET // WIDTH), row(WIDTH, 0), vec, vec],
        out_shape=[jax.ShapeDtypeStruct((s, PROJ_W), BF16), jax.ShapeDtypeStruct((s, WIDTH), BF16)]
        + [jax.ShapeDtypeStruct((1, WIDTH), F32)] * 2,
        input_output_aliases={0: 0},
        compiler_params=_cparams(("arbitrary",)),
    )(dproj, du, wbr, on, rstd, gn_w, gn_b, proj)


def _ret_bwd(dproj, qr, kr, proj, rall, do_ret, c128, s128, consts, gamma, s):
    cps = RET_CHUNKS if (s // BLK) % RET_CHUNKS == 0 else 1
    nb = s // (cps * BLK)
    decay, zeta, xi = consts

    def body(dproj_in, q_ref, k_ref, v_ref, r_ref, do_ref, c_ref, s_ref, dm_ref, zt_ref, xi_ref, out_ref, gstate):
        del dproj_in
        dq_ref, dk_ref, dv_ref = (out_ref.at[:, pl.ds(off, WIDTH)] for off in (R_Q, R_K, R_V))

        @pl.when(pl.program_id(0) == 0)
        def _():
            gstate[...] = jnp.zeros_like(gstate)

        for sub in reversed(range(cps)):
            rows = slice(sub * BLK, (sub + 1) * BLK)
            c, sg = c_ref[rows, :], s_ref[rows, :]
            for h in range(HEADS):
                sl = slice(h * DH, (h + 1) * DH)
                q, k, v, do = q_ref[rows, sl], k_ref[rows, sl], v_ref[rows, sl], do_ref[rows, sl]
                dm = dm_ref[h]
                g_next = gstate[h]
                gb = g_next.astype(BF16)
                a = (_dot(q, k, "nt") * dm).astype(BF16)
                da = (_dot(do, v, "nt") * dm).astype(BF16)
                dox = (do.astype(F32) * xi_ref[h]).astype(BF16)
                dq = _dot(da, k, "nn") + _dot(dox, r_ref[sub, h].astype(BF16), "nt")
                dk = _dot(da, q, "tn") + _dot(v, gb, "nt") * zt_ref[h]
                kz = (k.astype(F32) * zt_ref[h]).astype(BF16)
                dv = _dot(a, do, "tn") + _dot(kz, gb, "nn")
                gstate[h] = g_next * gamma[h] + _dot(q, dox, "tn")
                dk = dk * RET_SCALE
                dq_ref[rows, sl] = _rope128(dq, c, -sg).astype(BF16)
                dk_ref[rows, sl] = _rope128(dk, c, -sg).astype(BF16)
                dv_ref[rows, sl] = dv.astype(BF16)

    rev = lambda cb: pl.BlockSpec((cps * BLK, WIDTH), lambda i: (nb - 1 - i, cb))
    full = lambda a: pl.BlockSpec(a.shape, lambda i: (0,) * a.ndim)
    tab = pl.BlockSpec((cps * BLK, DH), lambda i: (nb - 1 - i, 0))
    return pl.pallas_call(
        body, name="ret_bwd", grid=(nb,),
        in_specs=[ANY, rev(0), rev(0), rev(R_V // WIDTH),
                  pl.BlockSpec((cps, HEADS, DH, DH), lambda i: (nb - 1 - i, 0, 0, 0)),
                  rev(0), tab, tab, full(decay), full(zeta), full(xi)],
        out_specs=pl.BlockSpec((cps * BLK, 3 * WIDTH), lambda i: (nb - 1 - i, R_Q // (3 * WIDTH))),
        out_shape=jax.ShapeDtypeStruct((s, PROJ_W), BF16),
        scratch_shapes=[pltpu.VMEM((HEADS, DH, DH), F32)],
        input_output_aliases={0: 0},
        compiler_params=_cparams(("arbitrary",)),
    )(dproj, qr, kr, proj, rall, do_ret, c128, s128, decay, zeta, xi)


def _attn_bwd(qq, kk, vv, do, lse, delta, s):
    t = _row_tile(s, 384)
    n = s // t

    def body(q_ref, k_ref, v_ref, do_ref, lse_ref, delta_ref, dq_ref, dk_ref, dv_ref, dk_acc, dv_acc):
        ki = pl.program_id(1)

        @pl.when(ki == 0)
        def _():
            dq_ref[...] = jnp.zeros(dq_ref.shape, F32)

        k, v = k_ref[...], v_ref[...]
        dk_acc[...] = jnp.zeros(dk_acc.shape, F32)
        dv_acc[...] = jnp.zeros(dv_acc.shape, F32)

        def rows(qi):
            return pl.ds(pl.multiple_of(qi * t, t), t)

        def products(qi):
            return _dot(k, q_ref[rows(qi), :], "nt"), _dot(v, do_ref[rows(qi), :], "nt")

        def tile(qi, st, dpt):
            q, dov = q_ref[rows(qi), :], do_ref[rows(qi), :]
            pt = jnp.exp(st - lse_ref[0, qi])
            dv_acc[...] += _dot(pt.astype(BF16), dov, "nn")
            dst = (pt * (dpt - delta_ref[0, qi])).astype(BF16)
            dk_acc[...] += _dot(dst, q, "nn")
            dq_ref[rows(qi), :] += _dot(dst, k, "tn")

        def inner(qi, carry):
            after = products(jnp.minimum(qi + 1, n - 1))
            tile(qi, *carry)
            return after

        st, dpt = products(ki)
        ahead = products(jnp.minimum(ki + 1, n - 1))
        tile(ki, jnp.where(_diag_mask(t), st, NEG_INF), dpt)
        lax.fori_loop(ki + 1, n, inner, ahead)

        dk_ref[...] = dk_acc[...].astype(BF16)
        dv_ref[...] = dv_acc[...].astype(BF16)

    head = lambda w: pl.BlockSpec((s, w), lambda h, i: (0, h))
    stat = pl.BlockSpec((1, n, 1, t), lambda h, i: (h, 0, 0, 0))
    return pl.pallas_call(
        body, name="attn_bwd", grid=(HEADS, n),
        in_specs=[head(HD), pl.BlockSpec((t, HD), lambda h, i: (i, h)), pl.BlockSpec((t, DH), lambda h, i: (i, 2 * h)),
                  head(DH), stat, stat],
        out_specs=[head(HD), pl.BlockSpec((t, HD), lambda h, i: (i, h)), pl.BlockSpec((t, DH), lambda h, i: (i, h))],
        out_shape=[jax.ShapeDtypeStruct((s, HEADS * HD), F32), jax.ShapeDtypeStruct((s, HEADS * HD), BF16),
                   jax.ShapeDtypeStruct((s, WIDTH), BF16)],
        scratch_shapes=[pltpu.VMEM((t, HD), F32), pltpu.VMEM((t, DH), F32)],
        compiler_params=_cparams(("parallel", "arbitrary")),
    )(qq, kk, vv, do, lse, delta)


def _rms_bwd(dn, nhat, r, w):
    g = dn * w
    return r * (g - nhat * jnp.mean(g * nhat, axis=-1, keepdims=True)), dn * nhat


def _mla_prep_bwd(dproj, dqq, dkk, dvv, proj, rq, rkv, qn_w, kvn_w, wq, wukv, qtab, ktab, s):
    tm = _row_tile(s, 384)
    tail = PROJ_W - C_Q

    def body(dproj_in, dqq_ref, dkk_ref, dvv_ref, cq_ref, ckv_ref, rq_ref, rkv_ref, qnw_ref, kvnw_ref, wq_ref,
             wukv_ref, qc_ref, qa_ref, qb_ref, kc_ref, ka_ref, kb_ref,
             tail_ref, dq_ref, dkv_ref, dqnw_ref, dkvnw_ref):
        del dproj_in
        dcq_ref = tail_ref.at[:, pl.ds(0, Q_RANK)]
        dckv_ref = tail_ref.at[:, pl.ds(C_KV - C_Q, KV_RANK)]
        dkpe_ref = tail_ref.at[:, pl.ds(K_PE - C_Q, 2 * DH)]

        @pl.when(pl.program_id(0) == 0)
        def _():
            dqnw_ref[...] = jnp.zeros_like(dqnw_ref)
            dkvnw_ref[...] = jnp.zeros_like(dkvnw_ref)

        qc, qa, qb = qc_ref[...], qa_ref[...], qb_ref[...]
        dkp = jnp.zeros((tm, DH), F32)
        for h in range(HEADS):
            lo, mid, hi = h * HD, h * HD + DH, (h + 1) * HD
            dq_ref[:, lo:hi] = _rope64(dqq_ref[:, lo:hi], qc, -qa, -qb).astype(BF16)
            dkv_ref[:, lo:mid] = dkk_ref[:, lo:mid]
            dkv_ref[:, mid:hi] = dvv_ref[:, h * DH:(h + 1) * DH]
            dkp = dkp + dkk_ref[:, mid:hi].astype(F32)
        dcqn = _dot(dq_ref[...], wq_ref[...], "nn")
        rq_v = rq_ref[...]
        dcq, prod = _rms_bwd(dcqn, cq_ref[...].astype(F32) * rq_v, rq_v, qnw_ref[...])
        dcq_ref[...] = dcq.astype(BF16)
        dqnw_ref[...] += jnp.sum(prod, axis=0, keepdims=True)
        dckvn = _dot(dkv_ref[...], wukv_ref[...], "nn")
        rkv_v = rkv_ref[...]
        dckv, prod = _rms_bwd(dckvn, ckv_ref[...].astype(F32) * rkv_v, rkv_v, kvnw_ref[...])
        dckv_ref[...] = dckv.astype(BF16)
        dkvnw_ref[...] += jnp.sum(prod, axis=0, keepdims=True)
        dkpe_ref[:, :DH] = _rope64(dkp, kc_ref[...], -ka_ref[...], -kb_ref[...]).astype(BF16)
        dkpe_ref[:, DH:] = jnp.zeros((tm, DH), BF16)

    row = lambda w, cb: pl.BlockSpec((tm, w), lambda i: (i, cb))
    full = lambda a: pl.BlockSpec(a.shape, lambda i: (0, 0))
    wide = jax.ShapeDtypeStruct((s, HEADS * HD), BF16)
    return pl.pallas_call(
        body, name="mla_prep_bwd", grid=(s // tm,),
        in_specs=[ANY, row(HEADS * HD, 0), row(HEADS * HD, 0), row(WIDTH, 0),
                  row(Q_RANK, C_Q // Q_RANK), row(KV_RANK, C_KV // KV_RANK), row(1, 0), row(1, 0),
                  full(qn_w), full(kvn_w), full(wq), full(wukv), row(HD, 0), row(HD, 0), row(HD, 0),
                  row(DH, 0), row(DH, 0), row(DH, 0)],
        out_specs=[row(tail, C_Q // tail), row(HEADS * HD, 0), row(HEADS * HD, 0),
                   pl.BlockSpec((1, Q_RANK), lambda i: (0, 0)), pl.BlockSpec((1, KV_RANK), lambda i: (0, 0))],
        out_shape=[jax.ShapeDtypeStruct((s, PROJ_W), BF16), wide, wide,
                   jax.ShapeDtypeStruct((1, Q_RANK), F32), jax.ShapeDtypeStruct((1, KV_RANK), F32)],
        input_output_aliases={0: 0},
        compiler_params=_cparams(("arbitrary",)),
    )(dproj, dqq, dkk, dvv, proj, proj, rq, rkv, qn_w, kvn_w, wq, wukv, *qtab, *ktab)


def _rms_in_bwd(dxn, h, r, dh2, norm_w, s):
    def body(dxn_ref, h_ref, r_ref, dh2_ref, w_ref, gx_ref, gm_ref, dw_ref):
        i = pl.program_id(0)
        r_v = r_ref[...]
        dx, prod = _rms_bwd(dxn_ref[...], h_ref[...] * r_v, r_v, w_ref[...])
        dh = dh2_ref[...] + dx

        @pl.when(i == 0)
        def _():
            gm_ref[...] = dh
            dw_ref[...] = jnp.sum(prod, axis=0, keepdims=True)

        @pl.when(i > 0)
        def _():
            gx_ref[...] = dh
            dw_ref[...] += jnp.sum(prod, axis=0, keepdims=True)

    blk = pl.BlockSpec((BLK, D), lambda i: (i, 0))
    return pl.pallas_call(
        body, name="rms_in_bwd", grid=(s // BLK,),
        in_specs=[blk, blk, pl.BlockSpec((BLK, 1), lambda i: (i, 0)), blk, pl.BlockSpec((1, D), lambda i: (0, 0))],
        out_specs=[pl.BlockSpec((BLK, D), lambda i: (jnp.maximum(i - 1, 0), 0)), pl.BlockSpec((BLK, D), lambda i: (0, 0)),
                   pl.BlockSpec((1, D), lambda i: (0, 0))],
        out_shape=[jax.ShapeDtypeStruct((s - BLK, D), F32), jax.ShapeDtypeStruct((BLK, D), F32),
                   jax.ShapeDtypeStruct((1, D), F32)],
        compiler_params=_cparams(("arbitrary",)),
    )(dxn, h, r, dh2, norm_w)


def _adam_math(w, g, m, v):
    m = ADAM_B1 * m + (1.0 - ADAM_B1) * g
    v = ADAM_B2 * v + (1.0 - ADAM_B2) * (g * g)
    m_hat = m / (1.0 - ADAM_B1 ** ADAM_STEP)
    v_hat = v / (1.0 - ADAM_B2 ** ADAM_STEP)
    return -ADAM_LR * (m_hat / (jnp.sqrt(v_hat) + ADAM_EPS) + ADAM_WD * w), m, v


def _adamw(w, g, m, v, name):
    rows, cols = w.shape
    tr = rows
    for cand in (128, 64, 32, 16, 8):
        if rows % cand == 0:
            tr = cand
            break

    def body(w_ref, g_ref, m_ref, v_ref, d_ref, nm_ref, nv_ref):
        d_ref[...], nm_ref[...], nv_ref[...] = _adam_math(w_ref[...], g_ref[...], m_ref[...], v_ref[...])

    spec = pl.BlockSpec((tr, cols), lambda i: (i, 0))
    return pl.pallas_call(
        body, name=name, grid=(rows // tr,), in_specs=[spec] * 4, out_specs=[spec] * 3,
        out_shape=[jax.ShapeDtypeStruct((rows, cols), F32)] * 3,
        compiler_params=_cparams(("parallel",)),
    )(w, g, m, v)


def _adamw_t(w_t, g_t, m_t, v_t, name):
    cols, rows = w_t.shape

    def body(w_ref, g_ref, m_ref, v_ref, go_ref, d_ref, nm_ref, nv_ref):
        g = g_ref[...]
        go_ref[...] = g
        d_ref[...], nm_ref[...], nv_ref[...] = _adam_math(w_ref[...], g, m_ref[...], v_ref[...])

    blk = pl.BlockSpec((128, rows), lambda k: (k, 0))
    return pl.pallas_call(
        body, name=name, grid=(pl.cdiv(cols, 128),), in_specs=[blk] * 4, out_specs=[blk] * 4,
        out_shape=[jax.ShapeDtypeStruct((cols, rows), F32)] * 4, compiler_params=_cparams(("parallel",)),
    )(w_t, g_t, m_t, v_t)


def _adamw_small(w, gall, m, v):
    def body(w_ref, g_ref, m_ref, v_ref, gs_ref, d_ref, nm_ref, nv_ref):
        g = g_ref[0]
        for dev in range(1, 8):
            g = g + g_ref[dev]
        gs_ref[...] = g
        d_ref[...], nm_ref[...], nv_ref[...] = _adam_math(w_ref[...], g, m_ref[...], v_ref[...])

    return pl.pallas_call(
        body, name="adamw_small", out_shape=[jax.ShapeDtypeStruct((SM_ROWS, 128), F32)] * 4,
        compiler_params=pltpu.CompilerParams(vmem_limit_bytes=VMEM_LIMIT),
    )(w, gall, m, v)


ADD_ROWS = 464
HALF_BLOCKS = PK_HALF // ADD_ROWS


def _rs_core_add(srcs, land, pos):
    def body(pos_ref, *refs):
        src_refs, (land_ref, out_ref, own, sems) = refs[:len(srcs)], refs[len(srcs):]
        step = pl.program_id(0)

        def fetch(s):
            for hf in range(2):
                @pl.when(pos_ref[1] == hf)
                def _():
                    for k, r0, p, n in _grad_half(s, hf):
                        pltpu.make_async_copy(src_refs[k].at[pl.ds(r0, n), :], own.at[s % 2, pl.ds(p, n), :],
                                              sems.at[s % 2]).start()

        @pl.when(step == 0)
        def _():
            fetch(0)

        for s in range(N_CHIPS):
            @pl.when(step == s)
            def _():
                if s + 1 < N_CHIPS:
                    fetch(s + 1)
                slot = own.at[s % 2]
                pltpu.make_async_copy(slot, slot, sems.at[s % 2]).wait()
                out_ref[0] = (slot[...].astype(F32) + land_ref[0].astype(F32)).astype(BF16)

    blk = pl.BlockSpec((1, PK_HALF, D), lambda s, pos_ref: (s, 0, 0))
    return pl.pallas_call(
        body, name="rs_core_add",
        grid_spec=pltpu.PrefetchScalarGridSpec(
            num_scalar_prefetch=1, grid=(N_CHIPS,), in_specs=[ANY] * len(srcs) + [blk], out_specs=blk,
            scratch_shapes=[pltpu.VMEM((2, PK_HALF, D), BF16), pltpu.SemaphoreType.DMA((2,))]),
        out_shape=jax.ShapeDtypeStruct((N_CHIPS, PK_HALF, D), BF16),
        compiler_params=_cparams(("arbitrary",)),
    )(pos, *srcs, land)


def _rs_chip_add(cp, land, pos):
    def body(pos_ref, a_ref, l_ref, o_ref):
        o_ref[0] = ((a_ref[0].astype(F32) + l_ref[0].astype(F32)) + l_ref[1].astype(F32)) + l_ref[2].astype(F32)

    return pl.pallas_call(
        body, name="rs_chip_add",
        grid_spec=pltpu.PrefetchScalarGridSpec(
            num_scalar_prefetch=1, grid=(HALF_BLOCKS,),
            in_specs=[pl.BlockSpec((1, ADD_ROWS, D), lambda i, pos_ref: (pos_ref[0], i, 0)),
                      pl.BlockSpec((3, ADD_ROWS, D), lambda i, pos_ref: (0, i, 0))],
            out_specs=pl.BlockSpec((1, ADD_ROWS, D), lambda i, pos_ref: (pos_ref[1], i, 0))),
        out_shape=jax.ShapeDtypeStruct((2, PK_HALF, D), F32),
        compiler_params=_cparams(("parallel",)),
    )(pos, cp, land)


ANY = pl.BlockSpec(memory_space=pl.ANY)


def _mesh_pos():
    return lax.axis_index("x"), lax.axis_index("y"), lax.axis_index("c")


def _chip_peer(x, y, c, r):
    return (jnp.bitwise_xor(x, r >> 1), jnp.bitwise_xor(y, r & 1), c)


def _allgather_w_in(big0, wt, meta_loc, x2, norm_w):
    s_rows = x2.shape[0] + BLK
    tok = next(t for t in (512, 256, 128) if x2.shape[0] % t == 0)

    def body(big0_ref, wt_ref, meta_ref, x_ref, nw_ref, big_ref, metaf_ref, h_ref, xn_ref, r_ref,
             lsem, ssem, rsem, fssem, frsem, msem_s, msem_r, osem, xbuf, xnbuf, rbuf, mbuf):
        del big0_ref
        x, y, c = _mesh_pos()
        j = 2 * x + y
        me, sibling = (x, y, c), (x, y, 1 - c)

        def half_wait(s_sem, r_sem, halves=1):
            rows = big_ref.at[pl.ds(0, halves * AG_IN_HALF), :]
            return pltpu.make_async_remote_copy(src_ref=rows, dst_ref=rows, send_sem=s_sem, recv_sem=r_sem,
                                                device_id=me, device_id_type=MESH)

        own_meta = pltpu.make_async_copy(meta_ref, metaf_ref.at[j], lsem)
        own_meta.start()
        for s in range(N_CHIPS):
            for hf in range(2):
                @pl.when((j == s) & (c == hf))
                def _():
                    for r in (1, 2, 3):
                        for p, b, n in _ag_half(s, True, hf):
                            pltpu.make_async_remote_copy(
                                src_ref=wt_ref.at[pl.ds(p, n), :], dst_ref=big_ref.at[pl.ds(b, n), :],
                                send_sem=ssem.at[r - 1], recv_sem=rsem.at[r - 1],
                                device_id=_chip_peer(x, y, c, r), device_id_type=MESH).start()

            @pl.when(j == s)
            def _():
                for hf in range(2):
                    for p, b, n in _ag_half(s, True, hf):
                        pltpu.make_async_remote_copy(
                            src_ref=wt_ref.at[pl.ds(p, n), :], dst_ref=big_ref.at[pl.ds(b, n), :],
                            send_sem=osem.at[0], recv_sem=osem.at[1], device_id=sibling, device_id_type=MESH).start()

        meta_copies = [pltpu.make_async_remote_copy(
            src_ref=meta_ref, dst_ref=metaf_ref.at[j], send_sem=msem_s.at[r - 1], recv_sem=msem_r.at[r - 1],
            device_id=_chip_peer(x, y, c, r), device_id_type=MESH) for r in (1, 2, 3)]
        for cp in meta_copies:
            cp.start()

        nw = nw_ref[...]

        def norm_rows(first, n):
            hv = xbuf[pl.ds(0, n), :]
            r = lax.rsqrt(jnp.mean(hv * hv, axis=-1, keepdims=True) + NORM_EPS)
            xnbuf[pl.ds(0, n), :] = (hv * r * nw).astype(BF16)
            rbuf[pl.ds(0, n), :] = r
            rows = pl.ds(first if isinstance(first, int) else pl.multiple_of(first, BLK), n)
            pltpu.sync_copy(xbuf.at[pl.ds(0, n), :], h_ref.at[rows, :])
            pltpu.sync_copy(xnbuf.at[pl.ds(0, n), :], xn_ref.at[rows, :])
            pltpu.sync_copy(rbuf.at[pl.ds(0, n), :], r_ref.at[rows, :])

        def token_rows(i, carry):
            pltpu.sync_copy(x_ref.at[pl.ds(pl.multiple_of(i * tok, tok), tok), :], xbuf)
            norm_rows(BLK + i * tok, tok)
            return carry

        lax.fori_loop(0, (s_rows - BLK) // tok, token_rows, 0)
        for cp in meta_copies:
            cp.wait_recv()
        own_meta.wait()
        xbuf[pl.ds(0, BLK), :] = jnp.zeros((BLK, D), F32)
        for sh in range(N_CHIPS):
            pltpu.sync_copy(metaf_ref.at[sh], mbuf)
            xbuf[PAD_FRONT:BLK, sh * (D // N_CHIPS):(sh + 1) * (D // N_CHIPS)] = mbuf[...]
        norm_rows(0, BLK)

        for r in (1, 2, 3):
            half_wait(ssem.at[r - 1], rsem.at[r - 1]).wait_recv()
            src_shard = jnp.bitwise_xor(j, r)
            for s in range(N_CHIPS):
                for hf in range(2):
                    @pl.when((src_shard == s) & (c == hf))
                    def _():
                        for _, b, n in _ag_half(s, True, hf):
                            rows = big_ref.at[pl.ds(b, n), :]
                            pltpu.make_async_remote_copy(
                                src_ref=rows, dst_ref=rows, send_sem=fssem.at[r - 1], recv_sem=frsem.at[r - 1],
                                device_id=sibling, device_id_type=MESH).start()
        for r in (1, 2, 3):
            half_wait(fssem.at[r - 1], frsem.at[r - 1]).wait_recv()
        for r in (1, 2, 3):
            half_wait(ssem.at[r - 1], rsem.at[r - 1]).wait_send()
            half_wait(fssem.at[r - 1], frsem.at[r - 1]).wait_send()
        own_rows = half_wait(osem.at[0], osem.at[1], halves=2)
        own_rows.wait_recv()
        own_rows.wait_send()
        for cp in meta_copies:
            cp.wait_send()

    dma3 = pltpu.SemaphoreType.DMA((3,))
    big, _, h, xn, r = pl.pallas_call(
        body, name="allgather_w_in", in_specs=[ANY, ANY, ANY, ANY, pl.BlockSpec(memory_space=pltpu.VMEM)],
        out_specs=[ANY] * 5,
        out_shape=[jax.ShapeDtypeStruct((BIG_ROWS, D), BF16), jax.ShapeDtypeStruct((N_CHIPS,) + meta_loc.shape, F32),
                   jax.ShapeDtypeStruct((s_rows, D), F32), jax.ShapeDtypeStruct((s_rows, D), BF16),
                   jax.ShapeDtypeStruct((s_rows, 1), F32)],
        scratch_shapes=[pltpu.SemaphoreType.DMA(()), dma3, dma3, dma3, dma3, dma3, dma3, pltpu.SemaphoreType.DMA((2,)),
                        pltpu.VMEM((tok, D), F32), pltpu.VMEM((tok, D), BF16), pltpu.VMEM((tok, 1), F32),
                        pltpu.VMEM(meta_loc.shape, F32)],
        input_output_aliases={0: 0},
        compiler_params=pltpu.CompilerParams(has_side_effects=True),
    )(big0, wt, meta_loc, x2, norm_w)
    return big, h, xn, r


HBM = pl.BlockSpec(memory_space=pltpu.HBM)
SEM = pl.BlockSpec(memory_space=pltpu.SEMAPHORE)
EFFECT = pltpu.SideEffectType.DATAFLOW_SIDE_EFFECTING
REST_PEERS = [(r, dc) for r in (1, 2, 3) for dc in (0, 1)]


def _allgather_rest_start(big):
    def body(big_ref, *rest):
        s_sems, r_sems = rest[:6], rest[6:12]
        x, y, c = _mesh_pos()
        j = 2 * x + y
        for s in range(N_CHIPS):
            for hf in range(2):
                @pl.when((j == s) & (c == hf))
                def _():
                    for k, (r, dc) in enumerate(REST_PEERS):
                        for _, b, n in _ag_half(s, False, hf):
                            rows = big_ref.at[pl.ds(b, n), :]
                            pltpu.make_async_remote_copy(
                                src_ref=rows, dst_ref=rows, send_sem=s_sems[k], recv_sem=r_sems[2 * (r - 1) + hf],
                                device_id=_chip_peer(x, y, dc, r), device_id_type=MESH).start()

    sem = pltpu.SemaphoreType.DMA(())
    out = pl.pallas_call(
        body, name="allgather_rest_start", out_shape=(sem,) * 12 + (pltpu.HBM(big.shape, big.dtype),),
        in_specs=(HBM,), out_specs=(SEM,) * 12 + (HBM,), input_output_aliases={0: 12},
        compiler_params=pltpu.CompilerParams(has_side_effects=EFFECT),
    )(pltpu.with_memory_space_constraint(big, pltpu.HBM))
    return out[:12], out[12]


def _allgather_rest_wait(sems, big, after):
    def body(big_ref, *rest):
        s_sems, r_sems = rest[:6], rest[6:12]
        me = _mesh_pos()
        rows = big_ref.at[pl.ds(0, AG_REST_HALF), :]
        for k in range(6):
            copy = pltpu.make_async_remote_copy(src_ref=rows, dst_ref=rows, send_sem=s_sems[k], recv_sem=r_sems[k],
                                                device_id=me, device_id_type=MESH)
            copy.wait_send()
            copy.wait_recv()

    return pl.pallas_call(
        body, name="allgather_rest_wait", out_shape=pltpu.HBM(big.shape, big.dtype),
        in_specs=(HBM,) + (SEM,) * 12 + (pl.BlockSpec(memory_space=pl.ANY),), out_specs=HBM,
        input_output_aliases={0: 0}, compiler_params=pltpu.CompilerParams(has_side_effects=EFFECT),
    )(big, *sems, after)


RS_SPLIT = 2


def _rs_core_exchange(srcs):
    def body(*refs):
        src_refs, (land_ref, ssem, rsem) = refs[:len(srcs)], refs[len(srcs):]
        x, y, c = _mesh_pos()
        me, sibling = (x, y, c), (x, y, 1 - c)
        for hf in range(2):
            @pl.when(c == hf)
            def _():
                for s in range(N_CHIPS):
                    for k, r0, p, n in _grad_half(s, 1 - hf):
                        pltpu.make_async_remote_copy(
                            src_ref=src_refs[k].at[pl.ds(r0, n), :], dst_ref=land_ref.at[s, pl.ds(p, n), :],
                            send_sem=ssem, recv_sem=rsem, device_id=sibling, device_id_type=MESH).start()
        whole = pltpu.make_async_remote_copy(src_ref=land_ref, dst_ref=land_ref, send_sem=ssem, recv_sem=rsem,
                                             device_id=me, device_id_type=MESH)
        whole.wait_recv()
        whole.wait_send()

    return pl.pallas_call(
        body, name="rs_core_exchange", in_specs=[ANY] * len(srcs), out_specs=ANY,
        out_shape=jax.ShapeDtypeStruct((N_CHIPS, PK_HALF, D), BF16),
        scratch_shapes=[pltpu.SemaphoreType.DMA(())] * 2,
        compiler_params=pltpu.CompilerParams(has_side_effects=True),
    )(*srcs)


def _rs_chip_start(cp):
    def body(cp_ref, land_ref, s1, s2, s3, r1, r2, r3, cp_thru, land_thru, token):
        x, y, c = _mesh_pos()
        j = 2 * x + y
        for r, s_sem, r_sem in zip((1, 2, 3), (s1, s2, s3), (r1, r2, r3)):
            pltpu.make_async_remote_copy(
                src_ref=cp_ref.at[jnp.bitwise_xor(j, r)], dst_ref=land_ref.at[r - 1], send_sem=s_sem, recv_sem=r_sem,
                device_id=_chip_peer(x, y, c, r), device_id_type=MESH).start()
        token[...] = jnp.zeros_like(token)

    land_shape = (3, PK_HALF, D)
    sem = pltpu.SemaphoreType.DMA(())
    out = pl.pallas_call(
        body, name="rs_chip_start",
        out_shape=(sem,) * 6 + (pltpu.HBM(cp.shape, cp.dtype), pltpu.HBM(land_shape, BF16),
                                jax.ShapeDtypeStruct((8, 128), F32)),
        in_specs=(HBM, HBM), out_specs=(SEM,) * 6 + (HBM, HBM, pl.BlockSpec(memory_space=pltpu.VMEM)),
        input_output_aliases={0: 6, 1: 7},
        compiler_params=pltpu.CompilerParams(has_side_effects=EFFECT),
    )(pltpu.with_memory_space_constraint(cp, pltpu.HBM),
      pltpu.with_memory_space_constraint(lax.empty(land_shape, BF16), pltpu.HBM))
    return out[:6], out[6], out[7], out[8]


def _rs_chip_wait(sems, cp_thru, land_thru, after):
    def body(cp_ref, land_ref, s1, s2, s3, r1, r2, r3, after_ref, cp_out, land_out):
        me = _mesh_pos()
        for r, s_sem, r_sem in zip((1, 2, 3), (s1, s2, s3), (r1, r2, r3)):
            copy = pltpu.make_async_remote_copy(src_ref=cp_ref.at[0], dst_ref=land_ref.at[r - 1], send_sem=s_sem,
                                                recv_sem=r_sem, device_id=me, device_id_type=MESH)
            copy.wait_send()
            copy.wait_recv()

    return pl.pallas_call(
        body, name="rs_chip_wait",
        out_shape=(pltpu.HBM(cp_thru.shape, cp_thru.dtype), pltpu.HBM(land_thru.shape, land_thru.dtype)),
        in_specs=(HBM, HBM) + (SEM,) * 6 + (pl.BlockSpec(memory_space=pl.ANY),), out_specs=(HBM, HBM),
        input_output_aliases={0: 0, 1: 1},
        compiler_params=pltpu.CompilerParams(has_side_effects=EFFECT),
    )(cp_thru, land_thru, *sems, after)


def _rs_finish(full, small):
    n = PK_HALF // RS_SPLIT

    def body(full_in_ref, sm_ref, full_ref, all_ref, lsem, ssem, rsem, sm_s, sm_r):
        del full_in_ref
        x, y, c = _mesh_pos()
        me, sibling = (x, y, c), (x, y, 1 - c)
        my_id = 4 * x + 2 * y + c
        own_sm = pltpu.make_async_copy(sm_ref, all_ref.at[my_id], lsem)
        own_sm.start()
        for q in range(RS_SPLIT):
            rows = full_ref.at[c, pl.ds(q * n, n), :]
            pltpu.make_async_remote_copy(src_ref=rows, dst_ref=rows, send_sem=ssem, recv_sem=rsem,
                                         device_id=sibling, device_id_type=MESH).start()
        half = pltpu.make_async_remote_copy(src_ref=full_ref.at[c], dst_ref=full_ref.at[c], send_sem=ssem,
                                            recv_sem=rsem, device_id=me, device_id_type=MESH)
        smalls = [pltpu.make_async_remote_copy(
            src_ref=sm_ref, dst_ref=all_ref.at[my_id], send_sem=sm_s.at[r - 1], recv_sem=sm_r.at[r - 1],
            device_id=(jnp.bitwise_xor(x, r >> 2), jnp.bitwise_xor(y, (r >> 1) & 1), jnp.bitwise_xor(c, r & 1)),
            device_id_type=MESH) for r in range(1, 8)]
        for cpy in smalls:
            cpy.start()
        half.wait_recv()
        half.wait_send()
        for cpy in smalls:
            cpy.wait_recv()
        for cpy in smalls:
            cpy.wait_send()
        own_sm.wait()

    dma7 = pltpu.SemaphoreType.DMA((7,))
    return pl.pallas_call(
        body, name="rs_finish", in_specs=[ANY, ANY], out_specs=[ANY, ANY],
        out_shape=[jax.ShapeDtypeStruct((2, PK_HALF, D), F32), jax.ShapeDtypeStruct((8, SM_ROWS, 128), F32)],
        scratch_shapes=[pltpu.SemaphoreType.DMA(()), pltpu.SemaphoreType.DMA(()), pltpu.SemaphoreType.DMA(()), dma7, dma7],
        input_output_aliases={0: 0},
        compiler_params=pltpu.CompilerParams(has_side_effects=True),
    )(full, small)


def _rope_tables(s):
    pos = jnp.arange(s, dtype=F32) - PAD_FRONT

    def cs(d):
        inv = ROPE_BASE ** (-jnp.arange(0, d, 2, dtype=F32) / d)
        ang = pos[:, None] * inv[None, :]
        return jnp.cos(ang), jnp.sin(ang)

    c, sn = cs(ROPE_D)
    z = jnp.zeros_like(c)
    ktab = (jnp.concatenate([c, c, z, z], axis=1), jnp.concatenate([-sn, z, z, z], axis=1),
            jnp.concatenate([z, sn, z, z], axis=1))
    one = jnp.ones_like(c)
    qtab = tuple(ATT_SCALE * jnp.concatenate(parts, axis=1) for parts in (
        [one, one, one, one, c, c, z, z], [z, z, z, z, -sn, z, z, z], [z, z, z, z, z, sn, z, z]))
    lane = jnp.arange(HD) == DH + ROPE_D
    qbias = lane.astype(F32)[None, :]
    kbias = jnp.where((pos < 0)[:, None] & lane[None, DH:], NEG_INF, 0.0).astype(F32)
    c, sn = cs(DH)
    return qtab, ktab, qbias, kbias, jnp.concatenate([c, c], axis=1), jnp.concatenate([-sn, sn], axis=1)


def _grad_mm(a, b, m, n, s, name):
    return _matmul(a, b, "tn", m, n, s, min(m, 512), min(n, 1024), s, BF16, name, n_outer=True)


def _local_step(x2, target2, hp0, big, small_w, normed=None, gather_wait=None, reduce_start=None, reduce_wait=None):
    norm_w, qn_w, kvn_w, gn_w, gn_b, fin_w = small_w
    s = x2.shape[0] + BLK
    tm = _row_tile(s, 1408)
    qtab, ktab, qbias, kbias, c128, s128 = _rope_tables(s)
    consts, gamma = _ret_consts()

    h, xn, r1 = _rms_in(x2, hp0, norm_w, s) if normed is None else normed
    proj = _matmul(xn, big, "nt", s, PROJ_W, D, tm, 512, D, BF16, "proj", b_off=(BIG_IN // 512, 0))
    if gather_wait is not None:
        big = gather_wait(big, proj)
    uq = big[BIG_UQ:BIG_UQ + 384].reshape(HEADS, DH + ROPE_D, Q_RANK)
    wq = jnp.pad(uq, ((0, 0), (0, HD - DH - ROPE_D), (0, 0))).reshape(HEADS * HD, Q_RANK)
    wukv = big[BIG_UKV:BIG_UKV + 256].reshape(2 * WIDTH, KV_RANK)
    wbm = big[BIG_BM:BIG_BM + 1024].reshape(D, WIDTH)
    wbr = big[BIG_BR:BIG_BR + 1024].reshape(D, WIDTH)
    cqn, rq, ckvn, rkv, qq, kk, vv = _mla_prep(proj, qn_w, kvn_w, wq, wukv, qtab, ktab, qbias, kbias, s)
    o_mla, y_mla, lse = _attn_fwd(qq, kk, vv, proj, s)
    y_ret, on, rstd, qr, kr, rall = _ret_fwd(proj, gn_w, gn_b, c128, s128, consts, gamma, s)
    u_mla, u_ret, merged = _merge_fwd(y_mla, y_ret, wbm, wbr, proj, s)
    h2 = _out_fwd(merged, big, h, s)
    dh2, dh2b, loss_blk, dfin = _loss_bwd(h2, target2, fin_w, s)

    dproj, du_mla, du_ret = _merge_bwd(dh2b, big, u_mla, u_ret, proj, s)
    g_out = _grad_mm(merged, dh2b, D, D, s, "grad_w_out")
    dproj, do_mla, delta = _branch_mla_bwd(dproj, du_mla, wbm, o_mla, proj, s)
    g_bm = _grad_mm(du_mla, y_mla, D, WIDTH, s, "grad_w_branch_mla")
    dproj, do_ret, dgw, dgb = _branch_ret_bwd(dproj, du_ret, wbr, on, rstd, gn_w, gn_b, proj, s)
    g_br = _grad_mm(du_ret, y_ret, D, WIDTH, s, "grad_w_branch_ret")
    dproj = _ret_bwd(dproj, qr, kr, proj, rall, do_ret, c128, s128, consts, gamma, s)
    dqq, dkk, dvv = _attn_bwd(qq, kk, vv, do_mla, lse, delta.reshape(lse.shape), s)
    dproj, dq, dkv, dqnw, dkvnw = _mla_prep_bwd(dproj, dqq, dkk, dvv, proj, rq, rkv, qn_w, kvn_w, wq, wukv,
                                                qtab, ktab, s)
    g_q = _grad_mm(dq, cqn, HEADS * HD, Q_RANK, s, "grad_w_uq")
    g_ukv = _grad_mm(dkv, ckvn, 2 * WIDTH, KV_RANK, s, "grad_w_ukv")
    g_in = _grad_mm(dproj, xn, PROJ_W, D, s, "grad_w_in")

    g_uq = g_q.reshape(HEADS, HD, Q_RANK)[:, :DH + ROPE_D]
    grads = (g_in, g_out, g_bm.reshape(1024, D), g_br.reshape(1024, D), g_uq.reshape(384, D), g_ukv.reshape(256, D))
    token, travelling = (None, grads) if reduce_start is None else reduce_start(grads)
    dxn = _matmul(dproj, big, "nn", s, D, PROJ_W, _row_tile(s, 384), 512, PROJ_W, F32, "dxn",
                  b_off=(BIG_IN // PROJ_W, 0), n_outer=True, after=token)
    grad_x, gmeta_blk, dnorm = _rms_in_bwd(dxn, h, r1, dh2, norm_w, s)
    small = _small_rows((dnorm, dqnw, dkvnw, dgw, dgb, dfin), loss_blk, gmeta_blk[PAD_FRONT:])
    return grad_x, travelling if reduce_wait is None else reduce_wait(travelling, dnorm), small


def _small_rows(ws, first=None, last=None):
    def part(a, rows):
        a = a.reshape(-1, 128)
        return a if a.shape[0] == rows else jnp.pad(a, ((0, rows - a.shape[0]), (0, 0)))

    bounds = (SM_NORM, SM_QN, SM_KVN, SM_GNW, SM_GNB, SM_FIN, SM_META)
    first = jnp.zeros((SM_NORM, 128), F32) if first is None else first
    last = jnp.zeros((SM_ROWS - SM_META, 128), F32) if last is None else last
    return jnp.concatenate([part(first, SM_NORM)] + [part(w, bounds[k + 1] - bounds[k]) for k, w in enumerate(ws)]
                           + [part(last, SM_ROWS - SM_META)], axis=0)


def kernel(x, meta, norm_w, w_in, mla_q_norm_w, mla_w_uq, mla_kv_norm_w, mla_w_ukv, ret_gn_w, ret_gn_b, w_branch_mla, w_branch_ret, w_out, final_norm_w, loss_target, m_meta, m_norm_w, m_w_in, m_mla_q_norm_w, m_mla_w_uq, m_mla_kv_norm_w, m_mla_w_ukv, m_ret_gn_w, m_ret_gn_b, m_w_branch_mla, m_w_branch_ret, m_w_out, m_final_norm_w, v_meta, v_norm_w, v_w_in, v_mla_q_norm_w, v_mla_w_uq, v_mla_kv_norm_w, v_mla_w_ukv, v_ret_gn_w, v_ret_gn_b, v_w_branch_mla, v_w_branch_ret, v_w_out, v_final_norm_w):
    j = 2 * lax.axis_index("x") + lax.axis_index("y")
    tr = lambda w: w[0].T.reshape(-1, D).astype(BF16)
    pos = jnp.stack([j, lax.axis_index("c")]).astype(jnp.int32)
    put = lambda buf, rows, off: lax.dynamic_update_slice(buf, rows, (off, 0))
    big0 = lax.empty((BIG_ROWS, D), BF16)
    big0 = put(big0, w_out[0].astype(BF16), BIG_OUT + 512 * j)
    big0 = put(big0, tr(w_branch_mla), BIG_BM + 256 * j)
    big0 = put(big0, tr(w_branch_ret), BIG_BR + 256 * j)
    big0 = put(big0, tr(mla_w_uq), BIG_UQ + 96 * j)
    big0 = put(big0, tr(mla_w_ukv), BIG_UKV + 64 * j)
    big0 = put(big0, jnp.zeros((AG_JUNK_END - AG_JUNK_REST, D), BF16), AG_JUNK_REST)
    big0 = put(big0, jnp.zeros((ZERO_ROWS, D), BF16), BIG_IN + IN_WIDTH)
    wt = jnp.concatenate([tr(w_in), jnp.zeros((2 * AG_IN_HALF - IN_SHARD, D), BF16)], axis=0)
    big, h, xn, r1 = _allgather_w_in(big0, wt, meta, x[0], norm_w)
    gather_sems, big = _allgather_rest_start(big)

    def gather_wait(big_travelling, after):
        return _allgather_rest_wait(gather_sems, big_travelling, after)

    small_w = (norm_w, mla_q_norm_w, mla_kv_norm_w, ret_gn_w, ret_gn_b, final_norm_w.reshape(1, D))

    def reduce_start(grads):
        chip_part = _rs_core_add(grads, _rs_core_exchange(grads), pos)
        sems, part_thru, land_thru, token = _rs_chip_start(chip_part)
        return token, (sems, part_thru, land_thru)

    def reduce_wait(state, after):
        return _rs_chip_wait(*state, after)

    grad_x, (chip_part, land2), small = _local_step(x[0], loss_target[0], None, big, small_w, (h, xn, r1), gather_wait,
                                                    reduce_start, reduce_wait)
    full, small_all = _rs_finish(_rs_chip_add(chip_part, land2, pos), small)
    full = full.reshape(PK_ROWS, D)

    untr = lambda lo, hi, rows: full[lo:hi].reshape(rows, -1).T
    grads = {
        "w_out": full[PK_OUT:PK_BM], "w_branch_mla": untr(PK_BM, PK_BR, 512), "w_branch_ret": untr(PK_BR, PK_UQ, 512),
        "mla_w_uq": untr(PK_UQ, PK_UKV, 384), "mla_w_ukv": untr(PK_UKV, PK_PAD, 512),
    }
    big_w = {"mla_w_uq": (mla_w_uq, m_mla_w_uq, v_mla_w_uq),
             "mla_w_ukv": (mla_w_ukv, m_mla_w_ukv, v_mla_w_ukv),
             "w_branch_mla": (w_branch_mla, m_w_branch_mla, v_w_branch_mla),
             "w_branch_ret": (w_branch_ret, m_w_branch_ret, v_w_branch_ret), "w_out": (w_out, m_w_out, v_w_out)}
    res = {"w_in": tuple(a.T[None] for a in _adamw_t(w_in[0].T, full, m_w_in[0].T, v_w_in[0].T, "adamw_w_in"))}
    for name, (w, m, v) in big_w.items():
        d, nm, nv = _adamw(w[0], grads[name], m[0], v[0], "adamw_" + name)
        res[name] = (grads[name][None], d[None], nm[None], nv[None])

    small_m = (m_norm_w, m_mla_q_norm_w, m_mla_kv_norm_w, m_ret_gn_w, m_ret_gn_b, m_final_norm_w.reshape(1, D))
    small_v = (v_norm_w, v_mla_q_norm_w, v_mla_kv_norm_w, v_ret_gn_w, v_ret_gn_b, v_final_norm_w.reshape(1, D))
    gs, ds, ms, vs = _adamw_small(_small_rows(small_w), small_all, _small_rows(small_m), _small_rows(small_v))
    names = ["norm_w", "mla_q_norm_w", "mla_kv_norm_w", "ret_gn_w", "ret_gn_b", "final_norm_w"]
    bounds = [SM_NORM, SM_QN, SM_KVN, SM_GNW, SM_GNB, SM_FIN]
    for k, name in enumerate(names):
        shape = (D,) if name == "final_norm_w" else (1, -1)
        rows = small_w[k].size // 128
        res[name] = tuple(a[bounds[k]:bounds[k] + rows].reshape(shape) for a in (gs, ds, ms, vs))
    g_meta = lax.dynamic_slice_in_dim(gs[SM_META:SM_META + 256].reshape(N_META, D), j * (D // N_CHIPS), D // N_CHIPS, axis=1)
    res["meta"] = (g_meta,) + tuple(_adamw(meta, g_meta, m_meta, v_meta, "adamw_meta"))

    order = ["meta", "norm_w", "w_in", "mla_q_norm_w", "mla_w_uq", "mla_kv_norm_w", "mla_w_ukv", "ret_gn_w", "ret_gn_b",
             "w_branch_mla", "w_branch_ret", "w_out", "final_norm_w"]
    return (gs[0, 0], grad_x[None]) + tuple(res[n][k] for k in range(4) for n in order)
```

```python
import functools
import math

import numpy as np
import jax
import jax.numpy as jnp
from jax import lax
from jax.experimental import pallas as pl
from jax.experimental.pallas import tpu as pltpu

F32 = jnp.float32
BF16 = jnp.bfloat16
MESH = pl.DeviceIdType.MESH

D = 2048
N_META = 16
BLK = 128
PAD_FRONT = BLK - N_META
HEADS = 8
DH = 128
ROPE_D = 64
Q_RANK = 512
KV_RANK = 256
WIDTH = HEADS * DH
ROPE_BASE = 10000.0
NORM_EPS = 1e-6
GN_EPS = 1e-5
NEG_INF = -1e30
ATT_SCALE = (DH + ROPE_D) ** -0.5
RET_SCALE = DH ** -0.5
IN_WIDTH = 10048
N_CHIPS = 4
IN_SHARD = IN_WIDTH // N_CHIPS
ADAM_LR, ADAM_B1, ADAM_B2, ADAM_EPS, ADAM_WD, ADAM_STEP = 0.001, 0.9, 0.999, 1e-08, 0.01, 10

R_Q, R_K, R_V, Z_RET, Z_MLA, GATE0, GATE1 = 0, 1024, 2048, 3072, 4096, 5120, 5120 + D
C_Q = 5120 + 2 * D
C_KV = C_Q + Q_RANK
K_PE = C_KV + KV_RANK
PROJ_W = 10240
IN_RUNS = ((0, 832, C_Q), (832, 1856, Z_MLA), (1856, 4928, R_Q), (4928, 5952, Z_RET), (5952, IN_WIDTH, GATE0))

BIG_IN, BIG_OUT, BIG_BM, BIG_BR, BIG_UQ, BIG_UKV, BIG_JUNK, BIG_ROWS = 0, 10240, 12288, 13312, 14336, 14720, 14976, 15360
ZERO_ROWS = PROJ_W - IN_WIDTH
PK_IN, PK_OUT, PK_BM, PK_BR, PK_UQ, PK_UKV, PK_PAD, PK_ROWS = 0, 2512, 3024, 3280, 3536, 3632, 3696, 3712
PK_HALF = PK_ROWS // 2
SM_LOSS, SM_NORM, SM_QN, SM_KVN, SM_GNW, SM_GNB, SM_FIN, SM_META, SM_ROWS = 0, 8, 24, 32, 40, 48, 56, 72, 328

VMEM_LIMIT = 56 * 1024 * 1024


def _pieces(shard):
    j = shard
    out = [(PK_OUT, BIG_OUT + 512 * j, 512), (PK_BM, BIG_BM + 256 * j, 256), (PK_BR, BIG_BR + 256 * j, 256),
           (PK_UQ, BIG_UQ + 96 * j, 96), (PK_UKV, BIG_UKV + 64 * j, 64)]
    for lo, hi, new in IN_RUNS:
        a, b = max(lo, IN_SHARD * j), min(hi, IN_SHARD * (j + 1))
        if b > a:
            out.append((PK_IN + a - IN_SHARD * j, BIG_IN + new + a - lo, b - a))
    out += [(PK_PAD, BIG_JUNK + 16 * j, 16)]
    return out


AG_JUNK_REST, AG_JUNK_IN, AG_JUNK_END = BIG_JUNK, BIG_JUNK + 128, BIG_JUNK + 192
AG_REST_HALF, AG_IN_HALF = 608, 1264


def _ag_half(shard, w_in, half):
    pieces = [(b, n) for p, b, n in _pieces(shard) if p < PK_PAD and (p < PK_OUT) == w_in]
    pieces.append((AG_JUNK_IN + 16 * shard, 16) if w_in else (AG_JUNK_REST + 32 * shard, 32))
    size = AG_IN_HALF if w_in else AG_REST_HALF
    out, pos = [], 0
    for b, n in pieces:
        s, e = max(pos, half * size), min(pos + n, (half + 1) * size)
        if e > s:
            out.append((s, b + s - pos, e - s))
        pos += n
    assert pos == 2 * size
    return out


def _grad_half(shard, half):
    j = shard
    pieces = [(0, b - BIG_IN, p, n) for p, b, n in _pieces(j) if p < PK_OUT]
    pieces += [(1, 512 * j, PK_OUT, 512), (2, 256 * j, PK_BM, 256), (3, 256 * j, PK_BR, 256), (4, 96 * j, PK_UQ, 96),
               (5, 64 * j, PK_UKV, 64), (0, IN_WIDTH, PK_PAD, PK_ROWS - PK_PAD)]
    lo, hi = half * PK_HALF, (half + 1) * PK_HALF
    out = []
    for k, r0, p, n in pieces:
        s, e = max(p, lo), min(p + n, hi)
        if e > s:
            out.append((k, r0 + s - p, s - lo, e - s))
    assert sum(n for _, _, _, n in out) == PK_HALF
    return out


def _row_tile(rows, cap):
    best = BLK
    for t in range(BLK, cap + 1, BLK):
        if rows % t == 0:
            best = t
    return best


def _cparams(sem):
    return pltpu.CompilerParams(dimension_semantics=sem, vmem_limit_bytes=VMEM_LIMIT)


def _dot(a, b, form):
    dn = {"nt": (((1,), (1,)), ((), ())), "nn": (((1,), (0,)), ((), ())), "tn": (((0,), (0,)), ((), ()))}[form]
    return lax.dot_general(a, b, dn, preferred_element_type=F32)


def _sigmoid(v):
    return 1.0 / (1.0 + jnp.exp(-v))


def _matmul(a, b, form, m, n, k, tm, tn, tk, out_dtype, name, a_off=(0, 0), b_off=(0, 0), n_outer=False, after=None):
    nk = k // tk
    gi, gj = m // tm, n // tn

    def ij(g0, g1):
        return (g1, g0) if n_outer else (g0, g1)

    if form == "nt":
        a_spec = pl.BlockSpec((tm, tk), lambda g0, g1, kk: (ij(g0, g1)[0] + a_off[0], kk + a_off[1]))
        b_spec = pl.BlockSpec((tn, tk), lambda g0, g1, kk: (ij(g0, g1)[1] + b_off[0], kk + b_off[1]))
    elif form == "nn":
        a_spec = pl.BlockSpec((tm, tk), lambda g0, g1, kk: (ij(g0, g1)[0] + a_off[0], kk + a_off[1]))
        b_spec = pl.BlockSpec((tk, tn), lambda g0, g1, kk: (kk + b_off[0], ij(g0, g1)[1] + b_off[1]))
    else:
        a_spec = pl.BlockSpec((tk, tm), lambda g0, g1, kk: (kk + a_off[0], ij(g0, g1)[0] + a_off[1]))
        b_spec = pl.BlockSpec((tk, tn), lambda g0, g1, kk: (kk + b_off[0], ij(g0, g1)[1] + b_off[1]))
    o_spec = pl.BlockSpec((tm, tn), lambda g0, g1, kk: ij(g0, g1))

    def body(a_ref, b_ref, *rest):
        o_ref, *acc = rest[0 if after is None else 1:]
        p = _dot(a_ref[...], b_ref[...], form)
        if nk == 1:
            o_ref[...] = p.astype(o_ref.dtype)
        else:
            acc_ref, = acc
            kk = pl.program_id(2)

            @pl.when(kk == 0)
            def _():
                acc_ref[...] = p

            @pl.when(kk > 0)
            def _():
                acc_ref[...] += p

            @pl.when(kk == nk - 1)
            def _():
                o_ref[...] = acc_ref[...].astype(o_ref.dtype)

    extra = [] if after is None else [after]
    return pl.pallas_call(
        body, name=name, grid=(gj, gi, nk) if n_outer else (gi, gj, nk),
        in_specs=[a_spec, b_spec] + [pl.BlockSpec(memory_space=pl.ANY)] * len(extra), out_specs=o_spec,
        out_shape=jax.ShapeDtypeStruct((m, n), out_dtype),
        scratch_shapes=[] if nk == 1 else [pltpu.VMEM((tm, tn), F32)],
        compiler_params=_cparams(("parallel", "parallel", "arbitrary")),
    )(a, b, *extra)


def _rms_in(x, hp0, norm_w, s):
    def body(x_ref, hp0_ref, w_ref, h_ref, xn_ref, r_ref):
        def run(hv):
            r = lax.rsqrt(jnp.mean(hv * hv, axis=-1, keepdims=True) + NORM_EPS)
            h_ref[...] = hv
            xn_ref[...] = (hv * r * w_ref[...]).astype(BF16)
            r_ref[...] = r

        @pl.when(pl.program_id(0) == 0)
        def _():
            run(hp0_ref[...])

        @pl.when(pl.program_id(0) > 0)
        def _():
            run(x_ref[...])

    return pl.pallas_call(
        body, name="rms_in", grid=(s // BLK,),
        in_specs=[pl.BlockSpec((BLK, D), lambda i: (jnp.maximum(i - 1, 0), 0)),
                  pl.BlockSpec((BLK, D), lambda i: (0, 0)), pl.BlockSpec((1, D), lambda i: (0, 0))],
        out_specs=[pl.BlockSpec((BLK, D), lambda i: (i, 0)), pl.BlockSpec((BLK, D), lambda i: (i, 0)),
                   pl.BlockSpec((BLK, 1), lambda i: (i, 0))],
        out_shape=[jax.ShapeDtypeStruct((s, D), F32), jax.ShapeDtypeStruct((s, D), BF16),
                   jax.ShapeDtypeStruct((s, 1), F32)],
        compiler_params=_cparams(("arbitrary",)),
    )(x, hp0, norm_w)


def _rope64(t, c, sa, sb):
    return t * c + pltpu.roll(t, t.shape[1] - ROPE_D // 2, 1) * sa + pltpu.roll(t, ROPE_D // 2, 1) * sb


def _rope128(t, c, sg):
    return t * c + pltpu.roll(t, DH // 2, 1) * sg


HD = 2 * DH


def _mla_prep(proj, qn_w, kvn_w, wq, wukv, qtab, ktab, qbias, kbias, s):
    tm = _row_tile(s, 384)

    def body(cq_ref, ckv_ref, kpe_ref, qnw_ref, kvnw_ref, wq_ref, wukv_ref, qc_ref, qa_ref, qb_ref,
             kc_ref, ka_ref, kb_ref, qbias_ref, kbias_ref, cqn_ref, rq_ref, ckvn_ref, rkv_ref, qq_ref, kk_ref, vv_ref):
        cq = cq_ref[...].astype(F32)
        rq = lax.rsqrt(jnp.mean(cq * cq, axis=-1, keepdims=True) + NORM_EPS)
        cqn = (cq * rq * qnw_ref[...]).astype(BF16)
        ckv = ckv_ref[...].astype(F32)
        rkv = lax.rsqrt(jnp.mean(ckv * ckv, axis=-1, keepdims=True) + NORM_EPS)
        ckvn = (ckv * rkv * kvnw_ref[...]).astype(BF16)
        cqn_ref[...] = cqn
        rq_ref[...] = rq
        ckvn_ref[...] = ckvn
        rkv_ref[...] = rkv
        q = _dot(cqn, wq_ref[...], "nt")
        kv = _dot(ckvn, wukv_ref[...], "nt")
        kp = (_rope64(kpe_ref[...].astype(F32), kc_ref[...], ka_ref[...], kb_ref[...]) + kbias_ref[...]).astype(BF16)
        qc, qa, qb, qbias = qc_ref[...], qa_ref[...], qb_ref[...], qbias_ref[...]
        ones = jnp.ones((tm, DH), BF16)
        for h in range(HEADS):
            lo, mid, hi = h * HD, h * HD + DH, (h + 1) * HD
            qq_ref[:, lo:hi] = (_rope64(q[:, lo:hi], qc, qa, qb) + qbias).astype(BF16)
            kk_ref[:, lo:mid] = kv[:, lo:mid].astype(BF16)
            kk_ref[:, mid:hi] = kp
            vv_ref[:, lo:mid] = kv[:, mid:hi].astype(BF16)
            vv_ref[:, mid:hi] = ones

    row = lambda w, cb: pl.BlockSpec((tm, w), lambda i: (i, cb))
    full = lambda a: pl.BlockSpec(a.shape, lambda i: (0, 0))
    wide = jax.ShapeDtypeStruct((s, HEADS * HD), BF16)
    return pl.pallas_call(
        body, name="mla_prep", grid=(s // tm,),
        in_specs=[row(Q_RANK, C_Q // Q_RANK), row(KV_RANK, C_KV // KV_RANK), row(DH, K_PE // DH),
                  full(qn_w), full(kvn_w), full(wq), full(wukv), row(HD, 0), row(HD, 0), row(HD, 0),
                  row(DH, 0), row(DH, 0), row(DH, 0), full(qbias), row(DH, 0)],
        out_specs=[row(Q_RANK, 0), row(1, 0), row(KV_RANK, 0), row(1, 0), row(HEADS * HD, 0), row(HEADS * HD, 0),
                   row(HEADS * HD, 0)],
        out_shape=[jax.ShapeDtypeStruct((s, Q_RANK), BF16), jax.ShapeDtypeStruct((s, 1), F32),
                   jax.ShapeDtypeStruct((s, KV_RANK), BF16), jax.ShapeDtypeStruct((s, 1), F32), wide, wide, wide],
        compiler_params=_cparams(("parallel",)),
    )(proj, proj, proj, qn_w, kvn_w, wq, wukv, *qtab, *ktab, qbias, kbias)


def _diag_mask(t):
    return lax.broadcasted_iota(jnp.int32, (t, t), 0) <= lax.broadcasted_iota(jnp.int32, (t, t), 1)


def _silu(z):
    return z * _sigmoid(z)


def _attn_fwd(qq, kk, vv, proj, s):
    t = _row_tile(s, 384)
    n = s // t

    def body(q_ref, k_ref, v_ref, z_ref, o_ref, y_ref, lse_ref, acc_ref, m_ref):
        qi = pl.program_id(1)
        q = q_ref[...]
        m_ref[...] = jnp.full(m_ref.shape, NEG_INF, F32)
        acc_ref[...] = jnp.zeros(acc_ref.shape, F32)

        def keys(ki):
            return pl.ds(pl.multiple_of(ki * t, t), t)

        def scores(ki):
            return _dot(k_ref[keys(ki), :], q, "nt")

        def tile(ki, st):
            m_old = m_ref[...]
            m_new = jnp.maximum(m_old, jnp.max(st, axis=0, keepdims=True))
            pt = jnp.exp(st - m_new).astype(BF16)
            acc_ref[...] = jnp.exp(m_old - m_new) * acc_ref[...] + _dot(v_ref[keys(ki), :], pt, "tn")
            m_ref[...] = m_new

        def inner(ki, st):
            st_after = scores(ki + 1)
            tile(ki, st)
            return st_after

        st_diag = lax.fori_loop(0, qi, inner, scores(0))
        tile(qi, jnp.where(_diag_mask(t), st_diag, NEG_INF))

        l = acc_ref[DH:DH + 1, :]
        o = (acc_ref[:DH, :] / l).T
        o_ref[...] = o.astype(BF16)
        y_ref[...] = (o * _silu(z_ref[...].astype(F32))).astype(BF16)
        lse_ref[0, 0] = m_ref[...] + jnp.log(l)

    qtile = pl.BlockSpec((t, DH), lambda h, i: (i, h))
    head = pl.BlockSpec((s, HD), lambda h, i: (0, h))
    return pl.pallas_call(
        body, name="attn_fwd", grid=(HEADS, s // t),
        in_specs=[pl.BlockSpec((t, HD), lambda h, i: (i, h)), head, head,
                  pl.BlockSpec((t, DH), lambda h, i: (i, Z_MLA // DH + h))],
        out_specs=[qtile, qtile, pl.BlockSpec((1, 1, 1, t), lambda h, i: (h, i, 0, 0))],
        out_shape=[jax.ShapeDtypeStruct((s, WIDTH), BF16), jax.ShapeDtypeStruct((s, WIDTH), BF16),
                   jax.ShapeDtypeStruct((HEADS, s // t, 1, t), F32)],
        scratch_shapes=[pltpu.VMEM((HD, t), F32), pltpu.VMEM((1, t), F32)],
        compiler_params=_cparams(("parallel", "arbitrary")),
    )(qq, kk, vv, proj)


RET_CHUNKS = 3


def _ret_consts():
    log_g = np.log1p(-(2.0 ** (-5.0 - np.arange(HEADS, dtype=np.float64))))
    n = np.arange(BLK, dtype=np.float64)
    diff = n[:, None] - n[None, :]
    decay = np.where(diff >= 0, np.exp(log_g[:, None, None] * np.maximum(diff, 0.0)), 0.0)
    zeta = np.exp(log_g[:, None] * (BLK - 1.0 - n))[:, :, None]
    xi = np.exp(log_g[:, None] * (n + 1.0))[:, :, None]
    gamma = [float(np.float32(np.exp(g * BLK))) for g in log_g]
    wide = lambda a: np.broadcast_to(a, (HEADS, BLK, DH))
    return ((jnp.asarray(decay, F32), jnp.asarray(wide(zeta), F32), jnp.asarray(wide(xi), F32)),
            (jnp.asarray(decay, F32), jnp.asarray(zeta, F32), jnp.asarray(xi, F32)), gamma)


def _ret_fwd(proj, gn_w, gn_b, c128, s128, consts, gamma, s):
    nb = s // BLK
    cps = RET_CHUNKS if nb % RET_CHUNKS == 0 else 1
    decay, zeta, xi = consts

    def body(rq_ref, rk_ref, rv_ref, z_ref, gw_ref, gb_ref, c_ref, s_ref, dm_ref, zt_ref, xi_ref,
             y_ref, on_ref, rstd_ref, qr_ref, kr_ref, rall_ref, state):
        @pl.when(pl.program_id(0) == 0)
        def _():
            state[...] = jnp.zeros_like(state)

        for sub in range(cps):
            rows = slice(sub * BLK, (sub + 1) * BLK)
            c, sg = c_ref[rows, :], s_ref[rows, :]
            for h in range(HEADS):
                sl = slice(h * DH, (h + 1) * DH)
                q = _rope128(rq_ref[rows, sl].astype(F32), c, sg).astype(BF16)
                kf = _rope128(rk_ref[rows, sl].astype(F32), c, sg) * RET_SCALE
                k = kf.astype(BF16)
                v = rv_ref[rows, sl]
                qr_ref[rows, sl] = q
                kr_ref[rows, sl] = k
                r_prev = state[h]
                rall_ref[sub, h] = r_prev
                a = _dot(q, k, "nt") * dm_ref[h]
                o = _dot(a.astype(BF16), v, "nn") + _dot(q, r_prev.astype(BF16), "nn") * xi_ref[h]
                state[h] = r_prev * gamma[h] + _dot((kf * zt_ref[h]).astype(BF16), v, "tn")
                mu = jnp.mean(o, axis=-1, keepdims=True)
                var = jnp.mean(jnp.square(o - mu), axis=-1, keepdims=True)
                rstd = lax.rsqrt(var + GN_EPS)
                on = (o - mu) * rstd
                rstd_ref[h, rows, :] = rstd
                on_ref[rows, sl] = on.astype(BF16)
                ogn = on * gw_ref[:, sl] + gb_ref[:, sl]
                y_ref[rows, sl] = (ogn * _silu(z_ref[rows, sl].astype(F32))).astype(BF16)

    seg = lambda cb: pl.BlockSpec((cps * BLK, WIDTH), lambda i: (i, cb))
    full = lambda a: pl.BlockSpec(a.shape, lambda i: (0,) * a.ndim)
    tab = pl.BlockSpec((cps * BLK, DH), lambda i: (i, 0))
    return pl.pallas_call(
        body, name="ret_fwd", grid=(nb // cps,),
        in_specs=[seg(R_Q // WIDTH), seg(R_K // WIDTH), seg(R_V // WIDTH), seg(Z_RET // WIDTH), full(gn_w), full(gn_b),
                  tab, tab, full(decay), full(zeta), full(xi)],
        out_specs=[seg(0), seg(0), pl.BlockSpec((HEADS, cps * BLK, 1), lambda i: (0, i, 0)), seg(0), seg(0),
                   pl.BlockSpec((cps, HEADS, DH, DH), lambda i: (i, 0, 0, 0))],
        out_shape=[jax.ShapeDtypeStruct((s, WIDTH), BF16), jax.ShapeDtypeStruct((s, WIDTH), BF16),
                   jax.ShapeDtypeStruct((HEADS, s, 1), F32), jax.ShapeDtypeStruct((s, WIDTH), BF16),
                   jax.ShapeDtypeStruct((s, WIDTH), BF16), jax.ShapeDtypeStruct((nb, HEADS, DH, DH), F32)],
        scratch_shapes=[pltpu.VMEM((HEADS, DH, DH), F32)],
        compiler_params=_cparams(("arbitrary",)),
    )(proj, proj, proj, proj, gn_w, gn_b, c128, s128, decay, zeta, xi)


def _merge_fwd(y_mla, y_ret, wbm, wbr, proj, s):
    tm, tn = _row_tile(s, 1408), 512

    def body(ym_ref, yr_ref, wm_ref, wr_ref, g0_ref, g1_ref, um_ref, ur_ref, mg_ref):
        um = _dot(ym_ref[...], wm_ref[...], "nt")
        ur = _dot(yr_ref[...], wr_ref[...], "nt")
        um_ref[...] = um.astype(BF16)
        ur_ref[...] = ur.astype(BF16)
        mg_ref[...] = (_sigmoid(g0_ref[...].astype(F32)) * um + _sigmoid(g1_ref[...].astype(F32)) * ur).astype(BF16)

    yspec = pl.BlockSpec((tm, WIDTH), lambda i, j: (i, 0))
    wspec = pl.BlockSpec((tn, WIDTH), lambda i, j: (j, 0))
    ospec = pl.BlockSpec((tm, tn), lambda i, j: (i, j))
    return pl.pallas_call(
        body, name="merge_fwd", grid=(s // tm, D // tn),
        in_specs=[yspec, yspec, wspec, wspec, pl.BlockSpec((tm, tn), lambda i, j: (i, GATE0 // tn + j)),
                  pl.BlockSpec((tm, tn), lambda i, j: (i, GATE1 // tn + j))],
        out_specs=[ospec, ospec, ospec],
        out_shape=[jax.ShapeDtypeStruct((s, D), BF16)] * 3,
        compiler_params=_cparams(("parallel", "parallel")),
    )(y_mla, y_ret, wbm, wbr, proj, proj)


def _out_fwd(merged, big, h, s):
    tm, tn = _row_tile(s, 1408), 512

    def body(m_ref, w_ref, h_ref, o_ref):
        o_ref[...] = h_ref[...] + _dot(m_ref[...], w_ref[...], "nn")

    return pl.pallas_call(
        body, name="out_fwd", grid=(s // tm, D // tn),
        in_specs=[pl.BlockSpec((tm, D), lambda i, j: (i, 0)), pl.BlockSpec((D, tn), lambda i, j: (BIG_OUT // D, j)),
                  pl.BlockSpec((tm, tn), lambda i, j: (i, j))],
        out_specs=pl.BlockSpec((tm, tn), lambda i, j: (i, j)),
        out_shape=jax.ShapeDtypeStruct((s, D), F32),
        compiler_params=_cparams(("parallel", "parallel")),
    )(merged, big, h)


def _loss_bwd(h2, target, fin_w, s):
    nb = s // BLK

    def body(h2_ref, t_ref, w_ref, dh_ref, dhb_ref, loss_ref, dw_ref):
        i = pl.program_id(0)

        @pl.when(i == 0)
        def _():
            dh_ref[...] = jnp.zeros_like(dh_ref)
            dhb_ref[...] = jnp.zeros_like(dhb_ref)
            loss_ref[...] = jnp.zeros_like(loss_ref)
            dw_ref[...] = jnp.zeros_like(dw_ref)

        @pl.when(i > 0)
        def _():
            hv = h2_ref[...]
            w = w_ref[...]
            r = lax.rsqrt(jnp.mean(hv * hv, axis=-1, keepdims=True) + NORM_EPS)
            nrm = hv * r
            e = nrm * w - t_ref[...]
            loss_ref[...] += jnp.full(loss_ref.shape, 0.5 / D, F32) * jnp.sum(e * e)
            dy = e * (1.0 / D)
            dw_ref[...] += jnp.sum(dy * nrm, axis=0, keepdims=True)
            g = dy * w
            dh = r * (g - nrm * jnp.mean(g * nrm, axis=-1, keepdims=True))
            dh_ref[...] = dh
            dhb_ref[...] = dh.astype(BF16)

    blk = pl.BlockSpec((BLK, D), lambda i: (i, 0))
    return pl.pallas_call(
        body, name="loss_bwd", grid=(nb,),
        in_specs=[blk, pl.BlockSpec((BLK, D), lambda i: (jnp.maximum(i - 1, 0), 0)), pl.BlockSpec((1, D), lambda i: (0, 0))],
        out_specs=[blk, blk, pl.BlockSpec((8, 128), lambda i: (0, 0)), pl.BlockSpec((1, D), lambda i: (0, 0))],
        out_shape=[jax.ShapeDtypeStruct((s, D), F32), jax.ShapeDtypeStruct((s, D), BF16),
                   jax.ShapeDtypeStruct((8, 128), F32), jax.ShapeDtypeStruct((1, D), F32)],
        compiler_params=_cparams(("arbitrary",)),
    )(h2, target, fin_w)


def _merge_bwd(dh2b, big, u_mla, u_ret, proj, s):
    tm, tn = _row_tile(s, 1408), 512

    def body(d_ref, w_ref, um_ref, ur_ref, gate_ref, dproj_ref, dum_ref, dur_ref, dm_ref):
        branch, j = pl.program_id(1), pl.program_id(2)
        cols = pl.ds(pl.multiple_of(j * tn, tn), tn)

        @pl.when(branch == 0)
        def _():
            dm_ref[:, cols] = _dot(d_ref[...], w_ref[...], "nt")

        dm = dm_ref[:, cols]
        gt = _sigmoid(gate_ref[...].astype(F32))

        @pl.when(branch == 0)
        def _():
            dproj_ref[...] = (dm * um_ref[...].astype(F32) * gt * (1.0 - gt)).astype(BF16)
            dum_ref[...] = (dm * gt).astype(BF16)

        @pl.when(branch == 1)
        def _():
            dproj_ref[...] = (dm * ur_ref[...].astype(F32) * gt * (1.0 - gt)).astype(BF16)
            dur_ref[...] = (dm * gt).astype(BF16)

    last = D // tn - 1
    mla = pl.BlockSpec((tm, tn), lambda i, b, j: (i, jnp.where(b == 0, j, last)))
    ret = pl.BlockSpec((tm, tn), lambda i, b, j: (i, jnp.where(b == 0, 0, j)))
    gate = pl.BlockSpec((tm, tn), lambda i, b, j: (i, GATE0 // tn + b * (D // tn) + j))
    return pl.pallas_call(
        body, name="merge_bwd", grid=(s // tm, 2, D // tn),
        in_specs=[pl.BlockSpec((tm, D), lambda i, b, j: (i, 0)),
                  pl.BlockSpec((tn, D), lambda i, b, j: (BIG_OUT // tn + jnp.where(b == 0, j, last), 0)), mla, ret, gate],
        out_specs=[gate, mla, ret],
        out_shape=[jax.ShapeDtypeStruct((s, PROJ_W), BF16), jax.ShapeDtypeStruct((s, D), BF16),
                   jax.ShapeDtypeStruct((s, D), BF16)],
        scratch_shapes=[pltpu.VMEM((tm, D), F32)],
        compiler_params=_cparams(("parallel", "arbitrary", "arbitrary")),
    )(dh2b, big, u_mla, u_ret, proj)


def _dsilu(z):
    sg = _sigmoid(z)
    return sg * (1.0 + z * (1.0 - sg))


def _branch_mla_bwd(dproj, du, wbm, o_mla, proj, s):
    tm = _row_tile(s, 384)

    def body(dproj_in, du_ref, w_ref, o_ref, z_ref, dz_ref, do_ref, delta_ref):
        del dproj_in
        dy = _dot(du_ref[...], w_ref[...], "nn")
        z = z_ref[...].astype(F32)
        o = o_ref[...].astype(F32)
        do = dy * _silu(z)
        do_ref[...] = do.astype(BF16)
        dz_ref[...] = (dy * o * _dsilu(z)).astype(BF16)
        prod = do * o
        for h in range(HEADS):
            delta_ref[h] = jnp.sum(prod[:, h * DH:(h + 1) * DH], axis=-1, keepdims=True)

    row = lambda w, cb: pl.BlockSpec((tm, w), lambda i: (i, cb))
    return pl.pallas_call(
        body, name="branch_mla_bwd", grid=(s // tm,),
        in_specs=[ANY, row(D, 0), pl.BlockSpec((D, WIDTH), lambda i: (0, 0)), row(WIDTH, 0),
                  row(WIDTH, Z_MLA // WIDTH)],
        out_specs=[row(WIDTH, Z_MLA // WIDTH), row(WIDTH, 0), pl.BlockSpec((HEADS, tm, 1), lambda i: (0, i, 0))],
        out_shape=[jax.ShapeDtypeStruct((s, PROJ_W), BF16), jax.ShapeDtypeStruct((s, WIDTH), BF16),
                   jax.ShapeDtypeStruct((HEADS, s, 1), F32)],
        input_output_aliases={0: 0},
        compiler_params=_cparams(("parallel",)),
    )(dproj, du, wbm, o_mla, proj)


def _branch_ret_bwd(dproj, du, wbr, on, rstd, gn_w, gn_b, proj, s):
    tm = _row_tile(s, 384)

    def body(dproj_in, du_ref, w_ref, on_ref, rstd_ref, gw_ref, gb_ref, z_ref, dz_ref, do_ref, dgw_ref, dgb_ref):
        del dproj_in

        @pl.when(pl.program_id(0) == 0)
        def _():
            dgw_ref[...] = jnp.zeros_like(dgw_ref)
            dgb_ref[...] = jnp.zeros_like(dgb_ref)

        dy = _dot(du_ref[...], w_ref[...], "nn")
        z = z_ref[...].astype(F32)
        on = on_ref[...].astype(F32)
        gw = gw_ref[...]
        dogn = dy * _silu(z)
        dz_ref[...] = (dy * (on * gw + gb_ref[...]) * _dsilu(z)).astype(BF16)
        dgw_ref[...] += jnp.sum(dogn * on, axis=0, keepdims=True)
        dgb_ref[...] += jnp.sum(dogn, axis=0, keepdims=True)
        don = dogn * gw
        for h in range(HEADS):
            sl = slice(h * DH, (h + 1) * DH)
            dn, nh = don[:, sl], on[:, sl]
            do = rstd_ref[h] * (dn - jnp.mean(dn, axis=-1, keepdims=True)
                                - nh * jnp.mean(dn * nh, axis=-1, keepdims=True))
            do_ref[:, sl] = do.astype(BF16)

    row = lambda w, cb: pl.BlockSpec((tm, w), lambda i: (i, cb))
    vec = pl.BlockSpec((1, WIDTH), lambda i: (0, 0))
    return pl.pallas_call(
        body, name="branch_ret_bwd", grid=(s // tm,),
        in_specs=[ANY, row(D, 0), pl.BlockSpec((D, WIDTH), lambda i: (0, 0)), row(WIDTH, 0),
                  pl.BlockSpec((HEADS, tm, 1), lambda i: (0, i, 0)), vec, vec, row(WIDTH, Z_RET // WIDTH)],
        out_specs=[row(WIDTH, Z_RET // WIDTH), row(WIDTH, 0), vec, vec],
        out_shape=[jax.ShapeDtypeStruct((s, PROJ_W), BF16), jax.ShapeDtypeStruct((s, WIDTH), BF16)]
        + [jax.ShapeDtypeStruct((1, WIDTH), F32)] * 2,
        input_output_aliases={0: 0},
        compiler_params=_cparams(("arbitrary",)),
    )(dproj, du, wbr, on, rstd, gn_w, gn_b, proj)


def _ret_bwd(dproj, qr, kr, proj, rall, do_ret, c128, s128, consts, gamma, s):
    cps = RET_CHUNKS if (s // BLK) % RET_CHUNKS == 0 else 1
    nb = s // (cps * BLK)
    decay, zeta, xi = consts

    def body(dproj_in, q_ref, k_ref, v_ref, r_ref, do_ref, c_ref, s_ref, dm_ref, zt_ref, xi_ref, out_ref, gstate):
        del dproj_in
        dq_ref, dk_ref, dv_ref = (out_ref.at[:, pl.ds(off, WIDTH)] for off in (R_Q, R_K, R_V))

        @pl.when(pl.program_id(0) == 0)
        def _():
            gstate[...] = jnp.zeros_like(gstate)

        for sub in reversed(range(cps)):
            rows = slice(sub * BLK, (sub + 1) * BLK)
            c, sg = c_ref[rows, :], s_ref[rows, :]
            for h in range(HEADS):
                sl = slice(h * DH, (h + 1) * DH)
                q, k, v, do = q_ref[rows, sl], k_ref[rows, sl], v_ref[rows, sl], do_ref[rows, sl]
                dm = dm_ref[h]
                g_next = gstate[h]
                gb = g_next.astype(BF16)
                a = (_dot(q, k, "nt") * dm).astype(BF16)
                da = (_dot(do, v, "nt") * dm).astype(BF16)
                dox = (do.astype(F32) * xi_ref[h]).astype(BF16)
                dq = _dot(da, k, "nn") + _dot(dox, r_ref[sub, h].astype(BF16), "nt")
                dk = _dot(da, q, "tn") + _dot(v, gb, "nt") * zt_ref[h]
                kz = (k.astype(F32) * zt_ref[h]).astype(BF16)
                dv = _dot(a, do, "tn") + _dot(kz, gb, "nn")
                gstate[h] = g_next * gamma[h] + _dot(q, dox, "tn")
                dk = dk * RET_SCALE
                dq_ref[rows, sl] = _rope128(dq, c, -sg).astype(BF16)
                dk_ref[rows, sl] = _rope128(dk, c, -sg).astype(BF16)
                dv_ref[rows, sl] = dv.astype(BF16)

    rev = lambda cb: pl.BlockSpec((cps * BLK, WIDTH), lambda i: (nb - 1 - i, cb))
    full = lambda a: pl.BlockSpec(a.shape, lambda i: (0,) * a.ndim)
    tab = pl.BlockSpec((cps * BLK, DH), lambda i: (nb - 1 - i, 0))
    return pl.pallas_call(
        body, name="ret_bwd", grid=(nb,),
        in_specs=[ANY, rev(0), rev(0), rev(R_V // WIDTH),
                  pl.BlockSpec((cps, HEADS, DH, DH), lambda i: (nb - 1 - i, 0, 0, 0)),
                  rev(0), tab, tab, full(decay), full(zeta), full(xi)],
        out_specs=pl.BlockSpec((cps * BLK, 3 * WIDTH), lambda i: (nb - 1 - i, R_Q // (3 * WIDTH))),
        out_shape=jax.ShapeDtypeStruct((s, PROJ_W), BF16),
        scratch_shapes=[pltpu.VMEM((HEADS, DH, DH), F32)],
        input_output_aliases={0: 0},
        compiler_params=_cparams(("arbitrary",)),
    )(dproj, qr, kr, proj, rall, do_ret, c128, s128, decay, zeta, xi)


def _attn_bwd(qq, kk, vv, do, lse, delta, s):
    t = _row_tile(s, 384)
    n = s // t

    def body(q_ref, k_ref, v_ref, do_ref, lse_ref, delta_ref, dq_ref, dk_ref, dv_ref, dk_acc, dv_acc):
        ki = pl.program_id(1)

        @pl.when(ki == 0)
        def _():
            dq_ref[...] = jnp.zeros(dq_ref.shape, F32)

        k, v = k_ref[...], v_ref[...]
        dk_acc[...] = jnp.zeros(dk_acc.shape, F32)
        dv_acc[...] = jnp.zeros(dv_acc.shape, F32)

        def rows(qi):
            return pl.ds(pl.multiple_of(qi * t, t), t)

        def products(qi):
            return _dot(k, q_ref[rows(qi), :], "nt"), _dot(v, do_ref[rows(qi), :], "nt")

        def tile(qi, st, dpt):
            q, dov = q_ref[rows(qi), :], do_ref[rows(qi), :]
            pt = jnp.exp(st - lse_ref[0, qi])
            dv_acc[...] += _dot(pt.astype(BF16), dov, "nn")
            dst = (pt * (dpt - delta_ref[0, qi])).astype(BF16)
            dk_acc[...] += _dot(dst, q, "nn")
            dq_ref[rows(qi), :] += _dot(dst, k, "tn")

        def inner(qi, carry):
            after = products(jnp.minimum(qi + 1, n - 1))
            tile(qi, *carry)
            return after

        st, dpt = products(ki)
        ahead = products(jnp.minimum(ki + 1, n - 1))
        tile(ki, jnp.where(_diag_mask(t), st, NEG_INF), dpt)
        lax.fori_loop(ki + 1, n, inner, ahead)

        dk_ref[...] = dk_acc[...].astype(BF16)
        dv_ref[...] = dv_acc[...].astype(BF16)

    head = lambda w: pl.BlockSpec((s, w), lambda h, i: (0, h))
    stat = pl.BlockSpec((1, n, 1, t), lambda h, i: (h, 0, 0, 0))
    return pl.pallas_call(
        body, name="attn_bwd", grid=(HEADS, n),
        in_specs=[head(HD), pl.BlockSpec((t, HD), lambda h, i: (i, h)), pl.BlockSpec((t, DH), lambda h, i: (i, 2 * h)),
                  head(DH), stat, stat],
        out_specs=[head(HD), pl.BlockSpec((t, HD), lambda h, i: (i, h)), pl.BlockSpec((t, DH), lambda h, i: (i, h))],
        out_shape=[jax.ShapeDtypeStruct((s, HEADS * HD), F32), jax.ShapeDtypeStruct((s, HEADS * HD), BF16),
                   jax.ShapeDtypeStruct((s, WIDTH), BF16)],
        scratch_shapes=[pltpu.VMEM((t, HD), F32), pltpu.VMEM((t, DH), F32)],
        compiler_params=_cparams(("parallel", "arbitrary")),
    )(qq, kk, vv, do, lse, delta)


def _rms_bwd(dn, nhat, r, w):
    g = dn * w
    return r * (g - nhat * jnp.mean(g * nhat, axis=-1, keepdims=True)), dn * nhat


def _mla_prep_bwd(dproj, dqq, dkk, dvv, proj, rq, rkv, qn_w, kvn_w, wq, wukv, qtab, ktab, s):
    tm = _row_tile(s, 384)
    tail = PROJ_W - C_Q

    def body(dproj_in, dqq_ref, dkk_ref, dvv_ref, cq_ref, ckv_ref, rq_ref, rkv_ref, qnw_ref, kvnw_ref, wq_ref,
             wukv_ref, qc_ref, qa_ref, qb_ref, kc_ref, ka_ref, kb_ref,
             tail_ref, dq_ref, dkv_ref, dqnw_ref, dkvnw_ref):
        del dproj_in
        dcq_ref = tail_ref.at[:, pl.ds(0, Q_RANK)]
        dckv_ref = tail_ref.at[:, pl.ds(C_KV - C_Q, KV_RANK)]
        dkpe_ref = tail_ref.at[:, pl.ds(K_PE - C_Q, 2 * DH)]

        @pl.when(pl.program_id(0) == 0)
        def _():
            dqnw_ref[...] = jnp.zeros_like(dqnw_ref)
            dkvnw_ref[...] = jnp.zeros_like(dkvnw_ref)

        qc, qa, qb = qc_ref[...], qa_ref[...], qb_ref[...]
        dkp = jnp.zeros((tm, DH), F32)
        for h in range(HEADS):
            lo, mid, hi = h * HD, h * HD + DH, (h + 1) * HD
            dq_ref[:, lo:hi] = _rope64(dqq_ref[:, lo:hi], qc, -qa, -qb).astype(BF16)
            dkv_ref[:, lo:mid] = dkk_ref[:, lo:mid]
            dkv_ref[:, mid:hi] = dvv_ref[:, h * DH:(h + 1) * DH]
            dkp = dkp + dkk_ref[:, mid:hi].astype(F32)
        dcqn = _dot(dq_ref[...], wq_ref[...], "nn")
        rq_v = rq_ref[...]
        dcq, prod = _rms_bwd(dcqn, cq_ref[...].astype(F32) * rq_v, rq_v, qnw_ref[...])
        dcq_ref[...] = dcq.astype(BF16)
        dqnw_ref[...] += jnp.sum(prod, axis=0, keepdims=True)
        dckvn = _dot(dkv_ref[...], wukv_ref[...], "nn")
        rkv_v = rkv_ref[...]
        dckv, prod = _rms_bwd(dckvn, ckv_ref[...].astype(F32) * rkv_v, rkv_v, kvnw_ref[...])
        dckv_ref[...] = dckv.astype(BF16)
        dkvnw_ref[...] += jnp.sum(prod, axis=0, keepdims=True)
        dkpe_ref[:, :DH] = _rope64(dkp, kc_ref[...], -ka_ref[...], -kb_ref[...]).astype(BF16)
        dkpe_ref[:, DH:] = jnp.zeros((tm, DH), BF16)

    row = lambda w, cb: pl.BlockSpec((tm, w), lambda i: (i, cb))
    full = lambda a: pl.BlockSpec(a.shape, lambda i: (0, 0))
    wide = jax.ShapeDtypeStruct((s, HEADS * HD), BF16)
    return pl.pallas_call(
        body, name="mla_prep_bwd", grid=(s // tm,),
        in_specs=[ANY, row(HEADS * HD, 0), row(HEADS * HD, 0), row(WIDTH, 0),
                  row(Q_RANK, C_Q // Q_RANK), row(KV_RANK, C_KV // KV_RANK), row(1, 0), row(1, 0),
                  full(qn_w), full(kvn_w), full(wq), full(wukv), row(HD, 0), row(HD, 0), row(HD, 0),
                  row(DH, 0), row(DH, 0), row(DH, 0)],
        out_specs=[row(tail, C_Q // tail), row(HEADS * HD, 0), row(HEADS * HD, 0),
                   pl.BlockSpec((1, Q_RANK), lambda i: (0, 0)), pl.BlockSpec((1, KV_RANK), lambda i: (0, 0))],
        out_shape=[jax.ShapeDtypeStruct((s, PROJ_W), BF16), wide, wide,
                   jax.ShapeDtypeStruct((1, Q_RANK), F32), jax.ShapeDtypeStruct((1, KV_RANK), F32)],
        input_output_aliases={0: 0},
        compiler_params=_cparams(("arbitrary",)),
    )(dproj, dqq, dkk, dvv, proj, proj, rq, rkv, qn_w, kvn_w, wq, wukv, *qtab, *ktab)


def _rms_in_bwd(dxn, h, r, dh2, norm_w, s):
    def body(dxn_ref, h_ref, r_ref, dh2_ref, w_ref, gx_ref, gm_ref, dw_ref):
        i = pl.program_id(0)
        r_v = r_ref[...]
        dx, prod = _rms_bwd(dxn_ref[...], h_ref[...] * r_v, r_v, w_ref[...])
        dh = dh2_ref[...] + dx

        @pl.when(i == 0)
        def _():
            gm_ref[...] = dh
            dw_ref[...] = jnp.sum(prod, axis=0, keepdims=True)

        @pl.when(i > 0)
        def _():
            gx_ref[...] = dh
            dw_ref[...] += jnp.sum(prod, axis=0, keepdims=True)

    blk = pl.BlockSpec((BLK, D), lambda i: (i, 0))
    return pl.pallas_call(
        body, name="rms_in_bwd", grid=(s // BLK,),
        in_specs=[blk, blk, pl.BlockSpec((BLK, 1), lambda i: (i, 0)), blk, pl.BlockSpec((1, D), lambda i: (0, 0))],
        out_specs=[pl.BlockSpec((BLK, D), lambda i: (jnp.maximum(i - 1, 0), 0)), pl.BlockSpec((BLK, D), lambda i: (0, 0)),
                   pl.BlockSpec((1, D), lambda i: (0, 0))],
        out_shape=[jax.ShapeDtypeStruct((s - BLK, D), F32), jax.ShapeDtypeStruct((BLK, D), F32),
                   jax.ShapeDtypeStruct((1, D), F32)],
        compiler_params=_cparams(("arbitrary",)),
    )(dxn, h, r, dh2, norm_w)


def _adam_math(w, g, m, v):
    m = ADAM_B1 * m + (1.0 - ADAM_B1) * g
    v = ADAM_B2 * v + (1.0 - ADAM_B2) * (g * g)
    m_hat = m / (1.0 - ADAM_B1 ** ADAM_STEP)
    v_hat = v / (1.0 - ADAM_B2 ** ADAM_STEP)
    return -ADAM_LR * (m_hat / (jnp.sqrt(v_hat) + ADAM_EPS) + ADAM_WD * w), m, v


def _adamw(w, g, m, v, name):
    rows, cols = w.shape
    tr = rows
    for cand in (128, 64, 32, 16, 8):
        if rows % cand == 0:
            tr = cand
            break

    def body(w_ref, g_ref, m_ref, v_ref, d_ref, nm_ref, nv_ref):
        d_ref[...], nm_ref[...], nv_ref[...] = _adam_math(w_ref[...], g_ref[...], m_ref[...], v_ref[...])

    spec = pl.BlockSpec((tr, cols), lambda i: (i, 0))
    return pl.pallas_call(
        body, name=name, grid=(rows // tr,), in_specs=[spec] * 4, out_specs=[spec] * 3,
        out_shape=[jax.ShapeDtypeStruct((rows, cols), F32)] * 3,
        compiler_params=_cparams(("parallel",)),
    )(w, g, m, v)


def _adamw_t(w_t, g_t, m_t, v_t, name):
    cols, rows = w_t.shape

    def body(w_ref, g_ref, m_ref, v_ref, go_ref, d_ref, nm_ref, nv_ref):
        g = g_ref[...]
        go_ref[...] = g
        d_ref[...], nm_ref[...], nv_ref[...] = _adam_math(w_ref[...], g, m_ref[...], v_ref[...])

    blk = pl.BlockSpec((128, rows), lambda k: (k, 0))
    return pl.pallas_call(
        body, name=name, grid=(pl.cdiv(cols, 128),), in_specs=[blk] * 4, out_specs=[blk] * 4,
        out_shape=[jax.ShapeDtypeStruct((cols, rows), F32)] * 4, compiler_params=_cparams(("parallel",)),
    )(w_t, g_t, m_t, v_t)


def _adamw_small(w, gall, m, v):
    def body(w_ref, g_ref, m_ref, v_ref, gs_ref, d_ref, nm_ref, nv_ref):
        g = g_ref[0]
        for dev in range(1, 8):
            g = g + g_ref[dev]
        gs_ref[...] = g
        d_ref[...], nm_ref[...], nv_ref[...] = _adam_math(w_ref[...], g, m_ref[...], v_ref[...])

    return pl.pallas_call(
        body, name="adamw_small", out_shape=[jax.ShapeDtypeStruct((SM_ROWS, 128), F32)] * 4,
        compiler_params=pltpu.CompilerParams(vmem_limit_bytes=VMEM_LIMIT),
    )(w, gall, m, v)


ADD_ROWS = 464
HALF_BLOCKS = PK_HALF // ADD_ROWS


def _rs_core_add(srcs, land, pos):
    def body(pos_ref, *refs):
        src_refs, (land_ref, out_ref, own, sems) = refs[:len(srcs)], refs[len(srcs):]
        step = pl.program_id(0)

        def fetch(s):
            for hf in range(2):
                @pl.when(pos_ref[1] == hf)
                def _():
                    for k, r0, p, n in _grad_half(s, hf):
                        pltpu.make_async_copy(src_refs[k].at[pl.ds(r0, n), :], own.at[s % 2, pl.ds(p, n), :],
                                              sems.at[s % 2]).start()

        @pl.when(step == 0)
        def _():
            fetch(0)

        for s in range(N_CHIPS):
            @pl.when(step == s)
            def _():
                if s + 1 < N_CHIPS:
                    fetch(s + 1)
                slot = own.at[s % 2]
                pltpu.make_async_copy(slot, slot, sems.at[s % 2]).wait()
                out_ref[0] = (slot[...].astype(F32) + land_ref[0].astype(F32)).astype(BF16)

    blk = pl.BlockSpec((1, PK_HALF, D), lambda s, pos_ref: (s, 0, 0))
    return pl.pallas_call(
        body, name="rs_core_add",
        grid_spec=pltpu.PrefetchScalarGridSpec(
            num_scalar_prefetch=1, grid=(N_CHIPS,), in_specs=[ANY] * len(srcs) + [blk], out_specs=blk,
            scratch_shapes=[pltpu.VMEM((2, PK_HALF, D), BF16), pltpu.SemaphoreType.DMA((2,))]),
        out_shape=jax.ShapeDtypeStruct((N_CHIPS, PK_HALF, D), BF16),
        compiler_params=_cparams(("arbitrary",)),
    )(pos, *srcs, land)


def _rs_chip_add(cp, land, pos):
    def body(pos_ref, a_ref, l_ref, o_ref):
        o_ref[0] = ((a_ref[0].astype(F32) + l_ref[0].astype(F32)) + l_ref[1].astype(F32)) + l_ref[2].astype(F32)

    return pl.pallas_call(
        body, name="rs_chip_add",
        grid_spec=pltpu.PrefetchScalarGridSpec(
            num_scalar_prefetch=1, grid=(HALF_BLOCKS,),
            in_specs=[pl.BlockSpec((1, ADD_ROWS, D), lambda i, pos_ref: (pos_ref[0], i, 0)),
                      pl.BlockSpec((3, ADD_ROWS, D), lambda i, pos_ref: (0, i, 0))],
            out_specs=pl.BlockSpec((1, ADD_ROWS, D), lambda i, pos_ref: (pos_ref[1], i, 0))),
        out_shape=jax.ShapeDtypeStruct((2, PK_HALF, D), F32),
        compiler_params=_cparams(("parallel",)),
    )(pos, cp, land)


ANY = pl.BlockSpec(memory_space=pl.ANY)


def _mesh_pos():
    return lax.axis_index("x"), lax.axis_index("y"), lax.axis_index("c")


def _chip_peer(x, y, c, r):
    return (jnp.bitwise_xor(x, r >> 1), jnp.bitwise_xor(y, r & 1), c)


def _allgather_w_in(big0, wt, meta_loc, x2, norm_w):
    s_rows = x2.shape[0] + BLK
    tok = next(t for t in (512, 256, 128) if x2.shape[0] % t == 0)

    def body(big0_ref, wt_ref, meta_ref, x_ref, nw_ref, big_ref, metaf_ref, h_ref, xn_ref, r_ref,
             lsem, ssem, rsem, fssem, frsem, msem_s, msem_r, osem, xbuf, xnbuf, rbuf, mbuf):
        del big0_ref
        x, y, c = _mesh_pos()
        j = 2 * x + y
        me, sibling = (x, y, c), (x, y, 1 - c)

        def half_wait(s_sem, r_sem, halves=1):
            rows = big_ref.at[pl.ds(0, halves * AG_IN_HALF), :]
            return pltpu.make_async_remote_copy(src_ref=rows, dst_ref=rows, send_sem=s_sem, recv_sem=r_sem,
                                                device_id=me, device_id_type=MESH)

        own_meta = pltpu.make_async_copy(meta_ref, metaf_ref.at[j], lsem)
        own_meta.start()
        for s in range(N_CHIPS):
            for hf in range(2):
                @pl.when((j == s) & (c == hf))
                def _():
                    for r in (1, 2, 3):
                        for p, b, n in _ag_half(s, True, hf):
                            pltpu.make_async_remote_copy(
                                src_ref=wt_ref.at[pl.ds(p, n), :], dst_ref=big_ref.at[pl.ds(b, n), :],
                                send_sem=ssem.at[r - 1], recv_sem=rsem.at[r - 1],
                                device_id=_chip_peer(x, y, c, r), device_id_type=MESH).start()

            @pl.when(j == s)
            def _():
                for hf in range(2):
                    for p, b, n in _ag_half(s, True, hf):
                        pltpu.make_async_remote_copy(
                            src_ref=wt_ref.at[pl.ds(p, n), :], dst_ref=big_ref.at[pl.ds(b, n), :],
                            send_sem=osem.at[0], recv_sem=osem.at[1], device_id=sibling, device_id_type=MESH).start()

        meta_copies = [pltpu.make_async_remote_copy(
            src_ref=meta_ref, dst_ref=metaf_ref.at[j], send_sem=msem_s.at[r - 1], recv_sem=msem_r.at[r - 1],
            device_id=_chip_peer(x, y, c, r), device_id_type=MESH) for r in (1, 2, 3)]
        for cp in meta_copies:
            cp.start()

        nw = nw_ref[...]

        def norm_rows(first, n):
            hv = xbuf[pl.ds(0, n), :]
            r = lax.rsqrt(jnp.mean(hv * hv, axis=-1, keepdims=True) + NORM_EPS)
            xnbuf[pl.ds(0, n), :] = (hv * r * nw).astype(BF16)
            rbuf[pl.ds(0, n), :] = r
            rows = pl.ds(first if isinstance(first, int) else pl.multiple_of(first, BLK), n)
            pltpu.sync_copy(xbuf.at[pl.ds(0, n), :], h_ref.at[rows, :])
            pltpu.sync_copy(xnbuf.at[pl.ds(0, n), :], xn_ref.at[rows, :])
            pltpu.sync_copy(rbuf.at[pl.ds(0, n), :], r_ref.at[rows, :])

        def token_rows(i, carry):
            pltpu.sync_copy(x_ref.at[pl.ds(pl.multiple_of(i * tok, tok), tok), :], xbuf)
            norm_rows(BLK + i * tok, tok)
            return carry

        lax.fori_loop(0, (s_rows - BLK) // tok, token_rows, 0)
        for cp in meta_copies:
            cp.wait_recv()
        own_meta.wait()
        xbuf[pl.ds(0, BLK), :] = jnp.zeros((BLK, D), F32)
        for sh in range(N_CHIPS):
            pltpu.sync_copy(metaf_ref.at[sh], mbuf)
            xbuf[PAD_FRONT:BLK, sh * (D // N_CHIPS):(sh + 1) * (D // N_CHIPS)] = mbuf[...]
        norm_rows(0, BLK)

        for r in (1, 2, 3):
            half_wait(ssem.at[r - 1], rsem.at[r - 1]).wait_recv()
            src_shard = jnp.bitwise_xor(j, r)
            for s in range(N_CHIPS):
                for hf in range(2):
                    @pl.when((src_shard == s) & (c == hf))
                    def _():
                        for _, b, n in _ag_half(s, True, hf):
                            rows = big_ref.at[pl.ds(b, n), :]
                            pltpu.make_async_remote_copy(
                                src_ref=rows, dst_ref=rows, send_sem=fssem.at[r - 1], recv_sem=frsem.at[r - 1],
                                device_id=sibling, device_id_type=MESH).start()
        for r in (1, 2, 3):
            half_wait(fssem.at[r - 1], frsem.at[r - 1]).wait_recv()
        for r in (1, 2, 3):
            half_wait(ssem.at[r - 1], rsem.at[r - 1]).wait_send()
            half_wait(fssem.at[r - 1], frsem.at[r - 1]).wait_send()
        own_rows = half_wait(osem.at[0], osem.at[1], halves=2)
        own_rows.wait_recv()
        own_rows.wait_send()
        for cp in meta_copies:
            cp.wait_send()

    dma3 = pltpu.SemaphoreType.DMA((3,))
    big, _, h, xn, r = pl.pallas_call(
        body, name="allgather_w_in", in_specs=[ANY, ANY, ANY, ANY, pl.BlockSpec(memory_space=pltpu.VMEM)],
        out_specs=[ANY] * 5,
        out_shape=[jax.ShapeDtypeStruct((BIG_ROWS, D), BF16), jax.ShapeDtypeStruct((N_CHIPS,) + meta_loc.shape, F32),
                   jax.ShapeDtypeStruct((s_rows, D), F32), jax.ShapeDtypeStruct((s_rows, D), BF16),
                   jax.ShapeDtypeStruct((s_rows, 1), F32)],
        scratch_shapes=[pltpu.SemaphoreType.DMA(()), dma3, dma3, dma3, dma3, dma3, dma3, pltpu.SemaphoreType.DMA((2,)),
                        pltpu.VMEM((tok, D), F32), pltpu.VMEM((tok, D), BF16), pltpu.VMEM((tok, 1), F32),
                        pltpu.VMEM(meta_loc.shape, F32)],
        input_output_aliases={0: 0},
        compiler_params=pltpu.CompilerParams(has_side_effects=True),
    )(big0, wt, meta_loc, x2, norm_w)
    return big, h, xn, r


HBM = pl.BlockSpec(memory_space=pltpu.HBM)
SEM = pl.BlockSpec(memory_space=pltpu.SEMAPHORE)
EFFECT = pltpu.SideEffectType.DATAFLOW_SIDE_EFFECTING
REST_PEERS = [(r, dc) for r in (1, 2, 3) for dc in (0, 1)]


def _allgather_rest_start(big):
    def body(big_ref, *rest):
        s_sems, r_sems = rest[:6], rest[6:12]
        x, y, c = _mesh_pos()
        j = 2 * x + y
        for s in range(N_CHIPS):
            for hf in range(2):
                @pl.when((j == s) & (c == hf))
                def _():
                    for k, (r, dc) in enumerate(REST_PEERS):
                        for _, b, n in _ag_half(s, False, hf):
                            rows = big_ref.at[pl.ds(b, n), :]
                            pltpu.make_async_remote_copy(
                                src_ref=rows, dst_ref=rows, send_sem=s_sems[k], recv_sem=r_sems[2 * (r - 1) + hf],
                                device_id=_chip_peer(x, y, dc, r), device_id_type=MESH).start()

    sem = pltpu.SemaphoreType.DMA(())
    out = pl.pallas_call(
        body, name="allgather_rest_start", out_shape=(sem,) * 12 + (pltpu.HBM(big.shape, big.dtype),),
        in_specs=(HBM,), out_specs=(SEM,) * 12 + (HBM,), input_output_aliases={0: 12},
        compiler_params=pltpu.CompilerParams(has_side_effects=EFFECT),
    )(pltpu.with_memory_space_constraint(big, pltpu.HBM))
    return out[:12], out[12]


def _allgather_rest_wait(sems, big, after):
    def body(big_ref, *rest):
        s_sems, r_sems = rest[:6], rest[6:12]
        me = _mesh_pos()
        rows = big_ref.at[pl.ds(0, AG_REST_HALF), :]
        for k in range(6):
            copy = pltpu.make_async_remote_copy(src_ref=rows, dst_ref=rows, send_sem=s_sems[k], recv_sem=r_sems[k],
                                                device_id=me, device_id_type=MESH)
            copy.wait_send()
            copy.wait_recv()

    return pl.pallas_call(
        body, name="allgather_rest_wait", out_shape=pltpu.HBM(big.shape, big.dtype),
        in_specs=(HBM,) + (SEM,) * 12 + (pl.BlockSpec(memory_space=pl.ANY),), out_specs=HBM,
        input_output_aliases={0: 0}, compiler_params=pltpu.CompilerParams(has_side_effects=EFFECT),
    )(big, *sems, after)


RS_SPLIT = 2


def _rs_core_exchange(srcs):
    def body(*refs):
        src_refs, (land_ref, ssem, rsem) = refs[:len(srcs)], refs[len(srcs):]
        x, y, c = _mesh_pos()
        me, sibling = (x, y, c), (x, y, 1 - c)
        for hf in range(2):
            @pl.when(c == hf)
            def _():
                for s in range(N_CHIPS):
                    for k, r0, p, n in _grad_half(s, 1 - hf):
                        pltpu.make_async_remote_copy(
                            src_ref=src_refs[k].at[pl.ds(r0, n), :], dst_ref=land_ref.at[s, pl.ds(p, n), :],
                            send_sem=ssem, recv_sem=rsem, device_id=sibling, device_id_type=MESH).start()
        whole = pltpu.make_async_remote_copy(src_ref=land_ref, dst_ref=land_ref, send_sem=ssem, recv_sem=rsem,
                                             device_id=me, device_id_type=MESH)
        whole.wait_recv()
        whole.wait_send()

    return pl.pallas_call(
        body, name="rs_core_exchange", in_specs=[ANY] * len(srcs), out_specs=ANY,
        out_shape=jax.ShapeDtypeStruct((N_CHIPS, PK_HALF, D), BF16),
        scratch_shapes=[pltpu.SemaphoreType.DMA(())] * 2,
        compiler_params=pltpu.CompilerParams(has_side_effects=True),
    )(*srcs)


def _rs_chip_start(cp):
    def body(cp_ref, land_ref, s1, s2, s3, r1, r2, r3, cp_thru, land_thru, token):
        x, y, c = _mesh_pos()
        j = 2 * x + y
        for r, s_sem, r_sem in zip((1, 2, 3), (s1, s2, s3), (r1, r2, r3)):
            pltpu.make_async_remote_copy(
                src_ref=cp_ref.at[jnp.bitwise_xor(j, r)], dst_ref=land_ref.at[r - 1], send_sem=s_sem, recv_sem=r_sem,
                device_id=_chip_peer(x, y, c, r), device_id_type=MESH).start()
        token[...] = jnp.zeros_like(token)

    land_shape = (3, PK_HALF, D)
    sem = pltpu.SemaphoreType.DMA(())
    out = pl.pallas_call(
        body, name="rs_chip_start",
        out_shape=(sem,) * 6 + (pltpu.HBM(cp.shape, cp.dtype), pltpu.HBM(land_shape, BF16),
                                jax.ShapeDtypeStruct((8, 128), F32)),
        in_specs=(HBM, HBM), out_specs=(SEM,) * 6 + (HBM, HBM, pl.BlockSpec(memory_space=pltpu.VMEM)),
        input_output_aliases={0: 6, 1: 7},
        compiler_params=pltpu.CompilerParams(has_side_effects=EFFECT),
    )(pltpu.with_memory_space_constraint(cp, pltpu.HBM),
      pltpu.with_memory_space_constraint(lax.empty(land_shape, BF16), pltpu.HBM))
    return out[:6], out[6], out[7], out[8]


def _rs_chip_wait(sems, cp_thru, land_thru, after):
    def body(cp_ref, land_ref, s1, s2, s3, r1, r2, r3, after_ref, cp_out, land_out):
        me = _mesh_pos()
        for r, s_sem, r_sem in zip((1, 2, 3), (s1, s2, s3), (r1, r2, r3)):
            copy = pltpu.make_async_remote_copy(src_ref=cp_ref.at[0], dst_ref=land_ref.at[r - 1], send_sem=s_sem,
                                                recv_sem=r_sem, device_id=me, device_id_type=MESH)
            copy.wait_send()
            copy.wait_recv()

    return pl.pallas_call(
        body, name="rs_chip_wait",
        out_shape=(pltpu.HBM(cp_thru.shape, cp_thru.dtype), pltpu.HBM(land_thru.shape, land_thru.dtype)),
        in_specs=(HBM, HBM) + (SEM,) * 6 + (pl.BlockSpec(memory_space=pl.ANY),), out_specs=(HBM, HBM),
        input_output_aliases={0: 0, 1: 1},
        compiler_params=pltpu.CompilerParams(has_side_effects=EFFECT),
    )(cp_thru, land_thru, *sems, after)


def _rs_finish(full, small):
    n = PK_HALF // RS_SPLIT

    def body(full_in_ref, sm_ref, full_ref, all_ref, lsem, ssem, rsem, sm_s, sm_r):
        del full_in_ref
        x, y, c = _mesh_pos()
        me, sibling = (x, y, c), (x, y, 1 - c)
        my_id = 4 * x + 2 * y + c
        own_sm = pltpu.make_async_copy(sm_ref, all_ref.at[my_id], lsem)
        own_sm.start()
        for q in range(RS_SPLIT):
            rows = full_ref.at[c, pl.ds(q * n, n), :]
            pltpu.make_async_remote_copy(src_ref=rows, dst_ref=rows, send_sem=ssem, recv_sem=rsem,
                                         device_id=sibling, device_id_type=MESH).start()
        half = pltpu.make_async_remote_copy(src_ref=full_ref.at[c], dst_ref=full_ref.at[c], send_sem=ssem,
                                            recv_sem=rsem, device_id=me, device_id_type=MESH)
        smalls = [pltpu.make_async_remote_copy(
            src_ref=sm_ref, dst_ref=all_ref.at[my_id], send_sem=sm_s.at[r - 1], recv_sem=sm_r.at[r - 1],
            device_id=(jnp.bitwise_xor(x, r >> 2), jnp.bitwise_xor(y, (r >> 1) & 1), jnp.bitwise_xor(c, r & 1)),
            device_id_type=MESH) for r in range(1, 8)]
        for cpy in smalls:
            cpy.start()
        half.wait_recv()
        half.wait_send()
        for cpy in smalls:
            cpy.wait_recv()
        for cpy in smalls:
            cpy.wait_send()
        own_sm.wait()

    dma7 = pltpu.SemaphoreType.DMA((7,))
    return pl.pallas_call(
        body, name="rs_finish", in_specs=[ANY, ANY], out_specs=[ANY, ANY],
        out_shape=[jax.ShapeDtypeStruct((2, PK_HALF, D), F32), jax.ShapeDtypeStruct((8, SM_ROWS, 128), F32)],
        scratch_shapes=[pltpu.SemaphoreType.DMA(()), pltpu.SemaphoreType.DMA(()), pltpu.SemaphoreType.DMA(()), dma7, dma7],
        input_output_aliases={0: 0},
        compiler_params=pltpu.CompilerParams(has_side_effects=True),
    )(full, small)


def _rope_tables(s):
    pos = jnp.arange(s, dtype=F32) - PAD_FRONT

    def cs(d):
        inv = ROPE_BASE ** (-jnp.arange(0, d, 2, dtype=F32) / d)
        ang = pos[:, None] * inv[None, :]
        return jnp.cos(ang), jnp.sin(ang)

    c, sn = cs(ROPE_D)
    z = jnp.zeros_like(c)
    ktab = (jnp.concatenate([c, c, z, z], axis=1), jnp.concatenate([-sn, z, z, z], axis=1),
            jnp.concatenate([z, sn, z, z], axis=1))
    one = jnp.ones_like(c)
    qtab = tuple(ATT_SCALE * jnp.concatenate(parts, axis=1) for parts in (
        [one, one, one, one, c, c, z, z], [z, z, z, z, -sn, z, z, z], [z, z, z, z, z, sn, z, z]))
    lane = jnp.arange(HD) == DH + ROPE_D
    qbias = lane.astype(F32)[None, :]
    kbias = jnp.where((pos < 0)[:, None] & lane[None, DH:], NEG_INF, 0.0).astype(F32)
    c, sn = cs(DH)
    return qtab, ktab, qbias, kbias, jnp.concatenate([c, c], axis=1), jnp.concatenate([-sn, sn], axis=1)


def _grad_mm(a, b, m, n, s, name):
    return _matmul(a, b, "tn", m, n, s, min(m, 512), min(n, 1024), s, BF16, name, n_outer=True)


def _local_step(x2, target2, hp0, big, small_w, normed=None, gather_wait=None, reduce_start=None, reduce_wait=None):
    norm_w, qn_w, kvn_w, gn_w, gn_b, fin_w = small_w
    s = x2.shape[0] + BLK
    tm = _row_tile(s, 1408)
    qtab, ktab, qbias, kbias, c128, s128 = _rope_tables(s)
    consts, consts_bwd, gamma = _ret_consts()

    h, xn, r1 = _rms_in(x2, hp0, norm_w, s) if normed is None else normed
    proj = _matmul(xn, big, "nt", s, PROJ_W, D, tm, 512, D, BF16, "proj", b_off=(BIG_IN // 512, 0))
    if gather_wait is not None:
        big = gather_wait(big, proj)
    uq = big[BIG_UQ:BIG_UQ + 384].reshape(HEADS, DH + ROPE_D, Q_RANK)
    wq = jnp.pad(uq, ((0, 0), (0, HD - DH - ROPE_D), (0, 0))).reshape(HEADS * HD, Q_RANK)
    wukv = big[BIG_UKV:BIG_UKV + 256].reshape(2 * WIDTH, KV_RANK)
    wbm = big[BIG_BM:BIG_BM + 1024].reshape(D, WIDTH)
    wbr = big[BIG_BR:BIG_BR + 1024].reshape(D, WIDTH)
    cqn, rq, ckvn, rkv, qq, kk, vv = _mla_prep(proj, qn_w, kvn_w, wq, wukv, qtab, ktab, qbias, kbias, s)
    o_mla, y_mla, lse = _attn_fwd(qq, kk, vv, proj, s)
    y_ret, on, rstd, qr, kr, rall = _ret_fwd(proj, gn_w, gn_b, c128, s128, consts, gamma, s)
    u_mla, u_ret, merged = _merge_fwd(y_mla, y_ret, wbm, wbr, proj, s)
    h2 = _out_fwd(merged, big, h, s)
    dh2, dh2b, loss_blk, dfin = _loss_bwd(h2, target2, fin_w, s)

    dproj, du_mla, du_ret = _merge_bwd(dh2b, big, u_mla, u_ret, proj, s)
    g_out = _grad_mm(merged, dh2b, D, D, s, "grad_w_out")
    dproj, do_mla, delta = _branch_mla_bwd(dproj, du_mla, wbm, o_mla, proj, s)
    g_bm = _grad_mm(du_mla, y_mla, D, WIDTH, s, "grad_w_branch_mla")
    dproj, do_ret, dgw, dgb = _branch_ret_bwd(dproj, du_ret, wbr, on, rstd, gn_w, gn_b, proj, s)
    g_br = _grad_mm(du_ret, y_ret, D, WIDTH, s, "grad_w_branch_ret")
    dproj = _ret_bwd(dproj, qr, kr, proj, rall, do_ret, c128, s128, consts_bwd, gamma, s)
    dqq, dkk, dvv = _attn_bwd(qq, kk, vv, do_mla, lse, delta.reshape(lse.shape), s)
    dproj, dq, dkv, dqnw, dkvnw = _mla_prep_bwd(dproj, dqq, dkk, dvv, proj, rq, rkv, qn_w, kvn_w, wq, wukv,
                                                qtab, ktab, s)
    g_q = _grad_mm(dq, cqn, HEADS * HD, Q_RANK, s, "grad_w_uq")
    g_ukv = _grad_mm(dkv, ckvn, 2 * WIDTH, KV_RANK, s, "grad_w_ukv")
    g_in = _grad_mm(dproj, xn, PROJ_W, D, s, "grad_w_in")

    g_uq = g_q.reshape(HEADS, HD, Q_RANK)[:, :DH + ROPE_D]
    grads = (g_in, g_out, g_bm.reshape(1024, D), g_br.reshape(1024, D), g_uq.reshape(384, D), g_ukv.reshape(256, D))
    token, travelling = (None, grads) if reduce_start is None else reduce_start(grads)
    dxn = _matmul(dproj, big, "nn", s, D, PROJ_W, _row_tile(s, 384), 512, PROJ_W, F32, "dxn",
                  b_off=(BIG_IN // PROJ_W, 0), n_outer=True, after=token)
    grad_x, gmeta_blk, dnorm = _rms_in_bwd(dxn, h, r1, dh2, norm_w, s)
    small = _small_rows((dnorm, dqnw, dkvnw, dgw, dgb, dfin), loss_blk, gmeta_blk[PAD_FRONT:])
    return grad_x, travelling if reduce_wait is None else reduce_wait(travelling, dnorm), small


def _small_rows(ws, first=None, last=None):
    def part(a, rows):
        a = a.reshape(-1, 128)
        return a if a.shape[0] == rows else jnp.pad(a, ((0, rows - a.shape[0]), (0, 0)))

    bounds = (SM_NORM, SM_QN, SM_KVN, SM_GNW, SM_GNB, SM_FIN, SM_META)
    first = jnp.zeros((SM_NORM, 128), F32) if first is None else first
    last = jnp.zeros((SM_ROWS - SM_META, 128), F32) if last is None else last
    return jnp.concatenate([part(first, SM_NORM)] + [part(w, bounds[k + 1] - bounds[k]) for k, w in enumerate(ws)]
                           + [part(last, SM_ROWS - SM_META)], axis=0)


def kernel(x, meta, norm_w, w_in, mla_q_norm_w, mla_w_uq, mla_kv_norm_w, mla_w_ukv, ret_gn_w, ret_gn_b, w_branch_mla, w_branch_ret, w_out, final_norm_w, loss_target, m_meta, m_norm_w, m_w_in, m_mla_q_norm_w, m_mla_w_uq, m_mla_kv_norm_w, m_mla_w_ukv, m_ret_gn_w, m_ret_gn_b, m_w_branch_mla, m_w_branch_ret, m_w_out, m_final_norm_w, v_meta, v_norm_w, v_w_in, v_mla_q_norm_w, v_mla_w_uq, v_mla_kv_norm_w, v_mla_w_ukv, v_ret_gn_w, v_ret_gn_b, v_w_branch_mla, v_w_branch_ret, v_w_out, v_final_norm_w):
    j = 2 * lax.axis_index("x") + lax.axis_index("y")
    tr = lambda w: w[0].T.reshape(-1, D).astype(BF16)
    pos = jnp.stack([j, lax.axis_index("c")]).astype(jnp.int32)
    put = lambda buf, rows, off: lax.dynamic_update_slice(buf, rows, (off, 0))
    big0 = lax.empty((BIG_ROWS, D), BF16)
    big0 = put(big0, w_out[0].astype(BF16), BIG_OUT + 512 * j)
    big0 = put(big0, tr(w_branch_mla), BIG_BM + 256 * j)
    big0 = put(big0, tr(w_branch_ret), BIG_BR + 256 * j)
    big0 = put(big0, tr(mla_w_uq), BIG_UQ + 96 * j)
    big0 = put(big0, tr(mla_w_ukv), BIG_UKV + 64 * j)
    big0 = put(big0, jnp.zeros((AG_JUNK_END - AG_JUNK_REST, D), BF16), AG_JUNK_REST)
    big0 = put(big0, jnp.zeros((ZERO_ROWS, D), BF16), BIG_IN + IN_WIDTH)
    wt = jnp.concatenate([tr(w_in), jnp.zeros((2 * AG_IN_HALF - IN_SHARD, D), BF16)], axis=0)
    big, h, xn, r1 = _allgather_w_in(big0, wt, meta, x[0], norm_w)
    gather_sems, big = _allgather_rest_start(big)

    def gather_wait(big_travelling, after):
        return _allgather_rest_wait(gather_sems, big_travelling, after)

    small_w = (norm_w, mla_q_norm_w, mla_kv_norm_w, ret_gn_w, ret_gn_b, final_norm_w.reshape(1, D))

    def reduce_start(grads):
        chip_part = _rs_core_add(grads, _rs_core_exchange(grads), pos)
        sems, part_thru, land_thru, token = _rs_chip_start(chip_part)
        return token, (sems, part_thru, land_thru)

    def reduce_wait(state, after):
        return _rs_chip_wait(*state, after)

    grad_x, (chip_part, land2), small = _local_step(x[0], loss_target[0], None, big, small_w, (h, xn, r1), gather_wait,
                                                    reduce_start, reduce_wait)
    full, small_all = _rs_finish(_rs_chip_add(chip_part, land2, pos), small)
    full = full.reshape(PK_ROWS, D)

    untr = lambda lo, hi, rows: full[lo:hi].reshape(rows, -1).T
    grads = {
        "w_out": full[PK_OUT:PK_BM], "w_branch_mla": untr(PK_BM, PK_BR, 512), "w_branch_ret": untr(PK_BR, PK_UQ, 512),
        "mla_w_uq": untr(PK_UQ, PK_UKV, 384), "mla_w_ukv": untr(PK_UKV, PK_PAD, 512),
    }
    big_w = {"mla_w_uq": (mla_w_uq, m_mla_w_uq, v_mla_w_uq),
             "mla_w_ukv": (mla_w_ukv, m_mla_w_ukv, v_mla_w_ukv),
             "w_branch_mla": (w_branch_mla, m_w_branch_mla, v_w_branch_mla),
             "w_branch_ret": (w_branch_ret, m_w_branch_ret, v_w_branch_ret), "w_out": (w_out, m_w_out, v_w_out)}
    res = {"w_in": tuple(a.T[None] for a in _adamw_t(w_in[0].T, full, m_w_in[0].T, v_w_in[0].T, "adamw_w_in"))}
    for name, (w, m, v) in big_w.items():
        d, nm, nv = _adamw(w[0], grads[name], m[0], v[0], "adamw_" + name)
        res[name] = (grads[name][None], d[None], nm[None], nv[None])

    small_m = (m_norm_w, m_mla_q_norm_w, m_mla_kv_norm_w, m_ret_gn_w, m_ret_gn_b, m_final_norm_w.reshape(1, D))
    small_v = (v_norm_w, v_mla_q_norm_w, v_mla_kv_norm_w, v_ret_gn_w, v_ret_gn_b, v_final_norm_w.reshape(1, D))
    gs, ds, ms, vs = _adamw_small(_small_rows(small_w), small_all, _small_rows(small_m), _small_rows(small_v))
    names = ["norm_w", "mla_q_norm_w", "mla_kv_norm_w", "ret_gn_w", "ret_gn_b", "final_norm_w"]
    bounds = [SM_NORM, SM_QN, SM_KVN, SM_GNW, SM_GNB, SM_FIN]
    for k, name in enumerate(names):
        shape = (D,) if name == "final_norm_w" else (1, -1)
        rows = small_w[k].size // 128
        res[name] = tuple(a[bounds[k]:bounds[k] + rows].reshape(shape) for a in (gs, ds, ms, vs))
    g_meta = lax.dynamic_slice_in_dim(gs[SM_META:SM_META + 256].reshape(N_META, D), j * (D // N_CHIPS), D // N_CHIPS, axis=1)
    res["meta"] = (g_meta,) + tuple(_adamw(meta, g_meta, m_meta, v_meta, "adamw_meta"))

    order = ["meta", "norm_w", "w_in", "mla_q_norm_w", "mla_w_uq", "mla_kv_norm_w", "mla_w_ukv", "ret_gn_w", "ret_gn_b",
             "w_branch_mla", "w_branch_ret", "w_out", "final_norm_w"]
    return (gs[0, 0], grad_x[None]) + tuple(res[n][k] for k in range(4) for n in order)
```

```python
import functools
import math

import numpy as np
import jax
import jax.numpy as jnp
from jax import lax
from jax.experimental import pallas as pl
from jax.experimental.pallas import tpu as pltpu

F32 = jnp.float32
BF16 = jnp.bfloat16
MESH = pl.DeviceIdType.MESH

D = 2048
N_META = 16
BLK = 128
PAD_FRONT = BLK - N_META
HEADS = 8
DH = 128
ROPE_D = 64
Q_RANK = 512
KV_RANK = 256
WIDTH = HEADS * DH
ROPE_BASE = 10000.0
NORM_EPS = 1e-6
GN_EPS = 1e-5
NEG_INF = -1e30
ATT_SCALE = (DH + ROPE_D) ** -0.5
RET_SCALE = DH ** -0.5
IN_WIDTH = 10048
N_CHIPS = 4
IN_SHARD = IN_WIDTH // N_CHIPS
ADAM_LR, ADAM_B1, ADAM_B2, ADAM_EPS, ADAM_WD, ADAM_STEP = 0.001, 0.9, 0.999, 1e-08, 0.01, 10

R_Q, R_K, R_V, Z_RET, Z_MLA, GATE0, GATE1 = 0, 1024, 2048, 3072, 4096, 5120, 5120 + D
C_Q = 5120 + 2 * D
C_KV = C_Q + Q_RANK
K_PE = C_KV + KV_RANK
PROJ_W = 10240
IN_RUNS = ((0, 832, C_Q), (832, 1856, Z_MLA), (1856, 4928, R_Q), (4928, 5952, Z_RET), (5952, IN_WIDTH, GATE0))

BIG_IN, BIG_OUT, BIG_BM, BIG_BR, BIG_UQ, BIG_UKV, BIG_JUNK, BIG_ROWS = 0, 10240, 12288, 13312, 14336, 14720, 14976, 15360
ZERO_ROWS = PROJ_W - IN_WIDTH
PK_IN, PK_OUT, PK_BM, PK_BR, PK_UQ, PK_UKV, PK_PAD, PK_ROWS = 0, 2512, 3024, 3280, 3536, 3632, 3696, 3712
PK_HALF = PK_ROWS // 2
SM_LOSS, SM_NORM, SM_QN, SM_KVN, SM_GNW, SM_GNB, SM_FIN, SM_META, SM_ROWS = 0, 8, 24, 32, 40, 48, 56, 72, 328

VMEM_LIMIT = 56 * 1024 * 1024


def _pieces(shard):
    j = shard
    out = [(PK_OUT, BIG_OUT + 512 * j, 512), (PK_BM, BIG_BM + 256 * j, 256), (PK_BR, BIG_BR + 256 * j, 256),
           (PK_UQ, BIG_UQ + 96 * j, 96), (PK_UKV, BIG_UKV + 64 * j, 64)]
    for lo, hi, new in IN_RUNS:
        a, b = max(lo, IN_SHARD * j), min(hi, IN_SHARD * (j + 1))
        if b > a:
            out.append((PK_IN + a - IN_SHARD * j, BIG_IN + new + a - lo, b - a))
    out += [(PK_PAD, BIG_JUNK + 16 * j, 16)]
    return out


AG_JUNK_REST, AG_JUNK_IN, AG_JUNK_END = BIG_JUNK, BIG_JUNK + 128, BIG_JUNK + 192
AG_REST_HALF, AG_IN_HALF = 608, 1264


def _ag_half(shard, w_in, half):
    pieces = [(b, n) for p, b, n in _pieces(shard) if p < PK_PAD and (p < PK_OUT) == w_in]
    pieces.append((AG_JUNK_IN + 16 * shard, 16) if w_in else (AG_JUNK_REST + 32 * shard, 32))
    size = AG_IN_HALF if w_in else AG_REST_HALF
    out, pos = [], 0
    for b, n in pieces:
        s, e = max(pos, half * size), min(pos + n, (half + 1) * size)
        if e > s:
            out.append((s, b + s - pos, e - s))
        pos += n
    assert pos == 2 * size
    return out


def _grad_half(shard, half):
    j = shard
    pieces = [(0, b - BIG_IN, p, n) for p, b, n in _pieces(j) if p < PK_OUT]
    pieces += [(1, 512 * j, PK_OUT, 512), (2, 256 * j, PK_BM, 256), (3, 256 * j, PK_BR, 256), (4, 96 * j, PK_UQ, 96),
               (5, 64 * j, PK_UKV, 64), (0, IN_WIDTH, PK_PAD, PK_ROWS - PK_PAD)]
    lo, hi = half * PK_HALF, (half + 1) * PK_HALF
    out = []
    for k, r0, p, n in pieces:
        s, e = max(p, lo), min(p + n, hi)
        if e > s:
            out.append((k, r0 + s - p, s - lo, e - s))
    assert sum(n for _, _, _, n in out) == PK_HALF
    return out


def _row_tile(rows, cap):
    best = BLK
    for t in range(BLK, cap + 1, BLK):
        if rows % t == 0:
            best = t
    return best


def _cparams(sem):
    return pltpu.CompilerParams(dimension_semantics=sem, vmem_limit_bytes=VMEM_LIMIT)


def _dot(a, b, form):
    dn = {"nt": (((1,), (1,)), ((), ())), "nn": (((1,), (0,)), ((), ())), "tn": (((0,), (0,)), ((), ()))}[form]
    return lax.dot_general(a, b, dn, preferred_element_type=F32)


def _sigmoid(v):
    return 1.0 / (1.0 + jnp.exp(-v))


def _matmul(a, b, form, m, n, k, tm, tn, tk, out_dtype, name, a_off=(0, 0), b_off=(0, 0), n_outer=False, after=None):
    nk = k // tk
    gi, gj = m // tm, n // tn

    def ij(g0, g1):
        return (g1, g0) if n_outer else (g0, g1)

    if form == "nt":
        a_spec = pl.BlockSpec((tm, tk), lambda g0, g1, kk: (ij(g0, g1)[0] + a_off[0], kk + a_off[1]))
        b_spec = pl.BlockSpec((tn, tk), lambda g0, g1, kk: (ij(g0, g1)[1] + b_off[0], kk + b_off[1]))
    elif form == "nn":
        a_spec = pl.BlockSpec((tm, tk), lambda g0, g1, kk: (ij(g0, g1)[0] + a_off[0], kk + a_off[1]))
        b_spec = pl.BlockSpec((tk, tn), lambda g0, g1, kk: (kk + b_off[0], ij(g0, g1)[1] + b_off[1]))
    else:
        a_spec = pl.BlockSpec((tk, tm), lambda g0, g1, kk: (kk + a_off[0], ij(g0, g1)[0] + a_off[1]))
        b_spec = pl.BlockSpec((tk, tn), lambda g0, g1, kk: (kk + b_off[0], ij(g0, g1)[1] + b_off[1]))
    o_spec = pl.BlockSpec((tm, tn), lambda g0, g1, kk: ij(g0, g1))

    def body(a_ref, b_ref, *rest):
        o_ref, *acc = rest[0 if after is None else 1:]
        p = _dot(a_ref[...], b_ref[...], form)
        if nk == 1:
            o_ref[...] = p.astype(o_ref.dtype)
        else:
            acc_ref, = acc
            kk = pl.program_id(2)

            @pl.when(kk == 0)
            def _():
                acc_ref[...] = p

            @pl.when(kk > 0)
            def _():
                acc_ref[...] += p

            @pl.when(kk == nk - 1)
            def _():
                o_ref[...] = acc_ref[...].astype(o_ref.dtype)

    extra = [] if after is None else [after]
    return pl.pallas_call(
        body, name=name, grid=(gj, gi, nk) if n_outer else (gi, gj, nk),
        in_specs=[a_spec, b_spec] + [pl.BlockSpec(memory_space=pl.ANY)] * len(extra), out_specs=o_spec,
        out_shape=jax.ShapeDtypeStruct((m, n), out_dtype),
        scratch_shapes=[] if nk == 1 else [pltpu.VMEM((tm, tn), F32)],
        compiler_params=_cparams(("parallel", "parallel", "arbitrary")),
    )(a, b, *extra)


def _rms_in(x, hp0, norm_w, s):
    def body(x_ref, hp0_ref, w_ref, h_ref, xn_ref, r_ref):
        def run(hv):
            r = lax.rsqrt(jnp.mean(hv * hv, axis=-1, keepdims=True) + NORM_EPS)
            h_ref[...] = hv
            xn_ref[...] = (hv * r * w_ref[...]).astype(BF16)
            r_ref[...] = r

        @pl.when(pl.program_id(0) == 0)
        def _():
            run(hp0_ref[...])

        @pl.when(pl.program_id(0) > 0)
        def _():
            run(x_ref[...])

    return pl.pallas_call(
        body, name="rms_in", grid=(s // BLK,),
        in_specs=[pl.BlockSpec((BLK, D), lambda i: (jnp.maximum(i - 1, 0), 0)),
                  pl.BlockSpec((BLK, D), lambda i: (0, 0)), pl.BlockSpec((1, D), lambda i: (0, 0))],
        out_specs=[pl.BlockSpec((BLK, D), lambda i: (i, 0)), pl.BlockSpec((BLK, D), lambda i: (i, 0)),
                   pl.BlockSpec((BLK, 1), lambda i: (i, 0))],
        out_shape=[jax.ShapeDtypeStruct((s, D), F32), jax.ShapeDtypeStruct((s, D), BF16),
                   jax.ShapeDtypeStruct((s, 1), F32)],
        compiler_params=_cparams(("arbitrary",)),
    )(x, hp0, norm_w)


def _rope64(t, c, sa, sb):
    return t * c + pltpu.roll(t, t.shape[1] - ROPE_D // 2, 1) * sa + pltpu.roll(t, ROPE_D // 2, 1) * sb


def _rope128(t, c, sg):
    return t * c + pltpu.roll(t, DH // 2, 1) * sg


HD = 2 * DH


def _mla_prep(proj, qn_w, kvn_w, wq, wukv, qtab, ktab, qbias, kbias, s):
    tm = _row_tile(s, 384)

    def body(cq_ref, ckv_ref, kpe_ref, qnw_ref, kvnw_ref, wq_ref, wukv_ref, qc_ref, qa_ref, qb_ref,
             kc_ref, ka_ref, kb_ref, qbias_ref, kbias_ref, cqn_ref, rq_ref, ckvn_ref, rkv_ref, qq_ref, kk_ref, vv_ref):
        cq = cq_ref[...].astype(F32)
        rq = lax.rsqrt(jnp.mean(cq * cq, axis=-1, keepdims=True) + NORM_EPS)
        cqn = (cq * rq * qnw_ref[...]).astype(BF16)
        ckv = ckv_ref[...].astype(F32)
        rkv = lax.rsqrt(jnp.mean(ckv * ckv, axis=-1, keepdims=True) + NORM_EPS)
        ckvn = (ckv * rkv * kvnw_ref[...]).astype(BF16)
        cqn_ref[...] = cqn
        rq_ref[...] = rq
        ckvn_ref[...] = ckvn
        rkv_ref[...] = rkv
        q = _dot(cqn, wq_ref[...], "nt")
        kv = _dot(ckvn, wukv_ref[...], "nt")
        kp = (_rope64(kpe_ref[...].astype(F32), kc_ref[...], ka_ref[...], kb_ref[...]) + kbias_ref[...]).astype(BF16)
        qc, qa, qb, qbias = qc_ref[...], qa_ref[...], qb_ref[...], qbias_ref[...]
        ones = jnp.ones((tm, DH), BF16)
        for h in range(HEADS):
            lo, mid, hi = h * HD, h * HD + DH, (h + 1) * HD
            qq_ref[:, lo:hi] = (_rope64(q[:, lo:hi], qc, qa, qb) + qbias).astype(BF16)
            kk_ref[:, lo:mid] = kv[:, lo:mid].astype(BF16)
            kk_ref[:, mid:hi] = kp
            vv_ref[:, lo:mid] = kv[:, mid:hi].astype(BF16)
            vv_ref[:, mid:hi] = ones

    row = lambda w, cb: pl.BlockSpec((tm, w), lambda i: (i, cb))
    full = lambda a: pl.BlockSpec(a.shape, lambda i: (0, 0))
    wide = jax.ShapeDtypeStruct((s, HEADS * HD), BF16)
    return pl.pallas_call(
        body, name="mla_prep", grid=(s // tm,),
        in_specs=[row(Q_RANK, C_Q // Q_RANK), row(KV_RANK, C_KV // KV_RANK), row(DH, K_PE // DH),
                  full(qn_w), full(kvn_w), full(wq), full(wukv), row(HD, 0), row(HD, 0), row(HD, 0),
                  row(DH, 0), row(DH, 0), row(DH, 0), full(qbias), row(DH, 0)],
        out_specs=[row(Q_RANK, 0), row(1, 0), row(KV_RANK, 0), row(1, 0), row(HEADS * HD, 0), row(HEADS * HD, 0),
                   row(HEADS * HD, 0)],
        out_shape=[jax.ShapeDtypeStruct((s, Q_RANK), BF16), jax.ShapeDtypeStruct((s, 1), F32),
                   jax.ShapeDtypeStruct((s, KV_RANK), BF16), jax.ShapeDtypeStruct((s, 1), F32), wide, wide, wide],
        compiler_params=_cparams(("parallel",)),
    )(proj, proj, proj, qn_w, kvn_w, wq, wukv, *qtab, *ktab, qbias, kbias)


def _diag_mask(t):
    return lax.broadcasted_iota(jnp.int32, (t, t), 0) <= lax.broadcasted_iota(jnp.int32, (t, t), 1)


def _silu(z):
    return z * _sigmoid(z)


def _attn_fwd(qq, kk, vv, proj, s):
    t = _row_tile(s, 384)
    n = s // t

    def body(q_ref, k_ref, v_ref, z_ref, o_ref, y_ref, lse_ref, acc_ref, m_ref):
        qi = pl.program_id(1)
        q = q_ref[...]
        m_ref[...] = jnp.full(m_ref.shape, NEG_INF, F32)
        acc_ref[...] = jnp.zeros(acc_ref.shape, F32)

        def keys(ki):
            return pl.ds(pl.multiple_of(ki * t, t), t)

        def scores(ki):
            return _dot(k_ref[keys(ki), :], q, "nt")

        def tile(ki, st):
            m_old = m_ref[...]
            m_new = jnp.maximum(m_old, jnp.max(st, axis=0, keepdims=True))
            pt = jnp.exp(st - m_new).astype(BF16)
            acc_ref[...] = jnp.exp(m_old - m_new) * acc_ref[...] + _dot(v_ref[keys(ki), :], pt, "tn")
            m_ref[...] = m_new

        def inner(ki, st):
            st_after = scores(ki + 1)
            tile(ki, st)
            return st_after

        st_diag = lax.fori_loop(0, qi, inner, scores(0))
        tile(qi, jnp.where(_diag_mask(t), st_diag, NEG_INF))

        l = acc_ref[DH:DH + 1, :]
        o = (acc_ref[:DH, :] / l).T
        o_ref[...] = o.astype(BF16)
        y_ref[...] = (o * _silu(z_ref[...].astype(F32))).astype(BF16)
        lse_ref[0, 0] = m_ref[...] + jnp.log(l)

    qtile = pl.BlockSpec((t, DH), lambda h, i: (i, h))
    head = pl.BlockSpec((s, HD), lambda h, i: (0, h))
    return pl.pallas_call(
        body, name="attn_fwd", grid=(HEADS, s // t),
        in_specs=[pl.BlockSpec((t, HD), lambda h, i: (i, h)), head, head,
                  pl.BlockSpec((t, DH), lambda h, i: (i, Z_MLA // DH + h))],
        out_specs=[qtile, qtile, pl.BlockSpec((1, 1, 1, t), lambda h, i: (h, i, 0, 0))],
        out_shape=[jax.ShapeDtypeStruct((s, WIDTH), BF16), jax.ShapeDtypeStruct((s, WIDTH), BF16),
                   jax.ShapeDtypeStruct((HEADS, s // t, 1, t), F32)],
        scratch_shapes=[pltpu.VMEM((HD, t), F32), pltpu.VMEM((1, t), F32)],
        compiler_params=_cparams(("parallel", "arbitrary")),
    )(qq, kk, vv, proj)


RET_CHUNKS = 3


def _ret_consts():
    log_g = np.log1p(-(2.0 ** (-5.0 - np.arange(HEADS, dtype=np.float64))))
    n = np.arange(BLK, dtype=np.float64)
    diff = n[:, None] - n[None, :]
    decay = np.where(diff >= 0, np.exp(log_g[:, None, None] * np.maximum(diff, 0.0)), 0.0)
    zeta = np.exp(log_g[:, None] * (BLK - 1.0 - n))[:, :, None]
    xi = np.exp(log_g[:, None] * (n + 1.0))[:, :, None]
    gamma = [float(np.float32(np.exp(g * BLK))) for g in log_g]
    wide = lambda a: np.broadcast_to(a, (HEADS, BLK, DH))
    return ((jnp.asarray(decay, F32), jnp.asarray(wide(zeta), F32), jnp.asarray(wide(xi), F32)),
            (jnp.asarray(decay, F32), jnp.asarray(zeta, F32), jnp.asarray(xi, F32)), gamma)


def _ret_fwd(proj, gn_w, gn_b, c128, s128, consts, gamma, s):
    nb = s // BLK
    cps = RET_CHUNKS if nb % RET_CHUNKS == 0 else 1
    decay, zeta, xi = consts

    def body(rq_ref, rk_ref, rv_ref, z_ref, gw_ref, gb_ref, c_ref, s_ref, dm_ref, zt_ref, xi_ref,
             y_ref, on_ref, rstd_ref, qr_ref, kr_ref, rall_ref, state):
        @pl.when(pl.program_id(0) == 0)
        def _():
            state[...] = jnp.zeros_like(state)

        for sub in range(cps):
            rows = slice(sub * BLK, (sub + 1) * BLK)
            c, sg = c_ref[rows, :], s_ref[rows, :]
            for h in range(HEADS):
                sl = slice(h * DH, (h + 1) * DH)
                q = _rope128(rq_ref[rows, sl].astype(F32), c, sg).astype(BF16)
                kf = _rope128(rk_ref[rows, sl].astype(F32), c, sg) * RET_SCALE
                k = kf.astype(BF16)
                v = rv_ref[rows, sl]
                qr_ref[rows, sl] = q
                kr_ref[rows, sl] = k
                r_prev = state[h]
                rall_ref[sub, h] = r_prev
                a = _dot(q, k, "nt") * dm_ref[h]
                o = _dot(a.astype(BF16), v, "nn") + _dot(q, r_prev.astype(BF16), "nn") * xi_ref[h]
                state[h] = r_prev * gamma[h] + _dot((kf * zt_ref[h]).astype(BF16), v, "tn")
                mu = jnp.mean(o, axis=-1, keepdims=True)
                var = jnp.mean(jnp.square(o - mu), axis=-1, keepdims=True)
                rstd = lax.rsqrt(var + GN_EPS)
                on = (o - mu) * rstd
                rstd_ref[h, rows, :] = rstd
                on_ref[rows, sl] = on.astype(BF16)
                ogn = on * gw_ref[:, sl] + gb_ref[:, sl]
                y_ref[rows, sl] = (ogn * _silu(z_ref[rows, sl].astype(F32))).astype(BF16)

    seg = lambda cb: pl.BlockSpec((cps * BLK, WIDTH), lambda i: (i, cb))
    full = lambda a: pl.BlockSpec(a.shape, lambda i: (0,) * a.ndim)
    tab = pl.BlockSpec((cps * BLK, DH), lambda i: (i, 0))
    return pl.pallas_call(
        body, name="ret_fwd", grid=(nb // cps,),
        in_specs=[seg(R_Q // WIDTH), seg(R_K // WIDTH), seg(R_V // WIDTH), seg(Z_RET // WIDTH), full(gn_w), full(gn_b),
                  tab, tab, full(decay), full(zeta), full(xi)],
        out_specs=[seg(0), seg(0), pl.BlockSpec((HEADS, cps * BLK, 1), lambda i: (0, i, 0)), seg(0), seg(0),
                   pl.BlockSpec((cps, HEADS, DH, DH), lambda i: (i, 0, 0, 0))],
        out_shape=[jax.ShapeDtypeStruct((s, WIDTH), BF16), jax.ShapeDtypeStruct((s, WIDTH), BF16),
                   jax.ShapeDtypeStruct((HEADS, s, 1), F32), jax.ShapeDtypeStruct((s, WIDTH), BF16),
                   jax.ShapeDtypeStruct((s, WIDTH), BF16), jax.ShapeDtypeStruct((nb, HEADS, DH, DH), F32)],
        scratch_shapes=[pltpu.VMEM((HEADS, DH, DH), F32)],
        compiler_params=_cparams(("arbitrary",)),
    )(proj, proj, proj, proj, gn_w, gn_b, c128, s128, decay, zeta, xi)


def _merge_fwd(y_mla, y_ret, wbm, wbr, proj, s):
    tm, tn = _row_tile(s, 1408), 512

    def body(ym_ref, yr_ref, wm_ref, wr_ref, g0_ref, g1_ref, um_ref, ur_ref, mg_ref):
        um = _dot(ym_ref[...], wm_ref[...], "nt")
        ur = _dot(yr_ref[...], wr_ref[...], "nt")
        um_ref[...] = um.astype(BF16)
        ur_ref[...] = ur.astype(BF16)
        mg_ref[...] = (_sigmoid(g0_ref[...].astype(F32)) * um + _sigmoid(g1_ref[...].astype(F32)) * ur).astype(BF16)

    yspec = pl.BlockSpec((tm, WIDTH), lambda i, j: (i, 0))
    wspec = pl.BlockSpec((tn, WIDTH), lambda i, j: (j, 0))
    ospec = pl.BlockSpec((tm, tn), lambda i, j: (i, j))
    return pl.pallas_call(
        body, name="merge_fwd", grid=(s // tm, D // tn),
        in_specs=[yspec, yspec, wspec, wspec, pl.BlockSpec((tm, tn), lambda i, j: (i, GATE0 // tn + j)),
                  pl.BlockSpec((tm, tn), lambda i, j: (i, GATE1 // tn + j))],
        out_specs=[ospec, ospec, ospec],
        out_shape=[jax.ShapeDtypeStruct((s, D), BF16)] * 3,
        compiler_params=_cparams(("parallel", "parallel")),
    )(y_mla, y_ret, wbm, wbr, proj, proj)


def _out_fwd(merged, big, h, s):
    tm, tn = _row_tile(s, 1408), 512

    def body(m_ref, w_ref, h_ref, o_ref):
        o_ref[...] = h_ref[...] + _dot(m_ref[...], w_ref[...], "nn")

    return pl.pallas_call(
        body, name="out_fwd", grid=(s // tm, D // tn),
        in_specs=[pl.BlockSpec((tm, D), lambda i, j: (i, 0)), pl.BlockSpec((D, tn), lambda i, j: (BIG_OUT // D, j)),
                  pl.BlockSpec((tm, tn), lambda i, j: (i, j))],
        out_specs=pl.BlockSpec((tm, tn), lambda i, j: (i, j)),
        out_shape=jax.ShapeDtypeStruct((s, D), F32),
        compiler_params=_cparams(("parallel", "parallel")),
    )(merged, big, h)


def _loss_bwd(h2, target, fin_w, s):
    nb = s // BLK

    def body(h2_ref, t_ref, w_ref, dh_ref, dhb_ref, loss_ref, dw_ref):
        i = pl.program_id(0)

        @pl.when(i == 0)
        def _():
            dh_ref[...] = jnp.zeros_like(dh_ref)
            dhb_ref[...] = jnp.zeros_like(dhb_ref)
            loss_ref[...] = jnp.zeros_like(loss_ref)
            dw_ref[...] = jnp.zeros_like(dw_ref)

        @pl.when(i > 0)
        def _():
            hv = h2_ref[...]
            w = w_ref[...]
            r = lax.rsqrt(jnp.mean(hv * hv, axis=-1, keepdims=True) + NORM_EPS)
            nrm = hv * r
            e = nrm * w - t_ref[...]
            loss_ref[...] += jnp.full(loss_ref.shape, 0.5 / D, F32) * jnp.sum(e * e)
            dy = e * (1.0 / D)
            dw_ref[...] += jnp.sum(dy * nrm, axis=0, keepdims=True)
            g = dy * w
            dh = r * (g - nrm * jnp.mean(g * nrm, axis=-1, keepdims=True))
            dh_ref[...] = dh
            dhb_ref[...] = dh.astype(BF16)

    blk = pl.BlockSpec((BLK, D), lambda i: (i, 0))
    return pl.pallas_call(
        body, name="loss_bwd", grid=(nb,),
        in_specs=[blk, pl.BlockSpec((BLK, D), lambda i: (jnp.maximum(i - 1, 0), 0)), pl.BlockSpec((1, D), lambda i: (0, 0))],
        out_specs=[blk, blk, pl.BlockSpec((8, 128), lambda i: (0, 0)), pl.BlockSpec((1, D), lambda i: (0, 0))],
        out_shape=[jax.ShapeDtypeStruct((s, D), F32), jax.ShapeDtypeStruct((s, D), BF16),
                   jax.ShapeDtypeStruct((8, 128), F32), jax.ShapeDtypeStruct((1, D), F32)],
        compiler_params=_cparams(("arbitrary",)),
    )(h2, target, fin_w)


def _merge_bwd(dh2b, big, u_mla, u_ret, proj, s):
    tm, tn = _row_tile(s, 1408), 512

    def body(d_ref, w_ref, um_ref, ur_ref, gate_ref, dproj_ref, dum_ref, dur_ref, dm_ref):
        branch, j = pl.program_id(1), pl.program_id(2)
        cols = pl.ds(pl.multiple_of(j * tn, tn), tn)

        @pl.when(branch == 0)
        def _():
            dm_ref[:, cols] = _dot(d_ref[...], w_ref[...], "nt")

        dm = dm_ref[:, cols]
        gt = _sigmoid(gate_ref[...].astype(F32))

        @pl.when(branch == 0)
        def _():
            dproj_ref[...] = (dm * um_ref[...].astype(F32) * gt * (1.0 - gt)).astype(BF16)
            dum_ref[...] = (dm * gt).astype(BF16)

        @pl.when(branch == 1)
        def _():
            dproj_ref[...] = (dm * ur_ref[...].astype(F32) * gt * (1.0 - gt)).astype(BF16)
            dur_ref[...] = (dm * gt).astype(BF16)

    last = D // tn - 1
    mla = pl.BlockSpec((tm, tn), lambda i, b, j: (i, jnp.where(b == 0, j, last)))
    ret = pl.BlockSpec((tm, tn), lambda i, b, j: (i, jnp.where(b == 0, 0, j)))
    gate = pl.BlockSpec((tm, tn), lambda i, b, j: (i, GATE0 // tn + b * (D // tn) + j))
    return pl.pallas_call(
        body, name="merge_bwd", grid=(s // tm, 2, D // tn),
        in_specs=[pl.BlockSpec((tm, D), lambda i, b, j: (i, 0)),
                  pl.BlockSpec((tn, D), lambda i, b, j: (BIG_OUT // tn + jnp.where(b == 0, j, last), 0)), mla, ret, gate],
        out_specs=[gate, mla, ret],
        out_shape=[jax.ShapeDtypeStruct((s, PROJ_W), BF16), jax.ShapeDtypeStruct((s, D), BF16),
                   jax.ShapeDtypeStruct((s, D), BF16)],
        scratch_shapes=[pltpu.VMEM((tm, D), F32)],
        compiler_params=_cparams(("parallel", "arbitrary", "arbitrary")),
    )(dh2b, big, u_mla, u_ret, proj)


def _dsilu(z):
    sg = _sigmoid(z)
    return sg * (1.0 + z * (1.0 - sg))


def _branch_mla_bwd(dproj, du, wbm, o_mla, proj, s):
    tm = _row_tile(s, 384)

    def body(dproj_in, du_ref, w_ref, o_ref, z_ref, dz_ref, do_ref, delta_ref):
        del dproj_in
        dy = _dot(du_ref[...], w_ref[...], "nn")
        z = z_ref[...].astype(F32)
        o = o_ref[...].astype(F32)
        do = dy * _silu(z)
        do_ref[...] = do.astype(BF16)
        dz_ref[...] = (dy * o * _dsilu(z)).astype(BF16)
        prod = do * o
        for h in range(HEADS):
            delta_ref[h] = jnp.sum(prod[:, h * DH:(h + 1) * DH], axis=-1, keepdims=True)

    row = lambda w, cb: pl.BlockSpec((tm, w), lambda i: (i, cb))
    return pl.pallas_call(
        body, name="branch_mla_bwd", grid=(s // tm,),
        in_specs=[ANY, row(D, 0), pl.BlockSpec((D, WIDTH), lambda i: (0, 0)), row(WIDTH, 0),
                  row(WIDTH, Z_MLA // WIDTH)],
        out_specs=[row(WIDTH, Z_MLA // WIDTH), row(WIDTH, 0), pl.BlockSpec((HEADS, tm, 1), lambda i: (0, i, 0))],
        out_shape=[jax.ShapeDtypeStruct((s, PROJ_W), BF16), jax.ShapeDtypeStruct((s, WIDTH), BF16),
                   jax.ShapeDtypeStruct((HEADS, s, 1), F32)],
        input_output_aliases={0: 0},
        compiler_params=_cparams(("parallel",)),
    )(dproj, du, wbm, o_mla, proj)


def _branch_ret_bwd(dproj, du, wbr, on, rstd, gn_w, gn_b, proj, s):
    tm = _row_tile(s, 384)

    def body(dproj_in, du_ref, w_ref, on_ref, rstd_ref, gw_ref, gb_ref, z_ref, dz_ref, do_ref, dgw_ref, dgb_ref):
        del dproj_in

        @pl.when(pl.program_id(0) == 0)
        def _():
            dgw_ref[...] = jnp.zeros_like(dgw_ref)
            dgb_ref[...] = jnp.zeros_like(dgb_ref)

        dy = _dot(du_ref[...], w_ref[...], "nn")
        z = z_ref[...].astype(F32)
        on = on_ref[...].astype(F32)
        gw = gw_ref[...]
        dogn = dy * _silu(z)
        dz_ref[...] = (dy * (on * gw + gb_ref[...]) * _dsilu(z)).astype(BF16)
        dgw_ref[...] += jnp.sum(dogn * on, axis=0, keepdims=True)
        dgb_ref[...] += jnp.sum(dogn, axis=0, keepdims=True)
        don = dogn * gw
        for h in range(HEADS):
            sl = slice(h * DH, (h + 1) * DH)
            dn, nh = don[:, sl], on[:, sl]
            do = rstd_ref[h] * (dn - jnp.mean(dn, axis=-1, keepdims=True)
                                - nh * jnp.mean(dn * nh, axis=-1, keepdims=True))
            do_ref[:, sl] = do.astype(BF16)

    row = lambda w, cb: pl.BlockSpec((tm, w), lambda i: (i, cb))
    vec = pl.BlockSpec((1, WIDTH), lambda i: (0, 0))
    return pl.pallas_call(
        body, name="branch_ret_bwd", grid=(s // tm,),
        in_specs=[ANY, row(D, 0), pl.BlockSpec((D, WIDTH), lambda i: (0, 0)), row(WIDTH, 0),
                  pl.BlockSpec((HEADS, tm, 1), lambda i: (0, i, 0)), vec, vec, row(WIDTH, Z_RET // WIDTH)],
        out_specs=[row(WIDTH, Z_RET // WIDTH), row(WIDTH, 0), vec, vec],
        out_shape=[jax.ShapeDtypeStruct((s, PROJ_W), BF16), jax.ShapeDtypeStruct((s, WIDTH), BF16)]
        + [jax.ShapeDtypeStruct((1, WIDTH), F32)] * 2,
        input_output_aliases={0: 0},
        compiler_params=_cparams(("arbitrary",)),
    )(dproj, du, wbr, on, rstd, gn_w, gn_b, proj)


def _ret_bwd(dproj, qr, kr, proj, rall, do_ret, c128, s128, consts, gamma, s):
    cps = RET_CHUNKS if (s // BLK) % RET_CHUNKS == 0 else 1
    nb = s // (cps * BLK)
    decay, zeta, xi = consts

    def body(dproj_in, q_ref, k_ref, v_ref, r_ref, do_ref, c_ref, s_ref, dm_ref, zt_ref, xi_ref, out_ref, gstate):
        del dproj_in
        dq_ref, dk_ref, dv_ref = (out_ref.at[:, pl.ds(off, WIDTH)] for off in (R_Q, R_K, R_V))

        @pl.when(pl.program_id(0) == 0)
        def _():
            gstate[...] = jnp.zeros_like(gstate)

        for sub in reversed(range(cps)):
            rows = slice(sub * BLK, (sub + 1) * BLK)
            c, sg = c_ref[rows, :], s_ref[rows, :]
            for h in range(HEADS):
                sl = slice(h * DH, (h + 1) * DH)
                q, k, v, do = q_ref[rows, sl], k_ref[rows, sl], v_ref[rows, sl], do_ref[rows, sl]
                dm = dm_ref[h]
                g_next = gstate[h]
                gb = g_next.astype(BF16)
                a = (_dot(q, k, "nt") * dm).astype(BF16)
                da = (_dot(do, v, "nt") * dm).astype(BF16)
                dox = (do.astype(F32) * xi_ref[h]).astype(BF16)
                dq = _dot(da, k, "nn") + _dot(dox, r_ref[sub, h].astype(BF16), "nt")
                dk = _dot(da, q, "tn") + _dot(v, gb, "nt") * zt_ref[h]
                kz = (k.astype(F32) * zt_ref[h]).astype(BF16)
                dv = _dot(a, do, "tn") + _dot(kz, gb, "nn")
                gstate[h] = g_next * gamma[h] + _dot(q, dox, "tn")
                dk = dk * RET_SCALE
                dq_ref[rows, sl] = _rope128(dq, c, -sg).astype(BF16)
                dk_ref[rows, sl] = _rope128(dk, c, -sg).astype(BF16)
                dv_ref[rows, sl] = dv.astype(BF16)

    rev = lambda cb: pl.BlockSpec((cps * BLK, WIDTH), lambda i: (nb - 1 - i, cb))
    full = lambda a: pl.BlockSpec(a.shape, lambda i: (0,) * a.ndim)
    tab = pl.BlockSpec((cps * BLK, DH), lambda i: (nb - 1 - i, 0))
    return pl.pallas_call(
        body, name="ret_bwd", grid=(nb,),
        in_specs=[ANY, rev(0), rev(0), rev(R_V // WIDTH),
                  pl.BlockSpec((cps, HEADS, DH, DH), lambda i: (nb - 1 - i, 0, 0, 0)),
                  rev(0), tab, tab, full(decay), full(zeta), full(xi)],
        out_specs=pl.BlockSpec((cps * BLK, 3 * WIDTH), lambda i: (nb - 1 - i, R_Q // (3 * WIDTH))),
        out_shape=jax.ShapeDtypeStruct((s, PROJ_W), BF16),
        scratch_shapes=[pltpu.VMEM((HEADS, DH, DH), F32)],
        input_output_aliases={0: 0},
        compiler_params=_cparams(("arbitrary",)),
    )(dproj, qr, kr, proj, rall, do_ret, c128, s128, decay, zeta, xi)


def _attn_bwd(qq, kk, vv, do, lse, delta, s):
    t = _row_tile(s, 384)
    n = s // t

    def body(q_ref, k_ref, v_ref, do_ref, lse_ref, delta_ref, dq_ref, dk_ref, dv_ref, dk_acc, dv_acc):
        ki = pl.program_id(1)

        @pl.when(ki == 0)
        def _():
            dq_ref[...] = jnp.zeros(dq_ref.shape, F32)

        k, v = k_ref[...], v_ref[...]
        dk_acc[...] = jnp.zeros(dk_acc.shape, F32)
        dv_acc[...] = jnp.zeros(dv_acc.shape, F32)

        def rows(qi):
            return pl.ds(pl.multiple_of(qi * t, t), t)

        def products(qi):
            return _dot(k, q_ref[rows(qi), :], "nt"), _dot(v, do_ref[rows(qi), :], "nt")

        def tile(qi, st, dpt):
            q, dov = q_ref[rows(qi), :], do_ref[rows(qi), :]
            pt = jnp.exp(st - lse_ref[0, qi])
            dv_acc[...] += _dot(pt.astype(BF16), dov, "nn")
            dst = (pt * (dpt - delta_ref[0, qi])).astype(BF16)
            dk_acc[...] += _dot(dst, q, "nn")
            dq_ref[rows(qi), :] += _dot(dst, k, "tn")

        def inner(qi, carry):
            after = products(jnp.minimum(qi + 1, n - 1))
            tile(qi, *carry)
            return after

        st, dpt = products(ki)
        ahead = products(jnp.minimum(ki + 1, n - 1))
        tile(ki, jnp.where(_diag_mask(t), st, NEG_INF), dpt)
        lax.fori_loop(ki + 1, n, inner, ahead)

        dk_ref[...] = dk_acc[...].astype(BF16)
        dv_ref[...] = dv_acc[...].astype(BF16)

    head = lambda w: pl.BlockSpec((s, w), lambda h, i: (0, h))
    stat = pl.BlockSpec((1, n, 1, t), lambda h, i: (h, 0, 0, 0))
    return pl.pallas_call(
        body, name="attn_bwd", grid=(HEADS, n),
        in_specs=[head(HD), pl.BlockSpec((t, HD), lambda h, i: (i, h)), pl.BlockSpec((t, DH), lambda h, i: (i, 2 * h)),
                  head(DH), stat, stat],
        out_specs=[head(HD), pl.BlockSpec((t, HD), lambda h, i: (i, h)), pl.BlockSpec((t, DH), lambda h, i: (i, h))],
        out_shape=[jax.ShapeDtypeStruct((s, HEADS * HD), F32), jax.ShapeDtypeStruct((s, HEADS * HD), BF16),
                   jax.ShapeDtypeStruct((s, WIDTH), BF16)],
        scratch_shapes=[pltpu.VMEM((t, HD), F32), pltpu.VMEM((t, DH), F32)],
        compiler_params=_cparams(("parallel", "arbitrary")),
    )(qq, kk, vv, do, lse, delta)


def _rms_bwd(dn, nhat, r, w):
    g = dn * w
    return r * (g - nhat * jnp.mean(g * nhat, axis=-1, keepdims=True)), dn * nhat


def _mla_prep_bwd(dproj, dqq, dkk, dvv, proj, rq, rkv, qn_w, kvn_w, wq, wukv, qtab, ktab, s):
    tm = _row_tile(s, 384)
    tail = PROJ_W - C_Q

    def body(dproj_in, dqq_ref, dkk_ref, dvv_ref, cq_ref, ckv_ref, rq_ref, rkv_ref, qnw_ref, kvnw_ref, wq_ref,
             wukv_ref, qc_ref, qa_ref, qb_ref, kc_ref, ka_ref, kb_ref,
             tail_ref, dq_ref, dkv_ref, dqnw_ref, dkvnw_ref):
        del dproj_in
        dcq_ref = tail_ref.at[:, pl.ds(0, Q_RANK)]
        dckv_ref = tail_ref.at[:, pl.ds(C_KV - C_Q, KV_RANK)]
        dkpe_ref = tail_ref.at[:, pl.ds(K_PE - C_Q, 2 * DH)]

        @pl.when(pl.program_id(0) == 0)
        def _():
            dqnw_ref[...] = jnp.zeros_like(dqnw_ref)
            dkvnw_ref[...] = jnp.zeros_like(dkvnw_ref)

        qc, qa, qb = qc_ref[...], qa_ref[...], qb_ref[...]
        dkp = jnp.zeros((tm, DH), F32)
        for h in range(HEADS):
            lo, mid, hi = h * HD, h * HD + DH, (h + 1) * HD
            dq_ref[:, lo:hi] = _rope64(dqq_ref[:, lo:hi], qc, -qa, -qb).astype(BF16)
            dkv_ref[:, lo:mid] = dkk_ref[:, lo:mid]
            dkv_ref[:, mid:hi] = dvv_ref[:, h * DH:(h + 1) * DH]
            dkp = dkp + dkk_ref[:, mid:hi].astype(F32)
        dcqn = _dot(dq_ref[...], wq_ref[...], "nn")
        rq_v = rq_ref[...]
        dcq, prod = _rms_bwd(dcqn, cq_ref[...].astype(F32) * rq_v, rq_v, qnw_ref[...])
        dcq_ref[...] = dcq.astype(BF16)
        dqnw_ref[...] += jnp.sum(prod, axis=0, keepdims=True)
        dckvn = _dot(dkv_ref[...], wukv_ref[...], "nn")
        rkv_v = rkv_ref[...]
        dckv, prod = _rms_bwd(dckvn, ckv_ref[...].astype(F32) * rkv_v, rkv_v, kvnw_ref[...])
        dckv_ref[...] = dckv.astype(BF16)
        dkvnw_ref[...] += jnp.sum(prod, axis=0, keepdims=True)
        dkpe_ref[:, :DH] = _rope64(dkp, kc_ref[...], -ka_ref[...], -kb_ref[...]).astype(BF16)
        dkpe_ref[:, DH:] = jnp.zeros((tm, DH), BF16)

    row = lambda w, cb: pl.BlockSpec((tm, w), lambda i: (i, cb))
    full = lambda a: pl.BlockSpec(a.shape, lambda i: (0, 0))
    wide = jax.ShapeDtypeStruct((s, HEADS * HD), BF16)
    return pl.pallas_call(
        body, name="mla_prep_bwd", grid=(s // tm,),
        in_specs=[ANY, row(HEADS * HD, 0), row(HEADS * HD, 0), row(WIDTH, 0),
                  row(Q_RANK, C_Q // Q_RANK), row(KV_RANK, C_KV // KV_RANK), row(1, 0), row(1, 0),
                  full(qn_w), full(kvn_w), full(wq), full(wukv), row(HD, 0), row(HD, 0), row(HD, 0),
                  row(DH, 0), row(DH, 0), row(DH, 0)],
        out_specs=[row(tail, C_Q // tail), row(HEADS * HD, 0), row(HEADS * HD, 0),
                   pl.BlockSpec((1, Q_RANK), lambda i: (0, 0)), pl.BlockSpec((1, KV_RANK), lambda i: (0, 0))],
        out_shape=[jax.ShapeDtypeStruct((s, PROJ_W), BF16), wide, wide,
                   jax.ShapeDtypeStruct((1, Q_RANK), F32), jax.ShapeDtypeStruct((1, KV_RANK), F32)],
        input_output_aliases={0: 0},
        compiler_params=_cparams(("arbitrary",)),
    )(dproj, dqq, dkk, dvv, proj, proj, rq, rkv, qn_w, kvn_w, wq, wukv, *qtab, *ktab)


def _rms_in_bwd(dxn, h, r, dh2, norm_w, s):
    def body(dxn_ref, h_ref, r_ref, dh2_ref, w_ref, gx_ref, gm_ref, dw_ref):
        i = pl.program_id(0)
        r_v = r_ref[...]
        dx, prod = _rms_bwd(dxn_ref[...], h_ref[...] * r_v, r_v, w_ref[...])
        dh = dh2_ref[...] + dx

        @pl.when(i == 0)
        def _():
            gm_ref[...] = dh
            dw_ref[...] = jnp.sum(prod, axis=0, keepdims=True)

        @pl.when(i > 0)
        def _():
            gx_ref[...] = dh
            dw_ref[...] += jnp.sum(prod, axis=0, keepdims=True)

    blk = pl.BlockSpec((BLK, D), lambda i: (i, 0))
    return pl.pallas_call(
        body, name="rms_in_bwd", grid=(s // BLK,),
        in_specs=[blk, blk, pl.BlockSpec((BLK, 1), lambda i: (i, 0)), blk, pl.BlockSpec((1, D), lambda i: (0, 0))],
        out_specs=[pl.BlockSpec((BLK, D), lambda i: (jnp.maximum(i - 1, 0), 0)), pl.BlockSpec((BLK, D), lambda i: (0, 0)),
                   pl.BlockSpec((1, D), lambda i: (0, 0))],
        out_shape=[jax.ShapeDtypeStruct((s - BLK, D), F32), jax.ShapeDtypeStruct((BLK, D), F32),
                   jax.ShapeDtypeStruct((1, D), F32)],
        compiler_params=_cparams(("arbitrary",)),
    )(dxn, h, r, dh2, norm_w)


def _adam_math(w, g, m, v):
    m = ADAM_B1 * m + (1.0 - ADAM_B1) * g
    v = ADAM_B2 * v + (1.0 - ADAM_B2) * (g * g)
    m_hat = m / (1.0 - ADAM_B1 ** ADAM_STEP)
    v_hat = v / (1.0 - ADAM_B2 ** ADAM_STEP)
    return -ADAM_LR * (m_hat / (jnp.sqrt(v_hat) + ADAM_EPS) + ADAM_WD * w), m, v


def _adamw(w, g, m, v, name):
    rows, cols = w.shape
    tr = rows
    for cand in (128, 64, 32, 16, 8):
        if rows % cand == 0:
            tr = cand
            break

    def body(w_ref, g_ref, m_ref, v_ref, d_ref, nm_ref, nv_ref):
        d_ref[...], nm_ref[...], nv_ref[...] = _adam_math(w_ref[...], g_ref[...], m_ref[...], v_ref[...])

    spec = pl.BlockSpec((tr, cols), lambda i: (i, 0))
    return pl.pallas_call(
        body, name=name, grid=(rows // tr,), in_specs=[spec] * 4, out_specs=[spec] * 3,
        out_shape=[jax.ShapeDtypeStruct((rows, cols), F32)] * 3,
        compiler_params=_cparams(("parallel",)),
    )(w, g, m, v)


def _adamw_t(w_t, g_t, m_t, v_t, name):
    cols, rows = w_t.shape

    def body(w_ref, g_ref, m_ref, v_ref, go_ref, d_ref, nm_ref, nv_ref):
        g = g_ref[...]
        go_ref[...] = g
        d_ref[...], nm_ref[...], nv_ref[...] = _adam_math(w_ref[...], g, m_ref[...], v_ref[...])

    blk = pl.BlockSpec((128, rows), lambda k: (k, 0))
    return pl.pallas_call(
        body, name=name, grid=(pl.cdiv(cols, 128),), in_specs=[blk] * 4, out_specs=[blk] * 4,
        out_shape=[jax.ShapeDtypeStruct((cols, rows), F32)] * 4, compiler_params=_cparams(("parallel",)),
    )(w_t, g_t, m_t, v_t)


def _adamw_small(w, gall, m, v):
    def body(w_ref, g_ref, m_ref, v_ref, gs_ref, d_ref, nm_ref, nv_ref):
        g = g_ref[0]
        for dev in range(1, 8):
            g = g + g_ref[dev]
        gs_ref[...] = g
        d_ref[...], nm_ref[...], nv_ref[...] = _adam_math(w_ref[...], g, m_ref[...], v_ref[...])

    return pl.pallas_call(
        body, name="adamw_small", out_shape=[jax.ShapeDtypeStruct((SM_ROWS, 128), F32)] * 4,
        compiler_params=pltpu.CompilerParams(vmem_limit_bytes=VMEM_LIMIT),
    )(w, gall, m, v)


ADD_ROWS = 464
HALF_BLOCKS = PK_HALF // ADD_ROWS


def _rs_core_add(srcs, land, pos):
    def body(pos_ref, *refs):
        src_refs, (land_ref, out_ref, own, sems) = refs[:len(srcs)], refs[len(srcs):]
        step = pl.program_id(0)

        def fetch(s):
            for hf in range(2):
                @pl.when(pos_ref[1] == hf)
                def _():
                    for k, r0, p, n in _grad_half(s, hf):
                        pltpu.make_async_copy(src_refs[k].at[pl.ds(r0, n), :], own.at[s % 2, pl.ds(p, n), :],
                                              sems.at[s % 2]).start()

        @pl.when(step == 0)
        def _():
            fetch(0)

        for s in range(N_CHIPS):
            @pl.when(step == s)
            def _():
                if s + 1 < N_CHIPS:
                    fetch(s + 1)
                slot = own.at[s % 2]
                pltpu.make_async_copy(slot, slot, sems.at[s % 2]).wait()
                out_ref[0] = (slot[...].astype(F32) + land_ref[0].astype(F32)).astype(BF16)

    blk = pl.BlockSpec((1, PK_HALF, D), lambda s, pos_ref: (s, 0, 0))
    return pl.pallas_call(
        body, name="rs_core_add",
        grid_spec=pltpu.PrefetchScalarGridSpec(
            num_scalar_prefetch=1, grid=(N_CHIPS,), in_specs=[ANY] * len(srcs) + [blk], out_specs=blk,
            scratch_shapes=[pltpu.VMEM((2, PK_HALF, D), BF16), pltpu.SemaphoreType.DMA((2,))]),
        out_shape=jax.ShapeDtypeStruct((N_CHIPS, PK_HALF, D), BF16),
        compiler_params=_cparams(("arbitrary",)),
    )(pos, *srcs, land)


def _rs_chip_add(cp, land, pos):
    def body(pos_ref, a_ref, l_ref, o_ref):
        o_ref[0] = ((a_ref[0].astype(F32) + l_ref[0].astype(F32)) + l_ref[1].astype(F32)) + l_ref[2].astype(F32)

    return pl.pallas_call(
        body, name="rs_chip_add",
        grid_spec=pltpu.PrefetchScalarGridSpec(
            num_scalar_prefetch=1, grid=(HALF_BLOCKS,),
            in_specs=[pl.BlockSpec((1, ADD_ROWS, D), lambda i, pos_ref: (pos_ref[0], i, 0)),
                      pl.BlockSpec((3, ADD_ROWS, D), lambda i, pos_ref: (0, i, 0))],
            out_specs=pl.BlockSpec((1, ADD_ROWS, D), lambda i, pos_ref: (pos_ref[1], i, 0))),
        out_shape=jax.ShapeDtypeStruct((2, PK_HALF, D), F32),
        compiler_params=_cparams(("parallel",)),
    )(pos, cp, land)


ANY = pl.BlockSpec(memory_space=pl.ANY)


def _mesh_pos():
    return lax.axis_index("x"), lax.axis_index("y"), lax.axis_index("c")


def _chip_peer(x, y, c, r):
    return (jnp.bitwise_xor(x, r >> 1), jnp.bitwise_xor(y, r & 1), c)


def _allgather_w_in(big0, wt, meta_loc, x2, norm_w):
    s_rows = x2.shape[0] + BLK
    tok = next(t for t in (512, 256, 128) if x2.shape[0] % t == 0)

    def body(big0_ref, wt_ref, meta_ref, x_ref, nw_ref, big_ref, metaf_ref, h_ref, xn_ref, r_ref,
             lsem, ssem, rsem, fssem, frsem, msem_s, msem_r, osem, xbuf, xnbuf, rbuf, mbuf, psem):
        del big0_ref
        x, y, c = _mesh_pos()
        j = 2 * x + y
        me, sibling = (x, y, c), (x, y, 1 - c)

        def half_wait(s_sem, r_sem, halves=1):
            rows = big_ref.at[pl.ds(0, halves * AG_IN_HALF), :]
            return pltpu.make_async_remote_copy(src_ref=rows, dst_ref=rows, send_sem=s_sem, recv_sem=r_sem,
                                                device_id=me, device_id_type=MESH)

        own_meta = pltpu.make_async_copy(meta_ref, metaf_ref.at[j], lsem)
        own_meta.start()
        for s in range(N_CHIPS):
            for hf in range(2):
                @pl.when((j == s) & (c == hf))
                def _():
                    for r in (1, 2, 3):
                        for p, b, n in _ag_half(s, True, hf):
                            pltpu.make_async_remote_copy(
                                src_ref=wt_ref.at[pl.ds(p, n), :], dst_ref=big_ref.at[pl.ds(b, n), :],
                                send_sem=ssem.at[r - 1], recv_sem=rsem.at[r - 1],
                                device_id=_chip_peer(x, y, c, r), device_id_type=MESH).start()

            @pl.when(j == s)
            def _():
                for hf in range(2):
                    for p, b, n in _ag_half(s, True, hf):
                        pltpu.make_async_remote_copy(
                            src_ref=wt_ref.at[pl.ds(p, n), :], dst_ref=big_ref.at[pl.ds(b, n), :],
                            send_sem=osem.at[0], recv_sem=osem.at[1], device_id=sibling, device_id_type=MESH).start()

        meta_copies = [pltpu.make_async_remote_copy(
            src_ref=meta_ref, dst_ref=metaf_ref.at[j], send_sem=msem_s.at[r - 1], recv_sem=msem_r.at[r - 1],
            device_id=_chip_peer(x, y, c, r), device_id_type=MESH) for r in (1, 2, 3)]
        for cp in meta_copies:
            cp.start()

        nw = nw_ref[...]

        n_tok = (s_rows - BLK) // tok

        def x_in(i, slot):
            return pltpu.make_async_copy(x_ref.at[pl.ds(pl.multiple_of(i * tok, tok), tok), :], xbuf.at[slot],
                                         psem.at[slot])

        def norm_rows(first, n, slot):
            hv = xbuf[slot, pl.ds(0, n), :]
            r = lax.rsqrt(jnp.mean(hv * hv, axis=-1, keepdims=True) + NORM_EPS)
            xnbuf[pl.ds(0, n), :] = (hv * r * nw).astype(BF16)
            rbuf[pl.ds(0, n), :] = r
            rows = pl.ds(first if isinstance(first, int) else pl.multiple_of(first, BLK), n)
            outs = [pltpu.make_async_copy(xbuf.at[slot, pl.ds(0, n), :], h_ref.at[rows, :], psem.at[2]),
                    pltpu.make_async_copy(xnbuf.at[pl.ds(0, n), :], xn_ref.at[rows, :], psem.at[3]),
                    pltpu.make_async_copy(rbuf.at[pl.ds(0, n), :], r_ref.at[rows, :], psem.at[4])]
            for cp in outs:
                cp.start(priority=1)
            for cp in outs:
                cp.wait()

        x_in(0, 0).start(priority=1)

        def token_rows(i, carry):
            slot = i % 2
            x_in(i, slot).wait()

            @pl.when(i + 1 < n_tok)
            def _():
                x_in(i + 1, 1 - slot).start(priority=1)

            norm_rows(BLK + i * tok, tok, slot)
            return carry

        lax.fori_loop(0, n_tok, token_rows, 0)
        for cp in meta_copies:
            cp.wait_recv()
        own_meta.wait()
        xbuf[0, pl.ds(0, BLK), :] = jnp.zeros((BLK, D), F32)
        for sh in range(N_CHIPS):
            pltpu.sync_copy(metaf_ref.at[sh], mbuf)
            xbuf[0, PAD_FRONT:BLK, sh * (D // N_CHIPS):(sh + 1) * (D // N_CHIPS)] = mbuf[...]
        norm_rows(0, BLK, 0)

        for r in (1, 2, 3):
            half_wait(ssem.at[r - 1], rsem.at[r - 1]).wait_recv()
            src_shard = jnp.bitwise_xor(j, r)
            for s in range(N_CHIPS):
                for hf in range(2):
                    @pl.when((src_shard == s) & (c == hf))
                    def _():
                        for _, b, n in _ag_half(s, True, hf):
                            rows = big_ref.at[pl.ds(b, n), :]
                            pltpu.make_async_remote_copy(
                                src_ref=rows, dst_ref=rows, send_sem=fssem.at[r - 1], recv_sem=frsem.at[r - 1],
                                device_id=sibling, device_id_type=MESH).start()
        for r in (1, 2, 3):
            half_wait(fssem.at[r - 1], frsem.at[r - 1]).wait_recv()
        for r in (1, 2, 3):
            half_wait(ssem.at[r - 1], rsem.at[r - 1]).wait_send()
            half_wait(fssem.at[r - 1], frsem.at[r - 1]).wait_send()
        own_rows = half_wait(osem.at[0], osem.at[1], halves=2)
        own_rows.wait_recv()
        own_rows.wait_send()
        for cp in meta_copies:
            cp.wait_send()

    dma3 = pltpu.SemaphoreType.DMA((3,))
    big, _, h, xn, r = pl.pallas_call(
        body, name="allgather_w_in", in_specs=[ANY, ANY, ANY, ANY, pl.BlockSpec(memory_space=pltpu.VMEM)],
        out_specs=[ANY] * 5,
        out_shape=[jax.ShapeDtypeStruct((BIG_ROWS, D), BF16), jax.ShapeDtypeStruct((N_CHIPS,) + meta_loc.shape, F32),
                   jax.ShapeDtypeStruct((s_rows, D), F32), jax.ShapeDtypeStruct((s_rows, D), BF16),
                   jax.ShapeDtypeStruct((s_rows, 1), F32)],
        scratch_shapes=[pltpu.SemaphoreType.DMA(()), dma3, dma3, dma3, dma3, dma3, dma3, pltpu.SemaphoreType.DMA((2,)),
                        pltpu.VMEM((2, tok, D), F32), pltpu.VMEM((tok, D), BF16), pltpu.VMEM((tok, 1), F32),
                        pltpu.VMEM(meta_loc.shape, F32), pltpu.SemaphoreType.DMA((5,))],
        input_output_aliases={0: 0},
        compiler_params=pltpu.CompilerParams(has_side_effects=True),
    )(big0, wt, meta_loc, x2, norm_w)
    return big, h, xn, r


HBM = pl.BlockSpec(memory_space=pltpu.HBM)
SEM = pl.BlockSpec(memory_space=pltpu.SEMAPHORE)
EFFECT = pltpu.SideEffectType.DATAFLOW_SIDE_EFFECTING
REST_PEERS = [(r, dc) for r in (1, 2, 3) for dc in (0, 1)]


def _allgather_rest_start(big):
    def body(big_ref, *rest):
        s_sems, r_sems = rest[:6], rest[6:12]
        x, y, c = _mesh_pos()
        j = 2 * x + y
        for s in range(N_CHIPS):
            for hf in range(2):
                @pl.when((j == s) & (c == hf))
                def _():
                    for k, (r, dc) in enumerate(REST_PEERS):
                        for _, b, n in _ag_half(s, False, hf):
                            rows = big_ref.at[pl.ds(b, n), :]
                            pltpu.make_async_remote_copy(
                                src_ref=rows, dst_ref=rows, send_sem=s_sems[k], recv_sem=r_sems[2 * (r - 1) + hf],
                                device_id=_chip_peer(x, y, dc, r), device_id_type=MESH).start()

    sem = pltpu.SemaphoreType.DMA(())
    out = pl.pallas_call(
        body, name="allgather_rest_start", out_shape=(sem,) * 12 + (pltpu.HBM(big.shape, big.dtype),),
        in_specs=(HBM,), out_specs=(SEM,) * 12 + (HBM,), input_output_aliases={0: 12},
        compiler_params=pltpu.CompilerParams(has_side_effects=EFFECT),
    )(pltpu.with_memory_space_constraint(big, pltpu.HBM))
    return out[:12], out[12]


def _allgather_rest_wait(sems, big, after):
    def body(big_ref, *rest):
        s_sems, r_sems = rest[:6], rest[6:12]
        me = _mesh_pos()
        rows = big_ref.at[pl.ds(0, AG_REST_HALF), :]
        for k in range(6):
            copy = pltpu.make_async_remote_copy(src_ref=rows, dst_ref=rows, send_sem=s_sems[k], recv_sem=r_sems[k],
                                                device_id=me, device_id_type=MESH)
            copy.wait_send()
            copy.wait_recv()

    return pl.pallas_call(
        body, name="allgather_rest_wait", out_shape=pltpu.HBM(big.shape, big.dtype),
        in_specs=(HBM,) + (SEM,) * 12 + (pl.BlockSpec(memory_space=pl.ANY),), out_specs=HBM,
        input_output_aliases={0: 0}, compiler_params=pltpu.CompilerParams(has_side_effects=EFFECT),
    )(big, *sems, after)


RS_SPLIT = 2


def _rs_core_exchange(srcs):
    def body(*refs):
        src_refs, (land_ref, ssem, rsem) = refs[:len(srcs)], refs[len(srcs):]
        x, y, c = _mesh_pos()
        me, sibling = (x, y, c), (x, y, 1 - c)
        for hf in range(2):
            @pl.when(c == hf)
            def _():
                for s in range(N_CHIPS):
                    for k, r0, p, n in _grad_half(s, 1 - hf):
                        pltpu.make_async_remote_copy(
                            src_ref=src_refs[k].at[pl.ds(r0, n), :], dst_ref=land_ref.at[s, pl.ds(p, n), :],
                            send_sem=ssem, recv_sem=rsem, device_id=sibling, device_id_type=MESH).start()
        whole = pltpu.make_async_remote_copy(src_ref=land_ref, dst_ref=land_ref, send_sem=ssem, recv_sem=rsem,
                                             device_id=me, device_id_type=MESH)
        whole.wait_recv()
        whole.wait_send()

    return pl.pallas_call(
        body, name="rs_core_exchange", in_specs=[ANY] * len(srcs), out_specs=ANY,
        out_shape=jax.ShapeDtypeStruct((N_CHIPS, PK_HALF, D), BF16),
        scratch_shapes=[pltpu.SemaphoreType.DMA(())] * 2,
        compiler_params=pltpu.CompilerParams(has_side_effects=True),
    )(*srcs)


def _rs_chip_start(cp):
    def body(cp_ref, land_ref, s1, s2, s3, r1, r2, r3, cp_thru, land_thru, token):
        x, y, c = _mesh_pos()
        j = 2 * x + y
        for r, s_sem, r_sem in zip((1, 2, 3), (s1, s2, s3), (r1, r2, r3)):
            pltpu.make_async_remote_copy(
                src_ref=cp_ref.at[jnp.bitwise_xor(j, r)], dst_ref=land_ref.at[r - 1], send_sem=s_sem, recv_sem=r_sem,
                device_id=_chip_peer(x, y, c, r), device_id_type=MESH).start()
        token[...] = jnp.zeros_like(token)

    land_shape = (3, PK_HALF, D)
    sem = pltpu.SemaphoreType.DMA(())
    out = pl.pallas_call(
        body, name="rs_chip_start",
        out_shape=(sem,) * 6 + (pltpu.HBM(cp.shape, cp.dtype), pltpu.HBM(land_shape, BF16),
                                jax.ShapeDtypeStruct((8, 128), F32)),
        in_specs=(HBM, HBM), out_specs=(SEM,) * 6 + (HBM, HBM, pl.BlockSpec(memory_space=pltpu.VMEM)),
        input_output_aliases={0: 6, 1: 7},
        compiler_params=pltpu.CompilerParams(has_side_effects=EFFECT),
    )(pltpu.with_memory_space_constraint(cp, pltpu.HBM),
      pltpu.with_memory_space_constraint(lax.empty(land_shape, BF16), pltpu.HBM))
    return out[:6], out[6], out[7], out[8]


def _rs_chip_wait(sems, cp_thru, land_thru, after):
    def body(cp_ref, land_ref, s1, s2, s3, r1, r2, r3, after_ref, cp_out, land_out):
        me = _mesh_pos()
        for r, s_sem, r_sem in zip((1, 2, 3), (s1, s2, s3), (r1, r2, r3)):
            copy = pltpu.make_async_remote_copy(src_ref=cp_ref.at[0], dst_ref=land_ref.at[r - 1], send_sem=s_sem,
                                                recv_sem=r_sem, device_id=me, device_id_type=MESH)
            copy.wait_send()
            copy.wait_recv()

    return pl.pallas_call(
        body, name="rs_chip_wait",
        out_shape=(pltpu.HBM(cp_thru.shape, cp_thru.dtype), pltpu.HBM(land_thru.shape, land_thru.dtype)),
        in_specs=(HBM, HBM) + (SEM,) * 6 + (pl.BlockSpec(memory_space=pl.ANY),), out_specs=(HBM, HBM),
        input_output_aliases={0: 0, 1: 1},
        compiler_params=pltpu.CompilerParams(has_side_effects=EFFECT),
    )(cp_thru, land_thru, *sems, after)


def _rs_finish(full, small):
    n = PK_HALF // RS_SPLIT

    def body(full_in_ref, sm_ref, full_ref, all_ref, lsem, ssem, rsem, sm_s, sm_r):
        del full_in_ref
        x, y, c = _mesh_pos()
        me, sibling = (x, y, c), (x, y, 1 - c)
        my_id = 4 * x + 2 * y + c
        own_sm = pltpu.make_async_copy(sm_ref, all_ref.at[my_id], lsem)
        own_sm.start()
        for q in range(RS_SPLIT):
            rows = full_ref.at[c, pl.ds(q * n, n), :]
            pltpu.make_async_remote_copy(src_ref=rows, dst_ref=rows, send_sem=ssem, recv_sem=rsem,
                                         device_id=sibling, device_id_type=MESH).start()
        half = pltpu.make_async_remote_copy(src_ref=full_ref.at[c], dst_ref=full_ref.at[c], send_sem=ssem,
                                            recv_sem=rsem, device_id=me, device_id_type=MESH)
        smalls = [pltpu.make_async_remote_copy(
            src_ref=sm_ref, dst_ref=all_ref.at[my_id], send_sem=sm_s.at[r - 1], recv_sem=sm_r.at[r - 1],
            device_id=(jnp.bitwise_xor(x, r >> 2), jnp.bitwise_xor(y, (r >> 1) & 1), jnp.bitwise_xor(c, r & 1)),
            device_id_type=MESH) for r in range(1, 8)]
        for cpy in smalls:
            cpy.start()
        half.wait_recv()
        half.wait_send()
        for cpy in smalls:
            cpy.wait_recv()
        for cpy in smalls:
            cpy.wait_send()
        own_sm.wait()

    dma7 = pltpu.SemaphoreType.DMA((7,))
    return pl.pallas_call(
        body, name="rs_finish", in_specs=[ANY, ANY], out_specs=[ANY, ANY],
        out_shape=[jax.ShapeDtypeStruct((2, PK_HALF, D), F32), jax.ShapeDtypeStruct((8, SM_ROWS, 128), F32)],
        scratch_shapes=[pltpu.SemaphoreType.DMA(()), pltpu.SemaphoreType.DMA(()), pltpu.SemaphoreType.DMA(()), dma7, dma7],
        input_output_aliases={0: 0},
        compiler_params=pltpu.CompilerParams(has_side_effects=True),
    )(full, small)


def _rope_tables(s):
    pos = jnp.arange(s, dtype=F32) - PAD_FRONT

    def cs(d):
        inv = ROPE_BASE ** (-jnp.arange(0, d, 2, dtype=F32) / d)
        ang = pos[:, None] * inv[None, :]
        return jnp.cos(ang), jnp.sin(ang)

    c, sn = cs(ROPE_D)
    z = jnp.zeros_like(c)
    ktab = (jnp.concatenate([c, c, z, z], axis=1), jnp.concatenate([-sn, z, z, z], axis=1),
            jnp.concatenate([z, sn, z, z], axis=1))
    one = jnp.ones_like(c)
    qtab = tuple(ATT_SCALE * jnp.concatenate(parts, axis=1) for parts in (
        [one, one, one, one, c, c, z, z], [z, z, z, z, -sn, z, z, z], [z, z, z, z, z, sn, z, z]))
    lane = jnp.arange(HD) == DH + ROPE_D
    qbias = lane.astype(F32)[None, :]
    kbias = jnp.where((pos < 0)[:, None] & lane[None, DH:], NEG_INF, 0.0).astype(F32)
    c, sn = cs(DH)
    return qtab, ktab, qbias, kbias, jnp.concatenate([c, c], axis=1), jnp.concatenate([-sn, sn], axis=1)


def _grad_mm(a, b, m, n, s, name):
    return _matmul(a, b, "tn", m, n, s, min(m, 512), min(n, 1024), s, BF16, name, n_outer=True)


def _local_step(x2, target2, hp0, big, small_w, normed=None, gather_wait=None, reduce_start=None, reduce_wait=None):
    norm_w, qn_w, kvn_w, gn_w, gn_b, fin_w = small_w
    s = x2.shape[0] + BLK
    tm = _row_tile(s, 1408)
    qtab, ktab, qbias, kbias, c128, s128 = _rope_tables(s)
    consts, consts_bwd, gamma = _ret_consts()

    h, xn, r1 = _rms_in(x2, hp0, norm_w, s) if normed is None else normed
    proj = _matmul(xn, big, "nt", s, PROJ_W, D, tm, 512, D, BF16, "proj", b_off=(BIG_IN // 512, 0))
    if gather_wait is not None:
        big = gather_wait(big, proj)
    uq = big[BIG_UQ:BIG_UQ + 384].reshape(HEADS, DH + ROPE_D, Q_RANK)
    wq = jnp.pad(uq, ((0, 0), (0, HD - DH - ROPE_D), (0, 0))).reshape(HEADS * HD, Q_RANK)
    wukv = big[BIG_UKV:BIG_UKV + 256].reshape(2 * WIDTH, KV_RANK)
    wbm = big[BIG_BM:BIG_BM + 1024].reshape(D, WIDTH)
    wbr = big[BIG_BR:BIG_BR + 1024].reshape(D, WIDTH)
    cqn, rq, ckvn, rkv, qq, kk, vv = _mla_prep(proj, qn_w, kvn_w, wq, wukv, qtab, ktab, qbias, kbias, s)
    o_mla, y_mla, lse = _attn_fwd(qq, kk, vv, proj, s)
    y_ret, on, rstd, qr, kr, rall = _ret_fwd(proj, gn_w, gn_b, c128, s128, consts, gamma, s)
    u_mla, u_ret, merged = _merge_fwd(y_mla, y_ret, wbm, wbr, proj, s)
    h2 = _out_fwd(merged, big, h, s)
    dh2, dh2b, loss_blk, dfin = _loss_bwd(h2, target2, fin_w, s)

    dproj, du_mla, du_ret = _merge_bwd(dh2b, big, u_mla, u_ret, proj, s)
    g_out = _grad_mm(merged, dh2b, D, D, s, "grad_w_out")
    dproj, do_mla, delta = _branch_mla_bwd(dproj, du_mla, wbm, o_mla, proj, s)
    g_bm = _grad_mm(du_mla, y_mla, D, WIDTH, s, "grad_w_branch_mla")
    dproj, do_ret, dgw, dgb = _branch_ret_bwd(dproj, du_ret, wbr, on, rstd, gn_w, gn_b, proj, s)
    g_br = _grad_mm(du_ret, y_ret, D, WIDTH, s, "grad_w_branch_ret")
    dproj = _ret_bwd(dproj, qr, kr, proj, rall, do_ret, c128, s128, consts_bwd, gamma, s)
    dqq, dkk, dvv = _attn_bwd(qq, kk, vv, do_mla, lse, delta.reshape(lse.shape), s)
    dproj, dq, dkv, dqnw, dkvnw = _mla_prep_bwd(dproj, dqq, dkk, dvv, proj, rq, rkv, qn_w, kvn_w, wq, wukv,
                                                qtab, ktab, s)
    g_q = _grad_mm(dq, cqn, HEADS * HD, Q_RANK, s, "grad_w_uq")
    g_ukv = _grad_mm(dkv, ckvn, 2 * WIDTH, KV_RANK, s, "grad_w_ukv")
    g_in = _grad_mm(dproj, xn, PROJ_W, D, s, "grad_w_in")

    g_uq = g_q.reshape(HEADS, HD, Q_RANK)[:, :DH + ROPE_D]
    grads = (g_in, g_out, g_bm.reshape(1024, D), g_br.reshape(1024, D), g_uq.reshape(384, D), g_ukv.reshape(256, D))
    token, travelling = (None, grads) if reduce_start is None else reduce_start(grads)
    dxn = _matmul(dproj, big, "nn", s, D, PROJ_W, _row_tile(s, 384), 512, PROJ_W, F32, "dxn",
                  b_off=(BIG_IN // PROJ_W, 0), n_outer=True, after=token)
    grad_x, gmeta_blk, dnorm = _rms_in_bwd(dxn, h, r1, dh2, norm_w, s)
    small = _small_rows((dnorm, dqnw, dkvnw, dgw, dgb, dfin), loss_blk, gmeta_blk[PAD_FRONT:])
    return grad_x, travelling if reduce_wait is None else reduce_wait(travelling, dnorm), small


def _small_rows(ws, first=None, last=None):
    def part(a, rows):
        a = a.reshape(-1, 128)
        return a if a.shape[0] == rows else jnp.pad(a, ((0, rows - a.shape[0]), (0, 0)))

    bounds = (SM_NORM, SM_QN, SM_KVN, SM_GNW, SM_GNB, SM_FIN, SM_META)
    first = jnp.zeros((SM_NORM, 128), F32) if first is None else first
    last = jnp.zeros((SM_ROWS - SM_META, 128), F32) if last is None else last
    return jnp.concatenate([part(first, SM_NORM)] + [part(w, bounds[k + 1] - bounds[k]) for k, w in enumerate(ws)]
                           + [part(last, SM_ROWS - SM_META)], axis=0)


def kernel(x, meta, norm_w, w_in, mla_q_norm_w, mla_w_uq, mla_kv_norm_w, mla_w_ukv, ret_gn_w, ret_gn_b, w_branch_mla, w_branch_ret, w_out, final_norm_w, loss_target, m_meta, m_norm_w, m_w_in, m_mla_q_norm_w, m_mla_w_uq, m_mla_kv_norm_w, m_mla_w_ukv, m_ret_gn_w, m_ret_gn_b, m_w_branch_mla, m_w_branch_ret, m_w_out, m_final_norm_w, v_meta, v_norm_w, v_w_in, v_mla_q_norm_w, v_mla_w_uq, v_mla_kv_norm_w, v_mla_w_ukv, v_ret_gn_w, v_ret_gn_b, v_w_branch_mla, v_w_branch_ret, v_w_out, v_final_norm_w):
    j = 2 * lax.axis_index("x") + lax.axis_index("y")
    tr = lambda w: w[0].T.reshape(-1, D).astype(BF16)
    pos = jnp.stack([j, lax.axis_index("c")]).astype(jnp.int32)
    put = lambda buf, rows, off: lax.dynamic_update_slice(buf, rows, (off, 0))
    big0 = lax.empty((BIG_ROWS, D), BF16)
    big0 = put(big0, w_out[0].astype(BF16), BIG_OUT + 512 * j)
    big0 = put(big0, tr(w_branch_mla), BIG_BM + 256 * j)
    big0 = put(big0, tr(w_branch_ret), BIG_BR + 256 * j)
    big0 = put(big0, tr(mla_w_uq), BIG_UQ + 96 * j)
    big0 = put(big0, tr(mla_w_ukv), BIG_UKV + 64 * j)
    big0 = put(big0, jnp.zeros((AG_JUNK_END - AG_JUNK_REST, D), BF16), AG_JUNK_REST)
    big0 = put(big0, jnp.zeros((ZERO_ROWS, D), BF16), BIG_IN + IN_WIDTH)
    wt = jnp.concatenate([tr(w_in), jnp.zeros((2 * AG_IN_HALF - IN_SHARD, D), BF16)], axis=0)
    big, h, xn, r1 = _allgather_w_in(big0, wt, meta, x[0], norm_w)
    gather_sems, big = _allgather_rest_start(big)

    def gather_wait(big_travelling, after):
        return _allgather_rest_wait(gather_sems, big_travelling, after)

    small_w = (norm_w, mla_q_norm_w, mla_kv_norm_w, ret_gn_w, ret_gn_b, final_norm_w.reshape(1, D))

    def reduce_start(grads):
        chip_part = _rs_core_add(grads, _rs_core_exchange(grads), pos)
        sems, part_thru, land_thru, token = _rs_chip_start(chip_part)
        return token, (sems, part_thru, land_thru)

    def reduce_wait(state, after):
        return _rs_chip_wait(*state, after)

    grad_x, (chip_part, land2), small = _local_step(x[0], loss_target[0], None, big, small_w, (h, xn, r1), gather_wait,
                                                    reduce_start, reduce_wait)
    full, small_all = _rs_finish(_rs_chip_add(chip_part, land2, pos), small)
    full = full.reshape(PK_ROWS, D)

    untr = lambda lo, hi, rows: full[lo:hi].reshape(rows, -1).T
    grads = {
        "w_out": full[PK_OUT:PK_BM], "w_branch_mla": untr(PK_BM, PK_BR, 512), "w_branch_ret": untr(PK_BR, PK_UQ, 512),
        "mla_w_uq": untr(PK_UQ, PK_UKV, 384), "mla_w_ukv": untr(PK_UKV, PK_PAD, 512),
    }
    big_w = {"mla_w_uq": (mla_w_uq, m_mla_w_uq, v_mla_w_uq),
             "mla_w_ukv": (mla_w_ukv, m_mla_w_ukv, v_mla_w_ukv),
             "w_branch_mla": (w_branch_mla, m_w_branch_mla, v_w_branch_mla),
             "w_branch_ret": (w_branch_ret, m_w_branch_ret, v_w_branch_ret), "w_out": (w_out, m_w_out, v_w_out)}
    res = {"w_in": tuple(a.T[None] for a in _adamw_t(w_in[0].T, full, m_w_in[0].T, v_w_in[0].T, "adamw_w_in"))}
    for name, (w, m, v) in big_w.items():
        d, nm, nv = _adamw(w[0], grads[name], m[0], v[0], "adamw_" + name)
        res[name] = (grads[name][None], d[None], nm[None], nv[None])

    small_m = (m_norm_w, m_mla_q_norm_w, m_mla_kv_norm_w, m_ret_gn_w, m_ret_gn_b, m_final_norm_w.reshape(1, D))
    small_v = (v_norm_w, v_mla_q_norm_w, v_mla_kv_norm_w, v_ret_gn_w, v_ret_gn_b, v_final_norm_w.reshape(1, D))
    gs, ds, ms, vs = _adamw_small(_small_rows(small_w), small_all, _small_rows(small_m), _small_rows(small_v))
    names = ["norm_w", "mla_q_norm_w", "mla_kv_norm_w", "ret_gn_w", "ret_gn_b", "final_norm_w"]
    bounds = [SM_NORM, SM_QN, SM_KVN, SM_GNW, SM_GNB, SM_FIN]
    for k, name in enumerate(names):
        shape = (D,) if name == "final_norm_w" else (1, -1)
        rows = small_w[k].size // 128
        res[name] = tuple(a[bounds[k]:bounds[k] + rows].reshape(shape) for a in (gs, ds, ms, vs))
    g_meta = lax.dynamic_slice_in_dim(gs[SM_META:SM_META + 256].reshape(N_META, D), j * (D // N_CHIPS), D // N_CHIPS, axis=1)
    res["meta"] = (g_meta,) + tuple(_adamw(meta, g_meta, m_meta, v_meta, "adamw_meta"))

    order = ["meta", "norm_w", "w_in", "mla_q_norm_w", "mla_w_uq", "mla_kv_norm_w", "mla_w_ukv", "ret_gn_w", "ret_gn_b",
             "w_branch_mla", "w_branch_ret", "w_out", "final_norm_w"]
    return (gs[0, 0], grad_x[None]) + tuple(res[n][k] for k in range(4) for n in order)
```

```python
import functools
import math

import numpy as np
import jax
import jax.numpy as jnp
from jax import lax
from jax.experimental import pallas as pl
from jax.experimental.pallas import tpu as pltpu

F32 = jnp.float32
BF16 = jnp.bfloat16
MESH = pl.DeviceIdType.MESH

D = 2048
N_META = 16
BLK = 128
PAD_FRONT = BLK - N_META
HEADS = 8
DH = 128
ROPE_D = 64
Q_RANK = 512
KV_RANK = 256
WIDTH = HEADS * DH
ROPE_BASE = 10000.0
NORM_EPS = 1e-6
GN_EPS = 1e-5
NEG_INF = -1e30
ATT_SCALE = (DH + ROPE_D) ** -0.5
RET_SCALE = DH ** -0.5
IN_WIDTH = 10048
N_CHIPS = 4
IN_SHARD = IN_WIDTH // N_CHIPS
ADAM_LR, ADAM_B1, ADAM_B2, ADAM_EPS, ADAM_WD, ADAM_STEP = 0.001, 0.9, 0.999, 1e-08, 0.01, 10

R_Q, R_K, R_V, Z_RET, Z_MLA, GATE0, GATE1 = 0, 1024, 2048, 3072, 4096, 5120, 5120 + D
C_Q = 5120 + 2 * D
C_KV = C_Q + Q_RANK
K_PE = C_KV + KV_RANK
PROJ_W = 10240
IN_RUNS = ((0, 832, C_Q), (832, 1856, Z_MLA), (1856, 4928, R_Q), (4928, 5952, Z_RET), (5952, IN_WIDTH, GATE0))

BIG_IN, BIG_OUT, BIG_BM, BIG_BR, BIG_UQ, BIG_UKV, BIG_JUNK, BIG_ROWS = 0, 10240, 12288, 13312, 14336, 14720, 14976, 15360
ZERO_ROWS = PROJ_W - IN_WIDTH
PK_IN, PK_OUT, PK_BM, PK_BR, PK_UQ, PK_UKV, PK_PAD, PK_ROWS = 0, 2512, 3024, 3280, 3536, 3632, 3696, 3712
PK_HALF = PK_ROWS // 2
SM_LOSS, SM_NORM, SM_QN, SM_KVN, SM_GNW, SM_GNB, SM_FIN, SM_META, SM_ROWS = 0, 8, 24, 32, 40, 48, 56, 72, 328

VMEM_LIMIT = 56 * 1024 * 1024


def _pieces(shard):
    j = shard
    out = [(PK_OUT, BIG_OUT + 512 * j, 512), (PK_BM, BIG_BM + 256 * j, 256), (PK_BR, BIG_BR + 256 * j, 256),
           (PK_UQ, BIG_UQ + 96 * j, 96), (PK_UKV, BIG_UKV + 64 * j, 64)]
    for lo, hi, new in IN_RUNS:
        a, b = max(lo, IN_SHARD * j), min(hi, IN_SHARD * (j + 1))
        if b > a:
            out.append((PK_IN + a - IN_SHARD * j, BIG_IN + new + a - lo, b - a))
    out += [(PK_PAD, BIG_JUNK + 16 * j, 16)]
    return out


AG_JUNK_REST, AG_JUNK_IN, AG_JUNK_END = BIG_JUNK, BIG_JUNK + 128, BIG_JUNK + 192
AG_REST_HALF, AG_IN_HALF = 608, 1264


def _ag_half(shard, w_in, half):
    pieces = [(b, n) for p, b, n in _pieces(shard) if p < PK_PAD and (p < PK_OUT) == w_in]
    pieces.append((AG_JUNK_IN + 16 * shard, 16) if w_in else (AG_JUNK_REST + 32 * shard, 32))
    size = AG_IN_HALF if w_in else AG_REST_HALF
    out, pos = [], 0
    for b, n in pieces:
        s, e = max(pos, half * size), min(pos + n, (half + 1) * size)
        if e > s:
            out.append((s, b + s - pos, e - s))
        pos += n
    assert pos == 2 * size
    return out


def _grad_half(shard, half):
    j = shard
    pieces = [(0, b - BIG_IN, p, n) for p, b, n in _pieces(j) if p < PK_OUT]
    pieces += [(1, 512 * j, PK_OUT, 512), (2, 256 * j, PK_BM, 256), (3, 256 * j, PK_BR, 256), (4, 96 * j, PK_UQ, 96),
               (5, 64 * j, PK_UKV, 64), (0, IN_WIDTH, PK_PAD, PK_ROWS - PK_PAD)]
    lo, hi = half * PK_HALF, (half + 1) * PK_HALF
    out = []
    for k, r0, p, n in pieces:
        s, e = max(p, lo), min(p + n, hi)
        if e > s:
            out.append((k, r0 + s - p, s - lo, e - s))
    assert sum(n for _, _, _, n in out) == PK_HALF
    return out


def _row_tile(rows, cap):
    best = BLK
    for t in range(BLK, cap + 1, BLK):
        if rows % t == 0:
            best = t
    return best


def _cparams(sem):
    return pltpu.CompilerParams(dimension_semantics=sem, vmem_limit_bytes=VMEM_LIMIT)


def _dot(a, b, form):
    dn = {"nt": (((1,), (1,)), ((), ())), "nn": (((1,), (0,)), ((), ())), "tn": (((0,), (0,)), ((), ()))}[form]
    return lax.dot_general(a, b, dn, preferred_element_type=F32)


def _sigmoid(v):
    return 1.0 / (1.0 + jnp.exp(-v))


def _matmul(a, b, form, m, n, k, tm, tn, tk, out_dtype, name, a_off=(0, 0), b_off=(0, 0), n_outer=False, after=None):
    nk = k // tk
    gi, gj = m // tm, n // tn

    def ij(g0, g1):
        return (g1, g0) if n_outer else (g0, g1)

    if form == "nt":
        a_spec = pl.BlockSpec((tm, tk), lambda g0, g1, kk: (ij(g0, g1)[0] + a_off[0], kk + a_off[1]))
        b_spec = pl.BlockSpec((tn, tk), lambda g0, g1, kk: (ij(g0, g1)[1] + b_off[0], kk + b_off[1]))
    elif form == "nn":
        a_spec = pl.BlockSpec((tm, tk), lambda g0, g1, kk: (ij(g0, g1)[0] + a_off[0], kk + a_off[1]))
        b_spec = pl.BlockSpec((tk, tn), lambda g0, g1, kk: (kk + b_off[0], ij(g0, g1)[1] + b_off[1]))
    else:
        a_spec = pl.BlockSpec((tk, tm), lambda g0, g1, kk: (kk + a_off[0], ij(g0, g1)[0] + a_off[1]))
        b_spec = pl.BlockSpec((tk, tn), lambda g0, g1, kk: (kk + b_off[0], ij(g0, g1)[1] + b_off[1]))
    o_spec = pl.BlockSpec((tm, tn), lambda g0, g1, kk: ij(g0, g1))

    def body(a_ref, b_ref, *rest):
        o_ref, *acc = rest[0 if after is None else 1:]
        p = _dot(a_ref[...], b_ref[...], form)
        if nk == 1:
            o_ref[...] = p.astype(o_ref.dtype)
        else:
            acc_ref, = acc
            kk = pl.program_id(2)

            @pl.when(kk == 0)
            def _():
                acc_ref[...] = p

            @pl.when(kk > 0)
            def _():
                acc_ref[...] += p

            @pl.when(kk == nk - 1)
            def _():
                o_ref[...] = acc_ref[...].astype(o_ref.dtype)

    extra = [] if after is None else [after]
    return pl.pallas_call(
        body, name=name, grid=(gj, gi, nk) if n_outer else (gi, gj, nk),
        in_specs=[a_spec, b_spec] + [pl.BlockSpec(memory_space=pl.ANY)] * len(extra), out_specs=o_spec,
        out_shape=jax.ShapeDtypeStruct((m, n), out_dtype),
        scratch_shapes=[] if nk == 1 else [pltpu.VMEM((tm, tn), F32)],
        compiler_params=_cparams(("parallel", "parallel", "arbitrary")),
    )(a, b, *extra)


def _rms_in(x, hp0, norm_w, s):
    def body(x_ref, hp0_ref, w_ref, h_ref, xn_ref, r_ref):
        def run(hv):
            r = lax.rsqrt(jnp.mean(hv * hv, axis=-1, keepdims=True) + NORM_EPS)
            h_ref[...] = hv
            xn_ref[...] = (hv * r * w_ref[...]).astype(BF16)
            r_ref[...] = r

        @pl.when(pl.program_id(0) == 0)
        def _():
            run(hp0_ref[...])

        @pl.when(pl.program_id(0) > 0)
        def _():
            run(x_ref[...])

    return pl.pallas_call(
        body, name="rms_in", grid=(s // BLK,),
        in_specs=[pl.BlockSpec((BLK, D), lambda i: (jnp.maximum(i - 1, 0), 0)),
                  pl.BlockSpec((BLK, D), lambda i: (0, 0)), pl.BlockSpec((1, D), lambda i: (0, 0))],
        out_specs=[pl.BlockSpec((BLK, D), lambda i: (i, 0)), pl.BlockSpec((BLK, D), lambda i: (i, 0)),
                   pl.BlockSpec((BLK, 1), lambda i: (i, 0))],
        out_shape=[jax.ShapeDtypeStruct((s, D), F32), jax.ShapeDtypeStruct((s, D), BF16),
                   jax.ShapeDtypeStruct((s, 1), F32)],
        compiler_params=_cparams(("arbitrary",)),
    )(x, hp0, norm_w)


def _rope64(t, c, sa, sb):
    return t * c + pltpu.roll(t, t.shape[1] - ROPE_D // 2, 1) * sa + pltpu.roll(t, ROPE_D // 2, 1) * sb


def _rope128(t, c, sg):
    return t * c + pltpu.roll(t, DH // 2, 1) * sg


HD = 2 * DH


def _mla_prep(proj, qn_w, kvn_w, wq, wukv, qtab, ktab, qbias, kbias, s):
    tm = _row_tile(s, 384)

    def body(cq_ref, ckv_ref, kpe_ref, qnw_ref, kvnw_ref, wq_ref, wukv_ref, qc_ref, qa_ref, qb_ref,
             kc_ref, ka_ref, kb_ref, qbias_ref, kbias_ref, cqn_ref, rq_ref, ckvn_ref, rkv_ref, qq_ref, kk_ref, vv_ref):
        cq = cq_ref[...].astype(F32)
        rq = lax.rsqrt(jnp.mean(cq * cq, axis=-1, keepdims=True) + NORM_EPS)
        cqn = (cq * rq * qnw_ref[...]).astype(BF16)
        ckv = ckv_ref[...].astype(F32)
        rkv = lax.rsqrt(jnp.mean(ckv * ckv, axis=-1, keepdims=True) + NORM_EPS)
        ckvn = (ckv * rkv * kvnw_ref[...]).astype(BF16)
        cqn_ref[...] = cqn
        rq_ref[...] = rq
        ckvn_ref[...] = ckvn
        rkv_ref[...] = rkv
        q = _dot(cqn, wq_ref[...], "nt")
        kv = _dot(ckvn, wukv_ref[...], "nt")
        kp = (_rope64(kpe_ref[...].astype(F32), kc_ref[...], ka_ref[...], kb_ref[...]) + kbias_ref[...]).astype(BF16)
        qc, qa, qb, qbias = qc_ref[...], qa_ref[...], qb_ref[...], qbias_ref[...]
        ones = jnp.ones((tm, DH), BF16)
        for h in range(HEADS):
            lo, mid, hi = h * HD, h * HD + DH, (h + 1) * HD
            qq_ref[:, lo:hi] = (_rope64(q[:, lo:hi], qc, qa, qb) + qbias).astype(BF16)
            kk_ref[:, lo:mid] = kv[:, lo:mid].astype(BF16)
            kk_ref[:, mid:hi] = kp
            vv_ref[:, lo:mid] = kv[:, mid:hi].astype(BF16)
            vv_ref[:, mid:hi] = ones

    row = lambda w, cb: pl.BlockSpec((tm, w), lambda i: (i, cb))
    full = lambda a: pl.BlockSpec(a.shape, lambda i: (0, 0))
    wide = jax.ShapeDtypeStruct((s, HEADS * HD), BF16)
    return pl.pallas_call(
        body, name="mla_prep", grid=(s // tm,),
        in_specs=[row(Q_RANK, C_Q // Q_RANK), row(KV_RANK, C_KV // KV_RANK), row(DH, K_PE // DH),
                  full(qn_w), full(kvn_w), full(wq), full(wukv), row(HD, 0), row(HD, 0), row(HD, 0),
                  row(DH, 0), row(DH, 0), row(DH, 0), full(qbias), row(DH, 0)],
        out_specs=[row(Q_RANK, 0), row(1, 0), row(KV_RANK, 0), row(1, 0), row(HEADS * HD, 0), row(HEADS * HD, 0),
                   row(HEADS * HD, 0)],
        out_shape=[jax.ShapeDtypeStruct((s, Q_RANK), BF16), jax.ShapeDtypeStruct((s, 1), F32),
                   jax.ShapeDtypeStruct((s, KV_RANK), BF16), jax.ShapeDtypeStruct((s, 1), F32), wide, wide, wide],
        compiler_params=_cparams(("parallel",)),
    )(proj, proj, proj, qn_w, kvn_w, wq, wukv, *qtab, *ktab, qbias, kbias)


def _diag_mask(t):
    return lax.broadcasted_iota(jnp.int32, (t, t), 0) <= lax.broadcasted_iota(jnp.int32, (t, t), 1)


def _silu(z):
    return z * _sigmoid(z)


def _attn_fwd(qq, kk, vv, proj, s):
    t = _row_tile(s, 384)
    n = s // t

    def body(q_ref, k_ref, v_ref, z_ref, o_ref, y_ref, lse_ref, acc_ref, m_ref):
        qi = pl.program_id(1)
        q = q_ref[...]
        m_ref[...] = jnp.full(m_ref.shape, NEG_INF, F32)
        acc_ref[...] = jnp.zeros(acc_ref.shape, F32)

        def keys(ki):
            return pl.ds(pl.multiple_of(ki * t, t), t)

        def scores(ki):
            return _dot(k_ref[keys(ki), :], q, "nt")

        def tile(ki, st):
            m_old = m_ref[...]
            m_new = jnp.maximum(m_old, jnp.max(st, axis=0, keepdims=True))
            pt = jnp.exp(st - m_new).astype(BF16)
            acc_ref[...] = jnp.exp(m_old - m_new) * acc_ref[...] + _dot(v_ref[keys(ki), :], pt, "tn")
            m_ref[...] = m_new

        def inner(ki, st):
            st_after = scores(ki + 1)
            tile(ki, st)
            return st_after

        st_diag = lax.fori_loop(0, qi, inner, scores(0))
        tile(qi, jnp.where(_diag_mask(t), st_diag, NEG_INF))

        l = acc_ref[DH:DH + 1, :]
        o = (acc_ref[:DH, :] / l).T
        o_ref[...] = o.astype(BF16)
        y_ref[...] = (o * _silu(z_ref[...].astype(F32))).astype(BF16)
        lse_ref[0, 0] = m_ref[...] + jnp.log(l)

    qtile = pl.BlockSpec((t, DH), lambda h, i: (i, h))
    head = pl.BlockSpec((s, HD), lambda h, i: (0, h))
    return pl.pallas_call(
        body, name="attn_fwd", grid=(HEADS, s // t),
        in_specs=[pl.BlockSpec((t, HD), lambda h, i: (i, h)), head, head,
                  pl.BlockSpec((t, DH), lambda h, i: (i, Z_MLA // DH + h))],
        out_specs=[qtile, qtile, pl.BlockSpec((1, 1, 1, t), lambda h, i: (h, i, 0, 0))],
        out_shape=[jax.ShapeDtypeStruct((s, WIDTH), BF16), jax.ShapeDtypeStruct((s, WIDTH), BF16),
                   jax.ShapeDtypeStruct((HEADS, s // t, 1, t), F32)],
        scratch_shapes=[pltpu.VMEM((HD, t), F32), pltpu.VMEM((1, t), F32)],
        compiler_params=_cparams(("parallel", "arbitrary")),
    )(qq, kk, vv, proj)


RET_CHUNKS = 3


def _ret_consts():
    log_g = np.log1p(-(2.0 ** (-5.0 - np.arange(HEADS, dtype=np.float64))))
    n = np.arange(BLK, dtype=np.float64)
    diff = n[:, None] - n[None, :]
    decay = np.where(diff >= 0, np.exp(log_g[:, None, None] * np.maximum(diff, 0.0)), 0.0)
    zeta = np.exp(log_g[:, None] * (BLK - 1.0 - n))[:, :, None]
    xi = np.exp(log_g[:, None] * (n + 1.0))[:, :, None]
    gamma = [float(np.float32(np.exp(g * BLK))) for g in log_g]
    wide = lambda a: np.broadcast_to(a, (HEADS, BLK, DH))
    return ((jnp.asarray(decay, F32), jnp.asarray(wide(zeta), F32), jnp.asarray(wide(xi), F32)),
            (jnp.asarray(decay, F32), jnp.asarray(zeta, F32), jnp.asarray(xi, F32)), gamma)


def _ret_fwd(proj, gn_w, gn_b, c128, s128, consts, gamma, s):
    nb = s // BLK
    cps = RET_CHUNKS if nb % RET_CHUNKS == 0 else 1
    decay, zeta, xi = consts

    def body(rq_ref, rk_ref, rv_ref, z_ref, gw_ref, gb_ref, c_ref, s_ref, dm_ref, zt_ref, xi_ref,
             y_ref, on_ref, rstd_ref, qr_ref, kr_ref, rall_ref, state):
        @pl.when(pl.program_id(0) == 0)
        def _():
            state[...] = jnp.zeros_like(state)

        for sub in range(cps):
            rows = slice(sub * BLK, (sub + 1) * BLK)
            c, sg = c_ref[rows, :], s_ref[rows, :]
            for h in range(HEADS):
                sl = slice(h * DH, (h + 1) * DH)
                q = _rope128(rq_ref[rows, sl].astype(F32), c, sg).astype(BF16)
                kf = _rope128(rk_ref[rows, sl].astype(F32), c, sg) * RET_SCALE
                k = kf.astype(BF16)
                v = rv_ref[rows, sl]
                qr_ref[rows, sl] = q
                kr_ref[rows, sl] = k
                r_prev = state[h]
                rall_ref[sub, h] = r_prev
                a = _dot(q, k, "nt") * dm_ref[h]
                o = _dot(a.astype(BF16), v, "nn") + _dot(q, r_prev.astype(BF16), "nn") * xi_ref[h]
                state[h] = r_prev * gamma[h] + _dot((kf * zt_ref[h]).astype(BF16), v, "tn")
                mu = jnp.mean(o, axis=-1, keepdims=True)
                var = jnp.mean(jnp.square(o - mu), axis=-1, keepdims=True)
                rstd = lax.rsqrt(var + GN_EPS)
                on = (o - mu) * rstd
                rstd_ref[h, rows, :] = rstd
                on_ref[rows, sl] = on.astype(BF16)
                ogn = on * gw_ref[:, sl] + gb_ref[:, sl]
                y_ref[rows, sl] = (ogn * _silu(z_ref[rows, sl].astype(F32))).astype(BF16)

    seg = lambda cb: pl.BlockSpec((cps * BLK, WIDTH), lambda i: (i, cb))
    full = lambda a: pl.BlockSpec(a.shape, lambda i: (0,) * a.ndim)
    tab = pl.BlockSpec((cps * BLK, DH), lambda i: (i, 0))
    return pl.pallas_call(
        body, name="ret_fwd", grid=(nb // cps,),
        in_specs=[seg(R_Q // WIDTH), seg(R_K // WIDTH), seg(R_V // WIDTH), seg(Z_RET // WIDTH), full(gn_w), full(gn_b),
                  tab, tab, full(decay), full(zeta), full(xi)],
        out_specs=[seg(0), seg(0), pl.BlockSpec((HEADS, cps * BLK, 1), lambda i: (0, i, 0)), seg(0), seg(0),
                   pl.BlockSpec((cps, HEADS, DH, DH), lambda i: (i, 0, 0, 0))],
        out_shape=[jax.ShapeDtypeStruct((s, WIDTH), BF16), jax.ShapeDtypeStruct((s, WIDTH), BF16),
                   jax.ShapeDtypeStruct((HEADS, s, 1), F32), jax.ShapeDtypeStruct((s, WIDTH), BF16),
                   jax.ShapeDtypeStruct((s, WIDTH), BF16), jax.ShapeDtypeStruct((nb, HEADS, DH, DH), F32)],
        scratch_shapes=[pltpu.VMEM((HEADS, DH, DH), F32)],
        compiler_params=_cparams(("arbitrary",)),
    )(proj, proj, proj, proj, gn_w, gn_b, c128, s128, decay, zeta, xi)


def _merge_fwd(y_mla, y_ret, wbm, wbr, proj, s):
    tm, tn = _row_tile(s, 1408), 512

    def body(ym_ref, yr_ref, wm_ref, wr_ref, g0_ref, g1_ref, um_ref, ur_ref, mg_ref):
        um = _dot(ym_ref[...], wm_ref[...], "nt")
        ur = _dot(yr_ref[...], wr_ref[...], "nt")
        um_ref[...] = um.astype(BF16)
        ur_ref[...] = ur.astype(BF16)
        mg_ref[...] = (_sigmoid(g0_ref[...].astype(F32)) * um + _sigmoid(g1_ref[...].astype(F32)) * ur).astype(BF16)

    yspec = pl.BlockSpec((tm, WIDTH), lambda i, j: (i, 0))
    wspec = pl.BlockSpec((tn, WIDTH), lambda i, j: (j, 0))
    ospec = pl.BlockSpec((tm, tn), lambda i, j: (i, j))
    return pl.pallas_call(
        body, name="merge_fwd", grid=(s // tm, D // tn),
        in_specs=[yspec, yspec, wspec, wspec, pl.BlockSpec((tm, tn), lambda i, j: (i, GATE0 // tn + j)),
                  pl.BlockSpec((tm, tn), lambda i, j: (i, GATE1 // tn + j))],
        out_specs=[ospec, ospec, ospec],
        out_shape=[jax.ShapeDtypeStruct((s, D), BF16)] * 3,
        compiler_params=_cparams(("parallel", "parallel")),
    )(y_mla, y_ret, wbm, wbr, proj, proj)


def _out_fwd(merged, big, h, s):
    tm, tn = _row_tile(s, 1408), 512

    def body(m_ref, w_ref, h_ref, o_ref):
        o_ref[...] = h_ref[...] + _dot(m_ref[...], w_ref[...], "nn")

    return pl.pallas_call(
        body, name="out_fwd", grid=(s // tm, D // tn),
        in_specs=[pl.BlockSpec((tm, D), lambda i, j: (i, 0)), pl.BlockSpec((D, tn), lambda i, j: (BIG_OUT // D, j)),
                  pl.BlockSpec((tm, tn), lambda i, j: (i, j))],
        out_specs=pl.BlockSpec((tm, tn), lambda i, j: (i, j)),
        out_shape=jax.ShapeDtypeStruct((s, D), F32),
        compiler_params=_cparams(("parallel", "parallel")),
    )(merged, big, h)


def _loss_bwd(h2, target, fin_w, s):
    tb = _row_tile(s, 384)
    sub, nb = tb // BLK, s // tb

    def body(h2_ref, *rest):
        t_refs, (w_ref, dh_ref, dhb_ref, loss_ref, dw_ref) = rest[:sub], rest[sub:]
        i = pl.program_id(0)

        @pl.when(i == 0)
        def _():
            loss_ref[...] = jnp.zeros_like(loss_ref)
            dw_ref[...] = jnp.zeros_like(dw_ref)

        w = w_ref[...]

        def chunk(k):
            rows = slice(k * BLK, (k + 1) * BLK)
            hv = h2_ref[rows, :]
            r = lax.rsqrt(jnp.mean(hv * hv, axis=-1, keepdims=True) + NORM_EPS)
            nrm = hv * r
            e = nrm * w - t_refs[k][...]
            loss_ref[...] += jnp.full(loss_ref.shape, 0.5 / D, F32) * jnp.sum(e * e)
            dy = e * (1.0 / D)
            dw_ref[...] += jnp.sum(dy * nrm, axis=0, keepdims=True)
            g = dy * w
            dh = r * (g - nrm * jnp.mean(g * nrm, axis=-1, keepdims=True))
            dh_ref[rows, :] = dh
            dhb_ref[rows, :] = dh.astype(BF16)

        @pl.when(i == 0)
        def _():
            dh_ref[:BLK, :] = jnp.zeros((BLK, D), F32)
            dhb_ref[:BLK, :] = jnp.zeros((BLK, D), BF16)

        @pl.when(i > 0)
        def _():
            chunk(0)

        for k in range(1, sub):
            chunk(k)

    blk = pl.BlockSpec((tb, D), lambda i: (i, 0))
    tgt = [pl.BlockSpec((BLK, D), lambda i, k=k: (jnp.maximum(i * sub + k - 1, 0), 0)) for k in range(sub)]
    return pl.pallas_call(
        body, name="loss_bwd", grid=(nb,),
        in_specs=[blk] + tgt + [pl.BlockSpec((1, D), lambda i: (0, 0))],
        out_specs=[blk, blk, pl.BlockSpec((8, 128), lambda i: (0, 0)), pl.BlockSpec((1, D), lambda i: (0, 0))],
        out_shape=[jax.ShapeDtypeStruct((s, D), F32), jax.ShapeDtypeStruct((s, D), BF16),
                   jax.ShapeDtypeStruct((8, 128), F32), jax.ShapeDtypeStruct((1, D), F32)],
        compiler_params=_cparams(("arbitrary",)),
    )(h2, *[target] * sub, fin_w)


def _merge_bwd(dh2b, big, u_mla, u_ret, proj, s):
    tm, tn = _row_tile(s, 1408), 512

    def body(d_ref, w_ref, um_ref, ur_ref, gate_ref, dproj_ref, dum_ref, dur_ref, dm_ref):
        branch, j = pl.program_id(1), pl.program_id(2)
        cols = pl.ds(pl.multiple_of(j * tn, tn), tn)

        @pl.when(branch == 0)
        def _():
            dm_ref[:, cols] = _dot(d_ref[...], w_ref[...], "nt")

        dm = dm_ref[:, cols]
        gt = _sigmoid(gate_ref[...].astype(F32))

        @pl.when(branch == 0)
        def _():
            dproj_ref[...] = (dm * um_ref[...].astype(F32) * gt * (1.0 - gt)).astype(BF16)
            dum_ref[...] = (dm * gt).astype(BF16)

        @pl.when(branch == 1)
        def _():
            dproj_ref[...] = (dm * ur_ref[...].astype(F32) * gt * (1.0 - gt)).astype(BF16)
            dur_ref[...] = (dm * gt).astype(BF16)

    last = D // tn - 1
    mla = pl.BlockSpec((tm, tn), lambda i, b, j: (i, jnp.where(b == 0, j, last)))
    ret = pl.BlockSpec((tm, tn), lambda i, b, j: (i, jnp.where(b == 0, 0, j)))
    gate = pl.BlockSpec((tm, tn), lambda i, b, j: (i, GATE0 // tn + b * (D // tn) + j))
    return pl.pallas_call(
        body, name="merge_bwd", grid=(s // tm, 2, D // tn),
        in_specs=[pl.BlockSpec((tm, D), lambda i, b, j: (i, 0)),
                  pl.BlockSpec((tn, D), lambda i, b, j: (BIG_OUT // tn + jnp.where(b == 0, j, last), 0)), mla, ret, gate],
        out_specs=[gate, mla, ret],
        out_shape=[jax.ShapeDtypeStruct((s, PROJ_W), BF16), jax.ShapeDtypeStruct((s, D), BF16),
                   jax.ShapeDtypeStruct((s, D), BF16)],
        scratch_shapes=[pltpu.VMEM((tm, D), F32)],
        compiler_params=_cparams(("parallel", "arbitrary", "arbitrary")),
    )(dh2b, big, u_mla, u_ret, proj)


def _dsilu(z):
    sg = _sigmoid(z)
    return sg * (1.0 + z * (1.0 - sg))


def _branch_mla_bwd(dproj, du, wbm, o_mla, proj, s):
    tm = _row_tile(s, 384)

    def body(dproj_in, du_ref, w_ref, o_ref, z_ref, dz_ref, do_ref, delta_ref):
        del dproj_in
        dy = _dot(du_ref[...], w_ref[...], "nn")
        z = z_ref[...].astype(F32)
        o = o_ref[...].astype(F32)
        do = dy * _silu(z)
        do_ref[...] = do.astype(BF16)
        dz_ref[...] = (dy * o * _dsilu(z)).astype(BF16)
        prod = do * o
        for h in range(HEADS):
            delta_ref[h] = jnp.sum(prod[:, h * DH:(h + 1) * DH], axis=-1, keepdims=True)

    row = lambda w, cb: pl.BlockSpec((tm, w), lambda i: (i, cb))
    return pl.pallas_call(
        body, name="branch_mla_bwd", grid=(s // tm,),
        in_specs=[ANY, row(D, 0), pl.BlockSpec((D, WIDTH), lambda i: (0, 0)), row(WIDTH, 0),
                  row(WIDTH, Z_MLA // WIDTH)],
        out_specs=[row(WIDTH, Z_MLA // WIDTH), row(WIDTH, 0), pl.BlockSpec((HEADS, tm, 1), lambda i: (0, i, 0))],
        out_shape=[jax.ShapeDtypeStruct((s, PROJ_W), BF16), jax.ShapeDtypeStruct((s, WIDTH), BF16),
                   jax.ShapeDtypeStruct((HEADS, s, 1), F32)],
        input_output_aliases={0: 0},
        compiler_params=_cparams(("parallel",)),
    )(dproj, du, wbm, o_mla, proj)


def _branch_ret_bwd(dproj, du, wbr, on, rstd, gn_w, gn_b, proj, s):
    tm = _row_tile(s, 384)

    def body(dproj_in, du_ref, w_ref, on_ref, rstd_ref, gw_ref, gb_ref, z_ref, dz_ref, do_ref, dgw_ref, dgb_ref):
        del dproj_in

        @pl.when(pl.program_id(0) == 0)
        def _():
            dgw_ref[...] = jnp.zeros_like(dgw_ref)
            dgb_ref[...] = jnp.zeros_like(dgb_ref)

        dy = _dot(du_ref[...], w_ref[...], "nn")
        z = z_ref[...].astype(F32)
        on = on_ref[...].astype(F32)
        gw = gw_ref[...]
        dogn = dy * _silu(z)
        dz_ref[...] = (dy * (on * gw + gb_ref[...]) * _dsilu(z)).astype(BF16)
        dgw_ref[...] += jnp.sum(dogn * on, axis=0, keepdims=True)
        dgb_ref[...] += jnp.sum(dogn, axis=0, keepdims=True)
        don = dogn * gw
        for h in range(HEADS):
            sl = slice(h * DH, (h + 1) * DH)
            dn, nh = don[:, sl], on[:, sl]
            do = rstd_ref[h] * (dn - jnp.mean(dn, axis=-1, keepdims=True)
                                - nh * jnp.mean(dn * nh, axis=-1, keepdims=True))
            do_ref[:, sl] = do.astype(BF16)

    row = lambda w, cb: pl.BlockSpec((tm, w), lambda i: (i, cb))
    vec = pl.BlockSpec((1, WIDTH), lambda i: (0, 0))
    return pl.pallas_call(
        body, name="branch_ret_bwd", grid=(s // tm,),
        in_specs=[ANY, row(D, 0), pl.BlockSpec((D, WIDTH), lambda i: (0, 0)), row(WIDTH, 0),
                  pl.BlockSpec((HEADS, tm, 1), lambda i: (0, i, 0)), vec, vec, row(WIDTH, Z_RET // WIDTH)],
        out_specs=[row(WIDTH, Z_RET // WIDTH), row(WIDTH, 0), vec, vec],
        out_shape=[jax.ShapeDtypeStruct((s, PROJ_W), BF16), jax.ShapeDtypeStruct((s, WIDTH), BF16)]
        + [jax.ShapeDtypeStruct((1, WIDTH), F32)] * 2,
        input_output_aliases={0: 0},
        compiler_params=_cparams(("arbitrary",)),
    )(dproj, du, wbr, on, rstd, gn_w, gn_b, proj)


def _ret_bwd(dproj, qr, kr, proj, rall, do_ret, c128, s128, consts, gamma, s):
    cps = RET_CHUNKS if (s // BLK) % RET_CHUNKS == 0 else 1
    nb = s // (cps * BLK)
    decay, zeta, xi = consts

    def body(dproj_in, q_ref, k_ref, v_ref, r_ref, do_ref, c_ref, s_ref, dm_ref, zt_ref, xi_ref, out_ref, gstate):
        del dproj_in
        dq_ref, dk_ref, dv_ref = (out_ref.at[:, pl.ds(off, WIDTH)] for off in (R_Q, R_K, R_V))

        @pl.when(pl.program_id(0) == 0)
        def _():
            gstate[...] = jnp.zeros_like(gstate)

        for sub in reversed(range(cps)):
            rows = slice(sub * BLK, (sub + 1) * BLK)
            c, sg = c_ref[rows, :], s_ref[rows, :]
            for h in range(HEADS):
                sl = slice(h * DH, (h + 1) * DH)
                q, k, v, do = q_ref[rows, sl], k_ref[rows, sl], v_ref[rows, sl], do_ref[rows, sl]
                dm = dm_ref[h]
                g_next = gstate[h]
                gb = g_next.astype(BF16)
                a = (_dot(q, k, "nt") * dm).astype(BF16)
                da = (_dot(do, v, "nt") * dm).astype(BF16)
                dox = (do.astype(F32) * xi_ref[h]).astype(BF16)
                dq = _dot(da, k, "nn") + _dot(dox, r_ref[sub, h].astype(BF16), "nt")
                dk = _dot(da, q, "tn") + _dot(v, gb, "nt") * zt_ref[h]
                kz = (k.astype(F32) * zt_ref[h]).astype(BF16)
                dv = _dot(a, do, "tn") + _dot(kz, gb, "nn")
                gstate[h] = g_next * gamma[h] + _dot(q, dox, "tn")
                dk = dk * RET_SCALE
                dq_ref[rows, sl] = _rope128(dq, c, -sg).astype(BF16)
                dk_ref[rows, sl] = _rope128(dk, c, -sg).astype(BF16)
                dv_ref[rows, sl] = dv.astype(BF16)

    rev = lambda cb: pl.BlockSpec((cps * BLK, WIDTH), lambda i: (nb - 1 - i, cb))
    full = lambda a: pl.BlockSpec(a.shape, lambda i: (0,) * a.ndim)
    tab = pl.BlockSpec((cps * BLK, DH), lambda i: (nb - 1 - i, 0))
    return pl.pallas_call(
        body, name="ret_bwd", grid=(nb,),
        in_specs=[ANY, rev(0), rev(0), rev(R_V // WIDTH),
                  pl.BlockSpec((cps, HEADS, DH, DH), lambda i: (nb - 1 - i, 0, 0, 0)),
                  rev(0), tab, tab, full(decay), full(zeta), full(xi)],
        out_specs=pl.BlockSpec((cps * BLK, 3 * WIDTH), lambda i: (nb - 1 - i, R_Q // (3 * WIDTH))),
        out_shape=jax.ShapeDtypeStruct((s, PROJ_W), BF16),
        scratch_shapes=[pltpu.VMEM((HEADS, DH, DH), F32)],
        input_output_aliases={0: 0},
        compiler_params=_cparams(("arbitrary",)),
    )(dproj, qr, kr, proj, rall, do_ret, c128, s128, decay, zeta, xi)


def _attn_bwd(qq, kk, vv, do, lse, delta, s):
    t = _row_tile(s, 384)
    n = s // t

    def body(q_ref, k_ref, v_ref, do_ref, lse_ref, delta_ref, dq_ref, dk_ref, dv_ref, dk_acc, dv_acc):
        ki = pl.program_id(1)

        @pl.when(ki == 0)
        def _():
            dq_ref[...] = jnp.zeros(dq_ref.shape, F32)

        k, v = k_ref[...], v_ref[...]
        dk_acc[...] = jnp.zeros(dk_acc.shape, F32)
        dv_acc[...] = jnp.zeros(dv_acc.shape, F32)

        def rows(qi):
            return pl.ds(pl.multiple_of(qi * t, t), t)

        def products(qi):
            return _dot(k, q_ref[rows(qi), :], "nt"), _dot(v, do_ref[rows(qi), :], "nt")

        def tile(qi, st, dpt):
            q, dov = q_ref[rows(qi), :], do_ref[rows(qi), :]
            pt = jnp.exp(st - lse_ref[0, qi])
            dv_acc[...] += _dot(pt.astype(BF16), dov, "nn")
            dst = (pt * (dpt - delta_ref[0, qi])).astype(BF16)
            dk_acc[...] += _dot(dst, q, "nn")
            dq_ref[rows(qi), :] += _dot(dst, k, "tn")

        def inner(qi, carry):
            after = products(jnp.minimum(qi + 1, n - 1))
            tile(qi, *carry)
            return after

        st, dpt = products(ki)
        ahead = products(jnp.minimum(ki + 1, n - 1))
        tile(ki, jnp.where(_diag_mask(t), st, NEG_INF), dpt)
        lax.fori_loop(ki + 1, n, inner, ahead)

        dk_ref[...] = dk_acc[...].astype(BF16)
        dv_ref[...] = dv_acc[...].astype(BF16)

    head = lambda w: pl.BlockSpec((s, w), lambda h, i: (0, h))
    stat = pl.BlockSpec((1, n, 1, t), lambda h, i: (h, 0, 0, 0))
    return pl.pallas_call(
        body, name="attn_bwd", grid=(HEADS, n),
        in_specs=[head(HD), pl.BlockSpec((t, HD), lambda h, i: (i, h)), pl.BlockSpec((t, DH), lambda h, i: (i, 2 * h)),
                  head(DH), stat, stat],
        out_specs=[head(HD), pl.BlockSpec((t, HD), lambda h, i: (i, h)), pl.BlockSpec((t, DH), lambda h, i: (i, h))],
        out_shape=[jax.ShapeDtypeStruct((s, HEADS * HD), F32), jax.ShapeDtypeStruct((s, HEADS * HD), BF16),
                   jax.ShapeDtypeStruct((s, WIDTH), BF16)],
        scratch_shapes=[pltpu.VMEM((t, HD), F32), pltpu.VMEM((t, DH), F32)],
        compiler_params=_cparams(("parallel", "arbitrary")),
    )(qq, kk, vv, do, lse, delta)


def _rms_bwd(dn, nhat, r, w):
    g = dn * w
    return r * (g - nhat * jnp.mean(g * nhat, axis=-1, keepdims=True)), dn * nhat


def _mla_prep_bwd(dproj, dqq, dkk, dvv, proj, rq, rkv, qn_w, kvn_w, wq, wukv, qtab, ktab, s):
    tm = _row_tile(s, 384)
    tail = PROJ_W - C_Q

    def body(dproj_in, dqq_ref, dkk_ref, dvv_ref, cq_ref, ckv_ref, rq_ref, rkv_ref, qnw_ref, kvnw_ref, wq_ref,
             wukv_ref, qc_ref, qa_ref, qb_ref, kc_ref, ka_ref, kb_ref,
             tail_ref, dq_ref, dkv_ref, dqnw_ref, dkvnw_ref):
        del dproj_in
        dcq_ref = tail_ref.at[:, pl.ds(0, Q_RANK)]
        dckv_ref = tail_ref.at[:, pl.ds(C_KV - C_Q, KV_RANK)]
        dkpe_ref = tail_ref.at[:, pl.ds(K_PE - C_Q, 2 * DH)]

        @pl.when(pl.program_id(0) == 0)
        def _():
            dqnw_ref[...] = jnp.zeros_like(dqnw_ref)
            dkvnw_ref[...] = jnp.zeros_like(dkvnw_ref)

        qc, qa, qb = qc_ref[...], qa_ref[...], qb_ref[...]
        dkp = jnp.zeros((tm, DH), F32)
        for h in range(HEADS):
            lo, mid, hi = h * HD, h * HD + DH, (h + 1) * HD
            dq_ref[:, lo:hi] = _rope64(dqq_ref[:, lo:hi], qc, -qa, -qb).astype(BF16)
            dkv_ref[:, lo:mid] = dkk_ref[:, lo:mid]
            dkv_ref[:, mid:hi] = dvv_ref[:, h * DH:(h + 1) * DH]
            dkp = dkp + dkk_ref[:, mid:hi].astype(F32)
        dcqn = _dot(dq_ref[...], wq_ref[...], "nn")
        rq_v = rq_ref[...]
        dcq, prod = _rms_bwd(dcqn, cq_ref[...].astype(F32) * rq_v, rq_v, qnw_ref[...])
        dcq_ref[...] = dcq.astype(BF16)
        dqnw_ref[...] += jnp.sum(prod, axis=0, keepdims=True)
        dckvn = _dot(dkv_ref[...], wukv_ref[...], "nn")
        rkv_v = rkv_ref[...]
        dckv, prod = _rms_bwd(dckvn, ckv_ref[...].astype(F32) * rkv_v, rkv_v, kvnw_ref[...])
        dckv_ref[...] = dckv.astype(BF16)
        dkvnw_ref[...] += jnp.sum(prod, axis=0, keepdims=True)
        dkpe_ref[:, :DH] = _rope64(dkp, kc_ref[...], -ka_ref[...], -kb_ref[...]).astype(BF16)
        dkpe_ref[:, DH:] = jnp.zeros((tm, DH), BF16)

    row = lambda w, cb: pl.BlockSpec((tm, w), lambda i: (i, cb))
    full = lambda a: pl.BlockSpec(a.shape, lambda i: (0, 0))
    wide = jax.ShapeDtypeStruct((s, HEADS * HD), BF16)
    return pl.pallas_call(
        body, name="mla_prep_bwd", grid=(s // tm,),
        in_specs=[ANY, row(HEADS * HD, 0), row(HEADS * HD, 0), row(WIDTH, 0),
                  row(Q_RANK, C_Q // Q_RANK), row(KV_RANK, C_KV // KV_RANK), row(1, 0), row(1, 0),
                  full(qn_w), full(kvn_w), full(wq), full(wukv), row(HD, 0), row(HD, 0), row(HD, 0),
                  row(DH, 0), row(DH, 0), row(DH, 0)],
        out_specs=[row(tail, C_Q // tail), row(HEADS * HD, 0), row(HEADS * HD, 0),
                   pl.BlockSpec((1, Q_RANK), lambda i: (0, 0)), pl.BlockSpec((1, KV_RANK), lambda i: (0, 0))],
        out_shape=[jax.ShapeDtypeStruct((s, PROJ_W), BF16), wide, wide,
                   jax.ShapeDtypeStruct((1, Q_RANK), F32), jax.ShapeDtypeStruct((1, KV_RANK), F32)],
        input_output_aliases={0: 0},
        compiler_params=_cparams(("arbitrary",)),
    )(dproj, dqq, dkk, dvv, proj, proj, rq, rkv, qn_w, kvn_w, wq, wukv, *qtab, *ktab)


def _rms_in_bwd(dxn, h, r, dh2, norm_w, s):
    def body(dxn_ref, h_ref, r_ref, dh2_ref, w_ref, gx_ref, gm_ref, dw_ref):
        i = pl.program_id(0)
        r_v = r_ref[...]
        dx, prod = _rms_bwd(dxn_ref[...], h_ref[...] * r_v, r_v, w_ref[...])
        dh = dh2_ref[...] + dx

        @pl.when(i == 0)
        def _():
            gm_ref[...] = dh
            dw_ref[...] = jnp.sum(prod, axis=0, keepdims=True)

        @pl.when(i > 0)
        def _():
            gx_ref[...] = dh
            dw_ref[...] += jnp.sum(prod, axis=0, keepdims=True)

    blk = pl.BlockSpec((BLK, D), lambda i: (i, 0))
    return pl.pallas_call(
        body, name="rms_in_bwd", grid=(s // BLK,),
        in_specs=[blk, blk, pl.BlockSpec((BLK, 1), lambda i: (i, 0)), blk, pl.BlockSpec((1, D), lambda i: (0, 0))],
        out_specs=[pl.BlockSpec((BLK, D), lambda i: (jnp.maximum(i - 1, 0), 0)), pl.BlockSpec((BLK, D), lambda i: (0, 0)),
                   pl.BlockSpec((1, D), lambda i: (0, 0))],
        out_shape=[jax.ShapeDtypeStruct((s - BLK, D), F32), jax.ShapeDtypeStruct((BLK, D), F32),
                   jax.ShapeDtypeStruct((1, D), F32)],
        compiler_params=_cparams(("arbitrary",)),
    )(dxn, h, r, dh2, norm_w)


def _adam_math(w, g, m, v):
    m = ADAM_B1 * m + (1.0 - ADAM_B1) * g
    v = ADAM_B2 * v + (1.0 - ADAM_B2) * (g * g)
    m_hat = m / (1.0 - ADAM_B1 ** ADAM_STEP)
    v_hat = v / (1.0 - ADAM_B2 ** ADAM_STEP)
    return -ADAM_LR * (m_hat / (jnp.sqrt(v_hat) + ADAM_EPS) + ADAM_WD * w), m, v


def _adamw(w, g, m, v, name):
    rows, cols = w.shape
    tr = rows
    for cand in (128, 64, 32, 16, 8):
        if rows % cand == 0:
            tr = cand
            break

    def body(w_ref, g_ref, m_ref, v_ref, d_ref, nm_ref, nv_ref):
        d_ref[...], nm_ref[...], nv_ref[...] = _adam_math(w_ref[...], g_ref[...], m_ref[...], v_ref[...])

    spec = pl.BlockSpec((tr, cols), lambda i: (i, 0))
    return pl.pallas_call(
        body, name=name, grid=(rows // tr,), in_specs=[spec] * 4, out_specs=[spec] * 3,
        out_shape=[jax.ShapeDtypeStruct((rows, cols), F32)] * 3,
        compiler_params=_cparams(("parallel",)),
    )(w, g, m, v)


def _adamw_t(w_t, g_t, m_t, v_t, name):
    cols, rows = w_t.shape

    def body(w_ref, g_ref, m_ref, v_ref, go_ref, d_ref, nm_ref, nv_ref):
        g = g_ref[...]
        go_ref[...] = g
        d_ref[...], nm_ref[...], nv_ref[...] = _adam_math(w_ref[...], g, m_ref[...], v_ref[...])

    blk = pl.BlockSpec((128, rows), lambda k: (k, 0))
    return pl.pallas_call(
        body, name=name, grid=(pl.cdiv(cols, 128),), in_specs=[blk] * 4, out_specs=[blk] * 4,
        out_shape=[jax.ShapeDtypeStruct((cols, rows), F32)] * 4, compiler_params=_cparams(("parallel",)),
    )(w_t, g_t, m_t, v_t)


def _adamw_small(w, gall, m, v):
    def body(w_ref, g_ref, m_ref, v_ref, gs_ref, d_ref, nm_ref, nv_ref):
        g = g_ref[0]
        for dev in range(1, 8):
            g = g + g_ref[dev]
        gs_ref[...] = g
        d_ref[...], nm_ref[...], nv_ref[...] = _adam_math(w_ref[...], g, m_ref[...], v_ref[...])

    return pl.pallas_call(
        body, name="adamw_small", out_shape=[jax.ShapeDtypeStruct((SM_ROWS, 128), F32)] * 4,
        compiler_params=pltpu.CompilerParams(vmem_limit_bytes=VMEM_LIMIT),
    )(w, gall, m, v)


ADD_ROWS = 464
HALF_BLOCKS = PK_HALF // ADD_ROWS


def _rs_core_add(srcs, land, pos):
    def body(pos_ref, *refs):
        src_refs, (land_ref, out_ref, own, sems) = refs[:len(srcs)], refs[len(srcs):]
        step = pl.program_id(0)

        def fetch(s):
            for hf in range(2):
                @pl.when(pos_ref[1] == hf)
                def _():
                    for k, r0, p, n in _grad_half(s, hf):
                        pltpu.make_async_copy(src_refs[k].at[pl.ds(r0, n), :], own.at[s % 2, pl.ds(p, n), :],
                                              sems.at[s % 2]).start()

        @pl.when(step == 0)
        def _():
            fetch(0)

        for s in range(N_CHIPS):
            @pl.when(step == s)
            def _():
                if s + 1 < N_CHIPS:
                    fetch(s + 1)
                slot = own.at[s % 2]
                pltpu.make_async_copy(slot, slot, sems.at[s % 2]).wait()
                out_ref[0] = (slot[...].astype(F32) + land_ref[0].astype(F32)).astype(BF16)

    blk = pl.BlockSpec((1, PK_HALF, D), lambda s, pos_ref: (s, 0, 0))
    return pl.pallas_call(
        body, name="rs_core_add",
        grid_spec=pltpu.PrefetchScalarGridSpec(
            num_scalar_prefetch=1, grid=(N_CHIPS,), in_specs=[ANY] * len(srcs) + [blk], out_specs=blk,
            scratch_shapes=[pltpu.VMEM((2, PK_HALF, D), BF16), pltpu.SemaphoreType.DMA((2,))]),
        out_shape=jax.ShapeDtypeStruct((N_CHIPS, PK_HALF, D), BF16),
        compiler_params=_cparams(("arbitrary",)),
    )(pos, *srcs, land)


def _rs_chip_add(cp, land, pos):
    def body(pos_ref, a_ref, l_ref, o_ref):
        o_ref[0] = ((a_ref[0].astype(F32) + l_ref[0].astype(F32)) + l_ref[1].astype(F32)) + l_ref[2].astype(F32)

    return pl.pallas_call(
        body, name="rs_chip_add",
        grid_spec=pltpu.PrefetchScalarGridSpec(
            num_scalar_prefetch=1, grid=(HALF_BLOCKS,),
            in_specs=[pl.BlockSpec((1, ADD_ROWS, D), lambda i, pos_ref: (pos_ref[0], i, 0)),
                      pl.BlockSpec((3, ADD_ROWS, D), lambda i, pos_ref: (0, i, 0))],
            out_specs=pl.BlockSpec((1, ADD_ROWS, D), lambda i, pos_ref: (pos_ref[1], i, 0))),
        out_shape=jax.ShapeDtypeStruct((2, PK_HALF, D), F32),
        compiler_params=_cparams(("parallel",)),
    )(pos, cp, land)


ANY = pl.BlockSpec(memory_space=pl.ANY)


def _mesh_pos():
    return lax.axis_index("x"), lax.axis_index("y"), lax.axis_index("c")


def _chip_peer(x, y, c, r):
    return (jnp.bitwise_xor(x, r >> 1), jnp.bitwise_xor(y, r & 1), c)


def _allgather_w_in(big0, wt, meta_loc, x2, norm_w):
    s_rows = x2.shape[0] + BLK
    tok = next(t for t in (512, 256, 128) if x2.shape[0] % t == 0)

    def body(big0_ref, wt_ref, meta_ref, x_ref, nw_ref, big_ref, metaf_ref, h_ref, xn_ref, r_ref,
             lsem, ssem, rsem, fssem, frsem, msem_s, msem_r, osem, xbuf, xnbuf, rbuf, mbuf, psem):
        del big0_ref
        x, y, c = _mesh_pos()
        j = 2 * x + y
        me, sibling = (x, y, c), (x, y, 1 - c)

        def half_wait(s_sem, r_sem, halves=1):
            rows = big_ref.at[pl.ds(0, halves * AG_IN_HALF), :]
            return pltpu.make_async_remote_copy(src_ref=rows, dst_ref=rows, send_sem=s_sem, recv_sem=r_sem,
                                                device_id=me, device_id_type=MESH)

        own_meta = pltpu.make_async_copy(meta_ref, metaf_ref.at[j], lsem)
        own_meta.start()
        for s in range(N_CHIPS):
            for hf in range(2):
                @pl.when((j == s) & (c == hf))
                def _():
                    for r in (1, 2, 3):
                        for p, b, n in _ag_half(s, True, hf):
                            pltpu.make_async_remote_copy(
                                src_ref=wt_ref.at[pl.ds(p, n), :], dst_ref=big_ref.at[pl.ds(b, n), :],
                                send_sem=ssem.at[r - 1], recv_sem=rsem.at[r - 1],
                                device_id=_chip_peer(x, y, c, r), device_id_type=MESH).start()

            @pl.when(j == s)
            def _():
                for hf in range(2):
                    for p, b, n in _ag_half(s, True, hf):
                        pltpu.make_async_remote_copy(
                            src_ref=wt_ref.at[pl.ds(p, n), :], dst_ref=big_ref.at[pl.ds(b, n), :],
                            send_sem=osem.at[0], recv_sem=osem.at[1], device_id=sibling, device_id_type=MESH).start()

        meta_copies = [pltpu.make_async_remote_copy(
            src_ref=meta_ref, dst_ref=metaf_ref.at[j], send_sem=msem_s.at[r - 1], recv_sem=msem_r.at[r - 1],
            device_id=_chip_peer(x, y, c, r), device_id_type=MESH) for r in (1, 2, 3)]
        for cp in meta_copies:
            cp.start()

        nw = nw_ref[...]

        n_tok = (s_rows - BLK) // tok

        def x_in(i, slot):
            return pltpu.make_async_copy(x_ref.at[pl.ds(pl.multiple_of(i * tok, tok), tok), :], xbuf.at[slot],
                                         psem.at[slot])

        def norm_rows(first, n, slot):
            hv = xbuf[slot, pl.ds(0, n), :]
            r = lax.rsqrt(jnp.mean(hv * hv, axis=-1, keepdims=True) + NORM_EPS)
            xnbuf[pl.ds(0, n), :] = (hv * r * nw).astype(BF16)
            rbuf[pl.ds(0, n), :] = r
            rows = pl.ds(first if isinstance(first, int) else pl.multiple_of(first, BLK), n)
            outs = [pltpu.make_async_copy(xbuf.at[slot, pl.ds(0, n), :], h_ref.at[rows, :], psem.at[2]),
                    pltpu.make_async_copy(xnbuf.at[pl.ds(0, n), :], xn_ref.at[rows, :], psem.at[3]),
                    pltpu.make_async_copy(rbuf.at[pl.ds(0, n), :], r_ref.at[rows, :], psem.at[4])]
            for cp in outs:
                cp.start(priority=1)
            for cp in outs:
                cp.wait()

        x_in(0, 0).start(priority=1)

        def token_rows(i, carry):
            slot = i % 2
            x_in(i, slot).wait()

            @pl.when(i + 1 < n_tok)
            def _():
                x_in(i + 1, 1 - slot).start(priority=1)

            norm_rows(BLK + i * tok, tok, slot)
            return carry

        lax.fori_loop(0, n_tok, token_rows, 0)
        for cp in meta_copies:
            cp.wait_recv()
        own_meta.wait()
        xbuf[0, pl.ds(0, BLK), :] = jnp.zeros((BLK, D), F32)
        for sh in range(N_CHIPS):
            pltpu.sync_copy(metaf_ref.at[sh], mbuf)
            xbuf[0, PAD_FRONT:BLK, sh * (D // N_CHIPS):(sh + 1) * (D // N_CHIPS)] = mbuf[...]
        norm_rows(0, BLK, 0)

        for r in (1, 2, 3):
            half_wait(ssem.at[r - 1], rsem.at[r - 1]).wait_recv()
            src_shard = jnp.bitwise_xor(j, r)
            for s in range(N_CHIPS):
                for hf in range(2):
                    @pl.when((src_shard == s) & (c == hf))
                    def _():
                        for _, b, n in _ag_half(s, True, hf):
                            rows = big_ref.at[pl.ds(b, n), :]
                            pltpu.make_async_remote_copy(
                                src_ref=rows, dst_ref=rows, send_sem=fssem.at[r - 1], recv_sem=frsem.at[r - 1],
                                device_id=sibling, device_id_type=MESH).start()
        for r in (1, 2, 3):
            half_wait(fssem.at[r - 1], frsem.at[r - 1]).wait_recv()
        for r in (1, 2, 3):
            half_wait(ssem.at[r - 1], rsem.at[r - 1]).wait_send()
            half_wait(fssem.at[r - 1], frsem.at[r - 1]).wait_send()
        own_rows = half_wait(osem.at[0], osem.at[1], halves=2)
        own_rows.wait_recv()
        own_rows.wait_send()
        for cp in meta_copies:
            cp.wait_send()

    dma3 = pltpu.SemaphoreType.DMA((3,))
    big, _, h, xn, r = pl.pallas_call(
        body, name="allgather_w_in", in_specs=[ANY, ANY, ANY, ANY, pl.BlockSpec(memory_space=pltpu.VMEM)],
        out_specs=[ANY] * 5,
        out_shape=[jax.ShapeDtypeStruct((BIG_ROWS, D), BF16), jax.ShapeDtypeStruct((N_CHIPS,) + meta_loc.shape, F32),
                   jax.ShapeDtypeStruct((s_rows, D), F32), jax.ShapeDtypeStruct((s_rows, D), BF16),
                   jax.ShapeDtypeStruct((s_rows, 1), F32)],
        scratch_shapes=[pltpu.SemaphoreType.DMA(()), dma3, dma3, dma3, dma3, dma3, dma3, pltpu.SemaphoreType.DMA((2,)),
                        pltpu.VMEM((2, tok, D), F32), pltpu.VMEM((tok, D), BF16), pltpu.VMEM((tok, 1), F32),
                        pltpu.VMEM(meta_loc.shape, F32), pltpu.SemaphoreType.DMA((5,))],
        input_output_aliases={0: 0},
        compiler_params=pltpu.CompilerParams(has_side_effects=True),
    )(big0, wt, meta_loc, x2, norm_w)
    return big, h, xn, r


HBM = pl.BlockSpec(memory_space=pltpu.HBM)
SEM = pl.BlockSpec(memory_space=pltpu.SEMAPHORE)
EFFECT = pltpu.SideEffectType.DATAFLOW_SIDE_EFFECTING
REST_PEERS = [(r, dc) for r in (1, 2, 3) for dc in (0, 1)]


def _allgather_rest_start(big):
    def body(big_ref, *rest):
        s_sems, r_sems = rest[:6], rest[6:12]
        x, y, c = _mesh_pos()
        j = 2 * x + y
        for s in range(N_CHIPS):
            for hf in range(2):
                @pl.when((j == s) & (c == hf))
                def _():
                    for k, (r, dc) in enumerate(REST_PEERS):
                        for _, b, n in _ag_half(s, False, hf):
                            rows = big_ref.at[pl.ds(b, n), :]
                            pltpu.make_async_remote_copy(
                                src_ref=rows, dst_ref=rows, send_sem=s_sems[k], recv_sem=r_sems[2 * (r - 1) + hf],
                                device_id=_chip_peer(x, y, dc, r), device_id_type=MESH).start()

    sem = pltpu.SemaphoreType.DMA(())
    out = pl.pallas_call(
        body, name="allgather_rest_start", out_shape=(sem,) * 12 + (pltpu.HBM(big.shape, big.dtype),),
        in_specs=(HBM,), out_specs=(SEM,) * 12 + (HBM,), input_output_aliases={0: 12},
        compiler_params=pltpu.CompilerParams(has_side_effects=EFFECT),
    )(pltpu.with_memory_space_constraint(big, pltpu.HBM))
    return out[:12], out[12]


def _allgather_rest_wait(sems, big, after):
    def body(big_ref, *rest):
        s_sems, r_sems = rest[:6], rest[6:12]
        me = _mesh_pos()
        rows = big_ref.at[pl.ds(0, AG_REST_HALF), :]
        for k in range(6):
            copy = pltpu.make_async_remote_copy(src_ref=rows, dst_ref=rows, send_sem=s_sems[k], recv_sem=r_sems[k],
                                                device_id=me, device_id_type=MESH)
            copy.wait_send()
            copy.wait_recv()

    return pl.pallas_call(
        body, name="allgather_rest_wait", out_shape=pltpu.HBM(big.shape, big.dtype),
        in_specs=(HBM,) + (SEM,) * 12 + (pl.BlockSpec(memory_space=pl.ANY),), out_specs=HBM,
        input_output_aliases={0: 0}, compiler_params=pltpu.CompilerParams(has_side_effects=EFFECT),
    )(big, *sems, after)


RS_SPLIT = 2


def _rs_core_exchange(srcs):
    def body(*refs):
        src_refs, (land_ref, ssem, rsem) = refs[:len(srcs)], refs[len(srcs):]
        x, y, c = _mesh_pos()
        me, sibling = (x, y, c), (x, y, 1 - c)
        for hf in range(2):
            @pl.when(c == hf)
            def _():
                for s in range(N_CHIPS):
                    for k, r0, p, n in _grad_half(s, 1 - hf):
                        pltpu.make_async_remote_copy(
                            src_ref=src_refs[k].at[pl.ds(r0, n), :], dst_ref=land_ref.at[s, pl.ds(p, n), :],
                            send_sem=ssem, recv_sem=rsem, device_id=sibling, device_id_type=MESH).start()
        whole = pltpu.make_async_remote_copy(src_ref=land_ref, dst_ref=land_ref, send_sem=ssem, recv_sem=rsem,
                                             device_id=me, device_id_type=MESH)
        whole.wait_recv()
        whole.wait_send()

    return pl.pallas_call(
        body, name="rs_core_exchange", in_specs=[ANY] * len(srcs), out_specs=ANY,
        out_shape=jax.ShapeDtypeStruct((N_CHIPS, PK_HALF, D), BF16),
        scratch_shapes=[pltpu.SemaphoreType.DMA(())] * 2,
        compiler_params=pltpu.CompilerParams(has_side_effects=True),
    )(*srcs)


def _rs_chip_start(cp):
    def body(cp_ref, land_ref, s1, s2, s3, r1, r2, r3, cp_thru, land_thru, token):
        x, y, c = _mesh_pos()
        j = 2 * x + y
        for r, s_sem, r_sem in zip((1, 2, 3), (s1, s2, s3), (r1, r2, r3)):
            pltpu.make_async_remote_copy(
                src_ref=cp_ref.at[jnp.bitwise_xor(j, r)], dst_ref=land_ref.at[r - 1], send_sem=s_sem, recv_sem=r_sem,
                device_id=_chip_peer(x, y, c, r), device_id_type=MESH).start()
        token[...] = jnp.zeros_like(token)

    land_shape = (3, PK_HALF, D)
    sem = pltpu.SemaphoreType.DMA(())
    out = pl.pallas_call(
        body, name="rs_chip_start",
        out_shape=(sem,) * 6 + (pltpu.HBM(cp.shape, cp.dtype), pltpu.HBM(land_shape, BF16),
                                jax.ShapeDtypeStruct((8, 128), F32)),
        in_specs=(HBM, HBM), out_specs=(SEM,) * 6 + (HBM, HBM, pl.BlockSpec(memory_space=pltpu.VMEM)),
        input_output_aliases={0: 6, 1: 7},
        compiler_params=pltpu.CompilerParams(has_side_effects=EFFECT),
    )(pltpu.with_memory_space_constraint(cp, pltpu.HBM),
      pltpu.with_memory_space_constraint(lax.empty(land_shape, BF16), pltpu.HBM))
    return out[:6], out[6], out[7], out[8]


def _rs_chip_wait(sems, cp_thru, land_thru, after):
    def body(cp_ref, land_ref, s1, s2, s3, r1, r2, r3, after_ref, cp_out, land_out):
        me = _mesh_pos()
        for r, s_sem, r_sem in zip((1, 2, 3), (s1, s2, s3), (r1, r2, r3)):
            copy = pltpu.make_async_remote_copy(src_ref=cp_ref.at[0], dst_ref=land_ref.at[r - 1], send_sem=s_sem,
                                                recv_sem=r_sem, device_id=me, device_id_type=MESH)
            copy.wait_send()
            copy.wait_recv()

    return pl.pallas_call(
        body, name="rs_chip_wait",
        out_shape=(pltpu.HBM(cp_thru.shape, cp_thru.dtype), pltpu.HBM(land_thru.shape, land_thru.dtype)),
        in_specs=(HBM, HBM) + (SEM,) * 6 + (pl.BlockSpec(memory_space=pl.ANY),), out_specs=(HBM, HBM),
        input_output_aliases={0: 0, 1: 1},
        compiler_params=pltpu.CompilerParams(has_side_effects=EFFECT),
    )(cp_thru, land_thru, *sems, after)


def _rs_finish(full, small):
    n = PK_HALF // RS_SPLIT

    def body(full_in_ref, sm_ref, full_ref, all_ref, lsem, ssem, rsem, sm_s, sm_r):
        del full_in_ref
        x, y, c = _mesh_pos()
        me, sibling = (x, y, c), (x, y, 1 - c)
        my_id = 4 * x + 2 * y + c
        own_sm = pltpu.make_async_copy(sm_ref, all_ref.at[my_id], lsem)
        own_sm.start()
        for q in range(RS_SPLIT):
            rows = full_ref.at[c, pl.ds(q * n, n), :]
            pltpu.make_async_remote_copy(src_ref=rows, dst_ref=rows, send_sem=ssem, recv_sem=rsem,
                                         device_id=sibling, device_id_type=MESH).start()
        half = pltpu.make_async_remote_copy(src_ref=full_ref.at[c], dst_ref=full_ref.at[c], send_sem=ssem,
                                            recv_sem=rsem, device_id=me, device_id_type=MESH)
        smalls = [pltpu.make_async_remote_copy(
            src_ref=sm_ref, dst_ref=all_ref.at[my_id], send_sem=sm_s.at[r - 1], recv_sem=sm_r.at[r - 1],
            device_id=(jnp.bitwise_xor(x, r >> 2), jnp.bitwise_xor(y, (r >> 1) & 1), jnp.bitwise_xor(c, r & 1)),
            device_id_type=MESH) for r in range(1, 8)]
        for cpy in smalls:
            cpy.start()
        half.wait_recv()
        half.wait_send()
        for cpy in smalls:
            cpy.wait_recv()
        for cpy in smalls:
            cpy.wait_send()
        own_sm.wait()

    dma7 = pltpu.SemaphoreType.DMA((7,))
    return pl.pallas_call(
        body, name="rs_finish", in_specs=[ANY, ANY], out_specs=[ANY, ANY],
        out_shape=[jax.ShapeDtypeStruct((2, PK_HALF, D), F32), jax.ShapeDtypeStruct((8, SM_ROWS, 128), F32)],
        scratch_shapes=[pltpu.SemaphoreType.DMA(()), pltpu.SemaphoreType.DMA(()), pltpu.SemaphoreType.DMA(()), dma7, dma7],
        input_output_aliases={0: 0},
        compiler_params=pltpu.CompilerParams(has_side_effects=True),
    )(full, small)


def _rope_tables(s):
    pos = jnp.arange(s, dtype=F32) - PAD_FRONT

    def cs(d):
        inv = ROPE_BASE ** (-jnp.arange(0, d, 2, dtype=F32) / d)
        ang = pos[:, None] * inv[None, :]
        return jnp.cos(ang), jnp.sin(ang)

    c, sn = cs(ROPE_D)
    z = jnp.zeros_like(c)
    ktab = (jnp.concatenate([c, c, z, z], axis=1), jnp.concatenate([-sn, z, z, z], axis=1),
            jnp.concatenate([z, sn, z, z], axis=1))
    one = jnp.ones_like(c)
    qtab = tuple(ATT_SCALE * jnp.concatenate(parts, axis=1) for parts in (
        [one, one, one, one, c, c, z, z], [z, z, z, z, -sn, z, z, z], [z, z, z, z, z, sn, z, z]))
    lane = jnp.arange(HD) == DH + ROPE_D
    qbias = lane.astype(F32)[None, :]
    kbias = jnp.where((pos < 0)[:, None] & lane[None, DH:], NEG_INF, 0.0).astype(F32)
    c, sn = cs(DH)
    return qtab, ktab, qbias, kbias, jnp.concatenate([c, c], axis=1), jnp.concatenate([-sn, sn], axis=1)


def _grad_mm(a, b, m, n, s, name):
    return _matmul(a, b, "tn", m, n, s, min(m, 512), min(n, 1024), s, BF16, name, n_outer=True)


def _local_step(x2, target2, hp0, big, small_w, normed=None, gather_wait=None, reduce_start=None, reduce_wait=None):
    norm_w, qn_w, kvn_w, gn_w, gn_b, fin_w = small_w
    s = x2.shape[0] + BLK
    tm = _row_tile(s, 1408)
    qtab, ktab, qbias, kbias, c128, s128 = _rope_tables(s)
    consts, consts_bwd, gamma = _ret_consts()

    h, xn, r1 = _rms_in(x2, hp0, norm_w, s) if normed is None else normed
    proj = _matmul(xn, big, "nt", s, PROJ_W, D, tm, 512, D, BF16, "proj", b_off=(BIG_IN // 512, 0))
    if gather_wait is not None:
        big = gather_wait(big, proj)
    uq = big[BIG_UQ:BIG_UQ + 384].reshape(HEADS, DH + ROPE_D, Q_RANK)
    wq = jnp.pad(uq, ((0, 0), (0, HD - DH - ROPE_D), (0, 0))).reshape(HEADS * HD, Q_RANK)
    wukv = big[BIG_UKV:BIG_UKV + 256].reshape(2 * WIDTH, KV_RANK)
    wbm = big[BIG_BM:BIG_BM + 1024].reshape(D, WIDTH)
    wbr = big[BIG_BR:BIG_BR + 1024].reshape(D, WIDTH)
    cqn, rq, ckvn, rkv, qq, kk, vv = _mla_prep(proj, qn_w, kvn_w, wq, wukv, qtab, ktab, qbias, kbias, s)
    o_mla, y_mla, lse = _attn_fwd(qq, kk, vv, proj, s)
    y_ret, on, rstd, qr, kr, rall = _ret_fwd(proj, gn_w, gn_b, c128, s128, consts, gamma, s)
    u_mla, u_ret, merged = _merge_fwd(y_mla, y_ret, wbm, wbr, proj, s)
    h2 = _out_fwd(merged, big, h, s)
    dh2, dh2b, loss_blk, dfin = _loss_bwd(h2, target2, fin_w, s)

    dproj, du_mla, du_ret = _merge_bwd(dh2b, big, u_mla, u_ret, proj, s)
    g_out = _grad_mm(merged, dh2b, D, D, s, "grad_w_out")
    dproj, do_mla, delta = _branch_mla_bwd(dproj, du_mla, wbm, o_mla, proj, s)
    g_bm = _grad_mm(du_mla, y_mla, D, WIDTH, s, "grad_w_branch_mla")
    dproj, do_ret, dgw, dgb = _branch_ret_bwd(dproj, du_ret, wbr, on, rstd, gn_w, gn_b, proj, s)
    g_br = _grad_mm(du_ret, y_ret, D, WIDTH, s, "grad_w_branch_ret")
    dproj = _ret_bwd(dproj, qr, kr, proj, rall, do_ret, c128, s128, consts_bwd, gamma, s)
    dqq, dkk, dvv = _attn_bwd(qq, kk, vv, do_mla, lse, delta.reshape(lse.shape), s)
    dproj, dq, dkv, dqnw, dkvnw = _mla_prep_bwd(dproj, dqq, dkk, dvv, proj, rq, rkv, qn_w, kvn_w, wq, wukv,
                                                qtab, ktab, s)
    g_q = _grad_mm(dq, cqn, HEADS * HD, Q_RANK, s, "grad_w_uq")
    g_ukv = _grad_mm(dkv, ckvn, 2 * WIDTH, KV_RANK, s, "grad_w_ukv")
    g_in = _grad_mm(dproj, xn, PROJ_W, D, s, "grad_w_in")

    g_uq = g_q.reshape(HEADS, HD, Q_RANK)[:, :DH + ROPE_D]
    grads = (g_in, g_out, g_bm.reshape(1024, D), g_br.reshape(1024, D), g_uq.reshape(384, D), g_ukv.reshape(256, D))
    token, travelling = (None, grads) if reduce_start is None else reduce_start(grads)
    dxn = _matmul(dproj, big, "nn", s, D, PROJ_W, _row_tile(s, 384), 512, PROJ_W, F32, "dxn",
                  b_off=(BIG_IN // PROJ_W, 0), n_outer=True, after=token)
    grad_x, gmeta_blk, dnorm = _rms_in_bwd(dxn, h, r1, dh2, norm_w, s)
    small = _small_rows((dnorm, dqnw, dkvnw, dgw, dgb, dfin), loss_blk, gmeta_blk[PAD_FRONT:])
    return grad_x, travelling if reduce_wait is None else reduce_wait(travelling, dnorm), small


def _small_rows(ws, first=None, last=None):
    def part(a, rows):
        a = a.reshape(-1, 128)
        return a if a.shape[0] == rows else jnp.pad(a, ((0, rows - a.shape[0]), (0, 0)))

    bounds = (SM_NORM, SM_QN, SM_KVN, SM_GNW, SM_GNB, SM_FIN, SM_META)
    first = jnp.zeros((SM_NORM, 128), F32) if first is None else first
    last = jnp.zeros((SM_ROWS - SM_META, 128), F32) if last is None else last
    return jnp.concatenate([part(first, SM_NORM)] + [part(w, bounds[k + 1] - bounds[k]) for k, w in enumerate(ws)]
                           + [part(last, SM_ROWS - SM_META)], axis=0)


def kernel(x, meta, norm_w, w_in, mla_q_norm_w, mla_w_uq, mla_kv_norm_w, mla_w_ukv, ret_gn_w, ret_gn_b, w_branch_mla, w_branch_ret, w_out, final_norm_w, loss_target, m_meta, m_norm_w, m_w_in, m_mla_q_norm_w, m_mla_w_uq, m_mla_kv_norm_w, m_mla_w_ukv, m_ret_gn_w, m_ret_gn_b, m_w_branch_mla, m_w_branch_ret, m_w_out, m_final_norm_w, v_meta, v_norm_w, v_w_in, v_mla_q_norm_w, v_mla_w_uq, v_mla_kv_norm_w, v_mla_w_ukv, v_ret_gn_w, v_ret_gn_b, v_w_branch_mla, v_w_branch_ret, v_w_out, v_final_norm_w):
    j = 2 * lax.axis_index("x") + lax.axis_index("y")
    tr = lambda w: w[0].T.reshape(-1, D).astype(BF16)
    pos = jnp.stack([j, lax.axis_index("c")]).astype(jnp.int32)
    put = lambda buf, rows, off: lax.dynamic_update_slice(buf, rows, (off, 0))
    big0 = lax.empty((BIG_ROWS, D), BF16)
    big0 = put(big0, w_out[0].astype(BF16), BIG_OUT + 512 * j)
    big0 = put(big0, tr(w_branch_mla), BIG_BM + 256 * j)
    big0 = put(big0, tr(w_branch_ret), BIG_BR + 256 * j)
    big0 = put(big0, tr(mla_w_uq), BIG_UQ + 96 * j)
    big0 = put(big0, tr(mla_w_ukv), BIG_UKV + 64 * j)
    big0 = put(big0, jnp.zeros((AG_JUNK_END - AG_JUNK_REST, D), BF16), AG_JUNK_REST)
    big0 = put(big0, jnp.zeros((ZERO_ROWS, D), BF16), BIG_IN + IN_WIDTH)
    wt = jnp.concatenate([tr(w_in), jnp.zeros((2 * AG_IN_HALF - IN_SHARD, D), BF16)], axis=0)
    big, h, xn, r1 = _allgather_w_in(big0, wt, meta, x[0], norm_w)
    gather_sems, big = _allgather_rest_start(big)

    def gather_wait(big_travelling, after):
        return _allgather_rest_wait(gather_sems, big_travelling, after)

    small_w = (norm_w, mla_q_norm_w, mla_kv_norm_w, ret_gn_w, ret_gn_b, final_norm_w.reshape(1, D))

    def reduce_start(grads):
        chip_part = _rs_core_add(grads, _rs_core_exchange(grads), pos)
        sems, part_thru, land_thru, token = _rs_chip_start(chip_part)
        return token, (sems, part_thru, land_thru)

    def reduce_wait(state, after):
        return _rs_chip_wait(*state, after)

    grad_x, (chip_part, land2), small = _local_step(x[0], loss_target[0], None, big, small_w, (h, xn, r1), gather_wait,
                                                    reduce_start, reduce_wait)
    full, small_all = _rs_finish(_rs_chip_add(chip_part, land2, pos), small)
    full = full.reshape(PK_ROWS, D)

    untr = lambda lo, hi, rows: full[lo:hi].reshape(rows, -1).T
    grads = {
        "w_out": full[PK_OUT:PK_BM], "w_branch_mla": untr(PK_BM, PK_BR, 512), "w_branch_ret": untr(PK_BR, PK_UQ, 512),
        "mla_w_uq": untr(PK_UQ, PK_UKV, 384), "mla_w_ukv": untr(PK_UKV, PK_PAD, 512),
    }
    big_w = {"mla_w_uq": (mla_w_uq, m_mla_w_uq, v_mla_w_uq),
             "mla_w_ukv": (mla_w_ukv, m_mla_w_ukv, v_mla_w_ukv),
             "w_branch_mla": (w_branch_mla, m_w_branch_mla, v_w_branch_mla),
             "w_branch_ret": (w_branch_ret, m_w_branch_ret, v_w_branch_ret), "w_out": (w_out, m_w_out, v_w_out)}
    res = {"w_in": tuple(a.T[None] for a in _adamw_t(w_in[0].T, full, m_w_in[0].T, v_w_in[0].T, "adamw_w_in"))}
    for name, (w, m, v) in big_w.items():
        d, nm, nv = _adamw(w[0], grads[name], m[0], v[0], "adamw_" + name)
        res[name] = (grads[name][None], d[None], nm[None], nv[None])

    small_m = (m_norm_w, m_mla_q_norm_w, m_mla_kv_norm_w, m_ret_gn_w, m_ret_gn_b, m_final_norm_w.reshape(1, D))
    small_v = (v_norm_w, v_mla_q_norm_w, v_mla_kv_norm_w, v_ret_gn_w, v_ret_gn_b, v_final_norm_w.reshape(1, D))
    gs, ds, ms, vs = _adamw_small(_small_rows(small_w), small_all, _small_rows(small_m), _small_rows(small_v))
    names = ["norm_w", "mla_q_norm_w", "mla_kv_norm_w", "ret_gn_w", "ret_gn_b", "final_norm_w"]
    bounds = [SM_NORM, SM_QN, SM_KVN, SM_GNW, SM_GNB, SM_FIN]
    for k, name in enumerate(names):
        shape = (D,) if name == "final_norm_w" else (1, -1)
        rows = small_w[k].size // 128
        res[name] = tuple(a[bounds[k]:bounds[k] + rows].reshape(shape) for a in (gs, ds, ms, vs))
    g_meta = lax.dynamic_slice_in_dim(gs[SM_META:SM_META + 256].reshape(N_META, D), j * (D // N_CHIPS), D // N_CHIPS, axis=1)
    res["meta"] = (g_meta,) + tuple(_adamw(meta, g_meta, m_meta, v_meta, "adamw_meta"))

    order = ["meta", "norm_w", "w_in", "mla_q_norm_w", "mla_w_uq", "mla_kv_norm_w", "mla_w_ukv", "ret_gn_w", "ret_gn_b",
             "w_branch_mla", "w_branch_ret", "w_out", "final_norm_w"]
    return (gs[0, 0], grad_x[None]) + tuple(res[n][k] for k in range(4) for n in order)
```

```python
import functools
import math

import numpy as np
import jax
import jax.numpy as jnp
from jax import lax
from jax.experimental import pallas as pl
from jax.experimental.pallas import tpu as pltpu

F32 = jnp.float32
BF16 = jnp.bfloat16
MESH = pl.DeviceIdType.MESH

D = 2048
N_META = 16
BLK = 128
PAD_FRONT = BLK - N_META
HEADS = 8
DH = 128
ROPE_D = 64
Q_RANK = 512
KV_RANK = 256
WIDTH = HEADS * DH
ROPE_BASE = 10000.0
NORM_EPS = 1e-6
GN_EPS = 1e-5
NEG_INF = -1e30
ATT_SCALE = (DH + ROPE_D) ** -0.5
RET_SCALE = DH ** -0.5
IN_WIDTH = 10048
N_CHIPS = 4
IN_SHARD = IN_WIDTH // N_CHIPS
ADAM_LR, ADAM_B1, ADAM_B2, ADAM_EPS, ADAM_WD, ADAM_STEP = 0.001, 0.9, 0.999, 1e-08, 0.01, 10

R_Q, R_K, R_V, Z_RET, Z_MLA, GATE0, GATE1 = 0, 1024, 2048, 3072, 4096, 5120, 5120 + D
C_Q = 5120 + 2 * D
C_KV = C_Q + Q_RANK
K_PE = C_KV + KV_RANK
PROJ_W = 10240
IN_RUNS = ((0, 832, C_Q), (832, 1856, Z_MLA), (1856, 4928, R_Q), (4928, 5952, Z_RET), (5952, IN_WIDTH, GATE0))

BIG_IN, BIG_OUT, BIG_BM, BIG_BR, BIG_UQ, BIG_UKV, BIG_JUNK, BIG_ROWS = 0, 10240, 12288, 13312, 14336, 14720, 14976, 15360
ZERO_ROWS = PROJ_W - IN_WIDTH
PK_IN, PK_OUT, PK_BM, PK_BR, PK_UQ, PK_UKV, PK_PAD, PK_ROWS = 0, 2512, 3024, 3280, 3536, 3632, 3696, 3712
PK_HALF = PK_ROWS // 2
SM_LOSS, SM_NORM, SM_QN, SM_KVN, SM_GNW, SM_GNB, SM_FIN, SM_META, SM_ROWS = 0, 8, 24, 32, 40, 48, 56, 72, 328

VMEM_LIMIT = 56 * 1024 * 1024


def _pieces(shard):
    j = shard
    out = [(PK_OUT, BIG_OUT + 512 * j, 512), (PK_BM, BIG_BM + 256 * j, 256), (PK_BR, BIG_BR + 256 * j, 256),
           (PK_UQ, BIG_UQ + 96 * j, 96), (PK_UKV, BIG_UKV + 64 * j, 64)]
    for lo, hi, new in IN_RUNS:
        a, b = max(lo, IN_SHARD * j), min(hi, IN_SHARD * (j + 1))
        if b > a:
            out.append((PK_IN + a - IN_SHARD * j, BIG_IN + new + a - lo, b - a))
    out += [(PK_PAD, BIG_JUNK + 16 * j, 16)]
    return out


AG_JUNK_REST, AG_JUNK_IN, AG_JUNK_END = BIG_JUNK, BIG_JUNK + 128, BIG_JUNK + 192
AG_REST_HALF, AG_IN_HALF = 608, 1264


def _ag_half(shard, w_in, half):
    pieces = [(b, n) for p, b, n in _pieces(shard) if p < PK_PAD and (p < PK_OUT) == w_in]
    pieces.append((AG_JUNK_IN + 16 * shard, 16) if w_in else (AG_JUNK_REST + 32 * shard, 32))
    size = AG_IN_HALF if w_in else AG_REST_HALF
    out, pos = [], 0
    for b, n in pieces:
        s, e = max(pos, half * size), min(pos + n, (half + 1) * size)
        if e > s:
            out.append((s, b + s - pos, e - s))
        pos += n
    assert pos == 2 * size
    return out


def _grad_half(shard, half):
    j = shard
    pieces = [(0, b - BIG_IN, p, n) for p, b, n in _pieces(j) if p < PK_OUT]
    pieces += [(1, 512 * j, PK_OUT, 512), (2, 256 * j, PK_BM, 256), (3, 256 * j, PK_BR, 256), (4, 96 * j, PK_UQ, 96),
               (5, 64 * j, PK_UKV, 64), (0, IN_WIDTH, PK_PAD, PK_ROWS - PK_PAD)]
    lo, hi = half * PK_HALF, (half + 1) * PK_HALF
    out = []
    for k, r0, p, n in pieces:
        s, e = max(p, lo), min(p + n, hi)
        if e > s:
            out.append((k, r0 + s - p, s - lo, e - s))
    assert sum(n for _, _, _, n in out) == PK_HALF
    return out


def _row_tile(rows, cap):
    best = BLK
    for t in range(BLK, cap + 1, BLK):
        if rows % t == 0:
            best = t
    return best


def _cparams(sem):
    return pltpu.CompilerParams(dimension_semantics=sem, vmem_limit_bytes=VMEM_LIMIT)


def _dot(a, b, form):
    dn = {"nt": (((1,), (1,)), ((), ())), "nn": (((1,), (0,)), ((), ())), "tn": (((0,), (0,)), ((), ()))}[form]
    return lax.dot_general(a, b, dn, preferred_element_type=F32)


def _sigmoid(v):
    return 1.0 / (1.0 + jnp.exp(-v))


def _matmul(a, b, form, m, n, k, tm, tn, tk, out_dtype, name, a_off=(0, 0), b_off=(0, 0), n_outer=False, after=None):
    nk = k // tk
    gi, gj = m // tm, n // tn

    def ij(g0, g1):
        return (g1, g0) if n_outer else (g0, g1)

    if form == "nt":
        a_spec = pl.BlockSpec((tm, tk), lambda g0, g1, kk: (ij(g0, g1)[0] + a_off[0], kk + a_off[1]))
        b_spec = pl.BlockSpec((tn, tk), lambda g0, g1, kk: (ij(g0, g1)[1] + b_off[0], kk + b_off[1]))
    elif form == "nn":
        a_spec = pl.BlockSpec((tm, tk), lambda g0, g1, kk: (ij(g0, g1)[0] + a_off[0], kk + a_off[1]))
        b_spec = pl.BlockSpec((tk, tn), lambda g0, g1, kk: (kk + b_off[0], ij(g0, g1)[1] + b_off[1]))
    else:
        a_spec = pl.BlockSpec((tk, tm), lambda g0, g1, kk: (kk + a_off[0], ij(g0, g1)[0] + a_off[1]))
        b_spec = pl.BlockSpec((tk, tn), lambda g0, g1, kk: (kk + b_off[0], ij(g0, g1)[1] + b_off[1]))
    o_spec = pl.BlockSpec((tm, tn), lambda g0, g1, kk: ij(g0, g1))

    def body(a_ref, b_ref, *rest):
        o_ref, *acc = rest[0 if after is None else 1:]
        p = _dot(a_ref[...], b_ref[...], form)
        if nk == 1:
            o_ref[...] = p.astype(o_ref.dtype)
        else:
            acc_ref, = acc
            kk = pl.program_id(2)

            @pl.when(kk == 0)
            def _():
                acc_ref[...] = p

            @pl.when(kk > 0)
            def _():
                acc_ref[...] += p

            @pl.when(kk == nk - 1)
            def _():
                o_ref[...] = acc_ref[...].astype(o_ref.dtype)

    extra = [] if after is None else [after]
    return pl.pallas_call(
        body, name=name, grid=(gj, gi, nk) if n_outer else (gi, gj, nk),
        in_specs=[a_spec, b_spec] + [pl.BlockSpec(memory_space=pl.ANY)] * len(extra), out_specs=o_spec,
        out_shape=jax.ShapeDtypeStruct((m, n), out_dtype),
        scratch_shapes=[] if nk == 1 else [pltpu.VMEM((tm, tn), F32)],
        compiler_params=_cparams(("parallel", "parallel", "arbitrary")),
    )(a, b, *extra)


def _rms_in(x, hp0, norm_w, s):
    def body(x_ref, hp0_ref, w_ref, h_ref, xn_ref, r_ref):
        def run(hv):
            r = lax.rsqrt(jnp.mean(hv * hv, axis=-1, keepdims=True) + NORM_EPS)
            h_ref[...] = hv
            xn_ref[...] = (hv * r * w_ref[...]).astype(BF16)
            r_ref[...] = r

        @pl.when(pl.program_id(0) == 0)
        def _():
            run(hp0_ref[...])

        @pl.when(pl.program_id(0) > 0)
        def _():
            run(x_ref[...])

    return pl.pallas_call(
        body, name="rms_in", grid=(s // BLK,),
        in_specs=[pl.BlockSpec((BLK, D), lambda i: (jnp.maximum(i - 1, 0), 0)),
                  pl.BlockSpec((BLK, D), lambda i: (0, 0)), pl.BlockSpec((1, D), lambda i: (0, 0))],
        out_specs=[pl.BlockSpec((BLK, D), lambda i: (i, 0)), pl.BlockSpec((BLK, D), lambda i: (i, 0)),
                   pl.BlockSpec((BLK, 1), lambda i: (i, 0))],
        out_shape=[jax.ShapeDtypeStruct((s, D), F32), jax.ShapeDtypeStruct((s, D), BF16),
                   jax.ShapeDtypeStruct((s, 1), F32)],
        compiler_params=_cparams(("arbitrary",)),
    )(x, hp0, norm_w)


def _rope64(t, c, sa, sb):
    return t * c + pltpu.roll(t, t.shape[1] - ROPE_D // 2, 1) * sa + pltpu.roll(t, ROPE_D // 2, 1) * sb


def _rope128(t, c, sg):
    return t * c + pltpu.roll(t, DH // 2, 1) * sg


HD = 2 * DH


def _mla_prep(proj, qn_w, kvn_w, wq, wukv, qtab, ktab, qbias, kbias, s):
    tm = _row_tile(s, 384)

    def body(cq_ref, ckv_ref, kpe_ref, qnw_ref, kvnw_ref, wq_ref, wukv_ref, qc_ref, qa_ref, qb_ref,
             kc_ref, ka_ref, kb_ref, qbias_ref, kbias_ref, cqn_ref, rq_ref, ckvn_ref, rkv_ref, qq_ref, kk_ref, vv_ref):
        cq = cq_ref[...].astype(F32)
        rq = lax.rsqrt(jnp.mean(cq * cq, axis=-1, keepdims=True) + NORM_EPS)
        cqn = (cq * rq * qnw_ref[...]).astype(BF16)
        ckv = ckv_ref[...].astype(F32)
        rkv = lax.rsqrt(jnp.mean(ckv * ckv, axis=-1, keepdims=True) + NORM_EPS)
        ckvn = (ckv * rkv * kvnw_ref[...]).astype(BF16)
        cqn_ref[...] = cqn
        rq_ref[...] = rq
        ckvn_ref[...] = ckvn
        rkv_ref[...] = rkv
        q = _dot(cqn, wq_ref[...], "nt")
        kv = _dot(ckvn, wukv_ref[...], "nt")
        kp = (_rope64(kpe_ref[...].astype(F32), kc_ref[...], ka_ref[...], kb_ref[...]) + kbias_ref[...]).astype(BF16)
        qc, qa, qb, qbias = qc_ref[...], qa_ref[...], qb_ref[...], qbias_ref[...]
        ones = jnp.ones((tm, DH), BF16)
        for h in range(HEADS):
            lo, mid, hi = h * HD, h * HD + DH, (h + 1) * HD
            qq_ref[:, lo:hi] = (_rope64(q[:, lo:hi], qc, qa, qb) + qbias).astype(BF16)
            kk_ref[:, lo:mid] = kv[:, lo:mid].astype(BF16)
            kk_ref[:, mid:hi] = kp
            vv_ref[:, lo:mid] = kv[:, mid:hi].astype(BF16)
            vv_ref[:, mid:hi] = ones

    row = lambda w, cb: pl.BlockSpec((tm, w), lambda i: (i, cb))
    full = lambda a: pl.BlockSpec(a.shape, lambda i: (0, 0))
    wide = jax.ShapeDtypeStruct((s, HEADS * HD), BF16)
    return pl.pallas_call(
        body, name="mla_prep", grid=(s // tm,),
        in_specs=[row(Q_RANK, C_Q // Q_RANK), row(KV_RANK, C_KV // KV_RANK), row(DH, K_PE // DH),
                  full(qn_w), full(kvn_w), full(wq), full(wukv), row(HD, 0), row(HD, 0), row(HD, 0),
                  row(DH, 0), row(DH, 0), row(DH, 0), full(qbias), row(DH, 0)],
        out_specs=[row(Q_RANK, 0), row(1, 0), row(KV_RANK, 0), row(1, 0), row(HEADS * HD, 0), row(HEADS * HD, 0),
                   row(HEADS * HD, 0)],
        out_shape=[jax.ShapeDtypeStruct((s, Q_RANK), BF16), jax.ShapeDtypeStruct((s, 1), F32),
                   jax.ShapeDtypeStruct((s, KV_RANK), BF16), jax.ShapeDtypeStruct((s, 1), F32), wide, wide, wide],
        compiler_params=_cparams(("parallel",)),
    )(proj, proj, proj, qn_w, kvn_w, wq, wukv, *qtab, *ktab, qbias, kbias)


def _diag_mask(t):
    return lax.broadcasted_iota(jnp.int32, (t, t), 0) <= lax.broadcasted_iota(jnp.int32, (t, t), 1)


def _silu(z):
    return z * _sigmoid(z)


def _attn_fwd(qq, kk, vv, proj, s):
    t = _row_tile(s, 384)
    n = s // t

    def body(q_ref, k_ref, v_ref, z_ref, o_ref, y_ref, lse_ref, acc_ref, m_ref):
        qi = pl.program_id(1)
        q = q_ref[...]
        m_ref[...] = jnp.full(m_ref.shape, NEG_INF, F32)
        acc_ref[...] = jnp.zeros(acc_ref.shape, F32)

        def keys(ki):
            return pl.ds(pl.multiple_of(ki * t, t), t)

        def scores(ki):
            return _dot(k_ref[keys(ki), :], q, "nt")

        def tile(ki, st):
            m_old = m_ref[...]
            m_new = jnp.maximum(m_old, jnp.max(st, axis=0, keepdims=True))
            pt = jnp.exp(st - m_new).astype(BF16)
            acc_ref[...] = jnp.exp(m_old - m_new) * acc_ref[...] + _dot(v_ref[keys(ki), :], pt, "tn")
            m_ref[...] = m_new

        def inner(ki, st):
            st_after = scores(ki + 1)
            tile(ki, st)
            return st_after

        st_diag = lax.fori_loop(0, qi, inner, scores(0))
        tile(qi, jnp.where(_diag_mask(t), st_diag, NEG_INF))

        l = acc_ref[DH:DH + 1, :]
        o = (acc_ref[:DH, :] / l).T
        o_ref[...] = o.astype(BF16)
        y_ref[...] = (o * _silu(z_ref[...].astype(F32))).astype(BF16)
        lse_ref[0, 0] = m_ref[...] + jnp.log(l)

    qtile = pl.BlockSpec((t, DH), lambda h, i: (i, h))
    head = pl.BlockSpec((s, HD), lambda h, i: (0, h))
    return pl.pallas_call(
        body, name="attn_fwd", grid=(HEADS, s // t),
        in_specs=[pl.BlockSpec((t, HD), lambda h, i: (i, h)), head, head,
                  pl.BlockSpec((t, DH), lambda h, i: (i, Z_MLA // DH + h))],
        out_specs=[qtile, qtile, pl.BlockSpec((1, 1, 1, t), lambda h, i: (h, i, 0, 0))],
        out_shape=[jax.ShapeDtypeStruct((s, WIDTH), BF16), jax.ShapeDtypeStruct((s, WIDTH), BF16),
                   jax.ShapeDtypeStruct((HEADS, s // t, 1, t), F32)],
        scratch_shapes=[pltpu.VMEM((HD, t), F32), pltpu.VMEM((1, t), F32)],
        compiler_params=_cparams(("parallel", "arbitrary")),
    )(qq, kk, vv, proj)


RET_CHUNKS = 3


def _ret_consts():
    log_g = np.log1p(-(2.0 ** (-5.0 - np.arange(HEADS, dtype=np.float64))))
    n = np.arange(BLK, dtype=np.float64)
    diff = n[:, None] - n[None, :]
    decay = np.where(diff >= 0, np.exp(log_g[:, None, None] * np.maximum(diff, 0.0)), 0.0)
    zeta = np.exp(log_g[:, None] * (BLK - 1.0 - n))[:, :, None]
    xi = np.exp(log_g[:, None] * (n + 1.0))[:, :, None]
    gamma = [float(np.float32(np.exp(g * BLK))) for g in log_g]
    wide = lambda a: np.broadcast_to(a, (HEADS, BLK, DH))
    return ((jnp.asarray(decay, F32), jnp.asarray(wide(zeta), F32), jnp.asarray(wide(xi), F32)),
            (jnp.asarray(decay, F32), jnp.asarray(zeta, F32), jnp.asarray(xi, F32)), gamma)


def _ret_fwd(proj, gn_w, gn_b, c128, s128, consts, gamma, s):
    nb = s // BLK
    cps = RET_CHUNKS if nb % RET_CHUNKS == 0 else 1
    decay, zeta, xi = consts

    def body(rq_ref, rk_ref, rv_ref, z_ref, gw_ref, gb_ref, c_ref, s_ref, dm_ref, zt_ref, xi_ref,
             y_ref, on_ref, rstd_ref, qr_ref, kr_ref, rall_ref, state):
        @pl.when(pl.program_id(0) == 0)
        def _():
            state[...] = jnp.zeros_like(state)

        for sub in range(cps):
            rows = slice(sub * BLK, (sub + 1) * BLK)
            c, sg = c_ref[rows, :], s_ref[rows, :]
            for h in range(HEADS):
                sl = slice(h * DH, (h + 1) * DH)
                q = _rope128(rq_ref[rows, sl].astype(F32), c, sg).astype(BF16)
                kf = _rope128(rk_ref[rows, sl].astype(F32), c, sg) * RET_SCALE
                k = kf.astype(BF16)
                v = rv_ref[rows, sl]
                qr_ref[rows, sl] = q
                kr_ref[rows, sl] = k
                r_prev = state[h]
                rall_ref[sub, h] = r_prev
                a = _dot(q, k, "nt") * dm_ref[h]
                o = _dot(a.astype(BF16), v, "nn") + _dot(q, r_prev.astype(BF16), "nn") * xi_ref[h]
                state[h] = r_prev * gamma[h] + _dot((kf * zt_ref[h]).astype(BF16), v, "tn")
                mu = jnp.mean(o, axis=-1, keepdims=True)
                var = jnp.mean(jnp.square(o - mu), axis=-1, keepdims=True)
                rstd = lax.rsqrt(var + GN_EPS)
                on = (o - mu) * rstd
                rstd_ref[h, rows, :] = rstd
                on_ref[rows, sl] = on.astype(BF16)
                ogn = on * gw_ref[:, sl] + gb_ref[:, sl]
                y_ref[rows, sl] = (ogn * _silu(z_ref[rows, sl].astype(F32))).astype(BF16)

    seg = lambda cb: pl.BlockSpec((cps * BLK, WIDTH), lambda i: (i, cb))
    full = lambda a: pl.BlockSpec(a.shape, lambda i: (0,) * a.ndim)
    tab = pl.BlockSpec((cps * BLK, DH), lambda i: (i, 0))
    return pl.pallas_call(
        body, name="ret_fwd", grid=(nb // cps,),
        in_specs=[seg(R_Q // WIDTH), seg(R_K // WIDTH), seg(R_V // WIDTH), seg(Z_RET // WIDTH), full(gn_w), full(gn_b),
                  tab, tab, full(decay), full(zeta), full(xi)],
        out_specs=[seg(0), seg(0), pl.BlockSpec((HEADS, cps * BLK, 1), lambda i: (0, i, 0)), seg(0), seg(0),
                   pl.BlockSpec((cps, HEADS, DH, DH), lambda i: (i, 0, 0, 0))],
        out_shape=[jax.ShapeDtypeStruct((s, WIDTH), BF16), jax.ShapeDtypeStruct((s, WIDTH), BF16),
                   jax.ShapeDtypeStruct((HEADS, s, 1), F32), jax.ShapeDtypeStruct((s, WIDTH), BF16),
                   jax.ShapeDtypeStruct((s, WIDTH), BF16), jax.ShapeDtypeStruct((nb, HEADS, DH, DH), F32)],
        scratch_shapes=[pltpu.VMEM((HEADS, DH, DH), F32)],
        compiler_params=_cparams(("arbitrary",)),
    )(proj, proj, proj, proj, gn_w, gn_b, c128, s128, decay, zeta, xi)


def _merge_fwd(y_mla, y_ret, wbm, wbr, proj, s):
    tm, tn = _row_tile(s, 1408), 512

    def body(ym_ref, yr_ref, wm_ref, wr_ref, g0_ref, g1_ref, um_ref, ur_ref, mg_ref):
        um = _dot(ym_ref[...], wm_ref[...], "nt")
        ur = _dot(yr_ref[...], wr_ref[...], "nt")
        um_ref[...] = um.astype(BF16)
        ur_ref[...] = ur.astype(BF16)
        mg_ref[...] = (_sigmoid(g0_ref[...].astype(F32)) * um + _sigmoid(g1_ref[...].astype(F32)) * ur).astype(BF16)

    yspec = pl.BlockSpec((tm, WIDTH), lambda i, j: (i, 0))
    wspec = pl.BlockSpec((tn, WIDTH), lambda i, j: (j, 0))
    ospec = pl.BlockSpec((tm, tn), lambda i, j: (i, j))
    return pl.pallas_call(
        body, name="merge_fwd", grid=(s // tm, D // tn),
        in_specs=[yspec, yspec, wspec, wspec, pl.BlockSpec((tm, tn), lambda i, j: (i, GATE0 // tn + j)),
                  pl.BlockSpec((tm, tn), lambda i, j: (i, GATE1 // tn + j))],
        out_specs=[ospec, ospec, ospec],
        out_shape=[jax.ShapeDtypeStruct((s, D), BF16)] * 3,
        compiler_params=_cparams(("parallel", "parallel")),
    )(y_mla, y_ret, wbm, wbr, proj, proj)


def _out_fwd(merged, big, h, s):
    tm, tn = _row_tile(s, 1408), 1024

    def body(m_ref, w_ref, h_ref, o_ref):
        o_ref[...] = h_ref[...] + _dot(m_ref[...], w_ref[...], "nn")

    return pl.pallas_call(
        body, name="out_fwd", grid=(s // tm, D // tn),
        in_specs=[pl.BlockSpec((tm, D), lambda i, j: (i, 0)), pl.BlockSpec((D, tn), lambda i, j: (BIG_OUT // D, j)),
                  pl.BlockSpec((tm, tn), lambda i, j: (i, j))],
        out_specs=pl.BlockSpec((tm, tn), lambda i, j: (i, j)),
        out_shape=jax.ShapeDtypeStruct((s, D), F32),
        compiler_params=_cparams(("parallel", "parallel")),
    )(merged, big, h)


def _loss_bwd(h2, target, fin_w, s):
    tb = _row_tile(s, 384)
    sub, nb = tb // BLK, s // tb

    def body(h2_ref, *rest):
        t_refs, (w_ref, dh_ref, dhb_ref, loss_ref, dw_ref) = rest[:sub], rest[sub:]
        i = pl.program_id(0)

        @pl.when(i == 0)
        def _():
            loss_ref[...] = jnp.zeros_like(loss_ref)
            dw_ref[...] = jnp.zeros_like(dw_ref)

        w = w_ref[...]

        def chunk(k):
            rows = slice(k * BLK, (k + 1) * BLK)
            hv = h2_ref[rows, :]
            r = lax.rsqrt(jnp.mean(hv * hv, axis=-1, keepdims=True) + NORM_EPS)
            nrm = hv * r
            e = nrm * w - t_refs[k][...]
            loss_ref[...] += jnp.full(loss_ref.shape, 0.5 / D, F32) * jnp.sum(e * e)
            dy = e * (1.0 / D)
            dw_ref[...] += jnp.sum(dy * nrm, axis=0, keepdims=True)
            g = dy * w
            dh = r * (g - nrm * jnp.mean(g * nrm, axis=-1, keepdims=True))
            dh_ref[rows, :] = dh
            dhb_ref[rows, :] = dh.astype(BF16)

        @pl.when(i == 0)
        def _():
            dh_ref[:BLK, :] = jnp.zeros((BLK, D), F32)
            dhb_ref[:BLK, :] = jnp.zeros((BLK, D), BF16)

        @pl.when(i > 0)
        def _():
            chunk(0)

        for k in range(1, sub):
            chunk(k)

    blk = pl.BlockSpec((tb, D), lambda i: (i, 0))
    tgt = [pl.BlockSpec((BLK, D), lambda i, k=k: (jnp.maximum(i * sub + k - 1, 0), 0)) for k in range(sub)]
    return pl.pallas_call(
        body, name="loss_bwd", grid=(nb,),
        in_specs=[blk] + tgt + [pl.BlockSpec((1, D), lambda i: (0, 0))],
        out_specs=[blk, blk, pl.BlockSpec((8, 128), lambda i: (0, 0)), pl.BlockSpec((1, D), lambda i: (0, 0))],
        out_shape=[jax.ShapeDtypeStruct((s, D), F32), jax.ShapeDtypeStruct((s, D), BF16),
                   jax.ShapeDtypeStruct((8, 128), F32), jax.ShapeDtypeStruct((1, D), F32)],
        compiler_params=_cparams(("arbitrary",)),
    )(h2, *[target] * sub, fin_w)


def _merge_bwd(dh2b, big, u_mla, u_ret, proj, s):
    tm, tn = _row_tile(s, 1408), 512

    def body(d_ref, w_ref, um_ref, ur_ref, gate_ref, dproj_ref, dum_ref, dur_ref, dm_ref):
        branch, j = pl.program_id(1), pl.program_id(2)
        cols = pl.ds(pl.multiple_of(j * tn, tn), tn)

        @pl.when(branch == 0)
        def _():
            dm_ref[:, cols] = _dot(d_ref[...], w_ref[...], "nt")

        dm = dm_ref[:, cols]
        gt = _sigmoid(gate_ref[...].astype(F32))

        @pl.when(branch == 0)
        def _():
            dproj_ref[...] = (dm * um_ref[...].astype(F32) * gt * (1.0 - gt)).astype(BF16)
            dum_ref[...] = (dm * gt).astype(BF16)

        @pl.when(branch == 1)
        def _():
            dproj_ref[...] = (dm * ur_ref[...].astype(F32) * gt * (1.0 - gt)).astype(BF16)
            dur_ref[...] = (dm * gt).astype(BF16)

    last = D // tn - 1
    mla = pl.BlockSpec((tm, tn), lambda i, b, j: (i, jnp.where(b == 0, j, last)))
    ret = pl.BlockSpec((tm, tn), lambda i, b, j: (i, jnp.where(b == 0, 0, j)))
    gate = pl.BlockSpec((tm, tn), lambda i, b, j: (i, GATE0 // tn + b * (D // tn) + j))
    return pl.pallas_call(
        body, name="merge_bwd", grid=(s // tm, 2, D // tn),
        in_specs=[pl.BlockSpec((tm, D), lambda i, b, j: (i, 0)),
                  pl.BlockSpec((tn, D), lambda i, b, j: (BIG_OUT // tn + jnp.where(b == 0, j, last), 0)), mla, ret, gate],
        out_specs=[gate, mla, ret],
        out_shape=[jax.ShapeDtypeStruct((s, PROJ_W), BF16), jax.ShapeDtypeStruct((s, D), BF16),
                   jax.ShapeDtypeStruct((s, D), BF16)],
        scratch_shapes=[pltpu.VMEM((tm, D), F32)],
        compiler_params=_cparams(("parallel", "arbitrary", "arbitrary")),
    )(dh2b, big, u_mla, u_ret, proj)


def _dsilu(z):
    sg = _sigmoid(z)
    return sg * (1.0 + z * (1.0 - sg))


def _branch_mla_bwd(dproj, du, wbm, o_mla, proj, s):
    tm = _row_tile(s, 384)

    def body(dproj_in, du_ref, w_ref, o_ref, z_ref, dz_ref, do_ref, delta_ref):
        del dproj_in
        dy = _dot(du_ref[...], w_ref[...], "nn")
        z = z_ref[...].astype(F32)
        o = o_ref[...].astype(F32)
        do = dy * _silu(z)
        do_ref[...] = do.astype(BF16)
        dz_ref[...] = (dy * o * _dsilu(z)).astype(BF16)
        prod = do * o
        for h in range(HEADS):
            delta_ref[h] = jnp.sum(prod[:, h * DH:(h + 1) * DH], axis=-1, keepdims=True)

    row = lambda w, cb: pl.BlockSpec((tm, w), lambda i: (i, cb))
    return pl.pallas_call(
        body, name="branch_mla_bwd", grid=(s // tm,),
        in_specs=[ANY, row(D, 0), pl.BlockSpec((D, WIDTH), lambda i: (0, 0)), row(WIDTH, 0),
                  row(WIDTH, Z_MLA // WIDTH)],
        out_specs=[row(WIDTH, Z_MLA // WIDTH), row(WIDTH, 0), pl.BlockSpec((HEADS, tm, 1), lambda i: (0, i, 0))],
        out_shape=[jax.ShapeDtypeStruct((s, PROJ_W), BF16), jax.ShapeDtypeStruct((s, WIDTH), BF16),
                   jax.ShapeDtypeStruct((HEADS, s, 1), F32)],
        input_output_aliases={0: 0},
        compiler_params=_cparams(("parallel",)),
    )(dproj, du, wbm, o_mla, proj)


def _branch_ret_bwd(dproj, du, wbr, on, rstd, gn_w, gn_b, proj, s):
    tm = _row_tile(s, 384)

    def body(dproj_in, du_ref, w_ref, on_ref, rstd_ref, gw_ref, gb_ref, z_ref, dz_ref, do_ref, dgw_ref, dgb_ref):
        del dproj_in

        @pl.when(pl.program_id(0) == 0)
        def _():
            dgw_ref[...] = jnp.zeros_like(dgw_ref)
            dgb_ref[...] = jnp.zeros_like(dgb_ref)

        dy = _dot(du_ref[...], w_ref[...], "nn")
        z = z_ref[...].astype(F32)
        on = on_ref[...].astype(F32)
        gw = gw_ref[...]
        dogn = dy * _silu(z)
        dz_ref[...] = (dy * (on * gw + gb_ref[...]) * _dsilu(z)).astype(BF16)
        dgw_ref[...] += jnp.sum(dogn * on, axis=0, keepdims=True)
        dgb_ref[...] += jnp.sum(dogn, axis=0, keepdims=True)
        don = dogn * gw
        for h in range(HEADS):
            sl = slice(h * DH, (h + 1) * DH)
            dn, nh = don[:, sl], on[:, sl]
            do = rstd_ref[h] * (dn - jnp.mean(dn, axis=-1, keepdims=True)
                                - nh * jnp.mean(dn * nh, axis=-1, keepdims=True))
            do_ref[:, sl] = do.astype(BF16)

    row = lambda w, cb: pl.BlockSpec((tm, w), lambda i: (i, cb))
    vec = pl.BlockSpec((1, WIDTH), lambda i: (0, 0))
    return pl.pallas_call(
        body, name="branch_ret_bwd", grid=(s // tm,),
        in_specs=[ANY, row(D, 0), pl.BlockSpec((D, WIDTH), lambda i: (0, 0)), row(WIDTH, 0),
                  pl.BlockSpec((HEADS, tm, 1), lambda i: (0, i, 0)), vec, vec, row(WIDTH, Z_RET // WIDTH)],
        out_specs=[row(WIDTH, Z_RET // WIDTH), row(WIDTH, 0), vec, vec],
        out_shape=[jax.ShapeDtypeStruct((s, PROJ_W), BF16), jax.ShapeDtypeStruct((s, WIDTH), BF16)]
        + [jax.ShapeDtypeStruct((1, WIDTH), F32)] * 2,
        input_output_aliases={0: 0},
        compiler_params=_cparams(("arbitrary",)),
    )(dproj, du, wbr, on, rstd, gn_w, gn_b, proj)


def _ret_bwd(dproj, qr, kr, proj, rall, do_ret, c128, s128, consts, gamma, s):
    cps = RET_CHUNKS if (s // BLK) % RET_CHUNKS == 0 else 1
    nb = s // (cps * BLK)
    decay, zeta, xi = consts

    def body(dproj_in, q_ref, k_ref, v_ref, r_ref, do_ref, c_ref, s_ref, dm_ref, zt_ref, xi_ref, out_ref, gstate):
        del dproj_in
        dq_ref, dk_ref, dv_ref = (out_ref.at[:, pl.ds(off, WIDTH)] for off in (R_Q, R_K, R_V))

        @pl.when(pl.program_id(0) == 0)
        def _():
            gstate[...] = jnp.zeros_like(gstate)

        for sub in reversed(range(cps)):
            rows = slice(sub * BLK, (sub + 1) * BLK)
            c, sg = c_ref[rows, :], s_ref[rows, :]
            for h in range(HEADS):
                sl = slice(h * DH, (h + 1) * DH)
                q, k, v, do = q_ref[rows, sl], k_ref[rows, sl], v_ref[rows, sl], do_ref[rows, sl]
                dm = dm_ref[h]
                g_next = gstate[h]
                gb = g_next.astype(BF16)
                a = (_dot(q, k, "nt") * dm).astype(BF16)
                da = (_dot(do, v, "nt") * dm).astype(BF16)
                dox = (do.astype(F32) * xi_ref[h]).astype(BF16)
                dq = _dot(da, k, "nn") + _dot(dox, r_ref[sub, h].astype(BF16), "nt")
                dk = _dot(da, q, "tn") + _dot(v, gb, "nt") * zt_ref[h]
                kz = (k.astype(F32) * zt_ref[h]).astype(BF16)
                dv = _dot(a, do, "tn") + _dot(kz, gb, "nn")
                gstate[h] = g_next * gamma[h] + _dot(q, dox, "tn")
                dk = dk * RET_SCALE
                dq_ref[rows, sl] = _rope128(dq, c, -sg).astype(BF16)
                dk_ref[rows, sl] = _rope128(dk, c, -sg).astype(BF16)
                dv_ref[rows, sl] = dv.astype(BF16)

    rev = lambda cb: pl.BlockSpec((cps * BLK, WIDTH), lambda i: (nb - 1 - i, cb))
    full = lambda a: pl.BlockSpec(a.shape, lambda i: (0,) * a.ndim)
    tab = pl.BlockSpec((cps * BLK, DH), lambda i: (nb - 1 - i, 0))
    return pl.pallas_call(
        body, name="ret_bwd", grid=(nb,),
        in_specs=[ANY, rev(0), rev(0), rev(R_V // WIDTH),
                  pl.BlockSpec((cps, HEADS, DH, DH), lambda i: (nb - 1 - i, 0, 0, 0)),
                  rev(0), tab, tab, full(decay), full(zeta), full(xi)],
        out_specs=pl.BlockSpec((cps * BLK, 3 * WIDTH), lambda i: (nb - 1 - i, R_Q // (3 * WIDTH))),
        out_shape=jax.ShapeDtypeStruct((s, PROJ_W), BF16),
        scratch_shapes=[pltpu.VMEM((HEADS, DH, DH), F32)],
        input_output_aliases={0: 0},
        compiler_params=_cparams(("arbitrary",)),
    )(dproj, qr, kr, proj, rall, do_ret, c128, s128, decay, zeta, xi)


def _attn_bwd(qq, kk, vv, do, lse, delta, s):
    t = _row_tile(s, 384)
    n = s // t

    def body(q_ref, k_ref, v_ref, do_ref, lse_ref, delta_ref, dq_ref, dk_ref, dv_ref, dk_acc, dv_acc):
        ki = pl.program_id(1)

        @pl.when(ki == 0)
        def _():
            dq_ref[...] = jnp.zeros(dq_ref.shape, F32)

        k, v = k_ref[...], v_ref[...]
        dk_acc[...] = jnp.zeros(dk_acc.shape, F32)
        dv_acc[...] = jnp.zeros(dv_acc.shape, F32)

        def rows(qi):
            return pl.ds(pl.multiple_of(qi * t, t), t)

        def products(qi):
            return _dot(k, q_ref[rows(qi), :], "nt"), _dot(v, do_ref[rows(qi), :], "nt")

        def tile(qi, st, dpt):
            q, dov = q_ref[rows(qi), :], do_ref[rows(qi), :]
            pt = jnp.exp(st - lse_ref[0, qi])
            dv_acc[...] += _dot(pt.astype(BF16), dov, "nn")
            dst = (pt * (dpt - delta_ref[0, qi])).astype(BF16)
            dk_acc[...] += _dot(dst, q, "nn")
            dq_ref[rows(qi), :] += _dot(dst, k, "tn")

        def inner(qi, carry):
            after = products(jnp.minimum(qi + 1, n - 1))
            tile(qi, *carry)
            return after

        st, dpt = products(ki)
        ahead = products(jnp.minimum(ki + 1, n - 1))
        tile(ki, jnp.where(_diag_mask(t), st, NEG_INF), dpt)
        lax.fori_loop(ki + 1, n, inner, ahead)

        dk_ref[...] = dk_acc[...].astype(BF16)
        dv_ref[...] = dv_acc[...].astype(BF16)

    head = lambda w: pl.BlockSpec((s, w), lambda h, i: (0, h))
    stat = pl.BlockSpec((1, n, 1, t), lambda h, i: (h, 0, 0, 0))
    return pl.pallas_call(
        body, name="attn_bwd", grid=(HEADS, n),
        in_specs=[head(HD), pl.BlockSpec((t, HD), lambda h, i: (i, h)), pl.BlockSpec((t, DH), lambda h, i: (i, 2 * h)),
                  head(DH), stat, stat],
        out_specs=[head(HD), pl.BlockSpec((t, HD), lambda h, i: (i, h)), pl.BlockSpec((t, DH), lambda h, i: (i, h))],
        out_shape=[jax.ShapeDtypeStruct((s, HEADS * HD), F32), jax.ShapeDtypeStruct((s, HEADS * HD), BF16),
                   jax.ShapeDtypeStruct((s, WIDTH), BF16)],
        scratch_shapes=[pltpu.VMEM((t, HD), F32), pltpu.VMEM((t, DH), F32)],
        compiler_params=_cparams(("parallel", "arbitrary")),
    )(qq, kk, vv, do, lse, delta)


def _rms_bwd(dn, nhat, r, w):
    g = dn * w
    return r * (g - nhat * jnp.mean(g * nhat, axis=-1, keepdims=True)), dn * nhat


def _mla_prep_bwd(dproj, dqq, dkk, dvv, proj, rq, rkv, qn_w, kvn_w, wq, wukv, qtab, ktab, s):
    tm = _row_tile(s, 384)
    tail = PROJ_W - C_Q

    def body(dproj_in, dqq_ref, dkk_ref, dvv_ref, cq_ref, ckv_ref, rq_ref, rkv_ref, qnw_ref, kvnw_ref, wq_ref,
             wukv_ref, qc_ref, qa_ref, qb_ref, kc_ref, ka_ref, kb_ref,
             tail_ref, dq_ref, dkv_ref, dqnw_ref, dkvnw_ref):
        del dproj_in
        dcq_ref = tail_ref.at[:, pl.ds(0, Q_RANK)]
        dckv_ref = tail_ref.at[:, pl.ds(C_KV - C_Q, KV_RANK)]
        dkpe_ref = tail_ref.at[:, pl.ds(K_PE - C_Q, 2 * DH)]

        @pl.when(pl.program_id(0) == 0)
        def _():
            dqnw_ref[...] = jnp.zeros_like(dqnw_ref)
            dkvnw_ref[...] = jnp.zeros_like(dkvnw_ref)

        qc, qa, qb = qc_ref[...], qa_ref[...], qb_ref[...]
        dkp = jnp.zeros((tm, DH), F32)
        for h in range(HEADS):
            lo, mid, hi = h * HD, h * HD + DH, (h + 1) * HD
            dq_ref[:, lo:hi] = _rope64(dqq_ref[:, lo:hi], qc, -qa, -qb).astype(BF16)
            dkv_ref[:, lo:mid] = dkk_ref[:, lo:mid]
            dkv_ref[:, mid:hi] = dvv_ref[:, h * DH:(h + 1) * DH]
            dkp = dkp + dkk_ref[:, mid:hi].astype(F32)
        dcqn = _dot(dq_ref[...], wq_ref[...], "nn")
        rq_v = rq_ref[...]
        dcq, prod = _rms_bwd(dcqn, cq_ref[...].astype(F32) * rq_v, rq_v, qnw_ref[...])
        dcq_ref[...] = dcq.astype(BF16)
        dqnw_ref[...] += jnp.sum(prod, axis=0, keepdims=True)
        dckvn = _dot(dkv_ref[...], wukv_ref[...], "nn")
        rkv_v = rkv_ref[...]
        dckv, prod = _rms_bwd(dckvn, ckv_ref[...].astype(F32) * rkv_v, rkv_v, kvnw_ref[...])
        dckv_ref[...] = dckv.astype(BF16)
        dkvnw_ref[...] += jnp.sum(prod, axis=0, keepdims=True)
        dkpe_ref[:, :DH] = _rope64(dkp, kc_ref[...], -ka_ref[...], -kb_ref[...]).astype(BF16)
        dkpe_ref[:, DH:] = jnp.zeros((tm, DH), BF16)

    row = lambda w, cb: pl.BlockSpec((tm, w), lambda i: (i, cb))
    full = lambda a: pl.BlockSpec(a.shape, lambda i: (0, 0))
    wide = jax.ShapeDtypeStruct((s, HEADS * HD), BF16)
    return pl.pallas_call(
        body, name="mla_prep_bwd", grid=(s // tm,),
        in_specs=[ANY, row(HEADS * HD, 0), row(HEADS * HD, 0), row(WIDTH, 0),
                  row(Q_RANK, C_Q // Q_RANK), row(KV_RANK, C_KV // KV_RANK), row(1, 0), row(1, 0),
                  full(qn_w), full(kvn_w), full(wq), full(wukv), row(HD, 0), row(HD, 0), row(HD, 0),
                  row(DH, 0), row(DH, 0), row(DH, 0)],
        out_specs=[row(tail, C_Q // tail), row(HEADS * HD, 0), row(HEADS * HD, 0),
                   pl.BlockSpec((1, Q_RANK), lambda i: (0, 0)), pl.BlockSpec((1, KV_RANK), lambda i: (0, 0))],
        out_shape=[jax.ShapeDtypeStruct((s, PROJ_W), BF16), wide, wide,
                   jax.ShapeDtypeStruct((1, Q_RANK), F32), jax.ShapeDtypeStruct((1, KV_RANK), F32)],
        input_output_aliases={0: 0},
        compiler_params=_cparams(("arbitrary",)),
    )(dproj, dqq, dkk, dvv, proj, proj, rq, rkv, qn_w, kvn_w, wq, wukv, *qtab, *ktab)


def _rms_in_bwd(dxn, h, r, dh2, norm_w, s):
    def body(dxn_ref, h_ref, r_ref, dh2_ref, w_ref, gx_ref, gm_ref, dw_ref):
        i = pl.program_id(0)
        r_v = r_ref[...]
        dx, prod = _rms_bwd(dxn_ref[...], h_ref[...] * r_v, r_v, w_ref[...])
        dh = dh2_ref[...] + dx

        @pl.when(i == 0)
        def _():
            gm_ref[...] = dh
            dw_ref[...] = jnp.sum(prod, axis=0, keepdims=True)

        @pl.when(i > 0)
        def _():
            gx_ref[...] = dh
            dw_ref[...] += jnp.sum(prod, axis=0, keepdims=True)

    blk = pl.BlockSpec((BLK, D), lambda i: (i, 0))
    return pl.pallas_call(
        body, name="rms_in_bwd", grid=(s // BLK,),
        in_specs=[blk, blk, pl.BlockSpec((BLK, 1), lambda i: (i, 0)), blk, pl.BlockSpec((1, D), lambda i: (0, 0))],
        out_specs=[pl.BlockSpec((BLK, D), lambda i: (jnp.maximum(i - 1, 0), 0)), pl.BlockSpec((BLK, D), lambda i: (0, 0)),
                   pl.BlockSpec((1, D), lambda i: (0, 0))],
        out_shape=[jax.ShapeDtypeStruct((s - BLK, D), F32), jax.ShapeDtypeStruct((BLK, D), F32),
                   jax.ShapeDtypeStruct((1, D), F32)],
        compiler_params=_cparams(("arbitrary",)),
    )(dxn, h, r, dh2, norm_w)


def _adam_math(w, g, m, v):
    m = ADAM_B1 * m + (1.0 - ADAM_B1) * g
    v = ADAM_B2 * v + (1.0 - ADAM_B2) * (g * g)
    m_hat = m / (1.0 - ADAM_B1 ** ADAM_STEP)
    v_hat = v / (1.0 - ADAM_B2 ** ADAM_STEP)
    return -ADAM_LR * (m_hat / (jnp.sqrt(v_hat) + ADAM_EPS) + ADAM_WD * w), m, v


def _adamw(w, g, m, v, name):
    rows, cols = w.shape
    tr = rows
    for cand in (128, 64, 32, 16, 8):
        if rows % cand == 0:
            tr = cand
            break

    def body(w_ref, g_ref, m_ref, v_ref, d_ref, nm_ref, nv_ref):
        d_ref[...], nm_ref[...], nv_ref[...] = _adam_math(w_ref[...], g_ref[...], m_ref[...], v_ref[...])

    spec = pl.BlockSpec((tr, cols), lambda i: (i, 0))
    return pl.pallas_call(
        body, name=name, grid=(rows // tr,), in_specs=[spec] * 4, out_specs=[spec] * 3,
        out_shape=[jax.ShapeDtypeStruct((rows, cols), F32)] * 3,
        compiler_params=_cparams(("parallel",)),
    )(w, g, m, v)


def _adamw_t(w_t, g_t, m_t, v_t, name):
    cols, rows = w_t.shape

    def body(w_ref, g_ref, m_ref, v_ref, go_ref, d_ref, nm_ref, nv_ref):
        g = g_ref[...]
        go_ref[...] = g
        d_ref[...], nm_ref[...], nv_ref[...] = _adam_math(w_ref[...], g, m_ref[...], v_ref[...])

    blk = pl.BlockSpec((128, rows), lambda k: (k, 0))
    return pl.pallas_call(
        body, name=name, grid=(pl.cdiv(cols, 128),), in_specs=[blk] * 4, out_specs=[blk] * 4,
        out_shape=[jax.ShapeDtypeStruct((cols, rows), F32)] * 4, compiler_params=_cparams(("parallel",)),
    )(w_t, g_t, m_t, v_t)


def _adamw_small(w, gall, m, v):
    def body(w_ref, g_ref, m_ref, v_ref, gs_ref, d_ref, nm_ref, nv_ref):
        g = g_ref[0]
        for dev in range(1, 8):
            g = g + g_ref[dev]
        gs_ref[...] = g
        d_ref[...], nm_ref[...], nv_ref[...] = _adam_math(w_ref[...], g, m_ref[...], v_ref[...])

    return pl.pallas_call(
        body, name="adamw_small", out_shape=[jax.ShapeDtypeStruct((SM_ROWS, 128), F32)] * 4,
        compiler_params=pltpu.CompilerParams(vmem_limit_bytes=VMEM_LIMIT),
    )(w, gall, m, v)


ADD_ROWS = 464
HALF_BLOCKS = PK_HALF // ADD_ROWS


def _rs_core_add(srcs, land, pos):
    def body(pos_ref, *refs):
        src_refs, (land_ref, out_ref, own, sems) = refs[:len(srcs)], refs[len(srcs):]
        step = pl.program_id(0)

        def fetch(s):
            for hf in range(2):
                @pl.when(pos_ref[1] == hf)
                def _():
                    for k, r0, p, n in _grad_half(s, hf):
                        pltpu.make_async_copy(src_refs[k].at[pl.ds(r0, n), :], own.at[s % 2, pl.ds(p, n), :],
                                              sems.at[s % 2]).start()

        @pl.when(step == 0)
        def _():
            fetch(0)

        for s in range(N_CHIPS):
            @pl.when(step == s)
            def _():
                if s + 1 < N_CHIPS:
                    fetch(s + 1)
                slot = own.at[s % 2]
                pltpu.make_async_copy(slot, slot, sems.at[s % 2]).wait()
                out_ref[0] = (slot[...].astype(F32) + land_ref[0].astype(F32)).astype(BF16)

    blk = pl.BlockSpec((1, PK_HALF, D), lambda s, pos_ref: (s, 0, 0))
    return pl.pallas_call(
        body, name="rs_core_add",
        grid_spec=pltpu.PrefetchScalarGridSpec(
            num_scalar_prefetch=1, grid=(N_CHIPS,), in_specs=[ANY] * len(srcs) + [blk], out_specs=blk,
            scratch_shapes=[pltpu.VMEM((2, PK_HALF, D), BF16), pltpu.SemaphoreType.DMA((2,))]),
        out_shape=jax.ShapeDtypeStruct((N_CHIPS, PK_HALF, D), BF16),
        compiler_params=_cparams(("arbitrary",)),
    )(pos, *srcs, land)


def _rs_chip_add(cp, land, pos):
    def body(pos_ref, a_ref, l_ref, o_ref):
        o_ref[0] = ((a_ref[0].astype(F32) + l_ref[0].astype(F32)) + l_ref[1].astype(F32)) + l_ref[2].astype(F32)

    return pl.pallas_call(
        body, name="rs_chip_add",
        grid_spec=pltpu.PrefetchScalarGridSpec(
            num_scalar_prefetch=1, grid=(HALF_BLOCKS,),
            in_specs=[pl.BlockSpec((1, ADD_ROWS, D), lambda i, pos_ref: (pos_ref[0], i, 0)),
                      pl.BlockSpec((3, ADD_ROWS, D), lambda i, pos_ref: (0, i, 0))],
            out_specs=pl.BlockSpec((1, ADD_ROWS, D), lambda i, pos_ref: (pos_ref[1], i, 0))),
        out_shape=jax.ShapeDtypeStruct((2, PK_HALF, D), F32),
        compiler_params=_cparams(("parallel",)),
    )(pos, cp, land)


ANY = pl.BlockSpec(memory_space=pl.ANY)


def _mesh_pos():
    return lax.axis_index("x"), lax.axis_index("y"), lax.axis_index("c")


def _chip_peer(x, y, c, r):
    return (jnp.bitwise_xor(x, r >> 1), jnp.bitwise_xor(y, r & 1), c)


def _allgather_w_in(big0, wt, meta_loc, x2, norm_w):
    s_rows = x2.shape[0] + BLK
    tok = next(t for t in (512, 256, 128) if x2.shape[0] % t == 0)

    def body(big0_ref, wt_ref, meta_ref, x_ref, nw_ref, big_ref, metaf_ref, h_ref, xn_ref, r_ref,
             lsem, ssem, rsem, fssem, frsem, msem_s, msem_r, osem, xbuf, xnbuf, rbuf, mbuf, psem):
        del big0_ref
        x, y, c = _mesh_pos()
        j = 2 * x + y
        me, sibling = (x, y, c), (x, y, 1 - c)

        def half_wait(s_sem, r_sem, halves=1):
            rows = big_ref.at[pl.ds(0, halves * AG_IN_HALF), :]
            return pltpu.make_async_remote_copy(src_ref=rows, dst_ref=rows, send_sem=s_sem, recv_sem=r_sem,
                                                device_id=me, device_id_type=MESH)

        own_meta = pltpu.make_async_copy(meta_ref, metaf_ref.at[j], lsem)
        own_meta.start()
        for s in range(N_CHIPS):
            for hf in range(2):
                @pl.when((j == s) & (c == hf))
                def _():
                    for r in (1, 2, 3):
                        for p, b, n in _ag_half(s, True, hf):
                            pltpu.make_async_remote_copy(
                                src_ref=wt_ref.at[pl.ds(p, n), :], dst_ref=big_ref.at[pl.ds(b, n), :],
                                send_sem=ssem.at[r - 1], recv_sem=rsem.at[r - 1],
                                device_id=_chip_peer(x, y, c, r), device_id_type=MESH).start()

            @pl.when(j == s)
            def _():
                for hf in range(2):
                    for p, b, n in _ag_half(s, True, hf):
                        pltpu.make_async_remote_copy(
                            src_ref=wt_ref.at[pl.ds(p, n), :], dst_ref=big_ref.at[pl.ds(b, n), :],
                            send_sem=osem.at[0], recv_sem=osem.at[1], device_id=sibling, device_id_type=MESH).start()

        meta_copies = [pltpu.make_async_remote_copy(
            src_ref=meta_ref, dst_ref=metaf_ref.at[j], send_sem=msem_s.at[r - 1], recv_sem=msem_r.at[r - 1],
            device_id=_chip_peer(x, y, c, r), device_id_type=MESH) for r in (1, 2, 3)]
        for cp in meta_copies:
            cp.start()

        nw = nw_ref[...]

        n_tok = (s_rows - BLK) // tok

        def x_in(i, slot):
            return pltpu.make_async_copy(x_ref.at[pl.ds(pl.multiple_of(i * tok, tok), tok), :], xbuf.at[slot],
                                         psem.at[slot])

        def norm_rows(first, n, slot):
            hv = xbuf[slot, pl.ds(0, n), :]
            r = lax.rsqrt(jnp.mean(hv * hv, axis=-1, keepdims=True) + NORM_EPS)
            xnbuf[pl.ds(0, n), :] = (hv * r * nw).astype(BF16)
            rbuf[pl.ds(0, n), :] = r
            rows = pl.ds(first if isinstance(first, int) else pl.multiple_of(first, BLK), n)
            outs = [pltpu.make_async_copy(xbuf.at[slot, pl.ds(0, n), :], h_ref.at[rows, :], psem.at[2]),
                    pltpu.make_async_copy(xnbuf.at[pl.ds(0, n), :], xn_ref.at[rows, :], psem.at[3]),
                    pltpu.make_async_copy(rbuf.at[pl.ds(0, n), :], r_ref.at[rows, :], psem.at[4])]
            for cp in outs:
                cp.start(priority=1)
            for cp in outs:
                cp.wait()

        x_in(0, 0).start(priority=1)

        def token_rows(i, carry):
            slot = i % 2
            x_in(i, slot).wait()

            @pl.when(i + 1 < n_tok)
            def _():
                x_in(i + 1, 1 - slot).start(priority=1)

            norm_rows(BLK + i * tok, tok, slot)
            return carry

        lax.fori_loop(0, n_tok, token_rows, 0)
        for cp in meta_copies:
            cp.wait_recv()
        own_meta.wait()
        xbuf[0, pl.ds(0, BLK), :] = jnp.zeros((BLK, D), F32)
        for sh in range(N_CHIPS):
            pltpu.sync_copy(metaf_ref.at[sh], mbuf)
            xbuf[0, PAD_FRONT:BLK, sh * (D // N_CHIPS):(sh + 1) * (D // N_CHIPS)] = mbuf[...]
        norm_rows(0, BLK, 0)

        for r in (1, 2, 3):
            half_wait(ssem.at[r - 1], rsem.at[r - 1]).wait_recv()
            src_shard = jnp.bitwise_xor(j, r)
            for s in range(N_CHIPS):
                for hf in range(2):
                    @pl.when((src_shard == s) & (c == hf))
                    def _():
                        for _, b, n in _ag_half(s, True, hf):
                            rows = big_ref.at[pl.ds(b, n), :]
                            pltpu.make_async_remote_copy(
                                src_ref=rows, dst_ref=rows, send_sem=fssem.at[r - 1], recv_sem=frsem.at[r - 1],
                                device_id=sibling, device_id_type=MESH).start()
        for r in (1, 2, 3):
            half_wait(fssem.at[r - 1], frsem.at[r - 1]).wait_recv()
        for r in (1, 2, 3):
            half_wait(ssem.at[r - 1], rsem.at[r - 1]).wait_send()
            half_wait(fssem.at[r - 1], frsem.at[r - 1]).wait_send()
        own_rows = half_wait(osem.at[0], osem.at[1], halves=2)
        own_rows.wait_recv()
        own_rows.wait_send()
        for cp in meta_copies:
            cp.wait_send()

    dma3 = pltpu.SemaphoreType.DMA((3,))
    big, _, h, xn, r = pl.pallas_call(
        body, name="allgather_w_in", in_specs=[ANY, ANY, ANY, ANY, pl.BlockSpec(memory_space=pltpu.VMEM)],
        out_specs=[ANY] * 5,
        out_shape=[jax.ShapeDtypeStruct((BIG_ROWS, D), BF16), jax.ShapeDtypeStruct((N_CHIPS,) + meta_loc.shape, F32),
                   jax.ShapeDtypeStruct((s_rows, D), F32), jax.ShapeDtypeStruct((s_rows, D), BF16),
                   jax.ShapeDtypeStruct((s_rows, 1), F32)],
        scratch_shapes=[pltpu.SemaphoreType.DMA(()), dma3, dma3, dma3, dma3, dma3, dma3, pltpu.SemaphoreType.DMA((2,)),
                        pltpu.VMEM((2, tok, D), F32), pltpu.VMEM((tok, D), BF16), pltpu.VMEM((tok, 1), F32),
                        pltpu.VMEM(meta_loc.shape, F32), pltpu.SemaphoreType.DMA((5,))],
        input_output_aliases={0: 0},
        compiler_params=pltpu.CompilerParams(has_side_effects=True),
    )(big0, wt, meta_loc, x2, norm_w)
    return big, h, xn, r


HBM = pl.BlockSpec(memory_space=pltpu.HBM)
SEM = pl.BlockSpec(memory_space=pltpu.SEMAPHORE)
EFFECT = pltpu.SideEffectType.DATAFLOW_SIDE_EFFECTING
REST_PEERS = [(r, dc) for r in (1, 2, 3) for dc in (0, 1)]


def _allgather_rest_start(big):
    def body(big_ref, *rest):
        s_sems, r_sems = rest[:6], rest[6:12]
        x, y, c = _mesh_pos()
        j = 2 * x + y
        for s in range(N_CHIPS):
            for hf in range(2):
                @pl.when((j == s) & (c == hf))
                def _():
                    for k, (r, dc) in enumerate(REST_PEERS):
                        for _, b, n in _ag_half(s, False, hf):
                            rows = big_ref.at[pl.ds(b, n), :]
                            pltpu.make_async_remote_copy(
                                src_ref=rows, dst_ref=rows, send_sem=s_sems[k], recv_sem=r_sems[2 * (r - 1) + hf],
                                device_id=_chip_peer(x, y, dc, r), device_id_type=MESH).start()

    sem = pltpu.SemaphoreType.DMA(())
    out = pl.pallas_call(
        body, name="allgather_rest_start", out_shape=(sem,) * 12 + (pltpu.HBM(big.shape, big.dtype),),
        in_specs=(HBM,), out_specs=(SEM,) * 12 + (HBM,), input_output_aliases={0: 12},
        compiler_params=pltpu.CompilerParams(has_side_effects=EFFECT),
    )(pltpu.with_memory_space_constraint(big, pltpu.HBM))
    return out[:12], out[12]


def _allgather_rest_wait(sems, big, after):
    def body(big_ref, *rest):
        s_sems, r_sems = rest[:6], rest[6:12]
        me = _mesh_pos()
        rows = big_ref.at[pl.ds(0, AG_REST_HALF), :]
        for k in range(6):
            copy = pltpu.make_async_remote_copy(src_ref=rows, dst_ref=rows, send_sem=s_sems[k], recv_sem=r_sems[k],
                                                device_id=me, device_id_type=MESH)
            copy.wait_send()
            copy.wait_recv()

    return pl.pallas_call(
        body, name="allgather_rest_wait", out_shape=pltpu.HBM(big.shape, big.dtype),
        in_specs=(HBM,) + (SEM,) * 12 + (pl.BlockSpec(memory_space=pl.ANY),), out_specs=HBM,
        input_output_aliases={0: 0}, compiler_params=pltpu.CompilerParams(has_side_effects=EFFECT),
    )(big, *sems, after)


RS_SPLIT = 2


def _rs_core_exchange(srcs):
    def body(*refs):
        src_refs, (land_ref, ssem, rsem) = refs[:len(srcs)], refs[len(srcs):]
        x, y, c = _mesh_pos()
        me, sibling = (x, y, c), (x, y, 1 - c)
        for hf in range(2):
            @pl.when(c == hf)
            def _():
                for s in range(N_CHIPS):
                    for k, r0, p, n in _grad_half(s, 1 - hf):
                        pltpu.make_async_remote_copy(
                            src_ref=src_refs[k].at[pl.ds(r0, n), :], dst_ref=land_ref.at[s, pl.ds(p, n), :],
                            send_sem=ssem, recv_sem=rsem, device_id=sibling, device_id_type=MESH).start()
        whole = pltpu.make_async_remote_copy(src_ref=land_ref, dst_ref=land_ref, send_sem=ssem, recv_sem=rsem,
                                             device_id=me, device_id_type=MESH)
        whole.wait_recv()
        whole.wait_send()

    return pl.pallas_call(
        body, name="rs_core_exchange", in_specs=[ANY] * len(srcs), out_specs=ANY,
        out_shape=jax.ShapeDtypeStruct((N_CHIPS, PK_HALF, D), BF16),
        scratch_shapes=[pltpu.SemaphoreType.DMA(())] * 2,
        compiler_params=pltpu.CompilerParams(has_side_effects=True),
    )(*srcs)


def _rs_chip_start(cp):
    def body(cp_ref, land_ref, s1, s2, s3, r1, r2, r3, cp_thru, land_thru, token):
        x, y, c = _mesh_pos()
        j = 2 * x + y
        for r, s_sem, r_sem in zip((1, 2, 3), (s1, s2, s3), (r1, r2, r3)):
            pltpu.make_async_remote_copy(
                src_ref=cp_ref.at[jnp.bitwise_xor(j, r)], dst_ref=land_ref.at[r - 1], send_sem=s_sem, recv_sem=r_sem,
                device_id=_chip_peer(x, y, c, r), device_id_type=MESH).start()
        token[...] = jnp.zeros_like(token)

    land_shape = (3, PK_HALF, D)
    sem = pltpu.SemaphoreType.DMA(())
    out = pl.pallas_call(
        body, name="rs_chip_start",
        out_shape=(sem,) * 6 + (pltpu.HBM(cp.shape, cp.dtype), pltpu.HBM(land_shape, BF16),
                                jax.ShapeDtypeStruct((8, 128), F32)),
        in_specs=(HBM, HBM), out_specs=(SEM,) * 6 + (HBM, HBM, pl.BlockSpec(memory_space=pltpu.VMEM)),
        input_output_aliases={0: 6, 1: 7},
        compiler_params=pltpu.CompilerParams(has_side_effects=EFFECT),
    )(pltpu.with_memory_space_constraint(cp, pltpu.HBM),
      pltpu.with_memory_space_constraint(lax.empty(land_shape, BF16), pltpu.HBM))
    return out[:6], out[6], out[7], out[8]


def _rs_chip_wait(sems, cp_thru, land_thru, after):
    def body(cp_ref, land_ref, s1, s2, s3, r1, r2, r3, after_ref, cp_out, land_out):
        me = _mesh_pos()
        for r, s_sem, r_sem in zip((1, 2, 3), (s1, s2, s3), (r1, r2, r3)):
            copy = pltpu.make_async_remote_copy(src_ref=cp_ref.at[0], dst_ref=land_ref.at[r - 1], send_sem=s_sem,
                                                recv_sem=r_sem, device_id=me, device_id_type=MESH)
            copy.wait_send()
            copy.wait_recv()

    return pl.pallas_call(
        body, name="rs_chip_wait",
        out_shape=(pltpu.HBM(cp_thru.shape, cp_thru.dtype), pltpu.HBM(land_thru.shape, land_thru.dtype)),
        in_specs=(HBM, HBM) + (SEM,) * 6 + (pl.BlockSpec(memory_space=pl.ANY),), out_specs=(HBM, HBM),
        input_output_aliases={0: 0, 1: 1},
        compiler_params=pltpu.CompilerParams(has_side_effects=EFFECT),
    )(cp_thru, land_thru, *sems, after)


def _rs_finish(full, small):
    n = PK_HALF // RS_SPLIT

    def body(full_in_ref, sm_ref, full_ref, all_ref, lsem, ssem, rsem, sm_s, sm_r):
        del full_in_ref
        x, y, c = _mesh_pos()
        me, sibling = (x, y, c), (x, y, 1 - c)
        my_id = 4 * x + 2 * y + c
        own_sm = pltpu.make_async_copy(sm_ref, all_ref.at[my_id], lsem)
        own_sm.start()
        for q in range(RS_SPLIT):
            rows = full_ref.at[c, pl.ds(q * n, n), :]
            pltpu.make_async_remote_copy(src_ref=rows, dst_ref=rows, send_sem=ssem, recv_sem=rsem,
                                         device_id=sibling, device_id_type=MESH).start()
        half = pltpu.make_async_remote_copy(src_ref=full_ref.at[c], dst_ref=full_ref.at[c], send_sem=ssem,
                                            recv_sem=rsem, device_id=me, device_id_type=MESH)
        smalls = [pltpu.make_async_remote_copy(
            src_ref=sm_ref, dst_ref=all_ref.at[my_id], send_sem=sm_s.at[r - 1], recv_sem=sm_r.at[r - 1],
            device_id=(jnp.bitwise_xor(x, r >> 2), jnp.bitwise_xor(y, (r >> 1) & 1), jnp.bitwise_xor(c, r & 1)),
            device_id_type=MESH) for r in range(1, 8)]
        for cpy in smalls:
            cpy.start()
        half.wait_recv()
        half.wait_send()
        for cpy in smalls:
            cpy.wait_recv()
        for cpy in smalls:
            cpy.wait_send()
        own_sm.wait()

    dma7 = pltpu.SemaphoreType.DMA((7,))
    return pl.pallas_call(
        body, name="rs_finish", in_specs=[ANY, ANY], out_specs=[ANY, ANY],
        out_shape=[jax.ShapeDtypeStruct((2, PK_HALF, D), F32), jax.ShapeDtypeStruct((8, SM_ROWS, 128), F32)],
        scratch_shapes=[pltpu.SemaphoreType.DMA(()), pltpu.SemaphoreType.DMA(()), pltpu.SemaphoreType.DMA(()), dma7, dma7],
        input_output_aliases={0: 0},
        compiler_params=pltpu.CompilerParams(has_side_effects=True),
    )(full, small)


def _rope_tables(s):
    pos = jnp.arange(s, dtype=F32) - PAD_FRONT

    def cs(d):
        inv = ROPE_BASE ** (-jnp.arange(0, d, 2, dtype=F32) / d)
        ang = pos[:, None] * inv[None, :]
        return jnp.cos(ang), jnp.sin(ang)

    c, sn = cs(ROPE_D)
    z = jnp.zeros_like(c)
    ktab = (jnp.concatenate([c, c, z, z], axis=1), jnp.concatenate([-sn, z, z, z], axis=1),
            jnp.concatenate([z, sn, z, z], axis=1))
    one = jnp.ones_like(c)
    qtab = tuple(ATT_SCALE * jnp.concatenate(parts, axis=1) for parts in (
        [one, one, one, one, c, c, z, z], [z, z, z, z, -sn, z, z, z], [z, z, z, z, z, sn, z, z]))
    lane = jnp.arange(HD) == DH + ROPE_D
    qbias = lane.astype(F32)[None, :]
    kbias = jnp.where((pos < 0)[:, None] & lane[None, DH:], NEG_INF, 0.0).astype(F32)
    c, sn = cs(DH)
    return qtab, ktab, qbias, kbias, jnp.concatenate([c, c], axis=1), jnp.concatenate([-sn, sn], axis=1)


def _grad_mm(a, b, m, n, s, name):
    return _matmul(a, b, "tn", m, n, s, min(m, 512), min(n, 1024), s, BF16, name, n_outer=True)


def _local_step(x2, target2, hp0, big, small_w, normed=None, gather_wait=None, reduce_start=None, reduce_wait=None):
    norm_w, qn_w, kvn_w, gn_w, gn_b, fin_w = small_w
    s = x2.shape[0] + BLK
    tm = _row_tile(s, 1408)
    qtab, ktab, qbias, kbias, c128, s128 = _rope_tables(s)
    consts, consts_bwd, gamma = _ret_consts()

    h, xn, r1 = _rms_in(x2, hp0, norm_w, s) if normed is None else normed
    proj = _matmul(xn, big, "nt", s, PROJ_W, D, tm, 512, D, BF16, "proj", b_off=(BIG_IN // 512, 0))
    if gather_wait is not None:
        big = gather_wait(big, proj)
    uq = big[BIG_UQ:BIG_UQ + 384].reshape(HEADS, DH + ROPE_D, Q_RANK)
    wq = jnp.pad(uq, ((0, 0), (0, HD - DH - ROPE_D), (0, 0))).reshape(HEADS * HD, Q_RANK)
    wukv = big[BIG_UKV:BIG_UKV + 256].reshape(2 * WIDTH, KV_RANK)
    wbm = big[BIG_BM:BIG_BM + 1024].reshape(D, WIDTH)
    wbr = big[BIG_BR:BIG_BR + 1024].reshape(D, WIDTH)
    cqn, rq, ckvn, rkv, qq, kk, vv = _mla_prep(proj, qn_w, kvn_w, wq, wukv, qtab, ktab, qbias, kbias, s)
    o_mla, y_mla, lse = _attn_fwd(qq, kk, vv, proj, s)
    y_ret, on, rstd, qr, kr, rall = _ret_fwd(proj, gn_w, gn_b, c128, s128, consts, gamma, s)
    u_mla, u_ret, merged = _merge_fwd(y_mla, y_ret, wbm, wbr, proj, s)
    h2 = _out_fwd(merged, big, h, s)
    dh2, dh2b, loss_blk, dfin = _loss_bwd(h2, target2, fin_w, s)

    dproj, du_mla, du_ret = _merge_bwd(dh2b, big, u_mla, u_ret, proj, s)
    g_out = _grad_mm(merged, dh2b, D, D, s, "grad_w_out")
    dproj, do_mla, delta = _branch_mla_bwd(dproj, du_mla, wbm, o_mla, proj, s)
    g_bm = _grad_mm(du_mla, y_mla, D, WIDTH, s, "grad_w_branch_mla")
    dproj, do_ret, dgw, dgb = _branch_ret_bwd(dproj, du_ret, wbr, on, rstd, gn_w, gn_b, proj, s)
    g_br = _grad_mm(du_ret, y_ret, D, WIDTH, s, "grad_w_branch_ret")
    dproj = _ret_bwd(dproj, qr, kr, proj, rall, do_ret, c128, s128, consts_bwd, gamma, s)
    dqq, dkk, dvv = _attn_bwd(qq, kk, vv, do_mla, lse, delta.reshape(lse.shape), s)
    dproj, dq, dkv, dqnw, dkvnw = _mla_prep_bwd(dproj, dqq, dkk, dvv, proj, rq, rkv, qn_w, kvn_w, wq, wukv,
                                                qtab, ktab, s)
    g_q = _grad_mm(dq, cqn, HEADS * HD, Q_RANK, s, "grad_w_uq")
    g_ukv = _grad_mm(dkv, ckvn, 2 * WIDTH, KV_RANK, s, "grad_w_ukv")
    g_in = _grad_mm(dproj, xn, PROJ_W, D, s, "grad_w_in")

    g_uq = g_q.reshape(HEADS, HD, Q_RANK)[:, :DH + ROPE_D]
    grads = (g_in, g_out, g_bm.reshape(1024, D), g_br.reshape(1024, D), g_uq.reshape(384, D), g_ukv.reshape(256, D))
    token, travelling = (None, grads) if reduce_start is None else reduce_start(grads)
    dxn = _matmul(dproj, big, "nn", s, D, PROJ_W, _row_tile(s, 384), 512, PROJ_W, F32, "dxn",
                  b_off=(BIG_IN // PROJ_W, 0), n_outer=True, after=token)
    grad_x, gmeta_blk, dnorm = _rms_in_bwd(dxn, h, r1, dh2, norm_w, s)
    small = _small_rows((dnorm, dqnw, dkvnw, dgw, dgb, dfin), loss_blk, gmeta_blk[PAD_FRONT:])
    return grad_x, travelling if reduce_wait is None else reduce_wait(travelling, dnorm), small


def _small_rows(ws, first=None, last=None):
    def part(a, rows):
        a = a.reshape(-1, 128)
        return a if a.shape[0] == rows else jnp.pad(a, ((0, rows - a.shape[0]), (0, 0)))

    bounds = (SM_NORM, SM_QN, SM_KVN, SM_GNW, SM_GNB, SM_FIN, SM_META)
    first = jnp.zeros((SM_NORM, 128), F32) if first is None else first
    last = jnp.zeros((SM_ROWS - SM_META, 128), F32) if last is None else last
    return jnp.concatenate([part(first, SM_NORM)] + [part(w, bounds[k + 1] - bounds[k]) for k, w in enumerate(ws)]
                           + [part(last, SM_ROWS - SM_META)], axis=0)


def kernel(x, meta, norm_w, w_in, mla_q_norm_w, mla_w_uq, mla_kv_norm_w, mla_w_ukv, ret_gn_w, ret_gn_b, w_branch_mla, w_branch_ret, w_out, final_norm_w, loss_target, m_meta, m_norm_w, m_w_in, m_mla_q_norm_w, m_mla_w_uq, m_mla_kv_norm_w, m_mla_w_ukv, m_ret_gn_w, m_ret_gn_b, m_w_branch_mla, m_w_branch_ret, m_w_out, m_final_norm_w, v_meta, v_norm_w, v_w_in, v_mla_q_norm_w, v_mla_w_uq, v_mla_kv_norm_w, v_mla_w_ukv, v_ret_gn_w, v_ret_gn_b, v_w_branch_mla, v_w_branch_ret, v_w_out, v_final_norm_w):
    j = 2 * lax.axis_index("x") + lax.axis_index("y")
    tr = lambda w: w[0].T.reshape(-1, D).astype(BF16)
    pos = jnp.stack([j, lax.axis_index("c")]).astype(jnp.int32)
    put = lambda buf, rows, off: lax.dynamic_update_slice(buf, rows, (off, 0))
    big0 = lax.empty((BIG_ROWS, D), BF16)
    big0 = put(big0, w_out[0].astype(BF16), BIG_OUT + 512 * j)
    big0 = put(big0, tr(w_branch_mla), BIG_BM + 256 * j)
    big0 = put(big0, tr(w_branch_ret), BIG_BR + 256 * j)
    big0 = put(big0, tr(mla_w_uq), BIG_UQ + 96 * j)
    big0 = put(big0, tr(mla_w_ukv), BIG_UKV + 64 * j)
    big0 = put(big0, jnp.zeros((AG_JUNK_END - AG_JUNK_REST, D), BF16), AG_JUNK_REST)
    big0 = put(big0, jnp.zeros((ZERO_ROWS, D), BF16), BIG_IN + IN_WIDTH)
    wt = jnp.concatenate([tr(w_in), jnp.zeros((2 * AG_IN_HALF - IN_SHARD, D), BF16)], axis=0)
    big, h, xn, r1 = _allgather_w_in(big0, wt, meta, x[0], norm_w)
    gather_sems, big = _allgather_rest_start(big)

    def gather_wait(big_travelling, after):
        return _allgather_rest_wait(gather_sems, big_travelling, after)

    small_w = (norm_w, mla_q_norm_w, mla_kv_norm_w, ret_gn_w, ret_gn_b, final_norm_w.reshape(1, D))

    def reduce_start(grads):
        chip_part = _rs_core_add(grads, _rs_core_exchange(grads), pos)
        sems, part_thru, land_thru, token = _rs_chip_start(chip_part)
        return token, (sems, part_thru, land_thru)

    def reduce_wait(state, after):
        return _rs_chip_wait(*state, after)

    grad_x, (chip_part, land2), small = _local_step(x[0], loss_target[0], None, big, small_w, (h, xn, r1), gather_wait,
                                                    reduce_start, reduce_wait)
    full, small_all = _rs_finish(_rs_chip_add(chip_part, land2, pos), small)
    full = full.reshape(PK_ROWS, D)

    untr = lambda lo, hi, rows: full[lo:hi].reshape(rows, -1).T
    grads = {
        "w_out": full[PK_OUT:PK_BM], "w_branch_mla": untr(PK_BM, PK_BR, 512), "w_branch_ret": untr(PK_BR, PK_UQ, 512),
        "mla_w_uq": untr(PK_UQ, PK_UKV, 384), "mla_w_ukv": untr(PK_UKV, PK_PAD, 512),
    }
    big_w = {"mla_w_uq": (mla_w_uq, m_mla_w_uq, v_mla_w_uq),
             "mla_w_ukv": (mla_w_ukv, m_mla_w_ukv, v_mla_w_ukv),
             "w_branch_mla": (w_branch_mla, m_w_branch_mla, v_w_branch_mla),
             "w_branch_ret": (w_branch_ret, m_w_branch_ret, v_w_branch_ret), "w_out": (w_out, m_w_out, v_w_out)}
    res = {"w_in": tuple(a.T[None] for a in _adamw_t(w_in[0].T, full, m_w_in[0].T, v_w_in[0].T, "adamw_w_in"))}
    for name, (w, m, v) in big_w.items():
        d, nm, nv = _adamw(w[0], grads[name], m[0], v[0], "adamw_" + name)
        res[name] = (grads[name][None], d[None], nm[None], nv[None])

    small_m = (m_norm_w, m_mla_q_norm_w, m_mla_kv_norm_w, m_ret_gn_w, m_ret_gn_b, m_final_norm_w.reshape(1, D))
    small_v = (v_norm_w, v_mla_q_norm_w, v_mla_kv_norm_w, v_ret_gn_w, v_ret_gn_b, v_final_norm_w.reshape(1, D))
    gs, ds, ms, vs = _adamw_small(_small_rows(small_w), small_all, _small_rows(small_m), _small_rows(small_v))
    names = ["norm_w", "mla_q_norm_w", "mla_kv_norm_w", "ret_gn_w", "ret_gn_b", "final_norm_w"]
    bounds = [SM_NORM, SM_QN, SM_KVN, SM_GNW, SM_GNB, SM_FIN]
    for k, name in enumerate(names):
        shape = (D,) if name == "final_norm_w" else (1, -1)
        rows = small_w[k].size // 128
        res[name] = tuple(a[bounds[k]:bounds[k] + rows].reshape(shape) for a in (gs, ds, ms, vs))
    g_meta = lax.dynamic_slice_in_dim(gs[SM_META:SM_META + 256].reshape(N_META, D), j * (D // N_CHIPS), D // N_CHIPS, axis=1)
    res["meta"] = (g_meta,) + tuple(_adamw(meta, g_meta, m_meta, v_meta, "adamw_meta"))

    order = ["meta", "norm_w", "w_in", "mla_q_norm_w", "mla_w_uq", "mla_kv_norm_w", "mla_w_ukv", "ret_gn_w", "ret_gn_b",
             "w_branch_mla", "w_branch_ret", "w_out", "final_norm_w"]
    return (gs[0, 0], grad_x[None]) + tuple(res[n][k] for k in range(4) for n in order)
```

```python
import functools
import math

import numpy as np
import jax
import jax.numpy as jnp
from jax import lax
from jax.experimental import pallas as pl
from jax.experimental.pallas import tpu as pltpu

F32 = jnp.float32
BF16 = jnp.bfloat16
MESH = pl.DeviceIdType.MESH

D = 2048
N_META = 16
BLK = 128
PAD_FRONT = BLK - N_META
HEADS = 8
DH = 128
ROPE_D = 64
Q_RANK = 512
KV_RANK = 256
WIDTH = HEADS * DH
ROPE_BASE = 10000.0
NORM_EPS = 1e-6
GN_EPS = 1e-5
NEG_INF = -1e30
ATT_SCALE = (DH + ROPE_D) ** -0.5
RET_SCALE = DH ** -0.5
IN_WIDTH = 10048
N_CHIPS = 4
IN_SHARD = IN_WIDTH // N_CHIPS
ADAM_LR, ADAM_B1, ADAM_B2, ADAM_EPS, ADAM_WD, ADAM_STEP = 0.001, 0.9, 0.999, 1e-08, 0.01, 10

R_Q, R_K, R_V, Z_RET, Z_MLA, GATE0, GATE1 = 0, 1024, 2048, 3072, 4096, 5120, 5120 + D
C_Q = 5120 + 2 * D
C_KV = C_Q + Q_RANK
K_PE = C_KV + KV_RANK
PROJ_W = 10240
IN_RUNS = ((0, 832, C_Q), (832, 1856, Z_MLA), (1856, 4928, R_Q), (4928, 5952, Z_RET), (5952, IN_WIDTH, GATE0))

BIG_IN, BIG_OUT, BIG_BM, BIG_BR, BIG_UQ, BIG_UKV, BIG_JUNK, BIG_ROWS = 0, 10240, 12288, 13312, 14336, 14720, 14976, 15360
ZERO_ROWS = PROJ_W - IN_WIDTH
PK_IN, PK_OUT, PK_BM, PK_BR, PK_UQ, PK_UKV, PK_PAD, PK_ROWS = 0, 2512, 3024, 3280, 3536, 3632, 3696, 3712
PK_HALF = PK_ROWS // 2
SM_LOSS, SM_NORM, SM_QN, SM_KVN, SM_GNW, SM_GNB, SM_FIN, SM_META, SM_ROWS = 0, 8, 24, 32, 40, 48, 56, 72, 328

VMEM_LIMIT = 56 * 1024 * 1024


def _pieces(shard):
    j = shard
    out = [(PK_OUT, BIG_OUT + 512 * j, 512), (PK_BM, BIG_BM + 256 * j, 256), (PK_BR, BIG_BR + 256 * j, 256),
           (PK_UQ, BIG_UQ + 96 * j, 96), (PK_UKV, BIG_UKV + 64 * j, 64)]
    for lo, hi, new in IN_RUNS:
        a, b = max(lo, IN_SHARD * j), min(hi, IN_SHARD * (j + 1))
        if b > a:
            out.append((PK_IN + a - IN_SHARD * j, BIG_IN + new + a - lo, b - a))
    out += [(PK_PAD, BIG_JUNK + 16 * j, 16)]
    return out


AG_JUNK_REST, AG_JUNK_IN, AG_JUNK_END = BIG_JUNK, BIG_JUNK + 128, BIG_JUNK + 192
AG_REST_HALF, AG_IN_HALF = 608, 1264


def _ag_half(shard, w_in, half):
    pieces = [(b, n) for p, b, n in _pieces(shard) if p < PK_PAD and (p < PK_OUT) == w_in]
    pieces.append((AG_JUNK_IN + 16 * shard, 16) if w_in else (AG_JUNK_REST + 32 * shard, 32))
    size = AG_IN_HALF if w_in else AG_REST_HALF
    out, pos = [], 0
    for b, n in pieces:
        s, e = max(pos, half * size), min(pos + n, (half + 1) * size)
        if e > s:
            out.append((s, b + s - pos, e - s))
        pos += n
    assert pos == 2 * size
    return out


def _grad_half(shard, half):
    j = shard
    pieces = [(0, b - BIG_IN, p, n) for p, b, n in _pieces(j) if p < PK_OUT]
    pieces += [(1, 512 * j, PK_OUT, 512), (2, 256 * j, PK_BM, 256), (3, 256 * j, PK_BR, 256), (4, 96 * j, PK_UQ, 96),
               (5, 64 * j, PK_UKV, 64), (0, IN_WIDTH, PK_PAD, PK_ROWS - PK_PAD)]
    lo, hi = half * PK_HALF, (half + 1) * PK_HALF
    out = []
    for k, r0, p, n in pieces:
        s, e = max(p, lo), min(p + n, hi)
        if e > s:
            out.append((k, r0 + s - p, s - lo, e - s))
    assert sum(n for _, _, _, n in out) == PK_HALF
    return out


def _row_tile(rows, cap):
    best = BLK
    for t in range(BLK, cap + 1, BLK):
        if rows % t == 0:
            best = t
    return best


def _cparams(sem):
    return pltpu.CompilerParams(dimension_semantics=sem, vmem_limit_bytes=VMEM_LIMIT)


def _dot(a, b, form):
    dn = {"nt": (((1,), (1,)), ((), ())), "nn": (((1,), (0,)), ((), ())), "tn": (((0,), (0,)), ((), ()))}[form]
    return lax.dot_general(a, b, dn, preferred_element_type=F32)


def _sigmoid(v):
    return 1.0 / (1.0 + jnp.exp(-v))


def _matmul(a, b, form, m, n, k, tm, tn, tk, out_dtype, name, a_off=(0, 0), b_off=(0, 0), n_outer=False, after=None):
    nk = k // tk
    gi, gj = m // tm, n // tn

    def ij(g0, g1):
        return (g1, g0) if n_outer else (g0, g1)

    if form == "nt":
        a_spec = pl.BlockSpec((tm, tk), lambda g0, g1, kk: (ij(g0, g1)[0] + a_off[0], kk + a_off[1]))
        b_spec = pl.BlockSpec((tn, tk), lambda g0, g1, kk: (ij(g0, g1)[1] + b_off[0], kk + b_off[1]))
    elif form == "nn":
        a_spec = pl.BlockSpec((tm, tk), lambda g0, g1, kk: (ij(g0, g1)[0] + a_off[0], kk + a_off[1]))
        b_spec = pl.BlockSpec((tk, tn), lambda g0, g1, kk: (kk + b_off[0], ij(g0, g1)[1] + b_off[1]))
    else:
        a_spec = pl.BlockSpec((tk, tm), lambda g0, g1, kk: (kk + a_off[0], ij(g0, g1)[0] + a_off[1]))
        b_spec = pl.BlockSpec((tk, tn), lambda g0, g1, kk: (kk + b_off[0], ij(g0, g1)[1] + b_off[1]))
    o_spec = pl.BlockSpec((tm, tn), lambda g0, g1, kk: ij(g0, g1))

    def body(a_ref, b_ref, *rest):
        o_ref, *acc = rest[0 if after is None else 1:]
        p = _dot(a_ref[...], b_ref[...], form)
        if nk == 1:
            o_ref[...] = p.astype(o_ref.dtype)
        else:
            acc_ref, = acc
            kk = pl.program_id(2)

            @pl.when(kk == 0)
            def _():
                acc_ref[...] = p

            @pl.when(kk > 0)
            def _():
                acc_ref[...] += p

            @pl.when(kk == nk - 1)
            def _():
                o_ref[...] = acc_ref[...].astype(o_ref.dtype)

    extra = [] if after is None else [after]
    return pl.pallas_call(
        body, name=name, grid=(gj, gi, nk) if n_outer else (gi, gj, nk),
        in_specs=[a_spec, b_spec] + [pl.BlockSpec(memory_space=pl.ANY)] * len(extra), out_specs=o_spec,
        out_shape=jax.ShapeDtypeStruct((m, n), out_dtype),
        scratch_shapes=[] if nk == 1 else [pltpu.VMEM((tm, tn), F32)],
        compiler_params=_cparams(("parallel", "parallel", "arbitrary")),
    )(a, b, *extra)


def _rms_in(x, hp0, norm_w, s):
    def body(x_ref, hp0_ref, w_ref, h_ref, xn_ref, r_ref):
        def run(hv):
            r = lax.rsqrt(jnp.mean(hv * hv, axis=-1, keepdims=True) + NORM_EPS)
            h_ref[...] = hv
            xn_ref[...] = (hv * r * w_ref[...]).astype(BF16)
            r_ref[...] = r

        @pl.when(pl.program_id(0) == 0)
        def _():
            run(hp0_ref[...])

        @pl.when(pl.program_id(0) > 0)
        def _():
            run(x_ref[...])

    return pl.pallas_call(
        body, name="rms_in", grid=(s // BLK,),
        in_specs=[pl.BlockSpec((BLK, D), lambda i: (jnp.maximum(i - 1, 0), 0)),
                  pl.BlockSpec((BLK, D), lambda i: (0, 0)), pl.BlockSpec((1, D), lambda i: (0, 0))],
        out_specs=[pl.BlockSpec((BLK, D), lambda i: (i, 0)), pl.BlockSpec((BLK, D), lambda i: (i, 0)),
                   pl.BlockSpec((BLK, 1), lambda i: (i, 0))],
        out_shape=[jax.ShapeDtypeStruct((s, D), F32), jax.ShapeDtypeStruct((s, D), BF16),
                   jax.ShapeDtypeStruct((s, 1), F32)],
        compiler_params=_cparams(("arbitrary",)),
    )(x, hp0, norm_w)


def _rope64(t, c, sa, sb):
    return t * c + pltpu.roll(t, t.shape[1] - ROPE_D // 2, 1) * sa + pltpu.roll(t, ROPE_D // 2, 1) * sb


def _rope128(t, c, sg):
    return t * c + pltpu.roll(t, DH // 2, 1) * sg


HD = 2 * DH


def _mla_prep(proj, qn_w, kvn_w, wq, wukv, qtab, ktab, qbias, kbias, s):
    tm = _row_tile(s, 384)

    def body(cq_ref, ckv_ref, kpe_ref, qnw_ref, kvnw_ref, wq_ref, wukv_ref, qc_ref, qa_ref, qb_ref,
             kc_ref, ka_ref, kb_ref, qbias_ref, kbias_ref, cqn_ref, rq_ref, ckvn_ref, rkv_ref, qq_ref, kk_ref, vv_ref):
        cq = cq_ref[...].astype(F32)
        rq = lax.rsqrt(jnp.mean(cq * cq, axis=-1, keepdims=True) + NORM_EPS)
        cqn = (cq * rq * qnw_ref[...]).astype(BF16)
        ckv = ckv_ref[...].astype(F32)
        rkv = lax.rsqrt(jnp.mean(ckv * ckv, axis=-1, keepdims=True) + NORM_EPS)
        ckvn = (ckv * rkv * kvnw_ref[...]).astype(BF16)
        cqn_ref[...] = cqn
        rq_ref[...] = rq
        ckvn_ref[...] = ckvn
        rkv_ref[...] = rkv
        q = _dot(cqn, wq_ref[...], "nt")
        kv = _dot(ckvn, wukv_ref[...], "nt")
        kp = (_rope64(kpe_ref[...].astype(F32), kc_ref[...], ka_ref[...], kb_ref[...]) + kbias_ref[...]).astype(BF16)
        qc, qa, qb, qbias = qc_ref[...], qa_ref[...], qb_ref[...], qbias_ref[...]
        ones = jnp.ones((tm, DH), BF16)
        for h in range(HEADS):
            lo, mid, hi = h * HD, h * HD + DH, (h + 1) * HD
            qq_ref[:, lo:hi] = (_rope64(q[:, lo:hi], qc, qa, qb) + qbias).astype(BF16)
            kk_ref[:, lo:mid] = kv[:, lo:mid].astype(BF16)
            kk_ref[:, mid:hi] = kp
            vv_ref[:, lo:mid] = kv[:, mid:hi].astype(BF16)
            vv_ref[:, mid:hi] = ones

    row = lambda w, cb: pl.BlockSpec((tm, w), lambda i: (i, cb))
    full = lambda a: pl.BlockSpec(a.shape, lambda i: (0, 0))
    wide = jax.ShapeDtypeStruct((s, HEADS * HD), BF16)
    return pl.pallas_call(
        body, name="mla_prep", grid=(s // tm,),
        in_specs=[row(Q_RANK, C_Q // Q_RANK), row(KV_RANK, C_KV // KV_RANK), row(DH, K_PE // DH),
                  full(qn_w), full(kvn_w), full(wq), full(wukv), row(HD, 0), row(HD, 0), row(HD, 0),
                  row(DH, 0), row(DH, 0), row(DH, 0), full(qbias), row(DH, 0)],
        out_specs=[row(Q_RANK, 0), row(1, 0), row(KV_RANK, 0), row(1, 0), row(HEADS * HD, 0), row(HEADS * HD, 0),
                   row(HEADS * HD, 0)],
        out_shape=[jax.ShapeDtypeStruct((s, Q_RANK), BF16), jax.ShapeDtypeStruct((s, 1), F32),
                   jax.ShapeDtypeStruct((s, KV_RANK), BF16), jax.ShapeDtypeStruct((s, 1), F32), wide, wide, wide],
        compiler_params=_cparams(("parallel",)),
    )(proj, proj, proj, qn_w, kvn_w, wq, wukv, *qtab, *ktab, qbias, kbias)


def _diag_mask(t):
    return lax.broadcasted_iota(jnp.int32, (t, t), 0) <= lax.broadcasted_iota(jnp.int32, (t, t), 1)


def _silu(z):
    return z * _sigmoid(z)


def _attn_fwd(qq, kk, vv, proj, s):
    t = _row_tile(s, 384)
    n = s // t

    def body(q_ref, k_ref, v_ref, z_ref, o_ref, y_ref, lse_ref, acc_ref, m_ref):
        qi = pl.program_id(1)
        q = q_ref[...]
        m_ref[...] = jnp.full(m_ref.shape, NEG_INF, F32)
        acc_ref[...] = jnp.zeros(acc_ref.shape, F32)

        def keys(ki):
            return pl.ds(pl.multiple_of(ki * t, t), t)

        def scores(ki):
            return _dot(k_ref[keys(ki), :], q, "nt")

        def tile(ki, st):
            m_old = m_ref[...]
            m_new = jnp.maximum(m_old, jnp.max(st, axis=0, keepdims=True))
            pt = jnp.exp(st - m_new).astype(BF16)
            acc_ref[...] = jnp.exp(m_old - m_new) * acc_ref[...] + _dot(v_ref[keys(ki), :], pt, "tn")
            m_ref[...] = m_new

        def inner(ki, st):
            st_after = scores(ki + 1)
            tile(ki, st)
            return st_after

        st_diag = lax.fori_loop(0, qi, inner, scores(0))
        tile(qi, jnp.where(_diag_mask(t), st_diag, NEG_INF))

        l = acc_ref[DH:DH + 1, :]
        o = (acc_ref[:DH, :] / l).T
        o_ref[...] = o.astype(BF16)
        y_ref[...] = (o * _silu(z_ref[...].astype(F32))).astype(BF16)
        lse_ref[0, 0] = m_ref[...] + jnp.log(l)

    qtile = pl.BlockSpec((t, DH), lambda h, i: (i, h))
    head = pl.BlockSpec((s, HD), lambda h, i: (0, h))
    return pl.pallas_call(
        body, name="attn_fwd", grid=(HEADS, s // t),
        in_specs=[pl.BlockSpec((t, HD), lambda h, i: (i, h)), head, head,
                  pl.BlockSpec((t, DH), lambda h, i: (i, Z_MLA // DH + h))],
        out_specs=[qtile, qtile, pl.BlockSpec((1, 1, 1, t), lambda h, i: (h, i, 0, 0))],
        out_shape=[jax.ShapeDtypeStruct((s, WIDTH), BF16), jax.ShapeDtypeStruct((s, WIDTH), BF16),
                   jax.ShapeDtypeStruct((HEADS, s // t, 1, t), F32)],
        scratch_shapes=[pltpu.VMEM((HD, t), F32), pltpu.VMEM((1, t), F32)],
        compiler_params=_cparams(("parallel", "arbitrary")),
    )(qq, kk, vv, proj)


RET_CHUNKS = 3


def _ret_consts():
    log_g = np.log1p(-(2.0 ** (-5.0 - np.arange(HEADS, dtype=np.float64))))
    n = np.arange(BLK, dtype=np.float64)
    diff = n[:, None] - n[None, :]
    decay = np.where(diff >= 0, np.exp(log_g[:, None, None] * np.maximum(diff, 0.0)), 0.0)
    zeta = np.exp(log_g[:, None] * (BLK - 1.0 - n))[:, :, None]
    xi = np.exp(log_g[:, None] * (n + 1.0))[:, :, None]
    gamma = [float(np.float32(np.exp(g * BLK))) for g in log_g]
    wide = lambda a: np.broadcast_to(a, (HEADS, BLK, DH))
    return ((jnp.asarray(decay, F32), jnp.asarray(wide(zeta), F32), jnp.asarray(wide(xi), F32)),
            (jnp.asarray(decay, F32), jnp.asarray(zeta, F32), jnp.asarray(xi, F32)), gamma)


def _ret_fwd(proj, gn_w, gn_b, c128, s128, consts, gamma, s):
    nb = s // BLK
    cps = RET_CHUNKS if nb % RET_CHUNKS == 0 else 1
    decay, zeta, xi = consts

    def body(rq_ref, rk_ref, rv_ref, z_ref, gw_ref, gb_ref, c_ref, s_ref, dm_ref, zt_ref, xi_ref,
             y_ref, on_ref, rstd_ref, qr_ref, kr_ref, rall_ref, state):
        @pl.when(pl.program_id(0) == 0)
        def _():
            state[...] = jnp.zeros_like(state)

        for sub in range(cps):
            rows = slice(sub * BLK, (sub + 1) * BLK)
            c, sg = c_ref[rows, :], s_ref[rows, :]
            for h in range(HEADS):
                sl = slice(h * DH, (h + 1) * DH)
                q = _rope128(rq_ref[rows, sl].astype(F32), c, sg).astype(BF16)
                kf = _rope128(rk_ref[rows, sl].astype(F32), c, sg) * RET_SCALE
                k = kf.astype(BF16)
                v = rv_ref[rows, sl]
                qr_ref[rows, sl] = q
                kr_ref[rows, sl] = k
                r_prev = state[h]
                rall_ref[sub, h] = r_prev
                a = _dot(q, k, "nt") * dm_ref[h]
                o = _dot(a.astype(BF16), v, "nn") + _dot(q, r_prev.astype(BF16), "nn") * xi_ref[h]
                state[h] = r_prev * gamma[h] + _dot((kf * zt_ref[h]).astype(BF16), v, "tn")
                mu = jnp.mean(o, axis=-1, keepdims=True)
                var = jnp.mean(jnp.square(o - mu), axis=-1, keepdims=True)
                rstd = lax.rsqrt(var + GN_EPS)
                on = (o - mu) * rstd
                rstd_ref[h, rows, :] = rstd
                on_ref[rows, sl] = on.astype(BF16)
                ogn = on * gw_ref[:, sl] + gb_ref[:, sl]
                y_ref[rows, sl] = (ogn * _silu(z_ref[rows, sl].astype(F32))).astype(BF16)

    seg = lambda cb: pl.BlockSpec((cps * BLK, WIDTH), lambda i: (i, cb))
    full = lambda a: pl.BlockSpec(a.shape, lambda i: (0,) * a.ndim)
    tab = pl.BlockSpec((cps * BLK, DH), lambda i: (i, 0))
    return pl.pallas_call(
        body, name="ret_fwd", grid=(nb // cps,),
        in_specs=[seg(R_Q // WIDTH), seg(R_K // WIDTH), seg(R_V // WIDTH), seg(Z_RET // WIDTH), full(gn_w), full(gn_b),
                  tab, tab, full(decay), full(zeta), full(xi)],
        out_specs=[seg(0), seg(0), pl.BlockSpec((HEADS, cps * BLK, 1), lambda i: (0, i, 0)), seg(0), seg(0),
                   pl.BlockSpec((cps, HEADS, DH, DH), lambda i: (i, 0, 0, 0))],
        out_shape=[jax.ShapeDtypeStruct((s, WIDTH), BF16), jax.ShapeDtypeStruct((s, WIDTH), BF16),
                   jax.ShapeDtypeStruct((HEADS, s, 1), F32), jax.ShapeDtypeStruct((s, WIDTH), BF16),
                   jax.ShapeDtypeStruct((s, WIDTH), BF16), jax.ShapeDtypeStruct((nb, HEADS, DH, DH), F32)],
        scratch_shapes=[pltpu.VMEM((HEADS, DH, DH), F32)],
        compiler_params=_cparams(("arbitrary",)),
    )(proj, proj, proj, proj, gn_w, gn_b, c128, s128, decay, zeta, xi)


def _merge_fwd(y_mla, y_ret, wbm, wbr, proj, s):
    tm, tn = _row_tile(s, 1408), 512

    def body(ym_ref, yr_ref, wm_ref, wr_ref, g0_ref, g1_ref, um_ref, ur_ref, mg_ref):
        um = _dot(ym_ref[...], wm_ref[...], "nt")
        ur = _dot(yr_ref[...], wr_ref[...], "nt")
        um_ref[...] = um.astype(BF16)
        ur_ref[...] = ur.astype(BF16)
        mg_ref[...] = (_sigmoid(g0_ref[...].astype(F32)) * um + _sigmoid(g1_ref[...].astype(F32)) * ur).astype(BF16)

    yspec = pl.BlockSpec((tm, WIDTH), lambda i, j: (i, 0))
    wspec = pl.BlockSpec((tn, WIDTH), lambda i, j: (j, 0))
    ospec = pl.BlockSpec((tm, tn), lambda i, j: (i, j))
    return pl.pallas_call(
        body, name="merge_fwd", grid=(s // tm, D // tn),
        in_specs=[yspec, yspec, wspec, wspec, pl.BlockSpec((tm, tn), lambda i, j: (i, GATE0 // tn + j)),
                  pl.BlockSpec((tm, tn), lambda i, j: (i, GATE1 // tn + j))],
        out_specs=[ospec, ospec, ospec],
        out_shape=[jax.ShapeDtypeStruct((s, D), BF16)] * 3,
        compiler_params=_cparams(("parallel", "parallel")),
    )(y_mla, y_ret, wbm, wbr, proj, proj)


def _out_fwd(merged, big, h, s):
    tm, tn = _row_tile(s, 1408), 1024

    def body(m_ref, w_ref, h_ref, o_ref):
        o_ref[...] = h_ref[...] + _dot(m_ref[...], w_ref[...], "nn")

    return pl.pallas_call(
        body, name="out_fwd", grid=(s // tm, D // tn),
        in_specs=[pl.BlockSpec((tm, D), lambda i, j: (i, 0)), pl.BlockSpec((D, tn), lambda i, j: (BIG_OUT // D, j)),
                  pl.BlockSpec((tm, tn), lambda i, j: (i, j))],
        out_specs=pl.BlockSpec((tm, tn), lambda i, j: (i, j)),
        out_shape=jax.ShapeDtypeStruct((s, D), F32),
        compiler_params=_cparams(("parallel", "parallel")),
    )(merged, big, h)


def _loss_bwd(h2, target, fin_w, s):
    tb = _row_tile(s, 384)
    sub, nb = tb // BLK, s // tb

    def body(h2_ref, *rest):
        t_refs, (w_ref, dh_ref, dhb_ref, loss_ref, dw_ref) = rest[:sub], rest[sub:]
        i = pl.program_id(0)

        @pl.when(i == 0)
        def _():
            loss_ref[...] = jnp.zeros_like(loss_ref)
            dw_ref[...] = jnp.zeros_like(dw_ref)

        w = w_ref[...]

        def chunk(k):
            rows = slice(k * BLK, (k + 1) * BLK)
            hv = h2_ref[rows, :]
            r = lax.rsqrt(jnp.mean(hv * hv, axis=-1, keepdims=True) + NORM_EPS)
            nrm = hv * r
            e = nrm * w - t_refs[k][...]
            loss_ref[...] += jnp.full(loss_ref.shape, 0.5 / D, F32) * jnp.sum(e * e)
            dy = e * (1.0 / D)
            dw_ref[...] += jnp.sum(dy * nrm, axis=0, keepdims=True)
            g = dy * w
            dh = r * (g - nrm * jnp.mean(g * nrm, axis=-1, keepdims=True))
            dh_ref[rows, :] = dh
            dhb_ref[rows, :] = dh.astype(BF16)

        @pl.when(i == 0)
        def _():
            dh_ref[:BLK, :] = jnp.zeros((BLK, D), F32)
            dhb_ref[:BLK, :] = jnp.zeros((BLK, D), BF16)

        @pl.when(i > 0)
        def _():
            chunk(0)

        for k in range(1, sub):
            chunk(k)

    blk = pl.BlockSpec((tb, D), lambda i: (i, 0))
    tgt = [pl.BlockSpec((BLK, D), lambda i, k=k: (jnp.maximum(i * sub + k - 1, 0), 0)) for k in range(sub)]
    return pl.pallas_call(
        body, name="loss_bwd", grid=(nb,),
        in_specs=[blk] + tgt + [pl.BlockSpec((1, D), lambda i: (0, 0))],
        out_specs=[blk, blk, pl.BlockSpec((8, 128), lambda i: (0, 0)), pl.BlockSpec((1, D), lambda i: (0, 0))],
        out_shape=[jax.ShapeDtypeStruct((s, D), F32), jax.ShapeDtypeStruct((s, D), BF16),
                   jax.ShapeDtypeStruct((8, 128), F32), jax.ShapeDtypeStruct((1, D), F32)],
        compiler_params=_cparams(("arbitrary",)),
    )(h2, *[target] * sub, fin_w)


def _merge_bwd(dh2b, big, u_mla, u_ret, proj, s):
    tm, tn = _row_tile(s, 1408), 512

    def body(d_ref, w_ref, um_ref, ur_ref, gate_ref, dproj_ref, dum_ref, dur_ref, dm_ref):
        branch, j = pl.program_id(1), pl.program_id(2)
        cols = pl.ds(pl.multiple_of(j * tn, tn), tn)

        @pl.when(branch == 0)
        def _():
            dm_ref[:, cols] = _dot(d_ref[...], w_ref[...], "nt")

        dm = dm_ref[:, cols]
        gt = _sigmoid(gate_ref[...].astype(F32))

        @pl.when(branch == 0)
        def _():
            dproj_ref[...] = (dm * um_ref[...].astype(F32) * gt * (1.0 - gt)).astype(BF16)
            dum_ref[...] = (dm * gt).astype(BF16)

        @pl.when(branch == 1)
        def _():
            dproj_ref[...] = (dm * ur_ref[...].astype(F32) * gt * (1.0 - gt)).astype(BF16)
            dur_ref[...] = (dm * gt).astype(BF16)

    last = D // tn - 1
    mla = pl.BlockSpec((tm, tn), lambda i, b, j: (i, jnp.where(b == 0, j, last)))
    ret = pl.BlockSpec((tm, tn), lambda i, b, j: (i, jnp.where(b == 0, 0, j)))
    gate = pl.BlockSpec((tm, tn), lambda i, b, j: (i, GATE0 // tn + b * (D // tn) + j))
    return pl.pallas_call(
        body, name="merge_bwd", grid=(s // tm, 2, D // tn),
        in_specs=[pl.BlockSpec((tm, D), lambda i, b, j: (i, 0)),
                  pl.BlockSpec((tn, D), lambda i, b, j: (BIG_OUT // tn + jnp.where(b == 0, j, last), 0)), mla, ret, gate],
        out_specs=[gate, mla, ret],
        out_shape=[jax.ShapeDtypeStruct((s, PROJ_W), BF16), jax.ShapeDtypeStruct((s, D), BF16),
                   jax.ShapeDtypeStruct((s, D), BF16)],
        scratch_shapes=[pltpu.VMEM((tm, D), F32)],
        compiler_params=_cparams(("parallel", "arbitrary", "arbitrary")),
    )(dh2b, big, u_mla, u_ret, proj)


def _dsilu(z):
    sg = _sigmoid(z)
    return sg * (1.0 + z * (1.0 - sg))


def _branch_mla_bwd(dproj, du, wbm, o_mla, proj, s):
    tm = _row_tile(s, 384)

    def body(dproj_in, du_ref, w_ref, o_ref, z_ref, dz_ref, do_ref, delta_ref):
        del dproj_in
        dy = _dot(du_ref[...], w_ref[...], "nn")
        z = z_ref[...].astype(F32)
        o = o_ref[...].astype(F32)
        do = dy * _silu(z)
        do_ref[...] = do.astype(BF16)
        dz_ref[...] = (dy * o * _dsilu(z)).astype(BF16)
        prod = do * o
        for h in range(HEADS):
            delta_ref[h] = jnp.sum(prod[:, h * DH:(h + 1) * DH], axis=-1, keepdims=True)

    row = lambda w, cb: pl.BlockSpec((tm, w), lambda i: (i, cb))
    return pl.pallas_call(
        body, name="branch_mla_bwd", grid=(s // tm,),
        in_specs=[ANY, row(D, 0), pl.BlockSpec((D, WIDTH), lambda i: (0, 0)), row(WIDTH, 0),
                  row(WIDTH, Z_MLA // WIDTH)],
        out_specs=[row(WIDTH, Z_MLA // WIDTH), row(WIDTH, 0), pl.BlockSpec((HEADS, tm, 1), lambda i: (0, i, 0))],
        out_shape=[jax.ShapeDtypeStruct((s, PROJ_W), BF16), jax.ShapeDtypeStruct((s, WIDTH), BF16),
                   jax.ShapeDtypeStruct((HEADS, s, 1), F32)],
        input_output_aliases={0: 0},
        compiler_params=_cparams(("parallel",)),
    )(dproj, du, wbm, o_mla, proj)


def _branch_ret_bwd(dproj, du, wbr, on, rstd, gn_w, gn_b, proj, s):
    tm = _row_tile(s, 384)

    def body(dproj_in, du_ref, w_ref, on_ref, rstd_ref, gw_ref, gb_ref, z_ref, dz_ref, do_ref, dgw_ref, dgb_ref):
        del dproj_in

        @pl.when(pl.program_id(0) == 0)
        def _():
            dgw_ref[...] = jnp.zeros_like(dgw_ref)
            dgb_ref[...] = jnp.zeros_like(dgb_ref)

        dy = _dot(du_ref[...], w_ref[...], "nn")
        z = z_ref[...].astype(F32)
        on = on_ref[...].astype(F32)
        gw = gw_ref[...]
        dogn = dy * _silu(z)
        dz_ref[...] = (dy * (on * gw + gb_ref[...]) * _dsilu(z)).astype(BF16)
        dgw_ref[...] += jnp.sum(dogn * on, axis=0, keepdims=True)
        dgb_ref[...] += jnp.sum(dogn, axis=0, keepdims=True)
        don = dogn * gw
        for h in range(HEADS):
            sl = slice(h * DH, (h + 1) * DH)
            dn, nh = don[:, sl], on[:, sl]
            do = rstd_ref[h] * (dn - jnp.mean(dn, axis=-1, keepdims=True)
                                - nh * jnp.mean(dn * nh, axis=-1, keepdims=True))
            do_ref[:, sl] = do.astype(BF16)

    row = lambda w, cb: pl.BlockSpec((tm, w), lambda i: (i, cb))
    vec = pl.BlockSpec((1, WIDTH), lambda i: (0, 0))
    return pl.pallas_call(
        body, name="branch_ret_bwd", grid=(s // tm,),
        in_specs=[ANY, row(D, 0), pl.BlockSpec((D, WIDTH), lambda i: (0, 0)), row(WIDTH, 0),
                  pl.BlockSpec((HEADS, tm, 1), lambda i: (0, i, 0)), vec, vec, row(WIDTH, Z_RET // WIDTH)],
        out_specs=[row(WIDTH, Z_RET // WIDTH), row(WIDTH, 0), vec, vec],
        out_shape=[jax.ShapeDtypeStruct((s, PROJ_W), BF16), jax.ShapeDtypeStruct((s, WIDTH), BF16)]
        + [jax.ShapeDtypeStruct((1, WIDTH), F32)] * 2,
        input_output_aliases={0: 0},
        compiler_params=_cparams(("arbitrary",)),
    )(dproj, du, wbr, on, rstd, gn_w, gn_b, proj)


def _ret_bwd(dproj, qr, kr, proj, rall, do_ret, c128, s128, consts, gamma, s):
    cps = RET_CHUNKS if (s // BLK) % RET_CHUNKS == 0 else 1
    nb = s // (cps * BLK)
    decay, zeta, xi = consts

    def body(dproj_in, q_ref, k_ref, v_ref, r_ref, do_ref, c_ref, s_ref, dm_ref, zt_ref, xi_ref, out_ref, gstate):
        del dproj_in
        dq_ref, dk_ref, dv_ref = (out_ref.at[:, pl.ds(off, WIDTH)] for off in (R_Q, R_K, R_V))

        @pl.when(pl.program_id(0) == 0)
        def _():
            gstate[...] = jnp.zeros_like(gstate)

        for sub in reversed(range(cps)):
            rows = slice(sub * BLK, (sub + 1) * BLK)
            c, sg = c_ref[rows, :], s_ref[rows, :]
            for h in range(HEADS):
                sl = slice(h * DH, (h + 1) * DH)
                q, k, v, do = q_ref[rows, sl], k_ref[rows, sl], v_ref[rows, sl], do_ref[rows, sl]
                dm = dm_ref[h]
                g_next = gstate[h]
                gb = g_next.astype(BF16)
                a = (_dot(q, k, "nt") * dm).astype(BF16)
                da = (_dot(do, v, "nt") * dm).astype(BF16)
                dox = (do.astype(F32) * xi_ref[h]).astype(BF16)
                dq = _dot(da, k, "nn") + _dot(dox, r_ref[sub, h].astype(BF16), "nt")
                dk = _dot(da, q, "tn") + _dot(v, gb, "nt") * zt_ref[h]
                kz = (k.astype(F32) * zt_ref[h]).astype(BF16)
                dv = _dot(a, do, "tn") + _dot(kz, gb, "nn")
                gstate[h] = g_next * gamma[h] + _dot(q, dox, "tn")
                dk = dk * RET_SCALE
                dq_ref[rows, sl] = _rope128(dq, c, -sg).astype(BF16)
                dk_ref[rows, sl] = _rope128(dk, c, -sg).astype(BF16)
                dv_ref[rows, sl] = dv.astype(BF16)

    rev = lambda cb: pl.BlockSpec((cps * BLK, WIDTH), lambda i: (nb - 1 - i, cb))
    full = lambda a: pl.BlockSpec(a.shape, lambda i: (0,) * a.ndim)
    tab = pl.BlockSpec((cps * BLK, DH), lambda i: (nb - 1 - i, 0))
    return pl.pallas_call(
        body, name="ret_bwd", grid=(nb,),
        in_specs=[ANY, rev(0), rev(0), rev(R_V // WIDTH),
                  pl.BlockSpec((cps, HEADS, DH, DH), lambda i: (nb - 1 - i, 0, 0, 0)),
                  rev(0), tab, tab, full(decay), full(zeta), full(xi)],
        out_specs=pl.BlockSpec((cps * BLK, 3 * WIDTH), lambda i: (nb - 1 - i, R_Q // (3 * WIDTH))),
        out_shape=jax.ShapeDtypeStruct((s, PROJ_W), BF16),
        scratch_shapes=[pltpu.VMEM((HEADS, DH, DH), F32)],
        input_output_aliases={0: 0},
        compiler_params=_cparams(("arbitrary",)),
    )(dproj, qr, kr, proj, rall, do_ret, c128, s128, decay, zeta, xi)


def _attn_bwd(qq, kk, vv, do, lse, delta, s):
    t = _row_tile(s, 384)
    n = s // t

    def body(q_ref, k_ref, v_ref, do_ref, lse_ref, delta_ref, dq_ref, dk_ref, dv_ref, dk_acc, dv_acc):
        ki = pl.program_id(1)

        @pl.when(ki == 0)
        def _():
            dq_ref[...] = jnp.zeros(dq_ref.shape, F32)

        k, v = k_ref[...], v_ref[...]
        dk_acc[...] = jnp.zeros(dk_acc.shape, F32)
        dv_acc[...] = jnp.zeros(dv_acc.shape, F32)

        def rows(qi):
            return pl.ds(pl.multiple_of(qi * t, t), t)

        def products(qi):
            return _dot(k, q_ref[rows(qi), :], "nt"), _dot(v, do_ref[rows(qi), :], "nt")

        def tile(qi, st, dpt):
            q, dov = q_ref[rows(qi), :], do_ref[rows(qi), :]
            pt = jnp.exp(st - lse_ref[0, qi])
            dv_acc[...] += _dot(pt.astype(BF16), dov, "nn")
            dst = (pt * (dpt - delta_ref[0, qi])).astype(BF16)
            dk_acc[...] += _dot(dst, q, "nn")
            dq_ref[rows(qi), :] += _dot(dst, k, "tn")

        def inner(qi, carry):
            after = products(jnp.minimum(qi + 1, n - 1))
            tile(qi, *carry)
            return after

        st, dpt = products(ki)
        ahead = products(jnp.minimum(ki + 1, n - 1))
        tile(ki, jnp.where(_diag_mask(t), st, NEG_INF), dpt)
        lax.fori_loop(ki + 1, n, inner, ahead)

        dk_ref[...] = dk_acc[...].astype(BF16)
        dv_ref[...] = dv_acc[...].astype(BF16)

    head = lambda w: pl.BlockSpec((s, w), lambda h, i: (0, h))
    stat = pl.BlockSpec((1, n, 1, t), lambda h, i: (h, 0, 0, 0))
    return pl.pallas_call(
        body, name="attn_bwd", grid=(HEADS, n),
        in_specs=[head(HD), pl.BlockSpec((t, HD), lambda h, i: (i, h)), pl.BlockSpec((t, DH), lambda h, i: (i, 2 * h)),
                  head(DH), stat, stat],
        out_specs=[head(HD), pl.BlockSpec((t, HD), lambda h, i: (i, h)), pl.BlockSpec((t, DH), lambda h, i: (i, h))],
        out_shape=[jax.ShapeDtypeStruct((s, HEADS * HD), F32), jax.ShapeDtypeStruct((s, HEADS * HD), BF16),
                   jax.ShapeDtypeStruct((s, WIDTH), BF16)],
        scratch_shapes=[pltpu.VMEM((t, HD), F32), pltpu.VMEM((t, DH), F32)],
        compiler_params=_cparams(("parallel", "arbitrary")),
    )(qq, kk, vv, do, lse, delta)


def _rms_bwd(dn, nhat, r, w):
    g = dn * w
    return r * (g - nhat * jnp.mean(g * nhat, axis=-1, keepdims=True)), dn * nhat


def _mla_prep_bwd(dproj, dqq, dkk, dvv, proj, rq, rkv, qn_w, kvn_w, wq, wukv, qtab, ktab, s):
    tm = _row_tile(s, 384)
    tail = PROJ_W - C_Q

    def body(dproj_in, dqq_ref, dkk_ref, dvv_ref, cq_ref, ckv_ref, rq_ref, rkv_ref, qnw_ref, kvnw_ref, wq_ref,
             wukv_ref, qc_ref, qa_ref, qb_ref, kc_ref, ka_ref, kb_ref,
             tail_ref, dq_ref, dkv_ref, dqnw_ref, dkvnw_ref):
        del dproj_in
        dcq_ref = tail_ref.at[:, pl.ds(0, Q_RANK)]
        dckv_ref = tail_ref.at[:, pl.ds(C_KV - C_Q, KV_RANK)]
        dkpe_ref = tail_ref.at[:, pl.ds(K_PE - C_Q, 2 * DH)]

        @pl.when(pl.program_id(0) == 0)
        def _():
            dqnw_ref[...] = jnp.zeros_like(dqnw_ref)
            dkvnw_ref[...] = jnp.zeros_like(dkvnw_ref)

        qc, qa, qb = qc_ref[...], qa_ref[...], qb_ref[...]
        dkp = jnp.zeros((tm, DH), F32)
        for h in range(HEADS):
            lo, mid, hi = h * HD, h * HD + DH, (h + 1) * HD
            dq_ref[:, lo:hi] = _rope64(dqq_ref[:, lo:hi], qc, -qa, -qb).astype(BF16)
            dkv_ref[:, lo:mid] = dkk_ref[:, lo:mid]
            dkv_ref[:, mid:hi] = dvv_ref[:, h * DH:(h + 1) * DH]
            dkp = dkp + dkk_ref[:, mid:hi].astype(F32)
        dcqn = _dot(dq_ref[...], wq_ref[...], "nn")
        rq_v = rq_ref[...]
        dcq, prod = _rms_bwd(dcqn, cq_ref[...].astype(F32) * rq_v, rq_v, qnw_ref[...])
        dcq_ref[...] = dcq.astype(BF16)
        dqnw_ref[...] += jnp.sum(prod, axis=0, keepdims=True)
        dckvn = _dot(dkv_ref[...], wukv_ref[...], "nn")
        rkv_v = rkv_ref[...]
        dckv, prod = _rms_bwd(dckvn, ckv_ref[...].astype(F32) * rkv_v, rkv_v, kvnw_ref[...])
        dckv_ref[...] = dckv.astype(BF16)
        dkvnw_ref[...] += jnp.sum(prod, axis=0, keepdims=True)
        dkpe_ref[:, :DH] = _rope64(dkp, kc_ref[...], -ka_ref[...], -kb_ref[...]).astype(BF16)
        dkpe_ref[:, DH:] = jnp.zeros((tm, DH), BF16)

    row = lambda w, cb: pl.BlockSpec((tm, w), lambda i: (i, cb))
    full = lambda a: pl.BlockSpec(a.shape, lambda i: (0, 0))
    wide = jax.ShapeDtypeStruct((s, HEADS * HD), BF16)
    return pl.pallas_call(
        body, name="mla_prep_bwd", grid=(s // tm,),
        in_specs=[ANY, row(HEADS * HD, 0), row(HEADS * HD, 0), row(WIDTH, 0),
                  row(Q_RANK, C_Q // Q_RANK), row(KV_RANK, C_KV // KV_RANK), row(1, 0), row(1, 0),
                  full(qn_w), full(kvn_w), full(wq), full(wukv), row(HD, 0), row(HD, 0), row(HD, 0),
                  row(DH, 0), row(DH, 0), row(DH, 0)],
        out_specs=[row(tail, C_Q // tail), row(HEADS * HD, 0), row(HEADS * HD, 0),
                   pl.BlockSpec((1, Q_RANK), lambda i: (0, 0)), pl.BlockSpec((1, KV_RANK), lambda i: (0, 0))],
        out_shape=[jax.ShapeDtypeStruct((s, PROJ_W), BF16), wide, wide,
                   jax.ShapeDtypeStruct((1, Q_RANK), F32), jax.ShapeDtypeStruct((1, KV_RANK), F32)],
        input_output_aliases={0: 0},
        compiler_params=_cparams(("arbitrary",)),
    )(dproj, dqq, dkk, dvv, proj, proj, rq, rkv, qn_w, kvn_w, wq, wukv, *qtab, *ktab)


def _rms_in_bwd(dxn, h, r, dh2, norm_w, s):
    def body(dxn_ref, h_ref, r_ref, dh2_ref, w_ref, gx_ref, gm_ref, dw_ref):
        i = pl.program_id(0)
        r_v = r_ref[...]
        dx, prod = _rms_bwd(dxn_ref[...], h_ref[...] * r_v, r_v, w_ref[...])
        dh = dh2_ref[...] + dx

        @pl.when(i == 0)
        def _():
            gm_ref[...] = dh
            dw_ref[...] = jnp.sum(prod, axis=0, keepdims=True)

        @pl.when(i > 0)
        def _():
            gx_ref[...] = dh
            dw_ref[...] += jnp.sum(prod, axis=0, keepdims=True)

    blk = pl.BlockSpec((BLK, D), lambda i: (i, 0))
    return pl.pallas_call(
        body, name="rms_in_bwd", grid=(s // BLK,),
        in_specs=[blk, blk, pl.BlockSpec((BLK, 1), lambda i: (i, 0)), blk, pl.BlockSpec((1, D), lambda i: (0, 0))],
        out_specs=[pl.BlockSpec((BLK, D), lambda i: (jnp.maximum(i - 1, 0), 0)), pl.BlockSpec((BLK, D), lambda i: (0, 0)),
                   pl.BlockSpec((1, D), lambda i: (0, 0))],
        out_shape=[jax.ShapeDtypeStruct((s - BLK, D), F32), jax.ShapeDtypeStruct((BLK, D), F32),
                   jax.ShapeDtypeStruct((1, D), F32)],
        compiler_params=_cparams(("arbitrary",)),
    )(dxn, h, r, dh2, norm_w)


def _adam_math(w, g, m, v):
    m = ADAM_B1 * m + (1.0 - ADAM_B1) * g
    v = ADAM_B2 * v + (1.0 - ADAM_B2) * (g * g)
    m_hat = m / (1.0 - ADAM_B1 ** ADAM_STEP)
    v_hat = v / (1.0 - ADAM_B2 ** ADAM_STEP)
    return -ADAM_LR * (m_hat / (jnp.sqrt(v_hat) + ADAM_EPS) + ADAM_WD * w), m, v


def _adamw(w, g, m, v, name):
    rows, cols = w.shape
    tr = rows
    for cand in (128, 64, 32, 16, 8):
        if rows % cand == 0:
            tr = cand
            break

    def body(w_ref, g_ref, m_ref, v_ref, d_ref, nm_ref, nv_ref):
        d_ref[...], nm_ref[...], nv_ref[...] = _adam_math(w_ref[...], g_ref[...], m_ref[...], v_ref[...])

    spec = pl.BlockSpec((tr, cols), lambda i: (i, 0))
    return pl.pallas_call(
        body, name=name, grid=(rows // tr,), in_specs=[spec] * 4, out_specs=[spec] * 3,
        out_shape=[jax.ShapeDtypeStruct((rows, cols), F32)] * 3,
        compiler_params=_cparams(("parallel",)),
    )(w, g, m, v)


def _adamw_t(w_t, g_t, m_t, v_t, name):
    cols, rows = w_t.shape

    def body(w_ref, g_ref, m_ref, v_ref, go_ref, d_ref, nm_ref, nv_ref):
        g = g_ref[...]
        go_ref[...] = g
        d_ref[...], nm_ref[...], nv_ref[...] = _adam_math(w_ref[...], g, m_ref[...], v_ref[...])

    blk = pl.BlockSpec((128, rows), lambda k: (k, 0))
    return pl.pallas_call(
        body, name=name, grid=(pl.cdiv(cols, 128),), in_specs=[blk] * 4, out_specs=[blk] * 4,
        out_shape=[jax.ShapeDtypeStruct((cols, rows), F32)] * 4, compiler_params=_cparams(("parallel",)),
    )(w_t, g_t, m_t, v_t)


def _adamw_small(w, gall, m, v):
    def body(w_ref, g_ref, m_ref, v_ref, gs_ref, d_ref, nm_ref, nv_ref):
        g = g_ref[0]
        for dev in range(1, 8):
            g = g + g_ref[dev]
        gs_ref[...] = g
        d_ref[...], nm_ref[...], nv_ref[...] = _adam_math(w_ref[...], g, m_ref[...], v_ref[...])

    return pl.pallas_call(
        body, name="adamw_small", out_shape=[jax.ShapeDtypeStruct((SM_ROWS, 128), F32)] * 4,
        compiler_params=pltpu.CompilerParams(vmem_limit_bytes=VMEM_LIMIT),
    )(w, gall, m, v)


ADD_ROWS = 464
HALF_BLOCKS = PK_HALF // ADD_ROWS


def _rs_core_add(srcs, land, pos):
    def body(pos_ref, *refs):
        src_refs, (land_ref, out_ref, own, sems) = refs[:len(srcs)], refs[len(srcs):]
        step = pl.program_id(0)

        def fetch(s):
            for hf in range(2):
                @pl.when(pos_ref[1] == hf)
                def _():
                    for k, r0, p, n in _grad_half(s, hf):
                        pltpu.make_async_copy(src_refs[k].at[pl.ds(r0, n), :], own.at[s % 2, pl.ds(p, n), :],
                                              sems.at[s % 2]).start()

        @pl.when(step == 0)
        def _():
            fetch(0)

        for s in range(N_CHIPS):
            @pl.when(step == s)
            def _():
                if s + 1 < N_CHIPS:
                    fetch(s + 1)
                slot = own.at[s % 2]
                pltpu.make_async_copy(slot, slot, sems.at[s % 2]).wait()
                out_ref[0] = (slot[...].astype(F32) + land_ref[0].astype(F32)).astype(BF16)

    blk = pl.BlockSpec((1, PK_HALF, D), lambda s, pos_ref: (s, 0, 0))
    return pl.pallas_call(
        body, name="rs_core_add",
        grid_spec=pltpu.PrefetchScalarGridSpec(
            num_scalar_prefetch=1, grid=(N_CHIPS,), in_specs=[ANY] * len(srcs) + [blk], out_specs=blk,
            scratch_shapes=[pltpu.VMEM((2, PK_HALF, D), BF16), pltpu.SemaphoreType.DMA((2,))]),
        out_shape=jax.ShapeDtypeStruct((N_CHIPS, PK_HALF, D), BF16),
        compiler_params=_cparams(("arbitrary",)),
    )(pos, *srcs, land)


def _rs_chip_add(cp, land, pos):
    def body(pos_ref, a_ref, l_ref, o_ref):
        o_ref[0] = ((a_ref[0].astype(F32) + l_ref[0].astype(F32)) + l_ref[1].astype(F32)) + l_ref[2].astype(F32)

    return pl.pallas_call(
        body, name="rs_chip_add",
        grid_spec=pltpu.PrefetchScalarGridSpec(
            num_scalar_prefetch=1, grid=(HALF_BLOCKS,),
            in_specs=[pl.BlockSpec((1, ADD_ROWS, D), lambda i, pos_ref: (pos_ref[0], i, 0)),
                      pl.BlockSpec((3, ADD_ROWS, D), lambda i, pos_ref: (0, i, 0))],
            out_specs=pl.BlockSpec((1, ADD_ROWS, D), lambda i, pos_ref: (pos_ref[1], i, 0))),
        out_shape=jax.ShapeDtypeStruct((2, PK_HALF, D), F32),
        compiler_params=_cparams(("parallel",)),
    )(pos, cp, land)


ANY = pl.BlockSpec(memory_space=pl.ANY)


def _mesh_pos():
    return lax.axis_index("x"), lax.axis_index("y"), lax.axis_index("c")


def _chip_peer(x, y, c, r):
    return (jnp.bitwise_xor(x, r >> 1), jnp.bitwise_xor(y, r & 1), c)


def _allgather_w_in(big0, wt, meta_loc, x2, norm_w):
    s_rows = x2.shape[0] + BLK
    tok = next(t for t in (512, 256, 128) if x2.shape[0] % t == 0)

    def body(big0_ref, wt_ref, meta_ref, x_ref, nw_ref, big_ref, metaf_ref, h_ref, xn_ref, r_ref,
             lsem, ssem, rsem, fssem, frsem, msem_s, msem_r, osem, xbuf, xnbuf, rbuf, mbuf, psem):
        del big0_ref
        x, y, c = _mesh_pos()
        j = 2 * x + y
        me, sibling = (x, y, c), (x, y, 1 - c)

        def half_wait(s_sem, r_sem, halves=1):
            rows = big_ref.at[pl.ds(0, halves * AG_IN_HALF), :]
            return pltpu.make_async_remote_copy(src_ref=rows, dst_ref=rows, send_sem=s_sem, recv_sem=r_sem,
                                                device_id=me, device_id_type=MESH)

        own_meta = pltpu.make_async_copy(meta_ref, metaf_ref.at[j], lsem)
        own_meta.start()
        for s in range(N_CHIPS):
            for hf in range(2):
                @pl.when((j == s) & (c == hf))
                def _():
                    for r in (1, 2, 3):
                        for p, b, n in _ag_half(s, True, hf):
                            pltpu.make_async_remote_copy(
                                src_ref=wt_ref.at[pl.ds(p, n), :], dst_ref=big_ref.at[pl.ds(b, n), :],
                                send_sem=ssem.at[r - 1], recv_sem=rsem.at[r - 1],
                                device_id=_chip_peer(x, y, c, r), device_id_type=MESH).start()

            @pl.when(j == s)
            def _():
                for hf in range(2):
                    for p, b, n in _ag_half(s, True, hf):
                        pltpu.make_async_remote_copy(
                            src_ref=wt_ref.at[pl.ds(p, n), :], dst_ref=big_ref.at[pl.ds(b, n), :],
                            send_sem=osem.at[0], recv_sem=osem.at[1], device_id=sibling, device_id_type=MESH).start()

        meta_copies = [pltpu.make_async_remote_copy(
            src_ref=meta_ref, dst_ref=metaf_ref.at[j], send_sem=msem_s.at[r - 1], recv_sem=msem_r.at[r - 1],
            device_id=_chip_peer(x, y, c, r), device_id_type=MESH) for r in (1, 2, 3)]
        for cp in meta_copies:
            cp.start()

        nw = nw_ref[...]

        n_tok = (s_rows - BLK) // tok

        def x_in(i, slot):
            return pltpu.make_async_copy(x_ref.at[pl.ds(pl.multiple_of(i * tok, tok), tok), :], xbuf.at[slot],
                                         psem.at[slot])

        def norm_rows(first, n, slot):
            hv = xbuf[slot, pl.ds(0, n), :]
            r = lax.rsqrt(jnp.mean(hv * hv, axis=-1, keepdims=True) + NORM_EPS)
            xnbuf[pl.ds(0, n), :] = (hv * r * nw).astype(BF16)
            rbuf[pl.ds(0, n), :] = r
            rows = pl.ds(first if isinstance(first, int) else pl.multiple_of(first, BLK), n)
            outs = [pltpu.make_async_copy(xbuf.at[slot, pl.ds(0, n), :], h_ref.at[rows, :], psem.at[2]),
                    pltpu.make_async_copy(xnbuf.at[pl.ds(0, n), :], xn_ref.at[rows, :], psem.at[3]),
                    pltpu.make_async_copy(rbuf.at[pl.ds(0, n), :], r_ref.at[rows, :], psem.at[4])]
            for cp in outs:
                cp.start(priority=1)
            for cp in outs:
                cp.wait()

        x_in(0, 0).start(priority=1)

        def token_rows(i, carry):
            slot = i % 2
            x_in(i, slot).wait()

            @pl.when(i + 1 < n_tok)
            def _():
                x_in(i + 1, 1 - slot).start(priority=1)

            norm_rows(BLK + i * tok, tok, slot)
            return carry

        lax.fori_loop(0, n_tok, token_rows, 0)
        for cp in meta_copies:
            cp.wait_recv()
        own_meta.wait()
        xbuf[0, pl.ds(0, BLK), :] = jnp.zeros((BLK, D), F32)
        for sh in range(N_CHIPS):
            pltpu.sync_copy(metaf_ref.at[sh], mbuf)
            xbuf[0, PAD_FRONT:BLK, sh * (D // N_CHIPS):(sh + 1) * (D // N_CHIPS)] = mbuf[...]
        norm_rows(0, BLK, 0)

        for r in (1, 2, 3):
            half_wait(ssem.at[r - 1], rsem.at[r - 1]).wait_recv()
            src_shard = jnp.bitwise_xor(j, r)
            for s in range(N_CHIPS):
                for hf in range(2):
                    @pl.when((src_shard == s) & (c == hf))
                    def _():
                        for _, b, n in _ag_half(s, True, hf):
                            rows = big_ref.at[pl.ds(b, n), :]
                            pltpu.make_async_remote_copy(
                                src_ref=rows, dst_ref=rows, send_sem=fssem.at[r - 1], recv_sem=frsem.at[r - 1],
                                device_id=sibling, device_id_type=MESH).start()
        for r in (1, 2, 3):
            half_wait(fssem.at[r - 1], frsem.at[r - 1]).wait_recv()
        for r in (1, 2, 3):
            half_wait(ssem.at[r - 1], rsem.at[r - 1]).wait_send()
            half_wait(fssem.at[r - 1], frsem.at[r - 1]).wait_send()
        own_rows = half_wait(osem.at[0], osem.at[1], halves=2)
        own_rows.wait_recv()
        own_rows.wait_send()
        for cp in meta_copies:
            cp.wait_send()

    dma3 = pltpu.SemaphoreType.DMA((3,))
    big, _, h, xn, r = pl.pallas_call(
        body, name="allgather_w_in", in_specs=[ANY, ANY, ANY, ANY, pl.BlockSpec(memory_space=pltpu.VMEM)],
        out_specs=[ANY] * 5,
        out_shape=[jax.ShapeDtypeStruct((BIG_ROWS, D), BF16), jax.ShapeDtypeStruct((N_CHIPS,) + meta_loc.shape, F32),
                   jax.ShapeDtypeStruct((s_rows, D), F32), jax.ShapeDtypeStruct((s_rows, D), BF16),
                   jax.ShapeDtypeStruct((s_rows, 1), F32)],
        scratch_shapes=[pltpu.SemaphoreType.DMA(()), dma3, dma3, dma3, dma3, dma3, dma3, pltpu.SemaphoreType.DMA((2,)),
                        pltpu.VMEM((2, tok, D), F32), pltpu.VMEM((tok, D), BF16), pltpu.VMEM((tok, 1), F32),
                        pltpu.VMEM(meta_loc.shape, F32), pltpu.SemaphoreType.DMA((5,))],
        input_output_aliases={0: 0},
        compiler_params=pltpu.CompilerParams(has_side_effects=True),
    )(big0, wt, meta_loc, x2, norm_w)
    return big, h, xn, r


HBM = pl.BlockSpec(memory_space=pltpu.HBM)
SEM = pl.BlockSpec(memory_space=pltpu.SEMAPHORE)
EFFECT = pltpu.SideEffectType.DATAFLOW_SIDE_EFFECTING
REST_PEERS = [(r, dc) for r in (1, 2, 3) for dc in (0, 1)]


def _allgather_rest_start(big):
    def body(big_ref, *rest):
        s_sems, r_sems = rest[:6], rest[6:12]
        x, y, c = _mesh_pos()
        j = 2 * x + y
        for s in range(N_CHIPS):
            for hf in range(2):
                @pl.when((j == s) & (c == hf))
                def _():
                    for k, (r, dc) in enumerate(REST_PEERS):
                        for _, b, n in _ag_half(s, False, hf):
                            rows = big_ref.at[pl.ds(b, n), :]
                            pltpu.make_async_remote_copy(
                                src_ref=rows, dst_ref=rows, send_sem=s_sems[k], recv_sem=r_sems[2 * (r - 1) + hf],
                                device_id=_chip_peer(x, y, dc, r), device_id_type=MESH).start()

    sem = pltpu.SemaphoreType.DMA(())
    out = pl.pallas_call(
        body, name="allgather_rest_start", out_shape=(sem,) * 12 + (pltpu.HBM(big.shape, big.dtype),),
        in_specs=(HBM,), out_specs=(SEM,) * 12 + (HBM,), input_output_aliases={0: 12},
        compiler_params=pltpu.CompilerParams(has_side_effects=EFFECT),
    )(pltpu.with_memory_space_constraint(big, pltpu.HBM))
    return out[:12], out[12]


def _allgather_rest_wait(sems, big, after):
    def body(big_ref, *rest):
        s_sems, r_sems = rest[:6], rest[6:12]
        me = _mesh_pos()
        rows = big_ref.at[pl.ds(0, AG_REST_HALF), :]
        for k in range(6):
            copy = pltpu.make_async_remote_copy(src_ref=rows, dst_ref=rows, send_sem=s_sems[k], recv_sem=r_sems[k],
                                                device_id=me, device_id_type=MESH)
            copy.wait_send()
            copy.wait_recv()

    return pl.pallas_call(
        body, name="allgather_rest_wait", out_shape=pltpu.HBM(big.shape, big.dtype),
        in_specs=(HBM,) + (SEM,) * 12 + (pl.BlockSpec(memory_space=pl.ANY),), out_specs=HBM,
        input_output_aliases={0: 0}, compiler_params=pltpu.CompilerParams(has_side_effects=EFFECT),
    )(big, *sems, after)


RS_SPLIT = 2


def _rs_core_exchange(srcs):
    def body(*refs):
        src_refs, (land_ref, ssem, rsem) = refs[:len(srcs)], refs[len(srcs):]
        x, y, c = _mesh_pos()
        me, sibling = (x, y, c), (x, y, 1 - c)
        for hf in range(2):
            @pl.when(c == hf)
            def _():
                for s in range(N_CHIPS):
                    for k, r0, p, n in _grad_half(s, 1 - hf):
                        pltpu.make_async_remote_copy(
                            src_ref=src_refs[k].at[pl.ds(r0, n), :], dst_ref=land_ref.at[s, pl.ds(p, n), :],
                            send_sem=ssem, recv_sem=rsem, device_id=sibling, device_id_type=MESH).start()
        whole = pltpu.make_async_remote_copy(src_ref=land_ref, dst_ref=land_ref, send_sem=ssem, recv_sem=rsem,
                                             device_id=me, device_id_type=MESH)
        whole.wait_recv()
        whole.wait_send()

    return pl.pallas_call(
        body, name="rs_core_exchange", in_specs=[ANY] * len(srcs), out_specs=ANY,
        out_shape=jax.ShapeDtypeStruct((N_CHIPS, PK_HALF, D), BF16),
        scratch_shapes=[pltpu.SemaphoreType.DMA(())] * 2,
        compiler_params=pltpu.CompilerParams(has_side_effects=True),
    )(*srcs)


def _rs_chip_start(cp):
    def body(cp_ref, land_ref, s1, s2, s3, r1, r2, r3, cp_thru, land_thru, token):
        x, y, c = _mesh_pos()
        j = 2 * x + y
        for r, s_sem, r_sem in zip((1, 2, 3), (s1, s2, s3), (r1, r2, r3)):
            pltpu.make_async_remote_copy(
                src_ref=cp_ref.at[jnp.bitwise_xor(j, r)], dst_ref=land_ref.at[r - 1], send_sem=s_sem, recv_sem=r_sem,
                device_id=_chip_peer(x, y, c, r), device_id_type=MESH).start()
        token[...] = jnp.zeros_like(token)

    land_shape = (3, PK_HALF, D)
    sem = pltpu.SemaphoreType.DMA(())
    out = pl.pallas_call(
        body, name="rs_chip_start",
        out_shape=(sem,) * 6 + (pltpu.HBM(cp.shape, cp.dtype), pltpu.HBM(land_shape, BF16),
                                jax.ShapeDtypeStruct((8, 128), F32)),
        in_specs=(HBM, HBM), out_specs=(SEM,) * 6 + (HBM, HBM, pl.BlockSpec(memory_space=pltpu.VMEM)),
        input_output_aliases={0: 6, 1: 7},
        compiler_params=pltpu.CompilerParams(has_side_effects=EFFECT),
    )(pltpu.with_memory_space_constraint(cp, pltpu.HBM),
      pltpu.with_memory_space_constraint(lax.empty(land_shape, BF16), pltpu.HBM))
    return out[:6], out[6], out[7], out[8]


def _rs_chip_wait(sems, cp_thru, land_thru, after):
    def body(cp_ref, land_ref, s1, s2, s3, r1, r2, r3, after_ref, cp_out, land_out):
        me = _mesh_pos()
        for r, s_sem, r_sem in zip((1, 2, 3), (s1, s2, s3), (r1, r2, r3)):
            copy = pltpu.make_async_remote_copy(src_ref=cp_ref.at[0], dst_ref=land_ref.at[r - 1], send_sem=s_sem,
                                                recv_sem=r_sem, device_id=me, device_id_type=MESH)
            copy.wait_send()
            copy.wait_recv()

    return pl.pallas_call(
        body, name="rs_chip_wait",
        out_shape=(pltpu.HBM(cp_thru.shape, cp_thru.dtype), pltpu.HBM(land_thru.shape, land_thru.dtype)),
        in_specs=(HBM, HBM) + (SEM,) * 6 + (pl.BlockSpec(memory_space=pl.ANY),), out_specs=(HBM, HBM),
        input_output_aliases={0: 0, 1: 1},
        compiler_params=pltpu.CompilerParams(has_side_effects=EFFECT),
    )(cp_thru, land_thru, *sems, after)


def _rs_finish(full, small):
    n = PK_HALF // RS_SPLIT

    def body(full_in_ref, sm_ref, full_ref, all_ref, lsem, ssem, rsem, sm_s, sm_r):
        del full_in_ref
        x, y, c = _mesh_pos()
        me, sibling = (x, y, c), (x, y, 1 - c)
        my_id = 4 * x + 2 * y + c
        own_sm = pltpu.make_async_copy(sm_ref, all_ref.at[my_id], lsem)
        own_sm.start()
        for q in range(RS_SPLIT):
            rows = full_ref.at[c, pl.ds(q * n, n), :]
            pltpu.make_async_remote_copy(src_ref=rows, dst_ref=rows, send_sem=ssem, recv_sem=rsem,
                                         device_id=sibling, device_id_type=MESH).start()
        half = pltpu.make_async_remote_copy(src_ref=full_ref.at[c], dst_ref=full_ref.at[c], send_sem=ssem,
                                            recv_sem=rsem, device_id=me, device_id_type=MESH)
        smalls = [pltpu.make_async_remote_copy(
            src_ref=sm_ref, dst_ref=all_ref.at[my_id], send_sem=sm_s.at[r - 1], recv_sem=sm_r.at[r - 1],
            device_id=(jnp.bitwise_xor(x, r >> 2), jnp.bitwise_xor(y, (r >> 1) & 1), jnp.bitwise_xor(c, r & 1)),
            device_id_type=MESH) for r in range(1, 8)]
        for cpy in smalls:
            cpy.start()
        half.wait_recv()
        half.wait_send()
        for cpy in smalls:
            cpy.wait_recv()
        for cpy in smalls:
            cpy.wait_send()
        own_sm.wait()

    dma7 = pltpu.SemaphoreType.DMA((7,))
    return pl.pallas_call(
        body, name="rs_finish", in_specs=[ANY, ANY], out_specs=[ANY, ANY],
        out_shape=[jax.ShapeDtypeStruct((2, PK_HALF, D), F32), jax.ShapeDtypeStruct((8, SM_ROWS, 128), F32)],
        scratch_shapes=[pltpu.SemaphoreType.DMA(()), pltpu.SemaphoreType.DMA(()), pltpu.SemaphoreType.DMA(()), dma7, dma7],
        input_output_aliases={0: 0},
        compiler_params=pltpu.CompilerParams(has_side_effects=True),
    )(full, small)


def _rope_tables(s):
    pos = jnp.arange(s, dtype=F32) - PAD_FRONT

    def cs(d):
        inv = ROPE_BASE ** (-jnp.arange(0, d, 2, dtype=F32) / d)
        ang = pos[:, None] * inv[None, :]
        return jnp.cos(ang), jnp.sin(ang)

    c, sn = cs(ROPE_D)
    z = jnp.zeros_like(c)
    ktab = (jnp.concatenate([c, c, z, z], axis=1), jnp.concatenate([-sn, z, z, z], axis=1),
            jnp.concatenate([z, sn, z, z], axis=1))
    one = jnp.ones_like(c)
    qtab = tuple(ATT_SCALE * jnp.concatenate(parts, axis=1) for parts in (
        [one, one, one, one, c, c, z, z], [z, z, z, z, -sn, z, z, z], [z, z, z, z, z, sn, z, z]))
    lane = jnp.arange(HD) == DH + ROPE_D
    qbias = lane.astype(F32)[None, :]
    kbias = jnp.where((pos < 0)[:, None] & lane[None, DH:], NEG_INF, 0.0).astype(F32)
    c, sn = cs(DH)
    return qtab, ktab, qbias, kbias, jnp.concatenate([c, c], axis=1), jnp.concatenate([-sn, sn], axis=1)


def _grad_mm(a, b, m, n, s, name):
    return _matmul(a, b, "tn", m, n, s, min(m, 512), min(n, 1024), s, BF16, name, n_outer=True)


def _local_step(x2, target2, hp0, big, small_w, normed=None, gather_wait=None, reduce_start=None, reduce_wait=None):
    norm_w, qn_w, kvn_w, gn_w, gn_b, fin_w = small_w
    s = x2.shape[0] + BLK
    tm = _row_tile(s, 1408)
    qtab, ktab, qbias, kbias, c128, s128 = _rope_tables(s)
    consts, consts_bwd, gamma = _ret_consts()

    h, xn, r1 = _rms_in(x2, hp0, norm_w, s) if normed is None else normed
    proj = _matmul(xn, big, "nt", s, PROJ_W, D, tm, 1024, D, BF16, "proj", b_off=(BIG_IN // 1024, 0))
    if gather_wait is not None:
        big = gather_wait(big, proj)
    uq = big[BIG_UQ:BIG_UQ + 384].reshape(HEADS, DH + ROPE_D, Q_RANK)
    wq = jnp.pad(uq, ((0, 0), (0, HD - DH - ROPE_D), (0, 0))).reshape(HEADS * HD, Q_RANK)
    wukv = big[BIG_UKV:BIG_UKV + 256].reshape(2 * WIDTH, KV_RANK)
    wbm = big[BIG_BM:BIG_BM + 1024].reshape(D, WIDTH)
    wbr = big[BIG_BR:BIG_BR + 1024].reshape(D, WIDTH)
    cqn, rq, ckvn, rkv, qq, kk, vv = _mla_prep(proj, qn_w, kvn_w, wq, wukv, qtab, ktab, qbias, kbias, s)
    o_mla, y_mla, lse = _attn_fwd(qq, kk, vv, proj, s)
    y_ret, on, rstd, qr, kr, rall = _ret_fwd(proj, gn_w, gn_b, c128, s128, consts, gamma, s)
    u_mla, u_ret, merged = _merge_fwd(y_mla, y_ret, wbm, wbr, proj, s)
    h2 = _out_fwd(merged, big, h, s)
    dh2, dh2b, loss_blk, dfin = _loss_bwd(h2, target2, fin_w, s)

    dproj, du_mla, du_ret = _merge_bwd(dh2b, big, u_mla, u_ret, proj, s)
    g_out = _grad_mm(merged, dh2b, D, D, s, "grad_w_out")
    dproj, do_mla, delta = _branch_mla_bwd(dproj, du_mla, wbm, o_mla, proj, s)
    g_bm = _grad_mm(du_mla, y_mla, D, WIDTH, s, "grad_w_branch_mla")
    dproj, do_ret, dgw, dgb = _branch_ret_bwd(dproj, du_ret, wbr, on, rstd, gn_w, gn_b, proj, s)
    g_br = _grad_mm(du_ret, y_ret, D, WIDTH, s, "grad_w_branch_ret")
    dproj = _ret_bwd(dproj, qr, kr, proj, rall, do_ret, c128, s128, consts_bwd, gamma, s)
    dqq, dkk, dvv = _attn_bwd(qq, kk, vv, do_mla, lse, delta.reshape(lse.shape), s)
    dproj, dq, dkv, dqnw, dkvnw = _mla_prep_bwd(dproj, dqq, dkk, dvv, proj, rq, rkv, qn_w, kvn_w, wq, wukv,
                                                qtab, ktab, s)
    g_q = _grad_mm(dq, cqn, HEADS * HD, Q_RANK, s, "grad_w_uq")
    g_ukv = _grad_mm(dkv, ckvn, 2 * WIDTH, KV_RANK, s, "grad_w_ukv")
    g_in = _grad_mm(dproj, xn, PROJ_W, D, s, "grad_w_in")

    g_uq = g_q.reshape(HEADS, HD, Q_RANK)[:, :DH + ROPE_D]
    grads = (g_in, g_out, g_bm.reshape(1024, D), g_br.reshape(1024, D), g_uq.reshape(384, D), g_ukv.reshape(256, D))
    token, travelling = (None, grads) if reduce_start is None else reduce_start(grads)
    dxn = _matmul(dproj, big, "nn", s, D, PROJ_W, _row_tile(s, 384), 512, PROJ_W, F32, "dxn",
                  b_off=(BIG_IN // PROJ_W, 0), n_outer=True, after=token)
    grad_x, gmeta_blk, dnorm = _rms_in_bwd(dxn, h, r1, dh2, norm_w, s)
    small = _small_rows((dnorm, dqnw, dkvnw, dgw, dgb, dfin), loss_blk, gmeta_blk[PAD_FRONT:])
    return grad_x, travelling if reduce_wait is None else reduce_wait(travelling, dnorm), small


def _small_rows(ws, first=None, last=None):
    def part(a, rows):
        a = a.reshape(-1, 128)
        return a if a.shape[0] == rows else jnp.pad(a, ((0, rows - a.shape[0]), (0, 0)))

    bounds = (SM_NORM, SM_QN, SM_KVN, SM_GNW, SM_GNB, SM_FIN, SM_META)
    first = jnp.zeros((SM_NORM, 128), F32) if first is None else first
    last = jnp.zeros((SM_ROWS - SM_META, 128), F32) if last is None else last
    return jnp.concatenate([part(first, SM_NORM)] + [part(w, bounds[k + 1] - bounds[k]) for k, w in enumerate(ws)]
                           + [part(last, SM_ROWS - SM_META)], axis=0)


def kernel(x, meta, norm_w, w_in, mla_q_norm_w, mla_w_uq, mla_kv_norm_w, mla_w_ukv, ret_gn_w, ret_gn_b, w_branch_mla, w_branch_ret, w_out, final_norm_w, loss_target, m_meta, m_norm_w, m_w_in, m_mla_q_norm_w, m_mla_w_uq, m_mla_kv_norm_w, m_mla_w_ukv, m_ret_gn_w, m_ret_gn_b, m_w_branch_mla, m_w_branch_ret, m_w_out, m_final_norm_w, v_meta, v_norm_w, v_w_in, v_mla_q_norm_w, v_mla_w_uq, v_mla_kv_norm_w, v_mla_w_ukv, v_ret_gn_w, v_ret_gn_b, v_w_branch_mla, v_w_branch_ret, v_w_out, v_final_norm_w):
    j = 2 * lax.axis_index("x") + lax.axis_index("y")
    tr = lambda w: w[0].T.reshape(-1, D).astype(BF16)
    pos = jnp.stack([j, lax.axis_index("c")]).astype(jnp.int32)
    put = lambda buf, rows, off: lax.dynamic_update_slice(buf, rows, (off, 0))
    big0 = lax.empty((BIG_ROWS, D), BF16)
    big0 = put(big0, w_out[0].astype(BF16), BIG_OUT + 512 * j)
    big0 = put(big0, tr(w_branch_mla), BIG_BM + 256 * j)
    big0 = put(big0, tr(w_branch_ret), BIG_BR + 256 * j)
    big0 = put(big0, tr(mla_w_uq), BIG_UQ + 96 * j)
    big0 = put(big0, tr(mla_w_ukv), BIG_UKV + 64 * j)
    big0 = put(big0, jnp.zeros((AG_JUNK_END - AG_JUNK_REST, D), BF16), AG_JUNK_REST)
    big0 = put(big0, jnp.zeros((ZERO_ROWS, D), BF16), BIG_IN + IN_WIDTH)
    wt = jnp.concatenate([tr(w_in), jnp.zeros((2 * AG_IN_HALF - IN_SHARD, D), BF16)], axis=0)
    big, h, xn, r1 = _allgather_w_in(big0, wt, meta, x[0], norm_w)
    gather_sems, big = _allgather_rest_start(big)

    def gather_wait(big_travelling, after):
        return _allgather_rest_wait(gather_sems, big_travelling, after)

    small_w = (norm_w, mla_q_norm_w, mla_kv_norm_w, ret_gn_w, ret_gn_b, final_norm_w.reshape(1, D))

    def reduce_start(grads):
        chip_part = _rs_core_add(grads, _rs_core_exchange(grads), pos)
        sems, part_thru, land_thru, token = _rs_chip_start(chip_part)
        return token, (sems, part_thru, land_thru)

    def reduce_wait(state, after):
        return _rs_chip_wait(*state, after)

    grad_x, (chip_part, land2), small = _local_step(x[0], loss_target[0], None, big, small_w, (h, xn, r1), gather_wait,
                                                    reduce_start, reduce_wait)
    full, small_all = _rs_finish(_rs_chip_add(chip_part, land2, pos), small)
    full = full.reshape(PK_ROWS, D)

    untr = lambda lo, hi, rows: full[lo:hi].reshape(rows, -1).T
    grads = {
        "w_out": full[PK_OUT:PK_BM], "w_branch_mla": untr(PK_BM, PK_BR, 512), "w_branch_ret": untr(PK_BR, PK_UQ, 512),
        "mla_w_uq": untr(PK_UQ, PK_UKV, 384), "mla_w_ukv": untr(PK_UKV, PK_PAD, 512),
    }
    big_w = {"mla_w_uq": (mla_w_uq, m_mla_w_uq, v_mla_w_uq),
             "mla_w_ukv": (mla_w_ukv, m_mla_w_ukv, v_mla_w_ukv),
             "w_branch_mla": (w_branch_mla, m_w_branch_mla, v_w_branch_mla),
             "w_branch_ret": (w_branch_ret, m_w_branch_ret, v_w_branch_ret), "w_out": (w_out, m_w_out, v_w_out)}
    res = {"w_in": tuple(a.T[None] for a in _adamw_t(w_in[0].T, full, m_w_in[0].T, v_w_in[0].T, "adamw_w_in"))}
    for name, (w, m, v) in big_w.items():
        d, nm, nv = _adamw(w[0], grads[name], m[0], v[0], "adamw_" + name)
        res[name] = (grads[name][None], d[None], nm[None], nv[None])

    small_m = (m_norm_w, m_mla_q_norm_w, m_mla_kv_norm_w, m_ret_gn_w, m_ret_gn_b, m_final_norm_w.reshape(1, D))
    small_v = (v_norm_w, v_mla_q_norm_w, v_mla_kv_norm_w, v_ret_gn_w, v_ret_gn_b, v_final_norm_w.reshape(1, D))
    gs, ds, ms, vs = _adamw_small(_small_rows(small_w), small_all, _small_rows(small_m), _small_rows(small_v))
    names = ["norm_w", "mla_q_norm_w", "mla_kv_norm_w", "ret_gn_w", "ret_gn_b", "final_norm_w"]
    bounds = [SM_NORM, SM_QN, SM_KVN, SM_GNW, SM_GNB, SM_FIN]
    for k, name in enumerate(names):
        shape = (D,) if name == "final_norm_w" else (1, -1)
        rows = small_w[k].size // 128
        res[name] = tuple(a[bounds[k]:bounds[k] + rows].reshape(shape) for a in (gs, ds, ms, vs))
    g_meta = lax.dynamic_slice_in_dim(gs[SM_META:SM_META + 256].reshape(N_META, D), j * (D // N_CHIPS), D // N_CHIPS, axis=1)
    res["meta"] = (g_meta,) + tuple(_adamw(meta, g_meta, m_meta, v_meta, "adamw_meta"))

    order = ["meta", "norm_w", "w_in", "mla_q_norm_w", "mla_w_uq", "mla_kv_norm_w", "mla_w_ukv", "ret_gn_w", "ret_gn_b",
             "w_branch_mla", "w_branch_ret", "w_out", "final_norm_w"]
    return (gs[0, 0], grad_x[None]) + tuple(res[n][k] for k in range(4) for n in order)
```
